```python
import math
import jax, jax.numpy as jnp
from jax import lax
import numpy as np

D_MODEL = 2048
BATCH = 16
SEQ = 2048
DEPTH = 1

MEM_LEN = 256
D_FF = 5632
POOL_GROUPS = 4
POOL_GROUP_DIM = 128
POOL_WIDTH = POOL_GROUPS * POOL_GROUP_DIM
POOL_WINDOWS = (2, 4, 8, 16)
FOX_HEADS = 16
FOX_HEAD_DIM = 64
FOX_WIDTH = FOX_HEADS * FOX_HEAD_DIM
MEM_HEADS = 4
MEM_HEAD_DIM = 128
MEM_WIDTH = MEM_HEADS * MEM_HEAD_DIM
N_BRANCHES = 3
GATE_WIDTH = N_BRANCHES * D_MODEL
Q_BLOCK = 128
EPS = 1e-6
IN_SPLITS = (POOL_WIDTH, FOX_WIDTH, FOX_WIDTH, FOX_WIDTH, FOX_HEADS, MEM_WIDTH, GATE_WIDTH)
IN_WIDTH = sum(IN_SPLITS)

kernel_name = "hybrid_pool_fox_memxattn_macaron"


def rmsnorm(x, g):
    xf = x.astype(jnp.float32)
    y = xf * lax.rsqrt(jnp.mean(xf * xf, axis=-1, keepdims=True) + EPS)
    return (y * g.astype(jnp.float32)).astype(x.dtype)


def swiglu_half_ffn(x, norm_g, w_gate_up, w_down):
    h = rmsnorm(x, norm_g)
    gate, up = jnp.split(h @ w_gate_up, 2, axis=-1)
    return 0.5 * ((jax.nn.silu(gate) * up) @ w_down)


def causal_window_mean(u, w):
    S = u.shape[1]
    cs = jnp.cumsum(u.astype(jnp.float32), axis=1)
    lagged = jnp.pad(cs, ((0, 0), (w, 0), (0, 0)))[:, :S]
    count = jnp.minimum(jnp.arange(1, S + 1), w).astype(jnp.float32)
    return ((cs - lagged) / count[None, :, None]).astype(u.dtype)


def pool_mixer(u, pool_w, pool_scale):
    B, S, _ = u.shape
    groups = u.reshape(B, S, POOL_GROUPS, POOL_GROUP_DIM)
    pooled = jnp.stack([causal_window_mean(groups[:, :, g], POOL_WINDOWS[g])
                        for g in range(POOL_GROUPS)], axis=2)
    mixed = jnp.einsum('bsgc,gcd->bsgd', pooled - groups, pool_w)
    return mixed.reshape(B, S, POOL_WIDTH) * pool_scale


def forgetting_attention(q, k, v, log_f):
    B, S, H, Dh = q.shape
    c = jnp.cumsum(log_f, axis=1).transpose(0, 2, 1)
    scale = Dh ** -0.5
    outs = []
    for i in range(S // Q_BLOCK):
        q0, q1 = i * Q_BLOCK, (i + 1) * Q_BLOCK
        logits = jnp.einsum('bqhd,bkhd->bhqk', q[:, q0:q1], k[:, :q1]).astype(jnp.float32) * scale
        logits = logits + c[:, :, q0:q1, None] - c[:, :, None, :q1]
        causal = (q0 + jnp.arange(Q_BLOCK))[:, None] >= jnp.arange(q1)[None, :]
        logits = jnp.where(causal[None, None], logits, -jnp.inf)
        p = jax.nn.softmax(logits, axis=-1).astype(v.dtype)
        outs.append(jnp.einsum('bhqk,bkhd->bqhd', p, v[:, :q1]))
    return jnp.concatenate(outs, axis=1)


def memory_attention(q, k, v):
    scale = q.shape[-1] ** -0.5
    logits = jnp.einsum('bshd,bmhd->bhsm', q, k).astype(jnp.float32) * scale
    p = jax.nn.softmax(logits, axis=-1).astype(v.dtype)
    return jnp.einsum('bhsm,bmhd->bshd', p, v)


def _fwd_setup_inputs(seed: int = 0) -> dict:
    key = jax.random.key(seed)
    ks = jax.random.split(key, 24)
    nrm = lambda k, shape, fan_in: jax.random.normal(k, shape, jnp.float32) * fan_in ** -0.5
    gain = lambda k, shape: 1.0 + 0.1 * jax.random.normal(k, shape, jnp.float32)
    L = DEPTH
    return {
        "x": jax.random.normal(ks[0], (BATCH, SEQ, D_MODEL), jnp.float32),
        "mem": jax.random.normal(ks[1], (BATCH, MEM_LEN, D_MODEL), jnp.float32),
        "ffn1_norm": gain(ks[2], (L, D_MODEL)),
        "ffn1_w_gate_up": nrm(ks[3], (L, D_MODEL, 2 * D_FF), D_MODEL),
        "ffn1_w_down": nrm(ks[4], (L, D_FF, D_MODEL), D_FF),
        "mix_norm": gain(ks[5], (L, D_MODEL)),
        "mem_norm": gain(ks[6], (L, D_MODEL)),
        "w_in": nrm(ks[7], (L, D_MODEL, IN_WIDTH), D_MODEL),
        "b_forget": 2.0 + 0.1 * jax.random.normal(ks[8], (L, FOX_HEADS), jnp.float32),
        "pool_w": nrm(ks[9], (L, POOL_GROUPS, POOL_GROUP_DIM, POOL_GROUP_DIM), POOL_GROUP_DIM),
        "pool_scale": gain(ks[10], (L, POOL_WIDTH)),
        "w_pool_up": nrm(ks[11], (L, POOL_WIDTH, D_MODEL), POOL_WIDTH),
        "fox_q_norm": gain(ks[12], (L, FOX_HEAD_DIM)),
        "fox_k_norm": gain(ks[13], (L, FOX_HEAD_DIM)),
        "w_fox_o": nrm(ks[14], (L, FOX_WIDTH, D_MODEL), FOX_WIDTH),
        "w_mem_kv": nrm(ks[15], (L, D_MODEL, 2 * MEM_WIDTH), D_MODEL),
        "mem_q_norm": gain(ks[16], (L, MEM_HEAD_DIM)),
        "mem_k_norm": gain(ks[17], (L, MEM_HEAD_DIM)),
        "w_mem_o": nrm(ks[18], (L, MEM_WIDTH, D_MODEL), MEM_WIDTH),
        "w_out": nrm(ks[19], (L, D_MODEL, D_MODEL), D_MODEL),
        "ffn2_norm": gain(ks[20], (L, D_MODEL)),
        "ffn2_w_gate_up": nrm(ks[21], (L, D_MODEL, 2 * D_FF), D_MODEL),
        "ffn2_w_down": nrm(ks[22], (L, D_FF, D_MODEL), D_FF),
    }


def _fwd_reference(x, mem, ffn1_norm, ffn1_w_gate_up, ffn1_w_down, mix_norm, mem_norm, w_in,
              b_forget, pool_w, pool_scale, w_pool_up, fox_q_norm, fox_k_norm, w_fox_o,
              w_mem_kv, mem_q_norm, mem_k_norm, w_mem_o, w_out,
              ffn2_norm, ffn2_w_gate_up, ffn2_w_down):
    B, S, _ = x.shape
    M = mem.shape[1]
    split_idx = list(np.cumsum(IN_SPLITS)[:-1])
    for l in range(DEPTH):
        x = x + swiglu_half_ffn(x, ffn1_norm[l], ffn1_w_gate_up[l], ffn1_w_down[l])

        h = rmsnorm(x, mix_norm[l])
        u_pool, q_f, k_f, v_f, f_logit, q_m, gate_logit = jnp.split(h @ w_in[l], split_idx, axis=-1)

        y_pool = pool_mixer(u_pool, pool_w[l], pool_scale[l]) @ w_pool_up[l]

        q_f = rmsnorm(q_f.reshape(B, S, FOX_HEADS, FOX_HEAD_DIM), fox_q_norm[l])
        k_f = rmsnorm(k_f.reshape(B, S, FOX_HEADS, FOX_HEAD_DIM), fox_k_norm[l])
        v_f = v_f.reshape(B, S, FOX_HEADS, FOX_HEAD_DIM)
        log_f = jax.nn.log_sigmoid(f_logit.astype(jnp.float32) + b_forget[l].astype(jnp.float32))
        y_fox = forgetting_attention(q_f, k_f, v_f, log_f).reshape(B, S, FOX_WIDTH) @ w_fox_o[l]

        k_m, v_m = jnp.split(rmsnorm(mem, mem_norm[l]) @ w_mem_kv[l], 2, axis=-1)
        q_m = rmsnorm(q_m.reshape(B, S, MEM_HEADS, MEM_HEAD_DIM), mem_q_norm[l])
        k_m = rmsnorm(k_m.reshape(B, M, MEM_HEADS, MEM_HEAD_DIM), mem_k_norm[l])
        v_m = v_m.reshape(B, M, MEM_HEADS, MEM_HEAD_DIM)
        y_mem = memory_attention(q_m, k_m, v_m).reshape(B, S, MEM_WIDTH) @ w_mem_o[l]

        g_pool, g_fox, g_mem = jnp.split(jax.nn.sigmoid(gate_logit), N_BRANCHES, axis=-1)
        merged = g_pool * y_pool + g_fox * y_fox + g_mem * y_mem
        x = x + merged @ w_out[l]

        x = x + swiglu_half_ffn(x, ffn2_norm[l], ffn2_w_gate_up[l], ffn2_w_down[l])
    return x


import jax as _jax
import jax.numpy as _jnp

TWIN_FORMAT = 'train_step'
FWD_PARAMS = ['x', 'mem', 'ffn1_norm', 'ffn1_w_gate_up', 'ffn1_w_down', 'mix_norm', 'mem_norm', 'w_in', 'b_forget', 'pool_w', 'pool_scale', 'w_pool_up', 'fox_q_norm', 'fox_k_norm', 'w_fox_o', 'w_mem_kv', 'mem_q_norm', 'mem_k_norm', 'w_mem_o', 'w_out', 'ffn2_norm', 'ffn2_w_gate_up', 'ffn2_w_down']
TWIN_WEIGHTS = ['ffn1_norm', 'ffn1_w_gate_up', 'ffn1_w_down', 'mix_norm', 'mem_norm', 'w_in', 'b_forget', 'pool_w', 'pool_scale', 'w_pool_up', 'fox_q_norm', 'fox_k_norm', 'w_fox_o', 'w_mem_kv', 'mem_q_norm', 'mem_k_norm', 'w_mem_o', 'w_out', 'ffn2_norm', 'ffn2_w_gate_up', 'ffn2_w_down']
TWIN_DIFF_INPUT = 'x'
TWIN_INPUTS = ['x', 'mem', 'ffn1_norm', 'ffn1_w_gate_up', 'ffn1_w_down', 'mix_norm', 'mem_norm', 'w_in', 'b_forget', 'pool_w', 'pool_scale', 'w_pool_up', 'fox_q_norm', 'fox_k_norm', 'w_fox_o', 'w_mem_kv', 'mem_q_norm', 'mem_k_norm', 'w_mem_o', 'w_out', 'ffn2_norm', 'ffn2_w_gate_up', 'ffn2_w_down', 'loss_target', 'm_ffn1_norm', 'm_ffn1_w_gate_up', 'm_ffn1_w_down', 'm_mix_norm', 'm_mem_norm', 'm_w_in', 'm_b_forget', 'm_pool_w', 'm_pool_scale', 'm_w_pool_up', 'm_fox_q_norm', 'm_fox_k_norm', 'm_w_fox_o', 'm_w_mem_kv', 'm_mem_q_norm', 'm_mem_k_norm', 'm_w_mem_o', 'm_w_out', 'm_ffn2_norm', 'm_ffn2_w_gate_up', 'm_ffn2_w_down', 'v_ffn1_norm', 'v_ffn1_w_gate_up', 'v_ffn1_w_down', 'v_mix_norm', 'v_mem_norm', 'v_w_in', 'v_b_forget', 'v_pool_w', 'v_pool_scale', 'v_w_pool_up', 'v_fox_q_norm', 'v_fox_k_norm', 'v_w_fox_o', 'v_w_mem_kv', 'v_mem_q_norm', 'v_mem_k_norm', 'v_w_mem_o', 'v_w_out', 'v_ffn2_norm', 'v_ffn2_w_gate_up', 'v_ffn2_w_down']
TWIN_OUTPUTS = ['loss', 'grad_x', 'grad_ffn1_norm', 'grad_ffn1_w_gate_up', 'grad_ffn1_w_down', 'grad_mix_norm', 'grad_mem_norm', 'grad_w_in', 'grad_b_forget', 'grad_pool_w', 'grad_pool_scale', 'grad_w_pool_up', 'grad_fox_q_norm', 'grad_fox_k_norm', 'grad_w_fox_o', 'grad_w_mem_kv', 'grad_mem_q_norm', 'grad_mem_k_norm', 'grad_w_mem_o', 'grad_w_out', 'grad_ffn2_norm', 'grad_ffn2_w_gate_up', 'grad_ffn2_w_down', 'delta_ffn1_norm', 'delta_ffn1_w_gate_up', 'delta_ffn1_w_down', 'delta_mix_norm', 'delta_mem_norm', 'delta_w_in', 'delta_b_forget', 'delta_pool_w', 'delta_pool_scale', 'delta_w_pool_up', 'delta_fox_q_norm', 'delta_fox_k_norm', 'delta_w_fox_o', 'delta_w_mem_kv', 'delta_mem_q_norm', 'delta_mem_k_norm', 'delta_w_mem_o', 'delta_w_out', 'delta_ffn2_norm', 'delta_ffn2_w_gate_up', 'delta_ffn2_w_down', 'new_m_ffn1_norm', 'new_m_ffn1_w_gate_up', 'new_m_ffn1_w_down', 'new_m_mix_norm', 'new_m_mem_norm', 'new_m_w_in', 'new_m_b_forget', 'new_m_pool_w', 'new_m_pool_scale', 'new_m_w_pool_up', 'new_m_fox_q_norm', 'new_m_fox_k_norm', 'new_m_w_fox_o', 'new_m_w_mem_kv', 'new_m_mem_q_norm', 'new_m_mem_k_norm', 'new_m_w_mem_o', 'new_m_w_out', 'new_m_ffn2_norm', 'new_m_ffn2_w_gate_up', 'new_m_ffn2_w_down', 'new_v_ffn1_norm', 'new_v_ffn1_w_gate_up', 'new_v_ffn1_w_down', 'new_v_mix_norm', 'new_v_mem_norm', 'new_v_w_in', 'new_v_b_forget', 'new_v_pool_w', 'new_v_pool_scale', 'new_v_w_pool_up', 'new_v_fox_q_norm', 'new_v_fox_k_norm', 'new_v_w_fox_o', 'new_v_w_mem_kv', 'new_v_mem_q_norm', 'new_v_mem_k_norm', 'new_v_w_mem_o', 'new_v_w_out', 'new_v_ffn2_norm', 'new_v_ffn2_w_gate_up', 'new_v_ffn2_w_down']
TWIN_LEAF_KINDS = {'loss': 'loss', 'grad_x': 'grad_x', 'grad_ffn1_norm': 'grad_w', 'grad_ffn1_w_gate_up': 'grad_w', 'grad_ffn1_w_down': 'grad_w', 'grad_mix_norm': 'grad_w', 'grad_mem_norm': 'grad_w', 'grad_w_in': 'grad_w', 'grad_b_forget': 'grad_w', 'grad_pool_w': 'grad_w', 'grad_pool_scale': 'grad_w', 'grad_w_pool_up': 'grad_w', 'grad_fox_q_norm': 'grad_w', 'grad_fox_k_norm': 'grad_w', 'grad_w_fox_o': 'grad_w', 'grad_w_mem_kv': 'grad_w', 'grad_mem_q_norm': 'grad_w', 'grad_mem_k_norm': 'grad_w', 'grad_w_mem_o': 'grad_w', 'grad_w_out': 'grad_w', 'grad_ffn2_norm': 'grad_w', 'grad_ffn2_w_gate_up': 'grad_w', 'grad_ffn2_w_down': 'grad_w', 'delta_ffn1_norm': 'delta_w', 'delta_ffn1_w_gate_up': 'delta_w', 'delta_ffn1_w_down': 'delta_w', 'delta_mix_norm': 'delta_w', 'delta_mem_norm': 'delta_w', 'delta_w_in': 'delta_w', 'delta_b_forget': 'delta_w', 'delta_pool_w': 'delta_w', 'delta_pool_scale': 'delta_w', 'delta_w_pool_up': 'delta_w', 'delta_fox_q_norm': 'delta_w', 'delta_fox_k_norm': 'delta_w', 'delta_w_fox_o': 'delta_w', 'delta_w_mem_kv': 'delta_w', 'delta_mem_q_norm': 'delta_w', 'delta_mem_k_norm': 'delta_w', 'delta_w_mem_o': 'delta_w', 'delta_w_out': 'delta_w', 'delta_ffn2_norm': 'delta_w', 'delta_ffn2_w_gate_up': 'delta_w', 'delta_ffn2_w_down': 'delta_w', 'new_m_ffn1_norm': 'new_m', 'new_m_ffn1_w_gate_up': 'new_m', 'new_m_ffn1_w_down': 'new_m', 'new_m_mix_norm': 'new_m', 'new_m_mem_norm': 'new_m', 'new_m_w_in': 'new_m', 'new_m_b_forget': 'new_m', 'new_m_pool_w': 'new_m', 'new_m_pool_scale': 'new_m', 'new_m_w_pool_up': 'new_m', 'new_m_fox_q_norm': 'new_m', 'new_m_fox_k_norm': 'new_m', 'new_m_w_fox_o': 'new_m', 'new_m_w_mem_kv': 'new_m', 'new_m_mem_q_norm': 'new_m', 'new_m_mem_k_norm': 'new_m', 'new_m_w_mem_o': 'new_m', 'new_m_w_out': 'new_m', 'new_m_ffn2_norm': 'new_m', 'new_m_ffn2_w_gate_up': 'new_m', 'new_m_ffn2_w_down': 'new_m', 'new_v_ffn1_norm': 'new_v', 'new_v_ffn1_w_gate_up': 'new_v', 'new_v_ffn1_w_down': 'new_v', 'new_v_mix_norm': 'new_v', 'new_v_mem_norm': 'new_v', 'new_v_w_in': 'new_v', 'new_v_b_forget': 'new_v', 'new_v_pool_w': 'new_v', 'new_v_pool_scale': 'new_v', 'new_v_w_pool_up': 'new_v', 'new_v_fox_q_norm': 'new_v', 'new_v_fox_k_norm': 'new_v', 'new_v_w_fox_o': 'new_v', 'new_v_w_mem_kv': 'new_v', 'new_v_mem_q_norm': 'new_v', 'new_v_mem_k_norm': 'new_v', 'new_v_w_mem_o': 'new_v', 'new_v_w_out': 'new_v', 'new_v_ffn2_norm': 'new_v', 'new_v_ffn2_w_gate_up': 'new_v', 'new_v_ffn2_w_down': 'new_v'}


def _forward(args):
    return _fwd_reference(*[args[k] for k in FWD_PARAMS])


def _output_shape():
    out = _jax.eval_shape(lambda: _forward(_fwd_setup_inputs(0)))
    return out.shape, out.dtype

N_MICROBATCH = 1
ADAM_LR = 0.001
ADAM_B1 = 0.9
ADAM_B2 = 0.999
ADAM_EPS = 1e-08
ADAM_WD = 0.01
ADAM_STEP = 10
PER_EXAMPLE_BATCH_AXIS = {'x': 0, 'mem': 0, 'loss_target': 0}
SHARED_INPUTS = []
_WEIGHT_DTYPES = {'ffn1_norm': _jnp.float32, 'ffn1_w_gate_up': _jnp.float32, 'ffn1_w_down': _jnp.float32, 'mix_norm': _jnp.float32, 'mem_norm': _jnp.float32, 'w_in': _jnp.float32, 'b_forget': _jnp.float32, 'pool_w': _jnp.float32, 'pool_scale': _jnp.float32, 'w_pool_up': _jnp.float32, 'fox_q_norm': _jnp.float32, 'fox_k_norm': _jnp.float32, 'w_fox_o': _jnp.float32, 'w_mem_kv': _jnp.float32, 'mem_q_norm': _jnp.float32, 'mem_k_norm': _jnp.float32, 'w_mem_o': _jnp.float32, 'w_out': _jnp.float32, 'ffn2_norm': _jnp.float32, 'ffn2_w_gate_up': _jnp.float32, 'ffn2_w_down': _jnp.float32}
MOMENT_SCALE = {'ffn1_norm': 3.074473e+00, 'ffn1_w_gate_up': 3.900988e-02, 'ffn1_w_down': 6.563474e-02, 'mix_norm': 5.093631e+00, 'mem_norm': 4.383903e-02, 'w_in': 1.003495e-01, 'b_forget': 3.120285e+01, 'pool_w': 1.317001e+00, 'pool_scale': 1.513970e+01, 'w_pool_up': 2.734049e-01, 'fox_q_norm': 1.026171e+01, 'fox_k_norm': 1.007320e+01, 'w_fox_o': 4.295142e-02, 'w_mem_kv': 2.904274e-02, 'mem_q_norm': 7.052072e-01, 'mem_k_norm': 7.149200e-01, 'w_mem_o': 1.406559e-02, 'w_out': 2.035126e-01, 'ffn2_norm': 3.197083e+00, 'ffn2_w_gate_up': 3.113529e-02, 'ffn2_w_down': 5.235428e-02}


def _to_microbatches(a, axis):
    t = _jnp.moveaxis(a, axis, 0)
    t = t.reshape((N_MICROBATCH, t.shape[0] // N_MICROBATCH) + t.shape[1:])
    return _jnp.moveaxis(t, 1, axis + 1)


def setup_inputs(seed: int = 0) -> dict:
    inp = _fwd_setup_inputs(seed)
    key = _jax.random.fold_in(_jax.random.key(seed), 7919)
    shape, _ = _output_shape()
    out = dict(inp)
    out["loss_target"] = _jax.random.normal(_jax.random.fold_in(key, 0), shape, _jnp.float32)
    for i, name in enumerate(TWIN_WEIGHTS):
        w = inp[name].astype(_jnp.float32)
        if MOMENT_SCALE is None:
            s = _jnp.sqrt(_jnp.mean(_jnp.square(w)) + 1e-30)
        else:
            s = MOMENT_SCALE[name]
        km, kv = _jax.random.split(_jax.random.fold_in(key, i + 1))
        out[name] = w
        out["m_" + name] = s * _jax.random.normal(km, w.shape, _jnp.float32)
        out["v_" + name] = (s * s) * _jax.random.uniform(kv, w.shape, _jnp.float32, 0.5, 1.5)
    if N_MICROBATCH > 1:
        for name, axis in PER_EXAMPLE_BATCH_AXIS.items():
            out[name] = _to_microbatches(out[name], axis)
    return {'x': out['x'], 'mem': out['mem'], 'ffn1_norm': out['ffn1_norm'], 'ffn1_w_gate_up': out['ffn1_w_gate_up'], 'ffn1_w_down': out['ffn1_w_down'], 'mix_norm': out['mix_norm'], 'mem_norm': out['mem_norm'], 'w_in': out['w_in'], 'b_forget': out['b_forget'], 'pool_w': out['pool_w'], 'pool_scale': out['pool_scale'], 'w_pool_up': out['w_pool_up'], 'fox_q_norm': out['fox_q_norm'], 'fox_k_norm': out['fox_k_norm'], 'w_fox_o': out['w_fox_o'], 'w_mem_kv': out['w_mem_kv'], 'mem_q_norm': out['mem_q_norm'], 'mem_k_norm': out['mem_k_norm'], 'w_mem_o': out['w_mem_o'], 'w_out': out['w_out'], 'ffn2_norm': out['ffn2_norm'], 'ffn2_w_gate_up': out['ffn2_w_gate_up'], 'ffn2_w_down': out['ffn2_w_down'], 'loss_target': out['loss_target'], 'm_ffn1_norm': out['m_ffn1_norm'], 'm_ffn1_w_gate_up': out['m_ffn1_w_gate_up'], 'm_ffn1_w_down': out['m_ffn1_w_down'], 'm_mix_norm': out['m_mix_norm'], 'm_mem_norm': out['m_mem_norm'], 'm_w_in': out['m_w_in'], 'm_b_forget': out['m_b_forget'], 'm_pool_w': out['m_pool_w'], 'm_pool_scale': out['m_pool_scale'], 'm_w_pool_up': out['m_w_pool_up'], 'm_fox_q_norm': out['m_fox_q_norm'], 'm_fox_k_norm': out['m_fox_k_norm'], 'm_w_fox_o': out['m_w_fox_o'], 'm_w_mem_kv': out['m_w_mem_kv'], 'm_mem_q_norm': out['m_mem_q_norm'], 'm_mem_k_norm': out['m_mem_k_norm'], 'm_w_mem_o': out['m_w_mem_o'], 'm_w_out': out['m_w_out'], 'm_ffn2_norm': out['m_ffn2_norm'], 'm_ffn2_w_gate_up': out['m_ffn2_w_gate_up'], 'm_ffn2_w_down': out['m_ffn2_w_down'], 'v_ffn1_norm': out['v_ffn1_norm'], 'v_ffn1_w_gate_up': out['v_ffn1_w_gate_up'], 'v_ffn1_w_down': out['v_ffn1_w_down'], 'v_mix_norm': out['v_mix_norm'], 'v_mem_norm': out['v_mem_norm'], 'v_w_in': out['v_w_in'], 'v_b_forget': out['v_b_forget'], 'v_pool_w': out['v_pool_w'], 'v_pool_scale': out['v_pool_scale'], 'v_w_pool_up': out['v_w_pool_up'], 'v_fox_q_norm': out['v_fox_q_norm'], 'v_fox_k_norm': out['v_fox_k_norm'], 'v_w_fox_o': out['v_w_fox_o'], 'v_w_mem_kv': out['v_w_mem_kv'], 'v_mem_q_norm': out['v_mem_q_norm'], 'v_mem_k_norm': out['v_mem_k_norm'], 'v_w_mem_o': out['v_w_mem_o'], 'v_w_out': out['v_w_out'], 'v_ffn2_norm': out['v_ffn2_norm'], 'v_ffn2_w_gate_up': out['v_ffn2_w_gate_up'], 'v_ffn2_w_down': out['v_ffn2_w_down']}


def _loss(weights, diff, rest, loss_target):
    with _jax.named_scope("forward"):
        args = {**rest, TWIN_DIFF_INPUT: diff, **{k: w.astype(_WEIGHT_DTYPES[k]) for k, w in weights.items()}}
        y = _forward(args)
    with _jax.named_scope("loss_head"):
        err = _jnp.square(y.astype(_jnp.float32) - loss_target)
        return 0.5 * _jnp.sum(_jnp.mean(err, axis=-1)) if err.ndim else 0.5 * err


def _adamw(w, g, m, v):
    m = ADAM_B1 * m + (1.0 - ADAM_B1) * g
    v = ADAM_B2 * v + (1.0 - ADAM_B2) * _jnp.square(g)
    m_hat = m / (1.0 - ADAM_B1 ** ADAM_STEP)
    v_hat = v / (1.0 - ADAM_B2 ** ADAM_STEP)
    delta = -ADAM_LR * (m_hat / (_jnp.sqrt(v_hat) + ADAM_EPS) + ADAM_WD * w)
    return delta, m, v


def reference(x, mem, ffn1_norm, ffn1_w_gate_up, ffn1_w_down, mix_norm, mem_norm, w_in, b_forget, pool_w, pool_scale, w_pool_up, fox_q_norm, fox_k_norm, w_fox_o, w_mem_kv, mem_q_norm, mem_k_norm, w_mem_o, w_out, ffn2_norm, ffn2_w_gate_up, ffn2_w_down, loss_target, m_ffn1_norm, m_ffn1_w_gate_up, m_ffn1_w_down, m_mix_norm, m_mem_norm, m_w_in, m_b_forget, m_pool_w, m_pool_scale, m_w_pool_up, m_fox_q_norm, m_fox_k_norm, m_w_fox_o, m_w_mem_kv, m_mem_q_norm, m_mem_k_norm, m_w_mem_o, m_w_out, m_ffn2_norm, m_ffn2_w_gate_up, m_ffn2_w_down, v_ffn1_norm, v_ffn1_w_gate_up, v_ffn1_w_down, v_mix_norm, v_mem_norm, v_w_in, v_b_forget, v_pool_w, v_pool_scale, v_w_pool_up, v_fox_q_norm, v_fox_k_norm, v_w_fox_o, v_w_mem_kv, v_mem_q_norm, v_mem_k_norm, v_w_mem_o, v_w_out, v_ffn2_norm, v_ffn2_w_gate_up, v_ffn2_w_down):
    given = dict(x=x, mem=mem, ffn1_norm=ffn1_norm, ffn1_w_gate_up=ffn1_w_gate_up, ffn1_w_down=ffn1_w_down, mix_norm=mix_norm, mem_norm=mem_norm, w_in=w_in, b_forget=b_forget, pool_w=pool_w, pool_scale=pool_scale, w_pool_up=w_pool_up, fox_q_norm=fox_q_norm, fox_k_norm=fox_k_norm, w_fox_o=w_fox_o, w_mem_kv=w_mem_kv, mem_q_norm=mem_q_norm, mem_k_norm=mem_k_norm, w_mem_o=w_mem_o, w_out=w_out, ffn2_norm=ffn2_norm, ffn2_w_gate_up=ffn2_w_gate_up, ffn2_w_down=ffn2_w_down, loss_target=loss_target, m_ffn1_norm=m_ffn1_norm, m_ffn1_w_gate_up=m_ffn1_w_gate_up, m_ffn1_w_down=m_ffn1_w_down, m_mix_norm=m_mix_norm, m_mem_norm=m_mem_norm, m_w_in=m_w_in, m_b_forget=m_b_forget, m_pool_w=m_pool_w, m_pool_scale=m_pool_scale, m_w_pool_up=m_w_pool_up, m_fox_q_norm=m_fox_q_norm, m_fox_k_norm=m_fox_k_norm, m_w_fox_o=m_w_fox_o, m_w_mem_kv=m_w_mem_kv, m_mem_q_norm=m_mem_q_norm, m_mem_k_norm=m_mem_k_norm, m_w_mem_o=m_w_mem_o, m_w_out=m_w_out, m_ffn2_norm=m_ffn2_norm, m_ffn2_w_gate_up=m_ffn2_w_gate_up, m_ffn2_w_down=m_ffn2_w_down, v_ffn1_norm=v_ffn1_norm, v_ffn1_w_gate_up=v_ffn1_w_gate_up, v_ffn1_w_down=v_ffn1_w_down, v_mix_norm=v_mix_norm, v_mem_norm=v_mem_norm, v_w_in=v_w_in, v_b_forget=v_b_forget, v_pool_w=v_pool_w, v_pool_scale=v_pool_scale, v_w_pool_up=v_w_pool_up, v_fox_q_norm=v_fox_q_norm, v_fox_k_norm=v_fox_k_norm, v_w_fox_o=v_w_fox_o, v_w_mem_kv=v_w_mem_kv, v_mem_q_norm=v_mem_q_norm, v_mem_k_norm=v_mem_k_norm, v_w_mem_o=v_w_mem_o, v_w_out=v_w_out, v_ffn2_norm=v_ffn2_norm, v_ffn2_w_gate_up=v_ffn2_w_gate_up, v_ffn2_w_down=v_ffn2_w_down)
    weights = {n: given[n] for n in TWIN_WEIGHTS}
    shared = {n: given[n] for n in SHARED_INPUTS}
    per_example = {n: given[n] for n in ['x', 'mem']}
    grad_fn = _jax.value_and_grad(_loss, argnums=(0, 1))

    def one_microbatch(ex, loss_target):
        ex = dict(ex)
        diff = ex.pop(TWIN_DIFF_INPUT)
        return grad_fn(weights, diff, {**shared, **ex}, loss_target)

    if N_MICROBATCH == 1:
        loss, (grad_w, grad_x) = one_microbatch(per_example, given["loss_target"])
    else:
        def body(carry, xs):
            loss_sum, grad_sum = carry
            l_k, (gw_k, gx_k) = one_microbatch(xs[0], xs[1])
            with _jax.named_scope("update"):
                return (loss_sum + l_k, _jax.tree.map(_jnp.add, grad_sum, gw_k)), gx_k

        init = (_jnp.zeros((), _jnp.float32), _jax.tree.map(_jnp.zeros_like, weights))
        (loss, grad_w), grad_x = _jax.lax.scan(body, init, (per_example, given["loss_target"]))
    with _jax.named_scope("update"):
        delta_w, new_m, new_v = {}, {}, {}
        for n in TWIN_WEIGHTS:
            delta_w[n], new_m[n], new_v[n] = _adamw(weights[n], grad_w[n], given["m_" + n], given["v_" + n])
    return (loss, grad_x, *[grad_w[n] for n in TWIN_WEIGHTS], *[delta_w[n] for n in TWIN_WEIGHTS],
            *[new_m[n] for n in TWIN_WEIGHTS], *[new_v[n] for n in TWIN_WEIGHTS])
```

```python
import functools

import jax
import jax.numpy as jnp
from jax import lax
from jax.experimental import pallas as pl
from jax.experimental.pallas import tpu as pltpu

F32 = jnp.float32
BF16 = jnp.bfloat16
MESH = pl.DeviceIdType.MESH
EPS = 1e-6
POOL_WINDOWS = (2, 4, 8, 16)
ADAM_LR, ADAM_B1, ADAM_B2, ADAM_EPS, ADAM_WD, ADAM_STEP = 0.001, 0.9, 0.999, 1e-08, 0.01, 10
LANES = 128
VMEM_LIMIT = 56 * 1024 * 1024
NEG = -1e30


def _pc(body, *, name, **kw):
    return pl.pallas_call(body, name=name, **kw)


def _cp(*sem):
    return pltpu.CompilerParams(dimension_semantics=sem, vmem_limit_bytes=VMEM_LIMIT)


def _tile(n, pref, mult=LANES):
    if n <= pref:
        return n
    t = (pref // mult) * mult
    while t >= mult:
        if n % t == 0:
            return t
        t -= mult
    return n


def _round_up(n, m):
    return (n + m - 1) // m * m


_DIMS = {"nn": (((1,), (0,)), ((), ())), "nt": (((1,), (1,)), ((), ())), "tn": (((0,), (0,)), ((), ()))}


def _mm(a, b, *, name, mode="nn", out_dtype=F32, scale=1.0, res=None, tm=512, tn=512, tk=2048):
    if mode == "nn":
        (M, K), (K2, N) = a.shape, b.shape
    elif mode == "nt":
        (M, K), (N, K2) = a.shape, b.shape
    else:
        (K, M), (K2, N) = a.shape, b.shape
    assert K == K2, (name, a.shape, b.shape)
    tm, tn, tk = _tile(M, tm, 8 if M % LANES else LANES), _tile(N, tn), _tile(K, tk)
    nk = K // tk
    dims = _DIMS[mode]
    a_spec = (pl.BlockSpec((tk, tm), lambda i, j, k: (k, i)) if mode == "tn"
              else pl.BlockSpec((tm, tk), lambda i, j, k: (i, k)))
    b_spec = (pl.BlockSpec((tn, tk), lambda i, j, k: (j, k)) if mode == "nt"
              else pl.BlockSpec((tk, tn), lambda i, j, k: (k, j)))
    o_spec = pl.BlockSpec((tm, tn), lambda i, j, k: (i, j))
    has_res = res is not None

    def body(*refs):
        if has_res:
            a_ref, b_ref, r_ref, o_ref, acc = refs
        else:
            a_ref, b_ref, o_ref, acc = refs
        k = pl.program_id(2)

        @pl.when(k == 0)
        def _():
            acc[...] = jnp.zeros_like(acc)

        acc[...] += lax.dot_general(a_ref[...], b_ref[...], dims, preferred_element_type=F32)

        @pl.when(k == nk - 1)
        def _():
            v = acc[...] * scale
            if has_res:
                v = r_ref[...] + v
            o_ref[...] = v.astype(o_ref.dtype)

    return _pc(body, name=name, out_shape=jax.ShapeDtypeStruct((M, N), out_dtype),
               grid=(M // tm, N // tn, nk),
               in_specs=[a_spec, b_spec] + ([o_spec] if has_res else []), out_specs=o_spec,
               scratch_shapes=[pltpu.VMEM((tm, tn), F32)],
               compiler_params=_cp("parallel", "parallel", "arbitrary"))(*((a, b, res) if has_res else (a, b)))


def _gateup(h, wgu, *, name, tm=512, tn=512):
    M, K = h.shape
    Fh = wgu.shape[1] // 2
    tm, tn = _tile(M, tm), _tile(Fh, tn)
    nj = Fh // tn

    def body(h_ref, wg_ref, wu_ref, g_ref, u_ref, a_ref):
        hv = h_ref[...]
        g = jnp.dot(hv, wg_ref[...], preferred_element_type=F32)
        u = jnp.dot(hv, wu_ref[...], preferred_element_type=F32)
        g_ref[...] = g.astype(BF16)
        u_ref[...] = u.astype(BF16)
        a_ref[...] = (g * jax.nn.sigmoid(g) * u).astype(BF16)

    o_spec = pl.BlockSpec((tm, tn), lambda i, j: (i, j))
    shp = jax.ShapeDtypeStruct((M, Fh), BF16)
    return _pc(body, name=name, out_shape=(shp, shp, shp), grid=(M // tm, nj),
               in_specs=[pl.BlockSpec((tm, K), lambda i, j: (i, 0)),
                         pl.BlockSpec((K, tn), lambda i, j: (0, j)),
                         pl.BlockSpec((K, tn), lambda i, j: (0, j + nj))],
               out_specs=(o_spec, o_spec, o_spec), compiler_params=_cp("parallel", "parallel"))(h, wgu, wgu)


def _rowmap(fn, ins, outs, *, rows, tm, name, accs=()):
    tm = _tile(rows, tm, 8)
    arrs, in_specs = [], []
    for d in ins:
        if d[0] == "row":
            _, a, w, cb = d
            w = a.shape[1] if w is None else w
            in_specs.append(pl.BlockSpec((tm, w), functools.partial(lambda i, cb: (i, cb), cb=cb)))
        else:
            a = d[1]
            in_specs.append(pl.BlockSpec(a.shape, functools.partial(lambda i, nd: (0,) * nd, nd=a.ndim)))
        arrs.append(a)
    out_shape = [jax.ShapeDtypeStruct((rows, w), dt) for w, dt in outs]
    out_specs = [pl.BlockSpec((tm, w), lambda i: (i, 0)) for w, _ in outs]
    for shp, dt in accs:
        out_shape.append(jax.ShapeDtypeStruct(shp, dt))
        out_specs.append(pl.BlockSpec(shp, functools.partial(lambda i, nd: (0,) * nd, nd=len(shp))))
    nin, nout, nacc = len(ins), len(outs), len(accs)

    def body(*refs):
        res = fn(*[r[...] for r in refs[:nin]])
        res = res if isinstance(res, (tuple, list)) else (res,)
        for r, v in zip(refs[nin:nin + nout], res[:nout]):
            r[...] = v.astype(r.dtype)
        if nacc:
            acc_refs = refs[nin + nout:]

            @pl.when(pl.program_id(0) == 0)
            def _():
                for r in acc_refs:
                    r[...] = jnp.zeros_like(r)

            for r, v in zip(acc_refs, res[nout:]):
                r[...] += v

    out = _pc(body, name=name, out_shape=tuple(out_shape), grid=(rows // tm,), in_specs=in_specs,
              out_specs=tuple(out_specs), compiler_params=_cp("arbitrary" if nacc else "parallel"))(*arrs)
    return out[0] if len(out) == 1 else out


def _rms_fwd(x, g, *, name):
    def fn(xv, gv):
        return xv * lax.rsqrt(jnp.mean(xv * xv, axis=-1, keepdims=True) + EPS) * gv
    return _rowmap(fn, [("row", x, None, 0), ("bc", g)], [(x.shape[1], BF16)], rows=x.shape[0], tm=256, name=name)


def _rms_bwd(x, dh, g, dres, *, name):
    D = x.shape[1]

    def fn(xv, dhv, gv, drv):
        r = lax.rsqrt(jnp.mean(xv * xv, axis=-1, keepdims=True) + EPS)
        xh = xv * r
        dhf = dhv.astype(F32)
        dxn = dhf * gv
        dx = drv + r * (dxn - xh * jnp.mean(dxn * xh, axis=-1, keepdims=True))
        return dx, dx, jnp.sum(dhf * xh, axis=0, keepdims=True)

    return _rowmap(fn, [("row", x, None, 0), ("row", dh, None, 0), ("bc", g), ("row", dres, None, 0)],
                   [(D, F32), (D, BF16)], rows=x.shape[0], tm=256, name=name, accs=[((1, D), F32)])


def _swiglu_bwd(da, gate, up, *, name):
    Fh = gate.shape[1]

    def fn(dav, gv, uv):
        d, g, u = dav.astype(F32), gv.astype(F32), uv.astype(F32)
        s = jax.nn.sigmoid(g)
        return jnp.concatenate([d * u * (s * (1.0 + g * (1.0 - s))), d * (g * s)], axis=-1)

    return _rowmap(fn, [("row", da, None, 0), ("row", gate, None, 0), ("row", up, None, 0)],
                   [(2 * Fh, BF16)], rows=da.shape[0], tm=128, name=name)


def _loss(y, tgt, *, name):
    D = y.shape[1]

    def fn(yv, tv):
        e = yv - tv
        d = e * (1.0 / D)
        part = 0.5 * jnp.sum(jnp.sum(e * e, axis=-1, keepdims=True) * (1.0 / D), axis=0, keepdims=True)
        return d, d, jnp.broadcast_to(part, (1, LANES))

    return _rowmap(fn, [("row", y, None, 0), ("row", tgt, None, 0)], [(D, F32), (D, BF16)],
                   rows=y.shape[0], tm=256, name=name, accs=[((1, LANES), F32)])


def _merge_fwd(z, yp, yf, ym, *, name):
    D = yp.shape[1]

    def fn(gp, gf, gm, a, b, c):
        return jax.nn.sigmoid(gp) * a + jax.nn.sigmoid(gf) * b + jax.nn.sigmoid(gm) * c

    return _rowmap(fn, [("row", z, D, 0), ("row", z, D, 1), ("row", z, D, 2),
                        ("row", yp, None, 0), ("row", yf, None, 0), ("row", ym, None, 0)],
                   [(D, BF16)], rows=yp.shape[0], tm=128, name=name)


def _merge_bwd(dm, z, yp, yf, ym, *, name):
    D = yp.shape[1]

    def fn(d, gp, gf, gm, a, b, c):
        outs, dgl = [], []
        for gl, yv in ((gp, a), (gf, b), (gm, c)):
            s = jax.nn.sigmoid(gl)
            outs.append(d * s)
            dgl.append(d * yv * (s * (1.0 - s)))
        return (*outs, jnp.concatenate(dgl, axis=-1))

    return _rowmap(fn, [("row", dm, None, 0), ("row", z, D, 0), ("row", z, D, 1), ("row", z, D, 2),
                        ("row", yp, None, 0), ("row", yf, None, 0), ("row", ym, None, 0)],
                   [(D, BF16), (D, BF16), (D, BF16), (3 * D, BF16)], rows=yp.shape[0], tm=128, name=name)


def _adamw(w, g, m, v, *, name):
    C = w.shape[1]

    def fn(wv, gv, mv, vv):
        m2 = ADAM_B1 * mv + (1.0 - ADAM_B1) * gv
        v2 = ADAM_B2 * vv + (1.0 - ADAM_B2) * (gv * gv)
        m_hat = m2 / (1.0 - ADAM_B1 ** ADAM_STEP)
        v_hat = v2 / (1.0 - ADAM_B2 ** ADAM_STEP)
        return -ADAM_LR * (m_hat / (jnp.sqrt(v_hat) + ADAM_EPS) + ADAM_WD * wv), m2, v2

    tm = max(8, (262144 // C) // 8 * 8)
    return _rowmap(fn, [("row", a, None, 0) for a in (w, g, m, v)], [(C, F32)] * 3,
                   rows=w.shape[0], tm=tm, name=name)


def _addn(arrs, out_dtype, *, name):
    R, C = arrs[0].shape

    def fn(*vs):
        acc = vs[0].astype(F32)
        for v in vs[1:]:
            acc = acc + v.astype(F32)
        return acc

    tm = max(16, (262144 // C) // 16 * 16)
    return _rowmap(fn, [("row", a, None, 0) for a in arrs], [(C, out_dtype)], rows=R, tm=tm, name=name)


def _seg_mean(v, hd):
    if hd == LANES:
        return jnp.mean(v, axis=-1, keepdims=True)
    lo = lax.broadcasted_iota(jnp.int32, v.shape, 1) < hd
    s0 = jnp.sum(jnp.where(lo, v, 0.0), axis=-1, keepdims=True)
    s1 = jnp.sum(jnp.where(lo, 0.0, v), axis=-1, keepdims=True)
    return jnp.where(lo, s0, s1) * (1.0 / hd)


def _gain128(g, hd):
    g = g.reshape(1, hd).astype(F32)
    return g if hd == LANES else jnp.concatenate([g, g], axis=-1)


def _headnorm_fwd(x, cb0, width, gain, hd, *, name):
    rows = x.shape[0]
    tm = _tile(rows, 512, 8)
    g128 = _gain128(gain, hd)

    def body(x_ref, g_ref, o_ref):
        xv = x_ref[...]
        o_ref[...] = (xv * lax.rsqrt(_seg_mean(xv * xv, hd) + EPS) * g_ref[...]).astype(BF16)

    return _pc(body, name=name, out_shape=jax.ShapeDtypeStruct((rows, width), BF16),
               grid=(rows // tm, width // LANES),
               in_specs=[pl.BlockSpec((tm, LANES), lambda i, j: (i, cb0 + j)),
                         pl.BlockSpec((1, LANES), lambda i, j: (0, 0))],
               out_specs=pl.BlockSpec((tm, LANES), lambda i, j: (i, j)),
               compiler_params=_cp("parallel", "parallel"))(x, g128)


def _headnorm_bwd(x, cb0, width, dy, gain, hd, post, spread, out_dtype, *, name):
    rows = x.shape[0]
    tm = _tile(rows, 512, 8)
    g128 = _gain128(gain, hd)
    dyw = 2 * LANES if spread else LANES

    def body(x_ref, dy_ref, g_ref, dx_ref, dg_ref):
        xv = x_ref[...]
        dyv = dy_ref[...]
        if spread:
            dyv = jnp.concatenate([dyv[:, :hd], dyv[:, LANES:LANES + hd]], axis=-1)
        dyv = dyv * post
        r = lax.rsqrt(_seg_mean(xv * xv, hd) + EPS)
        xh = xv * r
        dxn = dyv * g_ref[...]
        dx_ref[...] = (r * (dxn - xh * _seg_mean(dxn * xh, hd))).astype(dx_ref.dtype)

        @pl.when((pl.program_id(0) == 0) & (pl.program_id(1) == 0))
        def _():
            dg_ref[...] = jnp.zeros_like(dg_ref)

        dg_ref[...] += jnp.sum(dyv * xh, axis=0, keepdims=True)

    return _pc(body, name=name,
               out_shape=(jax.ShapeDtypeStruct((rows, width), out_dtype), jax.ShapeDtypeStruct((1, LANES), F32)),
               grid=(rows // tm, width // LANES),
               in_specs=[pl.BlockSpec((tm, LANES), lambda i, j: (i, cb0 + j)),
                         pl.BlockSpec((tm, dyw), lambda i, j: (i, j)),
                         pl.BlockSpec((1, LANES), lambda i, j: (0, 0))],
               out_specs=(pl.BlockSpec((tm, LANES), lambda i, j: (i, j)),
                          pl.BlockSpec((1, LANES), lambda i, j: (0, 0))),
               compiler_params=_cp("arbitrary", "arbitrary"))(x, dy, g128)


def _fold_gain(dg, hd):
    return dg if hd == LANES else dg[:, :hd] + dg[:, hd:]


def _shift_down(v, k, row):
    return jnp.where(row >= k, pltpu.roll(v, k, 0), 0.0)


def _shift_up(v, k, row, S):
    return jnp.where(row < S - k, pltpu.roll(v, S - k, 0), 0.0)


def _window_sum(v, g, row, shift):
    s, outs = v, []
    for k in (1, 2, 4, 8):
        s = s + shift(s, k)
        outs.append(s)
    return jnp.where(g == 0, outs[0], jnp.where(g == 1, outs[1], jnp.where(g == 2, outs[2], outs[3])))


def _pool_fwd(z, cb0, pool_w, pool_scale, nb, S, *, name):
    PG, PD = pool_w.shape[0], pool_w.shape[1]
    assert PD == LANES and PG <= len(POOL_WINDOWS)
    pw = pool_w.astype(BF16)

    def body(u_ref, w_ref, s_ref, o_ref):
        g = pl.program_id(0)
        u = u_ref[...]
        row = lax.broadcasted_iota(jnp.int32, u.shape, 0)
        cnt = jnp.minimum(row + 1, jnp.left_shift(2, g)).astype(F32)
        pooled = _window_sum(u, g, row, lambda v, k: _shift_down(v, k, row)) / cnt
        mixed = jnp.dot((pooled - u).astype(BF16), w_ref[...], preferred_element_type=F32)
        o_ref[...] = (mixed * s_ref[...]).astype(BF16)

    return _pc(body, name=name, out_shape=jax.ShapeDtypeStruct((nb * S, PG * PD), BF16), grid=(PG, nb),
               in_specs=[pl.BlockSpec((S, PD), lambda g, b: (b, cb0 + g)),
                         pl.BlockSpec((None, PD, PD), lambda g, b: (g, 0, 0)),
                         pl.BlockSpec((1, PD), lambda g, b: (0, g))],
               out_specs=pl.BlockSpec((S, PD), lambda g, b: (b, g)),
               compiler_params=_cp("parallel", "parallel"))(z, pw, pool_scale)


def _pool_bwd(z, cb0, dv, pool_w, pool_scale, nb, S, *, name):
    PG, PD = pool_w.shape[0], pool_w.shape[1]
    pw = pool_w.astype(BF16)

    def body(u_ref, dv_ref, w_ref, s_ref, du_ref, dw_ref, ds_ref):
        g, b = pl.program_id(0), pl.program_id(1)
        u, dvv, w = u_ref[...], dv_ref[...], w_ref[...]
        row = lax.broadcasted_iota(jnp.int32, u.shape, 0)
        cnt = jnp.minimum(row + 1, jnp.left_shift(2, g)).astype(F32)
        diff = (_window_sum(u, g, row, lambda v, k: _shift_down(v, k, row)) / cnt - u).astype(BF16)
        mixed = jnp.dot(diff, w, preferred_element_type=F32)
        dmixed = (dvv * s_ref[...]).astype(BF16)
        ddiff = lax.dot_general(dmixed, w, _DIMS["nt"], preferred_element_type=F32)
        du_ref[...] = (_window_sum(ddiff / cnt, g, row, lambda v, k: _shift_up(v, k, row, S)) - ddiff).astype(BF16)

        @pl.when(b == 0)
        def _():
            dw_ref[...] = jnp.zeros_like(dw_ref)
            ds_ref[...] = jnp.zeros_like(ds_ref)

        dw_ref[...] += lax.dot_general(diff, dmixed, _DIMS["tn"], preferred_element_type=F32)
        ds_ref[...] += jnp.sum(dvv * mixed, axis=0, keepdims=True)

    return _pc(body, name=name,
               out_shape=(jax.ShapeDtypeStruct((nb * S, PG * PD), BF16), jax.ShapeDtypeStruct((PG, PD, PD), F32),
                          jax.ShapeDtypeStruct((1, PG * PD), F32)),
               grid=(PG, nb),
               in_specs=[pl.BlockSpec((S, PD), lambda g, b: (b, cb0 + g)),
                         pl.BlockSpec((S, PD), lambda g, b: (b, g)),
                         pl.BlockSpec((None, PD, PD), lambda g, b: (g, 0, 0)),
                         pl.BlockSpec((1, PD), lambda g, b: (0, g))],
               out_specs=(pl.BlockSpec((S, PD), lambda g, b: (b, g)),
                          pl.BlockSpec((None, PD, PD), lambda g, b: (g, 0, 0)),
                          pl.BlockSpec((1, PD), lambda g, b: (0, g))),
               compiler_params=_cp("parallel", "arbitrary"))(z, dv, pw, pool_scale)


def _split3(c):
    hi = c.astype(BF16)
    r1 = c - hi.astype(F32)
    mid = r1.astype(BF16)
    lo = (r1 - mid.astype(F32)).astype(BF16)
    return hi, mid, lo


def _log_sigmoid(v):
    return jnp.minimum(v, 0.0) - jnp.log(1.0 + jnp.exp(-jnp.abs(v)))


def _fox_c(z, cb, b128, nb, S, NF, *, name):
    def body(f_ref, b_ref, hi_ref, mid_ref, lo_ref):
        v = f_ref[...] + b_ref[...]
        lane = lax.broadcasted_iota(jnp.int32, v.shape, 1)
        row = lax.broadcasted_iota(jnp.int32, v.shape, 0)
        c = jnp.where(lane < NF, _log_sigmoid(v), 0.0)
        k = 1
        while k < S:
            c = c + _shift_down(c, k, row)
            k *= 2
        hi, mid, lo = _split3(c)
        hi_ref[...], mid_ref[...], lo_ref[...] = hi, mid, lo

    shp = jax.ShapeDtypeStruct((nb * S, LANES), BF16)
    spec = pl.BlockSpec((S, LANES), lambda i: (i, 0))
    return _pc(body, name=name, out_shape=(shp, shp, shp), grid=(nb,),
               in_specs=[pl.BlockSpec((S, LANES), lambda i: (i, cb)), pl.BlockSpec((1, LANES), lambda i: (0, 0))],
               out_specs=(spec, spec, spec), compiler_params=_cp("parallel"))(z, b128)


def _fox_c_bwd(dc, z, cb, b128, nb, S, *, name):
    def body(dc_ref, f_ref, b_ref, df_ref, db_ref):
        d = dc_ref[...]
        row = lax.broadcasted_iota(jnp.int32, d.shape, 0)
        k = 1
        while k < S:
            d = d + _shift_up(d, k, row, S)
            k *= 2
        df = d * jax.nn.sigmoid(-(f_ref[...] + b_ref[...]))
        df_ref[...] = df

        @pl.when(pl.program_id(0) == 0)
        def _():
            db_ref[...] = jnp.zeros_like(db_ref)

        db_ref[...] += jnp.sum(df, axis=0, keepdims=True)

    return _pc(body, name=name,
               out_shape=(jax.ShapeDtypeStruct((nb * S, LANES), F32), jax.ShapeDtypeStruct((1, LANES), F32)),
               grid=(nb,),
               in_specs=[pl.BlockSpec((S, LANES), lambda i: (i, 0)), pl.BlockSpec((S, LANES), lambda i: (i, cb)),
                         pl.BlockSpec((1, LANES), lambda i: (0, 0))],
               out_specs=(pl.BlockSpec((S, LANES), lambda i: (i, 0)), pl.BlockSpec((1, LANES), lambda i: (0, 0))),
               compiler_params=_cp("arbitrary"))(dc, z, b128)


def _fox_qk(z, cbq, cbk, cbv, c3, gq, gk, HD, NF, *, name):
    T = z.shape[0]
    HP = NF * HD // LANES
    tm = _tile(T, 512, 8)
    scale = HD ** -0.5
    gq128, gk128 = _gain128(gq, HD), _gain128(gk, HD)

    def body(q_ref, k_ref, v_ref, hi_ref, mid_ref, lo_ref, gq_ref, gk_ref, qa_ref, ka_ref, vb_ref):
        hp = pl.program_id(1)
        q, k = q_ref[...], k_ref[...]
        qn = q * lax.rsqrt(_seg_mean(q * q, HD) + EPS) * (gq_ref[...] * scale)
        kn = k * lax.rsqrt(_seg_mean(k * k, HD) + EPS) * gk_ref[...]
        vb_ref[...] = v_ref[...].astype(BF16)
        c3v = jnp.concatenate([hi_ref[...], mid_ref[...], lo_ref[...]], axis=-1)
        r = lax.broadcasted_iota(jnp.int32, (3 * LANES, LANES - HD), 0)
        cc = lax.broadcasted_iota(jnp.int32, (3 * LANES, LANES - HD), 1)
        lane = lax.broadcasted_iota(jnp.int32, (tm, LANES - HD), 1)
        for hh in range(LANES // HD):
            head = hp * (LANES // HD) + hh
            sel_q = jnp.where((cc < 3) & (r == head + LANES * cc), 1.0, 0.0).astype(BF16)
            sel_k = jnp.where((cc >= 3) & (cc < 6) & (r == head + LANES * (cc - 3)), 1.0, 0.0).astype(BF16)
            qaug = jnp.dot(c3v, sel_q, preferred_element_type=F32) + jnp.where((lane >= 3) & (lane < 6), 1.0, 0.0)
            kaug = jnp.where(lane < 3, 1.0, 0.0) - jnp.dot(c3v, sel_k, preferred_element_type=F32)
            sl = slice(hh * HD, (hh + 1) * HD)
            qa_ref[:, hh * LANES:(hh + 1) * LANES] = jnp.concatenate([qn[:, sl], qaug], axis=-1).astype(BF16)
            ka_ref[:, hh * LANES:(hh + 1) * LANES] = jnp.concatenate([kn[:, sl], kaug], axis=-1).astype(BF16)

    aw = (LANES // HD) * LANES
    blk = lambda cb: pl.BlockSpec((tm, LANES), functools.partial(lambda i, j, cb: (i, cb + j), cb=cb))
    cspec = pl.BlockSpec((tm, LANES), lambda i, j: (i, 0))
    gspec = pl.BlockSpec((1, LANES), lambda i, j: (0, 0))
    return _pc(body, name=name,
               out_shape=(jax.ShapeDtypeStruct((T, HP * aw), BF16), jax.ShapeDtypeStruct((T, HP * aw), BF16),
                          jax.ShapeDtypeStruct((T, NF * HD), BF16)),
               grid=(T // tm, HP),
               in_specs=[blk(cbq), blk(cbk), blk(cbv), cspec, cspec, cspec, gspec, gspec],
               out_specs=(pl.BlockSpec((tm, aw), lambda i, j: (i, j)), pl.BlockSpec((tm, aw), lambda i, j: (i, j)),
                          pl.BlockSpec((tm, LANES), lambda i, j: (i, j))),
               compiler_params=_cp("parallel", "parallel"))(z, z, z, *c3, gq128, gk128)


def _fox_probs(qa, ka, q0):
    s = lax.dot_general(qa, ka, _DIMS["nt"], preferred_element_type=F32)
    row = lax.broadcasted_iota(jnp.int32, s.shape, 0) + q0
    col = lax.broadcasted_iota(jnp.int32, s.shape, 1)
    s = jnp.where(col <= row, s, NEG)
    e = jnp.exp(s - jnp.max(s, axis=-1, keepdims=True))
    return e / jnp.sum(e, axis=-1, keepdims=True)


def _fox_fwd(qa, ka, vb, nb, S, HD, *, name, tq=256):
    T = qa.shape[0]
    nh = LANES // HD
    aw = nh * LANES
    HP = qa.shape[1] // aw
    tq = _tile(S, tq, 8)
    nq = S // tq

    def body(qa_ref, ka_ref, v_ref, o_ref):
        q0 = pl.program_id(2) * tq
        outs = []
        v = v_ref[...]
        for hh in range(nh):
            p = _fox_probs(qa_ref[:, hh * LANES:(hh + 1) * LANES], ka_ref[:, hh * LANES:(hh + 1) * LANES], q0)
            outs.append(jnp.dot(p.astype(BF16), v[:, hh * HD:(hh + 1) * HD], preferred_element_type=F32))
        o_ref[...] = jnp.concatenate(outs, axis=-1).astype(BF16)

    return _pc(body, name=name, out_shape=jax.ShapeDtypeStruct((T, HP * LANES), BF16), grid=(nb, HP, nq),
               in_specs=[pl.BlockSpec((tq, aw), lambda b, h, i: (b * nq + i, h)),
                         pl.BlockSpec((S, aw), lambda b, h, i: (b, h)),
                         pl.BlockSpec((S, LANES), lambda b, h, i: (b, h))],
               out_specs=pl.BlockSpec((tq, LANES), lambda b, h, i: (b * nq + i, h)),
               compiler_params=_cp("parallel", "parallel", "arbitrary"))(qa, ka, vb)


def _fox_bwd(qa, ka, vb, do, nb, S, HD, *, name, tq=256):
    T = qa.shape[0]
    nh = LANES // HD
    aw = nh * LANES
    HP = qa.shape[1] // aw
    tq = _tile(S, tq, 8)
    nq = S // tq

    def body(qa_ref, ka_ref, v_ref, do_ref, dqa_ref, dka_ref, dv_ref, dcs_ref):
        i = pl.program_id(2)
        q0 = i * tq
        dkas, dvs, css = [], [], []
        v, dob = v_ref[...], do_ref[...]
        for hh in range(nh):
            al = slice(hh * LANES, (hh + 1) * LANES)
            hl = slice(hh * HD, (hh + 1) * HD)
            qav, kav, vv, dov = qa_ref[:, al], ka_ref[:, al], v[:, hl], dob[:, hl]
            p = _fox_probs(qav, kav, q0)
            dp = lax.dot_general(dov, vv, _DIMS["nt"], preferred_element_type=F32)
            dsf = p * (dp - jnp.sum(p * dp, axis=-1, keepdims=True))
            css.append(jnp.sum(dsf, axis=0, keepdims=True))
            ds = dsf.astype(BF16)
            dqa_ref[:, al] = jnp.dot(ds, kav, preferred_element_type=F32)
            dkas.append(lax.dot_general(ds, qav, _DIMS["tn"], preferred_element_type=F32))
            dvs.append(lax.dot_general(p.astype(BF16), dov, _DIMS["tn"], preferred_element_type=F32))

        @pl.when(i == 0)
        def _():
            dka_ref[...] = jnp.zeros_like(dka_ref)
            dv_ref[...] = jnp.zeros_like(dv_ref)
            dcs_ref[...] = jnp.zeros_like(dcs_ref)

        dka_ref[...] += jnp.concatenate(dkas, axis=-1)
        dv_ref[...] += jnp.concatenate(dvs, axis=-1)
        dcs_ref[...] += jnp.concatenate(css + [jnp.zeros((8 - nh, S), F32)], axis=0)

    return _pc(body, name=name,
               out_shape=(jax.ShapeDtypeStruct((T, HP * aw), F32), jax.ShapeDtypeStruct((T, HP * aw), F32),
                          jax.ShapeDtypeStruct((T, HP * LANES), F32), jax.ShapeDtypeStruct((nb, HP, 8, S), F32)),
               grid=(nb, HP, nq),
               in_specs=[pl.BlockSpec((tq, aw), lambda b, h, i: (b * nq + i, h)),
                         pl.BlockSpec((S, aw), lambda b, h, i: (b, h)),
                         pl.BlockSpec((S, LANES), lambda b, h, i: (b, h)),
                         pl.BlockSpec((tq, LANES), lambda b, h, i: (b * nq + i, h))],
               out_specs=(pl.BlockSpec((tq, aw), lambda b, h, i: (b * nq + i, h)),
                          pl.BlockSpec((S, aw), lambda b, h, i: (b, h)),
                          pl.BlockSpec((S, LANES), lambda b, h, i: (b, h)),
                          pl.BlockSpec((None, None, 8, S), lambda b, h, i: (b, h, 0, 0))),
               compiler_params=_cp("parallel", "parallel", "arbitrary"))(qa, ka, vb, do)


def _mem_probs(q, k, scale):
    s = lax.dot_general(q, k, _DIMS["nt"], preferred_element_type=F32) * scale
    e = jnp.exp(s - jnp.max(s, axis=-1, keepdims=True))
    return e / jnp.sum(e, axis=-1, keepdims=True)


def _mem_fwd(q, k, v, nb, S, ML, *, name, tq=512):
    T, MW = q.shape
    nh = MW // LANES
    tq = _tile(S, tq, 8)
    nq = S // tq
    scale = LANES ** -0.5

    def body(q_ref, k_ref, v_ref, o_ref):
        for h in range(nh):
            sl = slice(h * LANES, (h + 1) * LANES)
            p = _mem_probs(q_ref[:, sl], k_ref[:, sl], scale)
            o_ref[:, sl] = jnp.dot(p.astype(BF16), v_ref[:, sl], preferred_element_type=F32).astype(BF16)

    return _pc(body, name=name, out_shape=jax.ShapeDtypeStruct((T, MW), BF16), grid=(nb, nq),
               in_specs=[pl.BlockSpec((tq, MW), lambda b, i: (b * nq + i, 0)),
                         pl.BlockSpec((ML, MW), lambda b, i: (b, 0)), pl.BlockSpec((ML, MW), lambda b, i: (b, 0))],
               out_specs=pl.BlockSpec((tq, MW), lambda b, i: (b * nq + i, 0)),
               compiler_params=_cp("parallel", "arbitrary"))(q, k, v)


def _mem_bwd(q, k, v, do, nb, S, ML, *, name, tq=512):
    T, MW = q.shape
    nh = MW // LANES
    tq = _tile(S, tq, 8)
    nq = S // tq
    scale = LANES ** -0.5

    def body(q_ref, k_ref, v_ref, do_ref, dq_ref, dk_ref, dv_ref):
        i = pl.program_id(1)

        @pl.when(i == 0)
        def _():
            dk_ref[...] = jnp.zeros_like(dk_ref)
            dv_ref[...] = jnp.zeros_like(dv_ref)

        for h in range(nh):
            sl = slice(h * LANES, (h + 1) * LANES)
            qv, kv, vv, dov = q_ref[:, sl], k_ref[:, sl], v_ref[:, sl], do_ref[:, sl]
            p = _mem_probs(qv, kv, scale)
            dp = lax.dot_general(dov, vv, _DIMS["nt"], preferred_element_type=F32)
            ds = (p * (dp - jnp.sum(p * dp, axis=-1, keepdims=True)) * scale).astype(BF16)
            dq_ref[:, sl] = jnp.dot(ds, kv, preferred_element_type=F32)
            dk_ref[:, sl] += lax.dot_general(ds, qv, _DIMS["tn"], preferred_element_type=F32)
            dv_ref[:, sl] += lax.dot_general(p.astype(BF16), dov, _DIMS["tn"], preferred_element_type=F32)

    kvspec = pl.BlockSpec((ML, MW), lambda b, i: (b, 0))
    qspec = pl.BlockSpec((tq, MW), lambda b, i: (b * nq + i, 0))
    return _pc(body, name=name,
               out_shape=(jax.ShapeDtypeStruct((T, MW), F32), jax.ShapeDtypeStruct((nb * ML, MW), F32),
                          jax.ShapeDtypeStruct((nb * ML, MW), F32)),
               grid=(nb, nq), in_specs=[qspec, kvspec, kvspec, qspec], out_specs=(qspec, kvspec, kvspec),
               compiler_params=_cp("parallel", "arbitrary"))(q, k, v, do)


def _place():
    x, y, c = lax.axis_index("x"), lax.axis_index("y"), lax.axis_index("c")
    chips = [(1 - x, y), (x, 1 - y), (1 - x, 1 - y)]
    return x, y, c, chips


def _shard_win(ref, kind, R, C, j, h=None):
    if kind == "col":
        rows = pl.ds(0, R) if h is None else pl.ds(pl.multiple_of(h * (R // 2), 16), R // 2)
        return ref.at[rows, pl.ds(pl.multiple_of(j * C, LANES), C)]
    if h is None:
        return ref.at[pl.ds(pl.multiple_of(j * R, 16), R), :]
    return ref.at[pl.ds(pl.multiple_of(j * R + h * (R // 2), 16), R // 2), :]


def _any_specs(n):
    return [pl.BlockSpec(memory_space=pl.ANY)] * n


def _all_gather_weights(shards, kinds):
    n = len(shards)
    dims = [s.shape for s in shards]
    out_shape = [jax.ShapeDtypeStruct((R, 4 * C) if k == "col" else (4 * R, C), BF16)
                 for (R, C), k in zip(dims, kinds)]

    def body(*refs):
        ins, outs = refs[:n], refs[n:2 * n]
        send, recv, fsend, frecv, loc = refs[2 * n:]
        x, y, c, chips = _place()
        j = 2 * x + y
        local, sends, fwds = [], [], []
        for w in range(n):
            (R, C), kind = dims[w], kinds[w]
            cp = pltpu.make_async_copy(ins[w], _shard_win(outs[w], kind, R, C, j), loc.at[w])
            cp.start()
            local.append(cp)
            src = ins[w].at[pl.ds(pl.multiple_of(c * (R // 2), 16), R // 2), :]
            for q, chip in enumerate(chips):
                cp = pltpu.make_async_remote_copy(src, _shard_win(outs[w], kind, R, C, j, c), send.at[3 * w + q],
                                                  recv.at[3 * w + q], device_id=(*chip, c), device_id_type=MESH)
                cp.start()
                sends.append(cp)
        for w in range(n):
            (R, C), kind = dims[w], kinds[w]
            for q, (cx, cy) in enumerate(chips):
                win = _shard_win(outs[w], kind, R, C, 2 * cx + cy, c)
                pltpu.make_async_remote_copy(win, win, send.at[3 * w + q], recv.at[3 * w + q],
                                             device_id=(x, y, c), device_id_type=MESH).wait_recv()
                cp = pltpu.make_async_remote_copy(win, win, fsend.at[3 * w + q], frecv.at[3 * w + q],
                                                  device_id=(x, y, 1 - c), device_id_type=MESH)
                cp.start()
                fwds.append(cp)
        for w in range(n):
            (R, C), kind = dims[w], kinds[w]
            for q, (cx, cy) in enumerate(chips):
                win = _shard_win(outs[w], kind, R, C, 2 * cx + cy, 1 - c)
                pltpu.make_async_remote_copy(win, win, fsend.at[3 * w + q], frecv.at[3 * w + q],
                                             device_id=(x, y, c), device_id_type=MESH).wait_recv()
        for cp in sends + fwds:
            cp.wait_send()
        for cp in local:
            cp.wait()

    return _pc(body, name="ag_weights", out_shape=tuple(out_shape), in_specs=_any_specs(n),
               out_specs=tuple(_any_specs(n)),
               scratch_shapes=[pltpu.SemaphoreType.DMA((3 * n,))] * 4 + [pltpu.SemaphoreType.DMA((n,))])(*shards)


def _halves(g, kind):
    K, N = g.shape
    return g.reshape(2, K // 2, N) if kind == "col" else g.reshape(4, 2, K // 8, N)


def _rs_pair(grads, kinds):
    n = len(grads)
    views = [_halves(g, k) for g, k in zip(grads, kinds)]
    half_shape = [(v.shape[1:] if k == "col" else (4, 1) + v.shape[2:]) for v, k in zip(views, kinds)]
    out_shape = [jax.ShapeDtypeStruct(s, BF16) for s in half_shape]

    def body(*refs):
        ins, own, got = refs[:n], refs[n:2 * n], refs[2 * n:3 * n]
        send, recv, loc = refs[3 * n:]
        x, y, c, _ = _place()
        cps = []
        for w in range(n):
            pick = (lambda h, w=w: ins[w].at[h]) if kinds[w] == "col" else (lambda h, w=w: ins[w].at[:, pl.ds(h, 1)])
            cp = pltpu.make_async_copy(pick(c), own[w], loc.at[w])
            cp.start()
            cps.append(cp)
            cp = pltpu.make_async_remote_copy(pick(1 - c), got[w], send.at[w], recv.at[w],
                                              device_id=(x, y, 1 - c), device_id_type=MESH)
            cp.start()
            cps.append(cp)
        for cp in cps:
            cp.wait()

    res = _pc(body, name="rs_pair", out_shape=tuple(out_shape + out_shape), in_specs=_any_specs(n),
              out_specs=tuple(_any_specs(2 * n)),
              scratch_shapes=[pltpu.SemaphoreType.DMA((n,))] * 3)(*views)
    flat = [r.reshape(-1, r.shape[-1]) for r in res]
    return flat[:n], flat[n:]


def _rs_scatter(parts, kinds):
    n = len(parts)
    pdims = [((p.shape[0], p.shape[1] // 4) if k == "col" else (p.shape[0] // 4, p.shape[1]))
             for p, k in zip(parts, kinds)]
    own_shape = [jax.ShapeDtypeStruct(d, BF16) for d in pdims]
    got_shape = [jax.ShapeDtypeStruct((3,) + d, BF16) for d in pdims]

    def body(*refs):
        ins, own, got = refs[:n], refs[n:2 * n], refs[2 * n:3 * n]
        send, recv, loc = refs[3 * n:]
        x, y, c, chips = _place()
        j = 2 * x + y
        cps = []
        for w in range(n):
            R, C = pdims[w]
            cp = pltpu.make_async_copy(_shard_win(ins[w], kinds[w], R, C, j), own[w], loc.at[w])
            cp.start()
            cps.append(cp)
            for q, (cx, cy) in enumerate(chips):
                cp = pltpu.make_async_remote_copy(_shard_win(ins[w], kinds[w], R, C, 2 * cx + cy), got[w].at[q],
                                                  send.at[3 * w + q], recv.at[3 * w + q],
                                                  device_id=(cx, cy, c), device_id_type=MESH)
                cp.start()
                cps.append(cp)
        for cp in cps:
            cp.wait()

    res = _pc(body, name="rs_scatter", out_shape=tuple(own_shape + got_shape), in_specs=_any_specs(n),
              out_specs=tuple(_any_specs(2 * n)),
              scratch_shapes=[pltpu.SemaphoreType.DMA((3 * n,))] * 2 + [pltpu.SemaphoreType.DMA((n,))])(*parts)
    return res[:n], res[n:]


def _rs_join(pieces):
    n = len(pieces)
    out_shape = [jax.ShapeDtypeStruct((2,) + p.shape, F32) for p in pieces]

    def body(*refs):
        ins, outs = refs[:n], refs[n:2 * n]
        send, recv, loc = refs[2 * n:]
        x, y, c, _ = _place()
        cps = []
        for w in range(n):
            cp = pltpu.make_async_copy(ins[w], outs[w].at[c], loc.at[w])
            cp.start()
            cps.append(cp)
            cp = pltpu.make_async_remote_copy(ins[w], outs[w].at[c], send.at[w], recv.at[w],
                                              device_id=(x, y, 1 - c), device_id_type=MESH)
            cp.start()
            cps.append(cp)
        for cp in cps:
            cp.wait()

    res = _pc(body, name="rs_join", out_shape=tuple(out_shape), in_specs=_any_specs(n),
              out_specs=tuple(_any_specs(n)), scratch_shapes=[pltpu.SemaphoreType.DMA((n,))] * 3)(*pieces)
    return [r.reshape(-1, r.shape[-1]) for r in res]


def _all_reduce_small(pack):
    R = pack.shape[0]

    def body(p_ref, o_ref, buf, send, recv):
        x, y, c, _ = _place()
        me = 4 * x + 2 * y + c
        buf[0] = p_ref[...]
        cps = []
        for r in range(1, 8):
            peer = (x if r & 4 == 0 else 1 - x, y if r & 2 == 0 else 1 - y, c if r & 1 == 0 else 1 - c)
            cp = pltpu.make_async_remote_copy(p_ref, buf.at[r], send.at[r - 1], recv.at[r - 1],
                                              device_id=peer, device_id_type=MESH)
            cp.start()
            cps.append(cp)
        for cp in cps:
            cp.wait()
        acc = buf[jnp.bitwise_xor(me, 0)]
        for k in range(1, 8):
            acc = acc + buf[jnp.bitwise_xor(me, k)]
        o_ref[...] = acc

    return _pc(body, name="allreduce_small", out_shape=jax.ShapeDtypeStruct((R, LANES), F32),
               in_specs=[pl.BlockSpec(memory_space=pltpu.VMEM)], out_specs=pl.BlockSpec(memory_space=pltpu.VMEM),
               scratch_shapes=[pltpu.VMEM((8, R, LANES), F32), pltpu.SemaphoreType.DMA((7,)),
                               pltpu.SemaphoreType.DMA((7,))])(pack)


def _pack_small(arrs):
    parts, sizes = [], []
    for a in arrs:
        f = a.reshape(-1).astype(F32)
        n = _round_up(f.shape[0], 8 * LANES)
        parts.append(jnp.pad(f, (0, n - f.shape[0])).reshape(n // LANES, LANES))
        sizes.append((a.shape, f.shape[0], n // LANES))
    return jnp.concatenate(parts, axis=0), sizes


def _unpack_small(pack, sizes):
    out, r0 = [], 0
    for shape, cnt, rows in sizes:
        out.append(pack[r0:r0 + rows].reshape(-1)[:cnt].reshape(shape))
        r0 += rows
    return out


def _ffn_fwd(xin, norm, wgu, wd, tag):
    hb = _rms_fwd(xin, norm, name=tag + "_norm")
    gate, up, act = _gateup(hb, wgu, name=tag + "_gateup")
    xout = _mm(act, wd, res=xin, scale=0.5, name=tag + "_down", tm=1024, tn=512, tk=512)
    return xout, (hb, gate, up, act)


def _ffn_bwd(dy, dyb, xin, norm, wgu, wd, saved, tag):
    hb, gate, up, act = saved
    da = _mm(dyb, wd, mode="nt", scale=0.5, out_dtype=BF16, name=tag + "_dact", tm=1024, tn=512)
    d_wd = _mm(act, dyb, mode="tn", scale=0.5, out_dtype=BF16, name=tag + "_dwd", tm=512, tn=512, tk=4096)
    dgu = _swiglu_bwd(da, gate, up, name=tag + "_dswiglu")
    d_wgu = _mm(hb, dgu, mode="tn", out_dtype=BF16, name=tag + "_dwgu", tm=512, tn=512, tk=4096)
    dh = _mm(dgu, wgu, mode="nt", out_dtype=BF16, name=tag + "_dh", tm=1024, tn=512, tk=1024)
    dx, dxb, d_norm = _rms_bwd(xin, dh, norm, dy, name=tag + "_dnorm")
    return dx, dxb, d_wgu, d_wd, d_norm


def kernel(x, mem, ffn1_norm, ffn1_w_gate_up, ffn1_w_down, mix_norm, mem_norm, w_in, b_forget, pool_w, pool_scale, w_pool_up, fox_q_norm, fox_k_norm, w_fox_o, w_mem_kv, mem_q_norm, mem_k_norm, w_mem_o, w_out, ffn2_norm, ffn2_w_gate_up, ffn2_w_down, loss_target, m_ffn1_norm, m_ffn1_w_gate_up, m_ffn1_w_down, m_mix_norm, m_mem_norm, m_w_in, m_b_forget, m_pool_w, m_pool_scale, m_w_pool_up, m_fox_q_norm, m_fox_k_norm, m_w_fox_o, m_w_mem_kv, m_mem_q_norm, m_mem_k_norm, m_w_mem_o, m_w_out, m_ffn2_norm, m_ffn2_w_gate_up, m_ffn2_w_down, v_ffn1_norm, v_ffn1_w_gate_up, v_ffn1_w_down, v_mix_norm, v_mem_norm, v_w_in, v_b_forget, v_pool_w, v_pool_scale, v_w_pool_up, v_fox_q_norm, v_fox_k_norm, v_w_fox_o, v_w_mem_kv, v_mem_q_norm, v_mem_k_norm, v_w_mem_o, v_w_out, v_ffn2_norm, v_ffn2_w_gate_up, v_ffn2_w_down):
    P = dict(locals())
    big = ["ffn1_w_gate_up", "ffn1_w_down", "w_in", "w_pool_up", "w_fox_o", "w_mem_kv", "w_mem_o", "w_out",
           "ffn2_w_gate_up", "ffn2_w_down"]
    kinds = ["col", "row", "col", "col", "col", "row", "col", "row", "col", "row"]
    small = ["ffn1_norm", "mix_norm", "mem_norm", "b_forget", "pool_w", "pool_scale", "fox_q_norm", "fox_k_norm",
             "mem_q_norm", "mem_k_norm", "ffn2_norm"]
    order = ["ffn1_norm", "ffn1_w_gate_up", "ffn1_w_down", "mix_norm", "mem_norm", "w_in", "b_forget", "pool_w",
             "pool_scale", "w_pool_up", "fox_q_norm", "fox_k_norm", "w_fox_o", "w_mem_kv", "mem_q_norm", "mem_k_norm",
             "w_mem_o", "w_out", "ffn2_norm", "ffn2_w_gate_up", "ffn2_w_down"]

    nb, S, D = x.shape
    T = nb * S
    ML = mem.shape[1]
    NF, HD = b_forget.shape[-1], fox_q_norm.shape[-1]
    FW = NF * HD
    MW, MHD = w_mem_o.shape[1], mem_q_norm.shape[-1]
    PG, PD = pool_w.shape[1], pool_w.shape[2]
    PW = PG * PD
    n_in = w_in.shape[-1]
    n_in_pad = _round_up(n_in, LANES)
    in_w = 4 * n_in
    assert HD * 2 == LANES and MHD == LANES and PD == LANES and in_w == PW + 3 * FW + NF + MW + 3 * D

    shards = []
    for nme in big:
        wl = P[nme][0].astype(BF16)
        if nme == "w_in":
            wl = jnp.pad(wl, ((0, 0), (0, n_in_pad - n_in)))
        shards.append(wl)
    full = dict(zip(big, _all_gather_weights(shards, kinds)))
    win_o = jnp.concatenate([full["w_in"][:, j * n_in_pad:j * n_in_pad + n_in] for j in range(4)], axis=1)
    o_q, o_k, o_v, o_f = PW, PW + FW, PW + 2 * FW, PW + 3 * FW
    o_qm, o_gate = o_f + NF, o_f + NF + MW
    p_q, p_k, p_v, p_pool, p_qm, p_f = 3 * D, 3 * D + FW, 3 * D + 2 * FW, 3 * D + 3 * FW, 3 * D + 3 * FW + PW, 3 * D + 3 * FW + PW + MW
    inp = _round_up(p_f + LANES, 512)
    win_p = jnp.concatenate([win_o[:, o_gate:], win_o[:, o_q:o_f], win_o[:, :PW], win_o[:, o_qm:o_gate],
                             win_o[:, o_f:o_qm], jnp.zeros((D, inp - p_f - NF), BF16)], axis=1)

    x0 = x.reshape(T, D)
    x1, sv1 = _ffn_fwd(x0, ffn1_norm, full["ffn1_w_gate_up"], full["ffn1_w_down"], "ffn1")
    hb2 = _rms_fwd(x1, mix_norm, name="mix_norm")
    z = _mm(hb2, win_p, name="in_proj", tm=1024, tn=512)

    vpool = _pool_fwd(z, p_pool // LANES, pool_w[0], pool_scale, nb, S, name="pool_fwd")
    y_pool = _mm(vpool, full["w_pool_up"], name="pool_up", tm=1024, tn=512)

    b128 = jnp.pad(b_forget, ((0, 0), (0, LANES - NF)))
    c3 = _fox_c(z, p_f // LANES, b128, nb, S, NF, name="fox_c")
    qa, ka, vb = _fox_qk(z, p_q // LANES, p_k // LANES, p_v // LANES, c3, fox_q_norm, fox_k_norm, HD, NF,
                         name="fox_qk")
    o_fox = _fox_fwd(qa, ka, vb, nb, S, HD, name="fox_fwd")
    y_fox = _mm(o_fox, full["w_fox_o"], name="fox_out", tm=1024, tn=512)

    mem2 = mem.reshape(nb * ML, D)
    memn = _rms_fwd(mem2, mem_norm, name="mem_norm")
    kv = _mm(memn, full["w_mem_kv"], name="mem_kv")
    kmn = _headnorm_fwd(kv, 0, MW, mem_k_norm, MHD, name="mem_knorm")
    vmb = kv[:, MW:].astype(BF16)
    qmn = _headnorm_fwd(z, p_qm // LANES, MW, mem_q_norm, MHD, name="mem_qnorm")
    o_mem = _mem_fwd(qmn, kmn, vmb, nb, S, ML, name="mem_fwd")
    y_mem = _mm(o_mem, full["w_mem_o"], name="mem_out", tm=1024, tn=512)

    merged = _merge_fwd(z, y_pool, y_fox, y_mem, name="merge_fwd")
    x2 = _mm(merged, full["w_out"], res=x1, name="out_proj", tm=1024, tn=512)
    x3, sv2 = _ffn_fwd(x2, ffn2_norm, full["ffn2_w_gate_up"], full["ffn2_w_down"], "ffn2")
    dy3, dy3b, loss_part = _loss(x3, loss_target.reshape(T, D), name="loss")

    g = {}
    dx2, dx2b, g["ffn2_w_gate_up"], g["ffn2_w_down"], g["ffn2_norm"] = _ffn_bwd(
        dy3, dy3b, x2, ffn2_norm, full["ffn2_w_gate_up"], full["ffn2_w_down"], sv2, "ffn2")

    dmerged = _mm(dx2b, full["w_out"], mode="nt", name="d_merged", tm=1024, tn=512)
    g["w_out"] = _mm(merged, dx2b, mode="tn", out_dtype=BF16, name="d_w_out", tk=4096)
    dyp, dyf, dym, dgl = _merge_bwd(dmerged, z, y_pool, y_fox, y_mem, name="merge_bwd")

    g["w_pool_up"] = _mm(vpool, dyp, mode="tn", out_dtype=BF16, name="d_w_pool_up", tk=4096)
    dvp = _mm(dyp, full["w_pool_up"], mode="nt", name="d_vpool", tm=1024)
    du_pool, g["pool_w"], g["pool_scale"] = _pool_bwd(z, p_pool // LANES, dvp, pool_w[0], pool_scale, nb, S,
                                                      name="pool_bwd")

    g["w_fox_o"] = _mm(o_fox, dyf, mode="tn", out_dtype=BF16, name="d_w_fox_o", tk=4096)
    do_fox = _mm(dyf, full["w_fox_o"], mode="nt", out_dtype=BF16, name="d_o_fox", tm=1024)
    dqa, dka, dvf, dcs = _fox_bwd(qa, ka, vb, do_fox, nb, S, HD, name="fox_bwd")
    dq, dgq = _headnorm_bwd(z, p_q // LANES, FW, dqa, fox_q_norm, HD, HD ** -0.5, True, BF16, name="fox_dq")
    dk, dgk = _headnorm_bwd(z, p_k // LANES, FW, dka, fox_k_norm, HD, 1.0, True, BF16, name="fox_dk")
    g["fox_q_norm"], g["fox_k_norm"] = _fold_gain(dgq, HD), _fold_gain(dgk, HD)
    dc = -dcs[:, :, :LANES // HD, :].reshape(nb, NF, S).transpose(0, 2, 1).reshape(T, NF)
    df, db = _fox_c_bwd(jnp.pad(dc, ((0, 0), (0, LANES - NF))), z, p_f // LANES, b128, nb, S, name="fox_c_bwd")
    g["b_forget"] = db[:, :NF]

    g["w_mem_o"] = _mm(o_mem, dym, mode="tn", out_dtype=BF16, name="d_w_mem_o", tk=4096)
    do_mem = _mm(dym, full["w_mem_o"], mode="nt", out_dtype=BF16, name="d_o_mem", tm=1024)
    dqmn, dkmn, dvm = _mem_bwd(qmn, kmn, vmb, do_mem, nb, S, ML, name="mem_bwd")
    dqm, g["mem_q_norm"] = _headnorm_bwd(z, p_qm // LANES, MW, dqmn, mem_q_norm, MHD, 1.0, False, BF16, name="mem_dq")
    dkm, g["mem_k_norm"] = _headnorm_bwd(kv, 0, MW, dkmn, mem_k_norm, MHD, 1.0, False, BF16, name="mem_dk")
    dkv = jnp.concatenate([dkm, dvm.astype(BF16)], axis=1)
    g["w_mem_kv"] = _mm(memn, dkv, mode="tn", out_dtype=BF16, name="d_w_mem_kv")
    dmemn = _mm(dkv, full["w_mem_kv"], mode="nt", name="d_memn")
    _, _, g["mem_norm"] = _rms_bwd(mem2, dmemn, mem_norm, jnp.zeros_like(mem2), name="mem_dnorm")

    dz = jnp.concatenate([dgl, dq, dk, dvf.astype(BF16), du_pool, dqm, df.astype(BF16),
                          jnp.zeros((T, inp - p_f - LANES), BF16)], axis=1)
    dwin_p = _mm(hb2, dz, mode="tn", out_dtype=BF16, name="d_w_in", tk=4096)
    dh2 = _mm(dz, win_p, mode="nt", out_dtype=BF16, name="d_h2", tm=1024, tn=512, tk=512)
    dx1, dx1b, g["mix_norm"] = _rms_bwd(x1, dh2, mix_norm, dx2, name="mix_dnorm")

    dx0, _, g["ffn1_w_gate_up"], g["ffn1_w_down"], g["ffn1_norm"] = _ffn_bwd(
        dx1, dx1b, x0, ffn1_norm, full["ffn1_w_gate_up"], full["ffn1_w_down"], sv1, "ffn1")

    dwin_o = jnp.concatenate([dwin_p[:, p_pool:p_qm], dwin_p[:, p_q:p_pool], dwin_p[:, p_f:p_f + NF],
                              dwin_p[:, p_qm:p_f], dwin_p[:, :p_q]], axis=1)
    zpad = jnp.zeros((D, n_in_pad - n_in), BF16)
    g["w_in"] = jnp.concatenate(sum([[dwin_o[:, j * n_in:(j + 1) * n_in], zpad] for j in range(4)], []), axis=1)

    own, got = _rs_pair([g[nme] for nme in big], kinds)
    parts = [_addn([a, b], BF16, name="rs_pair_sum_" + nme) for nme, a, b in zip(big, own, got)]
    mine, theirs = _rs_scatter(parts, kinds)
    pieces = [_addn([a, b[0], b[1], b[2]], F32, name="rs_sum_" + nme) for nme, a, b in zip(big, mine, theirs)]
    gfull = dict(zip(big, _rs_join(pieces)))
    gfull["w_in"] = gfull["w_in"][:, :n_in]

    gpack, sizes = _pack_small([g[nme].reshape(P[nme].shape) for nme in small] + [loss_part[:, :1]])
    gsum = _unpack_small(_all_reduce_small(gpack), sizes)
    loss = gsum[-1].reshape(())
    for nme, a in zip(small, gsum[:-1]):
        gfull[nme] = a

    delta, new_m, new_v = {}, {}, {}
    for nme in big:
        shp = P[nme].shape
        w2 = P[nme].reshape(shp[-2:])
        delta[nme], new_m[nme], new_v[nme] = [
            a.reshape(shp) for a in _adamw(w2, gfull[nme], P["m_" + nme].reshape(w2.shape),
                                           P["v_" + nme].reshape(w2.shape), name="adamw_" + nme)]
        gfull[nme] = gfull[nme].reshape(shp)
    wpack, _ = _pack_small([P[nme] for nme in small])
    mpack, _ = _pack_small([P["m_" + nme] for nme in small])
    vpack, _ = _pack_small([P["v_" + nme] for nme in small])
    gpack2, ssz = _pack_small([gfull[nme] for nme in small])
    for dct, pk in zip((delta, new_m, new_v), _adamw(wpack, gpack2, mpack, vpack, name="adamw_small")):
        for nme, a in zip(small, _unpack_small(pk, ssz)):
            dct[nme] = a

    grad_x = dx0.reshape(nb, S, D)
    return (loss, grad_x, *[gfull[nme] for nme in order], *[delta[nme] for nme in order],
            *[new_m[nme] for nme in order], *[new_v[nme] for nme in order])
```

```python
import functools

import jax
import jax.numpy as jnp
from jax import lax
from jax.experimental import pallas as pl
from jax.experimental.pallas import tpu as pltpu

F32 = jnp.float32
BF16 = jnp.bfloat16
MESH = pl.DeviceIdType.MESH
EPS = 1e-6
POOL_WINDOWS = (2, 4, 8, 16)
ADAM_LR, ADAM_B1, ADAM_B2, ADAM_EPS, ADAM_WD, ADAM_STEP = 0.001, 0.9, 0.999, 1e-08, 0.01, 10
LANES = 128
VMEM_LIMIT = 56 * 1024 * 1024
NEG = -1e30


def _pc(body, *, name, **kw):
    return pl.pallas_call(body, name=name, **kw)


def _cp(*sem):
    return pltpu.CompilerParams(dimension_semantics=sem, vmem_limit_bytes=VMEM_LIMIT)


def _tile(n, pref, mult=LANES):
    if n <= pref:
        return n
    t = (pref // mult) * mult
    while t >= mult:
        if n % t == 0:
            return t
        t -= mult
    return n


def _round_up(n, m):
    return (n + m - 1) // m * m


_DIMS = {"nn": (((1,), (0,)), ((), ())), "nt": (((1,), (1,)), ((), ())), "tn": (((0,), (0,)), ((), ()))}


def _mm(a, b, *, name, mode="nn", out_dtype=F32, scale=1.0, res=None, tm=512, tn=512, tk=2048):
    if mode == "nn":
        (M, K), (K2, N) = a.shape, b.shape
    elif mode == "nt":
        (M, K), (N, K2) = a.shape, b.shape
    else:
        (K, M), (K2, N) = a.shape, b.shape
    assert K == K2, (name, a.shape, b.shape)
    tm, tn, tk = _tile(M, tm, 8 if M % LANES else LANES), _tile(N, tn), _tile(K, tk)
    nk = K // tk
    dims = _DIMS[mode]
    a_spec = (pl.BlockSpec((tk, tm), lambda i, j, k: (k, i)) if mode == "tn"
              else pl.BlockSpec((tm, tk), lambda i, j, k: (i, k)))
    b_spec = (pl.BlockSpec((tn, tk), lambda i, j, k: (j, k)) if mode == "nt"
              else pl.BlockSpec((tk, tn), lambda i, j, k: (k, j)))
    o_spec = pl.BlockSpec((tm, tn), lambda i, j, k: (i, j))
    has_res = res is not None

    def body(*refs):
        if has_res:
            a_ref, b_ref, r_ref, o_ref, acc = refs
        else:
            a_ref, b_ref, o_ref, acc = refs
        k = pl.program_id(2)

        @pl.when(k == 0)
        def _():
            acc[...] = jnp.zeros_like(acc)

        acc[...] += lax.dot_general(a_ref[...], b_ref[...], dims, preferred_element_type=F32)

        @pl.when(k == nk - 1)
        def _():
            v = acc[...] * scale
            if has_res:
                v = r_ref[...] + v
            o_ref[...] = v.astype(o_ref.dtype)

    return _pc(body, name=name, out_shape=jax.ShapeDtypeStruct((M, N), out_dtype),
               grid=(M // tm, N // tn, nk),
               in_specs=[a_spec, b_spec] + ([o_spec] if has_res else []), out_specs=o_spec,
               scratch_shapes=[pltpu.VMEM((tm, tn), F32)],
               compiler_params=_cp("parallel", "parallel", "arbitrary"))(*((a, b, res) if has_res else (a, b)))


def _gateup(h, wgu, *, name, tm=512, tn=512):
    M, K = h.shape
    Fh = wgu.shape[1] // 2
    tm, tn = _tile(M, tm), _tile(Fh, tn)
    nj = Fh // tn

    def body(h_ref, wg_ref, wu_ref, g_ref, u_ref, a_ref):
        hv = h_ref[...]
        g = jnp.dot(hv, wg_ref[...], preferred_element_type=F32)
        u = jnp.dot(hv, wu_ref[...], preferred_element_type=F32)
        g_ref[...] = g.astype(BF16)
        u_ref[...] = u.astype(BF16)
        a_ref[...] = (g * jax.nn.sigmoid(g) * u).astype(BF16)

    o_spec = pl.BlockSpec((tm, tn), lambda i, j: (i, j))
    shp = jax.ShapeDtypeStruct((M, Fh), BF16)
    return _pc(body, name=name, out_shape=(shp, shp, shp), grid=(M // tm, nj),
               in_specs=[pl.BlockSpec((tm, K), lambda i, j: (i, 0)),
                         pl.BlockSpec((K, tn), lambda i, j: (0, j)),
                         pl.BlockSpec((K, tn), lambda i, j: (0, j + nj))],
               out_specs=(o_spec, o_spec, o_spec), compiler_params=_cp("parallel", "parallel"))(h, wgu, wgu)


def _rowmap(fn, ins, outs, *, rows, tm, name, accs=()):
    tm = _tile(rows, tm, 8)
    arrs, in_specs = [], []
    for d in ins:
        if d[0] == "row":
            _, a, w, cb = d
            w = a.shape[1] if w is None else w
            in_specs.append(pl.BlockSpec((tm, w), functools.partial(lambda i, cb: (i, cb), cb=cb)))
        else:
            a = d[1]
            in_specs.append(pl.BlockSpec(a.shape, functools.partial(lambda i, nd: (0,) * nd, nd=a.ndim)))
        arrs.append(a)
    out_shape = [jax.ShapeDtypeStruct((rows, w), dt) for w, dt in outs]
    out_specs = [pl.BlockSpec((tm, w), lambda i: (i, 0)) for w, _ in outs]
    for shp, dt in accs:
        out_shape.append(jax.ShapeDtypeStruct(shp, dt))
        out_specs.append(pl.BlockSpec(shp, functools.partial(lambda i, nd: (0,) * nd, nd=len(shp))))
    nin, nout, nacc = len(ins), len(outs), len(accs)

    def body(*refs):
        res = fn(*[r[...] for r in refs[:nin]])
        res = res if isinstance(res, (tuple, list)) else (res,)
        for r, v in zip(refs[nin:nin + nout], res[:nout]):
            r[...] = v.astype(r.dtype)
        if nacc:
            acc_refs = refs[nin + nout:]

            @pl.when(pl.program_id(0) == 0)
            def _():
                for r in acc_refs:
                    r[...] = jnp.zeros_like(r)

            for r, v in zip(acc_refs, res[nout:]):
                r[...] += v

    out = _pc(body, name=name, out_shape=tuple(out_shape), grid=(rows // tm,), in_specs=in_specs,
              out_specs=tuple(out_specs), compiler_params=_cp("arbitrary" if nacc else "parallel"))(*arrs)
    return out[0] if len(out) == 1 else out


def _rms_fwd(x, g, *, name):
    def fn(xv, gv):
        return xv * lax.rsqrt(jnp.mean(xv * xv, axis=-1, keepdims=True) + EPS) * gv
    return _rowmap(fn, [("row", x, None, 0), ("bc", g)], [(x.shape[1], BF16)], rows=x.shape[0], tm=256, name=name)


def _rms_bwd(x, dh, g, dres, *, name):
    D = x.shape[1]

    def fn(xv, dhv, gv, drv):
        r = lax.rsqrt(jnp.mean(xv * xv, axis=-1, keepdims=True) + EPS)
        xh = xv * r
        dhf = dhv.astype(F32)
        dxn = dhf * gv
        dx = drv + r * (dxn - xh * jnp.mean(dxn * xh, axis=-1, keepdims=True))
        return dx, dx, jnp.sum(dhf * xh, axis=0, keepdims=True)

    return _rowmap(fn, [("row", x, None, 0), ("row", dh, None, 0), ("bc", g), ("row", dres, None, 0)],
                   [(D, F32), (D, BF16)], rows=x.shape[0], tm=256, name=name, accs=[((1, D), F32)])


def _swiglu_bwd(da, gate, up, *, name):
    Fh = gate.shape[1]

    def fn(dav, gv, uv):
        d, g, u = dav.astype(F32), gv.astype(F32), uv.astype(F32)
        s = jax.nn.sigmoid(g)
        return jnp.concatenate([d * u * (s * (1.0 + g * (1.0 - s))), d * (g * s)], axis=-1)

    return _rowmap(fn, [("row", da, None, 0), ("row", gate, None, 0), ("row", up, None, 0)],
                   [(2 * Fh, BF16)], rows=da.shape[0], tm=128, name=name)


def _loss(y, tgt, *, name):
    D = y.shape[1]

    def fn(yv, tv):
        e = yv - tv
        d = e * (1.0 / D)
        part = 0.5 * jnp.sum(jnp.sum(e * e, axis=-1, keepdims=True) * (1.0 / D), axis=0, keepdims=True)
        return d, d, jnp.broadcast_to(part, (1, LANES))

    return _rowmap(fn, [("row", y, None, 0), ("row", tgt, None, 0)], [(D, F32), (D, BF16)],
                   rows=y.shape[0], tm=256, name=name, accs=[((1, LANES), F32)])


def _merge_fwd(z, yp, yf, ym, *, name):
    D = yp.shape[1]

    def fn(gp, gf, gm, a, b, c):
        return jax.nn.sigmoid(gp) * a + jax.nn.sigmoid(gf) * b + jax.nn.sigmoid(gm) * c

    return _rowmap(fn, [("row", z, D, 0), ("row", z, D, 1), ("row", z, D, 2),
                        ("row", yp, None, 0), ("row", yf, None, 0), ("row", ym, None, 0)],
                   [(D, BF16)], rows=yp.shape[0], tm=128, name=name)


def _merge_bwd(dm, z, yp, yf, ym, *, name):
    D = yp.shape[1]

    def fn(d, gp, gf, gm, a, b, c):
        outs, dgl = [], []
        for gl, yv in ((gp, a), (gf, b), (gm, c)):
            s = jax.nn.sigmoid(gl)
            outs.append(d * s)
            dgl.append(d * yv * (s * (1.0 - s)))
        return (*outs, jnp.concatenate(dgl, axis=-1))

    return _rowmap(fn, [("row", dm, None, 0), ("row", z, D, 0), ("row", z, D, 1), ("row", z, D, 2),
                        ("row", yp, None, 0), ("row", yf, None, 0), ("row", ym, None, 0)],
                   [(D, BF16), (D, BF16), (D, BF16), (3 * D, BF16)], rows=yp.shape[0], tm=128, name=name)


def _adamw(w, g, m, v, *, name):
    C = w.shape[1]

    def fn(wv, gv, mv, vv):
        m2 = ADAM_B1 * mv + (1.0 - ADAM_B1) * gv
        v2 = ADAM_B2 * vv + (1.0 - ADAM_B2) * (gv * gv)
        m_hat = m2 / (1.0 - ADAM_B1 ** ADAM_STEP)
        v_hat = v2 / (1.0 - ADAM_B2 ** ADAM_STEP)
        return -ADAM_LR * (m_hat / (jnp.sqrt(v_hat) + ADAM_EPS) + ADAM_WD * wv), m2, v2

    tm = max(8, (262144 // C) // 8 * 8)
    return _rowmap(fn, [("row", a, None, 0) for a in (w, g, m, v)], [(C, F32)] * 3,
                   rows=w.shape[0], tm=tm, name=name)


def _seg_mean(v, hd):
    if hd == LANES:
        return jnp.mean(v, axis=-1, keepdims=True)
    lo = lax.broadcasted_iota(jnp.int32, v.shape, 1) < hd
    s0 = jnp.sum(jnp.where(lo, v, 0.0), axis=-1, keepdims=True)
    s1 = jnp.sum(jnp.where(lo, 0.0, v), axis=-1, keepdims=True)
    return jnp.where(lo, s0, s1) * (1.0 / hd)


def _gain128(g, hd):
    g = g.reshape(1, hd).astype(F32)
    return g if hd == LANES else jnp.concatenate([g, g], axis=-1)


def _headnorm_fwd(x, cb0, width, gain, hd, *, name):
    rows = x.shape[0]
    tm = _tile(rows, 512, 8)
    g128 = _gain128(gain, hd)

    def body(x_ref, g_ref, o_ref):
        xv = x_ref[...]
        o_ref[...] = (xv * lax.rsqrt(_seg_mean(xv * xv, hd) + EPS) * g_ref[...]).astype(BF16)

    return _pc(body, name=name, out_shape=jax.ShapeDtypeStruct((rows, width), BF16),
               grid=(rows // tm, width // LANES),
               in_specs=[pl.BlockSpec((tm, LANES), lambda i, j: (i, cb0 + j)),
                         pl.BlockSpec((1, LANES), lambda i, j: (0, 0))],
               out_specs=pl.BlockSpec((tm, LANES), lambda i, j: (i, j)),
               compiler_params=_cp("parallel", "parallel"))(x, g128)


def _headnorm_bwd(x, cb0, width, dy, gain, hd, post, spread, out_dtype, *, name):
    rows = x.shape[0]
    tm = _tile(rows, 512, 8)
    g128 = _gain128(gain, hd)
    dyw = 2 * LANES if spread else LANES

    def body(x_ref, dy_ref, g_ref, dx_ref, dg_ref):
        xv = x_ref[...]
        dyv = dy_ref[...]
        if spread:
            dyv = jnp.concatenate([dyv[:, :hd], dyv[:, LANES:LANES + hd]], axis=-1)
        dyv = dyv * post
        r = lax.rsqrt(_seg_mean(xv * xv, hd) + EPS)
        xh = xv * r
        dxn = dyv * g_ref[...]
        dx_ref[...] = (r * (dxn - xh * _seg_mean(dxn * xh, hd))).astype(dx_ref.dtype)

        @pl.when((pl.program_id(0) == 0) & (pl.program_id(1) == 0))
        def _():
            dg_ref[...] = jnp.zeros_like(dg_ref)

        dg_ref[...] += jnp.sum(dyv * xh, axis=0, keepdims=True)

    return _pc(body, name=name,
               out_shape=(jax.ShapeDtypeStruct((rows, width), out_dtype), jax.ShapeDtypeStruct((1, LANES), F32)),
               grid=(rows // tm, width // LANES),
               in_specs=[pl.BlockSpec((tm, LANES), lambda i, j: (i, cb0 + j)),
                         pl.BlockSpec((tm, dyw), lambda i, j: (i, j)),
                         pl.BlockSpec((1, LANES), lambda i, j: (0, 0))],
               out_specs=(pl.BlockSpec((tm, LANES), lambda i, j: (i, j)),
                          pl.BlockSpec((1, LANES), lambda i, j: (0, 0))),
               compiler_params=_cp("arbitrary", "arbitrary"))(x, dy, g128)


def _fold_gain(dg, hd):
    return dg if hd == LANES else dg[:, :hd] + dg[:, hd:]


def _shift_down(v, k, row):
    return jnp.where(row >= k, pltpu.roll(v, k, 0), 0.0)


def _shift_up(v, k, row, S):
    return jnp.where(row < S - k, pltpu.roll(v, S - k, 0), 0.0)


def _window_sum(v, g, row, shift):
    s, outs = v, []
    for k in (1, 2, 4, 8):
        s = s + shift(s, k)
        outs.append(s)
    return jnp.where(g == 0, outs[0], jnp.where(g == 1, outs[1], jnp.where(g == 2, outs[2], outs[3])))


def _pool_fwd(z, cb0, pool_w, pool_scale, nb, S, *, name):
    PG, PD = pool_w.shape[0], pool_w.shape[1]
    assert PD == LANES and PG <= len(POOL_WINDOWS)
    pw = pool_w.astype(BF16)

    def body(u_ref, w_ref, s_ref, o_ref):
        g = pl.program_id(0)
        u = u_ref[...]
        row = lax.broadcasted_iota(jnp.int32, u.shape, 0)
        cnt = jnp.minimum(row + 1, jnp.left_shift(2, g)).astype(F32)
        pooled = _window_sum(u, g, row, lambda v, k: _shift_down(v, k, row)) / cnt
        mixed = jnp.dot((pooled - u).astype(BF16), w_ref[...], preferred_element_type=F32)
        o_ref[...] = (mixed * s_ref[...]).astype(BF16)

    return _pc(body, name=name, out_shape=jax.ShapeDtypeStruct((nb * S, PG * PD), BF16), grid=(PG, nb),
               in_specs=[pl.BlockSpec((S, PD), lambda g, b: (b, cb0 + g)),
                         pl.BlockSpec((None, PD, PD), lambda g, b: (g, 0, 0)),
                         pl.BlockSpec((1, PD), lambda g, b: (0, g))],
               out_specs=pl.BlockSpec((S, PD), lambda g, b: (b, g)),
               compiler_params=_cp("parallel", "parallel"))(z, pw, pool_scale)


def _pool_bwd(z, cb0, dv, pool_w, pool_scale, nb, S, *, name):
    PG, PD = pool_w.shape[0], pool_w.shape[1]
    pw = pool_w.astype(BF16)

    def body(u_ref, dv_ref, w_ref, s_ref, du_ref, dw_ref, ds_ref):
        g, b = pl.program_id(0), pl.program_id(1)
        u, dvv, w = u_ref[...], dv_ref[...], w_ref[...]
        row = lax.broadcasted_iota(jnp.int32, u.shape, 0)
        cnt = jnp.minimum(row + 1, jnp.left_shift(2, g)).astype(F32)
        diff = (_window_sum(u, g, row, lambda v, k: _shift_down(v, k, row)) / cnt - u).astype(BF16)
        mixed = jnp.dot(diff, w, preferred_element_type=F32)
        dmixed = (dvv * s_ref[...]).astype(BF16)
        ddiff = lax.dot_general(dmixed, w, _DIMS["nt"], preferred_element_type=F32)
        du_ref[...] = (_window_sum(ddiff / cnt, g, row, lambda v, k: _shift_up(v, k, row, S)) - ddiff).astype(BF16)

        @pl.when(b == 0)
        def _():
            dw_ref[...] = jnp.zeros_like(dw_ref)
            ds_ref[...] = jnp.zeros_like(ds_ref)

        dw_ref[...] += lax.dot_general(diff, dmixed, _DIMS["tn"], preferred_element_type=F32)
        ds_ref[...] += jnp.sum(dvv * mixed, axis=0, keepdims=True)

    return _pc(body, name=name,
               out_shape=(jax.ShapeDtypeStruct((nb * S, PG * PD), BF16), jax.ShapeDtypeStruct((PG, PD, PD), F32),
                          jax.ShapeDtypeStruct((1, PG * PD), F32)),
               grid=(PG, nb),
               in_specs=[pl.BlockSpec((S, PD), lambda g, b: (b, cb0 + g)),
                         pl.BlockSpec((S, PD), lambda g, b: (b, g)),
                         pl.BlockSpec((None, PD, PD), lambda g, b: (g, 0, 0)),
                         pl.BlockSpec((1, PD), lambda g, b: (0, g))],
               out_specs=(pl.BlockSpec((S, PD), lambda g, b: (b, g)),
                          pl.BlockSpec((None, PD, PD), lambda g, b: (g, 0, 0)),
                          pl.BlockSpec((1, PD), lambda g, b: (0, g))),
               compiler_params=_cp("parallel", "arbitrary"))(z, dv, pw, pool_scale)


def _split3(c):
    hi = c.astype(BF16)
    r1 = c - hi.astype(F32)
    mid = r1.astype(BF16)
    lo = (r1 - mid.astype(F32)).astype(BF16)
    return hi, mid, lo


def _log_sigmoid(v):
    return jnp.minimum(v, 0.0) - jnp.log(1.0 + jnp.exp(-jnp.abs(v)))


def _fox_c(z, cb, b128, nb, S, NF, *, name):
    def body(f_ref, b_ref, hi_ref, mid_ref, lo_ref):
        v = f_ref[...] + b_ref[...]
        lane = lax.broadcasted_iota(jnp.int32, v.shape, 1)
        row = lax.broadcasted_iota(jnp.int32, v.shape, 0)
        c = jnp.where(lane < NF, _log_sigmoid(v), 0.0)
        k = 1
        while k < S:
            c = c + _shift_down(c, k, row)
            k *= 2
        hi, mid, lo = _split3(c)
        hi_ref[...], mid_ref[...], lo_ref[...] = hi, mid, lo

    shp = jax.ShapeDtypeStruct((nb * S, LANES), BF16)
    spec = pl.BlockSpec((S, LANES), lambda i: (i, 0))
    return _pc(body, name=name, out_shape=(shp, shp, shp), grid=(nb,),
               in_specs=[pl.BlockSpec((S, LANES), lambda i: (i, cb)), pl.BlockSpec((1, LANES), lambda i: (0, 0))],
               out_specs=(spec, spec, spec), compiler_params=_cp("parallel"))(z, b128)


def _fox_c_bwd(dc, z, cb, b128, nb, S, *, name):
    def body(dc_ref, f_ref, b_ref, df_ref, db_ref):
        d = dc_ref[...]
        row = lax.broadcasted_iota(jnp.int32, d.shape, 0)
        k = 1
        while k < S:
            d = d + _shift_up(d, k, row, S)
            k *= 2
        df = d * jax.nn.sigmoid(-(f_ref[...] + b_ref[...]))
        df_ref[...] = df

        @pl.when(pl.program_id(0) == 0)
        def _():
            db_ref[...] = jnp.zeros_like(db_ref)

        db_ref[...] += jnp.sum(df, axis=0, keepdims=True)

    return _pc(body, name=name,
               out_shape=(jax.ShapeDtypeStruct((nb * S, LANES), F32), jax.ShapeDtypeStruct((1, LANES), F32)),
               grid=(nb,),
               in_specs=[pl.BlockSpec((S, LANES), lambda i: (i, 0)), pl.BlockSpec((S, LANES), lambda i: (i, cb)),
                         pl.BlockSpec((1, LANES), lambda i: (0, 0))],
               out_specs=(pl.BlockSpec((S, LANES), lambda i: (i, 0)), pl.BlockSpec((1, LANES), lambda i: (0, 0))),
               compiler_params=_cp("arbitrary"))(dc, z, b128)


def _fox_qk(z, cbq, cbk, cbv, c3, gq, gk, HD, NF, *, name):
    T = z.shape[0]
    HP = NF * HD // LANES
    tm = _tile(T, 512, 8)
    scale = HD ** -0.5
    gq128, gk128 = _gain128(gq, HD), _gain128(gk, HD)

    def body(q_ref, k_ref, v_ref, hi_ref, mid_ref, lo_ref, gq_ref, gk_ref, qa_ref, ka_ref, vb_ref):
        hp = pl.program_id(1)
        q, k = q_ref[...], k_ref[...]
        qn = q * lax.rsqrt(_seg_mean(q * q, HD) + EPS) * (gq_ref[...] * scale)
        kn = k * lax.rsqrt(_seg_mean(k * k, HD) + EPS) * gk_ref[...]
        vb_ref[...] = v_ref[...].astype(BF16)
        c3v = jnp.concatenate([hi_ref[...], mid_ref[...], lo_ref[...]], axis=-1)
        r = lax.broadcasted_iota(jnp.int32, (3 * LANES, LANES - HD), 0)
        cc = lax.broadcasted_iota(jnp.int32, (3 * LANES, LANES - HD), 1)
        lane = lax.broadcasted_iota(jnp.int32, (tm, LANES - HD), 1)
        for hh in range(LANES // HD):
            head = hp * (LANES // HD) + hh
            sel_q = jnp.where((cc < 3) & (r == head + LANES * cc), 1.0, 0.0).astype(BF16)
            sel_k = jnp.where((cc >= 3) & (cc < 6) & (r == head + LANES * (cc - 3)), 1.0, 0.0).astype(BF16)
            qaug = jnp.dot(c3v, sel_q, preferred_element_type=F32) + jnp.where((lane >= 3) & (lane < 6), 1.0, 0.0)
            kaug = jnp.where(lane < 3, 1.0, 0.0) - jnp.dot(c3v, sel_k, preferred_element_type=F32)
            sl = slice(hh * HD, (hh + 1) * HD)
            qa_ref[:, hh * LANES:(hh + 1) * LANES] = jnp.concatenate([qn[:, sl], qaug], axis=-1).astype(BF16)
            ka_ref[:, hh * LANES:(hh + 1) * LANES] = jnp.concatenate([kn[:, sl], kaug], axis=-1).astype(BF16)

    aw = (LANES // HD) * LANES
    blk = lambda cb: pl.BlockSpec((tm, LANES), functools.partial(lambda i, j, cb: (i, cb + j), cb=cb))
    cspec = pl.BlockSpec((tm, LANES), lambda i, j: (i, 0))
    gspec = pl.BlockSpec((1, LANES), lambda i, j: (0, 0))
    return _pc(body, name=name,
               out_shape=(jax.ShapeDtypeStruct((T, HP * aw), BF16), jax.ShapeDtypeStruct((T, HP * aw), BF16),
                          jax.ShapeDtypeStruct((T, NF * HD), BF16)),
               grid=(T // tm, HP),
               in_specs=[blk(cbq), blk(cbk), blk(cbv), cspec, cspec, cspec, gspec, gspec],
               out_specs=(pl.BlockSpec((tm, aw), lambda i, j: (i, j)), pl.BlockSpec((tm, aw), lambda i, j: (i, j)),
                          pl.BlockSpec((tm, LANES), lambda i, j: (i, j))),
               compiler_params=_cp("parallel", "parallel"))(z, z, z, *c3, gq128, gk128)


def _fox_probs(qa, ka, q0):
    s = lax.dot_general(qa, ka, _DIMS["nt"], preferred_element_type=F32)
    row = lax.broadcasted_iota(jnp.int32, s.shape, 0) + q0
    col = lax.broadcasted_iota(jnp.int32, s.shape, 1)
    s = jnp.where(col <= row, s, NEG)
    e = jnp.exp(s - jnp.max(s, axis=-1, keepdims=True))
    return e / jnp.sum(e, axis=-1, keepdims=True)


def _fox_fwd(qa, ka, vb, nb, S, HD, *, name, tq=256):
    T = qa.shape[0]
    nh = LANES // HD
    aw = nh * LANES
    HP = qa.shape[1] // aw
    tq = _tile(S, tq, 8)
    nq = S // tq

    def body(qa_ref, ka_ref, v_ref, o_ref):
        q0 = pl.program_id(2) * tq
        outs = []
        v = v_ref[...]
        for hh in range(nh):
            p = _fox_probs(qa_ref[:, hh * LANES:(hh + 1) * LANES], ka_ref[:, hh * LANES:(hh + 1) * LANES], q0)
            outs.append(jnp.dot(p.astype(BF16), v[:, hh * HD:(hh + 1) * HD], preferred_element_type=F32))
        o_ref[...] = jnp.concatenate(outs, axis=-1).astype(BF16)

    return _pc(body, name=name, out_shape=jax.ShapeDtypeStruct((T, HP * LANES), BF16), grid=(nb, HP, nq),
               in_specs=[pl.BlockSpec((tq, aw), lambda b, h, i: (b * nq + i, h)),
                         pl.BlockSpec((S, aw), lambda b, h, i: (b, h)),
                         pl.BlockSpec((S, LANES), lambda b, h, i: (b, h))],
               out_specs=pl.BlockSpec((tq, LANES), lambda b, h, i: (b * nq + i, h)),
               compiler_params=_cp("parallel", "parallel", "arbitrary"))(qa, ka, vb)


def _fox_bwd(qa, ka, vb, do, nb, S, HD, *, name, tq=256):
    T = qa.shape[0]
    nh = LANES // HD
    aw = nh * LANES
    HP = qa.shape[1] // aw
    tq = _tile(S, tq, 8)
    nq = S // tq

    def body(qa_ref, ka_ref, v_ref, do_ref, dqa_ref, dka_ref, dv_ref, dcs_ref):
        i = pl.program_id(2)
        q0 = i * tq
        dkas, dvs, css = [], [], []
        v, dob = v_ref[...], do_ref[...]
        for hh in range(nh):
            al = slice(hh * LANES, (hh + 1) * LANES)
            hl = slice(hh * HD, (hh + 1) * HD)
            qav, kav, vv, dov = qa_ref[:, al], ka_ref[:, al], v[:, hl], dob[:, hl]
            p = _fox_probs(qav, kav, q0)
            dp = lax.dot_general(dov, vv, _DIMS["nt"], preferred_element_type=F32)
            dsf = p * (dp - jnp.sum(p * dp, axis=-1, keepdims=True))
            css.append(jnp.sum(dsf, axis=0, keepdims=True))
            ds = dsf.astype(BF16)
            dqa_ref[:, al] = jnp.dot(ds, kav, preferred_element_type=F32)
            dkas.append(lax.dot_general(ds, qav, _DIMS["tn"], preferred_element_type=F32))
            dvs.append(lax.dot_general(p.astype(BF16), dov, _DIMS["tn"], preferred_element_type=F32))

        @pl.when(i == 0)
        def _():
            dka_ref[...] = jnp.zeros_like(dka_ref)
            dv_ref[...] = jnp.zeros_like(dv_ref)
            dcs_ref[...] = jnp.zeros_like(dcs_ref)

        dka_ref[...] += jnp.concatenate(dkas, axis=-1)
        dv_ref[...] += jnp.concatenate(dvs, axis=-1)
        dcs_ref[...] += jnp.concatenate(css + [jnp.zeros((8 - nh, S), F32)], axis=0)

    return _pc(body, name=name,
               out_shape=(jax.ShapeDtypeStruct((T, HP * aw), F32), jax.ShapeDtypeStruct((T, HP * aw), F32),
                          jax.ShapeDtypeStruct((T, HP * LANES), F32), jax.ShapeDtypeStruct((nb, HP, 8, S), F32)),
               grid=(nb, HP, nq),
               in_specs=[pl.BlockSpec((tq, aw), lambda b, h, i: (b * nq + i, h)),
                         pl.BlockSpec((S, aw), lambda b, h, i: (b, h)),
                         pl.BlockSpec((S, LANES), lambda b, h, i: (b, h)),
                         pl.BlockSpec((tq, LANES), lambda b, h, i: (b * nq + i, h))],
               out_specs=(pl.BlockSpec((tq, aw), lambda b, h, i: (b * nq + i, h)),
                          pl.BlockSpec((S, aw), lambda b, h, i: (b, h)),
                          pl.BlockSpec((S, LANES), lambda b, h, i: (b, h)),
                          pl.BlockSpec((None, None, 8, S), lambda b, h, i: (b, h, 0, 0))),
               compiler_params=_cp("parallel", "parallel", "arbitrary"))(qa, ka, vb, do)


def _mem_probs(q, k, scale):
    s = lax.dot_general(q, k, _DIMS["nt"], preferred_element_type=F32) * scale
    e = jnp.exp(s - jnp.max(s, axis=-1, keepdims=True))
    return e / jnp.sum(e, axis=-1, keepdims=True)


def _mem_fwd(q, k, v, nb, S, ML, *, name, tq=512):
    T, MW = q.shape
    nh = MW // LANES
    tq = _tile(S, tq, 8)
    nq = S // tq
    scale = LANES ** -0.5

    def body(q_ref, k_ref, v_ref, o_ref):
        for h in range(nh):
            sl = slice(h * LANES, (h + 1) * LANES)
            p = _mem_probs(q_ref[:, sl], k_ref[:, sl], scale)
            o_ref[:, sl] = jnp.dot(p.astype(BF16), v_ref[:, sl], preferred_element_type=F32).astype(BF16)

    return _pc(body, name=name, out_shape=jax.ShapeDtypeStruct((T, MW), BF16), grid=(nb, nq),
               in_specs=[pl.BlockSpec((tq, MW), lambda b, i: (b * nq + i, 0)),
                         pl.BlockSpec((ML, MW), lambda b, i: (b, 0)), pl.BlockSpec((ML, MW), lambda b, i: (b, 0))],
               out_specs=pl.BlockSpec((tq, MW), lambda b, i: (b * nq + i, 0)),
               compiler_params=_cp("parallel", "arbitrary"))(q, k, v)


def _mem_bwd(q, k, v, do, nb, S, ML, *, name, tq=512):
    T, MW = q.shape
    nh = MW // LANES
    tq = _tile(S, tq, 8)
    nq = S // tq
    scale = LANES ** -0.5

    def body(q_ref, k_ref, v_ref, do_ref, dq_ref, dk_ref, dv_ref):
        i = pl.program_id(1)

        @pl.when(i == 0)
        def _():
            dk_ref[...] = jnp.zeros_like(dk_ref)
            dv_ref[...] = jnp.zeros_like(dv_ref)

        for h in range(nh):
            sl = slice(h * LANES, (h + 1) * LANES)
            qv, kv, vv, dov = q_ref[:, sl], k_ref[:, sl], v_ref[:, sl], do_ref[:, sl]
            p = _mem_probs(qv, kv, scale)
            dp = lax.dot_general(dov, vv, _DIMS["nt"], preferred_element_type=F32)
            ds = (p * (dp - jnp.sum(p * dp, axis=-1, keepdims=True)) * scale).astype(BF16)
            dq_ref[:, sl] = jnp.dot(ds, kv, preferred_element_type=F32)
            dk_ref[:, sl] += lax.dot_general(ds, qv, _DIMS["tn"], preferred_element_type=F32)
            dv_ref[:, sl] += lax.dot_general(p.astype(BF16), dov, _DIMS["tn"], preferred_element_type=F32)

    kvspec = pl.BlockSpec((ML, MW), lambda b, i: (b, 0))
    qspec = pl.BlockSpec((tq, MW), lambda b, i: (b * nq + i, 0))
    return _pc(body, name=name,
               out_shape=(jax.ShapeDtypeStruct((T, MW), F32), jax.ShapeDtypeStruct((nb * ML, MW), F32),
                          jax.ShapeDtypeStruct((nb * ML, MW), F32)),
               grid=(nb, nq), in_specs=[qspec, kvspec, kvspec, qspec], out_specs=(qspec, kvspec, kvspec),
               compiler_params=_cp("parallel", "arbitrary"))(q, k, v, do)


def _place():
    x, y, c = lax.axis_index("x"), lax.axis_index("y"), lax.axis_index("c")
    chips = [(1 - x, y), (x, 1 - y), (1 - x, 1 - y)]
    return x, y, c, chips


def _shard_win(ref, kind, R, C, j, h=None):
    if kind == "col":
        rows = pl.ds(0, R) if h is None else pl.ds(pl.multiple_of(h * (R // 2), 16), R // 2)
        return ref.at[rows, pl.ds(pl.multiple_of(j * C, LANES), C)]
    if h is None:
        return ref.at[pl.ds(pl.multiple_of(j * R, 16), R), :]
    return ref.at[pl.ds(pl.multiple_of(j * R + h * (R // 2), 16), R // 2), :]


def _any_specs(n):
    return [pl.BlockSpec(memory_space=pl.ANY)] * n


def _all_gather_weights(shards, kinds):
    n = len(shards)
    dims = [s.shape for s in shards]
    out_shape = [jax.ShapeDtypeStruct((R, 4 * C) if k == "col" else (4 * R, C), BF16)
                 for (R, C), k in zip(dims, kinds)]

    def body(*refs):
        ins, outs = refs[:n], refs[n:2 * n]
        send, recv, fsend, frecv, loc = refs[2 * n:]
        x, y, c, chips = _place()
        j = 2 * x + y
        local, sends, fwds = [], [], []
        for w in range(n):
            (R, C), kind = dims[w], kinds[w]
            cp = pltpu.make_async_copy(ins[w], _shard_win(outs[w], kind, R, C, j), loc.at[w])
            cp.start()
            local.append(cp)
            src = ins[w].at[pl.ds(pl.multiple_of(c * (R // 2), 16), R // 2), :]
            for q, chip in enumerate(chips):
                cp = pltpu.make_async_remote_copy(src, _shard_win(outs[w], kind, R, C, j, c), send.at[3 * w + q],
                                                  recv.at[3 * w + q], device_id=(*chip, c), device_id_type=MESH)
                cp.start()
                sends.append(cp)
        for w in range(n):
            (R, C), kind = dims[w], kinds[w]
            for q, (cx, cy) in enumerate(chips):
                win = _shard_win(outs[w], kind, R, C, 2 * cx + cy, c)
                pltpu.make_async_remote_copy(win, win, send.at[3 * w + q], recv.at[3 * w + q],
                                             device_id=(x, y, c), device_id_type=MESH).wait_recv()
                cp = pltpu.make_async_remote_copy(win, win, fsend.at[3 * w + q], frecv.at[3 * w + q],
                                                  device_id=(x, y, 1 - c), device_id_type=MESH)
                cp.start()
                fwds.append(cp)
        for w in range(n):
            (R, C), kind = dims[w], kinds[w]
            for q, (cx, cy) in enumerate(chips):
                win = _shard_win(outs[w], kind, R, C, 2 * cx + cy, 1 - c)
                pltpu.make_async_remote_copy(win, win, fsend.at[3 * w + q], frecv.at[3 * w + q],
                                             device_id=(x, y, c), device_id_type=MESH).wait_recv()
        for cp in sends + fwds:
            cp.wait_send()
        for cp in local:
            cp.wait()

    return _pc(body, name="ag_weights", out_shape=tuple(out_shape), in_specs=_any_specs(n),
               out_specs=tuple(_any_specs(n)),
               scratch_shapes=[pltpu.SemaphoreType.DMA((3 * n,))] * 4 + [pltpu.SemaphoreType.DMA((n,))])(*shards)


def _pair_peer():
    x, y, c = lax.axis_index("x"), lax.axis_index("y"), lax.axis_index("c")
    return c, (x, y, 1 - c)


def _exchange_step(i, n, src_ref, land, ssem, rsem, credit, peer, consume):
    def rdma(slot):
        return pltpu.make_async_remote_copy(src_ref, land.at[slot], ssem.at[slot], rsem.at[slot],
                                            device_id=peer, device_id_type=MESH)

    slot = i % 2

    @pl.when(i < n)
    def _():
        @pl.when(i >= 2)
        def _():
            pl.semaphore_wait(credit, 1)

        rdma(slot).start()

    @pl.when(i >= 1)
    def _():
        ps = (i - 1) % 2
        rdma(ps).wait_recv()
        consume(land[ps])

        @pl.when(i - 1 < n - 2)
        def _():
            pl.semaphore_signal(credit, 1, device_id=peer, device_id_type=MESH)

    @pl.when(i < n)
    def _():
        rdma(slot).wait_send()


def _xchg_scratch(tm, C, dtype):
    return [pltpu.VMEM((2, tm, C), dtype), pltpu.SemaphoreType.DMA((2,)), pltpu.SemaphoreType.DMA((2,)),
            pltpu.SemaphoreType.REGULAR]


def _rs_pair_sum(g, kind, *, name):
    K, N = g.shape
    NS, RH = (1, K // 2) if kind == "col" else (4, K // 8)
    tm = _tile(RH, max(16, (3 * 2 ** 19 // N) // 16 * 16), 16)
    if NS * (RH // tm) < 2:
        tm = RH // 2
    nb = RH // tm
    n = NS * nb
    cvec = lax.axis_index("c").reshape(1).astype(jnp.int32)

    def body(c_ref, send_ref, own_ref, out_ref, land, ssem, rsem, credit):
        _, peer = _pair_peer()

        def consume(v):
            out_ref[...] = (own_ref[...].astype(F32) + v.astype(F32)).astype(BF16)

        _exchange_step(pl.program_id(0), n, send_ref, land, ssem, rsem, credit, peer, consume)

    now = lambda i: jnp.minimum(i, n - 1)
    lag = lambda i: jnp.maximum(i - 1, 0)
    grid_spec = pltpu.PrefetchScalarGridSpec(
        num_scalar_prefetch=1, grid=(n + 1,),
        in_specs=[pl.BlockSpec((tm, N), lambda i, c: ((now(i) // nb * 2 + 1 - c[0]) * nb + now(i) % nb, 0)),
                  pl.BlockSpec((tm, N), lambda i, c: ((lag(i) // nb * 2 + c[0]) * nb + lag(i) % nb, 0))],
        out_specs=pl.BlockSpec((tm, N), lambda i, c: (lag(i), 0)),
        scratch_shapes=_xchg_scratch(tm, N, BF16))
    return _pc(body, name=name, out_shape=jax.ShapeDtypeStruct((NS * RH, N), BF16), grid_spec=grid_spec,
               compiler_params=_cp("arbitrary"))(cvec, g, g)


def _rs_scatter(parts, kinds):
    n = len(parts)
    pdims = [((p.shape[0], p.shape[1] // 4) if k == "col" else (p.shape[0] // 4, p.shape[1]))
             for p, k in zip(parts, kinds)]
    got_shape = [jax.ShapeDtypeStruct((3,) + d, BF16) for d in pdims]

    def body(*refs):
        ins, got = refs[:n], refs[n:2 * n]
        send, recv = refs[2 * n:]
        x, y, c, chips = _place()
        cps = []
        for w in range(n):
            R, C = pdims[w]
            for q, (cx, cy) in enumerate(chips):
                cp = pltpu.make_async_remote_copy(_shard_win(ins[w], kinds[w], R, C, 2 * cx + cy), got[w].at[q],
                                                  send.at[3 * w + q], recv.at[3 * w + q],
                                                  device_id=(cx, cy, c), device_id_type=MESH)
                cp.start()
                cps.append(cp)
        for cp in cps:
            cp.wait()

    return _pc(body, name="rs_scatter", out_shape=tuple(got_shape), in_specs=_any_specs(n),
               out_specs=tuple(_any_specs(n)), scratch_shapes=[pltpu.SemaphoreType.DMA((3 * n,))] * 2)(*parts)


def _rs_sum_join(part, got, kind, *, name):
    _, hr, hc = got.shape
    tm = _tile(hr, max(16, (3 * 2 ** 18 // hc) // 16 * 16), 16)
    if hr // tm < 2:
        tm = hr // 2
    nb = hr // tm
    jvec = (2 * lax.axis_index("x") + lax.axis_index("y")).reshape(1).astype(jnp.int32)

    def body(j_ref, p_ref, a_ref, b_ref, c_ref, mine_ref, theirs_ref, sbuf, land, ssem, rsem, credit):
        i = pl.program_id(0)
        _, peer = _pair_peer()

        @pl.when(i < nb)
        def _():
            v = ((p_ref[...].astype(F32) + a_ref[...].astype(F32)) + b_ref[...].astype(F32)) + c_ref[...].astype(F32)
            mine_ref[...] = v
            sbuf[...] = v

        def consume(v):
            theirs_ref[...] = v

        _exchange_step(i, nb, sbuf, land, ssem, rsem, credit, peer, consume)

    now = lambda i: jnp.minimum(i, nb - 1)
    lag = lambda i: jnp.maximum(i - 1, 0)
    own_spec = (pl.BlockSpec((tm, hc), lambda i, j: (now(i), j[0])) if kind == "col"
                else pl.BlockSpec((tm, hc), lambda i, j: (j[0] * nb + now(i), 0)))
    got_spec = lambda q: pl.BlockSpec((None, tm, hc), functools.partial(lambda i, j, q: (q, now(i), 0), q=q))
    grid_spec = pltpu.PrefetchScalarGridSpec(
        num_scalar_prefetch=1, grid=(nb + 1,),
        in_specs=[own_spec, got_spec(0), got_spec(1), got_spec(2)],
        out_specs=(pl.BlockSpec((tm, hc), lambda i, j: (now(i), 0)), pl.BlockSpec((tm, hc), lambda i, j: (lag(i), 0))),
        scratch_shapes=[pltpu.VMEM((tm, hc), F32)] + _xchg_scratch(tm, hc, F32))
    shp = jax.ShapeDtypeStruct((hr, hc), F32)
    return _pc(body, name=name, out_shape=(shp, shp), grid_spec=grid_spec,
               compiler_params=_cp("arbitrary"))(jvec, part, got, got, got)


def _adamw_join(w, mine, theirs, m, v, *, name):
    R2, wc = w.shape
    hr, hc = mine.shape
    assert R2 == 2 * hr and hc >= wc
    tm = _tile(hr, max(8, (2 ** 18 // hc) // 8 * 8), 8)

    def body(w_ref, a_ref, b_ref, m_ref, v_ref, g_ref, d_ref, m2_ref, v2_ref):
        gv = jnp.where(pl.program_id(0) == lax.axis_index("c"), a_ref[...], b_ref[...])[:, :wc]
        m2 = ADAM_B1 * m_ref[...] + (1.0 - ADAM_B1) * gv
        v2 = ADAM_B2 * v_ref[...] + (1.0 - ADAM_B2) * (gv * gv)
        m_hat = m2 / (1.0 - ADAM_B1 ** ADAM_STEP)
        v_hat = v2 / (1.0 - ADAM_B2 ** ADAM_STEP)
        g_ref[...] = gv
        d_ref[...] = -ADAM_LR * (m_hat / (jnp.sqrt(v_hat) + ADAM_EPS) + ADAM_WD * w_ref[...])
        m2_ref[...] = m2
        v2_ref[...] = v2

    wspec = pl.BlockSpec((None, tm, wc), lambda h, i: (h, i, 0))
    gspec = pl.BlockSpec((tm, hc), lambda h, i: (i, 0))
    shp = jax.ShapeDtypeStruct((2, hr, wc), F32)
    v3 = lambda a: a.reshape(2, hr, wc)
    outs = _pc(body, name=name, out_shape=(shp,) * 4, grid=(2, hr // tm),
               in_specs=[wspec, gspec, gspec, wspec, wspec], out_specs=(wspec,) * 4,
               compiler_params=_cp("parallel", "parallel"))(v3(w), mine, theirs, v3(m), v3(v))
    return [o.reshape(R2, wc) for o in outs]


def _all_reduce_small(pack):
    R = pack.shape[0]

    def body(p_ref, o_ref, buf, send, recv):
        x, y, c, _ = _place()
        me = 4 * x + 2 * y + c
        buf[0] = p_ref[...]
        cps = []
        for r in range(1, 8):
            peer = (x if r & 4 == 0 else 1 - x, y if r & 2 == 0 else 1 - y, c if r & 1 == 0 else 1 - c)
            cp = pltpu.make_async_remote_copy(p_ref, buf.at[r], send.at[r - 1], recv.at[r - 1],
                                              device_id=peer, device_id_type=MESH)
            cp.start()
            cps.append(cp)
        for cp in cps:
            cp.wait()
        acc = buf[jnp.bitwise_xor(me, 0)]
        for k in range(1, 8):
            acc = acc + buf[jnp.bitwise_xor(me, k)]
        o_ref[...] = acc

    return _pc(body, name="allreduce_small", out_shape=jax.ShapeDtypeStruct((R, LANES), F32),
               in_specs=[pl.BlockSpec(memory_space=pltpu.VMEM)], out_specs=pl.BlockSpec(memory_space=pltpu.VMEM),
               scratch_shapes=[pltpu.VMEM((8, R, LANES), F32), pltpu.SemaphoreType.DMA((7,)),
                               pltpu.SemaphoreType.DMA((7,))])(pack)


def _pack_small(arrs):
    parts, sizes = [], []
    for a in arrs:
        f = a.reshape(-1).astype(F32)
        n = _round_up(f.shape[0], 8 * LANES)
        parts.append(jnp.pad(f, (0, n - f.shape[0])).reshape(n // LANES, LANES))
        sizes.append((a.shape, f.shape[0], n // LANES))
    return jnp.concatenate(parts, axis=0), sizes


def _unpack_small(pack, sizes):
    out, r0 = [], 0
    for shape, cnt, rows in sizes:
        out.append(pack[r0:r0 + rows].reshape(-1)[:cnt].reshape(shape))
        r0 += rows
    return out


def _ffn_fwd(xin, norm, wgu, wd, tag):
    hb = _rms_fwd(xin, norm, name=tag + "_norm")
    gate, up, act = _gateup(hb, wgu, name=tag + "_gateup")
    xout = _mm(act, wd, res=xin, scale=0.5, name=tag + "_down", tm=1024, tn=512, tk=512)
    return xout, (hb, gate, up, act)


def _ffn_bwd(dy, dyb, xin, norm, wgu, wd, saved, tag):
    hb, gate, up, act = saved
    da = _mm(dyb, wd, mode="nt", scale=0.5, out_dtype=BF16, name=tag + "_dact", tm=1024, tn=512)
    d_wd = _mm(act, dyb, mode="tn", scale=0.5, out_dtype=BF16, name=tag + "_dwd", tm=512, tn=512, tk=4096)
    dgu = _swiglu_bwd(da, gate, up, name=tag + "_dswiglu")
    d_wgu = _mm(hb, dgu, mode="tn", out_dtype=BF16, name=tag + "_dwgu", tm=512, tn=512, tk=4096)
    dh = _mm(dgu, wgu, mode="nt", out_dtype=BF16, name=tag + "_dh", tm=1024, tn=512, tk=1024)
    dx, dxb, d_norm = _rms_bwd(xin, dh, norm, dy, name=tag + "_dnorm")
    return dx, dxb, d_wgu, d_wd, d_norm


def kernel(x, mem, ffn1_norm, ffn1_w_gate_up, ffn1_w_down, mix_norm, mem_norm, w_in, b_forget, pool_w, pool_scale, w_pool_up, fox_q_norm, fox_k_norm, w_fox_o, w_mem_kv, mem_q_norm, mem_k_norm, w_mem_o, w_out, ffn2_norm, ffn2_w_gate_up, ffn2_w_down, loss_target, m_ffn1_norm, m_ffn1_w_gate_up, m_ffn1_w_down, m_mix_norm, m_mem_norm, m_w_in, m_b_forget, m_pool_w, m_pool_scale, m_w_pool_up, m_fox_q_norm, m_fox_k_norm, m_w_fox_o, m_w_mem_kv, m_mem_q_norm, m_mem_k_norm, m_w_mem_o, m_w_out, m_ffn2_norm, m_ffn2_w_gate_up, m_ffn2_w_down, v_ffn1_norm, v_ffn1_w_gate_up, v_ffn1_w_down, v_mix_norm, v_mem_norm, v_w_in, v_b_forget, v_pool_w, v_pool_scale, v_w_pool_up, v_fox_q_norm, v_fox_k_norm, v_w_fox_o, v_w_mem_kv, v_mem_q_norm, v_mem_k_norm, v_w_mem_o, v_w_out, v_ffn2_norm, v_ffn2_w_gate_up, v_ffn2_w_down):
    P = dict(locals())
    big = ["ffn1_w_gate_up", "ffn1_w_down", "w_in", "w_pool_up", "w_fox_o", "w_mem_kv", "w_mem_o", "w_out",
           "ffn2_w_gate_up", "ffn2_w_down"]
    kinds = ["col", "row", "col", "col", "col", "row", "col", "row", "col", "row"]
    small = ["ffn1_norm", "mix_norm", "mem_norm", "b_forget", "pool_w", "pool_scale", "fox_q_norm", "fox_k_norm",
             "mem_q_norm", "mem_k_norm", "ffn2_norm"]
    order = ["ffn1_norm", "ffn1_w_gate_up", "ffn1_w_down", "mix_norm", "mem_norm", "w_in", "b_forget", "pool_w",
             "pool_scale", "w_pool_up", "fox_q_norm", "fox_k_norm", "w_fox_o", "w_mem_kv", "mem_q_norm", "mem_k_norm",
             "w_mem_o", "w_out", "ffn2_norm", "ffn2_w_gate_up", "ffn2_w_down"]

    nb, S, D = x.shape
    T = nb * S
    ML = mem.shape[1]
    NF, HD = b_forget.shape[-1], fox_q_norm.shape[-1]
    FW = NF * HD
    MW, MHD = w_mem_o.shape[1], mem_q_norm.shape[-1]
    PG, PD = pool_w.shape[1], pool_w.shape[2]
    PW = PG * PD
    n_in = w_in.shape[-1]
    n_in_pad = _round_up(n_in, LANES)
    in_w = 4 * n_in
    assert HD * 2 == LANES and MHD == LANES and PD == LANES and in_w == PW + 3 * FW + NF + MW + 3 * D

    shards = []
    for nme in big:
        wl = P[nme][0].astype(BF16)
        if nme == "w_in":
            wl = jnp.pad(wl, ((0, 0), (0, n_in_pad - n_in)))
        shards.append(wl)
    full = dict(zip(big, _all_gather_weights(shards, kinds)))
    win_o = jnp.concatenate([full["w_in"][:, j * n_in_pad:j * n_in_pad + n_in] for j in range(4)], axis=1)
    o_q, o_k, o_v, o_f = PW, PW + FW, PW + 2 * FW, PW + 3 * FW
    o_qm, o_gate = o_f + NF, o_f + NF + MW
    p_q, p_k, p_v, p_pool, p_qm, p_f = 3 * D, 3 * D + FW, 3 * D + 2 * FW, 3 * D + 3 * FW, 3 * D + 3 * FW + PW, 3 * D + 3 * FW + PW + MW
    inp = _round_up(p_f + LANES, 512)
    win_p = jnp.concatenate([win_o[:, o_gate:], win_o[:, o_q:o_f], win_o[:, :PW], win_o[:, o_qm:o_gate],
                             win_o[:, o_f:o_qm], jnp.zeros((D, inp - p_f - NF), BF16)], axis=1)

    x0 = x.reshape(T, D)
    x1, sv1 = _ffn_fwd(x0, ffn1_norm, full["ffn1_w_gate_up"], full["ffn1_w_down"], "ffn1")
    hb2 = _rms_fwd(x1, mix_norm, name="mix_norm")
    z = _mm(hb2, win_p, name="in_proj", tm=1024, tn=512)

    vpool = _pool_fwd(z, p_pool // LANES, pool_w[0], pool_scale, nb, S, name="pool_fwd")
    y_pool = _mm(vpool, full["w_pool_up"], name="pool_up", tm=1024, tn=512)

    b128 = jnp.pad(b_forget, ((0, 0), (0, LANES - NF)))
    c3 = _fox_c(z, p_f // LANES, b128, nb, S, NF, name="fox_c")
    qa, ka, vb = _fox_qk(z, p_q // LANES, p_k // LANES, p_v // LANES, c3, fox_q_norm, fox_k_norm, HD, NF,
                         name="fox_qk")
    o_fox = _fox_fwd(qa, ka, vb, nb, S, HD, name="fox_fwd")
    y_fox = _mm(o_fox, full["w_fox_o"], name="fox_out", tm=1024, tn=512)

    mem2 = mem.reshape(nb * ML, D)
    memn = _rms_fwd(mem2, mem_norm, name="mem_norm")
    kv = _mm(memn, full["w_mem_kv"], name="mem_kv")
    kmn = _headnorm_fwd(kv, 0, MW, mem_k_norm, MHD, name="mem_knorm")
    vmb = kv[:, MW:].astype(BF16)
    qmn = _headnorm_fwd(z, p_qm // LANES, MW, mem_q_norm, MHD, name="mem_qnorm")
    o_mem = _mem_fwd(qmn, kmn, vmb, nb, S, ML, name="mem_fwd")
    y_mem = _mm(o_mem, full["w_mem_o"], name="mem_out", tm=1024, tn=512)

    merged = _merge_fwd(z, y_pool, y_fox, y_mem, name="merge_fwd")
    x2 = _mm(merged, full["w_out"], res=x1, name="out_proj", tm=1024, tn=512)
    x3, sv2 = _ffn_fwd(x2, ffn2_norm, full["ffn2_w_gate_up"], full["ffn2_w_down"], "ffn2")
    dy3, dy3b, loss_part = _loss(x3, loss_target.reshape(T, D), name="loss")

    g = {}
    dx2, dx2b, g["ffn2_w_gate_up"], g["ffn2_w_down"], g["ffn2_norm"] = _ffn_bwd(
        dy3, dy3b, x2, ffn2_norm, full["ffn2_w_gate_up"], full["ffn2_w_down"], sv2, "ffn2")

    dmerged = _mm(dx2b, full["w_out"], mode="nt", name="d_merged", tm=1024, tn=512)
    g["w_out"] = _mm(merged, dx2b, mode="tn", out_dtype=BF16, name="d_w_out", tk=4096)
    dyp, dyf, dym, dgl = _merge_bwd(dmerged, z, y_pool, y_fox, y_mem, name="merge_bwd")

    g["w_pool_up"] = _mm(vpool, dyp, mode="tn", out_dtype=BF16, name="d_w_pool_up", tk=4096)
    dvp = _mm(dyp, full["w_pool_up"], mode="nt", name="d_vpool", tm=1024)
    du_pool, g["pool_w"], g["pool_scale"] = _pool_bwd(z, p_pool // LANES, dvp, pool_w[0], pool_scale, nb, S,
                                                      name="pool_bwd")

    g["w_fox_o"] = _mm(o_fox, dyf, mode="tn", out_dtype=BF16, name="d_w_fox_o", tk=4096)
    do_fox = _mm(dyf, full["w_fox_o"], mode="nt", out_dtype=BF16, name="d_o_fox", tm=1024)
    dqa, dka, dvf, dcs = _fox_bwd(qa, ka, vb, do_fox, nb, S, HD, name="fox_bwd")
    dq, dgq = _headnorm_bwd(z, p_q // LANES, FW, dqa, fox_q_norm, HD, HD ** -0.5, True, BF16, name="fox_dq")
    dk, dgk = _headnorm_bwd(z, p_k // LANES, FW, dka, fox_k_norm, HD, 1.0, True, BF16, name="fox_dk")
    g["fox_q_norm"], g["fox_k_norm"] = _fold_gain(dgq, HD), _fold_gain(dgk, HD)
    dc = -dcs[:, :, :LANES // HD, :].reshape(nb, NF, S).transpose(0, 2, 1).reshape(T, NF)
    df, db = _fox_c_bwd(jnp.pad(dc, ((0, 0), (0, LANES - NF))), z, p_f // LANES, b128, nb, S, name="fox_c_bwd")
    g["b_forget"] = db[:, :NF]

    g["w_mem_o"] = _mm(o_mem, dym, mode="tn", out_dtype=BF16, name="d_w_mem_o", tk=4096)
    do_mem = _mm(dym, full["w_mem_o"], mode="nt", out_dtype=BF16, name="d_o_mem", tm=1024)
    dqmn, dkmn, dvm = _mem_bwd(qmn, kmn, vmb, do_mem, nb, S, ML, name="mem_bwd")
    dqm, g["mem_q_norm"] = _headnorm_bwd(z, p_qm // LANES, MW, dqmn, mem_q_norm, MHD, 1.0, False, BF16, name="mem_dq")
    dkm, g["mem_k_norm"] = _headnorm_bwd(kv, 0, MW, dkmn, mem_k_norm, MHD, 1.0, False, BF16, name="mem_dk")
    dkv = jnp.concatenate([dkm, dvm.astype(BF16)], axis=1)
    g["w_mem_kv"] = _mm(memn, dkv, mode="tn", out_dtype=BF16, name="d_w_mem_kv")
    dmemn = _mm(dkv, full["w_mem_kv"], mode="nt", name="d_memn")
    _, _, g["mem_norm"] = _rms_bwd(mem2, dmemn, mem_norm, jnp.zeros_like(mem2), name="mem_dnorm")

    dz = jnp.concatenate([dgl, dq, dk, dvf.astype(BF16), du_pool, dqm, df.astype(BF16),
                          jnp.zeros((T, inp - p_f - LANES), BF16)], axis=1)
    dwin_p = _mm(hb2, dz, mode="tn", out_dtype=BF16, name="d_w_in", tk=4096)
    dh2 = _mm(dz, win_p, mode="nt", out_dtype=BF16, name="d_h2", tm=1024, tn=512, tk=512)
    dx1, dx1b, g["mix_norm"] = _rms_bwd(x1, dh2, mix_norm, dx2, name="mix_dnorm")

    dx0, _, g["ffn1_w_gate_up"], g["ffn1_w_down"], g["ffn1_norm"] = _ffn_bwd(
        dx1, dx1b, x0, ffn1_norm, full["ffn1_w_gate_up"], full["ffn1_w_down"], sv1, "ffn1")

    dwin_o = jnp.concatenate([dwin_p[:, p_pool:p_qm], dwin_p[:, p_q:p_pool], dwin_p[:, p_f:p_f + NF],
                              dwin_p[:, p_qm:p_f], dwin_p[:, :p_q]], axis=1)
    zpad = jnp.zeros((D, n_in_pad - n_in), BF16)
    g["w_in"] = jnp.concatenate(sum([[dwin_o[:, j * n_in:(j + 1) * n_in], zpad] for j in range(4)], []), axis=1)

    parts = [_rs_pair_sum(g[nme], k, name="rs_pair_" + nme) for nme, k in zip(big, kinds)]
    theirs = _rs_scatter(parts, kinds)
    halves = {nme: _rs_sum_join(p, t, k, name="rs_join_" + nme) for nme, p, t, k in zip(big, parts, theirs, kinds)}
    gfull = {}

    gpack, sizes = _pack_small([g[nme].reshape(P[nme].shape) for nme in small] + [loss_part[:, :1]])
    gsum = _unpack_small(_all_reduce_small(gpack), sizes)
    loss = gsum[-1].reshape(())
    for nme, a in zip(small, gsum[:-1]):
        gfull[nme] = a

    delta, new_m, new_v = {}, {}, {}
    for nme in big:
        shp = P[nme].shape
        two = lambda a: a.reshape(shp[-2:])
        gfull[nme], delta[nme], new_m[nme], new_v[nme] = [
            a.reshape(shp) for a in _adamw_join(two(P[nme]), *halves[nme], two(P["m_" + nme]), two(P["v_" + nme]),
                                                name="adamw_" + nme)]
    wpack, _ = _pack_small([P[nme] for nme in small])
    mpack, _ = _pack_small([P["m_" + nme] for nme in small])
    vpack, _ = _pack_small([P["v_" + nme] for nme in small])
    gpack2, ssz = _pack_small([gfull[nme] for nme in small])
    for dct, pk in zip((delta, new_m, new_v), _adamw(wpack, gpack2, mpack, vpack, name="adamw_small")):
        for nme, a in zip(small, _unpack_small(pk, ssz)):
            dct[nme] = a

    grad_x = dx0.reshape(nb, S, D)
    return (loss, grad_x, *[gfull[nme] for nme in order], *[delta[nme] for nme in order],
            *[new_m[nme] for nme in order], *[new_v[nme] for nme in order])
```

```python
import functools

import jax
import jax.numpy as jnp
from jax import lax
from jax.experimental import pallas as pl
from jax.experimental.pallas import tpu as pltpu

F32 = jnp.float32
BF16 = jnp.bfloat16
MESH = pl.DeviceIdType.MESH
EPS = 1e-6
POOL_WINDOWS = (2, 4, 8, 16)
ADAM_LR, ADAM_B1, ADAM_B2, ADAM_EPS, ADAM_WD, ADAM_STEP = 0.001, 0.9, 0.999, 1e-08, 0.01, 10
LANES = 128
VMEM_LIMIT = 56 * 1024 * 1024
NEG = -1e30


def _pc(body, *, name, cargo=None, **kw):
    if cargo is not None:
        body, kw = _load_cargo(body, cargo, kw)
    return pl.pallas_call(body, name=name, **kw)


def _cp(*sem):
    return pltpu.CompilerParams(dimension_semantics=sem, vmem_limit_bytes=VMEM_LIMIT)


def _tile(n, pref, mult=LANES):
    if n <= pref:
        return n
    t = (pref // mult) * mult
    while t >= mult:
        if n % t == 0:
            return t
        t -= mult
    return n


def _round_up(n, m):
    return (n + m - 1) // m * m


_DIMS = {"nn": (((1,), (0,)), ((), ())), "nt": (((1,), (1,)), ((), ())), "tn": (((0,), (0,)), ((), ()))}


def _split_cargo(res, cargo):
    if cargo is None:
        return res
    nco = len(cargo.out_shapes)
    own = res[:len(res) - nco]
    return (own[0] if len(own) == 1 else own), list(res[len(res) - nco:])


def _mm(a, b, *, name, mode="nn", out_dtype=F32, scale=1.0, res=None, tm=512, tn=512, tk=2048, cargo=None):
    if mode == "nn":
        (M, K), (K2, N) = a.shape, b.shape
    elif mode == "nt":
        (M, K), (N, K2) = a.shape, b.shape
    else:
        (K, M), (K2, N) = a.shape, b.shape
    assert K == K2, (name, a.shape, b.shape)
    tm, tn, tk = _tile(M, tm, 8 if M % LANES else LANES), _tile(N, tn), _tile(K, tk)
    nk = K // tk
    dims = _DIMS[mode]
    a_spec = (pl.BlockSpec((tk, tm), lambda i, j, k: (k, i)) if mode == "tn"
              else pl.BlockSpec((tm, tk), lambda i, j, k: (i, k)))
    b_spec = (pl.BlockSpec((tn, tk), lambda i, j, k: (j, k)) if mode == "nt"
              else pl.BlockSpec((tk, tn), lambda i, j, k: (k, j)))
    o_spec = pl.BlockSpec((tm, tn), lambda i, j, k: (i, j))
    has_res = res is not None

    def body(*refs):
        if has_res:
            a_ref, b_ref, r_ref, o_ref, acc = refs
        else:
            a_ref, b_ref, o_ref, acc = refs
        k = pl.program_id(2)

        @pl.when(k == 0)
        def _():
            acc[...] = jnp.zeros_like(acc)

        acc[...] += lax.dot_general(a_ref[...], b_ref[...], dims, preferred_element_type=F32)

        @pl.when(k == nk - 1)
        def _():
            v = acc[...] * scale
            if has_res:
                v = r_ref[...] + v
            o_ref[...] = v.astype(o_ref.dtype)

    out = _pc(body, name=name, cargo=cargo, out_shape=jax.ShapeDtypeStruct((M, N), out_dtype),
              grid=(M // tm, N // tn, nk),
              in_specs=[a_spec, b_spec] + ([o_spec] if has_res else []), out_specs=o_spec,
              scratch_shapes=[pltpu.VMEM((tm, tn), F32)],
              compiler_params=_cp("parallel", "parallel", "arbitrary"))(
                  *((a, b, res) if has_res else (a, b)), *(cargo.arrays if cargo else ()))
    return _split_cargo(out, cargo)


def _gateup(h, wgu, *, name, tm=512, tn=512, cargo=None):
    M, K = h.shape
    Fh = wgu.shape[1] // 2
    tm, tn = _tile(M, tm), _tile(Fh, tn)
    nj = Fh // tn

    def body(h_ref, wg_ref, wu_ref, g_ref, u_ref, a_ref):
        hv = h_ref[...]
        g = jnp.dot(hv, wg_ref[...], preferred_element_type=F32)
        u = jnp.dot(hv, wu_ref[...], preferred_element_type=F32)
        g_ref[...] = g.astype(BF16)
        u_ref[...] = u.astype(BF16)
        a_ref[...] = (g * jax.nn.sigmoid(g) * u).astype(BF16)

    o_spec = pl.BlockSpec((tm, tn), lambda i, j: (i, j))
    shp = jax.ShapeDtypeStruct((M, Fh), BF16)
    out = _pc(body, name=name, cargo=cargo, out_shape=(shp, shp, shp), grid=(M // tm, nj),
              in_specs=[pl.BlockSpec((tm, K), lambda i, j: (i, 0)),
                        pl.BlockSpec((K, tn), lambda i, j: (0, j)),
                        pl.BlockSpec((K, tn), lambda i, j: (0, j + nj))],
              out_specs=(o_spec, o_spec, o_spec),
              compiler_params=_cp("parallel", "parallel"))(h, wgu, wgu, *(cargo.arrays if cargo else ()))
    return _split_cargo(out, cargo)


def _rowmap(fn, ins, outs, *, rows, tm, name, accs=()):
    tm = _tile(rows, tm, 8)
    arrs, in_specs = [], []
    for d in ins:
        if d[0] == "row":
            _, a, w, cb = d
            w = a.shape[1] if w is None else w
            in_specs.append(pl.BlockSpec((tm, w), functools.partial(lambda i, cb: (i, cb), cb=cb)))
        else:
            a = d[1]
            in_specs.append(pl.BlockSpec(a.shape, functools.partial(lambda i, nd: (0,) * nd, nd=a.ndim)))
        arrs.append(a)
    out_shape = [jax.ShapeDtypeStruct((rows, w), dt) for w, dt in outs]
    out_specs = [pl.BlockSpec((tm, w), lambda i: (i, 0)) for w, _ in outs]
    for shp, dt in accs:
        out_shape.append(jax.ShapeDtypeStruct(shp, dt))
        out_specs.append(pl.BlockSpec(shp, functools.partial(lambda i, nd: (0,) * nd, nd=len(shp))))
    nin, nout, nacc = len(ins), len(outs), len(accs)

    def body(*refs):
        res = fn(*[r[...] for r in refs[:nin]])
        res = res if isinstance(res, (tuple, list)) else (res,)
        for r, v in zip(refs[nin:nin + nout], res[:nout]):
            r[...] = v.astype(r.dtype)
        if nacc:
            acc_refs = refs[nin + nout:]

            @pl.when(pl.program_id(0) == 0)
            def _():
                for r in acc_refs:
                    r[...] = jnp.zeros_like(r)

            for r, v in zip(acc_refs, res[nout:]):
                r[...] += v

    out = _pc(body, name=name, out_shape=tuple(out_shape), grid=(rows // tm,), in_specs=in_specs,
              out_specs=tuple(out_specs), compiler_params=_cp("arbitrary" if nacc else "parallel"))(*arrs)
    return out[0] if len(out) == 1 else out


def _rms_fwd(x, g, *, name):
    def fn(xv, gv):
        return xv * lax.rsqrt(jnp.mean(xv * xv, axis=-1, keepdims=True) + EPS) * gv
    return _rowmap(fn, [("row", x, None, 0), ("bc", g)], [(x.shape[1], BF16)], rows=x.shape[0], tm=256, name=name)


def _rms_bwd(x, dh, g, dres, *, name):
    D = x.shape[1]

    def fn(xv, dhv, gv, drv):
        r = lax.rsqrt(jnp.mean(xv * xv, axis=-1, keepdims=True) + EPS)
        xh = xv * r
        dhf = dhv.astype(F32)
        dxn = dhf * gv
        dx = drv + r * (dxn - xh * jnp.mean(dxn * xh, axis=-1, keepdims=True))
        return dx, dx, jnp.sum(dhf * xh, axis=0, keepdims=True)

    return _rowmap(fn, [("row", x, None, 0), ("row", dh, None, 0), ("bc", g), ("row", dres, None, 0)],
                   [(D, F32), (D, BF16)], rows=x.shape[0], tm=256, name=name, accs=[((1, D), F32)])


def _swiglu_bwd(da, gate, up, *, name):
    Fh = gate.shape[1]

    def fn(dav, gv, uv):
        d, g, u = dav.astype(F32), gv.astype(F32), uv.astype(F32)
        s = jax.nn.sigmoid(g)
        return jnp.concatenate([d * u * (s * (1.0 + g * (1.0 - s))), d * (g * s)], axis=-1)

    return _rowmap(fn, [("row", da, None, 0), ("row", gate, None, 0), ("row", up, None, 0)],
                   [(2 * Fh, BF16)], rows=da.shape[0], tm=128, name=name)


def _loss(y, tgt, *, name):
    D = y.shape[1]

    def fn(yv, tv):
        e = yv - tv
        d = e * (1.0 / D)
        part = 0.5 * jnp.sum(jnp.sum(e * e, axis=-1, keepdims=True) * (1.0 / D), axis=0, keepdims=True)
        return d, d, jnp.broadcast_to(part, (1, LANES))

    return _rowmap(fn, [("row", y, None, 0), ("row", tgt, None, 0)], [(D, F32), (D, BF16)],
                   rows=y.shape[0], tm=256, name=name, accs=[((1, LANES), F32)])


def _merge_fwd(z, yp, yf, ym, *, name):
    D = yp.shape[1]

    def fn(gp, gf, gm, a, b, c):
        return jax.nn.sigmoid(gp) * a + jax.nn.sigmoid(gf) * b + jax.nn.sigmoid(gm) * c

    return _rowmap(fn, [("row", z, D, 0), ("row", z, D, 1), ("row", z, D, 2),
                        ("row", yp, None, 0), ("row", yf, None, 0), ("row", ym, None, 0)],
                   [(D, BF16)], rows=yp.shape[0], tm=128, name=name)


def _merge_bwd(dm, z, yp, yf, ym, *, name):
    D = yp.shape[1]

    def fn(d, gp, gf, gm, a, b, c):
        outs, dgl = [], []
        for gl, yv in ((gp, a), (gf, b), (gm, c)):
            s = jax.nn.sigmoid(gl)
            outs.append(d * s)
            dgl.append(d * yv * (s * (1.0 - s)))
        return (*outs, jnp.concatenate(dgl, axis=-1))

    return _rowmap(fn, [("row", dm, None, 0), ("row", z, D, 0), ("row", z, D, 1), ("row", z, D, 2),
                        ("row", yp, None, 0), ("row", yf, None, 0), ("row", ym, None, 0)],
                   [(D, BF16), (D, BF16), (D, BF16), (3 * D, BF16)], rows=yp.shape[0], tm=128, name=name)


def _adamw(w, g, m, v, *, name):
    C = w.shape[1]

    def fn(wv, gv, mv, vv):
        m2 = ADAM_B1 * mv + (1.0 - ADAM_B1) * gv
        v2 = ADAM_B2 * vv + (1.0 - ADAM_B2) * (gv * gv)
        m_hat = m2 / (1.0 - ADAM_B1 ** ADAM_STEP)
        v_hat = v2 / (1.0 - ADAM_B2 ** ADAM_STEP)
        return -ADAM_LR * (m_hat / (jnp.sqrt(v_hat) + ADAM_EPS) + ADAM_WD * wv), m2, v2

    tm = max(8, (262144 // C) // 8 * 8)
    return _rowmap(fn, [("row", a, None, 0) for a in (w, g, m, v)], [(C, F32)] * 3,
                   rows=w.shape[0], tm=tm, name=name)


def _seg_mean(v, hd):
    if hd == LANES:
        return jnp.mean(v, axis=-1, keepdims=True)
    lo = lax.broadcasted_iota(jnp.int32, v.shape, 1) < hd
    s0 = jnp.sum(jnp.where(lo, v, 0.0), axis=-1, keepdims=True)
    s1 = jnp.sum(jnp.where(lo, 0.0, v), axis=-1, keepdims=True)
    return jnp.where(lo, s0, s1) * (1.0 / hd)


def _gain128(g, hd):
    g = g.reshape(1, hd).astype(F32)
    return g if hd == LANES else jnp.concatenate([g, g], axis=-1)


def _headnorm_fwd(x, cb0, width, gain, hd, *, name):
    rows = x.shape[0]
    tm = _tile(rows, 512, 8)
    g128 = _gain128(gain, hd)

    def body(x_ref, g_ref, o_ref):
        xv = x_ref[...]
        o_ref[...] = (xv * lax.rsqrt(_seg_mean(xv * xv, hd) + EPS) * g_ref[...]).astype(BF16)

    return _pc(body, name=name, out_shape=jax.ShapeDtypeStruct((rows, width), BF16),
               grid=(rows // tm, width // LANES),
               in_specs=[pl.BlockSpec((tm, LANES), lambda i, j: (i, cb0 + j)),
                         pl.BlockSpec((1, LANES), lambda i, j: (0, 0))],
               out_specs=pl.BlockSpec((tm, LANES), lambda i, j: (i, j)),
               compiler_params=_cp("parallel", "parallel"))(x, g128)


def _headnorm_bwd(x, cb0, width, dy, gain, hd, post, spread, out_dtype, *, name):
    rows = x.shape[0]
    tm = _tile(rows, 512, 8)
    g128 = _gain128(gain, hd)
    dyw = 2 * LANES if spread else LANES

    def body(x_ref, dy_ref, g_ref, dx_ref, dg_ref):
        xv = x_ref[...]
        dyv = dy_ref[...]
        if spread:
            dyv = jnp.concatenate([dyv[:, :hd], dyv[:, LANES:LANES + hd]], axis=-1)
        dyv = dyv * post
        r = lax.rsqrt(_seg_mean(xv * xv, hd) + EPS)
        xh = xv * r
        dxn = dyv * g_ref[...]
        dx_ref[...] = (r * (dxn - xh * _seg_mean(dxn * xh, hd))).astype(dx_ref.dtype)

        @pl.when((pl.program_id(0) == 0) & (pl.program_id(1) == 0))
        def _():
            dg_ref[...] = jnp.zeros_like(dg_ref)

        dg_ref[...] += jnp.sum(dyv * xh, axis=0, keepdims=True)

    return _pc(body, name=name,
               out_shape=(jax.ShapeDtypeStruct((rows, width), out_dtype), jax.ShapeDtypeStruct((1, LANES), F32)),
               grid=(rows // tm, width // LANES),
               in_specs=[pl.BlockSpec((tm, LANES), lambda i, j: (i, cb0 + j)),
                         pl.BlockSpec((tm, dyw), lambda i, j: (i, j)),
                         pl.BlockSpec((1, LANES), lambda i, j: (0, 0))],
               out_specs=(pl.BlockSpec((tm, LANES), lambda i, j: (i, j)),
                          pl.BlockSpec((1, LANES), lambda i, j: (0, 0))),
               compiler_params=_cp("arbitrary", "arbitrary"))(x, dy, g128)


def _fold_gain(dg, hd):
    return dg if hd == LANES else dg[:, :hd] + dg[:, hd:]


def _shift_down(v, k, row):
    return jnp.where(row >= k, pltpu.roll(v, k, 0), 0.0)


def _shift_up(v, k, row, S):
    return jnp.where(row < S - k, pltpu.roll(v, S - k, 0), 0.0)


def _window_sum(v, g, row, shift):
    s, outs = v, []
    for k in (1, 2, 4, 8):
        s = s + shift(s, k)
        outs.append(s)
    return jnp.where(g == 0, outs[0], jnp.where(g == 1, outs[1], jnp.where(g == 2, outs[2], outs[3])))


def _pool_fwd(z, cb0, pool_w, pool_scale, nb, S, *, name):
    PG, PD = pool_w.shape[0], pool_w.shape[1]
    assert PD == LANES and PG <= len(POOL_WINDOWS)
    pw = pool_w.astype(BF16)

    def body(u_ref, w_ref, s_ref, o_ref):
        g = pl.program_id(0)
        u = u_ref[...]
        row = lax.broadcasted_iota(jnp.int32, u.shape, 0)
        cnt = jnp.minimum(row + 1, jnp.left_shift(2, g)).astype(F32)
        pooled = _window_sum(u, g, row, lambda v, k: _shift_down(v, k, row)) / cnt
        mixed = jnp.dot((pooled - u).astype(BF16), w_ref[...], preferred_element_type=F32)
        o_ref[...] = (mixed * s_ref[...]).astype(BF16)

    return _pc(body, name=name, out_shape=jax.ShapeDtypeStruct((nb * S, PG * PD), BF16), grid=(PG, nb),
               in_specs=[pl.BlockSpec((S, PD), lambda g, b: (b, cb0 + g)),
                         pl.BlockSpec((None, PD, PD), lambda g, b: (g, 0, 0)),
                         pl.BlockSpec((1, PD), lambda g, b: (0, g))],
               out_specs=pl.BlockSpec((S, PD), lambda g, b: (b, g)),
               compiler_params=_cp("parallel", "parallel"))(z, pw, pool_scale)


def _pool_bwd(z, cb0, dv, pool_w, pool_scale, nb, S, *, name):
    PG, PD = pool_w.shape[0], pool_w.shape[1]
    pw = pool_w.astype(BF16)

    def body(u_ref, dv_ref, w_ref, s_ref, du_ref, dw_ref, ds_ref):
        g, b = pl.program_id(0), pl.program_id(1)
        u, dvv, w = u_ref[...], dv_ref[...], w_ref[...]
        row = lax.broadcasted_iota(jnp.int32, u.shape, 0)
        cnt = jnp.minimum(row + 1, jnp.left_shift(2, g)).astype(F32)
        diff = (_window_sum(u, g, row, lambda v, k: _shift_down(v, k, row)) / cnt - u).astype(BF16)
        mixed = jnp.dot(diff, w, preferred_element_type=F32)
        dmixed = (dvv * s_ref[...]).astype(BF16)
        ddiff = lax.dot_general(dmixed, w, _DIMS["nt"], preferred_element_type=F32)
        du_ref[...] = (_window_sum(ddiff / cnt, g, row, lambda v, k: _shift_up(v, k, row, S)) - ddiff).astype(BF16)

        @pl.when(b == 0)
        def _():
            dw_ref[...] = jnp.zeros_like(dw_ref)
            ds_ref[...] = jnp.zeros_like(ds_ref)

        dw_ref[...] += lax.dot_general(diff, dmixed, _DIMS["tn"], preferred_element_type=F32)
        ds_ref[...] += jnp.sum(dvv * mixed, axis=0, keepdims=True)

    return _pc(body, name=name,
               out_shape=(jax.ShapeDtypeStruct((nb * S, PG * PD), BF16), jax.ShapeDtypeStruct((PG, PD, PD), F32),
                          jax.ShapeDtypeStruct((1, PG * PD), F32)),
               grid=(PG, nb),
               in_specs=[pl.BlockSpec((S, PD), lambda g, b: (b, cb0 + g)),
                         pl.BlockSpec((S, PD), lambda g, b: (b, g)),
                         pl.BlockSpec((None, PD, PD), lambda g, b: (g, 0, 0)),
                         pl.BlockSpec((1, PD), lambda g, b: (0, g))],
               out_specs=(pl.BlockSpec((S, PD), lambda g, b: (b, g)),
                          pl.BlockSpec((None, PD, PD), lambda g, b: (g, 0, 0)),
                          pl.BlockSpec((1, PD), lambda g, b: (0, g))),
               compiler_params=_cp("parallel", "arbitrary"))(z, dv, pw, pool_scale)


def _split3(c):
    hi = c.astype(BF16)
    r1 = c - hi.astype(F32)
    mid = r1.astype(BF16)
    lo = (r1 - mid.astype(F32)).astype(BF16)
    return hi, mid, lo


def _log_sigmoid(v):
    return jnp.minimum(v, 0.0) - jnp.log(1.0 + jnp.exp(-jnp.abs(v)))


def _fox_c(z, cb, b128, nb, S, NF, *, name):
    def body(f_ref, b_ref, hi_ref, mid_ref, lo_ref):
        v = f_ref[...] + b_ref[...]
        lane = lax.broadcasted_iota(jnp.int32, v.shape, 1)
        row = lax.broadcasted_iota(jnp.int32, v.shape, 0)
        c = jnp.where(lane < NF, _log_sigmoid(v), 0.0)
        k = 1
        while k < S:
            c = c + _shift_down(c, k, row)
            k *= 2
        hi, mid, lo = _split3(c)
        hi_ref[...], mid_ref[...], lo_ref[...] = hi, mid, lo

    shp = jax.ShapeDtypeStruct((nb * S, LANES), BF16)
    spec = pl.BlockSpec((S, LANES), lambda i: (i, 0))
    return _pc(body, name=name, out_shape=(shp, shp, shp), grid=(nb,),
               in_specs=[pl.BlockSpec((S, LANES), lambda i: (i, cb)), pl.BlockSpec((1, LANES), lambda i: (0, 0))],
               out_specs=(spec, spec, spec), compiler_params=_cp("parallel"))(z, b128)


def _fox_c_bwd(dc, z, cb, b128, nb, S, *, name):
    def body(dc_ref, f_ref, b_ref, df_ref, db_ref):
        d = dc_ref[...]
        row = lax.broadcasted_iota(jnp.int32, d.shape, 0)
        k = 1
        while k < S:
            d = d + _shift_up(d, k, row, S)
            k *= 2
        df = d * jax.nn.sigmoid(-(f_ref[...] + b_ref[...]))
        df_ref[...] = df

        @pl.when(pl.program_id(0) == 0)
        def _():
            db_ref[...] = jnp.zeros_like(db_ref)

        db_ref[...] += jnp.sum(df, axis=0, keepdims=True)

    return _pc(body, name=name,
               out_shape=(jax.ShapeDtypeStruct((nb * S, LANES), F32), jax.ShapeDtypeStruct((1, LANES), F32)),
               grid=(nb,),
               in_specs=[pl.BlockSpec((S, LANES), lambda i: (i, 0)), pl.BlockSpec((S, LANES), lambda i: (i, cb)),
                         pl.BlockSpec((1, LANES), lambda i: (0, 0))],
               out_specs=(pl.BlockSpec((S, LANES), lambda i: (i, 0)), pl.BlockSpec((1, LANES), lambda i: (0, 0))),
               compiler_params=_cp("arbitrary"))(dc, z, b128)


def _fox_qk(z, cbq, cbk, cbv, c3, gq, gk, HD, NF, *, name):
    T = z.shape[0]
    HP = NF * HD // LANES
    tm = _tile(T, 512, 8)
    scale = HD ** -0.5
    gq128, gk128 = _gain128(gq, HD), _gain128(gk, HD)

    def body(q_ref, k_ref, v_ref, hi_ref, mid_ref, lo_ref, gq_ref, gk_ref, qa_ref, ka_ref, vb_ref):
        hp = pl.program_id(1)
        q, k = q_ref[...], k_ref[...]
        qn = q * lax.rsqrt(_seg_mean(q * q, HD) + EPS) * (gq_ref[...] * scale)
        kn = k * lax.rsqrt(_seg_mean(k * k, HD) + EPS) * gk_ref[...]
        vb_ref[...] = v_ref[...].astype(BF16)
        c3v = jnp.concatenate([hi_ref[...], mid_ref[...], lo_ref[...]], axis=-1)
        r = lax.broadcasted_iota(jnp.int32, (3 * LANES, LANES - HD), 0)
        cc = lax.broadcasted_iota(jnp.int32, (3 * LANES, LANES - HD), 1)
        lane = lax.broadcasted_iota(jnp.int32, (tm, LANES - HD), 1)
        for hh in range(LANES // HD):
            head = hp * (LANES // HD) + hh
            sel_q = jnp.where((cc < 3) & (r == head + LANES * cc), 1.0, 0.0).astype(BF16)
            sel_k = jnp.where((cc >= 3) & (cc < 6) & (r == head + LANES * (cc - 3)), 1.0, 0.0).astype(BF16)
            qaug = jnp.dot(c3v, sel_q, preferred_element_type=F32) + jnp.where((lane >= 3) & (lane < 6), 1.0, 0.0)
            kaug = jnp.where(lane < 3, 1.0, 0.0) - jnp.dot(c3v, sel_k, preferred_element_type=F32)
            sl = slice(hh * HD, (hh + 1) * HD)
            qa_ref[:, hh * LANES:(hh + 1) * LANES] = jnp.concatenate([qn[:, sl], qaug], axis=-1).astype(BF16)
            ka_ref[:, hh * LANES:(hh + 1) * LANES] = jnp.concatenate([kn[:, sl], kaug], axis=-1).astype(BF16)

    aw = (LANES // HD) * LANES
    blk = lambda cb: pl.BlockSpec((tm, LANES), functools.partial(lambda i, j, cb: (i, cb + j), cb=cb))
    cspec = pl.BlockSpec((tm, LANES), lambda i, j: (i, 0))
    gspec = pl.BlockSpec((1, LANES), lambda i, j: (0, 0))
    return _pc(body, name=name,
               out_shape=(jax.ShapeDtypeStruct((T, HP * aw), BF16), jax.ShapeDtypeStruct((T, HP * aw), BF16),
                          jax.ShapeDtypeStruct((T, NF * HD), BF16)),
               grid=(T // tm, HP),
               in_specs=[blk(cbq), blk(cbk), blk(cbv), cspec, cspec, cspec, gspec, gspec],
               out_specs=(pl.BlockSpec((tm, aw), lambda i, j: (i, j)), pl.BlockSpec((tm, aw), lambda i, j: (i, j)),
                          pl.BlockSpec((tm, LANES), lambda i, j: (i, j))),
               compiler_params=_cp("parallel", "parallel"))(z, z, z, *c3, gq128, gk128)


def _fox_probs(qa, ka, q0):
    s = lax.dot_general(qa, ka, _DIMS["nt"], preferred_element_type=F32)
    row = lax.broadcasted_iota(jnp.int32, s.shape, 0) + q0
    col = lax.broadcasted_iota(jnp.int32, s.shape, 1)
    s = jnp.where(col <= row, s, NEG)
    e = jnp.exp(s - jnp.max(s, axis=-1, keepdims=True))
    return e / jnp.sum(e, axis=-1, keepdims=True)


def _fox_fwd(qa, ka, vb, nb, S, HD, *, name, tq=256, cargo=None):
    T = qa.shape[0]
    nh = LANES // HD
    aw = nh * LANES
    HP = qa.shape[1] // aw
    tq = _tile(S, tq, 8)
    nq = S // tq

    def body(qa_ref, ka_ref, v_ref, o_ref):
        q0 = pl.program_id(2) * tq
        outs = []
        v = v_ref[...]
        for hh in range(nh):
            p = _fox_probs(qa_ref[:, hh * LANES:(hh + 1) * LANES], ka_ref[:, hh * LANES:(hh + 1) * LANES], q0)
            outs.append(jnp.dot(p.astype(BF16), v[:, hh * HD:(hh + 1) * HD], preferred_element_type=F32))
        o_ref[...] = jnp.concatenate(outs, axis=-1).astype(BF16)

    out = _pc(body, name=name, cargo=cargo, out_shape=jax.ShapeDtypeStruct((T, HP * LANES), BF16),
              grid=(nb, HP, nq),
              in_specs=[pl.BlockSpec((tq, aw), lambda b, h, i: (b * nq + i, h)),
                        pl.BlockSpec((S, aw), lambda b, h, i: (b, h)),
                        pl.BlockSpec((S, LANES), lambda b, h, i: (b, h))],
              out_specs=pl.BlockSpec((tq, LANES), lambda b, h, i: (b * nq + i, h)),
              compiler_params=_cp("parallel", "parallel", "arbitrary"))(qa, ka, vb, *(cargo.arrays if cargo else ()))
    return _split_cargo(out, cargo)


def _fox_bwd(qa, ka, vb, do, nb, S, HD, *, name, tq=256):
    T = qa.shape[0]
    nh = LANES // HD
    aw = nh * LANES
    HP = qa.shape[1] // aw
    tq = _tile(S, tq, 8)
    nq = S // tq

    def body(qa_ref, ka_ref, v_ref, do_ref, dqa_ref, dka_ref, dv_ref, dcs_ref):
        i = pl.program_id(2)
        q0 = i * tq
        dkas, dvs, css = [], [], []
        v, dob = v_ref[...], do_ref[...]
        for hh in range(nh):
            al = slice(hh * LANES, (hh + 1) * LANES)
            hl = slice(hh * HD, (hh + 1) * HD)
            qav, kav, vv, dov = qa_ref[:, al], ka_ref[:, al], v[:, hl], dob[:, hl]
            p = _fox_probs(qav, kav, q0)
            dp = lax.dot_general(dov, vv, _DIMS["nt"], preferred_element_type=F32)
            dsf = p * (dp - jnp.sum(p * dp, axis=-1, keepdims=True))
            css.append(jnp.sum(dsf, axis=0, keepdims=True))
            ds = dsf.astype(BF16)
            dqa_ref[:, al] = jnp.dot(ds, kav, preferred_element_type=F32)
            dkas.append(lax.dot_general(ds, qav, _DIMS["tn"], preferred_element_type=F32))
            dvs.append(lax.dot_general(p.astype(BF16), dov, _DIMS["tn"], preferred_element_type=F32))

        @pl.when(i == 0)
        def _():
            dka_ref[...] = jnp.zeros_like(dka_ref)
            dv_ref[...] = jnp.zeros_like(dv_ref)
            dcs_ref[...] = jnp.zeros_like(dcs_ref)

        dka_ref[...] += jnp.concatenate(dkas, axis=-1)
        dv_ref[...] += jnp.concatenate(dvs, axis=-1)
        dcs_ref[...] += jnp.concatenate(css + [jnp.zeros((8 - nh, S), F32)], axis=0)

    return _pc(body, name=name,
               out_shape=(jax.ShapeDtypeStruct((T, HP * aw), F32), jax.ShapeDtypeStruct((T, HP * aw), F32),
                          jax.ShapeDtypeStruct((T, HP * LANES), F32), jax.ShapeDtypeStruct((nb, HP, 8, S), F32)),
               grid=(nb, HP, nq),
               in_specs=[pl.BlockSpec((tq, aw), lambda b, h, i: (b * nq + i, h)),
                         pl.BlockSpec((S, aw), lambda b, h, i: (b, h)),
                         pl.BlockSpec((S, LANES), lambda b, h, i: (b, h)),
                         pl.BlockSpec((tq, LANES), lambda b, h, i: (b * nq + i, h))],
               out_specs=(pl.BlockSpec((tq, aw), lambda b, h, i: (b * nq + i, h)),
                          pl.BlockSpec((S, aw), lambda b, h, i: (b, h)),
                          pl.BlockSpec((S, LANES), lambda b, h, i: (b, h)),
                          pl.BlockSpec((None, None, 8, S), lambda b, h, i: (b, h, 0, 0))),
               compiler_params=_cp("parallel", "parallel", "arbitrary"))(qa, ka, vb, do)


def _mem_probs(q, k, scale):
    s = lax.dot_general(q, k, _DIMS["nt"], preferred_element_type=F32) * scale
    e = jnp.exp(s - jnp.max(s, axis=-1, keepdims=True))
    return e / jnp.sum(e, axis=-1, keepdims=True)


def _mem_fwd(q, k, v, nb, S, ML, *, name, tq=512):
    T, MW = q.shape
    nh = MW // LANES
    tq = _tile(S, tq, 8)
    nq = S // tq
    scale = LANES ** -0.5

    def body(q_ref, k_ref, v_ref, o_ref):
        for h in range(nh):
            sl = slice(h * LANES, (h + 1) * LANES)
            p = _mem_probs(q_ref[:, sl], k_ref[:, sl], scale)
            o_ref[:, sl] = jnp.dot(p.astype(BF16), v_ref[:, sl], preferred_element_type=F32).astype(BF16)

    return _pc(body, name=name, out_shape=jax.ShapeDtypeStruct((T, MW), BF16), grid=(nb, nq),
               in_specs=[pl.BlockSpec((tq, MW), lambda b, i: (b * nq + i, 0)),
                         pl.BlockSpec((ML, MW), lambda b, i: (b, 0)), pl.BlockSpec((ML, MW), lambda b, i: (b, 0))],
               out_specs=pl.BlockSpec((tq, MW), lambda b, i: (b * nq + i, 0)),
               compiler_params=_cp("parallel", "arbitrary"))(q, k, v)


def _mem_bwd(q, k, v, do, nb, S, ML, *, name, tq=512):
    T, MW = q.shape
    nh = MW // LANES
    tq = _tile(S, tq, 8)
    nq = S // tq
    scale = LANES ** -0.5

    def body(q_ref, k_ref, v_ref, do_ref, dq_ref, dk_ref, dv_ref):
        i = pl.program_id(1)

        @pl.when(i == 0)
        def _():
            dk_ref[...] = jnp.zeros_like(dk_ref)
            dv_ref[...] = jnp.zeros_like(dv_ref)

        for h in range(nh):
            sl = slice(h * LANES, (h + 1) * LANES)
            qv, kv, vv, dov = q_ref[:, sl], k_ref[:, sl], v_ref[:, sl], do_ref[:, sl]
            p = _mem_probs(qv, kv, scale)
            dp = lax.dot_general(dov, vv, _DIMS["nt"], preferred_element_type=F32)
            ds = (p * (dp - jnp.sum(p * dp, axis=-1, keepdims=True)) * scale).astype(BF16)
            dq_ref[:, sl] = jnp.dot(ds, kv, preferred_element_type=F32)
            dk_ref[:, sl] += lax.dot_general(ds, qv, _DIMS["tn"], preferred_element_type=F32)
            dv_ref[:, sl] += lax.dot_general(p.astype(BF16), dov, _DIMS["tn"], preferred_element_type=F32)

    kvspec = pl.BlockSpec((ML, MW), lambda b, i: (b, 0))
    qspec = pl.BlockSpec((tq, MW), lambda b, i: (b * nq + i, 0))
    return _pc(body, name=name,
               out_shape=(jax.ShapeDtypeStruct((T, MW), F32), jax.ShapeDtypeStruct((nb * ML, MW), F32),
                          jax.ShapeDtypeStruct((nb * ML, MW), F32)),
               grid=(nb, nq), in_specs=[qspec, kvspec, kvspec, qspec], out_specs=(qspec, kvspec, kvspec),
               compiler_params=_cp("parallel", "arbitrary"))(q, k, v, do)


def _place():
    x, y, c = lax.axis_index("x"), lax.axis_index("y"), lax.axis_index("c")
    chips = [(1 - x, y), (x, 1 - y), (1 - x, 1 - y)]
    return x, y, c, chips


def _shard_win(ref, kind, R, C, j, h=None):
    if kind == "col":
        rows = pl.ds(0, R) if h is None else pl.ds(pl.multiple_of(h * (R // 2), 16), R // 2)
        return ref.at[rows, pl.ds(pl.multiple_of(j * C, LANES), C)]
    if h is None:
        return ref.at[pl.ds(pl.multiple_of(j * R, 16), R), :]
    return ref.at[pl.ds(pl.multiple_of(j * R + h * (R // 2), 16), R // 2), :]


def _any_specs(n):
    return [pl.BlockSpec(memory_space=pl.ANY)] * n


class _Cargo:
    def __init__(self, arrays, out_shapes, ncopies, copies):
        self.arrays, self.out_shapes, self.ncopies, self.copies = list(arrays), list(out_shapes), ncopies, copies


def _load_cargo(body, cargo, kw):
    as_list = lambda v: list(v) if isinstance(v, (tuple, list)) else [v]
    grid = kw.get("grid", ())
    in_specs, out_specs, out_shape = as_list(kw["in_specs"]), as_list(kw["out_specs"]), as_list(kw["out_shape"])
    scratch = list(kw.get("scratch_shapes", ()))
    nin, nout, nci, nco, nscr = len(in_specs), len(out_specs), len(cargo.arrays), len(cargo.out_shapes), len(scratch)

    def loaded(*refs):
        ins, cin = refs[:nin], refs[nin:nin + nci]
        outs, cout = refs[nin + nci:nin + nci + nout], refs[nin + nci + nout:nin + nci + nout + nco]
        scr, (ssem, rsem) = refs[nin + nci + nout + nco:-2], refs[-2:]
        first = last = True
        for a, g in enumerate(grid):
            first = first & (pl.program_id(a) == 0)
            last = last & (pl.program_id(a) == g - 1)

        def start():
            for cp in cargo.copies(cin, cout, ssem, rsem, False):
                cp.start()

        def wait():
            for cp in cargo.copies(cin, cout, ssem, rsem, True):
                cp.wait_send()
                cp.wait_recv()

        if grid:
            pl.when(first)(start)
        else:
            start()
        body(*ins, *outs, *scr)
        if grid:
            pl.when(last)(wait)
        else:
            wait()

    kw = dict(kw, in_specs=in_specs + _any_specs(nci), out_specs=tuple(out_specs + _any_specs(nco)),
              out_shape=tuple(out_shape + cargo.out_shapes),
              scratch_shapes=scratch + [pltpu.SemaphoreType.DMA((cargo.ncopies,))] * 2)
    if grid:
        kw["compiler_params"] = _cp(*["arbitrary"] * len(grid))
    return loaded, kw


def _run_cargo(cargo, *, name):
    res = _pc(lambda: None, name=name, cargo=cargo, in_specs=[], out_specs=[], out_shape=[])(*cargo.arrays)
    return list(res)


def _full_shape(dims, kind):
    return (dims[0], 4 * dims[1]) if kind == "col" else (4 * dims[0], dims[1])


def _ag_copy(shard, full, kind, j_src, chip, c, ssem, rsem):
    R, C = shard.shape
    src = shard.at[pl.ds(pl.multiple_of(c * (R // 2), 16), R // 2), :]
    return pltpu.make_async_remote_copy(src, _shard_win(full, kind, R, C, j_src, c), ssem, rsem,
                                        device_id=(*chip, c), device_id_type=MESH)


def _ag_cargo(shards, kinds):
    def copies(ins, outs, ssem, rsem, waiting):
        x, y, c, chips = _place()
        return [_ag_copy(ins[w], outs[w], kinds[w], 2 * chip[0] + chip[1] if waiting else 2 * x + y, chip, c,
                         ssem.at[3 * w + q], rsem.at[3 * w + q])
                for w in range(len(shards)) for q, chip in enumerate(chips)]

    return _Cargo(shards, [jax.ShapeDtypeStruct(_full_shape(s.shape, k), BF16) for s, k in zip(shards, kinds)],
                  3 * len(shards), copies)


def _scatter_cargo(parts, kinds):
    pdims = [((p.shape[0], p.shape[1] // 4) if k == "col" else (p.shape[0] // 4, p.shape[1]))
             for p, k in zip(parts, kinds)]

    def copies(ins, outs, ssem, rsem, waiting):
        x, y, c, chips = _place()
        return [pltpu.make_async_remote_copy(_shard_win(ins[w], kinds[w], *pdims[w], 2 * cx + cy), outs[w].at[q],
                                             ssem.at[3 * w + q], rsem.at[3 * w + q],
                                             device_id=(cx, cy, c), device_id_type=MESH)
                for w in range(len(parts)) for q, (cx, cy) in enumerate(chips)]

    return _Cargo(parts, [jax.ShapeDtypeStruct((3,) + d, BF16) for d in pdims], 3 * len(parts), copies)


def _ag_fill(full, shard, kind, *, name):
    R, C = shard.shape
    RH = R // 2
    tm = _tile(RH, max(16, (3 * 2 ** 19 // C) // 16 * 16), 16)
    nbh = RH // tm
    n, nown = 3 * nbh, 2 * nbh
    xyc = jnp.stack([lax.axis_index("x"), lax.axis_index("y"), lax.axis_index("c")]).astype(jnp.int32)

    def chip_of(q, x, y):
        keep_x, keep_y = q % 2, (2 - q) // 2
        return 2 * ((1 - x) + keep_x * (2 * x - 1)) + (1 - y) + keep_y * (2 * y - 1)

    def window(ref, j, row0):
        if kind == "col":
            return ref.at[pl.ds(pl.multiple_of(row0, 16), tm), pl.ds(pl.multiple_of(j * C, LANES), C)]
        return ref.at[pl.ds(pl.multiple_of(j * R + row0, 16), tm), :]

    def body(p_ref, send_ref, own_ref, out_ref, land, ssem, rsem, wsem):
        i = pl.program_id(0)
        x, y = lax.axis_index("x"), lax.axis_index("y")
        c, peer = _pair_peer()
        own_cp = pltpu.make_async_copy(own_ref, window(out_ref, 2 * x + y, jnp.minimum(i, nown - 1) * tm), wsem.at[1])

        @pl.when(i < nown)
        def _():
            own_cp.start()

        def consume(land, ps):
            b = i - 1
            cp = pltpu.make_async_copy(land.at[ps], window(out_ref, chip_of(b // nbh, x, y),
                                                           (1 - c) * RH + (b % nbh) * tm), wsem.at[0])
            cp.start()
            cp.wait()

        _exchange_step(i, n, send_ref, land, ssem, rsem, None, peer, consume)

        @pl.when(i < nown)
        def _():
            own_cp.wait()

    now = lambda i: jnp.minimum(i, n - 1)

    def send_map(i, p):
        q, b = now(i) // nbh, now(i) % nbh
        j = chip_of(q, p[0], p[1])
        return (p[2] * nbh + b, j) if kind == "col" else ((j * 2 + p[2]) * nbh + b, 0)

    grid_spec = pltpu.PrefetchScalarGridSpec(
        num_scalar_prefetch=1, grid=(n + 1,),
        in_specs=[pl.BlockSpec((tm, C), send_map),
                  pl.BlockSpec((tm, C), lambda i, p: (jnp.minimum(i, nown - 1), 0))],
        out_specs=pl.BlockSpec(memory_space=pl.ANY),
        scratch_shapes=[pltpu.VMEM((n, tm, C), BF16), pltpu.SemaphoreType.DMA((n,)), pltpu.SemaphoreType.DMA((n,)),
                        pltpu.SemaphoreType.DMA((2,))])
    return _pc(body, name=name, out_shape=jax.ShapeDtypeStruct(full.shape, BF16), grid_spec=grid_spec,
               input_output_aliases={1: 0}, compiler_params=_cp("arbitrary"))(xyc, full, shard)


def _pair_peer():
    x, y, c = lax.axis_index("x"), lax.axis_index("y"), lax.axis_index("c")
    return c, (x, y, 1 - c)


def _exchange_step(i, n, src_ref, land, ssem, rsem, credit, peer, consume):
    def rdma(slot):
        return pltpu.make_async_remote_copy(src_ref, land.at[slot], ssem.at[slot], rsem.at[slot],
                                            device_id=peer, device_id_type=MESH)

    reuse = credit is not None
    slot = i % 2 if reuse else jnp.minimum(i, n - 1)

    @pl.when(i < n)
    def _():
        if reuse:
            @pl.when(i >= 2)
            def _():
                pl.semaphore_wait(credit, 1)

        rdma(slot).start()

    @pl.when(i >= 1)
    def _():
        ps = (i - 1) % 2 if reuse else i - 1
        rdma(ps).wait_recv()
        consume(land, ps)

        if reuse:
            @pl.when(i - 1 < n - 2)
            def _():
                pl.semaphore_signal(credit, 1, device_id=peer, device_id_type=MESH)

    @pl.when(i < n)
    def _():
        rdma(slot).wait_send()


def _xchg_scratch(tm, C, dtype):
    return [pltpu.VMEM((2, tm, C), dtype), pltpu.SemaphoreType.DMA((2,)), pltpu.SemaphoreType.DMA((2,)),
            pltpu.SemaphoreType.REGULAR]


def _rs_pair_sum(g, kind, *, name):
    K, N = g.shape
    NS, RH = (1, K // 2) if kind == "col" else (4, K // 8)
    tm = _tile(RH, max(16, (3 * 2 ** 19 // N) // 16 * 16), 16)
    if NS * (RH // tm) < 2:
        tm = RH // 2
    nb = RH // tm
    n = NS * nb
    cvec = lax.axis_index("c").reshape(1).astype(jnp.int32)

    def body(c_ref, send_ref, own_ref, out_ref, land, ssem, rsem, credit):
        _, peer = _pair_peer()

        def consume(land, ps):
            out_ref[...] = (own_ref[...].astype(F32) + land[ps].astype(F32)).astype(BF16)

        _exchange_step(pl.program_id(0), n, send_ref, land, ssem, rsem, credit, peer, consume)

    now = lambda i: jnp.minimum(i, n - 1)
    lag = lambda i: jnp.maximum(i - 1, 0)
    grid_spec = pltpu.PrefetchScalarGridSpec(
        num_scalar_prefetch=1, grid=(n + 1,),
        in_specs=[pl.BlockSpec((tm, N), lambda i, c: ((now(i) // nb * 2 + 1 - c[0]) * nb + now(i) % nb, 0)),
                  pl.BlockSpec((tm, N), lambda i, c: ((lag(i) // nb * 2 + c[0]) * nb + lag(i) % nb, 0))],
        out_specs=pl.BlockSpec((tm, N), lambda i, c: (lag(i), 0)),
        scratch_shapes=_xchg_scratch(tm, N, BF16))
    return _pc(body, name=name, out_shape=jax.ShapeDtypeStruct((NS * RH, N), BF16), grid_spec=grid_spec,
               compiler_params=_cp("arbitrary"))(cvec, g, g)


def _rs_sum_join(part, got, kind, *, name):
    _, hr, hc = got.shape
    tm = _tile(hr, max(16, (3 * 2 ** 18 // hc) // 16 * 16), 16)
    if hr // tm < 2:
        tm = hr // 2
    nb = hr // tm
    jvec = (2 * lax.axis_index("x") + lax.axis_index("y")).reshape(1).astype(jnp.int32)

    def body(j_ref, p_ref, a_ref, b_ref, c_ref, mine_ref, theirs_ref, sbuf, land, ssem, rsem, credit):
        i = pl.program_id(0)
        _, peer = _pair_peer()

        @pl.when(i < nb)
        def _():
            v = ((p_ref[...].astype(F32) + a_ref[...].astype(F32)) + b_ref[...].astype(F32)) + c_ref[...].astype(F32)
            mine_ref[...] = v
            sbuf[...] = v

        def consume(land, ps):
            theirs_ref[...] = land[ps]

        _exchange_step(i, nb, sbuf, land, ssem, rsem, credit, peer, consume)

    now = lambda i: jnp.minimum(i, nb - 1)
    lag = lambda i: jnp.maximum(i - 1, 0)
    own_spec = (pl.BlockSpec((tm, hc), lambda i, j: (now(i), j[0])) if kind == "col"
                else pl.BlockSpec((tm, hc), lambda i, j: (j[0] * nb + now(i), 0)))
    got_spec = lambda q: pl.BlockSpec((None, tm, hc), functools.partial(lambda i, j, q: (q, now(i), 0), q=q))
    grid_spec = pltpu.PrefetchScalarGridSpec(
        num_scalar_prefetch=1, grid=(nb + 1,),
        in_specs=[own_spec, got_spec(0), got_spec(1), got_spec(2)],
        out_specs=(pl.BlockSpec((tm, hc), lambda i, j: (now(i), 0)), pl.BlockSpec((tm, hc), lambda i, j: (lag(i), 0))),
        scratch_shapes=[pltpu.VMEM((tm, hc), F32)] + _xchg_scratch(tm, hc, F32))
    shp = jax.ShapeDtypeStruct((hr, hc), F32)
    return _pc(body, name=name, out_shape=(shp, shp), grid_spec=grid_spec,
               compiler_params=_cp("arbitrary"))(jvec, part, got, got, got)


def _adamw_join(w, mine, theirs, m, v, *, name):
    R2, wc = w.shape
    hr, hc = mine.shape
    assert R2 == 2 * hr and hc >= wc
    tm = _tile(hr, max(8, (2 ** 18 // hc) // 8 * 8), 8)

    def body(w_ref, a_ref, b_ref, m_ref, v_ref, g_ref, d_ref, m2_ref, v2_ref):
        gv = jnp.where(pl.program_id(0) == lax.axis_index("c"), a_ref[...], b_ref[...])[:, :wc]
        m2 = ADAM_B1 * m_ref[...] + (1.0 - ADAM_B1) * gv
        v2 = ADAM_B2 * v_ref[...] + (1.0 - ADAM_B2) * (gv * gv)
        m_hat = m2 / (1.0 - ADAM_B1 ** ADAM_STEP)
        v_hat = v2 / (1.0 - ADAM_B2 ** ADAM_STEP)
        g_ref[...] = gv
        d_ref[...] = -ADAM_LR * (m_hat / (jnp.sqrt(v_hat) + ADAM_EPS) + ADAM_WD * w_ref[...])
        m2_ref[...] = m2
        v2_ref[...] = v2

    wspec = pl.BlockSpec((None, tm, wc), lambda h, i: (h, i, 0))
    gspec = pl.BlockSpec((tm, hc), lambda h, i: (i, 0))
    shp = jax.ShapeDtypeStruct((2, hr, wc), F32)
    v3 = lambda a: a.reshape(2, hr, wc)
    outs = _pc(body, name=name, out_shape=(shp,) * 4, grid=(2, hr // tm),
               in_specs=[wspec, gspec, gspec, wspec, wspec], out_specs=(wspec,) * 4,
               compiler_params=_cp("parallel", "parallel"))(v3(w), mine, theirs, v3(m), v3(v))
    return [o.reshape(R2, wc) for o in outs]


def _all_reduce_small(pack):
    R = pack.shape[0]

    def body(p_ref, o_ref, buf, send, recv):
        x, y, c, _ = _place()
        me = 4 * x + 2 * y + c
        buf[0] = p_ref[...]
        cps = []
        for r in range(1, 8):
            peer = (x if r & 4 == 0 else 1 - x, y if r & 2 == 0 else 1 - y, c if r & 1 == 0 else 1 - c)
            cp = pltpu.make_async_remote_copy(p_ref, buf.at[r], send.at[r - 1], recv.at[r - 1],
                                              device_id=peer, device_id_type=MESH)
            cp.start()
            cps.append(cp)
        for cp in cps:
            cp.wait()
        acc = buf[jnp.bitwise_xor(me, 0)]
        for k in range(1, 8):
            acc = acc + buf[jnp.bitwise_xor(me, k)]
        o_ref[...] = acc

    return _pc(body, name="allreduce_small", out_shape=jax.ShapeDtypeStruct((R, LANES), F32),
               in_specs=[pl.BlockSpec(memory_space=pltpu.VMEM)], out_specs=pl.BlockSpec(memory_space=pltpu.VMEM),
               scratch_shapes=[pltpu.VMEM((8, R, LANES), F32), pltpu.SemaphoreType.DMA((7,)),
                               pltpu.SemaphoreType.DMA((7,))])(pack)


def _pack_small(arrs):
    parts, sizes = [], []
    for a in arrs:
        f = a.reshape(-1).astype(F32)
        n = _round_up(f.shape[0], 8 * LANES)
        parts.append(jnp.pad(f, (0, n - f.shape[0])).reshape(n // LANES, LANES))
        sizes.append((a.shape, f.shape[0], n // LANES))
    return jnp.concatenate(parts, axis=0), sizes


def _unpack_small(pack, sizes):
    out, r0 = [], 0
    for shape, cnt, rows in sizes:
        out.append(pack[r0:r0 + rows].reshape(-1)[:cnt].reshape(shape))
        r0 += rows
    return out


def _ffn_fwd(xin, norm, weight, tag, gather_on_gateup, gather_on_down):
    hb = _rms_fwd(xin, norm, name=tag + "_norm")
    cargo, landed = gather_on_gateup
    res = _gateup(hb, weight("w_gate_up"), name=tag + "_gateup", cargo=cargo)
    if cargo is not None:
        res, lands = res
        landed(lands)
    gate, up, act = res
    cargo, landed = gather_on_down
    xout = _mm(act, weight("w_down"), res=xin, scale=0.5, name=tag + "_down", tm=1024, tn=512, tk=512, cargo=cargo)
    if cargo is not None:
        xout, lands = xout
        landed(lands)
    return xout, (hb, gate, up, act)


def _ffn_bwd(dy, dyb, xin, norm, wgu, wd, saved, tag, scatter):
    hb, gate, up, act = saved
    da = _mm(dyb, wd, mode="nt", scale=0.5, out_dtype=BF16, name=tag + "_dact", tm=1024, tn=512)
    d_wd = _mm(act, dyb, mode="tn", scale=0.5, out_dtype=BF16, name=tag + "_dwd", tm=512, tn=512, tk=4096)
    cargo = scatter(tag + "_w_down", d_wd)
    dgu = _swiglu_bwd(da, gate, up, name=tag + "_dswiglu")
    d_wgu, lands = _mm(hb, dgu, mode="tn", out_dtype=BF16, name=tag + "_dwgu", tm=512, tn=512, tk=4096, cargo=cargo)
    scatter.done(lands)
    cargo = scatter(tag + "_w_gate_up", d_wgu)
    dh, lands = _mm(dgu, wgu, mode="nt", out_dtype=BF16, name=tag + "_dh", tm=1024, tn=512, tk=1024, cargo=cargo)
    scatter.done(lands)
    dx, dxb, d_norm = _rms_bwd(xin, dh, norm, dy, name=tag + "_dnorm")
    return dx, dxb, d_norm


def kernel(x, mem, ffn1_norm, ffn1_w_gate_up, ffn1_w_down, mix_norm, mem_norm, w_in, b_forget, pool_w, pool_scale, w_pool_up, fox_q_norm, fox_k_norm, w_fox_o, w_mem_kv, mem_q_norm, mem_k_norm, w_mem_o, w_out, ffn2_norm, ffn2_w_gate_up, ffn2_w_down, loss_target, m_ffn1_norm, m_ffn1_w_gate_up, m_ffn1_w_down, m_mix_norm, m_mem_norm, m_w_in, m_b_forget, m_pool_w, m_pool_scale, m_w_pool_up, m_fox_q_norm, m_fox_k_norm, m_w_fox_o, m_w_mem_kv, m_mem_q_norm, m_mem_k_norm, m_w_mem_o, m_w_out, m_ffn2_norm, m_ffn2_w_gate_up, m_ffn2_w_down, v_ffn1_norm, v_ffn1_w_gate_up, v_ffn1_w_down, v_mix_norm, v_mem_norm, v_w_in, v_b_forget, v_pool_w, v_pool_scale, v_w_pool_up, v_fox_q_norm, v_fox_k_norm, v_w_fox_o, v_w_mem_kv, v_mem_q_norm, v_mem_k_norm, v_w_mem_o, v_w_out, v_ffn2_norm, v_ffn2_w_gate_up, v_ffn2_w_down):
    P = dict(locals())
    big = ["ffn1_w_gate_up", "ffn1_w_down", "w_in", "w_pool_up", "w_fox_o", "w_mem_kv", "w_mem_o", "w_out",
           "ffn2_w_gate_up", "ffn2_w_down"]
    kinds = ["col", "row", "col", "col", "col", "row", "col", "row", "col", "row"]
    small = ["ffn1_norm", "mix_norm", "mem_norm", "b_forget", "pool_w", "pool_scale", "fox_q_norm", "fox_k_norm",
             "mem_q_norm", "mem_k_norm", "ffn2_norm"]
    order = ["ffn1_norm", "ffn1_w_gate_up", "ffn1_w_down", "mix_norm", "mem_norm", "w_in", "b_forget", "pool_w",
             "pool_scale", "w_pool_up", "fox_q_norm", "fox_k_norm", "w_fox_o", "w_mem_kv", "mem_q_norm", "mem_k_norm",
             "w_mem_o", "w_out", "ffn2_norm", "ffn2_w_gate_up", "ffn2_w_down"]

    nb, S, D = x.shape
    T = nb * S
    ML = mem.shape[1]
    NF, HD = b_forget.shape[-1], fox_q_norm.shape[-1]
    FW = NF * HD
    MW, MHD = w_mem_o.shape[1], mem_q_norm.shape[-1]
    PG, PD = pool_w.shape[1], pool_w.shape[2]
    PW = PG * PD
    n_in = w_in.shape[-1]
    n_in_pad = _round_up(n_in, LANES)
    in_w = 4 * n_in
    assert HD * 2 == LANES and MHD == LANES and PD == LANES and in_w == PW + 3 * FW + NF + MW + 3 * D

    shards = []
    for nme in big:
        wl = P[nme][0].astype(BF16)
        if nme == "w_in":
            wl = jnp.pad(wl, ((0, 0), (0, n_in_pad - n_in)))
        shards.append(wl)
    shard, kind_of, full = dict(zip(big, shards)), dict(zip(big, kinds)), {}

    def gather(names):
        def landed(lands):
            for nme, l in zip(names, lands):
                full[nme] = _ag_fill(l, shard[nme], kind_of[nme], name="ag_fill_" + nme)
        return _ag_cargo([shard[nme] for nme in names], [kind_of[nme] for nme in names]), landed

    cargo, landed = gather(["ffn1_w_gate_up"])
    landed(_run_cargo(cargo, name="ag_first"))

    x0 = x.reshape(T, D)
    x1, sv1 = _ffn_fwd(x0, ffn1_norm, lambda nme: full["ffn1_" + nme], "ffn1",
                       gather(["ffn1_w_down", "w_out", "w_fox_o", "w_mem_kv", "w_pool_up", "w_mem_o"]),
                       gather(["w_in"]))
    win_o = jnp.concatenate([full["w_in"][:, j * n_in_pad:j * n_in_pad + n_in] for j in range(4)], axis=1)
    o_q, o_k, o_v, o_f = PW, PW + FW, PW + 2 * FW, PW + 3 * FW
    o_qm, o_gate = o_f + NF, o_f + NF + MW
    p_q, p_k, p_v, p_pool, p_qm, p_f = 3 * D, 3 * D + FW, 3 * D + 2 * FW, 3 * D + 3 * FW, 3 * D + 3 * FW + PW, 3 * D + 3 * FW + PW + MW
    inp = _round_up(p_f + LANES, 512)
    win_p = jnp.concatenate([win_o[:, o_gate:], win_o[:, o_q:o_f], win_o[:, :PW], win_o[:, o_qm:o_gate],
                             win_o[:, o_f:o_qm], jnp.zeros((D, inp - p_f - NF), BF16)], axis=1)

    hb2 = _rms_fwd(x1, mix_norm, name="mix_norm")
    cargo, landed = gather(["ffn2_w_down"])
    z, lands = _mm(hb2, win_p, name="in_proj", tm=1024, tn=512, cargo=cargo)
    landed(lands)

    vpool = _pool_fwd(z, p_pool // LANES, pool_w[0], pool_scale, nb, S, name="pool_fwd")
    y_pool = _mm(vpool, full["w_pool_up"], name="pool_up", tm=1024, tn=512)

    b128 = jnp.pad(b_forget, ((0, 0), (0, LANES - NF)))
    c3 = _fox_c(z, p_f // LANES, b128, nb, S, NF, name="fox_c")
    qa, ka, vb = _fox_qk(z, p_q // LANES, p_k // LANES, p_v // LANES, c3, fox_q_norm, fox_k_norm, HD, NF,
                         name="fox_qk")
    cargo, landed = gather(["ffn2_w_gate_up"])
    o_fox, lands = _fox_fwd(qa, ka, vb, nb, S, HD, name="fox_fwd", cargo=cargo)
    landed(lands)
    y_fox = _mm(o_fox, full["w_fox_o"], name="fox_out", tm=1024, tn=512)

    mem2 = mem.reshape(nb * ML, D)
    memn = _rms_fwd(mem2, mem_norm, name="mem_norm")
    kv = _mm(memn, full["w_mem_kv"], name="mem_kv")
    kmn = _headnorm_fwd(kv, 0, MW, mem_k_norm, MHD, name="mem_knorm")
    vmb = kv[:, MW:].astype(BF16)
    qmn = _headnorm_fwd(z, p_qm // LANES, MW, mem_q_norm, MHD, name="mem_qnorm")
    o_mem = _mem_fwd(qmn, kmn, vmb, nb, S, ML, name="mem_fwd")
    y_mem = _mm(o_mem, full["w_mem_o"], name="mem_out", tm=1024, tn=512)

    merged = _merge_fwd(z, y_pool, y_fox, y_mem, name="merge_fwd")
    x2 = _mm(merged, full["w_out"], res=x1, name="out_proj", tm=1024, tn=512)
    x3, sv2 = _ffn_fwd(x2, ffn2_norm, lambda nme: full["ffn2_" + nme], "ffn2", (None, None), (None, None))
    dy3, dy3b, loss_part = _loss(x3, loss_target.reshape(T, D), name="loss")

    g, parts, got = {}, {}, {}

    def scatter(*named_grads):
        names = named_grads[0::2]
        for nme, grad in zip(names, named_grads[1::2]):
            parts[nme] = _rs_pair_sum(grad, kind_of[nme], name="rs_pair_" + nme)
        scatter.names = names
        return _scatter_cargo([parts[nme] for nme in names], [kind_of[nme] for nme in names])

    def scattered(lands):
        got.update(zip(scatter.names, lands))

    scatter.done = scattered
    dx2, dx2b, g["ffn2_norm"] = _ffn_bwd(dy3, dy3b, x2, ffn2_norm, full["ffn2_w_gate_up"], full["ffn2_w_down"], sv2,
                                         "ffn2", scatter)

    dmerged = _mm(dx2b, full["w_out"], mode="nt", name="d_merged", tm=1024, tn=512)
    g["w_out"] = _mm(merged, dx2b, mode="tn", out_dtype=BF16, name="d_w_out", tk=4096)
    dyp, dyf, dym, dgl = _merge_bwd(dmerged, z, y_pool, y_fox, y_mem, name="merge_bwd")

    g["w_pool_up"] = _mm(vpool, dyp, mode="tn", out_dtype=BF16, name="d_w_pool_up", tk=4096)
    dvp = _mm(dyp, full["w_pool_up"], mode="nt", name="d_vpool", tm=1024)
    du_pool, g["pool_w"], g["pool_scale"] = _pool_bwd(z, p_pool // LANES, dvp, pool_w[0], pool_scale, nb, S,
                                                      name="pool_bwd")

    g["w_fox_o"] = _mm(o_fox, dyf, mode="tn", out_dtype=BF16, name="d_w_fox_o", tk=4096)
    do_fox = _mm(dyf, full["w_fox_o"], mode="nt", out_dtype=BF16, name="d_o_fox", tm=1024)
    dqa, dka, dvf, dcs = _fox_bwd(qa, ka, vb, do_fox, nb, S, HD, name="fox_bwd")
    dq, dgq = _headnorm_bwd(z, p_q // LANES, FW, dqa, fox_q_norm, HD, HD ** -0.5, True, BF16, name="fox_dq")
    dk, dgk = _headnorm_bwd(z, p_k // LANES, FW, dka, fox_k_norm, HD, 1.0, True, BF16, name="fox_dk")
    g["fox_q_norm"], g["fox_k_norm"] = _fold_gain(dgq, HD), _fold_gain(dgk, HD)
    dc = -dcs[:, :, :LANES // HD, :].reshape(nb, NF, S).transpose(0, 2, 1).reshape(T, NF)
    df, db = _fox_c_bwd(jnp.pad(dc, ((0, 0), (0, LANES - NF))), z, p_f // LANES, b128, nb, S, name="fox_c_bwd")
    g["b_forget"] = db[:, :NF]

    g["w_mem_o"] = _mm(o_mem, dym, mode="tn", out_dtype=BF16, name="d_w_mem_o", tk=4096)
    do_mem = _mm(dym, full["w_mem_o"], mode="nt", out_dtype=BF16, name="d_o_mem", tm=1024)
    dqmn, dkmn, dvm = _mem_bwd(qmn, kmn, vmb, do_mem, nb, S, ML, name="mem_bwd")
    dqm, g["mem_q_norm"] = _headnorm_bwd(z, p_qm // LANES, MW, dqmn, mem_q_norm, MHD, 1.0, False, BF16, name="mem_dq")
    dkm, g["mem_k_norm"] = _headnorm_bwd(kv, 0, MW, dkmn, mem_k_norm, MHD, 1.0, False, BF16, name="mem_dk")
    dkv = jnp.concatenate([dkm, dvm.astype(BF16)], axis=1)
    g["w_mem_kv"] = _mm(memn, dkv, mode="tn", out_dtype=BF16, name="d_w_mem_kv")
    dmemn = _mm(dkv, full["w_mem_kv"], mode="nt", name="d_memn")
    _, _, g["mem_norm"] = _rms_bwd(mem2, dmemn, mem_norm, jnp.zeros_like(mem2), name="mem_dnorm")

    dz = jnp.concatenate([dgl, dq, dk, dvf.astype(BF16), du_pool, dqm, df.astype(BF16),
                          jnp.zeros((T, inp - p_f - LANES), BF16)], axis=1)
    cargo = scatter(*sum([[nme, g[nme]] for nme in ("w_out", "w_pool_up", "w_fox_o", "w_mem_o", "w_mem_kv")], []))
    dwin_p, lands = _mm(hb2, dz, mode="tn", out_dtype=BF16, name="d_w_in", tk=4096, cargo=cargo)
    scattered(lands)
    dwin_o = jnp.concatenate([dwin_p[:, p_pool:p_qm], dwin_p[:, p_q:p_pool], dwin_p[:, p_f:p_f + NF],
                              dwin_p[:, p_qm:p_f], dwin_p[:, :p_q]], axis=1)
    zpad = jnp.zeros((D, n_in_pad - n_in), BF16)
    cargo = scatter("w_in", jnp.concatenate(sum([[dwin_o[:, j * n_in:(j + 1) * n_in], zpad] for j in range(4)], []),
                                            axis=1))
    dh2, lands = _mm(dz, win_p, mode="nt", out_dtype=BF16, name="d_h2", tm=1024, tn=512, tk=512, cargo=cargo)
    scattered(lands)
    dx1, dx1b, g["mix_norm"] = _rms_bwd(x1, dh2, mix_norm, dx2, name="mix_dnorm")

    dx0, _, g["ffn1_norm"] = _ffn_bwd(dx1, dx1b, x0, ffn1_norm, full["ffn1_w_gate_up"], full["ffn1_w_down"], sv1,
                                      "ffn1", scatter)

    halves = {nme: _rs_sum_join(parts[nme], got[nme], kind_of[nme], name="rs_join_" + nme) for nme in big}
    gfull = {}

    gpack, sizes = _pack_small([g[nme].reshape(P[nme].shape) for nme in small] + [loss_part[:, :1]])
    gsum = _unpack_small(_all_reduce_small(gpack), sizes)
    loss = gsum[-1].reshape(())
    for nme, a in zip(small, gsum[:-1]):
        gfull[nme] = a

    delta, new_m, new_v = {}, {}, {}
    for nme in big:
        shp = P[nme].shape
        two = lambda a: a.reshape(shp[-2:])
        gfull[nme], delta[nme], new_m[nme], new_v[nme] = [
            a.reshape(shp) for a in _adamw_join(two(P[nme]), *halves[nme], two(P["m_" + nme]), two(P["v_" + nme]),
                                                name="adamw_" + nme)]
    wpack, _ = _pack_small([P[nme] for nme in small])
    mpack, _ = _pack_small([P["m_" + nme] for nme in small])
    vpack, _ = _pack_small([P["v_" + nme] for nme in small])
    gpack2, ssz = _pack_small([gfull[nme] for nme in small])
    for dct, pk in zip((delta, new_m, new_v), _adamw(wpack, gpack2, mpack, vpack, name="adamw_small")):
        for nme, a in zip(small, _unpack_small(pk, ssz)):
            dct[nme] = a

    grad_x = dx0.reshape(nb, S, D)
    return (loss, grad_x, *[gfull[nme] for nme in order], *[delta[nme] for nme in order],
            *[new_m[nme] for nme in order], *[new_v[nme] for nme in order])
```

```python
import functools

import jax
import jax.numpy as jnp
from jax import lax
from jax.experimental import pallas as pl
from jax.experimental.pallas import tpu as pltpu

F32 = jnp.float32
BF16 = jnp.bfloat16
MESH = pl.DeviceIdType.MESH
EPS = 1e-6
POOL_WINDOWS = (2, 4, 8, 16)
ADAM_LR, ADAM_B1, ADAM_B2, ADAM_EPS, ADAM_WD, ADAM_STEP = 0.001, 0.9, 0.999, 1e-08, 0.01, 10
LANES = 128
VMEM_LIMIT = 56 * 1024 * 1024
NEG = -1e30


def _pc(body, *, name, cargo=None, **kw):
    if cargo is not None:
        body, kw = _load_cargo(body, cargo, kw)
    return pl.pallas_call(body, name=name, **kw)


def _cp(*sem):
    return pltpu.CompilerParams(dimension_semantics=sem, vmem_limit_bytes=VMEM_LIMIT)


def _tile(n, pref, mult=LANES):
    if n <= pref:
        return n
    t = (pref // mult) * mult
    while t >= mult:
        if n % t == 0:
            return t
        t -= mult
    return n


def _round_up(n, m):
    return (n + m - 1) // m * m


_DIMS = {"nn": (((1,), (0,)), ((), ())), "nt": (((1,), (1,)), ((), ())), "tn": (((0,), (0,)), ((), ()))}


def _split_cargo(res, cargo):
    if cargo is None:
        return res
    nco = len(cargo.out_shapes)
    own = res[:len(res) - nco]
    return (own[0] if len(own) == 1 else own), list(res[len(res) - nco:])


def _mm(a, b, *, name, mode="nn", out_dtype=F32, scale=1.0, res=None, tm=512, tn=512, tk=2048, cargo=None):
    if mode == "nn":
        (M, K), (K2, N) = a.shape, b.shape
    elif mode == "nt":
        (M, K), (N, K2) = a.shape, b.shape
    else:
        (K, M), (K2, N) = a.shape, b.shape
    assert K == K2, (name, a.shape, b.shape)
    tm, tn, tk = _tile(M, tm, 8 if M % LANES else LANES), _tile(N, tn), _tile(K, tk)
    nk = K // tk
    dims = _DIMS[mode]
    a_spec = (pl.BlockSpec((tk, tm), lambda i, j, k: (k, i)) if mode == "tn"
              else pl.BlockSpec((tm, tk), lambda i, j, k: (i, k)))
    b_spec = (pl.BlockSpec((tn, tk), lambda i, j, k: (j, k)) if mode == "nt"
              else pl.BlockSpec((tk, tn), lambda i, j, k: (k, j)))
    o_spec = pl.BlockSpec((tm, tn), lambda i, j, k: (i, j))
    has_res = res is not None

    def body(*refs):
        a_ref, b_ref = refs[:2]
        r_ref = refs[2] if has_res else None
        o_ref = refs[3] if has_res else refs[2]

        def finish(v):
            v = v * scale
            if has_res:
                v = r_ref[...] + v
            o_ref[...] = v.astype(o_ref.dtype)

        prod = lax.dot_general(a_ref[...], b_ref[...], dims, preferred_element_type=F32)
        if nk == 1:
            finish(prod)
            return
        acc, k = refs[-1], pl.program_id(2)

        @pl.when(k == 0)
        def _():
            acc[...] = prod

        @pl.when(k > 0)
        def _():
            acc[...] += prod

        @pl.when(k == nk - 1)
        def _():
            finish(acc[...])

    out = _pc(body, name=name, cargo=cargo, out_shape=jax.ShapeDtypeStruct((M, N), out_dtype),
              grid=(M // tm, N // tn, nk),
              in_specs=[a_spec, b_spec] + ([o_spec] if has_res else []), out_specs=o_spec,
              scratch_shapes=[] if nk == 1 else [pltpu.VMEM((tm, tn), F32)],
              compiler_params=_cp("parallel", "parallel", "arbitrary"))(
                  *((a, b, res) if has_res else (a, b)), *(cargo.arrays if cargo else ()))
    return _split_cargo(out, cargo)


def _gateup(h, wgu, *, name, tm=512, tn=512, cargo=None):
    M, K = h.shape
    Fh = wgu.shape[1] // 2
    tm, tn = _tile(M, tm), _tile(Fh, tn)
    nj = Fh // tn

    def body(h_ref, wg_ref, wu_ref, g_ref, u_ref, a_ref):
        hv = h_ref[...]
        g = jnp.dot(hv, wg_ref[...], preferred_element_type=F32)
        u = jnp.dot(hv, wu_ref[...], preferred_element_type=F32)
        g_ref[...] = g.astype(BF16)
        u_ref[...] = u.astype(BF16)
        a_ref[...] = (g * jax.nn.sigmoid(g) * u).astype(BF16)

    o_spec = pl.BlockSpec((tm, tn), lambda i, j: (i, j))
    shp = jax.ShapeDtypeStruct((M, Fh), BF16)
    out = _pc(body, name=name, cargo=cargo, out_shape=(shp, shp, shp), grid=(M // tm, nj),
              in_specs=[pl.BlockSpec((tm, K), lambda i, j: (i, 0)),
                        pl.BlockSpec((K, tn), lambda i, j: (0, j)),
                        pl.BlockSpec((K, tn), lambda i, j: (0, j + nj))],
              out_specs=(o_spec, o_spec, o_spec),
              compiler_params=_cp("parallel", "parallel"))(h, wgu, wgu, *(cargo.arrays if cargo else ()))
    return _split_cargo(out, cargo)


def _rowmap(fn, ins, outs, *, rows, tm, name, accs=()):
    tm = _tile(rows, tm, 8)
    arrs, in_specs = [], []
    for d in ins:
        if d[0] == "row":
            _, a, w, cb = d
            w = a.shape[1] if w is None else w
            in_specs.append(pl.BlockSpec((tm, w), functools.partial(lambda i, cb: (i, cb), cb=cb)))
        else:
            a = d[1]
            in_specs.append(pl.BlockSpec(a.shape, functools.partial(lambda i, nd: (0,) * nd, nd=a.ndim)))
        arrs.append(a)
    out_shape = [jax.ShapeDtypeStruct((rows, w), dt) for w, dt in outs]
    out_specs = [pl.BlockSpec((tm, w), lambda i: (i, 0)) for w, _ in outs]
    for shp, dt in accs:
        out_shape.append(jax.ShapeDtypeStruct(shp, dt))
        out_specs.append(pl.BlockSpec(shp, functools.partial(lambda i, nd: (0,) * nd, nd=len(shp))))
    nin, nout, nacc = len(ins), len(outs), len(accs)

    def body(*refs):
        res = fn(*[r[...] for r in refs[:nin]])
        res = res if isinstance(res, (tuple, list)) else (res,)
        for r, v in zip(refs[nin:nin + nout], res[:nout]):
            r[...] = v.astype(r.dtype)
        if nacc:
            acc_refs = refs[nin + nout:]

            @pl.when(pl.program_id(0) == 0)
            def _():
                for r in acc_refs:
                    r[...] = jnp.zeros_like(r)

            for r, v in zip(acc_refs, res[nout:]):
                r[...] += v

    out = _pc(body, name=name, out_shape=tuple(out_shape), grid=(rows // tm,), in_specs=in_specs,
              out_specs=tuple(out_specs), compiler_params=_cp("arbitrary" if nacc else "parallel"))(*arrs)
    return out[0] if len(out) == 1 else out


def _rms_fwd(x, g, *, name):
    def fn(xv, gv):
        return xv * lax.rsqrt(jnp.mean(xv * xv, axis=-1, keepdims=True) + EPS) * gv
    return _rowmap(fn, [("row", x, None, 0), ("bc", g)], [(x.shape[1], BF16)], rows=x.shape[0], tm=256, name=name)


def _rms_bwd(x, dh, g, dres, *, name):
    D = x.shape[1]

    def fn(xv, dhv, gv, drv):
        r = lax.rsqrt(jnp.mean(xv * xv, axis=-1, keepdims=True) + EPS)
        xh = xv * r
        dhf = dhv.astype(F32)
        dxn = dhf * gv
        dx = drv + r * (dxn - xh * jnp.mean(dxn * xh, axis=-1, keepdims=True))
        return dx, dx, jnp.sum(dhf * xh, axis=0, keepdims=True)

    return _rowmap(fn, [("row", x, None, 0), ("row", dh, None, 0), ("bc", g), ("row", dres, None, 0)],
                   [(D, F32), (D, BF16)], rows=x.shape[0], tm=256, name=name, accs=[((1, D), F32)])


def _swiglu_bwd(da, gate, up, *, name):
    Fh = gate.shape[1]

    def fn(dav, gv, uv):
        d, g, u = dav.astype(F32), gv.astype(F32), uv.astype(F32)
        s = jax.nn.sigmoid(g)
        return jnp.concatenate([d * u * (s * (1.0 + g * (1.0 - s))), d * (g * s)], axis=-1)

    return _rowmap(fn, [("row", da, None, 0), ("row", gate, None, 0), ("row", up, None, 0)],
                   [(2 * Fh, BF16)], rows=da.shape[0], tm=128, name=name)


def _loss(y, tgt, *, name):
    D = y.shape[1]

    def fn(yv, tv):
        e = yv - tv
        d = e * (1.0 / D)
        part = 0.5 * jnp.sum(jnp.sum(e * e, axis=-1, keepdims=True) * (1.0 / D), axis=0, keepdims=True)
        return d, d, jnp.broadcast_to(part, (1, LANES))

    return _rowmap(fn, [("row", y, None, 0), ("row", tgt, None, 0)], [(D, F32), (D, BF16)],
                   rows=y.shape[0], tm=256, name=name, accs=[((1, LANES), F32)])


def _merge_fwd(z, yp, yf, ym, *, name):
    D = yp.shape[1]

    def fn(gp, gf, gm, a, b, c):
        return jax.nn.sigmoid(gp) * a + jax.nn.sigmoid(gf) * b + jax.nn.sigmoid(gm) * c

    return _rowmap(fn, [("row", z, D, 0), ("row", z, D, 1), ("row", z, D, 2),
                        ("row", yp, None, 0), ("row", yf, None, 0), ("row", ym, None, 0)],
                   [(D, BF16)], rows=yp.shape[0], tm=128, name=name)


def _merge_bwd(dm, z, yp, yf, ym, *, name):
    D = yp.shape[1]

    def fn(d, gp, gf, gm, a, b, c):
        outs, dgl = [], []
        for gl, yv in ((gp, a), (gf, b), (gm, c)):
            s = jax.nn.sigmoid(gl)
            outs.append(d * s)
            dgl.append(d * yv * (s * (1.0 - s)))
        return (*outs, jnp.concatenate(dgl, axis=-1))

    return _rowmap(fn, [("row", dm, None, 0), ("row", z, D, 0), ("row", z, D, 1), ("row", z, D, 2),
                        ("row", yp, None, 0), ("row", yf, None, 0), ("row", ym, None, 0)],
                   [(D, BF16), (D, BF16), (D, BF16), (3 * D, BF16)], rows=yp.shape[0], tm=128, name=name)


def _adamw(w, g, m, v, *, name):
    C = w.shape[1]

    def fn(wv, gv, mv, vv):
        m2 = ADAM_B1 * mv + (1.0 - ADAM_B1) * gv
        v2 = ADAM_B2 * vv + (1.0 - ADAM_B2) * (gv * gv)
        m_hat = m2 / (1.0 - ADAM_B1 ** ADAM_STEP)
        v_hat = v2 / (1.0 - ADAM_B2 ** ADAM_STEP)
        return -ADAM_LR * (m_hat / (jnp.sqrt(v_hat) + ADAM_EPS) + ADAM_WD * wv), m2, v2

    tm = max(8, (262144 // C) // 8 * 8)
    return _rowmap(fn, [("row", a, None, 0) for a in (w, g, m, v)], [(C, F32)] * 3,
                   rows=w.shape[0], tm=tm, name=name)


def _seg_mean(v, hd):
    if hd == LANES:
        return jnp.mean(v, axis=-1, keepdims=True)
    lo = lax.broadcasted_iota(jnp.int32, v.shape, 1) < hd
    s0 = jnp.sum(jnp.where(lo, v, 0.0), axis=-1, keepdims=True)
    s1 = jnp.sum(jnp.where(lo, 0.0, v), axis=-1, keepdims=True)
    return jnp.where(lo, s0, s1) * (1.0 / hd)


def _gain128(g, hd):
    g = g.reshape(1, hd).astype(F32)
    return g if hd == LANES else jnp.concatenate([g, g], axis=-1)


def _headnorm_fwd(x, cb0, width, gain, hd, *, name):
    rows = x.shape[0]
    tm = _tile(rows, 512, 8)
    g128 = _gain128(gain, hd)

    def body(x_ref, g_ref, o_ref):
        xv = x_ref[...]
        o_ref[...] = (xv * lax.rsqrt(_seg_mean(xv * xv, hd) + EPS) * g_ref[...]).astype(BF16)

    return _pc(body, name=name, out_shape=jax.ShapeDtypeStruct((rows, width), BF16),
               grid=(rows // tm, width // LANES),
               in_specs=[pl.BlockSpec((tm, LANES), lambda i, j: (i, cb0 + j)),
                         pl.BlockSpec((1, LANES), lambda i, j: (0, 0))],
               out_specs=pl.BlockSpec((tm, LANES), lambda i, j: (i, j)),
               compiler_params=_cp("parallel", "parallel"))(x, g128)


def _headnorm_bwd(x, cb0, width, dy, gain, hd, post, spread, out_dtype, *, name):
    rows = x.shape[0]
    tm = _tile(rows, 512, 8)
    g128 = _gain128(gain, hd)
    dyw = 2 * LANES if spread else LANES

    def body(x_ref, dy_ref, g_ref, dx_ref, dg_ref):
        xv = x_ref[...]
        dyv = dy_ref[...]
        if spread:
            dyv = jnp.concatenate([dyv[:, :hd], dyv[:, LANES:LANES + hd]], axis=-1)
        dyv = dyv * post
        r = lax.rsqrt(_seg_mean(xv * xv, hd) + EPS)
        xh = xv * r
        dxn = dyv * g_ref[...]
        dx_ref[...] = (r * (dxn - xh * _seg_mean(dxn * xh, hd))).astype(dx_ref.dtype)

        @pl.when((pl.program_id(0) == 0) & (pl.program_id(1) == 0))
        def _():
            dg_ref[...] = jnp.zeros_like(dg_ref)

        dg_ref[...] += jnp.sum(dyv * xh, axis=0, keepdims=True)

    return _pc(body, name=name,
               out_shape=(jax.ShapeDtypeStruct((rows, width), out_dtype), jax.ShapeDtypeStruct((1, LANES), F32)),
               grid=(rows // tm, width // LANES),
               in_specs=[pl.BlockSpec((tm, LANES), lambda i, j: (i, cb0 + j)),
                         pl.BlockSpec((tm, dyw), lambda i, j: (i, j)),
                         pl.BlockSpec((1, LANES), lambda i, j: (0, 0))],
               out_specs=(pl.BlockSpec((tm, LANES), lambda i, j: (i, j)),
                          pl.BlockSpec((1, LANES), lambda i, j: (0, 0))),
               compiler_params=_cp("arbitrary", "arbitrary"))(x, dy, g128)


def _fold_gain(dg, hd):
    return dg if hd == LANES else dg[:, :hd] + dg[:, hd:]


def _shift_down(v, k, row):
    return jnp.where(row >= k, pltpu.roll(v, k, 0), 0.0)


def _shift_up(v, k, row, S):
    return jnp.where(row < S - k, pltpu.roll(v, S - k, 0), 0.0)


def _window_sum(v, g, row, shift):
    s, outs = v, []
    for k in (1, 2, 4, 8):
        s = s + shift(s, k)
        outs.append(s)
    return jnp.where(g == 0, outs[0], jnp.where(g == 1, outs[1], jnp.where(g == 2, outs[2], outs[3])))


def _pool_fwd(z, cb0, pool_w, pool_scale, nb, S, *, name):
    PG, PD = pool_w.shape[0], pool_w.shape[1]
    assert PD == LANES and PG <= len(POOL_WINDOWS)
    pw = pool_w.astype(BF16)

    def body(u_ref, w_ref, s_ref, o_ref):
        g = pl.program_id(0)
        u = u_ref[...]
        row = lax.broadcasted_iota(jnp.int32, u.shape, 0)
        cnt = jnp.minimum(row + 1, jnp.left_shift(2, g)).astype(F32)
        pooled = _window_sum(u, g, row, lambda v, k: _shift_down(v, k, row)) / cnt
        mixed = jnp.dot((pooled - u).astype(BF16), w_ref[...], preferred_element_type=F32)
        o_ref[...] = (mixed * s_ref[...]).astype(BF16)

    return _pc(body, name=name, out_shape=jax.ShapeDtypeStruct((nb * S, PG * PD), BF16), grid=(PG, nb),
               in_specs=[pl.BlockSpec((S, PD), lambda g, b: (b, cb0 + g)),
                         pl.BlockSpec((None, PD, PD), lambda g, b: (g, 0, 0)),
                         pl.BlockSpec((1, PD), lambda g, b: (0, g))],
               out_specs=pl.BlockSpec((S, PD), lambda g, b: (b, g)),
               compiler_params=_cp("parallel", "parallel"))(z, pw, pool_scale)


def _pool_bwd(z, cb0, dv, pool_w, pool_scale, nb, S, *, name):
    PG, PD = pool_w.shape[0], pool_w.shape[1]
    pw = pool_w.astype(BF16)

    def body(u_ref, dv_ref, w_ref, s_ref, du_ref, dw_ref, ds_ref):
        g, b = pl.program_id(0), pl.program_id(1)
        u, dvv, w = u_ref[...], dv_ref[...], w_ref[...]
        row = lax.broadcasted_iota(jnp.int32, u.shape, 0)
        cnt = jnp.minimum(row + 1, jnp.left_shift(2, g)).astype(F32)
        diff = (_window_sum(u, g, row, lambda v, k: _shift_down(v, k, row)) / cnt - u).astype(BF16)
        mixed = jnp.dot(diff, w, preferred_element_type=F32)
        dmixed = (dvv * s_ref[...]).astype(BF16)
        ddiff = lax.dot_general(dmixed, w, _DIMS["nt"], preferred_element_type=F32)
        du_ref[...] = (_window_sum(ddiff / cnt, g, row, lambda v, k: _shift_up(v, k, row, S)) - ddiff).astype(BF16)

        @pl.when(b == 0)
        def _():
            dw_ref[...] = jnp.zeros_like(dw_ref)
            ds_ref[...] = jnp.zeros_like(ds_ref)

        dw_ref[...] += lax.dot_general(diff, dmixed, _DIMS["tn"], preferred_element_type=F32)
        ds_ref[...] += jnp.sum(dvv * mixed, axis=0, keepdims=True)

    return _pc(body, name=name,
               out_shape=(jax.ShapeDtypeStruct((nb * S, PG * PD), BF16), jax.ShapeDtypeStruct((PG, PD, PD), F32),
                          jax.ShapeDtypeStruct((1, PG * PD), F32)),
               grid=(PG, nb),
               in_specs=[pl.BlockSpec((S, PD), lambda g, b: (b, cb0 + g)),
                         pl.BlockSpec((S, PD), lambda g, b: (b, g)),
                         pl.BlockSpec((None, PD, PD), lambda g, b: (g, 0, 0)),
                         pl.BlockSpec((1, PD), lambda g, b: (0, g))],
               out_specs=(pl.BlockSpec((S, PD), lambda g, b: (b, g)),
                          pl.BlockSpec((None, PD, PD), lambda g, b: (g, 0, 0)),
                          pl.BlockSpec((1, PD), lambda g, b: (0, g))),
               compiler_params=_cp("parallel", "arbitrary"))(z, dv, pw, pool_scale)


def _split3(c):
    hi = c.astype(BF16)
    r1 = c - hi.astype(F32)
    mid = r1.astype(BF16)
    lo = (r1 - mid.astype(F32)).astype(BF16)
    return hi, mid, lo


def _log_sigmoid(v):
    return jnp.minimum(v, 0.0) - jnp.log(1.0 + jnp.exp(-jnp.abs(v)))


def _fox_c(z, cb, b128, nb, S, NF, *, name):
    def body(f_ref, b_ref, hi_ref, mid_ref, lo_ref):
        v = f_ref[...] + b_ref[...]
        lane = lax.broadcasted_iota(jnp.int32, v.shape, 1)
        row = lax.broadcasted_iota(jnp.int32, v.shape, 0)
        c = jnp.where(lane < NF, _log_sigmoid(v), 0.0)
        k = 1
        while k < S:
            c = c + _shift_down(c, k, row)
            k *= 2
        hi, mid, lo = _split3(c)
        hi_ref[...], mid_ref[...], lo_ref[...] = hi, mid, lo

    shp = jax.ShapeDtypeStruct((nb * S, LANES), BF16)
    spec = pl.BlockSpec((S, LANES), lambda i: (i, 0))
    return _pc(body, name=name, out_shape=(shp, shp, shp), grid=(nb,),
               in_specs=[pl.BlockSpec((S, LANES), lambda i: (i, cb)), pl.BlockSpec((1, LANES), lambda i: (0, 0))],
               out_specs=(spec, spec, spec), compiler_params=_cp("parallel"))(z, b128)


def _fox_c_bwd(dc, z, cb, b128, nb, S, *, name):
    def body(dc_ref, f_ref, b_ref, df_ref, db_ref):
        d = dc_ref[...]
        row = lax.broadcasted_iota(jnp.int32, d.shape, 0)
        k = 1
        while k < S:
            d = d + _shift_up(d, k, row, S)
            k *= 2
        df = d * jax.nn.sigmoid(-(f_ref[...] + b_ref[...]))
        df_ref[...] = df

        @pl.when(pl.program_id(0) == 0)
        def _():
            db_ref[...] = jnp.zeros_like(db_ref)

        db_ref[...] += jnp.sum(df, axis=0, keepdims=True)

    return _pc(body, name=name,
               out_shape=(jax.ShapeDtypeStruct((nb * S, LANES), F32), jax.ShapeDtypeStruct((1, LANES), F32)),
               grid=(nb,),
               in_specs=[pl.BlockSpec((S, LANES), lambda i: (i, 0)), pl.BlockSpec((S, LANES), lambda i: (i, cb)),
                         pl.BlockSpec((1, LANES), lambda i: (0, 0))],
               out_specs=(pl.BlockSpec((S, LANES), lambda i: (i, 0)), pl.BlockSpec((1, LANES), lambda i: (0, 0))),
               compiler_params=_cp("arbitrary"))(dc, z, b128)


def _fox_qk(z, cbq, cbk, cbv, c3, gq, gk, HD, NF, *, name):
    T = z.shape[0]
    HP = NF * HD // LANES
    tm = _tile(T, 512, 8)
    scale = HD ** -0.5
    gq128, gk128 = _gain128(gq, HD), _gain128(gk, HD)

    def body(q_ref, k_ref, v_ref, hi_ref, mid_ref, lo_ref, gq_ref, gk_ref, qa_ref, ka_ref, vb_ref):
        hp = pl.program_id(1)
        q, k = q_ref[...], k_ref[...]
        qn = q * lax.rsqrt(_seg_mean(q * q, HD) + EPS) * (gq_ref[...] * scale)
        kn = k * lax.rsqrt(_seg_mean(k * k, HD) + EPS) * gk_ref[...]
        vb_ref[...] = v_ref[...].astype(BF16)
        c3v = jnp.concatenate([hi_ref[...], mid_ref[...], lo_ref[...]], axis=-1)
        r = lax.broadcasted_iota(jnp.int32, (3 * LANES, LANES - HD), 0)
        cc = lax.broadcasted_iota(jnp.int32, (3 * LANES, LANES - HD), 1)
        lane = lax.broadcasted_iota(jnp.int32, (tm, LANES - HD), 1)
        for hh in range(LANES // HD):
            head = hp * (LANES // HD) + hh
            sel_q = jnp.where((cc < 3) & (r == head + LANES * cc), 1.0, 0.0).astype(BF16)
            sel_k = jnp.where((cc >= 3) & (cc < 6) & (r == head + LANES * (cc - 3)), 1.0, 0.0).astype(BF16)
            qaug = jnp.dot(c3v, sel_q, preferred_element_type=F32) + jnp.where((lane >= 3) & (lane < 6), 1.0, 0.0)
            kaug = jnp.where(lane < 3, 1.0, 0.0) - jnp.dot(c3v, sel_k, preferred_element_type=F32)
            sl = slice(hh * HD, (hh + 1) * HD)
            qa_ref[:, hh * LANES:(hh + 1) * LANES] = jnp.concatenate([qn[:, sl], qaug], axis=-1).astype(BF16)
            ka_ref[:, hh * LANES:(hh + 1) * LANES] = jnp.concatenate([kn[:, sl], kaug], axis=-1).astype(BF16)

    aw = (LANES // HD) * LANES
    blk = lambda cb: pl.BlockSpec((tm, LANES), functools.partial(lambda i, j, cb: (i, cb + j), cb=cb))
    cspec = pl.BlockSpec((tm, LANES), lambda i, j: (i, 0))
    gspec = pl.BlockSpec((1, LANES), lambda i, j: (0, 0))
    return _pc(body, name=name,
               out_shape=(jax.ShapeDtypeStruct((T, HP * aw), BF16), jax.ShapeDtypeStruct((T, HP * aw), BF16),
                          jax.ShapeDtypeStruct((T, NF * HD), BF16)),
               grid=(T // tm, HP),
               in_specs=[blk(cbq), blk(cbk), blk(cbv), cspec, cspec, cspec, gspec, gspec],
               out_specs=(pl.BlockSpec((tm, aw), lambda i, j: (i, j)), pl.BlockSpec((tm, aw), lambda i, j: (i, j)),
                          pl.BlockSpec((tm, LANES), lambda i, j: (i, j))),
               compiler_params=_cp("parallel", "parallel"))(z, z, z, *c3, gq128, gk128)


def _fox_probs(qa, ka, q0):
    s = lax.dot_general(qa, ka, _DIMS["nt"], preferred_element_type=F32)
    row = lax.broadcasted_iota(jnp.int32, s.shape, 0) + q0
    col = lax.broadcasted_iota(jnp.int32, s.shape, 1)
    s = jnp.where(col <= row, s, NEG)
    e = jnp.exp(s - jnp.max(s, axis=-1, keepdims=True))
    return e / jnp.sum(e, axis=-1, keepdims=True)


def _fox_fwd(qa, ka, vb, nb, S, HD, *, name, tq=256, cargo=None):
    T = qa.shape[0]
    nh = LANES // HD
    aw = nh * LANES
    HP = qa.shape[1] // aw
    tq = _tile(S, tq, 8)
    nq = S // tq

    def body(qa_ref, ka_ref, v_ref, o_ref):
        for k in range(nq):
            @pl.when(pl.program_id(2) == k)
            def _(k=k):
                L = (k + 1) * tq
                outs = []
                v = v_ref[0:L, :]
                for hh in range(nh):
                    al = slice(hh * LANES, (hh + 1) * LANES)
                    p = _fox_probs(qa_ref[:, al], ka_ref[0:L, al], k * tq)
                    outs.append(jnp.dot(p.astype(BF16), v[:, hh * HD:(hh + 1) * HD], preferred_element_type=F32))
                o_ref[...] = jnp.concatenate(outs, axis=-1).astype(BF16)

    out = _pc(body, name=name, cargo=cargo, out_shape=jax.ShapeDtypeStruct((T, HP * LANES), BF16),
              grid=(nb, HP, nq),
              in_specs=[pl.BlockSpec((tq, aw), lambda b, h, i: (b * nq + i, h)),
                        pl.BlockSpec((S, aw), lambda b, h, i: (b, h)),
                        pl.BlockSpec((S, LANES), lambda b, h, i: (b, h))],
              out_specs=pl.BlockSpec((tq, LANES), lambda b, h, i: (b * nq + i, h)),
              compiler_params=_cp("parallel", "parallel", "arbitrary"))(qa, ka, vb, *(cargo.arrays if cargo else ()))
    return _split_cargo(out, cargo)


def _fox_bwd(qa, ka, vb, do, nb, S, HD, *, name, tq=256):
    T = qa.shape[0]
    nh = LANES // HD
    aw = nh * LANES
    HP = qa.shape[1] // aw
    tq = _tile(S, tq, 8)
    nq = S // tq

    def body(qa_ref, ka_ref, v_ref, do_ref, dqa_ref, dka_ref, dv_ref, dcs_ref):
        i = pl.program_id(2)

        @pl.when(i == 0)
        def _():
            dka_ref[...] = jnp.zeros_like(dka_ref)
            dv_ref[...] = jnp.zeros_like(dv_ref)
            dcs_ref[...] = jnp.zeros_like(dcs_ref)

        for k in range(nq):
            @pl.when(i == k)
            def _(k=k):
                L = (k + 1) * tq
                dkas, dvs, css = [], [], []
                v, dob = v_ref[0:L, :], do_ref[...]
                for hh in range(nh):
                    al = slice(hh * LANES, (hh + 1) * LANES)
                    hl = slice(hh * HD, (hh + 1) * HD)
                    qav, kav, vv, dov = qa_ref[:, al], ka_ref[0:L, al], v[:, hl], dob[:, hl]
                    p = _fox_probs(qav, kav, k * tq)
                    dp = lax.dot_general(dov, vv, _DIMS["nt"], preferred_element_type=F32)
                    dsf = p * (dp - jnp.sum(p * dp, axis=-1, keepdims=True))
                    css.append(jnp.sum(dsf, axis=0, keepdims=True))
                    ds = dsf.astype(BF16)
                    dqa_ref[:, al] = jnp.dot(ds, kav, preferred_element_type=F32)
                    dkas.append(lax.dot_general(ds, qav, _DIMS["tn"], preferred_element_type=F32))
                    dvs.append(lax.dot_general(p.astype(BF16), dov, _DIMS["tn"], preferred_element_type=F32))
                dka_ref[0:L, :] += jnp.concatenate(dkas, axis=-1)
                dv_ref[0:L, :] += jnp.concatenate(dvs, axis=-1)
                dcs_ref[:, 0:L] += jnp.concatenate(css + [jnp.zeros((8 - nh, L), F32)], axis=0)

    return _pc(body, name=name,
               out_shape=(jax.ShapeDtypeStruct((T, HP * aw), F32), jax.ShapeDtypeStruct((T, HP * aw), F32),
                          jax.ShapeDtypeStruct((T, HP * LANES), F32), jax.ShapeDtypeStruct((nb, HP, 8, S), F32)),
               grid=(nb, HP, nq),
               in_specs=[pl.BlockSpec((tq, aw), lambda b, h, i: (b * nq + i, h)),
                         pl.BlockSpec((S, aw), lambda b, h, i: (b, h)),
                         pl.BlockSpec((S, LANES), lambda b, h, i: (b, h)),
                         pl.BlockSpec((tq, LANES), lambda b, h, i: (b * nq + i, h))],
               out_specs=(pl.BlockSpec((tq, aw), lambda b, h, i: (b * nq + i, h)),
                          pl.BlockSpec((S, aw), lambda b, h, i: (b, h)),
                          pl.BlockSpec((S, LANES), lambda b, h, i: (b, h)),
                          pl.BlockSpec((None, None, 8, S), lambda b, h, i: (b, h, 0, 0))),
               compiler_params=_cp("parallel", "parallel", "arbitrary"))(qa, ka, vb, do)


def _mem_probs(q, k, scale):
    s = lax.dot_general(q, k, _DIMS["nt"], preferred_element_type=F32) * scale
    e = jnp.exp(s - jnp.max(s, axis=-1, keepdims=True))
    return e / jnp.sum(e, axis=-1, keepdims=True)


def _mem_fwd(q, k, v, nb, S, ML, *, name, tq=512):
    T, MW = q.shape
    nh = MW // LANES
    tq = _tile(S, tq, 8)
    nq = S // tq
    scale = LANES ** -0.5

    def body(q_ref, k_ref, v_ref, o_ref):
        for h in range(nh):
            sl = slice(h * LANES, (h + 1) * LANES)
            p = _mem_probs(q_ref[:, sl], k_ref[:, sl], scale)
            o_ref[:, sl] = jnp.dot(p.astype(BF16), v_ref[:, sl], preferred_element_type=F32).astype(BF16)

    return _pc(body, name=name, out_shape=jax.ShapeDtypeStruct((T, MW), BF16), grid=(nb, nq),
               in_specs=[pl.BlockSpec((tq, MW), lambda b, i: (b * nq + i, 0)),
                         pl.BlockSpec((ML, MW), lambda b, i: (b, 0)), pl.BlockSpec((ML, MW), lambda b, i: (b, 0))],
               out_specs=pl.BlockSpec((tq, MW), lambda b, i: (b * nq + i, 0)),
               compiler_params=_cp("parallel", "arbitrary"))(q, k, v)


def _mem_bwd(q, k, v, do, nb, S, ML, *, name, tq=512):
    T, MW = q.shape
    nh = MW // LANES
    tq = _tile(S, tq, 8)
    nq = S // tq
    scale = LANES ** -0.5

    def body(q_ref, k_ref, v_ref, do_ref, dq_ref, dk_ref, dv_ref):
        i = pl.program_id(1)

        @pl.when(i == 0)
        def _():
            dk_ref[...] = jnp.zeros_like(dk_ref)
            dv_ref[...] = jnp.zeros_like(dv_ref)

        for h in range(nh):
            sl = slice(h * LANES, (h + 1) * LANES)
            qv, kv, vv, dov = q_ref[:, sl], k_ref[:, sl], v_ref[:, sl], do_ref[:, sl]
            p = _mem_probs(qv, kv, scale)
            dp = lax.dot_general(dov, vv, _DIMS["nt"], preferred_element_type=F32)
            ds = (p * (dp - jnp.sum(p * dp, axis=-1, keepdims=True)) * scale).astype(BF16)
            dq_ref[:, sl] = jnp.dot(ds, kv, preferred_element_type=F32)
            dk_ref[:, sl] += lax.dot_general(ds, qv, _DIMS["tn"], preferred_element_type=F32)
            dv_ref[:, sl] += lax.dot_general(p.astype(BF16), dov, _DIMS["tn"], preferred_element_type=F32)

    kvspec = pl.BlockSpec((ML, MW), lambda b, i: (b, 0))
    qspec = pl.BlockSpec((tq, MW), lambda b, i: (b * nq + i, 0))
    return _pc(body, name=name,
               out_shape=(jax.ShapeDtypeStruct((T, MW), F32), jax.ShapeDtypeStruct((nb * ML, MW), F32),
                          jax.ShapeDtypeStruct((nb * ML, MW), F32)),
               grid=(nb, nq), in_specs=[qspec, kvspec, kvspec, qspec], out_specs=(qspec, kvspec, kvspec),
               compiler_params=_cp("parallel", "arbitrary"))(q, k, v, do)


def _place():
    x, y, c = lax.axis_index("x"), lax.axis_index("y"), lax.axis_index("c")
    chips = [(1 - x, y), (x, 1 - y), (1 - x, 1 - y)]
    return x, y, c, chips


def _shard_win(ref, kind, R, C, j, h=None):
    if kind == "col":
        rows = pl.ds(0, R) if h is None else pl.ds(pl.multiple_of(h * (R // 2), 16), R // 2)
        return ref.at[rows, pl.ds(pl.multiple_of(j * C, LANES), C)]
    if h is None:
        return ref.at[pl.ds(pl.multiple_of(j * R, 16), R), :]
    return ref.at[pl.ds(pl.multiple_of(j * R + h * (R // 2), 16), R // 2), :]


def _any_specs(n):
    return [pl.BlockSpec(memory_space=pl.ANY)] * n


class _Cargo:
    def __init__(self, arrays, out_shapes, ncopies, copies):
        self.arrays, self.out_shapes, self.ncopies, self.copies = list(arrays), list(out_shapes), ncopies, copies


def _load_cargo(body, cargo, kw):
    as_list = lambda v: list(v) if isinstance(v, (tuple, list)) else [v]
    grid = kw.get("grid", ())
    in_specs, out_specs, out_shape = as_list(kw["in_specs"]), as_list(kw["out_specs"]), as_list(kw["out_shape"])
    scratch = list(kw.get("scratch_shapes", ()))
    nin, nout, nci, nco, nscr = len(in_specs), len(out_specs), len(cargo.arrays), len(cargo.out_shapes), len(scratch)

    def loaded(*refs):
        ins, cin = refs[:nin], refs[nin:nin + nci]
        outs, cout = refs[nin + nci:nin + nci + nout], refs[nin + nci + nout:nin + nci + nout + nco]
        scr, (ssem, rsem) = refs[nin + nci + nout + nco:-2], refs[-2:]
        first = last = True
        for a, g in enumerate(grid):
            first = first & (pl.program_id(a) == 0)
            last = last & (pl.program_id(a) == g - 1)

        def start():
            for cp in cargo.copies(cin, cout, ssem, rsem, False):
                cp.start()

        def wait():
            for cp in cargo.copies(cin, cout, ssem, rsem, True):
                cp.wait_send()
                cp.wait_recv()

        if grid:
            pl.when(first)(start)
        else:
            start()
        body(*ins, *outs, *scr)
        if grid:
            pl.when(last)(wait)
        else:
            wait()

    kw = dict(kw, in_specs=in_specs + _any_specs(nci), out_specs=tuple(out_specs + _any_specs(nco)),
              out_shape=tuple(out_shape + cargo.out_shapes),
              scratch_shapes=scratch + [pltpu.SemaphoreType.DMA((cargo.ncopies,))] * 2)
    if grid:
        kw["compiler_params"] = _cp(*["arbitrary"] * len(grid))
    return loaded, kw


def _run_cargo(cargo, *, name):
    res = _pc(lambda: None, name=name, cargo=cargo, in_specs=[], out_specs=[], out_shape=[])(*cargo.arrays)
    return list(res)


def _full_shape(dims, kind):
    return (dims[0], 4 * dims[1]) if kind == "col" else (4 * dims[0], dims[1])


def _ag_copy(shard, full, kind, j_src, chip, c, ssem, rsem):
    R, C = shard.shape
    src = shard.at[pl.ds(pl.multiple_of(c * (R // 2), 16), R // 2), :]
    return pltpu.make_async_remote_copy(src, _shard_win(full, kind, R, C, j_src, c), ssem, rsem,
                                        device_id=(*chip, c), device_id_type=MESH)


def _ag_cargo(shards, kinds):
    def copies(ins, outs, ssem, rsem, waiting):
        x, y, c, chips = _place()
        return [_ag_copy(ins[w], outs[w], kinds[w], 2 * chip[0] + chip[1] if waiting else 2 * x + y, chip, c,
                         ssem.at[3 * w + q], rsem.at[3 * w + q])
                for w in range(len(shards)) for q, chip in enumerate(chips)]

    return _Cargo(shards, [jax.ShapeDtypeStruct(_full_shape(s.shape, k), BF16) for s, k in zip(shards, kinds)],
                  3 * len(shards), copies)


def _scatter_cargo(parts, kinds):
    pdims = [((p.shape[0], p.shape[1] // 4) if k == "col" else (p.shape[0] // 4, p.shape[1]))
             for p, k in zip(parts, kinds)]

    def copies(ins, outs, ssem, rsem, waiting):
        x, y, c, chips = _place()
        return [pltpu.make_async_remote_copy(_shard_win(ins[w], kinds[w], *pdims[w], 2 * cx + cy), outs[w].at[q],
                                             ssem.at[3 * w + q], rsem.at[3 * w + q],
                                             device_id=(cx, cy, c), device_id_type=MESH)
                for w in range(len(parts)) for q, (cx, cy) in enumerate(chips)]

    return _Cargo(parts, [jax.ShapeDtypeStruct((3,) + d, BF16) for d in pdims], 3 * len(parts), copies)


def _ag_fill(full, shard, kind, *, name):
    R, C = shard.shape
    RH = R // 2
    tm = _tile(RH, max(16, (3 * 2 ** 19 // C) // 16 * 16), 16)
    nbh = RH // tm
    n, nown = 3 * nbh, 2 * nbh
    xyc = jnp.stack([lax.axis_index("x"), lax.axis_index("y"), lax.axis_index("c")]).astype(jnp.int32)

    def chip_of(q, x, y):
        keep_x, keep_y = q % 2, (2 - q) // 2
        return 2 * ((1 - x) + keep_x * (2 * x - 1)) + (1 - y) + keep_y * (2 * y - 1)

    def window(ref, j, row0):
        if kind == "col":
            return ref.at[pl.ds(pl.multiple_of(row0, 16), tm), pl.ds(pl.multiple_of(j * C, LANES), C)]
        return ref.at[pl.ds(pl.multiple_of(j * R + row0, 16), tm), :]

    def body(p_ref, send_ref, own_ref, out_ref, land, ssem, rsem, wsem):
        i = pl.program_id(0)
        x, y = lax.axis_index("x"), lax.axis_index("y")
        c, peer = _pair_peer()
        own_cp = pltpu.make_async_copy(own_ref, window(out_ref, 2 * x + y, jnp.minimum(i, nown - 1) * tm), wsem.at[1])

        @pl.when(i < nown)
        def _():
            own_cp.start()

        def consume(land, ps):
            b = i - 1
            cp = pltpu.make_async_copy(land.at[ps], window(out_ref, chip_of(b // nbh, x, y),
                                                           (1 - c) * RH + (b % nbh) * tm), wsem.at[0])
            cp.start()
            cp.wait()

        _exchange_step(i, n, send_ref, land, ssem, rsem, None, peer, consume)

        @pl.when(i < nown)
        def _():
            own_cp.wait()

    now = lambda i: jnp.minimum(i, n - 1)

    def send_map(i, p):
        q, b = now(i) // nbh, now(i) % nbh
        j = chip_of(q, p[0], p[1])
        return (p[2] * nbh + b, j) if kind == "col" else ((j * 2 + p[2]) * nbh + b, 0)

    grid_spec = pltpu.PrefetchScalarGridSpec(
        num_scalar_prefetch=1, grid=(n + 1,),
        in_specs=[pl.BlockSpec((tm, C), send_map),
                  pl.BlockSpec((tm, C), lambda i, p: (jnp.minimum(i, nown - 1), 0))],
        out_specs=pl.BlockSpec(memory_space=pl.ANY),
        scratch_shapes=[pltpu.VMEM((n, tm, C), BF16), pltpu.SemaphoreType.DMA((n,)), pltpu.SemaphoreType.DMA((n,)),
                        pltpu.SemaphoreType.DMA((2,))])
    return _pc(body, name=name, out_shape=jax.ShapeDtypeStruct(full.shape, BF16), grid_spec=grid_spec,
               input_output_aliases={1: 0}, compiler_params=_cp("arbitrary"))(xyc, full, shard)


def _pair_peer():
    x, y, c = lax.axis_index("x"), lax.axis_index("y"), lax.axis_index("c")
    return c, (x, y, 1 - c)


def _exchange_step(i, n, src_ref, land, ssem, rsem, credit, peer, consume):
    def rdma(slot):
        return pltpu.make_async_remote_copy(src_ref, land.at[slot], ssem.at[slot], rsem.at[slot],
                                            device_id=peer, device_id_type=MESH)

    reuse = credit is not None
    slot = i % 2 if reuse else jnp.minimum(i, n - 1)

    @pl.when(i < n)
    def _():
        if reuse:
            @pl.when(i >= 2)
            def _():
                pl.semaphore_wait(credit, 1)

        rdma(slot).start()

    @pl.when(i >= 1)
    def _():
        ps = (i - 1) % 2 if reuse else i - 1
        rdma(ps).wait_recv()
        consume(land, ps)

        if reuse:
            @pl.when(i - 1 < n - 2)
            def _():
                pl.semaphore_signal(credit, 1, device_id=peer, device_id_type=MESH)

    @pl.when(i < n)
    def _():
        rdma(slot).wait_send()


def _xchg_scratch(tm, C, dtype):
    return [pltpu.VMEM((2, tm, C), dtype), pltpu.SemaphoreType.DMA((2,)), pltpu.SemaphoreType.DMA((2,)),
            pltpu.SemaphoreType.REGULAR]


def _rs_pair_sum(g, kind, *, name):
    K, N = g.shape
    NS, RH = (1, K // 2) if kind == "col" else (4, K // 8)
    tm = _tile(RH, max(16, (3 * 2 ** 19 // N) // 16 * 16), 16)
    if NS * (RH // tm) < 2:
        tm = RH // 2
    nb = RH // tm
    n = NS * nb
    cvec = lax.axis_index("c").reshape(1).astype(jnp.int32)

    def body(c_ref, send_ref, own_ref, out_ref, land, ssem, rsem, credit):
        _, peer = _pair_peer()

        def consume(land, ps):
            out_ref[...] = (own_ref[...].astype(F32) + land[ps].astype(F32)).astype(BF16)

        _exchange_step(pl.program_id(0), n, send_ref, land, ssem, rsem, credit, peer, consume)

    now = lambda i: jnp.minimum(i, n - 1)
    lag = lambda i: jnp.maximum(i - 1, 0)
    grid_spec = pltpu.PrefetchScalarGridSpec(
        num_scalar_prefetch=1, grid=(n + 1,),
        in_specs=[pl.BlockSpec((tm, N), lambda i, c: ((now(i) // nb * 2 + 1 - c[0]) * nb + now(i) % nb, 0)),
                  pl.BlockSpec((tm, N), lambda i, c: ((lag(i) // nb * 2 + c[0]) * nb + lag(i) % nb, 0))],
        out_specs=pl.BlockSpec((tm, N), lambda i, c: (lag(i), 0)),
        scratch_shapes=_xchg_scratch(tm, N, BF16))
    return _pc(body, name=name, out_shape=jax.ShapeDtypeStruct((NS * RH, N), BF16), grid_spec=grid_spec,
               compiler_params=_cp("arbitrary"))(cvec, g, g)


def _rs_sum_join(part, got, kind, *, name):
    _, hr, hc = got.shape
    tm = _tile(hr, max(16, (3 * 2 ** 18 // hc) // 16 * 16), 16)
    if hr // tm < 2:
        tm = hr // 2
    nb = hr // tm
    jvec = (2 * lax.axis_index("x") + lax.axis_index("y")).reshape(1).astype(jnp.int32)

    def body(j_ref, p_ref, a_ref, b_ref, c_ref, mine_ref, theirs_ref, sbuf, land, ssem, rsem, credit):
        i = pl.program_id(0)
        _, peer = _pair_peer()

        @pl.when(i < nb)
        def _():
            v = ((p_ref[...].astype(F32) + a_ref[...].astype(F32)) + b_ref[...].astype(F32)) + c_ref[...].astype(F32)
            mine_ref[...] = v
            sbuf[...] = v

        def consume(land, ps):
            theirs_ref[...] = land[ps]

        _exchange_step(i, nb, sbuf, land, ssem, rsem, credit, peer, consume)

    now = lambda i: jnp.minimum(i, nb - 1)
    lag = lambda i: jnp.maximum(i - 1, 0)
    own_spec = (pl.BlockSpec((tm, hc), lambda i, j: (now(i), j[0])) if kind == "col"
                else pl.BlockSpec((tm, hc), lambda i, j: (j[0] * nb + now(i), 0)))
    got_spec = lambda q: pl.BlockSpec((None, tm, hc), functools.partial(lambda i, j, q: (q, now(i), 0), q=q))
    grid_spec = pltpu.PrefetchScalarGridSpec(
        num_scalar_prefetch=1, grid=(nb + 1,),
        in_specs=[own_spec, got_spec(0), got_spec(1), got_spec(2)],
        out_specs=(pl.BlockSpec((tm, hc), lambda i, j: (now(i), 0)), pl.BlockSpec((tm, hc), lambda i, j: (lag(i), 0))),
        scratch_shapes=[pltpu.VMEM((tm, hc), F32)] + _xchg_scratch(tm, hc, F32))
    shp = jax.ShapeDtypeStruct((hr, hc), F32)
    return _pc(body, name=name, out_shape=(shp, shp), grid_spec=grid_spec,
               compiler_params=_cp("arbitrary"))(jvec, part, got, got, got)


def _adamw_join(w, mine, theirs, m, v, *, name):
    R2, wc = w.shape
    hr, hc = mine.shape
    assert R2 == 2 * hr and hc >= wc
    tm = _tile(hr, max(8, (2 ** 18 // hc) // 8 * 8), 8)

    def body(w_ref, a_ref, b_ref, m_ref, v_ref, g_ref, d_ref, m2_ref, v2_ref):
        gv = jnp.where(pl.program_id(0) == lax.axis_index("c"), a_ref[...], b_ref[...])[:, :wc]
        m2 = ADAM_B1 * m_ref[...] + (1.0 - ADAM_B1) * gv
        v2 = ADAM_B2 * v_ref[...] + (1.0 - ADAM_B2) * (gv * gv)
        m_hat = m2 / (1.0 - ADAM_B1 ** ADAM_STEP)
        v_hat = v2 / (1.0 - ADAM_B2 ** ADAM_STEP)
        g_ref[...] = gv
        d_ref[...] = -ADAM_LR * (m_hat / (jnp.sqrt(v_hat) + ADAM_EPS) + ADAM_WD * w_ref[...])
        m2_ref[...] = m2
        v2_ref[...] = v2

    wspec = pl.BlockSpec((None, tm, wc), lambda h, i: (h, i, 0))
    gspec = pl.BlockSpec((tm, hc), lambda h, i: (i, 0))
    shp = jax.ShapeDtypeStruct((2, hr, wc), F32)
    v3 = lambda a: a.reshape(2, hr, wc)
    outs = _pc(body, name=name, out_shape=(shp,) * 4, grid=(2, hr // tm),
               in_specs=[wspec, gspec, gspec, wspec, wspec], out_specs=(wspec,) * 4,
               compiler_params=_cp("parallel", "parallel"))(v3(w), mine, theirs, v3(m), v3(v))
    return [o.reshape(R2, wc) for o in outs]


def _all_reduce_small(pack):
    R = pack.shape[0]

    def body(p_ref, o_ref, buf, send, recv):
        x, y, c, _ = _place()
        me = 4 * x + 2 * y + c
        buf[0] = p_ref[...]
        cps = []
        for r in range(1, 8):
            peer = (x if r & 4 == 0 else 1 - x, y if r & 2 == 0 else 1 - y, c if r & 1 == 0 else 1 - c)
            cp = pltpu.make_async_remote_copy(p_ref, buf.at[r], send.at[r - 1], recv.at[r - 1],
                                              device_id=peer, device_id_type=MESH)
            cp.start()
            cps.append(cp)
        for cp in cps:
            cp.wait()
        acc = buf[jnp.bitwise_xor(me, 0)]
        for k in range(1, 8):
            acc = acc + buf[jnp.bitwise_xor(me, k)]
        o_ref[...] = acc

    return _pc(body, name="allreduce_small", out_shape=jax.ShapeDtypeStruct((R, LANES), F32),
               in_specs=[pl.BlockSpec(memory_space=pltpu.VMEM)], out_specs=pl.BlockSpec(memory_space=pltpu.VMEM),
               scratch_shapes=[pltpu.VMEM((8, R, LANES), F32), pltpu.SemaphoreType.DMA((7,)),
                               pltpu.SemaphoreType.DMA((7,))])(pack)


def _pack_small(arrs):
    parts, sizes = [], []
    for a in arrs:
        f = a.reshape(-1).astype(F32)
        n = _round_up(f.shape[0], 8 * LANES)
        parts.append(jnp.pad(f, (0, n - f.shape[0])).reshape(n // LANES, LANES))
        sizes.append((a.shape, f.shape[0], n // LANES))
    return jnp.concatenate(parts, axis=0), sizes


def _unpack_small(pack, sizes):
    out, r0 = [], 0
    for shape, cnt, rows in sizes:
        out.append(pack[r0:r0 + rows].reshape(-1)[:cnt].reshape(shape))
        r0 += rows
    return out


def _ffn_fwd(xin, norm, weight, tag, gather_on_gateup, gather_on_down):
    hb = _rms_fwd(xin, norm, name=tag + "_norm")
    cargo, landed = gather_on_gateup
    res = _gateup(hb, weight("w_gate_up"), name=tag + "_gateup", cargo=cargo)
    if cargo is not None:
        res, lands = res
        landed(lands)
    gate, up, act = res
    cargo, landed = gather_on_down
    xout = _mm(act, weight("w_down"), res=xin, scale=0.5, name=tag + "_down", tm=512, tn=512, tk=8192, cargo=cargo)
    if cargo is not None:
        xout, lands = xout
        landed(lands)
    return xout, (hb, gate, up, act)


def _ffn_bwd(dy, dyb, xin, norm, wgu, wd, saved, tag, scatter):
    hb, gate, up, act = saved
    da = _mm(dyb, wd, mode="nt", scale=0.5, out_dtype=BF16, name=tag + "_dact", tm=1024, tn=512)
    d_wd = _mm(act, dyb, mode="tn", scale=0.5, out_dtype=BF16, name=tag + "_dwd", tm=512, tn=512, tk=4096)
    cargo = scatter(tag + "_w_down", d_wd)
    dgu = _swiglu_bwd(da, gate, up, name=tag + "_dswiglu")
    d_wgu, lands = _mm(hb, dgu, mode="tn", out_dtype=BF16, name=tag + "_dwgu", tm=512, tn=512, tk=4096, cargo=cargo)
    scatter.done(lands)
    cargo = scatter(tag + "_w_gate_up", d_wgu)
    dh, lands = _mm(dgu, wgu, mode="nt", out_dtype=BF16, name=tag + "_dh", tm=1024, tn=512, tk=2816, cargo=cargo)
    scatter.done(lands)
    dx, dxb, d_norm = _rms_bwd(xin, dh, norm, dy, name=tag + "_dnorm")
    return dx, dxb, d_norm


def kernel(x, mem, ffn1_norm, ffn1_w_gate_up, ffn1_w_down, mix_norm, mem_norm, w_in, b_forget, pool_w, pool_scale, w_pool_up, fox_q_norm, fox_k_norm, w_fox_o, w_mem_kv, mem_q_norm, mem_k_norm, w_mem_o, w_out, ffn2_norm, ffn2_w_gate_up, ffn2_w_down, loss_target, m_ffn1_norm, m_ffn1_w_gate_up, m_ffn1_w_down, m_mix_norm, m_mem_norm, m_w_in, m_b_forget, m_pool_w, m_pool_scale, m_w_pool_up, m_fox_q_norm, m_fox_k_norm, m_w_fox_o, m_w_mem_kv, m_mem_q_norm, m_mem_k_norm, m_w_mem_o, m_w_out, m_ffn2_norm, m_ffn2_w_gate_up, m_ffn2_w_down, v_ffn1_norm, v_ffn1_w_gate_up, v_ffn1_w_down, v_mix_norm, v_mem_norm, v_w_in, v_b_forget, v_pool_w, v_pool_scale, v_w_pool_up, v_fox_q_norm, v_fox_k_norm, v_w_fox_o, v_w_mem_kv, v_mem_q_norm, v_mem_k_norm, v_w_mem_o, v_w_out, v_ffn2_norm, v_ffn2_w_gate_up, v_ffn2_w_down):
    P = dict(locals())
    big = ["ffn1_w_gate_up", "ffn1_w_down", "w_in", "w_pool_up", "w_fox_o", "w_mem_kv", "w_mem_o", "w_out",
           "ffn2_w_gate_up", "ffn2_w_down"]
    kinds = ["col", "row", "col", "col", "col", "row", "col", "row", "col", "row"]
    small = ["ffn1_norm", "mix_norm", "mem_norm", "b_forget", "pool_w", "pool_scale", "fox_q_norm", "fox_k_norm",
             "mem_q_norm", "mem_k_norm", "ffn2_norm"]
    order = ["ffn1_norm", "ffn1_w_gate_up", "ffn1_w_down", "mix_norm", "mem_norm", "w_in", "b_forget", "pool_w",
             "pool_scale", "w_pool_up", "fox_q_norm", "fox_k_norm", "w_fox_o", "w_mem_kv", "mem_q_norm", "mem_k_norm",
             "w_mem_o", "w_out", "ffn2_norm", "ffn2_w_gate_up", "ffn2_w_down"]

    nb, S, D = x.shape
    T = nb * S
    ML = mem.shape[1]
    NF, HD = b_forget.shape[-1], fox_q_norm.shape[-1]
    FW = NF * HD
    MW, MHD = w_mem_o.shape[1], mem_q_norm.shape[-1]
    PG, PD = pool_w.shape[1], pool_w.shape[2]
    PW = PG * PD
    n_in = w_in.shape[-1]
    n_in_pad = _round_up(n_in, LANES)
    in_w = 4 * n_in
    assert HD * 2 == LANES and MHD == LANES and PD == LANES and in_w == PW + 3 * FW + NF + MW + 3 * D

    shards = []
    for nme in big:
        wl = P[nme][0].astype(BF16)
        if nme == "w_in":
            wl = jnp.pad(wl, ((0, 0), (0, n_in_pad - n_in)))
        shards.append(wl)
    shard, kind_of, full = dict(zip(big, shards)), dict(zip(big, kinds)), {}

    def gather(names):
        def landed(lands):
            for nme, l in zip(names, lands):
                full[nme] = _ag_fill(l, shard[nme], kind_of[nme], name="ag_fill_" + nme)
        return _ag_cargo([shard[nme] for nme in names], [kind_of[nme] for nme in names]), landed

    cargo, landed = gather(["ffn1_w_gate_up"])
    landed(_run_cargo(cargo, name="ag_first"))

    x0 = x.reshape(T, D)
    x1, sv1 = _ffn_fwd(x0, ffn1_norm, lambda nme: full["ffn1_" + nme], "ffn1",
                       gather(["ffn1_w_down", "w_out", "w_fox_o", "w_mem_kv", "w_pool_up", "w_mem_o"]),
                       gather(["w_in"]))
    win_o = jnp.concatenate([full["w_in"][:, j * n_in_pad:j * n_in_pad + n_in] for j in range(4)], axis=1)
    o_q, o_k, o_v, o_f = PW, PW + FW, PW + 2 * FW, PW + 3 * FW
    o_qm, o_gate = o_f + NF, o_f + NF + MW
    p_q, p_k, p_v, p_pool, p_qm, p_f = 3 * D, 3 * D + FW, 3 * D + 2 * FW, 3 * D + 3 * FW, 3 * D + 3 * FW + PW, 3 * D + 3 * FW + PW + MW
    inp = _round_up(p_f + LANES, 512)
    win_p = jnp.concatenate([win_o[:, o_gate:], win_o[:, o_q:o_f], win_o[:, :PW], win_o[:, o_qm:o_gate],
                             win_o[:, o_f:o_qm], jnp.zeros((D, inp - p_f - NF), BF16)], axis=1)

    hb2 = _rms_fwd(x1, mix_norm, name="mix_norm")
    cargo, landed = gather(["ffn2_w_down"])
    z, lands = _mm(hb2, win_p, name="in_proj", tm=1024, tn=512, cargo=cargo)
    landed(lands)

    vpool = _pool_fwd(z, p_pool // LANES, pool_w[0], pool_scale, nb, S, name="pool_fwd")
    y_pool = _mm(vpool, full["w_pool_up"], name="pool_up", tm=1024, tn=512)

    b128 = jnp.pad(b_forget, ((0, 0), (0, LANES - NF)))
    c3 = _fox_c(z, p_f // LANES, b128, nb, S, NF, name="fox_c")
    qa, ka, vb = _fox_qk(z, p_q // LANES, p_k // LANES, p_v // LANES, c3, fox_q_norm, fox_k_norm, HD, NF,
                         name="fox_qk")
    cargo, landed = gather(["ffn2_w_gate_up"])
    o_fox, lands = _fox_fwd(qa, ka, vb, nb, S, HD, name="fox_fwd", cargo=cargo)
    landed(lands)
    y_fox = _mm(o_fox, full["w_fox_o"], name="fox_out", tm=1024, tn=512)

    mem2 = mem.reshape(nb * ML, D)
    memn = _rms_fwd(mem2, mem_norm, name="mem_norm")
    kv = _mm(memn, full["w_mem_kv"], name="mem_kv")
    kmn = _headnorm_fwd(kv, 0, MW, mem_k_norm, MHD, name="mem_knorm")
    vmb = kv[:, MW:].astype(BF16)
    qmn = _headnorm_fwd(z, p_qm // LANES, MW, mem_q_norm, MHD, name="mem_qnorm")
    o_mem = _mem_fwd(qmn, kmn, vmb, nb, S, ML, name="mem_fwd")
    y_mem = _mm(o_mem, full["w_mem_o"], name="mem_out", tm=1024, tn=512)

    merged = _merge_fwd(z, y_pool, y_fox, y_mem, name="merge_fwd")
    x2 = _mm(merged, full["w_out"], res=x1, name="out_proj", tm=1024, tn=512)
    x3, sv2 = _ffn_fwd(x2, ffn2_norm, lambda nme: full["ffn2_" + nme], "ffn2", (None, None), (None, None))
    dy3, dy3b, loss_part = _loss(x3, loss_target.reshape(T, D), name="loss")

    g, parts, got = {}, {}, {}

    def scatter(*named_grads):
        names = named_grads[0::2]
        for nme, grad in zip(names, named_grads[1::2]):
            parts[nme] = _rs_pair_sum(grad, kind_of[nme], name="rs_pair_" + nme)
        scatter.names = names
        return _scatter_cargo([parts[nme] for nme in names], [kind_of[nme] for nme in names])

    def scattered(lands):
        got.update(zip(scatter.names, lands))

    scatter.done = scattered
    dx2, dx2b, g["ffn2_norm"] = _ffn_bwd(dy3, dy3b, x2, ffn2_norm, full["ffn2_w_gate_up"], full["ffn2_w_down"], sv2,
                                         "ffn2", scatter)

    dmerged = _mm(dx2b, full["w_out"], mode="nt", name="d_merged", tm=1024, tn=512)
    g["w_out"] = _mm(merged, dx2b, mode="tn", out_dtype=BF16, name="d_w_out", tk=4096)
    dyp, dyf, dym, dgl = _merge_bwd(dmerged, z, y_pool, y_fox, y_mem, name="merge_bwd")

    g["w_pool_up"] = _mm(vpool, dyp, mode="tn", out_dtype=BF16, name="d_w_pool_up", tk=4096)
    dvp = _mm(dyp, full["w_pool_up"], mode="nt", name="d_vpool", tm=1024)
    du_pool, g["pool_w"], g["pool_scale"] = _pool_bwd(z, p_pool // LANES, dvp, pool_w[0], pool_scale, nb, S,
                                                      name="pool_bwd")

    g["w_fox_o"] = _mm(o_fox, dyf, mode="tn", out_dtype=BF16, name="d_w_fox_o", tk=4096)
    do_fox = _mm(dyf, full["w_fox_o"], mode="nt", out_dtype=BF16, name="d_o_fox", tm=1024)
    dqa, dka, dvf, dcs = _fox_bwd(qa, ka, vb, do_fox, nb, S, HD, name="fox_bwd")
    dq, dgq = _headnorm_bwd(z, p_q // LANES, FW, dqa, fox_q_norm, HD, HD ** -0.5, True, BF16, name="fox_dq")
    dk, dgk = _headnorm_bwd(z, p_k // LANES, FW, dka, fox_k_norm, HD, 1.0, True, BF16, name="fox_dk")
    g["fox_q_norm"], g["fox_k_norm"] = _fold_gain(dgq, HD), _fold_gain(dgk, HD)
    dc = -dcs[:, :, :LANES // HD, :].reshape(nb, NF, S).transpose(0, 2, 1).reshape(T, NF)
    df, db = _fox_c_bwd(jnp.pad(dc, ((0, 0), (0, LANES - NF))), z, p_f // LANES, b128, nb, S, name="fox_c_bwd")
    g["b_forget"] = db[:, :NF]

    g["w_mem_o"] = _mm(o_mem, dym, mode="tn", out_dtype=BF16, name="d_w_mem_o", tk=4096)
    do_mem = _mm(dym, full["w_mem_o"], mode="nt", out_dtype=BF16, name="d_o_mem", tm=1024)
    dqmn, dkmn, dvm = _mem_bwd(qmn, kmn, vmb, do_mem, nb, S, ML, name="mem_bwd")
    dqm, g["mem_q_norm"] = _headnorm_bwd(z, p_qm // LANES, MW, dqmn, mem_q_norm, MHD, 1.0, False, BF16, name="mem_dq")
    dkm, g["mem_k_norm"] = _headnorm_bwd(kv, 0, MW, dkmn, mem_k_norm, MHD, 1.0, False, BF16, name="mem_dk")
    dkv = jnp.concatenate([dkm, dvm.astype(BF16)], axis=1)
    g["w_mem_kv"] = _mm(memn, dkv, mode="tn", out_dtype=BF16, name="d_w_mem_kv")
    dmemn = _mm(dkv, full["w_mem_kv"], mode="nt", name="d_memn")
    _, _, g["mem_norm"] = _rms_bwd(mem2, dmemn, mem_norm, jnp.zeros_like(mem2), name="mem_dnorm")

    dz = jnp.concatenate([dgl, dq, dk, dvf.astype(BF16), du_pool, dqm, df.astype(BF16),
                          jnp.zeros((T, inp - p_f - LANES), BF16)], axis=1)
    cargo = scatter(*sum([[nme, g[nme]] for nme in ("w_out", "w_pool_up", "w_fox_o", "w_mem_o", "w_mem_kv")], []))
    dwin_p, lands = _mm(hb2, dz, mode="tn", out_dtype=BF16, name="d_w_in", tk=4096, cargo=cargo)
    scattered(lands)
    dwin_o = jnp.concatenate([dwin_p[:, p_pool:p_qm], dwin_p[:, p_q:p_pool], dwin_p[:, p_f:p_f + NF],
                              dwin_p[:, p_qm:p_f], dwin_p[:, :p_q]], axis=1)
    zpad = jnp.zeros((D, n_in_pad - n_in), BF16)
    cargo = scatter("w_in", jnp.concatenate(sum([[dwin_o[:, j * n_in:(j + 1) * n_in], zpad] for j in range(4)], []),
                                            axis=1))
    dh2, lands = _mm(dz, win_p, mode="nt", out_dtype=BF16, name="d_h2", tm=1024, tn=512, tk=3584, cargo=cargo)
    scattered(lands)
    dx1, dx1b, g["mix_norm"] = _rms_bwd(x1, dh2, mix_norm, dx2, name="mix_dnorm")

    dx0, _, g["ffn1_norm"] = _ffn_bwd(dx1, dx1b, x0, ffn1_norm, full["ffn1_w_gate_up"], full["ffn1_w_down"], sv1,
                                      "ffn1", scatter)

    halves = {nme: _rs_sum_join(parts[nme], got[nme], kind_of[nme], name="rs_join_" + nme) for nme in big}
    gfull = {}

    gpack, sizes = _pack_small([g[nme].reshape(P[nme].shape) for nme in small] + [loss_part[:, :1]])
    gsum = _unpack_small(_all_reduce_small(gpack), sizes)
    loss = gsum[-1].reshape(())
    for nme, a in zip(small, gsum[:-1]):
        gfull[nme] = a

    delta, new_m, new_v = {}, {}, {}
    for nme in big:
        shp = P[nme].shape
        two = lambda a: a.reshape(shp[-2:])
        gfull[nme], delta[nme], new_m[nme], new_v[nme] = [
            a.reshape(shp) for a in _adamw_join(two(P[nme]), *halves[nme], two(P["m_" + nme]), two(P["v_" + nme]),
                                                name="adamw_" + nme)]
    wpack, _ = _pack_small([P[nme] for nme in small])
    mpack, _ = _pack_small([P["m_" + nme] for nme in small])
    vpack, _ = _pack_small([P["v_" + nme] for nme in small])
    gpack2, ssz = _pack_small([gfull[nme] for nme in small])
    for dct, pk in zip((delta, new_m, new_v), _adamw(wpack, gpack2, mpack, vpack, name="adamw_small")):
        for nme, a in zip(small, _unpack_small(pk, ssz)):
            dct[nme] = a

    grad_x = dx0.reshape(nb, S, D)
    return (loss, grad_x, *[gfull[nme] for nme in order], *[delta[nme] for nme in order],
            *[new_m[nme] for nme in order], *[new_v[nme] for nme in order])
```

```python
import functools

import jax
import jax.numpy as jnp
from jax import lax
from jax.experimental import pallas as pl
from jax.experimental.pallas import tpu as pltpu

F32 = jnp.float32
BF16 = jnp.bfloat16
MESH = pl.DeviceIdType.MESH
EPS = 1e-6
POOL_WINDOWS = (2, 4, 8, 16)
ADAM_LR, ADAM_B1, ADAM_B2, ADAM_EPS, ADAM_WD, ADAM_STEP = 0.001, 0.9, 0.999, 1e-08, 0.01, 10
LANES = 128
VMEM_LIMIT = 56 * 1024 * 1024
NEG = -1e30


def _pc(body, *, name, cargo=None, **kw):
    if cargo is not None:
        body, kw = _load_cargo(body, cargo, kw)
    return pl.pallas_call(body, name=name, **kw)


def _cp(*sem):
    return pltpu.CompilerParams(dimension_semantics=sem, vmem_limit_bytes=VMEM_LIMIT)


def _tile(n, pref, mult=LANES):
    if n <= pref:
        return n
    t = (pref // mult) * mult
    while t >= mult:
        if n % t == 0:
            return t
        t -= mult
    return n


def _round_up(n, m):
    return (n + m - 1) // m * m


_DIMS = {"nn": (((1,), (0,)), ((), ())), "nt": (((1,), (1,)), ((), ())), "tn": (((0,), (0,)), ((), ()))}


def _split_cargo(res, cargo):
    if cargo is None:
        return res
    nco = len(cargo.out_shapes)
    own = res[:len(res) - nco]
    return (own[0] if len(own) == 1 else own), list(res[len(res) - nco:])


def _mm(a, b, *, name, mode="nn", out_dtype=F32, scale=1.0, res=None, tm=512, tn=512, tk=2048, cargo=None,
        halves=None):
    if halves == "a":
        assert mode == "nt"
        (_, M, Kh), (N, K2) = a.shape, b.shape
        K = 2 * Kh
    elif halves == "b":
        assert mode == "tn"
        (K, M), (_, K2, Nh) = a.shape, b.shape
        N = 2 * Nh
    elif mode == "nn":
        (M, K), (K2, N) = a.shape, b.shape
    elif mode == "nt":
        (M, K), (N, K2) = a.shape, b.shape
    else:
        (K, M), (K2, N) = a.shape, b.shape
    assert K == K2, (name, a.shape, b.shape)
    tm = _tile(M, tm, 8 if M % LANES else LANES)
    tn = _tile(N // 2 if halves == "b" else N, tn)
    tk = _tile(K // 2 if halves == "a" else K, tk)
    nk = K // tk
    dims = _DIMS[mode]
    if halves == "a":
        a_spec = pl.BlockSpec((None, tm, tk), lambda i, j, k: (k // (nk // 2), i, k % (nk // 2)))
    elif mode == "tn":
        a_spec = pl.BlockSpec((tk, tm), lambda i, j, k: (k, i))
    else:
        a_spec = pl.BlockSpec((tm, tk), lambda i, j, k: (i, k))
    if halves == "b":
        njh = N // 2 // tn
        b_spec = pl.BlockSpec((None, tk, tn), lambda i, j, k: (j // njh, k, j % njh))
    elif mode == "nt":
        b_spec = pl.BlockSpec((tn, tk), lambda i, j, k: (j, k))
    else:
        b_spec = pl.BlockSpec((tk, tn), lambda i, j, k: (k, j))
    o_spec = pl.BlockSpec((tm, tn), lambda i, j, k: (i, j))
    has_res = res is not None

    def body(*refs):
        a_ref, b_ref = refs[:2]
        r_ref = refs[2] if has_res else None
        o_ref = refs[3] if has_res else refs[2]

        def finish(v):
            v = v * scale
            if has_res:
                v = r_ref[...] + v
            o_ref[...] = v.astype(o_ref.dtype)

        prod = lax.dot_general(a_ref[...], b_ref[...], dims, preferred_element_type=F32)
        if nk == 1:
            finish(prod)
            return
        acc, k = refs[-1], pl.program_id(2)

        @pl.when(k == 0)
        def _():
            acc[...] = prod

        @pl.when(k > 0)
        def _():
            acc[...] += prod

        @pl.when(k == nk - 1)
        def _():
            finish(acc[...])

    out = _pc(body, name=name, cargo=cargo, out_shape=jax.ShapeDtypeStruct((M, N), out_dtype),
              grid=(M // tm, N // tn, nk),
              in_specs=[a_spec, b_spec] + ([o_spec] if has_res else []), out_specs=o_spec,
              scratch_shapes=[] if nk == 1 else [pltpu.VMEM((tm, tn), F32)],
              compiler_params=_cp("parallel", "parallel", "arbitrary"))(
                  *((a, b, res) if has_res else (a, b)), *(cargo.arrays if cargo else ()))
    return _split_cargo(out, cargo)


def _dact_swiglu(dyb, wd, gate, up, *, name, tm=1024, tn=512):
    M, K = dyb.shape
    Fh = wd.shape[0]
    tm, tn = _tile(M, tm), _tile(Fh, tn)

    def body(dy_ref, w_ref, g_ref, u_ref, o_ref):
        d = lax.dot_general(dy_ref[...], w_ref[...], _DIMS["nt"], preferred_element_type=F32) * 0.5
        g, u = g_ref[...].astype(F32), u_ref[...].astype(F32)
        s = jax.nn.sigmoid(g)
        o_ref[0] = (d * u * (s * (1.0 + g * (1.0 - s)))).astype(BF16)
        o_ref[1] = (d * (g * s)).astype(BF16)

    t_spec = pl.BlockSpec((tm, tn), lambda i, j: (i, j))
    return _pc(body, name=name, out_shape=jax.ShapeDtypeStruct((2, M, Fh), BF16), grid=(M // tm, Fh // tn),
               in_specs=[pl.BlockSpec((tm, K), lambda i, j: (i, 0)), pl.BlockSpec((tn, K), lambda i, j: (j, 0)),
                         t_spec, t_spec],
               out_specs=pl.BlockSpec((2, tm, tn), lambda i, j: (0, i, j)),
               compiler_params=_cp("parallel", "parallel"))(dyb, wd, gate, up)


def _gateup(h, wgu, *, name, tm=512, tn=512, cargo=None):
    M, K = h.shape
    Fh = wgu.shape[1] // 2
    tm, tn = _tile(M, tm), _tile(Fh, tn)
    nj = Fh // tn

    def body(h_ref, wg_ref, wu_ref, g_ref, u_ref, a_ref):
        hv = h_ref[...]
        g = jnp.dot(hv, wg_ref[...], preferred_element_type=F32)
        u = jnp.dot(hv, wu_ref[...], preferred_element_type=F32)
        g_ref[...] = g.astype(BF16)
        u_ref[...] = u.astype(BF16)
        a_ref[...] = (g * jax.nn.sigmoid(g) * u).astype(BF16)

    o_spec = pl.BlockSpec((tm, tn), lambda i, j: (i, j))
    shp = jax.ShapeDtypeStruct((M, Fh), BF16)
    out = _pc(body, name=name, cargo=cargo, out_shape=(shp, shp, shp), grid=(M // tm, nj),
              in_specs=[pl.BlockSpec((tm, K), lambda i, j: (i, 0)),
                        pl.BlockSpec((K, tn), lambda i, j: (0, j)),
                        pl.BlockSpec((K, tn), lambda i, j: (0, j + nj))],
              out_specs=(o_spec, o_spec, o_spec),
              compiler_params=_cp("parallel", "parallel"))(h, wgu, wgu, *(cargo.arrays if cargo else ()))
    return _split_cargo(out, cargo)


def _rowmap(fn, ins, outs, *, rows, tm, name, accs=()):
    tm = _tile(rows, tm, 8)
    arrs, in_specs = [], []
    for d in ins:
        if d[0] == "row":
            _, a, w, cb = d
            w = a.shape[1] if w is None else w
            in_specs.append(pl.BlockSpec((tm, w), functools.partial(lambda i, cb: (i, cb), cb=cb)))
        else:
            a = d[1]
            in_specs.append(pl.BlockSpec(a.shape, functools.partial(lambda i, nd: (0,) * nd, nd=a.ndim)))
        arrs.append(a)
    out_shape = [jax.ShapeDtypeStruct((rows, w), dt) for w, dt in outs]
    out_specs = [pl.BlockSpec((tm, w), lambda i: (i, 0)) for w, _ in outs]
    for shp, dt in accs:
        out_shape.append(jax.ShapeDtypeStruct(shp, dt))
        out_specs.append(pl.BlockSpec(shp, functools.partial(lambda i, nd: (0,) * nd, nd=len(shp))))
    nin, nout, nacc = len(ins), len(outs), len(accs)

    def body(*refs):
        res = fn(*[r[...] for r in refs[:nin]])
        res = res if isinstance(res, (tuple, list)) else (res,)
        for r, v in zip(refs[nin:nin + nout], res[:nout]):
            r[...] = v.astype(r.dtype)
        if nacc:
            acc_refs = refs[nin + nout:]

            @pl.when(pl.program_id(0) == 0)
            def _():
                for r in acc_refs:
                    r[...] = jnp.zeros_like(r)

            for r, v in zip(acc_refs, res[nout:]):
                r[...] += v

    out = _pc(body, name=name, out_shape=tuple(out_shape), grid=(rows // tm,), in_specs=in_specs,
              out_specs=tuple(out_specs), compiler_params=_cp("arbitrary" if nacc else "parallel"))(*arrs)
    return out[0] if len(out) == 1 else out


def _rms_fwd(x, g, *, name):
    def fn(xv, gv):
        return xv * lax.rsqrt(jnp.mean(xv * xv, axis=-1, keepdims=True) + EPS) * gv
    return _rowmap(fn, [("row", x, None, 0), ("bc", g)], [(x.shape[1], BF16)], rows=x.shape[0], tm=256, name=name)


def _rms_bwd(x, dh, g, dres, *, name):
    D = x.shape[1]

    def fn(xv, dhv, gv, drv):
        r = lax.rsqrt(jnp.mean(xv * xv, axis=-1, keepdims=True) + EPS)
        xh = xv * r
        dhf = dhv.astype(F32)
        dxn = dhf * gv
        dx = drv + r * (dxn - xh * jnp.mean(dxn * xh, axis=-1, keepdims=True))
        return dx, dx, jnp.sum(dhf * xh, axis=0, keepdims=True)

    return _rowmap(fn, [("row", x, None, 0), ("row", dh, None, 0), ("bc", g), ("row", dres, None, 0)],
                   [(D, F32), (D, BF16)], rows=x.shape[0], tm=256, name=name, accs=[((1, D), F32)])


def _loss(y, tgt, *, name):
    D = y.shape[1]

    def fn(yv, tv):
        e = yv - tv
        d = e * (1.0 / D)
        part = 0.5 * jnp.sum(jnp.sum(e * e, axis=-1, keepdims=True) * (1.0 / D), axis=0, keepdims=True)
        return d, d, jnp.broadcast_to(part, (1, LANES))

    return _rowmap(fn, [("row", y, None, 0), ("row", tgt, None, 0)], [(D, F32), (D, BF16)],
                   rows=y.shape[0], tm=256, name=name, accs=[((1, LANES), F32)])


def _merge_fwd(z, yp, yf, ym, *, name):
    D = yp.shape[1]

    def fn(gp, gf, gm, a, b, c):
        return jax.nn.sigmoid(gp) * a + jax.nn.sigmoid(gf) * b + jax.nn.sigmoid(gm) * c

    return _rowmap(fn, [("row", z, D, 0), ("row", z, D, 1), ("row", z, D, 2),
                        ("row", yp, None, 0), ("row", yf, None, 0), ("row", ym, None, 0)],
                   [(D, BF16)], rows=yp.shape[0], tm=128, name=name)


def _merge_bwd(dm, z, yp, yf, ym, *, name):
    D = yp.shape[1]

    def fn(d, gp, gf, gm, a, b, c):
        outs, dgl = [], []
        for gl, yv in ((gp, a), (gf, b), (gm, c)):
            s = jax.nn.sigmoid(gl)
            outs.append(d * s)
            dgl.append(d * yv * (s * (1.0 - s)))
        return (*outs, jnp.concatenate(dgl, axis=-1))

    return _rowmap(fn, [("row", dm, None, 0), ("row", z, D, 0), ("row", z, D, 1), ("row", z, D, 2),
                        ("row", yp, None, 0), ("row", yf, None, 0), ("row", ym, None, 0)],
                   [(D, BF16), (D, BF16), (D, BF16), (3 * D, BF16)], rows=yp.shape[0], tm=128, name=name)


def _adamw(w, g, m, v, *, name):
    C = w.shape[1]

    def fn(wv, gv, mv, vv):
        m2 = ADAM_B1 * mv + (1.0 - ADAM_B1) * gv
        v2 = ADAM_B2 * vv + (1.0 - ADAM_B2) * (gv * gv)
        m_hat = m2 / (1.0 - ADAM_B1 ** ADAM_STEP)
        v_hat = v2 / (1.0 - ADAM_B2 ** ADAM_STEP)
        return -ADAM_LR * (m_hat / (jnp.sqrt(v_hat) + ADAM_EPS) + ADAM_WD * wv), m2, v2

    tm = max(8, (262144 // C) // 8 * 8)
    return _rowmap(fn, [("row", a, None, 0) for a in (w, g, m, v)], [(C, F32)] * 3,
                   rows=w.shape[0], tm=tm, name=name)


def _seg_mean(v, hd):
    if hd == LANES:
        return jnp.mean(v, axis=-1, keepdims=True)
    lo = lax.broadcasted_iota(jnp.int32, v.shape, 1) < hd
    s0 = jnp.sum(jnp.where(lo, v, 0.0), axis=-1, keepdims=True)
    s1 = jnp.sum(jnp.where(lo, 0.0, v), axis=-1, keepdims=True)
    return jnp.where(lo, s0, s1) * (1.0 / hd)


def _gain128(g, hd):
    g = g.reshape(1, hd).astype(F32)
    return g if hd == LANES else jnp.concatenate([g, g], axis=-1)


def _headnorm_fwd(x, cb0, width, gain, hd, *, name):
    rows = x.shape[0]
    tm = _tile(rows, 512, 8)
    g128 = _gain128(gain, hd)

    def body(x_ref, g_ref, o_ref):
        xv = x_ref[...]
        o_ref[...] = (xv * lax.rsqrt(_seg_mean(xv * xv, hd) + EPS) * g_ref[...]).astype(BF16)

    return _pc(body, name=name, out_shape=jax.ShapeDtypeStruct((rows, width), BF16),
               grid=(rows // tm, width // LANES),
               in_specs=[pl.BlockSpec((tm, LANES), lambda i, j: (i, cb0 + j)),
                         pl.BlockSpec((1, LANES), lambda i, j: (0, 0))],
               out_specs=pl.BlockSpec((tm, LANES), lambda i, j: (i, j)),
               compiler_params=_cp("parallel", "parallel"))(x, g128)


def _headnorm_bwd(x, cb0, width, dy, gain, hd, post, spread, out_dtype, *, name):
    rows = x.shape[0]
    tm = _tile(rows, 512, 8)
    g128 = _gain128(gain, hd)
    dyw = 2 * LANES if spread else LANES

    def body(x_ref, dy_ref, g_ref, dx_ref, dg_ref):
        xv = x_ref[...]
        dyv = dy_ref[...]
        if spread:
            dyv = jnp.concatenate([dyv[:, :hd], dyv[:, LANES:LANES + hd]], axis=-1)
        dyv = dyv * post
        r = lax.rsqrt(_seg_mean(xv * xv, hd) + EPS)
        xh = xv * r
        dxn = dyv * g_ref[...]
        dx_ref[...] = (r * (dxn - xh * _seg_mean(dxn * xh, hd))).astype(dx_ref.dtype)

        @pl.when((pl.program_id(0) == 0) & (pl.program_id(1) == 0))
        def _():
            dg_ref[...] = jnp.zeros_like(dg_ref)

        dg_ref[...] += jnp.sum(dyv * xh, axis=0, keepdims=True)

    return _pc(body, name=name,
               out_shape=(jax.ShapeDtypeStruct((rows, width), out_dtype), jax.ShapeDtypeStruct((1, LANES), F32)),
               grid=(rows // tm, width // LANES),
               in_specs=[pl.BlockSpec((tm, LANES), lambda i, j: (i, cb0 + j)),
                         pl.BlockSpec((tm, dyw), lambda i, j: (i, j)),
                         pl.BlockSpec((1, LANES), lambda i, j: (0, 0))],
               out_specs=(pl.BlockSpec((tm, LANES), lambda i, j: (i, j)),
                          pl.BlockSpec((1, LANES), lambda i, j: (0, 0))),
               compiler_params=_cp("arbitrary", "arbitrary"))(x, dy, g128)


def _fold_gain(dg, hd):
    return dg if hd == LANES else dg[:, :hd] + dg[:, hd:]


def _shift_down(v, k, row):
    return jnp.where(row >= k, pltpu.roll(v, k, 0), 0.0)


def _shift_up(v, k, row, S):
    return jnp.where(row < S - k, pltpu.roll(v, S - k, 0), 0.0)


def _window_sum(v, g, row, shift):
    s, outs = v, []
    for k in (1, 2, 4, 8):
        s = s + shift(s, k)
        outs.append(s)
    return jnp.where(g == 0, outs[0], jnp.where(g == 1, outs[1], jnp.where(g == 2, outs[2], outs[3])))


def _pool_fwd(z, cb0, pool_w, pool_scale, nb, S, *, name):
    PG, PD = pool_w.shape[0], pool_w.shape[1]
    assert PD == LANES and PG <= len(POOL_WINDOWS)
    pw = pool_w.astype(BF16)

    def body(u_ref, w_ref, s_ref, o_ref):
        g = pl.program_id(0)
        u = u_ref[...]
        row = lax.broadcasted_iota(jnp.int32, u.shape, 0)
        cnt = jnp.minimum(row + 1, jnp.left_shift(2, g)).astype(F32)
        pooled = _window_sum(u, g, row, lambda v, k: _shift_down(v, k, row)) / cnt
        mixed = jnp.dot((pooled - u).astype(BF16), w_ref[...], preferred_element_type=F32)
        o_ref[...] = (mixed * s_ref[...]).astype(BF16)

    return _pc(body, name=name, out_shape=jax.ShapeDtypeStruct((nb * S, PG * PD), BF16), grid=(PG, nb),
               in_specs=[pl.BlockSpec((S, PD), lambda g, b: (b, cb0 + g)),
                         pl.BlockSpec((None, PD, PD), lambda g, b: (g, 0, 0)),
                         pl.BlockSpec((1, PD), lambda g, b: (0, g))],
               out_specs=pl.BlockSpec((S, PD), lambda g, b: (b, g)),
               compiler_params=_cp("parallel", "parallel"))(z, pw, pool_scale)


def _pool_bwd(z, cb0, dv, pool_w, pool_scale, nb, S, *, name):
    PG, PD = pool_w.shape[0], pool_w.shape[1]
    pw = pool_w.astype(BF16)

    def body(u_ref, dv_ref, w_ref, s_ref, du_ref, dw_ref, ds_ref):
        g, b = pl.program_id(0), pl.program_id(1)
        u, dvv, w = u_ref[...], dv_ref[...], w_ref[...]
        row = lax.broadcasted_iota(jnp.int32, u.shape, 0)
        cnt = jnp.minimum(row + 1, jnp.left_shift(2, g)).astype(F32)
        diff = (_window_sum(u, g, row, lambda v, k: _shift_down(v, k, row)) / cnt - u).astype(BF16)
        mixed = jnp.dot(diff, w, preferred_element_type=F32)
        dmixed = (dvv * s_ref[...]).astype(BF16)
        ddiff = lax.dot_general(dmixed, w, _DIMS["nt"], preferred_element_type=F32)
        du_ref[...] = (_window_sum(ddiff / cnt, g, row, lambda v, k: _shift_up(v, k, row, S)) - ddiff).astype(BF16)

        @pl.when(b == 0)
        def _():
            dw_ref[...] = jnp.zeros_like(dw_ref)
            ds_ref[...] = jnp.zeros_like(ds_ref)

        dw_ref[...] += lax.dot_general(diff, dmixed, _DIMS["tn"], preferred_element_type=F32)
        ds_ref[...] += jnp.sum(dvv * mixed, axis=0, keepdims=True)

    return _pc(body, name=name,
               out_shape=(jax.ShapeDtypeStruct((nb * S, PG * PD), BF16), jax.ShapeDtypeStruct((PG, PD, PD), F32),
                          jax.ShapeDtypeStruct((1, PG * PD), F32)),
               grid=(PG, nb),
               in_specs=[pl.BlockSpec((S, PD), lambda g, b: (b, cb0 + g)),
                         pl.BlockSpec((S, PD), lambda g, b: (b, g)),
                         pl.BlockSpec((None, PD, PD), lambda g, b: (g, 0, 0)),
                         pl.BlockSpec((1, PD), lambda g, b: (0, g))],
               out_specs=(pl.BlockSpec((S, PD), lambda g, b: (b, g)),
                          pl.BlockSpec((None, PD, PD), lambda g, b: (g, 0, 0)),
                          pl.BlockSpec((1, PD), lambda g, b: (0, g))),
               compiler_params=_cp("parallel", "arbitrary"))(z, dv, pw, pool_scale)


def _split3(c):
    hi = c.astype(BF16)
    r1 = c - hi.astype(F32)
    mid = r1.astype(BF16)
    lo = (r1 - mid.astype(F32)).astype(BF16)
    return hi, mid, lo


def _log_sigmoid(v):
    return jnp.minimum(v, 0.0) - jnp.log(1.0 + jnp.exp(-jnp.abs(v)))


def _fox_c(z, cb, b128, nb, S, NF, *, name):
    def body(f_ref, b_ref, hi_ref, mid_ref, lo_ref):
        v = f_ref[...] + b_ref[...]
        lane = lax.broadcasted_iota(jnp.int32, v.shape, 1)
        row = lax.broadcasted_iota(jnp.int32, v.shape, 0)
        c = jnp.where(lane < NF, _log_sigmoid(v), 0.0)
        k = 1
        while k < S:
            c = c + _shift_down(c, k, row)
            k *= 2
        hi, mid, lo = _split3(c)
        hi_ref[...], mid_ref[...], lo_ref[...] = hi, mid, lo

    shp = jax.ShapeDtypeStruct((nb * S, LANES), BF16)
    spec = pl.BlockSpec((S, LANES), lambda i: (i, 0))
    return _pc(body, name=name, out_shape=(shp, shp, shp), grid=(nb,),
               in_specs=[pl.BlockSpec((S, LANES), lambda i: (i, cb)), pl.BlockSpec((1, LANES), lambda i: (0, 0))],
               out_specs=(spec, spec, spec), compiler_params=_cp("parallel"))(z, b128)


def _fox_c_bwd(dc, z, cb, b128, nb, S, *, name):
    def body(dc_ref, f_ref, b_ref, df_ref, db_ref):
        d = dc_ref[...]
        row = lax.broadcasted_iota(jnp.int32, d.shape, 0)
        k = 1
        while k < S:
            d = d + _shift_up(d, k, row, S)
            k *= 2
        df = d * jax.nn.sigmoid(-(f_ref[...] + b_ref[...]))
        df_ref[...] = df

        @pl.when(pl.program_id(0) == 0)
        def _():
            db_ref[...] = jnp.zeros_like(db_ref)

        db_ref[...] += jnp.sum(df, axis=0, keepdims=True)

    return _pc(body, name=name,
               out_shape=(jax.ShapeDtypeStruct((nb * S, LANES), F32), jax.ShapeDtypeStruct((1, LANES), F32)),
               grid=(nb,),
               in_specs=[pl.BlockSpec((S, LANES), lambda i: (i, 0)), pl.BlockSpec((S, LANES), lambda i: (i, cb)),
                         pl.BlockSpec((1, LANES), lambda i: (0, 0))],
               out_specs=(pl.BlockSpec((S, LANES), lambda i: (i, 0)), pl.BlockSpec((1, LANES), lambda i: (0, 0))),
               compiler_params=_cp("arbitrary"))(dc, z, b128)


def _fox_qk(z, cbq, cbk, cbv, c3, gq, gk, HD, NF, *, name):
    T = z.shape[0]
    HP = NF * HD // LANES
    tm = _tile(T, 512, 8)
    scale = HD ** -0.5
    gq128, gk128 = _gain128(gq, HD), _gain128(gk, HD)

    def body(q_ref, k_ref, v_ref, hi_ref, mid_ref, lo_ref, gq_ref, gk_ref, qa_ref, ka_ref, vb_ref):
        hp = pl.program_id(1)
        q, k = q_ref[...], k_ref[...]
        qn = q * lax.rsqrt(_seg_mean(q * q, HD) + EPS) * (gq_ref[...] * scale)
        kn = k * lax.rsqrt(_seg_mean(k * k, HD) + EPS) * gk_ref[...]
        vb_ref[...] = v_ref[...].astype(BF16)
        c3v = jnp.concatenate([hi_ref[...], mid_ref[...], lo_ref[...]], axis=-1)
        r = lax.broadcasted_iota(jnp.int32, (3 * LANES, LANES - HD), 0)
        cc = lax.broadcasted_iota(jnp.int32, (3 * LANES, LANES - HD), 1)
        lane = lax.broadcasted_iota(jnp.int32, (tm, LANES - HD), 1)
        for hh in range(LANES // HD):
            head = hp * (LANES // HD) + hh
            sel_q = jnp.where((cc < 3) & (r == head + LANES * cc), 1.0, 0.0).astype(BF16)
            sel_k = jnp.where((cc >= 3) & (cc < 6) & (r == head + LANES * (cc - 3)), 1.0, 0.0).astype(BF16)
            qaug = jnp.dot(c3v, sel_q, preferred_element_type=F32) + jnp.where((lane >= 3) & (lane < 6), 1.0, 0.0)
            kaug = jnp.where(lane < 3, 1.0, 0.0) - jnp.dot(c3v, sel_k, preferred_element_type=F32)
            sl = slice(hh * HD, (hh + 1) * HD)
            qa_ref[:, hh * LANES:(hh + 1) * LANES] = jnp.concatenate([qn[:, sl], qaug], axis=-1).astype(BF16)
            ka_ref[:, hh * LANES:(hh + 1) * LANES] = jnp.concatenate([kn[:, sl], kaug], axis=-1).astype(BF16)

    aw = (LANES // HD) * LANES
    blk = lambda cb: pl.BlockSpec((tm, LANES), functools.partial(lambda i, j, cb: (i, cb + j), cb=cb))
    cspec = pl.BlockSpec((tm, LANES), lambda i, j: (i, 0))
    gspec = pl.BlockSpec((1, LANES), lambda i, j: (0, 0))
    return _pc(body, name=name,
               out_shape=(jax.ShapeDtypeStruct((T, HP * aw), BF16), jax.ShapeDtypeStruct((T, HP * aw), BF16),
                          jax.ShapeDtypeStruct((T, NF * HD), BF16)),
               grid=(T // tm, HP),
               in_specs=[blk(cbq), blk(cbk), blk(cbv), cspec, cspec, cspec, gspec, gspec],
               out_specs=(pl.BlockSpec((tm, aw), lambda i, j: (i, j)), pl.BlockSpec((tm, aw), lambda i, j: (i, j)),
                          pl.BlockSpec((tm, LANES), lambda i, j: (i, j))),
               compiler_params=_cp("parallel", "parallel"))(z, z, z, *c3, gq128, gk128)


def _fox_probs(qa, ka, q0):
    s = lax.dot_general(qa, ka, _DIMS["nt"], preferred_element_type=F32)
    row = lax.broadcasted_iota(jnp.int32, s.shape, 0) + q0
    col = lax.broadcasted_iota(jnp.int32, s.shape, 1)
    s = jnp.where(col <= row, s, NEG)
    e = jnp.exp(s - jnp.max(s, axis=-1, keepdims=True))
    return e / jnp.sum(e, axis=-1, keepdims=True)


def _fox_fwd(qa, ka, vb, nb, S, HD, *, name, tq=256, cargo=None):
    T = qa.shape[0]
    nh = LANES // HD
    aw = nh * LANES
    HP = qa.shape[1] // aw
    tq = _tile(S, tq, 8)
    nq = S // tq

    def body(qa_ref, ka_ref, v_ref, o_ref):
        for k in range(nq):
            @pl.when(pl.program_id(2) == k)
            def _(k=k):
                L = (k + 1) * tq
                outs = []
                v = v_ref[0:L, :]
                for hh in range(nh):
                    al = slice(hh * LANES, (hh + 1) * LANES)
                    p = _fox_probs(qa_ref[:, al], ka_ref[0:L, al], k * tq)
                    outs.append(jnp.dot(p.astype(BF16), v[:, hh * HD:(hh + 1) * HD], preferred_element_type=F32))
                o_ref[...] = jnp.concatenate(outs, axis=-1).astype(BF16)

    out = _pc(body, name=name, cargo=cargo, out_shape=jax.ShapeDtypeStruct((T, HP * LANES), BF16),
              grid=(nb, HP, nq),
              in_specs=[pl.BlockSpec((tq, aw), lambda b, h, i: (b * nq + i, h)),
                        pl.BlockSpec((S, aw), lambda b, h, i: (b, h)),
                        pl.BlockSpec((S, LANES), lambda b, h, i: (b, h))],
              out_specs=pl.BlockSpec((tq, LANES), lambda b, h, i: (b * nq + i, h)),
              compiler_params=_cp("parallel", "parallel", "arbitrary"))(qa, ka, vb, *(cargo.arrays if cargo else ()))
    return _split_cargo(out, cargo)


def _fox_bwd(qa, ka, vb, do, nb, S, HD, *, name, tq=256):
    T = qa.shape[0]
    nh = LANES // HD
    aw = nh * LANES
    HP = qa.shape[1] // aw
    tq = _tile(S, tq, 8)
    nq = S // tq

    def body(qa_ref, ka_ref, v_ref, do_ref, dqa_ref, dka_ref, dv_ref, dcs_ref):
        i = pl.program_id(2)

        @pl.when(i == 0)
        def _():
            dka_ref[...] = jnp.zeros_like(dka_ref)
            dv_ref[...] = jnp.zeros_like(dv_ref)
            dcs_ref[...] = jnp.zeros_like(dcs_ref)

        for k in range(nq):
            @pl.when(i == k)
            def _(k=k):
                L = (k + 1) * tq
                dkas, dvs, css = [], [], []
                v, dob = v_ref[0:L, :], do_ref[...]
                for hh in range(nh):
                    al = slice(hh * LANES, (hh + 1) * LANES)
                    hl = slice(hh * HD, (hh + 1) * HD)
                    qav, kav, vv, dov = qa_ref[:, al], ka_ref[0:L, al], v[:, hl], dob[:, hl]
                    p = _fox_probs(qav, kav, k * tq)
                    dp = lax.dot_general(dov, vv, _DIMS["nt"], preferred_element_type=F32)
                    dsf = p * (dp - jnp.sum(p * dp, axis=-1, keepdims=True))
                    css.append(jnp.sum(dsf, axis=0, keepdims=True))
                    ds = dsf.astype(BF16)
                    dqa_ref[:, al] = jnp.dot(ds, kav, preferred_element_type=F32)
                    dkas.append(lax.dot_general(ds, qav, _DIMS["tn"], preferred_element_type=F32))
                    dvs.append(lax.dot_general(p.astype(BF16), dov, _DIMS["tn"], preferred_element_type=F32))
                dka_ref[0:L, :] += jnp.concatenate(dkas, axis=-1)
                dv_ref[0:L, :] += jnp.concatenate(dvs, axis=-1)
                dcs_ref[:, 0:L] += jnp.concatenate(css + [jnp.zeros((8 - nh, L), F32)], axis=0)

    return _pc(body, name=name,
               out_shape=(jax.ShapeDtypeStruct((T, HP * aw), F32), jax.ShapeDtypeStruct((T, HP * aw), F32),
                          jax.ShapeDtypeStruct((T, HP * LANES), F32), jax.ShapeDtypeStruct((nb, HP, 8, S), F32)),
               grid=(nb, HP, nq),
               in_specs=[pl.BlockSpec((tq, aw), lambda b, h, i: (b * nq + i, h)),
                         pl.BlockSpec((S, aw), lambda b, h, i: (b, h)),
                         pl.BlockSpec((S, LANES), lambda b, h, i: (b, h)),
                         pl.BlockSpec((tq, LANES), lambda b, h, i: (b * nq + i, h))],
               out_specs=(pl.BlockSpec((tq, aw), lambda b, h, i: (b * nq + i, h)),
                          pl.BlockSpec((S, aw), lambda b, h, i: (b, h)),
                          pl.BlockSpec((S, LANES), lambda b, h, i: (b, h)),
                          pl.BlockSpec((None, None, 8, S), lambda b, h, i: (b, h, 0, 0))),
               compiler_params=_cp("parallel", "parallel", "arbitrary"))(qa, ka, vb, do)


def _mem_probs(q, k, scale):
    s = lax.dot_general(q, k, _DIMS["nt"], preferred_element_type=F32) * scale
    e = jnp.exp(s - jnp.max(s, axis=-1, keepdims=True))
    return e / jnp.sum(e, axis=-1, keepdims=True)


def _mem_fwd(q, k, v, nb, S, ML, *, name, tq=512):
    T, MW = q.shape
    nh = MW // LANES
    tq = _tile(S, tq, 8)
    nq = S // tq
    scale = LANES ** -0.5

    def body(q_ref, k_ref, v_ref, o_ref):
        for h in range(nh):
            sl = slice(h * LANES, (h + 1) * LANES)
            p = _mem_probs(q_ref[:, sl], k_ref[:, sl], scale)
            o_ref[:, sl] = jnp.dot(p.astype(BF16), v_ref[:, sl], preferred_element_type=F32).astype(BF16)

    return _pc(body, name=name, out_shape=jax.ShapeDtypeStruct((T, MW), BF16), grid=(nb, nq),
               in_specs=[pl.BlockSpec((tq, MW), lambda b, i: (b * nq + i, 0)),
                         pl.BlockSpec((ML, MW), lambda b, i: (b, 0)), pl.BlockSpec((ML, MW), lambda b, i: (b, 0))],
               out_specs=pl.BlockSpec((tq, MW), lambda b, i: (b * nq + i, 0)),
               compiler_params=_cp("parallel", "arbitrary"))(q, k, v)


def _mem_bwd(q, k, v, do, nb, S, ML, *, name, tq=512):
    T, MW = q.shape
    nh = MW // LANES
    tq = _tile(S, tq, 8)
    nq = S // tq
    scale = LANES ** -0.5

    def body(q_ref, k_ref, v_ref, do_ref, dq_ref, dk_ref, dv_ref):
        i = pl.program_id(1)

        @pl.when(i == 0)
        def _():
            dk_ref[...] = jnp.zeros_like(dk_ref)
            dv_ref[...] = jnp.zeros_like(dv_ref)

        for h in range(nh):
            sl = slice(h * LANES, (h + 1) * LANES)
            qv, kv, vv, dov = q_ref[:, sl], k_ref[:, sl], v_ref[:, sl], do_ref[:, sl]
            p = _mem_probs(qv, kv, scale)
            dp = lax.dot_general(dov, vv, _DIMS["nt"], preferred_element_type=F32)
            ds = (p * (dp - jnp.sum(p * dp, axis=-1, keepdims=True)) * scale).astype(BF16)
            dq_ref[:, sl] = jnp.dot(ds, kv, preferred_element_type=F32)
            dk_ref[:, sl] += lax.dot_general(ds, qv, _DIMS["tn"], preferred_element_type=F32)
            dv_ref[:, sl] += lax.dot_general(p.astype(BF16), dov, _DIMS["tn"], preferred_element_type=F32)

    kvspec = pl.BlockSpec((ML, MW), lambda b, i: (b, 0))
    qspec = pl.BlockSpec((tq, MW), lambda b, i: (b * nq + i, 0))
    return _pc(body, name=name,
               out_shape=(jax.ShapeDtypeStruct((T, MW), F32), jax.ShapeDtypeStruct((nb * ML, MW), F32),
                          jax.ShapeDtypeStruct((nb * ML, MW), F32)),
               grid=(nb, nq), in_specs=[qspec, kvspec, kvspec, qspec], out_specs=(qspec, kvspec, kvspec),
               compiler_params=_cp("parallel", "arbitrary"))(q, k, v, do)


def _place():
    x, y, c = lax.axis_index("x"), lax.axis_index("y"), lax.axis_index("c")
    chips = [(1 - x, y), (x, 1 - y), (1 - x, 1 - y)]
    return x, y, c, chips


def _shard_win(ref, kind, R, C, j, h=None):
    if kind == "col":
        rows = pl.ds(0, R) if h is None else pl.ds(pl.multiple_of(h * (R // 2), 16), R // 2)
        return ref.at[rows, pl.ds(pl.multiple_of(j * C, LANES), C)]
    if h is None:
        return ref.at[pl.ds(pl.multiple_of(j * R, 16), R), :]
    return ref.at[pl.ds(pl.multiple_of(j * R + h * (R // 2), 16), R // 2), :]


def _any_specs(n):
    return [pl.BlockSpec(memory_space=pl.ANY)] * n


class _Cargo:
    def __init__(self, arrays, out_shapes, ncopies, copies, alias=None):
        self.arrays, self.out_shapes, self.ncopies, self.copies = list(arrays), list(out_shapes), ncopies, copies
        self.alias = alias or {}


def _load_cargo(body, cargo, kw):
    as_list = lambda v: list(v) if isinstance(v, (tuple, list)) else [v]
    grid = kw.get("grid", ())
    in_specs, out_specs, out_shape = as_list(kw["in_specs"]), as_list(kw["out_specs"]), as_list(kw["out_shape"])
    scratch = list(kw.get("scratch_shapes", ()))
    nin, nout, nci, nco, nscr = len(in_specs), len(out_specs), len(cargo.arrays), len(cargo.out_shapes), len(scratch)

    def loaded(*refs):
        ins, cin = refs[:nin], refs[nin:nin + nci]
        outs, cout = refs[nin + nci:nin + nci + nout], refs[nin + nci + nout:nin + nci + nout + nco]
        scr, (ssem, rsem) = refs[nin + nci + nout + nco:-2], refs[-2:]
        first = last = True
        for a, g in enumerate(grid):
            first = first & (pl.program_id(a) == 0)
            last = last & (pl.program_id(a) == g - 1)

        def start():
            for cp in cargo.copies(cin, cout, ssem, rsem, False):
                cp.start()

        def wait():
            for cp in cargo.copies(cin, cout, ssem, rsem, True):
                cp.wait_send()
                cp.wait_recv()

        if grid:
            pl.when(first)(start)
        else:
            start()
        body(*ins, *outs, *scr)
        if grid:
            pl.when(last)(wait)
        else:
            wait()

    kw = dict(kw, in_specs=in_specs + _any_specs(nci), out_specs=tuple(out_specs + _any_specs(nco)),
              out_shape=tuple(out_shape + cargo.out_shapes),
              scratch_shapes=scratch + [pltpu.SemaphoreType.DMA((cargo.ncopies,))] * 2)
    if cargo.alias:
        kw["input_output_aliases"] = {nin + i: nout + o for i, o in cargo.alias.items()}
    if grid:
        kw["compiler_params"] = _cp(*["arbitrary"] * len(grid))
    return loaded, kw


class _SemView:
    def __init__(self, ref, off):
        self.ref, self.off, self.at = ref, off, self

    def __getitem__(self, k):
        return self.ref.at[self.off + k]


def _join_cargo(a, b):
    na, nao = len(a.arrays), len(a.out_shapes)

    def copies(ins, outs, ssem, rsem, waiting):
        return (a.copies(ins[:na], outs[:nao], ssem, rsem, waiting) +
                b.copies(ins[na:], outs[nao:], _SemView(ssem, a.ncopies), _SemView(rsem, a.ncopies), waiting))

    alias = dict(a.alias)
    alias.update({na + i: nao + o for i, o in b.alias.items()})
    return _Cargo(a.arrays + b.arrays, a.out_shapes + b.out_shapes, a.ncopies + b.ncopies, copies, alias)


def _run_cargo(cargo, *, name):
    res = _pc(lambda: None, name=name, cargo=cargo, in_specs=[], out_specs=[], out_shape=[])(*cargo.arrays)
    return list(res)


def _full_shape(dims, kind):
    return (dims[0], 4 * dims[1]) if kind == "col" else (4 * dims[0], dims[1])


def _ag_copy(shard, full, kind, j_src, chip, c, ssem, rsem, piece):
    R, C = shard.shape
    p, npieces = piece
    rows = R // 2 // npieces
    row0 = pl.multiple_of(c * (R // 2) + p * rows, 16)
    if kind == "col":
        dst = full.at[pl.ds(row0, rows), pl.ds(pl.multiple_of(j_src * C, LANES), C)]
    else:
        dst = full.at[pl.ds(pl.multiple_of(j_src * R + row0, 16), rows), :]
    return pltpu.make_async_remote_copy(shard.at[pl.ds(row0, rows), :], dst, ssem, rsem,
                                        device_id=(*chip, c), device_id_type=MESH)


def _ag_cargo(shards, kinds, piece=(0, 1), lands=None):
    n = len(shards)

    def copies(ins, outs, ssem, rsem, waiting):
        x, y, c, chips = _place()
        return [_ag_copy(ins[w], outs[w], kinds[w], 2 * chip[0] + chip[1] if waiting else 2 * x + y, chip, c,
                         ssem.at[3 * w + q], rsem.at[3 * w + q], piece)
                for w in range(n) for q, chip in enumerate(chips)]

    shapes = [jax.ShapeDtypeStruct(_full_shape(s.shape, k), BF16) for s, k in zip(shards, kinds)]
    if lands is None:
        return _Cargo(shards, shapes, 3 * n, copies)
    return _Cargo(list(shards) + list(lands), shapes, 3 * n, copies, alias={n + w: w for w in range(n)})


def _scatter_cargo(parts, kinds):
    pdims = [((p.shape[0], p.shape[1] // 4) if k == "col" else (p.shape[0] // 4, p.shape[1]))
             for p, k in zip(parts, kinds)]

    def copies(ins, outs, ssem, rsem, waiting):
        x, y, c, chips = _place()
        return [pltpu.make_async_remote_copy(_shard_win(ins[w], kinds[w], *pdims[w], 2 * cx + cy), outs[w].at[q],
                                             ssem.at[3 * w + q], rsem.at[3 * w + q],
                                             device_id=(cx, cy, c), device_id_type=MESH)
                for w in range(len(parts)) for q, (cx, cy) in enumerate(chips)]

    return _Cargo(parts, [jax.ShapeDtypeStruct((3,) + d, BF16) for d in pdims], 3 * len(parts), copies)


def _ag_fill(full, shard, kind, *, name):
    R, C = shard.shape
    RH = R // 2
    tm = _tile(RH, max(16, (3 * 2 ** 19 // C) // 16 * 16), 16)
    nbh = RH // tm
    n, nown = 3 * nbh, 2 * nbh
    xyc = jnp.stack([lax.axis_index("x"), lax.axis_index("y"), lax.axis_index("c")]).astype(jnp.int32)

    def chip_of(q, x, y):
        keep_x, keep_y = q % 2, (2 - q) // 2
        return 2 * ((1 - x) + keep_x * (2 * x - 1)) + (1 - y) + keep_y * (2 * y - 1)

    def window(ref, j, row0):
        if kind == "col":
            return ref.at[pl.ds(pl.multiple_of(row0, 16), tm), pl.ds(pl.multiple_of(j * C, LANES), C)]
        return ref.at[pl.ds(pl.multiple_of(j * R + row0, 16), tm), :]

    def body(p_ref, send_ref, own_ref, out_ref, land, ssem, rsem, wsem):
        i = pl.program_id(0)
        x, y = lax.axis_index("x"), lax.axis_index("y")
        c, peer = _pair_peer()
        own_cp = pltpu.make_async_copy(own_ref, window(out_ref, 2 * x + y, jnp.minimum(i, nown - 1) * tm), wsem.at[1])

        @pl.when(i < nown)
        def _():
            own_cp.start()

        def consume(land, ps):
            b = i - 1
            cp = pltpu.make_async_copy(land.at[ps], window(out_ref, chip_of(b // nbh, x, y),
                                                           (1 - c) * RH + (b % nbh) * tm), wsem.at[0])
            cp.start()
            cp.wait()

        _exchange_step(i, n, send_ref, land, ssem, rsem, None, peer, consume)

        @pl.when(i < nown)
        def _():
            own_cp.wait()

    now = lambda i: jnp.minimum(i, n - 1)

    def send_map(i, p):
        q, b = now(i) // nbh, now(i) % nbh
        j = chip_of(q, p[0], p[1])
        return (p[2] * nbh + b, j) if kind == "col" else ((j * 2 + p[2]) * nbh + b, 0)

    grid_spec = pltpu.PrefetchScalarGridSpec(
        num_scalar_prefetch=1, grid=(n + 1,),
        in_specs=[pl.BlockSpec((tm, C), send_map),
                  pl.BlockSpec((tm, C), lambda i, p: (jnp.minimum(i, nown - 1), 0))],
        out_specs=pl.BlockSpec(memory_space=pl.ANY),
        scratch_shapes=[pltpu.VMEM((n, tm, C), BF16), pltpu.SemaphoreType.DMA((n,)), pltpu.SemaphoreType.DMA((n,)),
                        pltpu.SemaphoreType.DMA((2,))])
    return _pc(body, name=name, out_shape=jax.ShapeDtypeStruct(full.shape, BF16), grid_spec=grid_spec,
               input_output_aliases={1: 0}, compiler_params=_cp("arbitrary"))(xyc, full, shard)


def _pair_peer():
    x, y, c = lax.axis_index("x"), lax.axis_index("y"), lax.axis_index("c")
    return c, (x, y, 1 - c)


def _exchange_step(i, n, src_ref, land, ssem, rsem, credit, peer, consume):
    def rdma(slot):
        return pltpu.make_async_remote_copy(src_ref, land.at[slot], ssem.at[slot], rsem.at[slot],
                                            device_id=peer, device_id_type=MESH)

    reuse = credit is not None
    slot = i % 2 if reuse else jnp.minimum(i, n - 1)

    @pl.when(i < n)
    def _():
        if reuse:
            @pl.when(i >= 2)
            def _():
                pl.semaphore_wait(credit, 1)

        rdma(slot).start()

    @pl.when(i >= 1)
    def _():
        ps = (i - 1) % 2 if reuse else i - 1
        rdma(ps).wait_recv()
        consume(land, ps)

        if reuse:
            @pl.when(i - 1 < n - 2)
            def _():
                pl.semaphore_signal(credit, 1, device_id=peer, device_id_type=MESH)

    @pl.when(i < n)
    def _():
        rdma(slot).wait_send()


def _xchg_scratch(tm, C, dtype):
    return [pltpu.VMEM((2, tm, C), dtype), pltpu.SemaphoreType.DMA((2,)), pltpu.SemaphoreType.DMA((2,)),
            pltpu.SemaphoreType.REGULAR]


def _rs_pair_sum(g, kind, *, name):
    K, N = g.shape
    NS, RH = (1, K // 2) if kind == "col" else (4, K // 8)
    tm = _tile(RH, max(16, (3 * 2 ** 19 // N) // 16 * 16), 16)
    if NS * (RH // tm) < 2:
        tm = RH // 2
    nb = RH // tm
    n = NS * nb
    cvec = lax.axis_index("c").reshape(1).astype(jnp.int32)

    def body(c_ref, send_ref, own_ref, out_ref, land, ssem, rsem, credit):
        _, peer = _pair_peer()

        def consume(land, ps):
            out_ref[...] = (own_ref[...].astype(F32) + land[ps].astype(F32)).astype(BF16)

        _exchange_step(pl.program_id(0), n, send_ref, land, ssem, rsem, credit, peer, consume)

    now = lambda i: jnp.minimum(i, n - 1)
    lag = lambda i: jnp.maximum(i - 1, 0)
    grid_spec = pltpu.PrefetchScalarGridSpec(
        num_scalar_prefetch=1, grid=(n + 1,),
        in_specs=[pl.BlockSpec((tm, N), lambda i, c: ((now(i) // nb * 2 + 1 - c[0]) * nb + now(i) % nb, 0)),
                  pl.BlockSpec((tm, N), lambda i, c: ((lag(i) // nb * 2 + c[0]) * nb + lag(i) % nb, 0))],
        out_specs=pl.BlockSpec((tm, N), lambda i, c: (lag(i), 0)),
        scratch_shapes=_xchg_scratch(tm, N, BF16))
    return _pc(body, name=name, out_shape=jax.ShapeDtypeStruct((NS * RH, N), BF16), grid_spec=grid_spec,
               compiler_params=_cp("arbitrary"))(cvec, g, g)


def _rs_sum_join(part, got, kind, *, name):
    _, hr, hc = got.shape
    tm = _tile(hr, max(16, (3 * 2 ** 18 // hc) // 16 * 16), 16)
    if hr // tm < 2:
        tm = hr // 2
    nb = hr // tm
    jvec = (2 * lax.axis_index("x") + lax.axis_index("y")).reshape(1).astype(jnp.int32)

    def body(j_ref, p_ref, a_ref, b_ref, c_ref, mine_ref, theirs_ref, sbuf, land, ssem, rsem, credit):
        i = pl.program_id(0)
        _, peer = _pair_peer()

        @pl.when(i < nb)
        def _():
            v = ((p_ref[...].astype(F32) + a_ref[...].astype(F32)) + b_ref[...].astype(F32)) + c_ref[...].astype(F32)
            mine_ref[...] = v
            sbuf[...] = v

        def consume(land, ps):
            theirs_ref[...] = land[ps]

        _exchange_step(i, nb, sbuf, land, ssem, rsem, credit, peer, consume)

    now = lambda i: jnp.minimum(i, nb - 1)
    lag = lambda i: jnp.maximum(i - 1, 0)
    own_spec = (pl.BlockSpec((tm, hc), lambda i, j: (now(i), j[0])) if kind == "col"
                else pl.BlockSpec((tm, hc), lambda i, j: (j[0] * nb + now(i), 0)))
    got_spec = lambda q: pl.BlockSpec((None, tm, hc), functools.partial(lambda i, j, q: (q, now(i), 0), q=q))
    grid_spec = pltpu.PrefetchScalarGridSpec(
        num_scalar_prefetch=1, grid=(nb + 1,),
        in_specs=[own_spec, got_spec(0), got_spec(1), got_spec(2)],
        out_specs=(pl.BlockSpec((tm, hc), lambda i, j: (now(i), 0)), pl.BlockSpec((tm, hc), lambda i, j: (lag(i), 0))),
        scratch_shapes=[pltpu.VMEM((tm, hc), F32)] + _xchg_scratch(tm, hc, F32))
    shp = jax.ShapeDtypeStruct((hr, hc), F32)
    return _pc(body, name=name, out_shape=(shp, shp), grid_spec=grid_spec,
               compiler_params=_cp("arbitrary"))(jvec, part, got, got, got)


def _adamw_join(w, mine, theirs, m, v, *, name):
    R2, wc = w.shape
    hr, hc = mine.shape
    assert R2 == 2 * hr and hc >= wc
    tm = _tile(hr, max(8, (2 ** 18 // hc) // 8 * 8), 8)

    def body(w_ref, a_ref, b_ref, m_ref, v_ref, g_ref, d_ref, m2_ref, v2_ref):
        gv = jnp.where(pl.program_id(0) == lax.axis_index("c"), a_ref[...], b_ref[...])[:, :wc]
        m2 = ADAM_B1 * m_ref[...] + (1.0 - ADAM_B1) * gv
        v2 = ADAM_B2 * v_ref[...] + (1.0 - ADAM_B2) * (gv * gv)
        m_hat = m2 / (1.0 - ADAM_B1 ** ADAM_STEP)
        v_hat = v2 / (1.0 - ADAM_B2 ** ADAM_STEP)
        g_ref[...] = gv
        d_ref[...] = -ADAM_LR * (m_hat / (jnp.sqrt(v_hat) + ADAM_EPS) + ADAM_WD * w_ref[...])
        m2_ref[...] = m2
        v2_ref[...] = v2

    wspec = pl.BlockSpec((None, tm, wc), lambda h, i: (h, i, 0))
    gspec = pl.BlockSpec((tm, hc), lambda h, i: (i, 0))
    shp = jax.ShapeDtypeStruct((2, hr, wc), F32)
    v3 = lambda a: a.reshape(2, hr, wc)
    outs = _pc(body, name=name, out_shape=(shp,) * 4, grid=(2, hr // tm),
               in_specs=[wspec, gspec, gspec, wspec, wspec], out_specs=(wspec,) * 4,
               compiler_params=_cp("parallel", "parallel"))(v3(w), mine, theirs, v3(m), v3(v))
    return [o.reshape(R2, wc) for o in outs]


def _all_reduce_small(pack):
    R = pack.shape[0]

    def body(p_ref, o_ref, buf, send, recv):
        x, y, c, _ = _place()
        me = 4 * x + 2 * y + c
        buf[0] = p_ref[...]
        cps = []
        for r in range(1, 8):
            peer = (x if r & 4 == 0 else 1 - x, y if r & 2 == 0 else 1 - y, c if r & 1 == 0 else 1 - c)
            cp = pltpu.make_async_remote_copy(p_ref, buf.at[r], send.at[r - 1], recv.at[r - 1],
                                              device_id=peer, device_id_type=MESH)
            cp.start()
            cps.append(cp)
        for cp in cps:
            cp.wait()
        acc = buf[jnp.bitwise_xor(me, 0)]
        for k in range(1, 8):
            acc = acc + buf[jnp.bitwise_xor(me, k)]
        o_ref[...] = acc

    return _pc(body, name="allreduce_small", out_shape=jax.ShapeDtypeStruct((R, LANES), F32),
               in_specs=[pl.BlockSpec(memory_space=pltpu.VMEM)], out_specs=pl.BlockSpec(memory_space=pltpu.VMEM),
               scratch_shapes=[pltpu.VMEM((8, R, LANES), F32), pltpu.SemaphoreType.DMA((7,)),
                               pltpu.SemaphoreType.DMA((7,))])(pack)


def _pack_small(arrs):
    parts, sizes = [], []
    for a in arrs:
        f = a.reshape(-1).astype(F32)
        n = _round_up(f.shape[0], 8 * LANES)
        parts.append(jnp.pad(f, (0, n - f.shape[0])).reshape(n // LANES, LANES))
        sizes.append((a.shape, f.shape[0], n // LANES))
    return jnp.concatenate(parts, axis=0), sizes


def _unpack_small(pack, sizes):
    out, r0 = [], 0
    for shape, cnt, rows in sizes:
        out.append(pack[r0:r0 + rows].reshape(-1)[:cnt].reshape(shape))
        r0 += rows
    return out


def _ffn_fwd(xin, norm, weight, tag, gather_on_gateup, gather_on_down):
    hb = _rms_fwd(xin, norm, name=tag + "_norm")
    cargo, landed = gather_on_gateup()
    res = _gateup(hb, weight("w_gate_up"), name=tag + "_gateup", tm=1024, cargo=cargo)
    if cargo is not None:
        res, lands = res
        landed(lands)
    gate, up, act = res
    cargo, landed = gather_on_down()
    xout = _mm(act, weight("w_down"), res=xin, scale=0.5, name=tag + "_down", tm=512, tn=512, tk=8192, cargo=cargo)
    if cargo is not None:
        xout, lands = xout
        landed(lands)
    return xout, (hb, gate, up, act)


def _ffn_bwd(dy, dyb, xin, norm, wgu, wd, saved, tag, scatter):
    hb, gate, up, act = saved
    d_wd = _mm(act, dyb, mode="tn", scale=0.5, out_dtype=BF16, name=tag + "_dwd", tm=512, tn=1024, tk=4096)
    cargo = scatter(tag + "_w_down", d_wd)
    dgu = _dact_swiglu(dyb, wd, gate, up, name=tag + "_dact")
    d_wgu, lands = _mm(hb, dgu, mode="tn", halves="b", out_dtype=BF16, name=tag + "_dwgu", tm=512, tn=512, tk=4096,
                       cargo=cargo)
    scatter.done(lands)
    cargo = scatter(tag + "_w_gate_up", d_wgu)
    dh, lands = _mm(dgu, wgu, mode="nt", halves="a", out_dtype=BF16, name=tag + "_dh", tm=1024, tn=512, tk=2816,
                    cargo=cargo)
    scatter.done(lands)
    dx, dxb, d_norm = _rms_bwd(xin, dh, norm, dy, name=tag + "_dnorm")
    return dx, dxb, d_norm


def kernel(x, mem, ffn1_norm, ffn1_w_gate_up, ffn1_w_down, mix_norm, mem_norm, w_in, b_forget, pool_w, pool_scale, w_pool_up, fox_q_norm, fox_k_norm, w_fox_o, w_mem_kv, mem_q_norm, mem_k_norm, w_mem_o, w_out, ffn2_norm, ffn2_w_gate_up, ffn2_w_down, loss_target, m_ffn1_norm, m_ffn1_w_gate_up, m_ffn1_w_down, m_mix_norm, m_mem_norm, m_w_in, m_b_forget, m_pool_w, m_pool_scale, m_w_pool_up, m_fox_q_norm, m_fox_k_norm, m_w_fox_o, m_w_mem_kv, m_mem_q_norm, m_mem_k_norm, m_w_mem_o, m_w_out, m_ffn2_norm, m_ffn2_w_gate_up, m_ffn2_w_down, v_ffn1_norm, v_ffn1_w_gate_up, v_ffn1_w_down, v_mix_norm, v_mem_norm, v_w_in, v_b_forget, v_pool_w, v_pool_scale, v_w_pool_up, v_fox_q_norm, v_fox_k_norm, v_w_fox_o, v_w_mem_kv, v_mem_q_norm, v_mem_k_norm, v_w_mem_o, v_w_out, v_ffn2_norm, v_ffn2_w_gate_up, v_ffn2_w_down):
    P = dict(locals())
    big = ["ffn1_w_gate_up", "ffn1_w_down", "w_in", "w_pool_up", "w_fox_o", "w_mem_kv", "w_mem_o", "w_out",
           "ffn2_w_gate_up", "ffn2_w_down"]
    kinds = ["col", "row", "col", "col", "col", "row", "col", "row", "col", "row"]
    small = ["ffn1_norm", "mix_norm", "mem_norm", "b_forget", "pool_w", "pool_scale", "fox_q_norm", "fox_k_norm",
             "mem_q_norm", "mem_k_norm", "ffn2_norm"]
    order = ["ffn1_norm", "ffn1_w_gate_up", "ffn1_w_down", "mix_norm", "mem_norm", "w_in", "b_forget", "pool_w",
             "pool_scale", "w_pool_up", "fox_q_norm", "fox_k_norm", "w_fox_o", "w_mem_kv", "mem_q_norm", "mem_k_norm",
             "w_mem_o", "w_out", "ffn2_norm", "ffn2_w_gate_up", "ffn2_w_down"]

    nb, S, D = x.shape
    T = nb * S
    ML = mem.shape[1]
    NF, HD = b_forget.shape[-1], fox_q_norm.shape[-1]
    FW = NF * HD
    MW, MHD = w_mem_o.shape[1], mem_q_norm.shape[-1]
    PG, PD = pool_w.shape[1], pool_w.shape[2]
    PW = PG * PD
    n_in = w_in.shape[-1]
    n_in_pad = _round_up(n_in, LANES)
    in_w = 4 * n_in
    assert HD * 2 == LANES and MHD == LANES and PD == LANES and in_w == PW + 3 * FW + NF + MW + 3 * D

    shards = []
    for nme in big:
        wl = P[nme][0].astype(BF16)
        if nme == "w_in":
            wl = jnp.pad(wl, ((0, 0), (0, n_in_pad - n_in)))
        shards.append(wl)
    shard, kind_of, full = dict(zip(big, shards)), dict(zip(big, kinds)), {}

    def landed(names):
        def fill(lands):
            for nme, l in zip(names, lands):
                full[nme] = _ag_fill(l, shard[nme], kind_of[nme], name="ag_fill_" + nme)
        return fill

    def gather(names, piece=(0, 1), lands=None):
        return _ag_cargo([shard[nme] for nme in names], [kind_of[nme] for nme in names], piece, lands)

    landed(["ffn1_w_gate_up"])(_run_cargo(gather(["ffn1_w_gate_up"]), name="ag_first"))
    begun = {}

    def on_gateup1():
        def fill(lands):
            landed(["ffn1_w_down"])(lands[:1])
            begun["w_in"] = lands[1]
        return _join_cargo(gather(["ffn1_w_down"]), gather(["w_in"], (0, 2))), fill

    x0 = x.reshape(T, D)
    x1, sv1 = _ffn_fwd(x0, ffn1_norm, lambda nme: full["ffn1_" + nme], "ffn1", on_gateup1,
                       lambda: (gather(["w_in"], (1, 2), [begun["w_in"]]), landed(["w_in"])))
    win_o = jnp.concatenate([full["w_in"][:, j * n_in_pad:j * n_in_pad + n_in] for j in range(4)], axis=1)
    o_q, o_k, o_v, o_f = PW, PW + FW, PW + 2 * FW, PW + 3 * FW
    o_qm, o_gate = o_f + NF, o_f + NF + MW
    p_q, p_k, p_v, p_pool, p_qm, p_f = 3 * D, 3 * D + FW, 3 * D + 2 * FW, 3 * D + 3 * FW, 3 * D + 3 * FW + PW, 3 * D + 3 * FW + PW + MW
    inp = _round_up(p_f + LANES, 512)
    win_p = jnp.concatenate([win_o[:, o_gate:], win_o[:, o_q:o_f], win_o[:, :PW], win_o[:, o_qm:o_gate],
                             win_o[:, o_f:o_qm], jnp.zeros((D, inp - p_f - NF), BF16)], axis=1)

    hb2 = _rms_fwd(x1, mix_norm, name="mix_norm")
    later = ["ffn2_w_down", "w_out", "w_fox_o", "w_mem_kv", "w_pool_up", "w_mem_o"]
    z, lands = _mm(hb2, win_p, name="in_proj", tm=1024, tn=1536, cargo=gather(later))
    landed(later)(lands)

    vpool = _pool_fwd(z, p_pool // LANES, pool_w[0], pool_scale, nb, S, name="pool_fwd")
    y_pool = _mm(vpool, full["w_pool_up"], name="pool_up", tm=1024, tn=512)

    b128 = jnp.pad(b_forget, ((0, 0), (0, LANES - NF)))
    c3 = _fox_c(z, p_f // LANES, b128, nb, S, NF, name="fox_c")
    qa, ka, vb = _fox_qk(z, p_q // LANES, p_k // LANES, p_v // LANES, c3, fox_q_norm, fox_k_norm, HD, NF,
                         name="fox_qk")
    o_fox, lands = _fox_fwd(qa, ka, vb, nb, S, HD, name="fox_fwd", cargo=gather(["ffn2_w_gate_up"]))
    landed(["ffn2_w_gate_up"])(lands)
    y_fox = _mm(o_fox, full["w_fox_o"], name="fox_out", tm=1024, tn=512)

    mem2 = mem.reshape(nb * ML, D)
    memn = _rms_fwd(mem2, mem_norm, name="mem_norm")
    kv = _mm(memn, full["w_mem_kv"], name="mem_kv")
    kmn = _headnorm_fwd(kv, 0, MW, mem_k_norm, MHD, name="mem_knorm")
    vmb = kv[:, MW:].astype(BF16)
    qmn = _headnorm_fwd(z, p_qm // LANES, MW, mem_q_norm, MHD, name="mem_qnorm")
    o_mem = _mem_fwd(qmn, kmn, vmb, nb, S, ML, name="mem_fwd")
    y_mem = _mm(o_mem, full["w_mem_o"], name="mem_out", tm=1024, tn=512)

    merged = _merge_fwd(z, y_pool, y_fox, y_mem, name="merge_fwd")
    x2 = _mm(merged, full["w_out"], res=x1, name="out_proj", tm=1024, tn=512)
    x3, sv2 = _ffn_fwd(x2, ffn2_norm, lambda nme: full["ffn2_" + nme], "ffn2", lambda: (None, None),
                       lambda: (None, None))
    dy3, dy3b, loss_part = _loss(x3, loss_target.reshape(T, D), name="loss")

    g, parts, got = {}, {}, {}

    def scatter(*named_grads):
        names = named_grads[0::2]
        for nme, grad in zip(names, named_grads[1::2]):
            parts[nme] = _rs_pair_sum(grad, kind_of[nme], name="rs_pair_" + nme)
        scatter.names = names
        return _scatter_cargo([parts[nme] for nme in names], [kind_of[nme] for nme in names])

    def scattered(lands):
        got.update(zip(scatter.names, lands))

    scatter.done = scattered
    dx2, dx2b, g["ffn2_norm"] = _ffn_bwd(dy3, dy3b, x2, ffn2_norm, full["ffn2_w_gate_up"], full["ffn2_w_down"], sv2,
                                         "ffn2", scatter)

    dmerged = _mm(dx2b, full["w_out"], mode="nt", name="d_merged", tm=1024, tn=512)
    g["w_out"] = _mm(merged, dx2b, mode="tn", out_dtype=BF16, name="d_w_out", tk=4096)
    dyp, dyf, dym, dgl = _merge_bwd(dmerged, z, y_pool, y_fox, y_mem, name="merge_bwd")

    g["w_pool_up"] = _mm(vpool, dyp, mode="tn", out_dtype=BF16, name="d_w_pool_up", tk=4096)
    dvp = _mm(dyp, full["w_pool_up"], mode="nt", name="d_vpool", tm=1024)
    du_pool, g["pool_w"], g["pool_scale"] = _pool_bwd(z, p_pool // LANES, dvp, pool_w[0], pool_scale, nb, S,
                                                      name="pool_bwd")

    g["w_fox_o"] = _mm(o_fox, dyf, mode="tn", out_dtype=BF16, name="d_w_fox_o", tk=4096)
    do_fox = _mm(dyf, full["w_fox_o"], mode="nt", out_dtype=BF16, name="d_o_fox", tm=1024)
    dqa, dka, dvf, dcs = _fox_bwd(qa, ka, vb, do_fox, nb, S, HD, name="fox_bwd")
    dq, dgq = _headnorm_bwd(z, p_q // LANES, FW, dqa, fox_q_norm, HD, HD ** -0.5, True, BF16, name="fox_dq")
    dk, dgk = _headnorm_bwd(z, p_k // LANES, FW, dka, fox_k_norm, HD, 1.0, True, BF16, name="fox_dk")
    g["fox_q_norm"], g["fox_k_norm"] = _fold_gain(dgq, HD), _fold_gain(dgk, HD)
    dc = -dcs[:, :, :LANES // HD, :].reshape(nb, NF, S).transpose(0, 2, 1).reshape(T, NF)
    df, db = _fox_c_bwd(jnp.pad(dc, ((0, 0), (0, LANES - NF))), z, p_f // LANES, b128, nb, S, name="fox_c_bwd")
    g["b_forget"] = db[:, :NF]

    g["w_mem_o"] = _mm(o_mem, dym, mode="tn", out_dtype=BF16, name="d_w_mem_o", tk=4096)
    do_mem = _mm(dym, full["w_mem_o"], mode="nt", out_dtype=BF16, name="d_o_mem", tm=1024)
    dqmn, dkmn, dvm = _mem_bwd(qmn, kmn, vmb, do_mem, nb, S, ML, name="mem_bwd")
    dqm, g["mem_q_norm"] = _headnorm_bwd(z, p_qm // LANES, MW, dqmn, mem_q_norm, MHD, 1.0, False, BF16, name="mem_dq")
    dkm, g["mem_k_norm"] = _headnorm_bwd(kv, 0, MW, dkmn, mem_k_norm, MHD, 1.0, False, BF16, name="mem_dk")
    dkv = jnp.concatenate([dkm, dvm.astype(BF16)], axis=1)
    g["w_mem_kv"] = _mm(memn, dkv, mode="tn", out_dtype=BF16, name="d_w_mem_kv")
    dmemn = _mm(dkv, full["w_mem_kv"], mode="nt", name="d_memn")
    _, _, g["mem_norm"] = _rms_bwd(mem2, dmemn, mem_norm, jnp.zeros_like(mem2), name="mem_dnorm")

    dz = jnp.concatenate([dgl, dq, dk, dvf.astype(BF16), du_pool, dqm, df.astype(BF16),
                          jnp.zeros((T, inp - p_f - LANES), BF16)], axis=1)
    cargo = scatter(*sum([[nme, g[nme]] for nme in ("w_out", "w_pool_up", "w_fox_o", "w_mem_o", "w_mem_kv")], []))
    dwin_p, lands = _mm(hb2, dz, mode="tn", out_dtype=BF16, name="d_w_in", tn=1536, tk=4096, cargo=cargo)
    scattered(lands)
    dwin_o = jnp.concatenate([dwin_p[:, p_pool:p_qm], dwin_p[:, p_q:p_pool], dwin_p[:, p_f:p_f + NF],
                              dwin_p[:, p_qm:p_f], dwin_p[:, :p_q]], axis=1)
    zpad = jnp.zeros((D, n_in_pad - n_in), BF16)
    cargo = scatter("w_in", jnp.concatenate(sum([[dwin_o[:, j * n_in:(j + 1) * n_in], zpad] for j in range(4)], []),
                                            axis=1))
    dh2, lands = _mm(dz, win_p, mode="nt", out_dtype=BF16, name="d_h2", tm=1024, tn=512, tk=3584, cargo=cargo)
    scattered(lands)
    dx1, dx1b, g["mix_norm"] = _rms_bwd(x1, dh2, mix_norm, dx2, name="mix_dnorm")

    dx0, _, g["ffn1_norm"] = _ffn_bwd(dx1, dx1b, x0, ffn1_norm, full["ffn1_w_gate_up"], full["ffn1_w_down"], sv1,
                                      "ffn1", scatter)

    halves = {nme: _rs_sum_join(parts[nme], got[nme], kind_of[nme], name="rs_join_" + nme) for nme in big}
    gfull = {}

    gpack, sizes = _pack_small([g[nme].reshape(P[nme].shape) for nme in small] + [loss_part[:, :1]])
    gsum = _unpack_small(_all_reduce_small(gpack), sizes)
    loss = gsum[-1].reshape(())
    for nme, a in zip(small, gsum[:-1]):
        gfull[nme] = a

    delta, new_m, new_v = {}, {}, {}
    for nme in big:
        shp = P[nme].shape
        two = lambda a: a.reshape(shp[-2:])
        gfull[nme], delta[nme], new_m[nme], new_v[nme] = [
            a.reshape(shp) for a in _adamw_join(two(P[nme]), *halves[nme], two(P["m_" + nme]), two(P["v_" + nme]),
                                                name="adamw_" + nme)]
    wpack, _ = _pack_small([P[nme] for nme in small])
    mpack, _ = _pack_small([P["m_" + nme] for nme in small])
    vpack, _ = _pack_small([P["v_" + nme] for nme in small])
    gpack2, ssz = _pack_small([gfull[nme] for nme in small])
    for dct, pk in zip((delta, new_m, new_v), _adamw(wpack, gpack2, mpack, vpack, name="adamw_small")):
        for nme, a in zip(small, _unpack_small(pk, ssz)):
            dct[nme] = a

    grad_x = dx0.reshape(nb, S, D)
    return (loss, grad_x, *[gfull[nme] for nme in order], *[delta[nme] for nme in order],
            *[new_m[nme] for nme in order], *[new_v[nme] for nme in order])
```

```python
import functools

import jax
import jax.numpy as jnp
from jax import lax
from jax.experimental import pallas as pl
from jax.experimental.pallas import tpu as pltpu

F32 = jnp.float32
BF16 = jnp.bfloat16
MESH = pl.DeviceIdType.MESH
EPS = 1e-6
POOL_WINDOWS = (2, 4, 8, 16)
ADAM_LR, ADAM_B1, ADAM_B2, ADAM_EPS, ADAM_WD, ADAM_STEP = 0.001, 0.9, 0.999, 1e-08, 0.01, 10
LANES = 128
VMEM_LIMIT = 56 * 1024 * 1024
NEG = -1e30


def _pc(body, *, name, cargo=None, **kw):
    if cargo is not None:
        body, kw = _load_cargo(body, cargo, kw)
    return pl.pallas_call(body, name=name, **kw)


def _cp(*sem):
    return pltpu.CompilerParams(dimension_semantics=sem, vmem_limit_bytes=VMEM_LIMIT)


def _tile(n, pref, mult=LANES):
    if n <= pref:
        return n
    t = (pref // mult) * mult
    while t >= mult:
        if n % t == 0:
            return t
        t -= mult
    return n


def _round_up(n, m):
    return (n + m - 1) // m * m


_DIMS = {"nn": (((1,), (0,)), ((), ())), "nt": (((1,), (1,)), ((), ())), "tn": (((0,), (0,)), ((), ()))}


def _split_cargo(res, cargo):
    if cargo is None:
        return res
    nco = len(cargo.out_shapes)
    own = res[:len(res) - nco]
    return (own[0] if len(own) == 1 else own), list(res[len(res) - nco:])


def _mm(a, b, *, name, mode="nn", out_dtype=F32, scale=1.0, res=None, tm=512, tn=512, tk=2048, cargo=None,
        halves=None):
    if halves == "a":
        assert mode == "nt"
        (_, M, Kh), (N, K2) = a.shape, b.shape
        K = 2 * Kh
    elif halves == "b":
        assert mode == "tn"
        (K, M), (_, K2, Nh) = a.shape, b.shape
        N = 2 * Nh
    elif mode == "nn":
        (M, K), (K2, N) = a.shape, b.shape
    elif mode == "nt":
        (M, K), (N, K2) = a.shape, b.shape
    else:
        (K, M), (K2, N) = a.shape, b.shape
    assert K == K2, (name, a.shape, b.shape)
    tm = _tile(M, tm, 8 if M % LANES else LANES)
    tn = _tile(N // 2 if halves == "b" else N, tn)
    tk = _tile(K // 2 if halves == "a" else K, tk)
    nk = K // tk
    dims = _DIMS[mode]
    if halves == "a":
        a_spec = pl.BlockSpec((None, tm, tk), lambda i, j, k: (k // (nk // 2), i, k % (nk // 2)))
    elif mode == "tn":
        a_spec = pl.BlockSpec((tk, tm), lambda i, j, k: (k, i))
    else:
        a_spec = pl.BlockSpec((tm, tk), lambda i, j, k: (i, k))
    if halves == "b":
        njh = N // 2 // tn
        b_spec = pl.BlockSpec((None, tk, tn), lambda i, j, k: (j // njh, k, j % njh))
    elif mode == "nt":
        b_spec = pl.BlockSpec((tn, tk), lambda i, j, k: (j, k))
    else:
        b_spec = pl.BlockSpec((tk, tn), lambda i, j, k: (k, j))
    o_spec = pl.BlockSpec((tm, tn), lambda i, j, k: (i, j))
    has_res = res is not None

    def body(*refs):
        a_ref, b_ref = refs[:2]
        r_ref = refs[2] if has_res else None
        o_ref = refs[3] if has_res else refs[2]

        def finish(v):
            v = v * scale
            if has_res:
                v = r_ref[...] + v
            o_ref[...] = v.astype(o_ref.dtype)

        prod = lax.dot_general(a_ref[...], b_ref[...], dims, preferred_element_type=F32)
        if nk == 1:
            finish(prod)
            return
        acc, k = refs[-1], pl.program_id(2)

        @pl.when(k == 0)
        def _():
            acc[...] = prod

        @pl.when(k > 0)
        def _():
            acc[...] += prod

        @pl.when(k == nk - 1)
        def _():
            finish(acc[...])

    out = _pc(body, name=name, cargo=cargo, out_shape=jax.ShapeDtypeStruct((M, N), out_dtype),
              grid=(M // tm, N // tn, nk),
              in_specs=[a_spec, b_spec] + ([o_spec] if has_res else []), out_specs=o_spec,
              scratch_shapes=[] if nk == 1 else [pltpu.VMEM((tm, tn), F32)],
              compiler_params=_cp("parallel", "parallel", "arbitrary"))(
                  *((a, b, res) if has_res else (a, b)), *(cargo.arrays if cargo else ()))
    return _split_cargo(out, cargo)


def _dact_swiglu(dyb, wd, gate, up, *, name, tm=1024, tn=512):
    M, K = dyb.shape
    Fh = wd.shape[0]
    tm, tn = _tile(M, tm), _tile(Fh, tn)

    def body(dy_ref, w_ref, g_ref, u_ref, o_ref):
        d = lax.dot_general(dy_ref[...], w_ref[...], _DIMS["nt"], preferred_element_type=F32) * 0.5
        g, u = g_ref[...].astype(F32), u_ref[...].astype(F32)
        s = jax.nn.sigmoid(g)
        o_ref[0] = (d * u * (s * (1.0 + g * (1.0 - s)))).astype(BF16)
        o_ref[1] = (d * (g * s)).astype(BF16)

    t_spec = pl.BlockSpec((tm, tn), lambda i, j: (i, j))
    return _pc(body, name=name, out_shape=jax.ShapeDtypeStruct((2, M, Fh), BF16), grid=(M // tm, Fh // tn),
               in_specs=[pl.BlockSpec((tm, K), lambda i, j: (i, 0)), pl.BlockSpec((tn, K), lambda i, j: (j, 0)),
                         t_spec, t_spec],
               out_specs=pl.BlockSpec((2, tm, tn), lambda i, j: (0, i, j)),
               compiler_params=_cp("parallel", "parallel"))(dyb, wd, gate, up)


def _gateup(h, wgu, *, name, tm=512, tn=512, cargo=None):
    M, K = h.shape
    Fh = wgu.shape[1] // 2
    tm, tn = _tile(M, tm), _tile(Fh, tn)
    nj = Fh // tn

    def body(h_ref, wg_ref, wu_ref, g_ref, u_ref, a_ref):
        hv = h_ref[...]
        g = jnp.dot(hv, wg_ref[...], preferred_element_type=F32)
        u = jnp.dot(hv, wu_ref[...], preferred_element_type=F32)
        g_ref[...] = g.astype(BF16)
        u_ref[...] = u.astype(BF16)
        a_ref[...] = (g * jax.nn.sigmoid(g) * u).astype(BF16)

    o_spec = pl.BlockSpec((tm, tn), lambda i, j: (i, j))
    shp = jax.ShapeDtypeStruct((M, Fh), BF16)
    out = _pc(body, name=name, cargo=cargo, out_shape=(shp, shp, shp), grid=(M // tm, nj),
              in_specs=[pl.BlockSpec((tm, K), lambda i, j: (i, 0)),
                        pl.BlockSpec((K, tn), lambda i, j: (0, j)),
                        pl.BlockSpec((K, tn), lambda i, j: (0, j + nj))],
              out_specs=(o_spec, o_spec, o_spec),
              compiler_params=_cp("parallel", "parallel"))(h, wgu, wgu, *(cargo.arrays if cargo else ()))
    return _split_cargo(out, cargo)


def _rowmap(fn, ins, outs, *, rows, tm, name, accs=()):
    tm = _tile(rows, tm, 8)
    arrs, in_specs = [], []
    for d in ins:
        if d[0] == "row":
            _, a, w, cb = d
            w = a.shape[1] if w is None else w
            in_specs.append(pl.BlockSpec((tm, w), functools.partial(lambda i, cb: (i, cb), cb=cb)))
        else:
            a = d[1]
            in_specs.append(pl.BlockSpec(a.shape, functools.partial(lambda i, nd: (0,) * nd, nd=a.ndim)))
        arrs.append(a)
    out_shape = [jax.ShapeDtypeStruct((rows, w), dt) for w, dt in outs]
    out_specs = [pl.BlockSpec((tm, w), lambda i: (i, 0)) for w, _ in outs]
    for shp, dt in accs:
        out_shape.append(jax.ShapeDtypeStruct(shp, dt))
        out_specs.append(pl.BlockSpec(shp, functools.partial(lambda i, nd: (0,) * nd, nd=len(shp))))
    nin, nout, nacc = len(ins), len(outs), len(accs)

    def body(*refs):
        res = fn(*[r[...] for r in refs[:nin]])
        res = res if isinstance(res, (tuple, list)) else (res,)
        for r, v in zip(refs[nin:nin + nout], res[:nout]):
            r[...] = v.astype(r.dtype)
        if nacc:
            acc_refs = refs[nin + nout:]

            @pl.when(pl.program_id(0) == 0)
            def _():
                for r in acc_refs:
                    r[...] = jnp.zeros_like(r)

            for r, v in zip(acc_refs, res[nout:]):
                r[...] += v

    out = _pc(body, name=name, out_shape=tuple(out_shape), grid=(rows // tm,), in_specs=in_specs,
              out_specs=tuple(out_specs), compiler_params=_cp("arbitrary" if nacc else "parallel"))(*arrs)
    return out[0] if len(out) == 1 else out


def _rms_fwd(x, g, *, name):
    def fn(xv, gv):
        return xv * lax.rsqrt(jnp.mean(xv * xv, axis=-1, keepdims=True) + EPS) * gv
    return _rowmap(fn, [("row", x, None, 0), ("bc", g)], [(x.shape[1], BF16)], rows=x.shape[0], tm=256, name=name)


def _rms_bwd(x, dh, g, dres, *, name):
    D = x.shape[1]

    def fn(xv, dhv, gv, drv):
        r = lax.rsqrt(jnp.mean(xv * xv, axis=-1, keepdims=True) + EPS)
        xh = xv * r
        dhf = dhv.astype(F32)
        dxn = dhf * gv
        dx = drv + r * (dxn - xh * jnp.mean(dxn * xh, axis=-1, keepdims=True))
        return dx, dx, jnp.sum(dhf * xh, axis=0, keepdims=True)

    return _rowmap(fn, [("row", x, None, 0), ("row", dh, None, 0), ("bc", g), ("row", dres, None, 0)],
                   [(D, F32), (D, BF16)], rows=x.shape[0], tm=256, name=name, accs=[((1, D), F32)])


def _loss(y, tgt, *, name):
    D = y.shape[1]

    def fn(yv, tv):
        e = yv - tv
        d = e * (1.0 / D)
        part = 0.5 * jnp.sum(jnp.sum(e * e, axis=-1, keepdims=True) * (1.0 / D), axis=0, keepdims=True)
        return d, d, jnp.broadcast_to(part, (1, LANES))

    return _rowmap(fn, [("row", y, None, 0), ("row", tgt, None, 0)], [(D, F32), (D, BF16)],
                   rows=y.shape[0], tm=256, name=name, accs=[((1, LANES), F32)])


def _merge_fwd(z, yp, yf, ym, *, name):
    D = yp.shape[1]

    def fn(gp, gf, gm, a, b, c):
        return jax.nn.sigmoid(gp) * a + jax.nn.sigmoid(gf) * b + jax.nn.sigmoid(gm) * c

    return _rowmap(fn, [("row", z, D, 0), ("row", z, D, 1), ("row", z, D, 2),
                        ("row", yp, None, 0), ("row", yf, None, 0), ("row", ym, None, 0)],
                   [(D, BF16)], rows=yp.shape[0], tm=128, name=name)


def _merge_bwd(dm, z, yp, yf, ym, *, name):
    D = yp.shape[1]

    def fn(d, gp, gf, gm, a, b, c):
        outs, dgl = [], []
        for gl, yv in ((gp, a), (gf, b), (gm, c)):
            s = jax.nn.sigmoid(gl)
            outs.append(d * s)
            dgl.append(d * yv * (s * (1.0 - s)))
        return (*outs, jnp.concatenate(dgl, axis=-1))

    return _rowmap(fn, [("row", dm, None, 0), ("row", z, D, 0), ("row", z, D, 1), ("row", z, D, 2),
                        ("row", yp, None, 0), ("row", yf, None, 0), ("row", ym, None, 0)],
                   [(D, BF16), (D, BF16), (D, BF16), (3 * D, BF16)], rows=yp.shape[0], tm=128, name=name)


def _adamw(w, g, m, v, *, name):
    C = w.shape[1]

    def fn(wv, gv, mv, vv):
        m2 = ADAM_B1 * mv + (1.0 - ADAM_B1) * gv
        v2 = ADAM_B2 * vv + (1.0 - ADAM_B2) * (gv * gv)
        m_hat = m2 / (1.0 - ADAM_B1 ** ADAM_STEP)
        v_hat = v2 / (1.0 - ADAM_B2 ** ADAM_STEP)
        return -ADAM_LR * (m_hat / (jnp.sqrt(v_hat) + ADAM_EPS) + ADAM_WD * wv), m2, v2

    tm = max(8, (262144 // C) // 8 * 8)
    return _rowmap(fn, [("row", a, None, 0) for a in (w, g, m, v)], [(C, F32)] * 3,
                   rows=w.shape[0], tm=tm, name=name)


def _seg_mean(v, hd):
    if hd == LANES:
        return jnp.mean(v, axis=-1, keepdims=True)
    lo = lax.broadcasted_iota(jnp.int32, v.shape, 1) < hd
    s0 = jnp.sum(jnp.where(lo, v, 0.0), axis=-1, keepdims=True)
    s1 = jnp.sum(jnp.where(lo, 0.0, v), axis=-1, keepdims=True)
    return jnp.where(lo, s0, s1) * (1.0 / hd)


def _gain128(g, hd):
    g = g.reshape(1, hd).astype(F32)
    return g if hd == LANES else jnp.concatenate([g, g], axis=-1)


def _headnorm_fwd(x, cb0, width, gain, hd, *, name):
    rows = x.shape[0]
    tm = _tile(rows, 512, 8)
    g128 = _gain128(gain, hd)

    def body(x_ref, g_ref, o_ref):
        xv = x_ref[...]
        o_ref[...] = (xv * lax.rsqrt(_seg_mean(xv * xv, hd) + EPS) * g_ref[...]).astype(BF16)

    return _pc(body, name=name, out_shape=jax.ShapeDtypeStruct((rows, width), BF16),
               grid=(rows // tm, width // LANES),
               in_specs=[pl.BlockSpec((tm, LANES), lambda i, j: (i, cb0 + j)),
                         pl.BlockSpec((1, LANES), lambda i, j: (0, 0))],
               out_specs=pl.BlockSpec((tm, LANES), lambda i, j: (i, j)),
               compiler_params=_cp("parallel", "parallel"))(x, g128)


def _headnorm_bwd(x, cb0, width, dy, gain, hd, post, spread, out_dtype, *, name):
    rows = x.shape[0]
    tm = _tile(rows, 512, 8)
    g128 = _gain128(gain, hd)
    dyw = 2 * LANES if spread else LANES

    def body(x_ref, dy_ref, g_ref, dx_ref, dg_ref):
        xv = x_ref[...]
        dyv = dy_ref[...]
        if spread:
            dyv = jnp.concatenate([dyv[:, :hd], dyv[:, LANES:LANES + hd]], axis=-1)
        dyv = dyv * post
        r = lax.rsqrt(_seg_mean(xv * xv, hd) + EPS)
        xh = xv * r
        dxn = dyv * g_ref[...]
        dx_ref[...] = (r * (dxn - xh * _seg_mean(dxn * xh, hd))).astype(dx_ref.dtype)

        @pl.when((pl.program_id(0) == 0) & (pl.program_id(1) == 0))
        def _():
            dg_ref[...] = jnp.zeros_like(dg_ref)

        dg_ref[...] += jnp.sum(dyv * xh, axis=0, keepdims=True)

    return _pc(body, name=name,
               out_shape=(jax.ShapeDtypeStruct((rows, width), out_dtype), jax.ShapeDtypeStruct((1, LANES), F32)),
               grid=(rows // tm, width // LANES),
               in_specs=[pl.BlockSpec((tm, LANES), lambda i, j: (i, cb0 + j)),
                         pl.BlockSpec((tm, dyw), lambda i, j: (i, j)),
                         pl.BlockSpec((1, LANES), lambda i, j: (0, 0))],
               out_specs=(pl.BlockSpec((tm, LANES), lambda i, j: (i, j)),
                          pl.BlockSpec((1, LANES), lambda i, j: (0, 0))),
               compiler_params=_cp("arbitrary", "arbitrary"))(x, dy, g128)


def _fold_gain(dg, hd):
    return dg if hd == LANES else dg[:, :hd] + dg[:, hd:]


def _shift_down(v, k, row):
    return jnp.where(row >= k, pltpu.roll(v, k, 0), 0.0)


def _shift_up(v, k, row, S):
    return jnp.where(row < S - k, pltpu.roll(v, S - k, 0), 0.0)


def _window_sum(v, g, row, shift):
    s, outs = v, []
    for k in (1, 2, 4, 8):
        s = s + shift(s, k)
        outs.append(s)
    return jnp.where(g == 0, outs[0], jnp.where(g == 1, outs[1], jnp.where(g == 2, outs[2], outs[3])))


def _pool_fwd(z, cb0, pool_w, pool_scale, nb, S, *, name):
    PG, PD = pool_w.shape[0], pool_w.shape[1]
    assert PD == LANES and PG <= len(POOL_WINDOWS)
    pw = pool_w.astype(BF16)

    def body(u_ref, w_ref, s_ref, o_ref):
        g = pl.program_id(0)
        u = u_ref[...]
        row = lax.broadcasted_iota(jnp.int32, u.shape, 0)
        cnt = jnp.minimum(row + 1, jnp.left_shift(2, g)).astype(F32)
        pooled = _window_sum(u, g, row, lambda v, k: _shift_down(v, k, row)) / cnt
        mixed = jnp.dot((pooled - u).astype(BF16), w_ref[...], preferred_element_type=F32)
        o_ref[...] = (mixed * s_ref[...]).astype(BF16)

    return _pc(body, name=name, out_shape=jax.ShapeDtypeStruct((nb * S, PG * PD), BF16), grid=(PG, nb),
               in_specs=[pl.BlockSpec((S, PD), lambda g, b: (b, cb0 + g)),
                         pl.BlockSpec((None, PD, PD), lambda g, b: (g, 0, 0)),
                         pl.BlockSpec((1, PD), lambda g, b: (0, g))],
               out_specs=pl.BlockSpec((S, PD), lambda g, b: (b, g)),
               compiler_params=_cp("parallel", "parallel"))(z, pw, pool_scale)


def _pool_bwd(z, cb0, dv, pool_w, pool_scale, nb, S, *, name):
    PG, PD = pool_w.shape[0], pool_w.shape[1]
    pw = pool_w.astype(BF16)

    def body(u_ref, dv_ref, w_ref, s_ref, du_ref, dw_ref, ds_ref):
        g, b = pl.program_id(0), pl.program_id(1)
        u, dvv, w = u_ref[...], dv_ref[...], w_ref[...]
        row = lax.broadcasted_iota(jnp.int32, u.shape, 0)
        cnt = jnp.minimum(row + 1, jnp.left_shift(2, g)).astype(F32)
        diff = (_window_sum(u, g, row, lambda v, k: _shift_down(v, k, row)) / cnt - u).astype(BF16)
        mixed = jnp.dot(diff, w, preferred_element_type=F32)
        dmixed = (dvv * s_ref[...]).astype(BF16)
        ddiff = lax.dot_general(dmixed, w, _DIMS["nt"], preferred_element_type=F32)
        du_ref[...] = (_window_sum(ddiff / cnt, g, row, lambda v, k: _shift_up(v, k, row, S)) - ddiff).astype(BF16)

        @pl.when(b == 0)
        def _():
            dw_ref[...] = jnp.zeros_like(dw_ref)
            ds_ref[...] = jnp.zeros_like(ds_ref)

        dw_ref[...] += lax.dot_general(diff, dmixed, _DIMS["tn"], preferred_element_type=F32)
        ds_ref[...] += jnp.sum(dvv * mixed, axis=0, keepdims=True)

    return _pc(body, name=name,
               out_shape=(jax.ShapeDtypeStruct((nb * S, PG * PD), BF16), jax.ShapeDtypeStruct((PG, PD, PD), F32),
                          jax.ShapeDtypeStruct((1, PG * PD), F32)),
               grid=(PG, nb),
               in_specs=[pl.BlockSpec((S, PD), lambda g, b: (b, cb0 + g)),
                         pl.BlockSpec((S, PD), lambda g, b: (b, g)),
                         pl.BlockSpec((None, PD, PD), lambda g, b: (g, 0, 0)),
                         pl.BlockSpec((1, PD), lambda g, b: (0, g))],
               out_specs=(pl.BlockSpec((S, PD), lambda g, b: (b, g)),
                          pl.BlockSpec((None, PD, PD), lambda g, b: (g, 0, 0)),
                          pl.BlockSpec((1, PD), lambda g, b: (0, g))),
               compiler_params=_cp("parallel", "arbitrary"))(z, dv, pw, pool_scale)


def _split3(c):
    hi = c.astype(BF16)
    r1 = c - hi.astype(F32)
    mid = r1.astype(BF16)
    lo = (r1 - mid.astype(F32)).astype(BF16)
    return hi, mid, lo


def _log_sigmoid(v):
    return jnp.minimum(v, 0.0) - jnp.log(1.0 + jnp.exp(-jnp.abs(v)))


def _fox_c(z, cb, b128, nb, S, NF, *, name):
    def body(f_ref, b_ref, hi_ref, mid_ref, lo_ref):
        v = f_ref[...] + b_ref[...]
        lane = lax.broadcasted_iota(jnp.int32, v.shape, 1)
        row = lax.broadcasted_iota(jnp.int32, v.shape, 0)
        c = jnp.where(lane < NF, _log_sigmoid(v), 0.0)
        k = 1
        while k < S:
            c = c + _shift_down(c, k, row)
            k *= 2
        hi, mid, lo = _split3(c)
        hi_ref[...], mid_ref[...], lo_ref[...] = hi, mid, lo

    shp = jax.ShapeDtypeStruct((nb * S, LANES), BF16)
    spec = pl.BlockSpec((S, LANES), lambda i: (i, 0))
    return _pc(body, name=name, out_shape=(shp, shp, shp), grid=(nb,),
               in_specs=[pl.BlockSpec((S, LANES), lambda i: (i, cb)), pl.BlockSpec((1, LANES), lambda i: (0, 0))],
               out_specs=(spec, spec, spec), compiler_params=_cp("parallel"))(z, b128)


def _fox_c_bwd(dc, z, cb, b128, nb, S, *, name):
    def body(dc_ref, f_ref, b_ref, df_ref, db_ref):
        d = dc_ref[...]
        row = lax.broadcasted_iota(jnp.int32, d.shape, 0)
        k = 1
        while k < S:
            d = d + _shift_up(d, k, row, S)
            k *= 2
        df = d * jax.nn.sigmoid(-(f_ref[...] + b_ref[...]))
        df_ref[...] = df

        @pl.when(pl.program_id(0) == 0)
        def _():
            db_ref[...] = jnp.zeros_like(db_ref)

        db_ref[...] += jnp.sum(df, axis=0, keepdims=True)

    return _pc(body, name=name,
               out_shape=(jax.ShapeDtypeStruct((nb * S, LANES), F32), jax.ShapeDtypeStruct((1, LANES), F32)),
               grid=(nb,),
               in_specs=[pl.BlockSpec((S, LANES), lambda i: (i, 0)), pl.BlockSpec((S, LANES), lambda i: (i, cb)),
                         pl.BlockSpec((1, LANES), lambda i: (0, 0))],
               out_specs=(pl.BlockSpec((S, LANES), lambda i: (i, 0)), pl.BlockSpec((1, LANES), lambda i: (0, 0))),
               compiler_params=_cp("arbitrary"))(dc, z, b128)


def _fox_qk(z, cbq, cbk, cbv, c3, gq, gk, HD, NF, *, name):
    T = z.shape[0]
    HP = NF * HD // LANES
    tm = _tile(T, 512, 8)
    scale = HD ** -0.5
    gq128, gk128 = _gain128(gq, HD), _gain128(gk, HD)

    def body(q_ref, k_ref, v_ref, hi_ref, mid_ref, lo_ref, gq_ref, gk_ref, qa_ref, ka_ref, vb_ref):
        hp = pl.program_id(1)
        q, k = q_ref[...], k_ref[...]
        qn = q * lax.rsqrt(_seg_mean(q * q, HD) + EPS) * (gq_ref[...] * scale)
        kn = k * lax.rsqrt(_seg_mean(k * k, HD) + EPS) * gk_ref[...]
        vb_ref[...] = v_ref[...].astype(BF16)
        c3v = jnp.concatenate([hi_ref[...], mid_ref[...], lo_ref[...]], axis=-1)
        r = lax.broadcasted_iota(jnp.int32, (3 * LANES, LANES - HD), 0)
        cc = lax.broadcasted_iota(jnp.int32, (3 * LANES, LANES - HD), 1)
        lane = lax.broadcasted_iota(jnp.int32, (tm, LANES - HD), 1)
        for hh in range(LANES // HD):
            head = hp * (LANES // HD) + hh
            sel_q = jnp.where((cc < 3) & (r == head + LANES * cc), 1.0, 0.0).astype(BF16)
            sel_k = jnp.where((cc >= 3) & (cc < 6) & (r == head + LANES * (cc - 3)), 1.0, 0.0).astype(BF16)
            qaug = jnp.dot(c3v, sel_q, preferred_element_type=F32) + jnp.where((lane >= 3) & (lane < 6), 1.0, 0.0)
            kaug = jnp.where(lane < 3, 1.0, 0.0) - jnp.dot(c3v, sel_k, preferred_element_type=F32)
            sl = slice(hh * HD, (hh + 1) * HD)
            qa_ref[:, hh * LANES:(hh + 1) * LANES] = jnp.concatenate([qn[:, sl], qaug], axis=-1).astype(BF16)
            ka_ref[:, hh * LANES:(hh + 1) * LANES] = jnp.concatenate([kn[:, sl], kaug], axis=-1).astype(BF16)

    aw = (LANES // HD) * LANES
    blk = lambda cb: pl.BlockSpec((tm, LANES), functools.partial(lambda i, j, cb: (i, cb + j), cb=cb))
    cspec = pl.BlockSpec((tm, LANES), lambda i, j: (i, 0))
    gspec = pl.BlockSpec((1, LANES), lambda i, j: (0, 0))
    return _pc(body, name=name,
               out_shape=(jax.ShapeDtypeStruct((T, HP * aw), BF16), jax.ShapeDtypeStruct((T, HP * aw), BF16),
                          jax.ShapeDtypeStruct((T, NF * HD), BF16)),
               grid=(T // tm, HP),
               in_specs=[blk(cbq), blk(cbk), blk(cbv), cspec, cspec, cspec, gspec, gspec],
               out_specs=(pl.BlockSpec((tm, aw), lambda i, j: (i, j)), pl.BlockSpec((tm, aw), lambda i, j: (i, j)),
                          pl.BlockSpec((tm, LANES), lambda i, j: (i, j))),
               compiler_params=_cp("parallel", "parallel"))(z, z, z, *c3, gq128, gk128)


def _fox_probs(qa, ka, q0):
    s = lax.dot_general(qa, ka, _DIMS["nt"], preferred_element_type=F32)
    row = lax.broadcasted_iota(jnp.int32, s.shape, 0) + q0
    col = lax.broadcasted_iota(jnp.int32, s.shape, 1)
    s = jnp.where(col <= row, s, NEG)
    e = jnp.exp(s - jnp.max(s, axis=-1, keepdims=True))
    return e / jnp.sum(e, axis=-1, keepdims=True)


def _fox_fwd(qa, ka, vb, nb, S, HD, *, name, tq=256, cargo=None):
    T = qa.shape[0]
    nh = LANES // HD
    aw = nh * LANES
    HP = qa.shape[1] // aw
    tq = _tile(S, tq, 8)
    nq = S // tq

    def body(qa_ref, ka_ref, v_ref, o_ref):
        for k in range(nq):
            @pl.when(pl.program_id(2) == k)
            def _(k=k):
                L = (k + 1) * tq
                outs = []
                v = v_ref[0:L, :]
                for hh in range(nh):
                    al = slice(hh * LANES, (hh + 1) * LANES)
                    p = _fox_probs(qa_ref[:, al], ka_ref[0:L, al], k * tq)
                    outs.append(jnp.dot(p.astype(BF16), v[:, hh * HD:(hh + 1) * HD], preferred_element_type=F32))
                o_ref[...] = jnp.concatenate(outs, axis=-1).astype(BF16)

    out = _pc(body, name=name, cargo=cargo, out_shape=jax.ShapeDtypeStruct((T, HP * LANES), BF16),
              grid=(nb, HP, nq),
              in_specs=[pl.BlockSpec((tq, aw), lambda b, h, i: (b * nq + i, h)),
                        pl.BlockSpec((S, aw), lambda b, h, i: (b, h)),
                        pl.BlockSpec((S, LANES), lambda b, h, i: (b, h))],
              out_specs=pl.BlockSpec((tq, LANES), lambda b, h, i: (b * nq + i, h)),
              compiler_params=_cp("parallel", "parallel", "arbitrary"))(qa, ka, vb, *(cargo.arrays if cargo else ()))
    return _split_cargo(out, cargo)


def _fox_bwd(qa, ka, vb, do, nb, S, HD, *, name, tq=256):
    T = qa.shape[0]
    nh = LANES // HD
    aw = nh * LANES
    HP = qa.shape[1] // aw
    tq = _tile(S, tq, 8)
    nq = S // tq

    def body(qa_ref, ka_ref, v_ref, do_ref, dqa_ref, dka_ref, dv_ref, dcs_ref):
        i = pl.program_id(2)

        @pl.when(i == 0)
        def _():
            dka_ref[...] = jnp.zeros_like(dka_ref)
            dv_ref[...] = jnp.zeros_like(dv_ref)
            dcs_ref[...] = jnp.zeros_like(dcs_ref)

        for k in range(nq):
            @pl.when(i == k)
            def _(k=k):
                L = (k + 1) * tq
                dkas, dvs, css = [], [], []
                v, dob = v_ref[0:L, :], do_ref[...]
                for hh in range(nh):
                    al = slice(hh * LANES, (hh + 1) * LANES)
                    hl = slice(hh * HD, (hh + 1) * HD)
                    qav, kav, vv, dov = qa_ref[:, al], ka_ref[0:L, al], v[:, hl], dob[:, hl]
                    p = _fox_probs(qav, kav, k * tq)
                    dp = lax.dot_general(dov, vv, _DIMS["nt"], preferred_element_type=F32)
                    dsf = p * (dp - jnp.sum(p * dp, axis=-1, keepdims=True))
                    css.append(jnp.sum(dsf, axis=0, keepdims=True))
                    ds = dsf.astype(BF16)
                    dqa_ref[:, al] = jnp.dot(ds, kav, preferred_element_type=F32)
                    dkas.append(lax.dot_general(ds, qav, _DIMS["tn"], preferred_element_type=F32))
                    dvs.append(lax.dot_general(p.astype(BF16), dov, _DIMS["tn"], preferred_element_type=F32))
                dka_ref[0:L, :] += jnp.concatenate(dkas, axis=-1)
                dv_ref[0:L, :] += jnp.concatenate(dvs, axis=-1)
                dcs_ref[:, 0:L] += jnp.concatenate(css + [jnp.zeros((8 - nh, L), F32)], axis=0)

    return _pc(body, name=name,
               out_shape=(jax.ShapeDtypeStruct((T, HP * aw), F32), jax.ShapeDtypeStruct((T, HP * aw), F32),
                          jax.ShapeDtypeStruct((T, HP * LANES), F32), jax.ShapeDtypeStruct((nb, HP, 8, S), F32)),
               grid=(nb, HP, nq),
               in_specs=[pl.BlockSpec((tq, aw), lambda b, h, i: (b * nq + i, h)),
                         pl.BlockSpec((S, aw), lambda b, h, i: (b, h)),
                         pl.BlockSpec((S, LANES), lambda b, h, i: (b, h)),
                         pl.BlockSpec((tq, LANES), lambda b, h, i: (b * nq + i, h))],
               out_specs=(pl.BlockSpec((tq, aw), lambda b, h, i: (b * nq + i, h)),
                          pl.BlockSpec((S, aw), lambda b, h, i: (b, h)),
                          pl.BlockSpec((S, LANES), lambda b, h, i: (b, h)),
                          pl.BlockSpec((None, None, 8, S), lambda b, h, i: (b, h, 0, 0))),
               compiler_params=_cp("parallel", "parallel", "arbitrary"))(qa, ka, vb, do)


def _mem_probs(q, k, scale):
    s = lax.dot_general(q, k, _DIMS["nt"], preferred_element_type=F32) * scale
    e = jnp.exp(s - jnp.max(s, axis=-1, keepdims=True))
    return e / jnp.sum(e, axis=-1, keepdims=True)


def _mem_fwd(q, k, v, nb, S, ML, *, name, tq=512):
    T, MW = q.shape
    nh = MW // LANES
    tq = _tile(S, tq, 8)
    nq = S // tq
    scale = LANES ** -0.5

    def body(q_ref, k_ref, v_ref, o_ref):
        for h in range(nh):
            sl = slice(h * LANES, (h + 1) * LANES)
            p = _mem_probs(q_ref[:, sl], k_ref[:, sl], scale)
            o_ref[:, sl] = jnp.dot(p.astype(BF16), v_ref[:, sl], preferred_element_type=F32).astype(BF16)

    return _pc(body, name=name, out_shape=jax.ShapeDtypeStruct((T, MW), BF16), grid=(nb, nq),
               in_specs=[pl.BlockSpec((tq, MW), lambda b, i: (b * nq + i, 0)),
                         pl.BlockSpec((ML, MW), lambda b, i: (b, 0)), pl.BlockSpec((ML, MW), lambda b, i: (b, 0))],
               out_specs=pl.BlockSpec((tq, MW), lambda b, i: (b * nq + i, 0)),
               compiler_params=_cp("parallel", "arbitrary"))(q, k, v)


def _mem_bwd(q, k, v, do, nb, S, ML, *, name, tq=512):
    T, MW = q.shape
    nh = MW // LANES
    tq = _tile(S, tq, 8)
    nq = S // tq
    scale = LANES ** -0.5

    def body(q_ref, k_ref, v_ref, do_ref, dq_ref, dk_ref, dv_ref):
        i = pl.program_id(1)

        @pl.when(i == 0)
        def _():
            dk_ref[...] = jnp.zeros_like(dk_ref)
            dv_ref[...] = jnp.zeros_like(dv_ref)

        for h in range(nh):
            sl = slice(h * LANES, (h + 1) * LANES)
            qv, kv, vv, dov = q_ref[:, sl], k_ref[:, sl], v_ref[:, sl], do_ref[:, sl]
            p = _mem_probs(qv, kv, scale)
            dp = lax.dot_general(dov, vv, _DIMS["nt"], preferred_element_type=F32)
            ds = (p * (dp - jnp.sum(p * dp, axis=-1, keepdims=True)) * scale).astype(BF16)
            dq_ref[:, sl] = jnp.dot(ds, kv, preferred_element_type=F32)
            dk_ref[:, sl] += lax.dot_general(ds, qv, _DIMS["tn"], preferred_element_type=F32)
            dv_ref[:, sl] += lax.dot_general(p.astype(BF16), dov, _DIMS["tn"], preferred_element_type=F32)

    kvspec = pl.BlockSpec((ML, MW), lambda b, i: (b, 0))
    qspec = pl.BlockSpec((tq, MW), lambda b, i: (b * nq + i, 0))
    return _pc(body, name=name,
               out_shape=(jax.ShapeDtypeStruct((T, MW), F32), jax.ShapeDtypeStruct((nb * ML, MW), F32),
                          jax.ShapeDtypeStruct((nb * ML, MW), F32)),
               grid=(nb, nq), in_specs=[qspec, kvspec, kvspec, qspec], out_specs=(qspec, kvspec, kvspec),
               compiler_params=_cp("parallel", "arbitrary"))(q, k, v, do)


def _place():
    x, y, c = lax.axis_index("x"), lax.axis_index("y"), lax.axis_index("c")
    chips = [(1 - x, y), (x, 1 - y), (1 - x, 1 - y)]
    return x, y, c, chips


def _shard_win(ref, kind, R, C, j, h=None):
    if kind == "col":
        rows = pl.ds(0, R) if h is None else pl.ds(pl.multiple_of(h * (R // 2), 16), R // 2)
        return ref.at[rows, pl.ds(pl.multiple_of(j * C, LANES), C)]
    if h is None:
        return ref.at[pl.ds(pl.multiple_of(j * R, 16), R), :]
    return ref.at[pl.ds(pl.multiple_of(j * R + h * (R // 2), 16), R // 2), :]


def _any_specs(n):
    return [pl.BlockSpec(memory_space=pl.ANY)] * n


class _Cargo:
    def __init__(self, arrays, out_shapes, ncopies, copies, alias=None):
        self.arrays, self.out_shapes, self.ncopies, self.copies = list(arrays), list(out_shapes), ncopies, copies
        self.alias = alias or {}


def _load_cargo(body, cargo, kw):
    as_list = lambda v: list(v) if isinstance(v, (tuple, list)) else [v]
    grid = kw.get("grid", ())
    in_specs, out_specs, out_shape = as_list(kw["in_specs"]), as_list(kw["out_specs"]), as_list(kw["out_shape"])
    scratch = list(kw.get("scratch_shapes", ()))
    nin, nout, nci, nco, nscr = len(in_specs), len(out_specs), len(cargo.arrays), len(cargo.out_shapes), len(scratch)

    def loaded(*refs):
        ins, cin = refs[:nin], refs[nin:nin + nci]
        outs, cout = refs[nin + nci:nin + nci + nout], refs[nin + nci + nout:nin + nci + nout + nco]
        scr, (ssem, rsem) = refs[nin + nci + nout + nco:-2], refs[-2:]
        first = last = True
        for a, g in enumerate(grid):
            first = first & (pl.program_id(a) == 0)
            last = last & (pl.program_id(a) == g - 1)

        def start():
            for cp in cargo.copies(cin, cout, ssem, rsem, False):
                cp.start()

        def wait():
            for cp in cargo.copies(cin, cout, ssem, rsem, True):
                cp.wait_send()
                cp.wait_recv()

        if grid:
            pl.when(first)(start)
        else:
            start()
        body(*ins, *outs, *scr)
        if grid:
            pl.when(last)(wait)
        else:
            wait()

    kw = dict(kw, in_specs=in_specs + _any_specs(nci), out_specs=tuple(out_specs + _any_specs(nco)),
              out_shape=tuple(out_shape + cargo.out_shapes),
              scratch_shapes=scratch + [pltpu.SemaphoreType.DMA((cargo.ncopies,))] * 2)
    if cargo.alias:
        kw["input_output_aliases"] = {nin + i: nout + o for i, o in cargo.alias.items()}
    if grid:
        kw["compiler_params"] = _cp(*["arbitrary"] * len(grid))
    return loaded, kw


class _SemView:
    def __init__(self, ref, off):
        self.ref, self.off, self.at = ref, off, self

    def __getitem__(self, k):
        return self.ref.at[self.off + k]


def _join_cargo(a, b):
    na, nao = len(a.arrays), len(a.out_shapes)

    def copies(ins, outs, ssem, rsem, waiting):
        return (a.copies(ins[:na], outs[:nao], ssem, rsem, waiting) +
                b.copies(ins[na:], outs[nao:], _SemView(ssem, a.ncopies), _SemView(rsem, a.ncopies), waiting))

    alias = dict(a.alias)
    alias.update({na + i: nao + o for i, o in b.alias.items()})
    return _Cargo(a.arrays + b.arrays, a.out_shapes + b.out_shapes, a.ncopies + b.ncopies, copies, alias)


def _run_cargo(cargo, *, name):
    res = _pc(lambda: None, name=name, cargo=cargo, in_specs=[], out_specs=[], out_shape=[])(*cargo.arrays)
    return list(res)


def _full_shape(dims, kind):
    return (dims[0], 4 * dims[1]) if kind == "col" else (4 * dims[0], dims[1])


def _ag_copy(shard, full, kind, j_src, chip, c, ssem, rsem, piece):
    R, C = shard.shape
    p, npieces = piece
    rows = R // 2 // npieces
    row0 = pl.multiple_of(c * (R // 2) + p * rows, 16)
    if kind == "col":
        dst = full.at[pl.ds(row0, rows), pl.ds(pl.multiple_of(j_src * C, LANES), C)]
    else:
        dst = full.at[pl.ds(pl.multiple_of(j_src * R + row0, 16), rows), :]
    return pltpu.make_async_remote_copy(shard.at[pl.ds(row0, rows), :], dst, ssem, rsem,
                                        device_id=(*chip, c), device_id_type=MESH)


def _ag_cargo(shards, kinds, piece=(0, 1), lands=None):
    n = len(shards)

    def copies(ins, outs, ssem, rsem, waiting):
        x, y, c, chips = _place()
        return [_ag_copy(ins[w], outs[w], kinds[w], 2 * chip[0] + chip[1] if waiting else 2 * x + y, chip, c,
                         ssem.at[3 * w + q], rsem.at[3 * w + q], piece)
                for w in range(n) for q, chip in enumerate(chips)]

    shapes = [jax.ShapeDtypeStruct(_full_shape(s.shape, k), BF16) for s, k in zip(shards, kinds)]
    if lands is None:
        return _Cargo(shards, shapes, 3 * n, copies)
    return _Cargo(list(shards) + list(lands), shapes, 3 * n, copies, alias={n + w: w for w in range(n)})


def _scatter_cargo(parts, kinds):
    pdims = [((p.shape[0], p.shape[1] // 4) if k == "col" else (p.shape[0] // 4, p.shape[1]))
             for p, k in zip(parts, kinds)]

    def copies(ins, outs, ssem, rsem, waiting):
        x, y, c, chips = _place()
        return [pltpu.make_async_remote_copy(_shard_win(ins[w], kinds[w], *pdims[w], 2 * cx + cy), outs[w].at[q],
                                             ssem.at[3 * w + q], rsem.at[3 * w + q],
                                             device_id=(cx, cy, c), device_id_type=MESH)
                for w in range(len(parts)) for q, (cx, cy) in enumerate(chips)]

    return _Cargo(parts, [jax.ShapeDtypeStruct((3,) + d, BF16) for d in pdims], 3 * len(parts), copies)


def _ag_fill(full, shard, kind, *, name):
    R, C = shard.shape
    RH = R // 2
    tm = _tile(RH, max(16, (3 * 2 ** 19 // C) // 16 * 16), 16)
    nbh = RH // tm
    n, nown = 3 * nbh, 2 * nbh
    xyc = jnp.stack([lax.axis_index("x"), lax.axis_index("y"), lax.axis_index("c")]).astype(jnp.int32)

    def chip_of(q, x, y):
        keep_x, keep_y = q % 2, (2 - q) // 2
        return 2 * ((1 - x) + keep_x * (2 * x - 1)) + (1 - y) + keep_y * (2 * y - 1)

    def window(ref, j, row0):
        if kind == "col":
            return ref.at[pl.ds(pl.multiple_of(row0, 16), tm), pl.ds(pl.multiple_of(j * C, LANES), C)]
        return ref.at[pl.ds(pl.multiple_of(j * R + row0, 16), tm), :]

    def body(p_ref, send_ref, own_ref, out_ref, land, ssem, rsem, wsem):
        i = pl.program_id(0)
        x, y = lax.axis_index("x"), lax.axis_index("y")
        c, peer = _pair_peer()
        own_cp = pltpu.make_async_copy(own_ref, window(out_ref, 2 * x + y, jnp.minimum(i, nown - 1) * tm), wsem.at[1])

        @pl.when(i < nown)
        def _():
            own_cp.start()

        def consume(land, ps):
            b = i - 1
            cp = pltpu.make_async_copy(land.at[ps], window(out_ref, chip_of(b // nbh, x, y),
                                                           (1 - c) * RH + (b % nbh) * tm), wsem.at[0])
            cp.start()
            cp.wait()

        _exchange_step(i, n, send_ref, land, ssem, rsem, None, peer, consume)

        @pl.when(i < nown)
        def _():
            own_cp.wait()

    now = lambda i: jnp.minimum(i, n - 1)

    def send_map(i, p):
        q, b = now(i) // nbh, now(i) % nbh
        j = chip_of(q, p[0], p[1])
        return (p[2] * nbh + b, j) if kind == "col" else ((j * 2 + p[2]) * nbh + b, 0)

    grid_spec = pltpu.PrefetchScalarGridSpec(
        num_scalar_prefetch=1, grid=(n + 1,),
        in_specs=[pl.BlockSpec((tm, C), send_map),
                  pl.BlockSpec((tm, C), lambda i, p: (jnp.minimum(i, nown - 1), 0))],
        out_specs=pl.BlockSpec(memory_space=pl.ANY),
        scratch_shapes=[pltpu.VMEM((n, tm, C), BF16), pltpu.SemaphoreType.DMA((n,)), pltpu.SemaphoreType.DMA((n,)),
                        pltpu.SemaphoreType.DMA((2,))])
    return _pc(body, name=name, out_shape=jax.ShapeDtypeStruct(full.shape, BF16), grid_spec=grid_spec,
               input_output_aliases={1: 0}, compiler_params=_cp("arbitrary"))(xyc, full, shard)


def _pair_peer():
    x, y, c = lax.axis_index("x"), lax.axis_index("y"), lax.axis_index("c")
    return c, (x, y, 1 - c)


def _exchange_step(i, n, src_ref, land, ssem, rsem, credit, peer, consume):
    def rdma(slot):
        return pltpu.make_async_remote_copy(src_ref, land.at[slot], ssem.at[slot], rsem.at[slot],
                                            device_id=peer, device_id_type=MESH)

    reuse = credit is not None
    slot = i % 2 if reuse else jnp.minimum(i, n - 1)

    @pl.when(i < n)
    def _():
        if reuse:
            @pl.when(i >= 2)
            def _():
                pl.semaphore_wait(credit, 1)

        rdma(slot).start()

    @pl.when(i >= 1)
    def _():
        ps = (i - 1) % 2 if reuse else i - 1
        rdma(ps).wait_recv()
        consume(land, ps)

        if reuse:
            @pl.when(i - 1 < n - 2)
            def _():
                pl.semaphore_signal(credit, 1, device_id=peer, device_id_type=MESH)

    @pl.when(i < n)
    def _():
        rdma(slot).wait_send()


def _xchg_scratch(tm, C, dtype):
    return [pltpu.VMEM((2, tm, C), dtype), pltpu.SemaphoreType.DMA((2,)), pltpu.SemaphoreType.DMA((2,)),
            pltpu.SemaphoreType.REGULAR]


def _rs_pair_sum(g, kind, *, name):
    K, N = g.shape
    NS, RH = (1, K // 2) if kind == "col" else (4, K // 8)
    tm = _tile(RH, max(16, (3 * 2 ** 19 // N) // 16 * 16), 16)
    if NS * (RH // tm) < 2:
        tm = RH // 2
    nb = RH // tm
    n = NS * nb
    cvec = lax.axis_index("c").reshape(1).astype(jnp.int32)

    def body(c_ref, send_ref, own_ref, out_ref, land, ssem, rsem, credit):
        _, peer = _pair_peer()

        def consume(land, ps):
            out_ref[...] = (own_ref[...].astype(F32) + land[ps].astype(F32)).astype(BF16)

        _exchange_step(pl.program_id(0), n, send_ref, land, ssem, rsem, credit, peer, consume)

    now = lambda i: jnp.minimum(i, n - 1)
    lag = lambda i: jnp.maximum(i - 1, 0)
    grid_spec = pltpu.PrefetchScalarGridSpec(
        num_scalar_prefetch=1, grid=(n + 1,),
        in_specs=[pl.BlockSpec((tm, N), lambda i, c: ((now(i) // nb * 2 + 1 - c[0]) * nb + now(i) % nb, 0)),
                  pl.BlockSpec((tm, N), lambda i, c: ((lag(i) // nb * 2 + c[0]) * nb + lag(i) % nb, 0))],
        out_specs=pl.BlockSpec((tm, N), lambda i, c: (lag(i), 0)),
        scratch_shapes=_xchg_scratch(tm, N, BF16))
    return _pc(body, name=name, out_shape=jax.ShapeDtypeStruct((NS * RH, N), BF16), grid_spec=grid_spec,
               compiler_params=_cp("arbitrary"))(cvec, g, g)


def _rs_sum_join(part, got, kind, *, name):
    _, hr, hc = got.shape
    tm = _tile(hr, max(16, (3 * 2 ** 18 // hc) // 16 * 16), 16)
    if hr // tm < 2:
        tm = hr // 2
    nb = hr // tm
    jvec = (2 * lax.axis_index("x") + lax.axis_index("y")).reshape(1).astype(jnp.int32)

    def body(j_ref, p_ref, a_ref, b_ref, c_ref, mine_ref, theirs_ref, sbuf, land, ssem, rsem, credit):
        i = pl.program_id(0)
        _, peer = _pair_peer()

        @pl.when(i < nb)
        def _():
            v = ((p_ref[...].astype(F32) + a_ref[...].astype(F32)) + b_ref[...].astype(F32)) + c_ref[...].astype(F32)
            mine_ref[...] = v
            sbuf[...] = v

        def consume(land, ps):
            theirs_ref[...] = land[ps]

        _exchange_step(i, nb, sbuf, land, ssem, rsem, credit, peer, consume)

    now = lambda i: jnp.minimum(i, nb - 1)
    lag = lambda i: jnp.maximum(i - 1, 0)
    own_spec = (pl.BlockSpec((tm, hc), lambda i, j: (now(i), j[0])) if kind == "col"
                else pl.BlockSpec((tm, hc), lambda i, j: (j[0] * nb + now(i), 0)))
    got_spec = lambda q: pl.BlockSpec((None, tm, hc), functools.partial(lambda i, j, q: (q, now(i), 0), q=q))
    grid_spec = pltpu.PrefetchScalarGridSpec(
        num_scalar_prefetch=1, grid=(nb + 1,),
        in_specs=[own_spec, got_spec(0), got_spec(1), got_spec(2)],
        out_specs=(pl.BlockSpec((tm, hc), lambda i, j: (now(i), 0)), pl.BlockSpec((tm, hc), lambda i, j: (lag(i), 0))),
        scratch_shapes=[pltpu.VMEM((tm, hc), F32)] + _xchg_scratch(tm, hc, F32))
    shp = jax.ShapeDtypeStruct((hr, hc), F32)
    return _pc(body, name=name, out_shape=(shp, shp), grid_spec=grid_spec,
               compiler_params=_cp("arbitrary"))(jvec, part, got, got, got)


def _adamw_join(w, mine, theirs, m, v, *, name):
    R2, wc = w.shape
    hr, hc = mine.shape
    assert hr < R2 <= 2 * hr and hc >= wc
    pref, rows8 = max(8, 2 ** 18 // hc), _round_up(R2, 8)
    tm = max(t for t in range(8, hr + 1, 8) if hr % t == 0 and rows8 % t == 0 and (t <= pref or t == 8))
    nbh = hr // tm
    cvec = lax.axis_index("c").reshape(1).astype(jnp.int32)

    def body(c_ref, w_ref, a_ref, b_ref, m_ref, v_ref, g_ref, d_ref, m2_ref, v2_ref):
        gv = jnp.where(pl.program_id(0) // nbh == c_ref[0], a_ref[...], b_ref[...])[:, :wc]
        m2 = ADAM_B1 * m_ref[...] + (1.0 - ADAM_B1) * gv
        v2 = ADAM_B2 * v_ref[...] + (1.0 - ADAM_B2) * (gv * gv)
        m_hat = m2 / (1.0 - ADAM_B1 ** ADAM_STEP)
        v_hat = v2 / (1.0 - ADAM_B2 ** ADAM_STEP)
        g_ref[...] = gv
        d_ref[...] = -ADAM_LR * (m_hat / (jnp.sqrt(v_hat) + ADAM_EPS) + ADAM_WD * w_ref[...])
        m2_ref[...] = m2
        v2_ref[...] = v2

    wspec = pl.BlockSpec((tm, wc), lambda i, c: (i, 0))
    mine_spec = pl.BlockSpec((tm, hc), lambda i, c: (jnp.where(i // nbh == c[0], i % nbh, 0), 0))
    theirs_spec = pl.BlockSpec((tm, hc), lambda i, c: (jnp.where(i // nbh == c[0], 0, i % nbh), 0))
    grid_spec = pltpu.PrefetchScalarGridSpec(
        num_scalar_prefetch=1, grid=(pl.cdiv(R2, tm),),
        in_specs=[wspec, mine_spec, theirs_spec, wspec, wspec], out_specs=(wspec,) * 4)
    return _pc(body, name=name, out_shape=(jax.ShapeDtypeStruct((R2, wc), F32),) * 4, grid_spec=grid_spec,
               compiler_params=_cp("parallel"))(cvec, w, mine, theirs, m, v)


def _all_reduce_small(pack):
    R = pack.shape[0]

    def body(p_ref, o_ref, buf, send, recv):
        x, y, c, _ = _place()
        me = 4 * x + 2 * y + c
        buf[0] = p_ref[...]
        cps = []
        for r in range(1, 8):
            peer = (x if r & 4 == 0 else 1 - x, y if r & 2 == 0 else 1 - y, c if r & 1 == 0 else 1 - c)
            cp = pltpu.make_async_remote_copy(p_ref, buf.at[r], send.at[r - 1], recv.at[r - 1],
                                              device_id=peer, device_id_type=MESH)
            cp.start()
            cps.append(cp)
        for cp in cps:
            cp.wait()
        acc = buf[jnp.bitwise_xor(me, 0)]
        for k in range(1, 8):
            acc = acc + buf[jnp.bitwise_xor(me, k)]
        o_ref[...] = acc

    return _pc(body, name="allreduce_small", out_shape=jax.ShapeDtypeStruct((R, LANES), F32),
               in_specs=[pl.BlockSpec(memory_space=pltpu.VMEM)], out_specs=pl.BlockSpec(memory_space=pltpu.VMEM),
               scratch_shapes=[pltpu.VMEM((8, R, LANES), F32), pltpu.SemaphoreType.DMA((7,)),
                               pltpu.SemaphoreType.DMA((7,))])(pack)


def _pack_small(arrs):
    parts, sizes = [], []
    for a in arrs:
        f = a.reshape(-1).astype(F32)
        n = _round_up(f.shape[0], 8 * LANES)
        parts.append(jnp.pad(f, (0, n - f.shape[0])).reshape(n // LANES, LANES))
        sizes.append((a.shape, f.shape[0], n // LANES))
    return jnp.concatenate(parts, axis=0), sizes


def _unpack_small(pack, sizes):
    out, r0 = [], 0
    for shape, cnt, rows in sizes:
        out.append(pack[r0:r0 + rows].reshape(-1)[:cnt].reshape(shape))
        r0 += rows
    return out


def _ffn_fwd(xin, norm, weight, tag, gather_on_gateup, gather_on_down):
    hb = _rms_fwd(xin, norm, name=tag + "_norm")
    cargo, landed = gather_on_gateup()
    res = _gateup(hb, weight("w_gate_up"), name=tag + "_gateup", tm=1024, cargo=cargo)
    if cargo is not None:
        res, lands = res
        landed(lands)
    gate, up, act = res
    cargo, landed = gather_on_down()
    xout = _mm(act, weight("w_down"), res=xin, scale=0.5, name=tag + "_down", tm=512, tn=512, tk=8192, cargo=cargo)
    if cargo is not None:
        xout, lands = xout
        landed(lands)
    return xout, (hb, gate, up, act)


def _ffn_bwd(dy, dyb, xin, norm, wgu, wd, saved, tag, scatter):
    hb, gate, up, act = saved
    d_wd = _mm(act, dyb, mode="tn", scale=0.5, out_dtype=BF16, name=tag + "_dwd", tm=512, tn=1024, tk=4096)
    cargo = scatter(tag + "_w_down", d_wd)
    dgu = _dact_swiglu(dyb, wd, gate, up, name=tag + "_dact")
    d_wgu, lands = _mm(hb, dgu, mode="tn", halves="b", out_dtype=BF16, name=tag + "_dwgu", tm=512, tn=512, tk=4096,
                       cargo=cargo)
    scatter.done(lands)
    cargo = scatter(tag + "_w_gate_up", d_wgu)
    dh, lands = _mm(dgu, wgu, mode="nt", halves="a", out_dtype=BF16, name=tag + "_dh", tm=1024, tn=512, tk=2816,
                    cargo=cargo)
    scatter.done(lands)
    dx, dxb, d_norm = _rms_bwd(xin, dh, norm, dy, name=tag + "_dnorm")
    return dx, dxb, d_norm


def kernel(x, mem, ffn1_norm, ffn1_w_gate_up, ffn1_w_down, mix_norm, mem_norm, w_in, b_forget, pool_w, pool_scale, w_pool_up, fox_q_norm, fox_k_norm, w_fox_o, w_mem_kv, mem_q_norm, mem_k_norm, w_mem_o, w_out, ffn2_norm, ffn2_w_gate_up, ffn2_w_down, loss_target, m_ffn1_norm, m_ffn1_w_gate_up, m_ffn1_w_down, m_mix_norm, m_mem_norm, m_w_in, m_b_forget, m_pool_w, m_pool_scale, m_w_pool_up, m_fox_q_norm, m_fox_k_norm, m_w_fox_o, m_w_mem_kv, m_mem_q_norm, m_mem_k_norm, m_w_mem_o, m_w_out, m_ffn2_norm, m_ffn2_w_gate_up, m_ffn2_w_down, v_ffn1_norm, v_ffn1_w_gate_up, v_ffn1_w_down, v_mix_norm, v_mem_norm, v_w_in, v_b_forget, v_pool_w, v_pool_scale, v_w_pool_up, v_fox_q_norm, v_fox_k_norm, v_w_fox_o, v_w_mem_kv, v_mem_q_norm, v_mem_k_norm, v_w_mem_o, v_w_out, v_ffn2_norm, v_ffn2_w_gate_up, v_ffn2_w_down):
    P = dict(locals())
    big = ["ffn1_w_gate_up", "ffn1_w_down", "w_in", "w_pool_up", "w_fox_o", "w_mem_kv", "w_mem_o", "w_out",
           "ffn2_w_gate_up", "ffn2_w_down"]
    kinds = ["col", "row", "col", "col", "col", "row", "col", "row", "col", "row"]
    small = ["ffn1_norm", "mix_norm", "mem_norm", "b_forget", "pool_w", "pool_scale", "fox_q_norm", "fox_k_norm",
             "mem_q_norm", "mem_k_norm", "ffn2_norm"]
    order = ["ffn1_norm", "ffn1_w_gate_up", "ffn1_w_down", "mix_norm", "mem_norm", "w_in", "b_forget", "pool_w",
             "pool_scale", "w_pool_up", "fox_q_norm", "fox_k_norm", "w_fox_o", "w_mem_kv", "mem_q_norm", "mem_k_norm",
             "w_mem_o", "w_out", "ffn2_norm", "ffn2_w_gate_up", "ffn2_w_down"]

    nb, S, D = x.shape
    T = nb * S
    ML = mem.shape[1]
    NF, HD = b_forget.shape[-1], fox_q_norm.shape[-1]
    FW = NF * HD
    MW, MHD = w_mem_o.shape[1], mem_q_norm.shape[-1]
    PG, PD = pool_w.shape[1], pool_w.shape[2]
    PW = PG * PD
    n_in = w_in.shape[-1]
    n_in_pad = _round_up(n_in, LANES)
    in_w = 4 * n_in
    assert HD * 2 == LANES and MHD == LANES and PD == LANES and in_w == PW + 3 * FW + NF + MW + 3 * D

    shards = []
    for nme in big:
        wl = P[nme][0].astype(BF16)
        if nme == "w_in":
            wl = jnp.pad(wl, ((0, 0), (0, n_in_pad - n_in)))
        shards.append(wl)
    shard, kind_of, full = dict(zip(big, shards)), dict(zip(big, kinds)), {}

    def landed(names):
        def fill(lands):
            for nme, l in zip(names, lands):
                full[nme] = _ag_fill(l, shard[nme], kind_of[nme], name="ag_fill_" + nme)
        return fill

    def gather(names, piece=(0, 1), lands=None):
        return _ag_cargo([shard[nme] for nme in names], [kind_of[nme] for nme in names], piece, lands)

    landed(["ffn1_w_gate_up"])(_run_cargo(gather(["ffn1_w_gate_up"]), name="ag_first"))
    begun = {}

    def on_gateup1():
        def fill(lands):
            landed(["ffn1_w_down"])(lands[:1])
            begun["w_in"] = lands[1]
        return _join_cargo(gather(["ffn1_w_down"]), gather(["w_in"], (0, 2))), fill

    x0 = x.reshape(T, D)
    x1, sv1 = _ffn_fwd(x0, ffn1_norm, lambda nme: full["ffn1_" + nme], "ffn1", on_gateup1,
                       lambda: (gather(["w_in"], (1, 2), [begun["w_in"]]), landed(["w_in"])))
    win_o = jnp.concatenate([full["w_in"][:, j * n_in_pad:j * n_in_pad + n_in] for j in range(4)], axis=1)
    o_q, o_k, o_v, o_f = PW, PW + FW, PW + 2 * FW, PW + 3 * FW
    o_qm, o_gate = o_f + NF, o_f + NF + MW
    p_q, p_k, p_v, p_pool, p_qm, p_f = 3 * D, 3 * D + FW, 3 * D + 2 * FW, 3 * D + 3 * FW, 3 * D + 3 * FW + PW, 3 * D + 3 * FW + PW + MW
    inp = _round_up(p_f + LANES, 512)
    win_p = jnp.concatenate([win_o[:, o_gate:], win_o[:, o_q:o_f], win_o[:, :PW], win_o[:, o_qm:o_gate],
                             win_o[:, o_f:o_qm], jnp.zeros((D, inp - p_f - NF), BF16)], axis=1)

    hb2 = _rms_fwd(x1, mix_norm, name="mix_norm")
    later = ["ffn2_w_down", "w_out", "w_fox_o", "w_mem_kv", "w_pool_up", "w_mem_o"]
    z, lands = _mm(hb2, win_p, name="in_proj", tm=1024, tn=1536, cargo=gather(later))
    landed(later)(lands)

    vpool = _pool_fwd(z, p_pool // LANES, pool_w[0], pool_scale, nb, S, name="pool_fwd")
    y_pool = _mm(vpool, full["w_pool_up"], name="pool_up", tm=1024, tn=512)

    b128 = jnp.pad(b_forget, ((0, 0), (0, LANES - NF)))
    c3 = _fox_c(z, p_f // LANES, b128, nb, S, NF, name="fox_c")
    qa, ka, vb = _fox_qk(z, p_q // LANES, p_k // LANES, p_v // LANES, c3, fox_q_norm, fox_k_norm, HD, NF,
                         name="fox_qk")
    o_fox, lands = _fox_fwd(qa, ka, vb, nb, S, HD, name="fox_fwd", cargo=gather(["ffn2_w_gate_up"]))
    landed(["ffn2_w_gate_up"])(lands)
    y_fox = _mm(o_fox, full["w_fox_o"], name="fox_out", tm=1024, tn=512)

    mem2 = mem.reshape(nb * ML, D)
    memn = _rms_fwd(mem2, mem_norm, name="mem_norm")
    kv = _mm(memn, full["w_mem_kv"], name="mem_kv")
    kmn = _headnorm_fwd(kv, 0, MW, mem_k_norm, MHD, name="mem_knorm")
    vmb = kv[:, MW:].astype(BF16)
    qmn = _headnorm_fwd(z, p_qm // LANES, MW, mem_q_norm, MHD, name="mem_qnorm")
    o_mem = _mem_fwd(qmn, kmn, vmb, nb, S, ML, name="mem_fwd")
    y_mem = _mm(o_mem, full["w_mem_o"], name="mem_out", tm=1024, tn=512)

    merged = _merge_fwd(z, y_pool, y_fox, y_mem, name="merge_fwd")
    x2 = _mm(merged, full["w_out"], res=x1, name="out_proj", tm=1024, tn=512)
    x3, sv2 = _ffn_fwd(x2, ffn2_norm, lambda nme: full["ffn2_" + nme], "ffn2", lambda: (None, None),
                       lambda: (None, None))
    dy3, dy3b, loss_part = _loss(x3, loss_target.reshape(T, D), name="loss")

    g, parts, got = {}, {}, {}
    rs_kind = dict(kind_of, w_in="row")

    def scatter(*named_grads):
        names = named_grads[0::2]
        for nme, grad in zip(names, named_grads[1::2]):
            parts[nme] = _rs_pair_sum(grad, rs_kind[nme], name="rs_pair_" + nme)
        scatter.names = names
        return _scatter_cargo([parts[nme] for nme in names], [rs_kind[nme] for nme in names])

    def scattered(lands):
        got.update(zip(scatter.names, lands))

    scatter.done = scattered
    dx2, dx2b, g["ffn2_norm"] = _ffn_bwd(dy3, dy3b, x2, ffn2_norm, full["ffn2_w_gate_up"], full["ffn2_w_down"], sv2,
                                         "ffn2", scatter)

    dmerged = _mm(dx2b, full["w_out"], mode="nt", name="d_merged", tm=1024, tn=512)
    g["w_out"] = _mm(merged, dx2b, mode="tn", out_dtype=BF16, name="d_w_out", tk=4096)
    dyp, dyf, dym, dgl = _merge_bwd(dmerged, z, y_pool, y_fox, y_mem, name="merge_bwd")

    g["w_pool_up"] = _mm(vpool, dyp, mode="tn", out_dtype=BF16, name="d_w_pool_up", tk=4096)
    dvp = _mm(dyp, full["w_pool_up"], mode="nt", name="d_vpool", tm=1024)
    du_pool, g["pool_w"], g["pool_scale"] = _pool_bwd(z, p_pool // LANES, dvp, pool_w[0], pool_scale, nb, S,
                                                      name="pool_bwd")

    g["w_fox_o"] = _mm(o_fox, dyf, mode="tn", out_dtype=BF16, name="d_w_fox_o", tk=4096)
    do_fox = _mm(dyf, full["w_fox_o"], mode="nt", out_dtype=BF16, name="d_o_fox", tm=1024)
    dqa, dka, dvf, dcs = _fox_bwd(qa, ka, vb, do_fox, nb, S, HD, name="fox_bwd")
    dq, dgq = _headnorm_bwd(z, p_q // LANES, FW, dqa, fox_q_norm, HD, HD ** -0.5, True, BF16, name="fox_dq")
    dk, dgk = _headnorm_bwd(z, p_k // LANES, FW, dka, fox_k_norm, HD, 1.0, True, BF16, name="fox_dk")
    g["fox_q_norm"], g["fox_k_norm"] = _fold_gain(dgq, HD), _fold_gain(dgk, HD)
    dc = -dcs[:, :, :LANES // HD, :].reshape(nb, NF, S).transpose(0, 2, 1).reshape(T, NF)
    df, db = _fox_c_bwd(jnp.pad(dc, ((0, 0), (0, LANES - NF))), z, p_f // LANES, b128, nb, S, name="fox_c_bwd")
    g["b_forget"] = db[:, :NF]

    g["w_mem_o"] = _mm(o_mem, dym, mode="tn", out_dtype=BF16, name="d_w_mem_o", tk=4096)
    do_mem = _mm(dym, full["w_mem_o"], mode="nt", out_dtype=BF16, name="d_o_mem", tm=1024)
    dqmn, dkmn, dvm = _mem_bwd(qmn, kmn, vmb, do_mem, nb, S, ML, name="mem_bwd")
    dqm, g["mem_q_norm"] = _headnorm_bwd(z, p_qm // LANES, MW, dqmn, mem_q_norm, MHD, 1.0, False, BF16, name="mem_dq")
    dkm, g["mem_k_norm"] = _headnorm_bwd(kv, 0, MW, dkmn, mem_k_norm, MHD, 1.0, False, BF16, name="mem_dk")
    dkv = jnp.concatenate([dkm, dvm.astype(BF16)], axis=1)
    g["w_mem_kv"] = _mm(memn, dkv, mode="tn", out_dtype=BF16, name="d_w_mem_kv")
    dmemn = _mm(dkv, full["w_mem_kv"], mode="nt", name="d_memn")
    _, _, g["mem_norm"] = _rms_bwd(mem2, dmemn, mem_norm, jnp.zeros_like(mem2), name="mem_dnorm")

    dz = jnp.concatenate([dgl, dq, dk, dvf.astype(BF16), du_pool, dqm, df.astype(BF16),
                          jnp.zeros((T, inp - p_f - LANES), BF16)], axis=1)
    cargo = scatter(*sum([[nme, g[nme]] for nme in ("w_out", "w_pool_up", "w_fox_o", "w_mem_o", "w_mem_kv")], []))
    dwin_p, lands = _mm(dz, hb2, mode="tn", out_dtype=BF16, name="d_w_in", tm=512, tn=1024, tk=4096, cargo=cargo)
    scattered(lands)
    dwin_o = jnp.concatenate([dwin_p[p_pool:p_qm], dwin_p[p_q:p_pool], dwin_p[p_f:p_f + NF], dwin_p[p_qm:p_f],
                              dwin_p[:p_q]], axis=0)
    cargo = scatter("w_in", jnp.concatenate(
        [jnp.pad(dwin_o[j * n_in:(j + 1) * n_in], ((0, n_in_pad - n_in), (0, 0))) for j in range(4)], axis=0))
    dh2, lands = _mm(dz, win_p, mode="nt", out_dtype=BF16, name="d_h2", tm=1024, tn=512, tk=3584, cargo=cargo)
    scattered(lands)
    dx1, dx1b, g["mix_norm"] = _rms_bwd(x1, dh2, mix_norm, dx2, name="mix_dnorm")

    dx0, _, g["ffn1_norm"] = _ffn_bwd(dx1, dx1b, x0, ffn1_norm, full["ffn1_w_gate_up"], full["ffn1_w_down"], sv1,
                                      "ffn1", scatter)

    halves = {nme: _rs_sum_join(parts[nme], got[nme], rs_kind[nme], name="rs_join_" + nme) for nme in big}
    gfull = {}

    gpack, sizes = _pack_small([g[nme].reshape(P[nme].shape) for nme in small] + [loss_part[:, :1]])
    gsum = _unpack_small(_all_reduce_small(gpack), sizes)
    loss = gsum[-1].reshape(())
    for nme, a in zip(small, gsum[:-1]):
        gfull[nme] = a

    delta, new_m, new_v = {}, {}, {}
    for nme in big:
        shp = P[nme].shape
        two = (lambda a: a.reshape(shp[-2:]).T) if nme == "w_in" else (lambda a: a.reshape(shp[-2:]))
        back = (lambda a: a.T.reshape(shp)) if nme == "w_in" else (lambda a: a.reshape(shp))
        gfull[nme], delta[nme], new_m[nme], new_v[nme] = [
            back(a) for a in _adamw_join(two(P[nme]), *halves[nme], two(P["m_" + nme]), two(P["v_" + nme]),
                                         name="adamw_" + nme)]
    wpack, _ = _pack_small([P[nme] for nme in small])
    mpack, _ = _pack_small([P["m_" + nme] for nme in small])
    vpack, _ = _pack_small([P["v_" + nme] for nme in small])
    gpack2, ssz = _pack_small([gfull[nme] for nme in small])
    for dct, pk in zip((delta, new_m, new_v), _adamw(wpack, gpack2, mpack, vpack, name="adamw_small")):
        for nme, a in zip(small, _unpack_small(pk, ssz)):
            dct[nme] = a

    grad_x = dx0.reshape(nb, S, D)
    return (loss, grad_x, *[gfull[nme] for nme in order], *[delta[nme] for nme in order],
            *[new_m[nme] for nme in order], *[new_v[nme] for nme in order])
```

```python
import functools

import jax
import jax.numpy as jnp
from jax import lax
from jax.experimental import pallas as pl
from jax.experimental.pallas import tpu as pltpu

F32 = jnp.float32
BF16 = jnp.bfloat16
MESH = pl.DeviceIdType.MESH
EPS = 1e-6
POOL_WINDOWS = (2, 4, 8, 16)
ADAM_LR, ADAM_B1, ADAM_B2, ADAM_EPS, ADAM_WD, ADAM_STEP = 0.001, 0.9, 0.999, 1e-08, 0.01, 10
LANES = 128
VMEM_LIMIT = 56 * 1024 * 1024
NEG = -1e30


def _pc(body, *, name, cargo=None, **kw):
    if cargo is not None:
        body, kw = _load_cargo(body, cargo, kw)
    return pl.pallas_call(body, name=name, **kw)


def _cp(*sem):
    return pltpu.CompilerParams(dimension_semantics=sem, vmem_limit_bytes=VMEM_LIMIT)


def _tile(n, pref, mult=LANES):
    if n <= pref:
        return n
    t = (pref // mult) * mult
    while t >= mult:
        if n % t == 0:
            return t
        t -= mult
    return n


def _round_up(n, m):
    return (n + m - 1) // m * m


_DIMS = {"nn": (((1,), (0,)), ((), ())), "nt": (((1,), (1,)), ((), ())), "tn": (((0,), (0,)), ((), ()))}


def _split_cargo(res, cargo):
    if cargo is None:
        return res
    nco = len(cargo.out_shapes)
    own = res[:len(res) - nco]
    return (own[0] if len(own) == 1 else own), list(res[len(res) - nco:])


def _mm(a, b, *, name, mode="nn", out_dtype=F32, scale=1.0, res=None, tm=512, tn=512, tk=2048, cargo=None,
        halves=None):
    if halves == "a":
        assert mode == "nt"
        (_, M, Kh), (N, K2) = a.shape, b.shape
        K = 2 * Kh
    elif halves == "b":
        assert mode == "tn"
        (K, M), (_, K2, Nh) = a.shape, b.shape
        N = 2 * Nh
    elif mode == "nn":
        (M, K), (K2, N) = a.shape, b.shape
    elif mode == "nt":
        (M, K), (N, K2) = a.shape, b.shape
    else:
        (K, M), (K2, N) = a.shape, b.shape
    assert K == K2, (name, a.shape, b.shape)
    tm = _tile(M, tm, 8 if M % LANES else LANES)
    tn = _tile(N // 2 if halves == "b" else N, tn)
    tk = _tile(K // 2 if halves == "a" else K, tk)
    nk = K // tk
    dims = _DIMS[mode]
    if halves == "a":
        a_spec = pl.BlockSpec((None, tm, tk), lambda i, j, k: (k // (nk // 2), i, k % (nk // 2)))
    elif mode == "tn":
        a_spec = pl.BlockSpec((tk, tm), lambda i, j, k: (k, i))
    else:
        a_spec = pl.BlockSpec((tm, tk), lambda i, j, k: (i, k))
    if halves == "b":
        njh = N // 2 // tn
        b_spec = pl.BlockSpec((None, tk, tn), lambda i, j, k: (j // njh, k, j % njh))
    elif mode == "nt":
        b_spec = pl.BlockSpec((tn, tk), lambda i, j, k: (j, k))
    else:
        b_spec = pl.BlockSpec((tk, tn), lambda i, j, k: (k, j))
    o_spec = pl.BlockSpec((tm, tn), lambda i, j, k: (i, j))
    has_res = res is not None

    def body(*refs):
        a_ref, b_ref = refs[:2]
        r_ref = refs[2] if has_res else None
        o_ref = refs[3] if has_res else refs[2]

        def finish(v):
            v = v * scale
            if has_res:
                v = r_ref[...] + v
            o_ref[...] = v.astype(o_ref.dtype)

        prod = lax.dot_general(a_ref[...], b_ref[...], dims, preferred_element_type=F32)
        if nk == 1:
            finish(prod)
            return
        acc, k = refs[-1], pl.program_id(2)

        @pl.when(k == 0)
        def _():
            acc[...] = prod

        @pl.when(k > 0)
        def _():
            acc[...] += prod

        @pl.when(k == nk - 1)
        def _():
            finish(acc[...])

    out = _pc(body, name=name, cargo=cargo, out_shape=jax.ShapeDtypeStruct((M, N), out_dtype),
              grid=(M // tm, N // tn, nk),
              in_specs=[a_spec, b_spec] + ([o_spec] if has_res else []), out_specs=o_spec,
              scratch_shapes=[] if nk == 1 else [pltpu.VMEM((tm, tn), F32)],
              compiler_params=_cp("parallel", "parallel", "arbitrary"))(
                  *((a, b, res) if has_res else (a, b)), *(cargo.arrays if cargo else ()))
    return _split_cargo(out, cargo)


def _dact_swiglu(dyb, wd, gate, up, *, name, tm=1024, tn=512, cargo=None):
    M, K = dyb.shape
    Fh = wd.shape[0]
    tm, tn = _tile(M, tm), _tile(Fh, tn)

    rows = _tile(tm, 256, 16)

    def body(dy_ref, w_ref, g_ref, u_ref, o_ref):
        w = w_ref[...]
        for r in range(0, tm, rows):
            sl = slice(r, r + rows)
            d = lax.dot_general(dy_ref[sl, :], w, _DIMS["nt"], preferred_element_type=F32) * 0.5
            g, u = g_ref[sl, :].astype(F32), u_ref[sl, :].astype(F32)
            s = jax.nn.sigmoid(g)
            o_ref[0, sl, :] = (d * u * (s * (1.0 + g * (1.0 - s)))).astype(BF16)
            o_ref[1, sl, :] = (d * (g * s)).astype(BF16)

    t_spec = pl.BlockSpec((tm, tn), lambda i, j: (i, j))
    out = _pc(body, name=name, cargo=cargo, out_shape=jax.ShapeDtypeStruct((2, M, Fh), BF16),
              grid=(M // tm, Fh // tn),
              in_specs=[pl.BlockSpec((tm, K), lambda i, j: (i, 0)), pl.BlockSpec((tn, K), lambda i, j: (j, 0)),
                        t_spec, t_spec],
              out_specs=pl.BlockSpec((2, tm, tn), lambda i, j: (0, i, j)),
              compiler_params=_cp("parallel", "parallel"))(dyb, wd, gate, up, *(cargo.arrays if cargo else ()))
    return _split_cargo(out, cargo)


def _gateup(h, wgu, *, name, tm=512, tn=512, cargo=None):
    M, K = h.shape
    Fh = wgu.shape[1] // 2
    tm, tn = _tile(M, tm), _tile(Fh, tn)
    nj = Fh // tn

    def body(h_ref, wg_ref, wu_ref, g_ref, u_ref, a_ref):
        hv = h_ref[...]
        g = jnp.dot(hv, wg_ref[...], preferred_element_type=F32)
        u = jnp.dot(hv, wu_ref[...], preferred_element_type=F32)
        g_ref[...] = g.astype(BF16)
        u_ref[...] = u.astype(BF16)
        a_ref[...] = (g * jax.nn.sigmoid(g) * u).astype(BF16)

    o_spec = pl.BlockSpec((tm, tn), lambda i, j: (i, j))
    shp = jax.ShapeDtypeStruct((M, Fh), BF16)
    out = _pc(body, name=name, cargo=cargo, out_shape=(shp, shp, shp), grid=(M // tm, nj),
              in_specs=[pl.BlockSpec((tm, K), lambda i, j: (i, 0)),
                        pl.BlockSpec((K, tn), lambda i, j: (0, j)),
                        pl.BlockSpec((K, tn), lambda i, j: (0, j + nj))],
              out_specs=(o_spec, o_spec, o_spec),
              compiler_params=_cp("parallel", "parallel"))(h, wgu, wgu, *(cargo.arrays if cargo else ()))
    return _split_cargo(out, cargo)


def _rowmap(fn, ins, outs, *, rows, tm, name, accs=()):
    tm = _tile(rows, tm, 8)
    arrs, in_specs = [], []
    for d in ins:
        if d[0] == "row":
            _, a, w, cb = d
            w = a.shape[1] if w is None else w
            in_specs.append(pl.BlockSpec((tm, w), functools.partial(lambda i, cb: (i, cb), cb=cb)))
        else:
            a = d[1]
            in_specs.append(pl.BlockSpec(a.shape, functools.partial(lambda i, nd: (0,) * nd, nd=a.ndim)))
        arrs.append(a)
    out_shape = [jax.ShapeDtypeStruct((rows, w), dt) for w, dt in outs]
    out_specs = [pl.BlockSpec((tm, w), lambda i: (i, 0)) for w, _ in outs]
    for shp, dt in accs:
        out_shape.append(jax.ShapeDtypeStruct(shp, dt))
        out_specs.append(pl.BlockSpec(shp, functools.partial(lambda i, nd: (0,) * nd, nd=len(shp))))
    nin, nout, nacc = len(ins), len(outs), len(accs)

    def body(*refs):
        res = fn(*[r[...] for r in refs[:nin]])
        res = res if isinstance(res, (tuple, list)) else (res,)
        for r, v in zip(refs[nin:nin + nout], res[:nout]):
            r[...] = v.astype(r.dtype)
        if nacc:
            acc_refs = refs[nin + nout:]

            @pl.when(pl.program_id(0) == 0)
            def _():
                for r in acc_refs:
                    r[...] = jnp.zeros_like(r)

            for r, v in zip(acc_refs, res[nout:]):
                r[...] += v

    out = _pc(body, name=name, out_shape=tuple(out_shape), grid=(rows // tm,), in_specs=in_specs,
              out_specs=tuple(out_specs), compiler_params=_cp("arbitrary" if nacc else "parallel"))(*arrs)
    return out[0] if len(out) == 1 else out


def _rms_fwd(x, g, *, name):
    def fn(xv, gv):
        return xv * lax.rsqrt(jnp.mean(xv * xv, axis=-1, keepdims=True) + EPS) * gv
    return _rowmap(fn, [("row", x, None, 0), ("bc", g)], [(x.shape[1], BF16)], rows=x.shape[0], tm=256, name=name)


def _rms_bwd(x, dh, g, dres, *, name):
    D = x.shape[1]

    def fn(xv, dhv, gv, drv):
        r = lax.rsqrt(jnp.mean(xv * xv, axis=-1, keepdims=True) + EPS)
        xh = xv * r
        dhf = dhv.astype(F32)
        dxn = dhf * gv
        dx = drv + r * (dxn - xh * jnp.mean(dxn * xh, axis=-1, keepdims=True))
        return dx, dx, jnp.sum(dhf * xh, axis=0, keepdims=True)

    return _rowmap(fn, [("row", x, None, 0), ("row", dh, None, 0), ("bc", g), ("row", dres, None, 0)],
                   [(D, F32), (D, BF16)], rows=x.shape[0], tm=256, name=name, accs=[((1, D), F32)])


def _loss(y, tgt, *, name):
    D = y.shape[1]

    def fn(yv, tv):
        e = yv - tv
        d = e * (1.0 / D)
        part = 0.5 * jnp.sum(jnp.sum(e * e, axis=-1, keepdims=True) * (1.0 / D), axis=0, keepdims=True)
        return d, d, jnp.broadcast_to(part, (1, LANES))

    return _rowmap(fn, [("row", y, None, 0), ("row", tgt, None, 0)], [(D, F32), (D, BF16)],
                   rows=y.shape[0], tm=256, name=name, accs=[((1, LANES), F32)])


def _merge_fwd(z, yp, yf, ym, *, name):
    D = yp.shape[1]

    def fn(gp, gf, gm, a, b, c):
        return jax.nn.sigmoid(gp) * a + jax.nn.sigmoid(gf) * b + jax.nn.sigmoid(gm) * c

    return _rowmap(fn, [("row", z, D, 0), ("row", z, D, 1), ("row", z, D, 2),
                        ("row", yp, None, 0), ("row", yf, None, 0), ("row", ym, None, 0)],
                   [(D, BF16)], rows=yp.shape[0], tm=128, name=name)


def _merge_bwd(dm, z, yp, yf, ym, *, name):
    D = yp.shape[1]

    def fn(d, gp, gf, gm, a, b, c):
        outs, dgl = [], []
        for gl, yv in ((gp, a), (gf, b), (gm, c)):
            s = jax.nn.sigmoid(gl)
            outs.append(d * s)
            dgl.append(d * yv * (s * (1.0 - s)))
        return (*outs, jnp.concatenate(dgl, axis=-1))

    return _rowmap(fn, [("row", dm, None, 0), ("row", z, D, 0), ("row", z, D, 1), ("row", z, D, 2),
                        ("row", yp, None, 0), ("row", yf, None, 0), ("row", ym, None, 0)],
                   [(D, BF16), (D, BF16), (D, BF16), (3 * D, BF16)], rows=yp.shape[0], tm=128, name=name)


def _adamw(w, g, m, v, *, name):
    C = w.shape[1]

    def fn(wv, gv, mv, vv):
        m2 = ADAM_B1 * mv + (1.0 - ADAM_B1) * gv
        v2 = ADAM_B2 * vv + (1.0 - ADAM_B2) * (gv * gv)
        m_hat = m2 / (1.0 - ADAM_B1 ** ADAM_STEP)
        v_hat = v2 / (1.0 - ADAM_B2 ** ADAM_STEP)
        return -ADAM_LR * (m_hat / (jnp.sqrt(v_hat) + ADAM_EPS) + ADAM_WD * wv), m2, v2

    tm = max(8, (262144 // C) // 8 * 8)
    return _rowmap(fn, [("row", a, None, 0) for a in (w, g, m, v)], [(C, F32)] * 3,
                   rows=w.shape[0], tm=tm, name=name)


def _seg_mean(v, hd):
    if hd == LANES:
        return jnp.mean(v, axis=-1, keepdims=True)
    lo = lax.broadcasted_iota(jnp.int32, v.shape, 1) < hd
    s0 = jnp.sum(jnp.where(lo, v, 0.0), axis=-1, keepdims=True)
    s1 = jnp.sum(jnp.where(lo, 0.0, v), axis=-1, keepdims=True)
    return jnp.where(lo, s0, s1) * (1.0 / hd)


def _gain128(g, hd):
    g = g.reshape(1, hd).astype(F32)
    return g if hd == LANES else jnp.concatenate([g, g], axis=-1)


def _headnorm_fwd(x, cb0, width, gain, hd, *, name):
    rows = x.shape[0]
    tm = _tile(rows, 512, 8)
    g128 = _gain128(gain, hd)

    def body(x_ref, g_ref, o_ref):
        xv = x_ref[...]
        o_ref[...] = (xv * lax.rsqrt(_seg_mean(xv * xv, hd) + EPS) * g_ref[...]).astype(BF16)

    return _pc(body, name=name, out_shape=jax.ShapeDtypeStruct((rows, width), BF16),
               grid=(rows // tm, width // LANES),
               in_specs=[pl.BlockSpec((tm, LANES), lambda i, j: (i, cb0 + j)),
                         pl.BlockSpec((1, LANES), lambda i, j: (0, 0))],
               out_specs=pl.BlockSpec((tm, LANES), lambda i, j: (i, j)),
               compiler_params=_cp("parallel", "parallel"))(x, g128)


def _headnorm_bwd(x, cb0, width, dy, gain, hd, post, spread, out_dtype, *, name):
    rows = x.shape[0]
    tm = _tile(rows, 512, 8)
    g128 = _gain128(gain, hd)
    dyw = 2 * LANES if spread else LANES

    def body(x_ref, dy_ref, g_ref, dx_ref, dg_ref):
        xv = x_ref[...]
        dyv = dy_ref[...]
        if spread:
            dyv = jnp.concatenate([dyv[:, :hd], dyv[:, LANES:LANES + hd]], axis=-1)
        dyv = dyv * post
        r = lax.rsqrt(_seg_mean(xv * xv, hd) + EPS)
        xh = xv * r
        dxn = dyv * g_ref[...]
        dx_ref[...] = (r * (dxn - xh * _seg_mean(dxn * xh, hd))).astype(dx_ref.dtype)

        @pl.when((pl.program_id(0) == 0) & (pl.program_id(1) == 0))
        def _():
            dg_ref[...] = jnp.zeros_like(dg_ref)

        dg_ref[...] += jnp.sum(dyv * xh, axis=0, keepdims=True)

    return _pc(body, name=name,
               out_shape=(jax.ShapeDtypeStruct((rows, width), out_dtype), jax.ShapeDtypeStruct((1, LANES), F32)),
               grid=(rows // tm, width // LANES),
               in_specs=[pl.BlockSpec((tm, LANES), lambda i, j: (i, cb0 + j)),
                         pl.BlockSpec((tm, dyw), lambda i, j: (i, j)),
                         pl.BlockSpec((1, LANES), lambda i, j: (0, 0))],
               out_specs=(pl.BlockSpec((tm, LANES), lambda i, j: (i, j)),
                          pl.BlockSpec((1, LANES), lambda i, j: (0, 0))),
               compiler_params=_cp("arbitrary", "arbitrary"))(x, dy, g128)


def _fold_gain(dg, hd):
    return dg if hd == LANES else dg[:, :hd] + dg[:, hd:]


def _shift_down(v, k, row):
    return jnp.where(row >= k, pltpu.roll(v, k, 0), 0.0)


def _shift_up(v, k, row, S):
    return jnp.where(row < S - k, pltpu.roll(v, S - k, 0), 0.0)


def _window_sum(v, g, row, shift):
    s, outs = v, []
    for k in (1, 2, 4, 8):
        s = s + shift(s, k)
        outs.append(s)
    return jnp.where(g == 0, outs[0], jnp.where(g == 1, outs[1], jnp.where(g == 2, outs[2], outs[3])))


def _pool_fwd(z, cb0, pool_w, pool_scale, nb, S, *, name):
    PG, PD = pool_w.shape[0], pool_w.shape[1]
    assert PD == LANES and PG <= len(POOL_WINDOWS)
    pw = pool_w.astype(BF16)

    def body(u_ref, w_ref, s_ref, o_ref):
        g = pl.program_id(0)
        u = u_ref[...]
        row = lax.broadcasted_iota(jnp.int32, u.shape, 0)
        cnt = jnp.minimum(row + 1, jnp.left_shift(2, g)).astype(F32)
        pooled = _window_sum(u, g, row, lambda v, k: _shift_down(v, k, row)) / cnt
        mixed = jnp.dot((pooled - u).astype(BF16), w_ref[...], preferred_element_type=F32)
        o_ref[...] = (mixed * s_ref[...]).astype(BF16)

    return _pc(body, name=name, out_shape=jax.ShapeDtypeStruct((nb * S, PG * PD), BF16), grid=(PG, nb),
               in_specs=[pl.BlockSpec((S, PD), lambda g, b: (b, cb0 + g)),
                         pl.BlockSpec((None, PD, PD), lambda g, b: (g, 0, 0)),
                         pl.BlockSpec((1, PD), lambda g, b: (0, g))],
               out_specs=pl.BlockSpec((S, PD), lambda g, b: (b, g)),
               compiler_params=_cp("parallel", "parallel"))(z, pw, pool_scale)


def _pool_bwd(z, cb0, dv, pool_w, pool_scale, nb, S, *, name):
    PG, PD = pool_w.shape[0], pool_w.shape[1]
    pw = pool_w.astype(BF16)

    def body(u_ref, dv_ref, w_ref, s_ref, du_ref, dw_ref, ds_ref):
        g, b = pl.program_id(0), pl.program_id(1)
        u, dvv, w = u_ref[...], dv_ref[...], w_ref[...]
        row = lax.broadcasted_iota(jnp.int32, u.shape, 0)
        cnt = jnp.minimum(row + 1, jnp.left_shift(2, g)).astype(F32)
        diff = (_window_sum(u, g, row, lambda v, k: _shift_down(v, k, row)) / cnt - u).astype(BF16)
        mixed = jnp.dot(diff, w, preferred_element_type=F32)
        dmixed = (dvv * s_ref[...]).astype(BF16)
        ddiff = lax.dot_general(dmixed, w, _DIMS["nt"], preferred_element_type=F32)
        du_ref[...] = (_window_sum(ddiff / cnt, g, row, lambda v, k: _shift_up(v, k, row, S)) - ddiff).astype(BF16)

        @pl.when(b == 0)
        def _():
            dw_ref[...] = jnp.zeros_like(dw_ref)
            ds_ref[...] = jnp.zeros_like(ds_ref)

        dw_ref[...] += lax.dot_general(diff, dmixed, _DIMS["tn"], preferred_element_type=F32)
        ds_ref[...] += jnp.sum(dvv * mixed, axis=0, keepdims=True)

    return _pc(body, name=name,
               out_shape=(jax.ShapeDtypeStruct((nb * S, PG * PD), BF16), jax.ShapeDtypeStruct((PG, PD, PD), F32),
                          jax.ShapeDtypeStruct((1, PG * PD), F32)),
               grid=(PG, nb),
               in_specs=[pl.BlockSpec((S, PD), lambda g, b: (b, cb0 + g)),
                         pl.BlockSpec((S, PD), lambda g, b: (b, g)),
                         pl.BlockSpec((None, PD, PD), lambda g, b: (g, 0, 0)),
                         pl.BlockSpec((1, PD), lambda g, b: (0, g))],
               out_specs=(pl.BlockSpec((S, PD), lambda g, b: (b, g)),
                          pl.BlockSpec((None, PD, PD), lambda g, b: (g, 0, 0)),
                          pl.BlockSpec((1, PD), lambda g, b: (0, g))),
               compiler_params=_cp("parallel", "arbitrary"))(z, dv, pw, pool_scale)


def _split3(c):
    hi = c.astype(BF16)
    r1 = c - hi.astype(F32)
    mid = r1.astype(BF16)
    lo = (r1 - mid.astype(F32)).astype(BF16)
    return hi, mid, lo


def _log_sigmoid(v):
    return jnp.minimum(v, 0.0) - jnp.log(1.0 + jnp.exp(-jnp.abs(v)))


def _fox_c(z, cb, b128, nb, S, NF, *, name):
    def body(f_ref, b_ref, hi_ref, mid_ref, lo_ref):
        v = f_ref[...] + b_ref[...]
        lane = lax.broadcasted_iota(jnp.int32, v.shape, 1)
        row = lax.broadcasted_iota(jnp.int32, v.shape, 0)
        c = jnp.where(lane < NF, _log_sigmoid(v), 0.0)
        k = 1
        while k < S:
            c = c + _shift_down(c, k, row)
            k *= 2
        hi, mid, lo = _split3(c)
        hi_ref[...], mid_ref[...], lo_ref[...] = hi, mid, lo

    shp = jax.ShapeDtypeStruct((nb * S, LANES), BF16)
    spec = pl.BlockSpec((S, LANES), lambda i: (i, 0))
    return _pc(body, name=name, out_shape=(shp, shp, shp), grid=(nb,),
               in_specs=[pl.BlockSpec((S, LANES), lambda i: (i, cb)), pl.BlockSpec((1, LANES), lambda i: (0, 0))],
               out_specs=(spec, spec, spec), compiler_params=_cp("parallel"))(z, b128)


def _fox_c_bwd(dc, z, cb, b128, nb, S, *, name):
    def body(dc_ref, f_ref, b_ref, df_ref, db_ref):
        d = dc_ref[...]
        row = lax.broadcasted_iota(jnp.int32, d.shape, 0)
        k = 1
        while k < S:
            d = d + _shift_up(d, k, row, S)
            k *= 2
        df = d * jax.nn.sigmoid(-(f_ref[...] + b_ref[...]))
        df_ref[...] = df

        @pl.when(pl.program_id(0) == 0)
        def _():
            db_ref[...] = jnp.zeros_like(db_ref)

        db_ref[...] += jnp.sum(df, axis=0, keepdims=True)

    return _pc(body, name=name,
               out_shape=(jax.ShapeDtypeStruct((nb * S, LANES), F32), jax.ShapeDtypeStruct((1, LANES), F32)),
               grid=(nb,),
               in_specs=[pl.BlockSpec((S, LANES), lambda i: (i, 0)), pl.BlockSpec((S, LANES), lambda i: (i, cb)),
                         pl.BlockSpec((1, LANES), lambda i: (0, 0))],
               out_specs=(pl.BlockSpec((S, LANES), lambda i: (i, 0)), pl.BlockSpec((1, LANES), lambda i: (0, 0))),
               compiler_params=_cp("arbitrary"))(dc, z, b128)


def _fox_qk(z, cbq, cbk, cbv, c3, gq, gk, HD, NF, *, name):
    T = z.shape[0]
    HP = NF * HD // LANES
    tm = _tile(T, 512, 8)
    scale = HD ** -0.5
    gq128, gk128 = _gain128(gq, HD), _gain128(gk, HD)

    def body(q_ref, k_ref, v_ref, hi_ref, mid_ref, lo_ref, gq_ref, gk_ref, qa_ref, ka_ref, vb_ref):
        hp = pl.program_id(1)
        q, k = q_ref[...], k_ref[...]
        qn = q * lax.rsqrt(_seg_mean(q * q, HD) + EPS) * (gq_ref[...] * scale)
        kn = k * lax.rsqrt(_seg_mean(k * k, HD) + EPS) * gk_ref[...]
        vb_ref[...] = v_ref[...].astype(BF16)
        c3v = jnp.concatenate([hi_ref[...], mid_ref[...], lo_ref[...]], axis=-1)
        r = lax.broadcasted_iota(jnp.int32, (3 * LANES, LANES - HD), 0)
        cc = lax.broadcasted_iota(jnp.int32, (3 * LANES, LANES - HD), 1)
        lane = lax.broadcasted_iota(jnp.int32, (tm, LANES - HD), 1)
        for hh in range(LANES // HD):
            head = hp * (LANES // HD) + hh
            sel_q = jnp.where((cc < 3) & (r == head + LANES * cc), 1.0, 0.0).astype(BF16)
            sel_k = jnp.where((cc >= 3) & (cc < 6) & (r == head + LANES * (cc - 3)), 1.0, 0.0).astype(BF16)
            qaug = jnp.dot(c3v, sel_q, preferred_element_type=F32) + jnp.where((lane >= 3) & (lane < 6), 1.0, 0.0)
            kaug = jnp.where(lane < 3, 1.0, 0.0) - jnp.dot(c3v, sel_k, preferred_element_type=F32)
            sl = slice(hh * HD, (hh + 1) * HD)
            qa_ref[:, hh * LANES:(hh + 1) * LANES] = jnp.concatenate([qn[:, sl], qaug], axis=-1).astype(BF16)
            ka_ref[:, hh * LANES:(hh + 1) * LANES] = jnp.concatenate([kn[:, sl], kaug], axis=-1).astype(BF16)

    aw = (LANES // HD) * LANES
    blk = lambda cb: pl.BlockSpec((tm, LANES), functools.partial(lambda i, j, cb: (i, cb + j), cb=cb))
    cspec = pl.BlockSpec((tm, LANES), lambda i, j: (i, 0))
    gspec = pl.BlockSpec((1, LANES), lambda i, j: (0, 0))
    return _pc(body, name=name,
               out_shape=(jax.ShapeDtypeStruct((T, HP * aw), BF16), jax.ShapeDtypeStruct((T, HP * aw), BF16),
                          jax.ShapeDtypeStruct((T, NF * HD), BF16)),
               grid=(T // tm, HP),
               in_specs=[blk(cbq), blk(cbk), blk(cbv), cspec, cspec, cspec, gspec, gspec],
               out_specs=(pl.BlockSpec((tm, aw), lambda i, j: (i, j)), pl.BlockSpec((tm, aw), lambda i, j: (i, j)),
                          pl.BlockSpec((tm, LANES), lambda i, j: (i, j))),
               compiler_params=_cp("parallel", "parallel"))(z, z, z, *c3, gq128, gk128)


def _fox_probs(qa, ka, q0):
    s = lax.dot_general(qa, ka, _DIMS["nt"], preferred_element_type=F32)
    row = lax.broadcasted_iota(jnp.int32, s.shape, 0) + q0
    col = lax.broadcasted_iota(jnp.int32, s.shape, 1)
    s = jnp.where(col <= row, s, NEG)
    e = jnp.exp(s - jnp.max(s, axis=-1, keepdims=True))
    return e / jnp.sum(e, axis=-1, keepdims=True)


def _fox_fwd(qa, ka, vb, nb, S, HD, *, name, tq=256, cargo=None):
    T = qa.shape[0]
    nh = LANES // HD
    aw = nh * LANES
    HP = qa.shape[1] // aw
    tq = _tile(S, tq, 8)
    nq = S // tq

    def body(qa_ref, ka_ref, v_ref, o_ref):
        for k in range(nq):
            @pl.when(pl.program_id(2) == k)
            def _(k=k):
                L = (k + 1) * tq
                outs = []
                v = v_ref[0:L, :]
                for hh in range(nh):
                    al = slice(hh * LANES, (hh + 1) * LANES)
                    p = _fox_probs(qa_ref[:, al], ka_ref[0:L, al], k * tq)
                    outs.append(jnp.dot(p.astype(BF16), v[:, hh * HD:(hh + 1) * HD], preferred_element_type=F32))
                o_ref[...] = jnp.concatenate(outs, axis=-1).astype(BF16)

    out = _pc(body, name=name, cargo=cargo, out_shape=jax.ShapeDtypeStruct((T, HP * LANES), BF16),
              grid=(nb, HP, nq),
              in_specs=[pl.BlockSpec((tq, aw), lambda b, h, i: (b * nq + i, h)),
                        pl.BlockSpec((S, aw), lambda b, h, i: (b, h)),
                        pl.BlockSpec((S, LANES), lambda b, h, i: (b, h))],
              out_specs=pl.BlockSpec((tq, LANES), lambda b, h, i: (b * nq + i, h)),
              compiler_params=_cp("parallel", "parallel", "arbitrary"))(qa, ka, vb, *(cargo.arrays if cargo else ()))
    return _split_cargo(out, cargo)


def _fox_bwd(qa, ka, vb, do, nb, S, HD, *, name, tq=256, cargo=None):
    T = qa.shape[0]
    nh = LANES // HD
    aw = nh * LANES
    HP = qa.shape[1] // aw
    tq = _tile(S, tq, 8)
    nq = S // tq

    def body(qa_ref, ka_ref, v_ref, do_ref, dqa_ref, dka_ref, dv_ref, dcs_ref):
        i = pl.program_id(2)

        @pl.when(i == 0)
        def _():
            dka_ref[...] = jnp.zeros_like(dka_ref)
            dv_ref[...] = jnp.zeros_like(dv_ref)
            dcs_ref[...] = jnp.zeros_like(dcs_ref)

        for k in range(nq):
            @pl.when(i == k)
            def _(k=k):
                L = (k + 1) * tq
                dkas, dvs, css = [], [], []
                v, dob = v_ref[0:L, :], do_ref[...]
                for hh in range(nh):
                    al = slice(hh * LANES, (hh + 1) * LANES)
                    hl = slice(hh * HD, (hh + 1) * HD)
                    qav, kav, vv, dov = qa_ref[:, al], ka_ref[0:L, al], v[:, hl], dob[:, hl]
                    p = _fox_probs(qav, kav, k * tq)
                    dp = lax.dot_general(dov, vv, _DIMS["nt"], preferred_element_type=F32)
                    dsf = p * (dp - jnp.sum(p * dp, axis=-1, keepdims=True))
                    css.append(jnp.sum(dsf, axis=0, keepdims=True))
                    ds = dsf.astype(BF16)
                    dqa_ref[:, al] = jnp.dot(ds, kav, preferred_element_type=F32)
                    dkas.append(lax.dot_general(ds, qav, _DIMS["tn"], preferred_element_type=F32))
                    dvs.append(lax.dot_general(p.astype(BF16), dov, _DIMS["tn"], preferred_element_type=F32))
                dka_ref[0:L, :] += jnp.concatenate(dkas, axis=-1)
                dv_ref[0:L, :] += jnp.concatenate(dvs, axis=-1)
                dcs_ref[:, 0:L] += jnp.concatenate(css + [jnp.zeros((8 - nh, L), F32)], axis=0)

    out = _pc(body, name=name, cargo=cargo,
              out_shape=(jax.ShapeDtypeStruct((T, HP * aw), F32), jax.ShapeDtypeStruct((T, HP * aw), F32),
                         jax.ShapeDtypeStruct((T, HP * LANES), F32), jax.ShapeDtypeStruct((nb, HP, 8, S), F32)),
              grid=(nb, HP, nq),
              in_specs=[pl.BlockSpec((tq, aw), lambda b, h, i: (b * nq + i, h)),
                        pl.BlockSpec((S, aw), lambda b, h, i: (b, h)),
                        pl.BlockSpec((S, LANES), lambda b, h, i: (b, h)),
                        pl.BlockSpec((tq, LANES), lambda b, h, i: (b * nq + i, h))],
              out_specs=(pl.BlockSpec((tq, aw), lambda b, h, i: (b * nq + i, h)),
                         pl.BlockSpec((S, aw), lambda b, h, i: (b, h)),
                         pl.BlockSpec((S, LANES), lambda b, h, i: (b, h)),
                         pl.BlockSpec((None, None, 8, S), lambda b, h, i: (b, h, 0, 0))),
              compiler_params=_cp("parallel", "parallel", "arbitrary"))(qa, ka, vb, do,
                                                                         *(cargo.arrays if cargo else ()))
    return _split_cargo(out, cargo)


def _mem_probs(q, k, scale):
    s = lax.dot_general(q, k, _DIMS["nt"], preferred_element_type=F32) * scale
    e = jnp.exp(s - jnp.max(s, axis=-1, keepdims=True))
    return e / jnp.sum(e, axis=-1, keepdims=True)


def _mem_fwd(q, k, v, nb, S, ML, *, name, tq=512):
    T, MW = q.shape
    nh = MW // LANES
    tq = _tile(S, tq, 8)
    nq = S // tq
    scale = LANES ** -0.5

    def body(q_ref, k_ref, v_ref, o_ref):
        for h in range(nh):
            sl = slice(h * LANES, (h + 1) * LANES)
            p = _mem_probs(q_ref[:, sl], k_ref[:, sl], scale)
            o_ref[:, sl] = jnp.dot(p.astype(BF16), v_ref[:, sl], preferred_element_type=F32).astype(BF16)

    return _pc(body, name=name, out_shape=jax.ShapeDtypeStruct((T, MW), BF16), grid=(nb, nq),
               in_specs=[pl.BlockSpec((tq, MW), lambda b, i: (b * nq + i, 0)),
                         pl.BlockSpec((ML, MW), lambda b, i: (b, 0)), pl.BlockSpec((ML, MW), lambda b, i: (b, 0))],
               out_specs=pl.BlockSpec((tq, MW), lambda b, i: (b * nq + i, 0)),
               compiler_params=_cp("parallel", "arbitrary"))(q, k, v)


def _mem_bwd(q, k, v, do, nb, S, ML, *, name, tq=512):
    T, MW = q.shape
    nh = MW // LANES
    tq = _tile(S, tq, 8)
    nq = S // tq
    scale = LANES ** -0.5

    def body(q_ref, k_ref, v_ref, do_ref, dq_ref, dk_ref, dv_ref):
        i = pl.program_id(1)

        @pl.when(i == 0)
        def _():
            dk_ref[...] = jnp.zeros_like(dk_ref)
            dv_ref[...] = jnp.zeros_like(dv_ref)

        for h in range(nh):
            sl = slice(h * LANES, (h + 1) * LANES)
            qv, kv, vv, dov = q_ref[:, sl], k_ref[:, sl], v_ref[:, sl], do_ref[:, sl]
            p = _mem_probs(qv, kv, scale)
            dp = lax.dot_general(dov, vv, _DIMS["nt"], preferred_element_type=F32)
            ds = (p * (dp - jnp.sum(p * dp, axis=-1, keepdims=True)) * scale).astype(BF16)
            dq_ref[:, sl] = jnp.dot(ds, kv, preferred_element_type=F32)
            dk_ref[:, sl] += lax.dot_general(ds, qv, _DIMS["tn"], preferred_element_type=F32)
            dv_ref[:, sl] += lax.dot_general(p.astype(BF16), dov, _DIMS["tn"], preferred_element_type=F32)

    kvspec = pl.BlockSpec((ML, MW), lambda b, i: (b, 0))
    qspec = pl.BlockSpec((tq, MW), lambda b, i: (b * nq + i, 0))
    return _pc(body, name=name,
               out_shape=(jax.ShapeDtypeStruct((T, MW), F32), jax.ShapeDtypeStruct((nb * ML, MW), F32),
                          jax.ShapeDtypeStruct((nb * ML, MW), F32)),
               grid=(nb, nq), in_specs=[qspec, kvspec, kvspec, qspec], out_specs=(qspec, kvspec, kvspec),
               compiler_params=_cp("parallel", "arbitrary"))(q, k, v, do)


def _place():
    x, y, c = lax.axis_index("x"), lax.axis_index("y"), lax.axis_index("c")
    chips = [(1 - x, y), (x, 1 - y), (1 - x, 1 - y)]
    return x, y, c, chips


def _any_specs(n):
    return [pl.BlockSpec(memory_space=pl.ANY)] * n


class _Cargo:
    def __init__(self, arrays, out_shapes, ncopies, copies, alias=None):
        self.arrays, self.out_shapes, self.ncopies, self.copies = list(arrays), list(out_shapes), ncopies, copies
        self.alias = alias or {}


def _load_cargo(body, cargo, kw):
    as_list = lambda v: list(v) if isinstance(v, (tuple, list)) else [v]
    grid = kw.get("grid", ())
    in_specs, out_specs, out_shape = as_list(kw["in_specs"]), as_list(kw["out_specs"]), as_list(kw["out_shape"])
    scratch = list(kw.get("scratch_shapes", ()))
    nin, nout, nci, nco, nscr = len(in_specs), len(out_specs), len(cargo.arrays), len(cargo.out_shapes), len(scratch)

    def loaded(*refs):
        ins, cin = refs[:nin], refs[nin:nin + nci]
        outs, cout = refs[nin + nci:nin + nci + nout], refs[nin + nci + nout:nin + nci + nout + nco]
        scr, (ssem, rsem) = refs[nin + nci + nout + nco:-2], refs[-2:]
        first = last = True
        for a, g in enumerate(grid):
            first = first & (pl.program_id(a) == 0)
            last = last & (pl.program_id(a) == g - 1)

        def start():
            for cp in cargo.copies(cin, cout, ssem, rsem, False):
                cp.start()

        def wait():
            for cp in cargo.copies(cin, cout, ssem, rsem, True):
                cp.wait_send()
                cp.wait_recv()

        if grid:
            pl.when(first)(start)
        else:
            start()
        body(*ins, *outs, *scr)
        if grid:
            pl.when(last)(wait)
        else:
            wait()

    kw = dict(kw, in_specs=in_specs + _any_specs(nci), out_specs=tuple(out_specs + _any_specs(nco)),
              out_shape=tuple(out_shape + cargo.out_shapes),
              scratch_shapes=scratch + [pltpu.SemaphoreType.DMA((cargo.ncopies,))] * 2)
    if cargo.alias:
        kw["input_output_aliases"] = {nin + i: nout + o for i, o in cargo.alias.items()}
    if grid:
        kw["compiler_params"] = _cp(*["arbitrary"] * len(grid))
    return loaded, kw


class _SemView:
    def __init__(self, ref, off):
        self.ref, self.off, self.at = ref, off, self

    def __getitem__(self, k):
        return self.ref.at[self.off + k]


def _join_cargo(a, b):
    na, nao = len(a.arrays), len(a.out_shapes)

    def copies(ins, outs, ssem, rsem, waiting):
        return (a.copies(ins[:na], outs[:nao], ssem, rsem, waiting) +
                b.copies(ins[na:], outs[nao:], _SemView(ssem, a.ncopies), _SemView(rsem, a.ncopies), waiting))

    alias = dict(a.alias)
    alias.update({na + i: nao + o for i, o in b.alias.items()})
    return _Cargo(a.arrays + b.arrays, a.out_shapes + b.out_shapes, a.ncopies + b.ncopies, copies, alias)


def _run_cargo(cargo, *, name):
    res = _pc(lambda: None, name=name, cargo=cargo, in_specs=[], out_specs=[], out_shape=[])(*cargo.arrays)
    return list(res)


def _full_shape(dims, kind):
    return (dims[0], 4 * dims[1]) if kind == "col" else (4 * dims[0], dims[1])


def _ag_copy(shard, full, kind, j_src, chip, c, ssem, rsem, piece):
    R, C = shard.shape
    p, npieces = piece
    rows = R // 2 // npieces
    row0 = pl.multiple_of(c * (R // 2) + p * rows, 16)
    if kind == "col":
        dst = full.at[pl.ds(row0, rows), pl.ds(pl.multiple_of(j_src * C, LANES), C)]
    else:
        dst = full.at[pl.ds(pl.multiple_of(j_src * R + row0, 16), rows), :]
    return pltpu.make_async_remote_copy(shard.at[pl.ds(row0, rows), :], dst, ssem, rsem,
                                        device_id=(*chip, c), device_id_type=MESH)


def _ag_cargo(shards, kinds, piece=(0, 1), lands=None):
    n = len(shards)

    def copies(ins, outs, ssem, rsem, waiting):
        x, y, c, chips = _place()
        return [_ag_copy(ins[w], outs[w], kinds[w], 2 * chip[0] + chip[1] if waiting else 2 * x + y, chip, c,
                         ssem.at[3 * w + q], rsem.at[3 * w + q], piece)
                for w in range(n) for q, chip in enumerate(chips)]

    shapes = [jax.ShapeDtypeStruct(_full_shape(s.shape, k), BF16) for s, k in zip(shards, kinds)]
    if lands is None:
        return _Cargo(shards, shapes, 3 * n, copies)
    return _Cargo(list(shards) + list(lands), shapes, 3 * n, copies, alias={n + w: w for w in range(n)})


def _scatter_cargo(parts, kinds, piece=(0, 1), lands=None):
    n = len(parts)
    pdims = [((p.shape[0], p.shape[1] // 4) if k == "col" else (p.shape[0] // 4, p.shape[1]))
             for p, k in zip(parts, kinds)]

    def copy(src, dst, kind, R, C, q, j, chip, c, ssem, rsem):
        rows = R // piece[1]
        r0 = piece[0] * rows
        if kind == "col":
            win = src.at[pl.ds(r0, rows), pl.ds(pl.multiple_of(j * C, LANES), C)]
        else:
            win = src.at[pl.ds(pl.multiple_of(j * R + r0, 16), rows), :]
        return pltpu.make_async_remote_copy(win, dst.at[q, pl.ds(r0, rows), :], ssem, rsem,
                                            device_id=(*chip, c), device_id_type=MESH)

    def copies(ins, outs, ssem, rsem, waiting):
        x, y, c, chips = _place()
        return [copy(ins[w], outs[w], kinds[w], *pdims[w], q, 2 * chip[0] + chip[1], chip, c,
                     ssem.at[3 * w + q], rsem.at[3 * w + q])
                for w in range(n) for q, chip in enumerate(chips)]

    shapes = [jax.ShapeDtypeStruct((3,) + d, BF16) for d in pdims]
    if lands is None:
        return _Cargo(parts, shapes, 3 * n, copies)
    return _Cargo(list(parts) + list(lands), shapes, 3 * n, copies, alias={n + w: w for w in range(n)})


def _ag_fill(full, shard, kind, *, name):
    R, C = shard.shape
    RH = R // 2
    tm = _tile(RH, max(16, (3 * 2 ** 19 // C) // 16 * 16), 16)
    nbh = RH // tm
    n, nown = 3 * nbh, 2 * nbh
    xyc = jnp.stack([lax.axis_index("x"), lax.axis_index("y"), lax.axis_index("c")]).astype(jnp.int32)

    def chip_of(q, x, y):
        keep_x, keep_y = q % 2, (2 - q) // 2
        return 2 * ((1 - x) + keep_x * (2 * x - 1)) + (1 - y) + keep_y * (2 * y - 1)

    def window(ref, j, row0):
        if kind == "col":
            return ref.at[pl.ds(pl.multiple_of(row0, 16), tm), pl.ds(pl.multiple_of(j * C, LANES), C)]
        return ref.at[pl.ds(pl.multiple_of(j * R + row0, 16), tm), :]

    def body(p_ref, send_ref, own_ref, out_ref, land, ssem, rsem, wsem):
        i = pl.program_id(0)
        x, y = lax.axis_index("x"), lax.axis_index("y")
        c, peer = _pair_peer()
        own_cp = pltpu.make_async_copy(own_ref, window(out_ref, 2 * x + y, jnp.minimum(i, nown - 1) * tm), wsem.at[1])

        @pl.when(i < nown)
        def _():
            own_cp.start()

        def consume(land, ps):
            b = i - 1
            cp = pltpu.make_async_copy(land.at[ps], window(out_ref, chip_of(b // nbh, x, y),
                                                           (1 - c) * RH + (b % nbh) * tm), wsem.at[0])
            cp.start()
            cp.wait()

        _exchange_step(i, n, send_ref, land, ssem, rsem, None, peer, consume)

        @pl.when(i < nown)
        def _():
            own_cp.wait()

    now = lambda i: jnp.minimum(i, n - 1)

    def send_map(i, p):
        q, b = now(i) // nbh, now(i) % nbh
        j = chip_of(q, p[0], p[1])
        return (p[2] * nbh + b, j) if kind == "col" else ((j * 2 + p[2]) * nbh + b, 0)

    grid_spec = pltpu.PrefetchScalarGridSpec(
        num_scalar_prefetch=1, grid=(n + 1,),
        in_specs=[pl.BlockSpec((tm, C), send_map),
                  pl.BlockSpec((tm, C), lambda i, p: (jnp.minimum(i, nown - 1), 0))],
        out_specs=pl.BlockSpec(memory_space=pl.ANY),
        scratch_shapes=[pltpu.VMEM((n, tm, C), BF16), pltpu.SemaphoreType.DMA((n,)), pltpu.SemaphoreType.DMA((n,)),
                        pltpu.SemaphoreType.DMA((2,))])
    return _pc(body, name=name, out_shape=jax.ShapeDtypeStruct(full.shape, BF16), grid_spec=grid_spec,
               input_output_aliases={1: 0}, compiler_params=_cp("arbitrary"))(xyc, full, shard)


def _pair_peer():
    x, y, c = lax.axis_index("x"), lax.axis_index("y"), lax.axis_index("c")
    return c, (x, y, 1 - c)


def _exchange_step(i, n, src_ref, land, ssem, rsem, credit, peer, consume):
    def rdma(slot):
        return pltpu.make_async_remote_copy(src_ref, land.at[slot], ssem.at[slot], rsem.at[slot],
                                            device_id=peer, device_id_type=MESH)

    reuse = credit is not None
    slot = i % 2 if reuse else jnp.minimum(i, n - 1)

    @pl.when(i < n)
    def _():
        if reuse:
            @pl.when(i >= 2)
            def _():
                pl.semaphore_wait(credit, 1)

        rdma(slot).start()

    @pl.when(i >= 1)
    def _():
        ps = (i - 1) % 2 if reuse else i - 1
        rdma(ps).wait_recv()
        consume(land, ps)

        if reuse:
            @pl.when(i - 1 < n - 2)
            def _():
                pl.semaphore_signal(credit, 1, device_id=peer, device_id_type=MESH)

    @pl.when(i < n)
    def _():
        rdma(slot).wait_send()


def _xchg_scratch(tm, C, dtype):
    return [pltpu.VMEM((2, tm, C), dtype), pltpu.SemaphoreType.DMA((2,)), pltpu.SemaphoreType.DMA((2,)),
            pltpu.SemaphoreType.REGULAR]


def _rs_pair_sum(g, kind, *, name):
    K, N = g.shape
    NS, RH = (1, K // 2) if kind == "col" else (4, K // 8)
    tm = _tile(RH, max(16, (3 * 2 ** 19 // N) // 16 * 16), 16)
    if NS * (RH // tm) < 2:
        tm = RH // 2
    nb = RH // tm
    n = NS * nb
    cvec = lax.axis_index("c").reshape(1).astype(jnp.int32)

    def body(c_ref, send_ref, own_ref, out_ref, land, ssem, rsem, credit):
        _, peer = _pair_peer()

        def consume(land, ps):
            out_ref[...] = (own_ref[...].astype(F32) + land[ps].astype(F32)).astype(BF16)

        _exchange_step(pl.program_id(0), n, send_ref, land, ssem, rsem, credit, peer, consume)

    now = lambda i: jnp.minimum(i, n - 1)
    lag = lambda i: jnp.maximum(i - 1, 0)
    grid_spec = pltpu.PrefetchScalarGridSpec(
        num_scalar_prefetch=1, grid=(n + 1,),
        in_specs=[pl.BlockSpec((tm, N), lambda i, c: ((now(i) // nb * 2 + 1 - c[0]) * nb + now(i) % nb, 0)),
                  pl.BlockSpec((tm, N), lambda i, c: ((lag(i) // nb * 2 + c[0]) * nb + lag(i) % nb, 0))],
        out_specs=pl.BlockSpec((tm, N), lambda i, c: (lag(i), 0)),
        scratch_shapes=_xchg_scratch(tm, N, BF16))
    return _pc(body, name=name, out_shape=jax.ShapeDtypeStruct((NS * RH, N), BF16), grid_spec=grid_spec,
               compiler_params=_cp("arbitrary"))(cvec, g, g)


def _rs_sum_join(part, got, kind, *, name):
    _, hr, hc = got.shape
    tm = _tile(hr, max(16, (3 * 2 ** 18 // hc) // 16 * 16), 16)
    if hr // tm < 2:
        tm = hr // 2
    nb = hr // tm
    jvec = (2 * lax.axis_index("x") + lax.axis_index("y")).reshape(1).astype(jnp.int32)

    def body(j_ref, p_ref, a_ref, b_ref, c_ref, mine_ref, theirs_ref, sbuf, land, ssem, rsem, credit):
        i = pl.program_id(0)
        _, peer = _pair_peer()

        @pl.when(i < nb)
        def _():
            v = ((p_ref[...].astype(F32) + a_ref[...].astype(F32)) + b_ref[...].astype(F32)) + c_ref[...].astype(F32)
            mine_ref[...] = v
            sbuf[...] = v

        def consume(land, ps):
            theirs_ref[...] = land[ps]

        _exchange_step(i, nb, sbuf, land, ssem, rsem, credit, peer, consume)

    now = lambda i: jnp.minimum(i, nb - 1)
    lag = lambda i: jnp.maximum(i - 1, 0)
    own_spec = (pl.BlockSpec((tm, hc), lambda i, j: (now(i), j[0])) if kind == "col"
                else pl.BlockSpec((tm, hc), lambda i, j: (j[0] * nb + now(i), 0)))
    got_spec = lambda q: pl.BlockSpec((None, tm, hc), functools.partial(lambda i, j, q: (q, now(i), 0), q=q))
    grid_spec = pltpu.PrefetchScalarGridSpec(
        num_scalar_prefetch=1, grid=(nb + 1,),
        in_specs=[own_spec, got_spec(0), got_spec(1), got_spec(2)],
        out_specs=(pl.BlockSpec((tm, hc), lambda i, j: (now(i), 0)), pl.BlockSpec((tm, hc), lambda i, j: (lag(i), 0))),
        scratch_shapes=[pltpu.VMEM((tm, hc), F32)] + _xchg_scratch(tm, hc, F32))
    shp = jax.ShapeDtypeStruct((hr, hc), F32)
    return _pc(body, name=name, out_shape=(shp, shp), grid_spec=grid_spec,
               compiler_params=_cp("arbitrary"))(jvec, part, got, got, got)


def _adamw_join(w, mine, theirs, m, v, *, name):
    R2, wc = w.shape
    hr, hc = mine.shape
    assert hr < R2 <= 2 * hr and hc >= wc
    pref, rows8 = max(8, 2 ** 18 // hc), _round_up(R2, 8)
    tm = max(t for t in range(8, hr + 1, 8) if hr % t == 0 and rows8 % t == 0 and (t <= pref or t == 8))
    nbh = hr // tm
    cvec = lax.axis_index("c").reshape(1).astype(jnp.int32)

    def body(c_ref, w_ref, a_ref, b_ref, m_ref, v_ref, g_ref, d_ref, m2_ref, v2_ref):
        gv = jnp.where(pl.program_id(0) // nbh == c_ref[0], a_ref[...], b_ref[...])[:, :wc]
        m2 = ADAM_B1 * m_ref[...] + (1.0 - ADAM_B1) * gv
        v2 = ADAM_B2 * v_ref[...] + (1.0 - ADAM_B2) * (gv * gv)
        m_hat = m2 / (1.0 - ADAM_B1 ** ADAM_STEP)
        v_hat = v2 / (1.0 - ADAM_B2 ** ADAM_STEP)
        g_ref[...] = gv
        d_ref[...] = -ADAM_LR * (m_hat / (jnp.sqrt(v_hat) + ADAM_EPS) + ADAM_WD * w_ref[...])
        m2_ref[...] = m2
        v2_ref[...] = v2

    wspec = pl.BlockSpec((tm, wc), lambda i, c: (i, 0))
    mine_spec = pl.BlockSpec((tm, hc), lambda i, c: (jnp.where(i // nbh == c[0], i % nbh, 0), 0))
    theirs_spec = pl.BlockSpec((tm, hc), lambda i, c: (jnp.where(i // nbh == c[0], 0, i % nbh), 0))
    grid_spec = pltpu.PrefetchScalarGridSpec(
        num_scalar_prefetch=1, grid=(pl.cdiv(R2, tm),),
        in_specs=[wspec, mine_spec, theirs_spec, wspec, wspec], out_specs=(wspec,) * 4)
    return _pc(body, name=name, out_shape=(jax.ShapeDtypeStruct((R2, wc), F32),) * 4, grid_spec=grid_spec,
               compiler_params=_cp("parallel"))(cvec, w, mine, theirs, m, v)


def _all_reduce_small(pack):
    R = pack.shape[0]

    def body(p_ref, o_ref, buf, send, recv):
        x, y, c, _ = _place()
        me = 4 * x + 2 * y + c
        buf[0] = p_ref[...]
        cps = []
        for r in range(1, 8):
            peer = (x if r & 4 == 0 else 1 - x, y if r & 2 == 0 else 1 - y, c if r & 1 == 0 else 1 - c)
            cp = pltpu.make_async_remote_copy(p_ref, buf.at[r], send.at[r - 1], recv.at[r - 1],
                                              device_id=peer, device_id_type=MESH)
            cp.start()
            cps.append(cp)
        for cp in cps:
            cp.wait()
        acc = buf[jnp.bitwise_xor(me, 0)]
        for k in range(1, 8):
            acc = acc + buf[jnp.bitwise_xor(me, k)]
        o_ref[...] = acc

    return _pc(body, name="allreduce_small", out_shape=jax.ShapeDtypeStruct((R, LANES), F32),
               in_specs=[pl.BlockSpec(memory_space=pltpu.VMEM)], out_specs=pl.BlockSpec(memory_space=pltpu.VMEM),
               scratch_shapes=[pltpu.VMEM((8, R, LANES), F32), pltpu.SemaphoreType.DMA((7,)),
                               pltpu.SemaphoreType.DMA((7,))])(pack)


def _pack_small(arrs):
    parts, sizes = [], []
    for a in arrs:
        f = a.reshape(-1).astype(F32)
        n = _round_up(f.shape[0], 8 * LANES)
        parts.append(jnp.pad(f, (0, n - f.shape[0])).reshape(n // LANES, LANES))
        sizes.append((a.shape, f.shape[0], n // LANES))
    return jnp.concatenate(parts, axis=0), sizes


def _unpack_small(pack, sizes):
    out, r0 = [], 0
    for shape, cnt, rows in sizes:
        out.append(pack[r0:r0 + rows].reshape(-1)[:cnt].reshape(shape))
        r0 += rows
    return out


def _ffn_fwd(xin, norm, weight, tag, gather_on_gateup, gather_on_down):
    hb = _rms_fwd(xin, norm, name=tag + "_norm")
    cargo, landed = gather_on_gateup()
    res = _gateup(hb, weight("w_gate_up"), name=tag + "_gateup", tm=1024, cargo=cargo)
    if cargo is not None:
        res, lands = res
        landed(lands)
    gate, up, act = res
    cargo, landed = gather_on_down()
    xout = _mm(act, weight("w_down"), res=xin, scale=0.5, name=tag + "_down", tm=1024, tn=512, tk=8192, cargo=cargo)
    if cargo is not None:
        xout, lands = xout
        landed(lands)
    return xout, (hb, gate, up, act)


def _ffn_bwd(dy, dyb, xin, norm, wgu, wd, saved, tag, scatter, gu_piece=(0, 1)):
    hb, gate, up, act = saved
    d_wd = _mm(act, dyb, mode="tn", scale=0.5, out_dtype=BF16, name=tag + "_dwd", tm=512, tn=1024, tk=4096)
    cargo = scatter(tag + "_w_down", d_wd)
    rest, rest_done = scatter.leftover()
    dgu = _dact_swiglu(dyb, wd, gate, up, name=tag + "_dact", cargo=rest)
    if rest is not None:
        dgu, lands = dgu
        rest_done(lands)
    d_wgu, lands = _mm(hb, dgu, mode="tn", halves="b", out_dtype=BF16, name=tag + "_dwgu", tm=1024, tn=512, tk=4096,
                       cargo=cargo)
    scatter.done(lands)
    cargo = scatter(tag + "_w_gate_up", d_wgu, piece=gu_piece)
    dh, lands = _mm(dgu, wgu, mode="nt", halves="a", out_dtype=BF16, name=tag + "_dh", tm=1024, tn=512, tk=2816,
                    cargo=cargo)
    scatter.done(lands)
    dx, dxb, d_norm = _rms_bwd(xin, dh, norm, dy, name=tag + "_dnorm")
    return dx, dxb, d_norm


def kernel(x, mem, ffn1_norm, ffn1_w_gate_up, ffn1_w_down, mix_norm, mem_norm, w_in, b_forget, pool_w, pool_scale, w_pool_up, fox_q_norm, fox_k_norm, w_fox_o, w_mem_kv, mem_q_norm, mem_k_norm, w_mem_o, w_out, ffn2_norm, ffn2_w_gate_up, ffn2_w_down, loss_target, m_ffn1_norm, m_ffn1_w_gate_up, m_ffn1_w_down, m_mix_norm, m_mem_norm, m_w_in, m_b_forget, m_pool_w, m_pool_scale, m_w_pool_up, m_fox_q_norm, m_fox_k_norm, m_w_fox_o, m_w_mem_kv, m_mem_q_norm, m_mem_k_norm, m_w_mem_o, m_w_out, m_ffn2_norm, m_ffn2_w_gate_up, m_ffn2_w_down, v_ffn1_norm, v_ffn1_w_gate_up, v_ffn1_w_down, v_mix_norm, v_mem_norm, v_w_in, v_b_forget, v_pool_w, v_pool_scale, v_w_pool_up, v_fox_q_norm, v_fox_k_norm, v_w_fox_o, v_w_mem_kv, v_mem_q_norm, v_mem_k_norm, v_w_mem_o, v_w_out, v_ffn2_norm, v_ffn2_w_gate_up, v_ffn2_w_down):
    P = dict(locals())
    big = ["ffn1_w_gate_up", "ffn1_w_down", "w_in", "w_pool_up", "w_fox_o", "w_mem_kv", "w_mem_o", "w_out",
           "ffn2_w_gate_up", "ffn2_w_down"]
    kinds = ["col", "row", "col", "col", "col", "row", "col", "row", "col", "row"]
    small = ["ffn1_norm", "mix_norm", "mem_norm", "b_forget", "pool_w", "pool_scale", "fox_q_norm", "fox_k_norm",
             "mem_q_norm", "mem_k_norm", "ffn2_norm"]
    order = ["ffn1_norm", "ffn1_w_gate_up", "ffn1_w_down", "mix_norm", "mem_norm", "w_in", "b_forget", "pool_w",
             "pool_scale", "w_pool_up", "fox_q_norm", "fox_k_norm", "w_fox_o", "w_mem_kv", "mem_q_norm", "mem_k_norm",
             "w_mem_o", "w_out", "ffn2_norm", "ffn2_w_gate_up", "ffn2_w_down"]

    nb, S, D = x.shape
    T = nb * S
    ML = mem.shape[1]
    NF, HD = b_forget.shape[-1], fox_q_norm.shape[-1]
    FW = NF * HD
    MW, MHD = w_mem_o.shape[1], mem_q_norm.shape[-1]
    PG, PD = pool_w.shape[1], pool_w.shape[2]
    PW = PG * PD
    n_in = w_in.shape[-1]
    n_in_pad = _round_up(n_in, LANES)
    in_w = 4 * n_in
    assert HD * 2 == LANES and MHD == LANES and PD == LANES and in_w == PW + 3 * FW + NF + MW + 3 * D

    shards = []
    for nme in big:
        wl = P[nme][0].astype(BF16)
        if nme == "w_in":
            wl = jnp.pad(wl, ((0, 0), (0, n_in_pad - n_in)))
        shards.append(wl)
    shard, kind_of, full = dict(zip(big, shards)), dict(zip(big, kinds)), {}

    def landed(names):
        def fill(lands):
            for nme, l in zip(names, lands):
                full[nme] = _ag_fill(l, shard[nme], kind_of[nme], name="ag_fill_" + nme)
        return fill

    def gather(names, piece=(0, 1), lands=None):
        return _ag_cargo([shard[nme] for nme in names], [kind_of[nme] for nme in names], piece, lands)

    landed(["ffn1_w_gate_up"])(_run_cargo(gather(["ffn1_w_gate_up"]), name="ag_first"))
    begun = {}

    def on_gateup1():
        def fill(lands):
            landed(["ffn1_w_down"])(lands[:1])
            begun["w_in"] = lands[1]
        return _join_cargo(gather(["ffn1_w_down"]), gather(["w_in"], (0, 2))), fill

    x0 = x.reshape(T, D)
    x1, sv1 = _ffn_fwd(x0, ffn1_norm, lambda nme: full["ffn1_" + nme], "ffn1", on_gateup1,
                       lambda: (gather(["w_in"], (1, 2), [begun["w_in"]]), landed(["w_in"])))
    win_o = jnp.concatenate([full["w_in"][:, j * n_in_pad:j * n_in_pad + n_in] for j in range(4)], axis=1)
    o_q, o_k, o_v, o_f = PW, PW + FW, PW + 2 * FW, PW + 3 * FW
    o_qm, o_gate = o_f + NF, o_f + NF + MW
    p_q, p_k, p_v, p_pool, p_qm, p_f = 3 * D, 3 * D + FW, 3 * D + 2 * FW, 3 * D + 3 * FW, 3 * D + 3 * FW + PW, 3 * D + 3 * FW + PW + MW
    inp = _round_up(p_f + LANES, 512)
    win_p = jnp.concatenate([win_o[:, o_gate:], win_o[:, o_q:o_f], win_o[:, :PW], win_o[:, o_qm:o_gate],
                             win_o[:, o_f:o_qm], jnp.zeros((D, inp - p_f - NF), BF16)], axis=1)

    hb2 = _rms_fwd(x1, mix_norm, name="mix_norm")
    later = ["ffn2_w_down", "w_out", "w_fox_o", "w_mem_kv", "w_pool_up", "w_mem_o"]
    z, lands = _mm(hb2, win_p, name="in_proj", tm=1024, tn=1536, cargo=gather(later))
    landed(later)(lands)

    vpool = _pool_fwd(z, p_pool // LANES, pool_w[0], pool_scale, nb, S, name="pool_fwd")
    y_pool = _mm(vpool, full["w_pool_up"], name="pool_up", tm=1024, tn=512)

    b128 = jnp.pad(b_forget, ((0, 0), (0, LANES - NF)))
    c3 = _fox_c(z, p_f // LANES, b128, nb, S, NF, name="fox_c")
    qa, ka, vb = _fox_qk(z, p_q // LANES, p_k // LANES, p_v // LANES, c3, fox_q_norm, fox_k_norm, HD, NF,
                         name="fox_qk")
    o_fox, lands = _fox_fwd(qa, ka, vb, nb, S, HD, name="fox_fwd", cargo=gather(["ffn2_w_gate_up"]))
    landed(["ffn2_w_gate_up"])(lands)
    y_fox = _mm(o_fox, full["w_fox_o"], name="fox_out", tm=1024, tn=512)

    mem2 = mem.reshape(nb * ML, D)
    memn = _rms_fwd(mem2, mem_norm, name="mem_norm")
    kv = _mm(memn, full["w_mem_kv"], name="mem_kv")
    kmn = _headnorm_fwd(kv, 0, MW, mem_k_norm, MHD, name="mem_knorm")
    vmb = kv[:, MW:].astype(BF16)
    qmn = _headnorm_fwd(z, p_qm // LANES, MW, mem_q_norm, MHD, name="mem_qnorm")
    o_mem = _mem_fwd(qmn, kmn, vmb, nb, S, ML, name="mem_fwd")
    y_mem = _mm(o_mem, full["w_mem_o"], name="mem_out", tm=1024, tn=512)

    merged = _merge_fwd(z, y_pool, y_fox, y_mem, name="merge_fwd")
    x2 = _mm(merged, full["w_out"], res=x1, name="out_proj", tm=1024, tn=512)
    x3, sv2 = _ffn_fwd(x2, ffn2_norm, lambda nme: full["ffn2_" + nme], "ffn2", lambda: (None, None),
                       lambda: (None, None))
    dy3, dy3b, loss_part = _loss(x3, loss_target.reshape(T, D), name="loss")

    g, parts, got = {}, {}, {}
    rs_kind = dict(kind_of, w_in="row")

    leftovers = []

    def scatter(*named_grads, piece=(0, 1)):
        names = named_grads[0::2]
        for nme, grad in zip(names, named_grads[1::2]):
            parts[nme] = _rs_pair_sum(grad, rs_kind[nme], name="rs_pair_" + nme)
        scatter.names = names
        if piece[1] == 2:
            leftovers.append(names)
        return _scatter_cargo([parts[nme] for nme in names], [rs_kind[nme] for nme in names], piece)

    def scattered(lands):
        got.update(zip(scatter.names, lands))

    def leftover():
        if not leftovers:
            return None, None
        names = leftovers.pop(0)
        cargo = _scatter_cargo([parts[nme] for nme in names], [rs_kind[nme] for nme in names], (1, 2),
                               [got[nme] for nme in names])
        return cargo, lambda lands: got.update(zip(names, lands))

    scatter.done, scatter.leftover = scattered, leftover
    dx2, dx2b, g["ffn2_norm"] = _ffn_bwd(dy3, dy3b, x2, ffn2_norm, full["ffn2_w_gate_up"], full["ffn2_w_down"], sv2,
                                         "ffn2", scatter, gu_piece=(0, 2))

    dmerged = _mm(dx2b, full["w_out"], mode="nt", name="d_merged", tm=1024, tn=512)
    g["w_out"] = _mm(merged, dx2b, mode="tn", out_dtype=BF16, name="d_w_out", tk=4096)
    dyp, dyf, dym, dgl = _merge_bwd(dmerged, z, y_pool, y_fox, y_mem, name="merge_bwd")

    g["w_pool_up"] = _mm(vpool, dyp, mode="tn", out_dtype=BF16, name="d_w_pool_up", tk=4096)
    dvp = _mm(dyp, full["w_pool_up"], mode="nt", name="d_vpool", tm=1024)
    du_pool, g["pool_w"], g["pool_scale"] = _pool_bwd(z, p_pool // LANES, dvp, pool_w[0], pool_scale, nb, S,
                                                      name="pool_bwd")

    g["w_fox_o"] = _mm(o_fox, dyf, mode="tn", out_dtype=BF16, name="d_w_fox_o", tk=4096)
    do_fox = _mm(dyf, full["w_fox_o"], mode="nt", out_dtype=BF16, name="d_o_fox", tm=1024)
    rest, rest_done = leftover()
    (dqa, dka, dvf, dcs), lands = _fox_bwd(qa, ka, vb, do_fox, nb, S, HD, name="fox_bwd", cargo=rest)
    rest_done(lands)
    dq, dgq = _headnorm_bwd(z, p_q // LANES, FW, dqa, fox_q_norm, HD, HD ** -0.5, True, BF16, name="fox_dq")
    dk, dgk = _headnorm_bwd(z, p_k // LANES, FW, dka, fox_k_norm, HD, 1.0, True, BF16, name="fox_dk")
    g["fox_q_norm"], g["fox_k_norm"] = _fold_gain(dgq, HD), _fold_gain(dgk, HD)
    dc = -dcs[:, :, :LANES // HD, :].reshape(nb, NF, S).transpose(0, 2, 1).reshape(T, NF)
    df, db = _fox_c_bwd(jnp.pad(dc, ((0, 0), (0, LANES - NF))), z, p_f // LANES, b128, nb, S, name="fox_c_bwd")
    g["b_forget"] = db[:, :NF]

    g["w_mem_o"] = _mm(o_mem, dym, mode="tn", out_dtype=BF16, name="d_w_mem_o", tk=4096)
    do_mem = _mm(dym, full["w_mem_o"], mode="nt", out_dtype=BF16, name="d_o_mem", tm=1024)
    dqmn, dkmn, dvm = _mem_bwd(qmn, kmn, vmb, do_mem, nb, S, ML, name="mem_bwd")
    dqm, g["mem_q_norm"] = _headnorm_bwd(z, p_qm // LANES, MW, dqmn, mem_q_norm, MHD, 1.0, False, BF16, name="mem_dq")
    dkm, g["mem_k_norm"] = _headnorm_bwd(kv, 0, MW, dkmn, mem_k_norm, MHD, 1.0, False, BF16, name="mem_dk")
    dkv = jnp.concatenate([dkm, dvm.astype(BF16)], axis=1)
    g["w_mem_kv"] = _mm(memn, dkv, mode="tn", out_dtype=BF16, name="d_w_mem_kv")
    dmemn = _mm(dkv, full["w_mem_kv"], mode="nt", name="d_memn")
    _, _, g["mem_norm"] = _rms_bwd(mem2, dmemn, mem_norm, jnp.zeros_like(mem2), name="mem_dnorm")

    dz = jnp.concatenate([dgl, dq, dk, dvf.astype(BF16), du_pool, dqm, df.astype(BF16),
                          jnp.zeros((T, inp - p_f - LANES), BF16)], axis=1)
    cargo = scatter(*sum([[nme, g[nme]] for nme in ("w_out", "w_pool_up", "w_fox_o", "w_mem_o", "w_mem_kv")], []))
    dwin_p, lands = _mm(dz, hb2, mode="tn", out_dtype=BF16, name="d_w_in", tm=512, tn=1024, tk=4096, cargo=cargo)
    scattered(lands)
    dwin_o = jnp.concatenate([dwin_p[p_pool:p_qm], dwin_p[p_q:p_pool], dwin_p[p_f:p_f + NF], dwin_p[p_qm:p_f],
                              dwin_p[:p_q]], axis=0)
    cargo = scatter("w_in", jnp.concatenate(
        [jnp.pad(dwin_o[j * n_in:(j + 1) * n_in], ((0, n_in_pad - n_in), (0, 0))) for j in range(4)], axis=0),
        piece=(0, 2))
    dh2, lands = _mm(dz, win_p, mode="nt", out_dtype=BF16, name="d_h2", tm=1024, tn=512, tk=3584, cargo=cargo)
    scattered(lands)
    dx1, dx1b, g["mix_norm"] = _rms_bwd(x1, dh2, mix_norm, dx2, name="mix_dnorm")

    dx0, _, g["ffn1_norm"] = _ffn_bwd(dx1, dx1b, x0, ffn1_norm, full["ffn1_w_gate_up"], full["ffn1_w_down"], sv1,
                                      "ffn1", scatter)

    halves = {nme: _rs_sum_join(parts[nme], got[nme], rs_kind[nme], name="rs_join_" + nme) for nme in big}
    gfull = {}

    gpack, sizes = _pack_small([g[nme].reshape(P[nme].shape) for nme in small] + [loss_part[:, :1]])
    gsum = _unpack_small(_all_reduce_small(gpack), sizes)
    loss = gsum[-1].reshape(())
    for nme, a in zip(small, gsum[:-1]):
        gfull[nme] = a

    delta, new_m, new_v = {}, {}, {}
    for nme in big:
        shp = P[nme].shape
        two = (lambda a: a.reshape(shp[-2:]).T) if nme == "w_in" else (lambda a: a.reshape(shp[-2:]))
        back = (lambda a: a.T.reshape(shp)) if nme == "w_in" else (lambda a: a.reshape(shp))
        gfull[nme], delta[nme], new_m[nme], new_v[nme] = [
            back(a) for a in _adamw_join(two(P[nme]), *halves[nme], two(P["m_" + nme]), two(P["v_" + nme]),
                                         name="adamw_" + nme)]
    wpack, _ = _pack_small([P[nme] for nme in small])
    mpack, _ = _pack_small([P["m_" + nme] for nme in small])
    vpack, _ = _pack_small([P["v_" + nme] for nme in small])
    gpack2, ssz = _pack_small([gfull[nme] for nme in small])
    for dct, pk in zip((delta, new_m, new_v), _adamw(wpack, gpack2, mpack, vpack, name="adamw_small")):
        for nme, a in zip(small, _unpack_small(pk, ssz)):
            dct[nme] = a

    grad_x = dx0.reshape(nb, S, D)
    return (loss, grad_x, *[gfull[nme] for nme in order], *[delta[nme] for nme in order],
            *[new_m[nme] for nme in order], *[new_v[nme] for nme in order])
```

```python
import functools

import jax
import jax.numpy as jnp
from jax import lax
from jax.experimental import pallas as pl
from jax.experimental.pallas import tpu as pltpu

F32 = jnp.float32
BF16 = jnp.bfloat16
MESH = pl.DeviceIdType.MESH
EPS = 1e-6
POOL_WINDOWS = (2, 4, 8, 16)
ADAM_LR, ADAM_B1, ADAM_B2, ADAM_EPS, ADAM_WD, ADAM_STEP = 0.001, 0.9, 0.999, 1e-08, 0.01, 10
LANES = 128
VMEM_LIMIT = 56 * 1024 * 1024
NEG = -1e30


def _pc(body, *, name, cargo=None, **kw):
    if cargo is not None:
        body, kw = _load_cargo(body, cargo, kw)
    return pl.pallas_call(body, name=name, **kw)


def _cp(*sem):
    return pltpu.CompilerParams(dimension_semantics=sem, vmem_limit_bytes=VMEM_LIMIT)


def _tile(n, pref, mult=LANES):
    if n <= pref:
        return n
    t = (pref // mult) * mult
    while t >= mult:
        if n % t == 0:
            return t
        t -= mult
    return n


def _round_up(n, m):
    return (n + m - 1) // m * m


_DIMS = {"nn": (((1,), (0,)), ((), ())), "nt": (((1,), (1,)), ((), ())), "tn": (((0,), (0,)), ((), ()))}


def _split_cargo(res, cargo):
    if cargo is None:
        return res
    nco = len(cargo.out_shapes)
    own = res[:len(res) - nco]
    return (own[0] if len(own) == 1 else own), list(res[len(res) - nco:])


def _mm(a, b, *, name, mode="nn", out_dtype=F32, scale=1.0, res=None, tm=512, tn=512, tk=2048, cargo=None,
        halves=None):
    if halves == "a":
        assert mode == "nt"
        (_, M, Kh), (N, K2) = a.shape, b.shape
        K = 2 * Kh
    elif halves == "b":
        assert mode == "tn"
        (K, M), (_, K2, Nh) = a.shape, b.shape
        N = 2 * Nh
    elif mode == "nn":
        (M, K), (K2, N) = a.shape, b.shape
    elif mode == "nt":
        (M, K), (N, K2) = a.shape, b.shape
    else:
        (K, M), (K2, N) = a.shape, b.shape
    assert K == K2, (name, a.shape, b.shape)
    tm = _tile(M, tm, 8 if M % LANES else LANES)
    tn = _tile(N // 2 if halves == "b" else N, tn)
    tk = _tile(K // 2 if halves == "a" else K, tk)
    nk = K // tk
    dims = _DIMS[mode]
    if halves == "a":
        a_spec = pl.BlockSpec((None, tm, tk), lambda i, j, k: (k // (nk // 2), i, k % (nk // 2)))
    elif mode == "tn":
        a_spec = pl.BlockSpec((tk, tm), lambda i, j, k: (k, i))
    else:
        a_spec = pl.BlockSpec((tm, tk), lambda i, j, k: (i, k))
    if halves == "b":
        njh = N // 2 // tn
        b_spec = pl.BlockSpec((None, tk, tn), lambda i, j, k: (j // njh, k, j % njh))
    elif mode == "nt":
        b_spec = pl.BlockSpec((tn, tk), lambda i, j, k: (j, k))
    else:
        b_spec = pl.BlockSpec((tk, tn), lambda i, j, k: (k, j))
    o_spec = pl.BlockSpec((tm, tn), lambda i, j, k: (i, j))
    has_res = res is not None

    def body(*refs):
        a_ref, b_ref = refs[:2]
        r_ref = refs[2] if has_res else None
        o_ref = refs[3] if has_res else refs[2]

        def finish(v):
            v = v * scale
            if has_res:
                v = r_ref[...] + v
            o_ref[...] = v.astype(o_ref.dtype)

        prod = lax.dot_general(a_ref[...], b_ref[...], dims, preferred_element_type=F32)
        if nk == 1:
            finish(prod)
            return
        acc, k = refs[-1], pl.program_id(2)

        @pl.when(k == 0)
        def _():
            acc[...] = prod

        @pl.when(k > 0)
        def _():
            acc[...] += prod

        @pl.when(k == nk - 1)
        def _():
            finish(acc[...])

    out = _pc(body, name=name, cargo=cargo, out_shape=jax.ShapeDtypeStruct((M, N), out_dtype),
              grid=(M // tm, N // tn, nk),
              in_specs=[a_spec, b_spec] + ([o_spec] if has_res else []), out_specs=o_spec,
              scratch_shapes=[] if nk == 1 else [pltpu.VMEM((tm, tn), F32)],
              compiler_params=_cp("parallel", "parallel", "arbitrary"))(
                  *((a, b, res) if has_res else (a, b)), *(cargo.arrays if cargo else ()))
    return _split_cargo(out, cargo)


def _dact_swiglu(dyb, wd, gate, up, *, name, tm=1024, tn=512, cargo=None):
    M, K = dyb.shape
    Fh = wd.shape[0]
    tm, tn = _tile(M, tm), _tile(Fh, tn)

    rows = _tile(tm, 256, 16)

    def body(dy_ref, w_ref, g_ref, u_ref, o_ref):
        w = w_ref[...]
        for r in range(0, tm, rows):
            sl = slice(r, r + rows)
            d = lax.dot_general(dy_ref[sl, :], w, _DIMS["nt"], preferred_element_type=F32) * 0.5
            g, u = g_ref[sl, :].astype(F32), u_ref[sl, :].astype(F32)
            s = jax.nn.sigmoid(g)
            o_ref[0, sl, :] = (d * u * (s * (1.0 + g * (1.0 - s)))).astype(BF16)
            o_ref[1, sl, :] = (d * (g * s)).astype(BF16)

    t_spec = pl.BlockSpec((tm, tn), lambda i, j: (i, j))
    out = _pc(body, name=name, cargo=cargo, out_shape=jax.ShapeDtypeStruct((2, M, Fh), BF16),
              grid=(M // tm, Fh // tn),
              in_specs=[pl.BlockSpec((tm, K), lambda i, j: (i, 0)), pl.BlockSpec((tn, K), lambda i, j: (j, 0)),
                        t_spec, t_spec],
              out_specs=pl.BlockSpec((2, tm, tn), lambda i, j: (0, i, j)),
              compiler_params=_cp("parallel", "parallel"))(dyb, wd, gate, up, *(cargo.arrays if cargo else ()))
    return _split_cargo(out, cargo)


def _gateup(h, wgu, *, name, tm=512, tn=512, cargo=None):
    M, K = h.shape
    Fh = wgu.shape[1] // 2
    tm, tn = _tile(M, tm), _tile(Fh, tn)
    nj = Fh // tn

    def body(h_ref, wg_ref, wu_ref, g_ref, u_ref, a_ref):
        hv = h_ref[...]
        g = jnp.dot(hv, wg_ref[...], preferred_element_type=F32)
        u = jnp.dot(hv, wu_ref[...], preferred_element_type=F32)
        g_ref[...] = g.astype(BF16)
        u_ref[...] = u.astype(BF16)
        a_ref[...] = (g * jax.nn.sigmoid(g) * u).astype(BF16)

    o_spec = pl.BlockSpec((tm, tn), lambda i, j: (i, j))
    shp = jax.ShapeDtypeStruct((M, Fh), BF16)
    out = _pc(body, name=name, cargo=cargo, out_shape=(shp, shp, shp), grid=(M // tm, nj),
              in_specs=[pl.BlockSpec((tm, K), lambda i, j: (i, 0)),
                        pl.BlockSpec((K, tn), lambda i, j: (0, j)),
                        pl.BlockSpec((K, tn), lambda i, j: (0, j + nj))],
              out_specs=(o_spec, o_spec, o_spec),
              compiler_params=_cp("parallel", "parallel"))(h, wgu, wgu, *(cargo.arrays if cargo else ()))
    return _split_cargo(out, cargo)


def _rowmap(fn, ins, outs, *, rows, tm, name, accs=()):
    tm = _tile(rows, tm, 8)
    arrs, in_specs = [], []
    for d in ins:
        if d[0] == "row":
            _, a, w, cb = d
            w = a.shape[1] if w is None else w
            in_specs.append(pl.BlockSpec((tm, w), functools.partial(lambda i, cb: (i, cb), cb=cb)))
        else:
            a = d[1]
            in_specs.append(pl.BlockSpec(a.shape, functools.partial(lambda i, nd: (0,) * nd, nd=a.ndim)))
        arrs.append(a)
    out_shape = [jax.ShapeDtypeStruct((rows, w), dt) for w, dt in outs]
    out_specs = [pl.BlockSpec((tm, w), lambda i: (i, 0)) for w, _ in outs]
    for shp, dt in accs:
        out_shape.append(jax.ShapeDtypeStruct(shp, dt))
        out_specs.append(pl.BlockSpec(shp, functools.partial(lambda i, nd: (0,) * nd, nd=len(shp))))
    nin, nout, nacc = len(ins), len(outs), len(accs)

    def body(*refs):
        res = fn(*[r[...] for r in refs[:nin]])
        res = res if isinstance(res, (tuple, list)) else (res,)
        for r, v in zip(refs[nin:nin + nout], res[:nout]):
            r[...] = v.astype(r.dtype)
        if nacc:
            acc_refs = refs[nin + nout:]

            @pl.when(pl.program_id(0) == 0)
            def _():
                for r in acc_refs:
                    r[...] = jnp.zeros_like(r)

            for r, v in zip(acc_refs, res[nout:]):
                r[...] += v

    out = _pc(body, name=name, out_shape=tuple(out_shape), grid=(rows // tm,), in_specs=in_specs,
              out_specs=tuple(out_specs), compiler_params=_cp("arbitrary" if nacc else "parallel"))(*arrs)
    return out[0] if len(out) == 1 else out


def _rms_fwd(x, g, *, name):
    def fn(xv, gv):
        return xv * lax.rsqrt(jnp.mean(xv * xv, axis=-1, keepdims=True) + EPS) * gv
    return _rowmap(fn, [("row", x, None, 0), ("bc", g)], [(x.shape[1], BF16)], rows=x.shape[0], tm=256, name=name)


def _rms_bwd(x, dh, g, dres, *, name):
    D = x.shape[1]

    def fn(xv, dhv, gv, drv):
        r = lax.rsqrt(jnp.mean(xv * xv, axis=-1, keepdims=True) + EPS)
        xh = xv * r
        dhf = dhv.astype(F32)
        dxn = dhf * gv
        dx = drv + r * (dxn - xh * jnp.mean(dxn * xh, axis=-1, keepdims=True))
        return dx, dx, jnp.sum(dhf * xh, axis=0, keepdims=True)

    return _rowmap(fn, [("row", x, None, 0), ("row", dh, None, 0), ("bc", g), ("row", dres, None, 0)],
                   [(D, F32), (D, BF16)], rows=x.shape[0], tm=256, name=name, accs=[((1, D), F32)])


def _loss(y, tgt, *, name):
    D = y.shape[1]

    def fn(yv, tv):
        e = yv - tv
        d = e * (1.0 / D)
        part = 0.5 * jnp.sum(jnp.sum(e * e, axis=-1, keepdims=True) * (1.0 / D), axis=0, keepdims=True)
        return d, d, jnp.broadcast_to(part, (1, LANES))

    return _rowmap(fn, [("row", y, None, 0), ("row", tgt, None, 0)], [(D, F32), (D, BF16)],
                   rows=y.shape[0], tm=256, name=name, accs=[((1, LANES), F32)])


def _merge_fwd(z, yp, yf, ym, *, name):
    D = yp.shape[1]

    def fn(gp, gf, gm, a, b, c):
        return jax.nn.sigmoid(gp) * a + jax.nn.sigmoid(gf) * b + jax.nn.sigmoid(gm) * c

    return _rowmap(fn, [("row", z, D, 0), ("row", z, D, 1), ("row", z, D, 2),
                        ("row", yp, None, 0), ("row", yf, None, 0), ("row", ym, None, 0)],
                   [(D, BF16)], rows=yp.shape[0], tm=128, name=name)


def _merge_bwd(dm, z, yp, yf, ym, *, name):
    D = yp.shape[1]

    def fn(d, gp, gf, gm, a, b, c):
        outs, dgl = [], []
        for gl, yv in ((gp, a), (gf, b), (gm, c)):
            s = jax.nn.sigmoid(gl)
            outs.append(d * s)
            dgl.append(d * yv * (s * (1.0 - s)))
        return (*outs, jnp.concatenate(dgl, axis=-1))

    return _rowmap(fn, [("row", dm, None, 0), ("row", z, D, 0), ("row", z, D, 1), ("row", z, D, 2),
                        ("row", yp, None, 0), ("row", yf, None, 0), ("row", ym, None, 0)],
                   [(D, BF16), (D, BF16), (D, BF16), (3 * D, BF16)], rows=yp.shape[0], tm=128, name=name)


def _adamw(w, g, m, v, *, name):
    C = w.shape[1]

    def fn(wv, gv, mv, vv):
        m2 = ADAM_B1 * mv + (1.0 - ADAM_B1) * gv
        v2 = ADAM_B2 * vv + (1.0 - ADAM_B2) * (gv * gv)
        m_hat = m2 / (1.0 - ADAM_B1 ** ADAM_STEP)
        v_hat = v2 / (1.0 - ADAM_B2 ** ADAM_STEP)
        return -ADAM_LR * (m_hat / (jnp.sqrt(v_hat) + ADAM_EPS) + ADAM_WD * wv), m2, v2

    tm = max(8, (262144 // C) // 8 * 8)
    return _rowmap(fn, [("row", a, None, 0) for a in (w, g, m, v)], [(C, F32)] * 3,
                   rows=w.shape[0], tm=tm, name=name)


def _seg_mean(v, hd):
    if hd == LANES:
        return jnp.mean(v, axis=-1, keepdims=True)
    lo = lax.broadcasted_iota(jnp.int32, v.shape, 1) < hd
    s0 = jnp.sum(jnp.where(lo, v, 0.0), axis=-1, keepdims=True)
    s1 = jnp.sum(jnp.where(lo, 0.0, v), axis=-1, keepdims=True)
    return jnp.where(lo, s0, s1) * (1.0 / hd)


def _gain128(g, hd):
    g = g.reshape(1, hd).astype(F32)
    return g if hd == LANES else jnp.concatenate([g, g], axis=-1)


def _headnorm_fwd(x, cb0, width, gain, hd, *, name):
    rows = x.shape[0]
    tm = _tile(rows, 512, 8)
    g128 = _gain128(gain, hd)

    def body(x_ref, g_ref, o_ref):
        xv = x_ref[...]
        o_ref[...] = (xv * lax.rsqrt(_seg_mean(xv * xv, hd) + EPS) * g_ref[...]).astype(BF16)

    return _pc(body, name=name, out_shape=jax.ShapeDtypeStruct((rows, width), BF16),
               grid=(rows // tm, width // LANES),
               in_specs=[pl.BlockSpec((tm, LANES), lambda i, j: (i, cb0 + j)),
                         pl.BlockSpec((1, LANES), lambda i, j: (0, 0))],
               out_specs=pl.BlockSpec((tm, LANES), lambda i, j: (i, j)),
               compiler_params=_cp("parallel", "parallel"))(x, g128)


def _headnorm_bwd(x, cb0, width, dy, gain, hd, post, spread, out_dtype, *, name):
    rows = x.shape[0]
    tm = _tile(rows, 512, 8)
    g128 = _gain128(gain, hd)
    dyw = 2 * LANES if spread else LANES

    def body(x_ref, dy_ref, g_ref, dx_ref, dg_ref):
        xv = x_ref[...]
        dyv = dy_ref[...]
        if spread:
            dyv = jnp.concatenate([dyv[:, :hd], dyv[:, LANES:LANES + hd]], axis=-1)
        dyv = dyv * post
        r = lax.rsqrt(_seg_mean(xv * xv, hd) + EPS)
        xh = xv * r
        dxn = dyv * g_ref[...]
        dx_ref[...] = (r * (dxn - xh * _seg_mean(dxn * xh, hd))).astype(dx_ref.dtype)

        @pl.when((pl.program_id(0) == 0) & (pl.program_id(1) == 0))
        def _():
            dg_ref[...] = jnp.zeros_like(dg_ref)

        dg_ref[...] += jnp.sum(dyv * xh, axis=0, keepdims=True)

    return _pc(body, name=name,
               out_shape=(jax.ShapeDtypeStruct((rows, width), out_dtype), jax.ShapeDtypeStruct((1, LANES), F32)),
               grid=(rows // tm, width // LANES),
               in_specs=[pl.BlockSpec((tm, LANES), lambda i, j: (i, cb0 + j)),
                         pl.BlockSpec((tm, dyw), lambda i, j: (i, j)),
                         pl.BlockSpec((1, LANES), lambda i, j: (0, 0))],
               out_specs=(pl.BlockSpec((tm, LANES), lambda i, j: (i, j)),
                          pl.BlockSpec((1, LANES), lambda i, j: (0, 0))),
               compiler_params=_cp("arbitrary", "arbitrary"))(x, dy, g128)


def _fold_gain(dg, hd):
    return dg if hd == LANES else dg[:, :hd] + dg[:, hd:]


def _shift_down(v, k, row):
    return jnp.where(row >= k, pltpu.roll(v, k, 0), 0.0)


def _shift_up(v, k, row, S):
    return jnp.where(row < S - k, pltpu.roll(v, S - k, 0), 0.0)


def _window_sum(v, g, row, shift):
    s, outs = v, []
    for k in (1, 2, 4, 8):
        s = s + shift(s, k)
        outs.append(s)
    return jnp.where(g == 0, outs[0], jnp.where(g == 1, outs[1], jnp.where(g == 2, outs[2], outs[3])))


def _pool_fwd(z, cb0, pool_w, pool_scale, nb, S, *, name):
    PG, PD = pool_w.shape[0], pool_w.shape[1]
    assert PD == LANES and PG <= len(POOL_WINDOWS)
    pw = pool_w.astype(BF16)

    def body(u_ref, w_ref, s_ref, o_ref):
        g = pl.program_id(0)
        u = u_ref[...]
        row = lax.broadcasted_iota(jnp.int32, u.shape, 0)
        cnt = jnp.minimum(row + 1, jnp.left_shift(2, g)).astype(F32)
        pooled = _window_sum(u, g, row, lambda v, k: _shift_down(v, k, row)) / cnt
        mixed = jnp.dot((pooled - u).astype(BF16), w_ref[...], preferred_element_type=F32)
        o_ref[...] = (mixed * s_ref[...]).astype(BF16)

    return _pc(body, name=name, out_shape=jax.ShapeDtypeStruct((nb * S, PG * PD), BF16), grid=(PG, nb),
               in_specs=[pl.BlockSpec((S, PD), lambda g, b: (b, cb0 + g)),
                         pl.BlockSpec((None, PD, PD), lambda g, b: (g, 0, 0)),
                         pl.BlockSpec((1, PD), lambda g, b: (0, g))],
               out_specs=pl.BlockSpec((S, PD), lambda g, b: (b, g)),
               compiler_params=_cp("parallel", "parallel"))(z, pw, pool_scale)


def _pool_bwd(z, cb0, dv, pool_w, pool_scale, nb, S, *, name):
    PG, PD = pool_w.shape[0], pool_w.shape[1]
    pw = pool_w.astype(BF16)

    def body(u_ref, dv_ref, w_ref, s_ref, du_ref, dw_ref, ds_ref):
        g, b = pl.program_id(0), pl.program_id(1)
        u, dvv, w = u_ref[...], dv_ref[...], w_ref[...]
        row = lax.broadcasted_iota(jnp.int32, u.shape, 0)
        cnt = jnp.minimum(row + 1, jnp.left_shift(2, g)).astype(F32)
        diff = (_window_sum(u, g, row, lambda v, k: _shift_down(v, k, row)) / cnt - u).astype(BF16)
        mixed = jnp.dot(diff, w, preferred_element_type=F32)
        dmixed = (dvv * s_ref[...]).astype(BF16)
        ddiff = lax.dot_general(dmixed, w, _DIMS["nt"], preferred_element_type=F32)
        du_ref[...] = (_window_sum(ddiff / cnt, g, row, lambda v, k: _shift_up(v, k, row, S)) - ddiff).astype(BF16)

        @pl.when(b == 0)
        def _():
            dw_ref[...] = jnp.zeros_like(dw_ref)
            ds_ref[...] = jnp.zeros_like(ds_ref)

        dw_ref[...] += lax.dot_general(diff, dmixed, _DIMS["tn"], preferred_element_type=F32)
        ds_ref[...] += jnp.sum(dvv * mixed, axis=0, keepdims=True)

    return _pc(body, name=name,
               out_shape=(jax.ShapeDtypeStruct((nb * S, PG * PD), BF16), jax.ShapeDtypeStruct((PG, PD, PD), F32),
                          jax.ShapeDtypeStruct((1, PG * PD), F32)),
               grid=(PG, nb),
               in_specs=[pl.BlockSpec((S, PD), lambda g, b: (b, cb0 + g)),
                         pl.BlockSpec((S, PD), lambda g, b: (b, g)),
                         pl.BlockSpec((None, PD, PD), lambda g, b: (g, 0, 0)),
                         pl.BlockSpec((1, PD), lambda g, b: (0, g))],
               out_specs=(pl.BlockSpec((S, PD), lambda g, b: (b, g)),
                          pl.BlockSpec((None, PD, PD), lambda g, b: (g, 0, 0)),
                          pl.BlockSpec((1, PD), lambda g, b: (0, g))),
               compiler_params=_cp("parallel", "arbitrary"))(z, dv, pw, pool_scale)


def _split3(c):
    hi = c.astype(BF16)
    r1 = c - hi.astype(F32)
    mid = r1.astype(BF16)
    lo = (r1 - mid.astype(F32)).astype(BF16)
    return hi, mid, lo


def _log_sigmoid(v):
    return jnp.minimum(v, 0.0) - jnp.log(1.0 + jnp.exp(-jnp.abs(v)))


def _fox_c(z, cb, b128, nb, S, NF, *, name):
    def body(f_ref, b_ref, hi_ref, mid_ref, lo_ref):
        v = f_ref[...] + b_ref[...]
        lane = lax.broadcasted_iota(jnp.int32, v.shape, 1)
        row = lax.broadcasted_iota(jnp.int32, v.shape, 0)
        c = jnp.where(lane < NF, _log_sigmoid(v), 0.0)
        k = 1
        while k < S:
            c = c + _shift_down(c, k, row)
            k *= 2
        hi, mid, lo = _split3(c)
        hi_ref[...], mid_ref[...], lo_ref[...] = hi, mid, lo

    shp = jax.ShapeDtypeStruct((nb * S, LANES), BF16)
    spec = pl.BlockSpec((S, LANES), lambda i: (i, 0))
    return _pc(body, name=name, out_shape=(shp, shp, shp), grid=(nb,),
               in_specs=[pl.BlockSpec((S, LANES), lambda i: (i, cb)), pl.BlockSpec((1, LANES), lambda i: (0, 0))],
               out_specs=(spec, spec, spec), compiler_params=_cp("parallel"))(z, b128)


def _fox_c_bwd(dc, z, cb, b128, nb, S, *, name):
    def body(dc_ref, f_ref, b_ref, df_ref, db_ref):
        d = dc_ref[...]
        row = lax.broadcasted_iota(jnp.int32, d.shape, 0)
        k = 1
        while k < S:
            d = d + _shift_up(d, k, row, S)
            k *= 2
        df = d * jax.nn.sigmoid(-(f_ref[...] + b_ref[...]))
        df_ref[...] = df

        @pl.when(pl.program_id(0) == 0)
        def _():
            db_ref[...] = jnp.zeros_like(db_ref)

        db_ref[...] += jnp.sum(df, axis=0, keepdims=True)

    return _pc(body, name=name,
               out_shape=(jax.ShapeDtypeStruct((nb * S, LANES), F32), jax.ShapeDtypeStruct((1, LANES), F32)),
               grid=(nb,),
               in_specs=[pl.BlockSpec((S, LANES), lambda i: (i, 0)), pl.BlockSpec((S, LANES), lambda i: (i, cb)),
                         pl.BlockSpec((1, LANES), lambda i: (0, 0))],
               out_specs=(pl.BlockSpec((S, LANES), lambda i: (i, 0)), pl.BlockSpec((1, LANES), lambda i: (0, 0))),
               compiler_params=_cp("arbitrary"))(dc, z, b128)


def _fox_qk(z, cbq, cbk, cbv, c3, gq, gk, HD, NF, *, name):
    T = z.shape[0]
    HP = NF * HD // LANES
    tm = _tile(T, 512, 8)
    scale = HD ** -0.5
    gq128, gk128 = _gain128(gq, HD), _gain128(gk, HD)

    def body(q_ref, k_ref, v_ref, hi_ref, mid_ref, lo_ref, gq_ref, gk_ref, qa_ref, ka_ref, vb_ref):
        hp = pl.program_id(1)
        q, k = q_ref[...], k_ref[...]
        qn = q * lax.rsqrt(_seg_mean(q * q, HD) + EPS) * (gq_ref[...] * scale)
        kn = k * lax.rsqrt(_seg_mean(k * k, HD) + EPS) * gk_ref[...]
        vb_ref[...] = v_ref[...].astype(BF16)
        c3v = jnp.concatenate([hi_ref[...], mid_ref[...], lo_ref[...]], axis=-1)
        r = lax.broadcasted_iota(jnp.int32, (3 * LANES, LANES - HD), 0)
        cc = lax.broadcasted_iota(jnp.int32, (3 * LANES, LANES - HD), 1)
        lane = lax.broadcasted_iota(jnp.int32, (tm, LANES - HD), 1)
        for hh in range(LANES // HD):
            head = hp * (LANES // HD) + hh
            sel_q = jnp.where((cc < 3) & (r == head + LANES * cc), 1.0, 0.0).astype(BF16)
            sel_k = jnp.where((cc >= 3) & (cc < 6) & (r == head + LANES * (cc - 3)), 1.0, 0.0).astype(BF16)
            qaug = jnp.dot(c3v, sel_q, preferred_element_type=F32) + jnp.where((lane >= 3) & (lane < 6), 1.0, 0.0)
            kaug = (jnp.where((lane < 3) | ((lane >= 6) & (lane < 9)), 1.0, 0.0)
                    - jnp.dot(c3v, sel_k, preferred_element_type=F32))
            sl = slice(hh * HD, (hh + 1) * HD)
            qa_ref[:, hh * LANES:(hh + 1) * LANES] = jnp.concatenate([qn[:, sl], qaug], axis=-1).astype(BF16)
            ka_ref[:, hh * LANES:(hh + 1) * LANES] = jnp.concatenate([kn[:, sl], kaug], axis=-1).astype(BF16)

    aw = (LANES // HD) * LANES
    blk = lambda cb: pl.BlockSpec((tm, LANES), functools.partial(lambda i, j, cb: (i, cb + j), cb=cb))
    cspec = pl.BlockSpec((tm, LANES), lambda i, j: (i, 0))
    gspec = pl.BlockSpec((1, LANES), lambda i, j: (0, 0))
    return _pc(body, name=name,
               out_shape=(jax.ShapeDtypeStruct((T, HP * aw), BF16), jax.ShapeDtypeStruct((T, HP * aw), BF16),
                          jax.ShapeDtypeStruct((T, NF * HD), BF16)),
               grid=(T // tm, HP),
               in_specs=[blk(cbq), blk(cbk), blk(cbv), cspec, cspec, cspec, gspec, gspec],
               out_specs=(pl.BlockSpec((tm, aw), lambda i, j: (i, j)), pl.BlockSpec((tm, aw), lambda i, j: (i, j)),
                          pl.BlockSpec((tm, LANES), lambda i, j: (i, j))),
               compiler_params=_cp("parallel", "parallel"))(z, z, z, *c3, gq128, gk128)


def _fox_logits(qa, ka, fill):
    s = lax.dot_general(qa, ka, _DIMS["nt"], preferred_element_type=F32)
    tq, L = s.shape
    diag = s[:, L - tq:]
    keep = lax.broadcasted_iota(jnp.int32, diag.shape, 1) <= lax.broadcasted_iota(jnp.int32, diag.shape, 0)
    diag = jnp.where(keep, diag, fill)
    return diag if L == tq else jnp.concatenate([s[:, :L - tq], diag], axis=-1), keep


def _fox_fwd(qa, ka, vb, nb, S, HD, *, name, tq=256, cargo=None):
    T = qa.shape[0]
    nh = LANES // HD
    aw = nh * LANES
    HP = qa.shape[1] // aw
    tq = _tile(S, tq, 8)
    nq = S // tq

    def body(qa_ref, ka_ref, v_ref, o_ref, qb_ref):
        lane = lax.broadcasted_iota(jnp.int32, (tq, LANES), 1)
        for k in range(nq):
            @pl.when(pl.program_id(2) == k)
            def _(k=k):
                L = (k + 1) * tq
                outs = []
                v = v_ref[0:L, :]
                for hh in range(nh):
                    al = slice(hh * LANES, (hh + 1) * LANES)
                    qav = qa_ref[:, al]
                    s, _ = _fox_logits(qav, ka_ref[0:L, al], NEG)
                    m = jnp.max(s, axis=-1, keepdims=True)
                    e = jnp.exp(s - m)
                    l = jnp.sum(e, axis=-1, keepdims=True)
                    p = e / l
                    outs.append(jnp.dot(p.astype(BF16), v[:, hh * HD:(hh + 1) * HD], preferred_element_type=F32))
                    hi, mid, lo = _split3(-(m + jnp.log(l)))
                    qb_ref[:, al] = jnp.where(lane == HD + 6, hi, jnp.where(lane == HD + 7, mid,
                                                                         jnp.where(lane == HD + 8, lo, qav)))
                o_ref[...] = jnp.concatenate(outs, axis=-1).astype(BF16)

    qspec = pl.BlockSpec((tq, aw), lambda b, h, i: (b * nq + i, h))
    out = _pc(body, name=name, cargo=cargo,
              out_shape=(jax.ShapeDtypeStruct((T, HP * LANES), BF16), jax.ShapeDtypeStruct(qa.shape, BF16)),
              grid=(nb, HP, nq),
              in_specs=[qspec, pl.BlockSpec((S, aw), lambda b, h, i: (b, h)),
                        pl.BlockSpec((S, LANES), lambda b, h, i: (b, h))],
              out_specs=(pl.BlockSpec((tq, LANES), lambda b, h, i: (b * nq + i, h)), qspec),
              compiler_params=_cp("parallel", "parallel", "arbitrary"))(qa, ka, vb, *(cargo.arrays if cargo else ()))
    return _split_cargo(out, cargo)


def _fox_bwd(qa, ka, vb, do, nb, S, HD, *, name, tq=256, cargo=None):
    T = qa.shape[0]
    nh = LANES // HD
    aw = nh * LANES
    HP = qa.shape[1] // aw
    tq = _tile(S, tq, 8)
    nq = S // tq

    def body(qa_ref, ka_ref, v_ref, do_ref, dqa_ref, dka_ref, dv_ref, dcs_ref):
        i = pl.program_id(2)

        @pl.when(i == 0)
        def _():
            dka_ref[...] = jnp.zeros_like(dka_ref)
            dv_ref[...] = jnp.zeros_like(dv_ref)
            dcs_ref[...] = jnp.zeros_like(dcs_ref)

        for k in range(nq):
            @pl.when(i == k)
            def _(k=k):
                L = (k + 1) * tq
                dkas, dvs, css = [], [], []
                v, dob = v_ref[0:L, :], do_ref[...]
                for hh in range(nh):
                    al = slice(hh * LANES, (hh + 1) * LANES)
                    hl = slice(hh * HD, (hh + 1) * HD)
                    qav, kav, vv, dov = qa_ref[:, al], ka_ref[0:L, al], v[:, hl], dob[:, hl]
                    p = jnp.exp(_fox_logits(qav, kav, NEG)[0])
                    dp = lax.dot_general(dov, vv, _DIMS["nt"], preferred_element_type=F32)
                    dsf = p * (dp - jnp.sum(p * dp, axis=-1, keepdims=True))
                    css.append(jnp.sum(dsf, axis=0, keepdims=True))
                    ds = dsf.astype(BF16)
                    dqa_ref[:, al] = jnp.dot(ds, kav, preferred_element_type=F32)
                    dkas.append(lax.dot_general(ds, qav, _DIMS["tn"], preferred_element_type=F32))
                    dvs.append(lax.dot_general(p.astype(BF16), dov, _DIMS["tn"], preferred_element_type=F32))
                dka_ref[0:L, :] += jnp.concatenate(dkas, axis=-1)
                dv_ref[0:L, :] += jnp.concatenate(dvs, axis=-1)
                dcs_ref[:, 0:L] += jnp.concatenate(css + [jnp.zeros((8 - nh, L), F32)], axis=0)

    out = _pc(body, name=name, cargo=cargo,
              out_shape=(jax.ShapeDtypeStruct((T, HP * aw), F32), jax.ShapeDtypeStruct((T, HP * aw), F32),
                         jax.ShapeDtypeStruct((T, HP * LANES), F32), jax.ShapeDtypeStruct((nb, HP, 8, S), F32)),
              grid=(nb, HP, nq),
              in_specs=[pl.BlockSpec((tq, aw), lambda b, h, i: (b * nq + i, h)),
                        pl.BlockSpec((S, aw), lambda b, h, i: (b, h)),
                        pl.BlockSpec((S, LANES), lambda b, h, i: (b, h)),
                        pl.BlockSpec((tq, LANES), lambda b, h, i: (b * nq + i, h))],
              out_specs=(pl.BlockSpec((tq, aw), lambda b, h, i: (b * nq + i, h)),
                         pl.BlockSpec((S, aw), lambda b, h, i: (b, h)),
                         pl.BlockSpec((S, LANES), lambda b, h, i: (b, h)),
                         pl.BlockSpec((None, None, 8, S), lambda b, h, i: (b, h, 0, 0))),
              compiler_params=_cp("parallel", "parallel", "arbitrary"))(qa, ka, vb, do,
                                                                         *(cargo.arrays if cargo else ()))
    return _split_cargo(out, cargo)


def _mem_probs(q, k, scale):
    s = lax.dot_general(q, k, _DIMS["nt"], preferred_element_type=F32) * scale
    e = jnp.exp(s - jnp.max(s, axis=-1, keepdims=True))
    return e / jnp.sum(e, axis=-1, keepdims=True)


def _mem_fwd(q, k, v, nb, S, ML, *, name, tq=512):
    T, MW = q.shape
    nh = MW // LANES
    tq = _tile(S, tq, 8)
    nq = S // tq
    scale = LANES ** -0.5

    def body(q_ref, k_ref, v_ref, o_ref):
        for h in range(nh):
            sl = slice(h * LANES, (h + 1) * LANES)
            p = _mem_probs(q_ref[:, sl], k_ref[:, sl], scale)
            o_ref[:, sl] = jnp.dot(p.astype(BF16), v_ref[:, sl], preferred_element_type=F32).astype(BF16)

    return _pc(body, name=name, out_shape=jax.ShapeDtypeStruct((T, MW), BF16), grid=(nb, nq),
               in_specs=[pl.BlockSpec((tq, MW), lambda b, i: (b * nq + i, 0)),
                         pl.BlockSpec((ML, MW), lambda b, i: (b, 0)), pl.BlockSpec((ML, MW), lambda b, i: (b, 0))],
               out_specs=pl.BlockSpec((tq, MW), lambda b, i: (b * nq + i, 0)),
               compiler_params=_cp("parallel", "arbitrary"))(q, k, v)


def _mem_bwd(q, k, v, do, nb, S, ML, *, name, tq=512):
    T, MW = q.shape
    nh = MW // LANES
    tq = _tile(S, tq, 8)
    nq = S // tq
    scale = LANES ** -0.5

    def body(q_ref, k_ref, v_ref, do_ref, dq_ref, dk_ref, dv_ref):
        i = pl.program_id(1)

        @pl.when(i == 0)
        def _():
            dk_ref[...] = jnp.zeros_like(dk_ref)
            dv_ref[...] = jnp.zeros_like(dv_ref)

        for h in range(nh):
            sl = slice(h * LANES, (h + 1) * LANES)
            qv, kv, vv, dov = q_ref[:, sl], k_ref[:, sl], v_ref[:, sl], do_ref[:, sl]
            p = _mem_probs(qv, kv, scale)
            dp = lax.dot_general(dov, vv, _DIMS["nt"], preferred_element_type=F32)
            ds = (p * (dp - jnp.sum(p * dp, axis=-1, keepdims=True)) * scale).astype(BF16)
            dq_ref[:, sl] = jnp.dot(ds, kv, preferred_element_type=F32)
            dk_ref[:, sl] += lax.dot_general(ds, qv, _DIMS["tn"], preferred_element_type=F32)
            dv_ref[:, sl] += lax.dot_general(p.astype(BF16), dov, _DIMS["tn"], preferred_element_type=F32)

    kvspec = pl.BlockSpec((ML, MW), lambda b, i: (b, 0))
    qspec = pl.BlockSpec((tq, MW), lambda b, i: (b * nq + i, 0))
    return _pc(body, name=name,
               out_shape=(jax.ShapeDtypeStruct((T, MW), F32), jax.ShapeDtypeStruct((nb * ML, MW), F32),
                          jax.ShapeDtypeStruct((nb * ML, MW), F32)),
               grid=(nb, nq), in_specs=[qspec, kvspec, kvspec, qspec], out_specs=(qspec, kvspec, kvspec),
               compiler_params=_cp("parallel", "arbitrary"))(q, k, v, do)


def _place():
    x, y, c = lax.axis_index("x"), lax.axis_index("y"), lax.axis_index("c")
    chips = [(1 - x, y), (x, 1 - y), (1 - x, 1 - y)]
    return x, y, c, chips


def _any_specs(n):
    return [pl.BlockSpec(memory_space=pl.ANY)] * n


class _Cargo:
    def __init__(self, arrays, out_shapes, ncopies, copies, alias=None):
        self.arrays, self.out_shapes, self.ncopies, self.copies = list(arrays), list(out_shapes), ncopies, copies
        self.alias = alias or {}


def _load_cargo(body, cargo, kw):
    as_list = lambda v: list(v) if isinstance(v, (tuple, list)) else [v]
    grid = kw.get("grid", ())
    in_specs, out_specs, out_shape = as_list(kw["in_specs"]), as_list(kw["out_specs"]), as_list(kw["out_shape"])
    scratch = list(kw.get("scratch_shapes", ()))
    nin, nout, nci, nco, nscr = len(in_specs), len(out_specs), len(cargo.arrays), len(cargo.out_shapes), len(scratch)

    def loaded(*refs):
        ins, cin = refs[:nin], refs[nin:nin + nci]
        outs, cout = refs[nin + nci:nin + nci + nout], refs[nin + nci + nout:nin + nci + nout + nco]
        scr, (ssem, rsem) = refs[nin + nci + nout + nco:-2], refs[-2:]
        first = last = True
        for a, g in enumerate(grid):
            first = first & (pl.program_id(a) == 0)
            last = last & (pl.program_id(a) == g - 1)

        def start():
            for cp in cargo.copies(cin, cout, ssem, rsem, False):
                cp.start()

        def wait():
            for cp in cargo.copies(cin, cout, ssem, rsem, True):
                cp.wait_send()
                cp.wait_recv()

        if grid:
            pl.when(first)(start)
        else:
            start()
        body(*ins, *outs, *scr)
        if grid:
            pl.when(last)(wait)
        else:
            wait()

    kw = dict(kw, in_specs=in_specs + _any_specs(nci), out_specs=tuple(out_specs + _any_specs(nco)),
              out_shape=tuple(out_shape + cargo.out_shapes),
              scratch_shapes=scratch + [pltpu.SemaphoreType.DMA((cargo.ncopies,))] * 2)
    if cargo.alias:
        kw["input_output_aliases"] = {nin + i: nout + o for i, o in cargo.alias.items()}
    if grid:
        kw["compiler_params"] = _cp(*["arbitrary"] * len(grid))
    return loaded, kw


class _SemView:
    def __init__(self, ref, off):
        self.ref, self.off, self.at = ref, off, self

    def __getitem__(self, k):
        return self.ref.at[self.off + k]


def _join_cargo(a, b):
    na, nao = len(a.arrays), len(a.out_shapes)

    def copies(ins, outs, ssem, rsem, waiting):
        return (a.copies(ins[:na], outs[:nao], ssem, rsem, waiting) +
                b.copies(ins[na:], outs[nao:], _SemView(ssem, a.ncopies), _SemView(rsem, a.ncopies), waiting))

    alias = dict(a.alias)
    alias.update({na + i: nao + o for i, o in b.alias.items()})
    return _Cargo(a.arrays + b.arrays, a.out_shapes + b.out_shapes, a.ncopies + b.ncopies, copies, alias)


def _run_cargo(cargo, *, name):
    res = _pc(lambda: None, name=name, cargo=cargo, in_specs=[], out_specs=[], out_shape=[])(*cargo.arrays)
    return list(res)


def _full_shape(dims, kind):
    return (dims[0], 4 * dims[1]) if kind == "col" else (4 * dims[0], dims[1])


def _ag_copy(shard, full, kind, j_src, chip, c, ssem, rsem, piece):
    R, C = shard.shape
    p, npieces = piece
    rows = R // 2 // npieces
    row0 = pl.multiple_of(c * (R // 2) + p * rows, 16)
    if kind == "col":
        dst = full.at[pl.ds(row0, rows), pl.ds(pl.multiple_of(j_src * C, LANES), C)]
    else:
        dst = full.at[pl.ds(pl.multiple_of(j_src * R + row0, 16), rows), :]
    return pltpu.make_async_remote_copy(shard.at[pl.ds(row0, rows), :], dst, ssem, rsem,
                                        device_id=(*chip, c), device_id_type=MESH)


def _ag_cargo(shards, kinds, piece=(0, 1), lands=None):
    n = len(shards)

    def copies(ins, outs, ssem, rsem, waiting):
        x, y, c, chips = _place()
        return [_ag_copy(ins[w], outs[w], kinds[w], 2 * chip[0] + chip[1] if waiting else 2 * x + y, chip, c,
                         ssem.at[3 * w + q], rsem.at[3 * w + q], piece)
                for w in range(n) for q, chip in enumerate(chips)]

    shapes = [jax.ShapeDtypeStruct(_full_shape(s.shape, k), BF16) for s, k in zip(shards, kinds)]
    if lands is None:
        return _Cargo(shards, shapes, 3 * n, copies)
    return _Cargo(list(shards) + list(lands), shapes, 3 * n, copies, alias={n + w: w for w in range(n)})


def _scatter_cargo(parts, kinds, piece=(0, 1), lands=None):
    n = len(parts)
    pdims = [((p.shape[0], p.shape[1] // 4) if k == "col" else (p.shape[0] // 4, p.shape[1]))
             for p, k in zip(parts, kinds)]

    def copy(src, dst, kind, R, C, q, j, chip, c, ssem, rsem):
        rows = R // piece[1]
        r0 = piece[0] * rows
        if kind == "col":
            win = src.at[pl.ds(r0, rows), pl.ds(pl.multiple_of(j * C, LANES), C)]
        else:
            win = src.at[pl.ds(pl.multiple_of(j * R + r0, 16), rows), :]
        return pltpu.make_async_remote_copy(win, dst.at[q, pl.ds(r0, rows), :], ssem, rsem,
                                            device_id=(*chip, c), device_id_type=MESH)

    def copies(ins, outs, ssem, rsem, waiting):
        x, y, c, chips = _place()
        return [copy(ins[w], outs[w], kinds[w], *pdims[w], q, 2 * chip[0] + chip[1], chip, c,
                     ssem.at[3 * w + q], rsem.at[3 * w + q])
                for w in range(n) for q, chip in enumerate(chips)]

    shapes = [jax.ShapeDtypeStruct((3,) + d, BF16) for d in pdims]
    if lands is None:
        return _Cargo(parts, shapes, 3 * n, copies)
    return _Cargo(list(parts) + list(lands), shapes, 3 * n, copies, alias={n + w: w for w in range(n)})


def _ag_fill(full, shard, kind, *, name):
    R, C = shard.shape
    RH = R // 2
    tm = _tile(RH, max(16, (3 * 2 ** 19 // C) // 16 * 16), 16)
    nbh = RH // tm
    n, nown = 3 * nbh, 2 * nbh
    xyc = jnp.stack([lax.axis_index("x"), lax.axis_index("y"), lax.axis_index("c")]).astype(jnp.int32)

    def chip_of(q, x, y):
        keep_x, keep_y = q % 2, (2 - q) // 2
        return 2 * ((1 - x) + keep_x * (2 * x - 1)) + (1 - y) + keep_y * (2 * y - 1)

    def window(ref, j, row0):
        if kind == "col":
            return ref.at[pl.ds(pl.multiple_of(row0, 16), tm), pl.ds(pl.multiple_of(j * C, LANES), C)]
        return ref.at[pl.ds(pl.multiple_of(j * R + row0, 16), tm), :]

    def body(p_ref, send_ref, own_ref, out_ref, land, ssem, rsem, wsem):
        i = pl.program_id(0)
        x, y = lax.axis_index("x"), lax.axis_index("y")
        c, peer = _pair_peer()
        own_cp = pltpu.make_async_copy(own_ref, window(out_ref, 2 * x + y, jnp.minimum(i, nown - 1) * tm), wsem.at[1])

        @pl.when(i < nown)
        def _():
            own_cp.start()

        def consume(land, ps):
            b = i - 1
            cp = pltpu.make_async_copy(land.at[ps], window(out_ref, chip_of(b // nbh, x, y),
                                                           (1 - c) * RH + (b % nbh) * tm), wsem.at[0])
            cp.start()
            cp.wait()

        _exchange_step(i, n, send_ref, land, ssem, rsem, None, peer, consume)

        @pl.when(i < nown)
        def _():
            own_cp.wait()

    now = lambda i: jnp.minimum(i, n - 1)

    def send_map(i, p):
        q, b = now(i) // nbh, now(i) % nbh
        j = chip_of(q, p[0], p[1])
        return (p[2] * nbh + b, j) if kind == "col" else ((j * 2 + p[2]) * nbh + b, 0)

    grid_spec = pltpu.PrefetchScalarGridSpec(
        num_scalar_prefetch=1, grid=(n + 1,),
        in_specs=[pl.BlockSpec((tm, C), send_map),
                  pl.BlockSpec((tm, C), lambda i, p: (jnp.minimum(i, nown - 1), 0))],
        out_specs=pl.BlockSpec(memory_space=pl.ANY),
        scratch_shapes=[pltpu.VMEM((n, tm, C), BF16), pltpu.SemaphoreType.DMA((n,)), pltpu.SemaphoreType.DMA((n,)),
                        pltpu.SemaphoreType.DMA((2,))])
    return _pc(body, name=name, out_shape=jax.ShapeDtypeStruct(full.shape, BF16), grid_spec=grid_spec,
               input_output_aliases={1: 0}, compiler_params=_cp("arbitrary"))(xyc, full, shard)


def _pair_peer():
    x, y, c = lax.axis_index("x"), lax.axis_index("y"), lax.axis_index("c")
    return c, (x, y, 1 - c)


def _exchange_step(i, n, src_ref, land, ssem, rsem, credit, peer, consume):
    def rdma(slot):
        return pltpu.make_async_remote_copy(src_ref, land.at[slot], ssem.at[slot], rsem.at[slot],
                                            device_id=peer, device_id_type=MESH)

    reuse = credit is not None
    slot = i % 2 if reuse else jnp.minimum(i, n - 1)

    @pl.when(i < n)
    def _():
        if reuse:
            @pl.when(i >= 2)
            def _():
                pl.semaphore_wait(credit, 1)

        rdma(slot).start()

    @pl.when(i >= 1)
    def _():
        ps = (i - 1) % 2 if reuse else i - 1
        rdma(ps).wait_recv()
        consume(land, ps)

        if reuse:
            @pl.when(i - 1 < n - 2)
            def _():
                pl.semaphore_signal(credit, 1, device_id=peer, device_id_type=MESH)

    @pl.when(i < n)
    def _():
        rdma(slot).wait_send()


def _xchg_scratch(tm, C, dtype):
    return [pltpu.VMEM((2, tm, C), dtype), pltpu.SemaphoreType.DMA((2,)), pltpu.SemaphoreType.DMA((2,)),
            pltpu.SemaphoreType.REGULAR]


def _rs_pair_sum(g, kind, *, name):
    K, N = g.shape
    NS, RH = (1, K // 2) if kind == "col" else (4, K // 8)
    tm = _tile(RH, max(16, (3 * 2 ** 19 // N) // 16 * 16), 16)
    if NS * (RH // tm) < 2:
        tm = RH // 2
    nb = RH // tm
    n = NS * nb
    cvec = lax.axis_index("c").reshape(1).astype(jnp.int32)

    def body(c_ref, send_ref, own_ref, out_ref, land, ssem, rsem, credit):
        _, peer = _pair_peer()

        def consume(land, ps):
            out_ref[...] = (own_ref[...].astype(F32) + land[ps].astype(F32)).astype(BF16)

        _exchange_step(pl.program_id(0), n, send_ref, land, ssem, rsem, credit, peer, consume)

    now = lambda i: jnp.minimum(i, n - 1)
    lag = lambda i: jnp.maximum(i - 1, 0)
    grid_spec = pltpu.PrefetchScalarGridSpec(
        num_scalar_prefetch=1, grid=(n + 1,),
        in_specs=[pl.BlockSpec((tm, N), lambda i, c: ((now(i) // nb * 2 + 1 - c[0]) * nb + now(i) % nb, 0)),
                  pl.BlockSpec((tm, N), lambda i, c: ((lag(i) // nb * 2 + c[0]) * nb + lag(i) % nb, 0))],
        out_specs=pl.BlockSpec((tm, N), lambda i, c: (lag(i), 0)),
        scratch_shapes=_xchg_scratch(tm, N, BF16))
    return _pc(body, name=name, out_shape=jax.ShapeDtypeStruct((NS * RH, N), BF16), grid_spec=grid_spec,
               compiler_params=_cp("arbitrary"))(cvec, g, g)


def _rs_sum_join(part, got, kind, *, name):
    _, hr, hc = got.shape
    tm = _tile(hr, max(16, (3 * 2 ** 18 // hc) // 16 * 16), 16)
    if hr // tm < 2:
        tm = hr // 2
    nb = hr // tm
    jvec = (2 * lax.axis_index("x") + lax.axis_index("y")).reshape(1).astype(jnp.int32)

    def body(j_ref, p_ref, a_ref, b_ref, c_ref, mine_ref, theirs_ref, sbuf, land, ssem, rsem, credit):
        i = pl.program_id(0)
        _, peer = _pair_peer()

        @pl.when(i < nb)
        def _():
            v = ((p_ref[...].astype(F32) + a_ref[...].astype(F32)) + b_ref[...].astype(F32)) + c_ref[...].astype(F32)
            mine_ref[...] = v
            sbuf[...] = v

        def consume(land, ps):
            theirs_ref[...] = land[ps]

        _exchange_step(i, nb, sbuf, land, ssem, rsem, credit, peer, consume)

    now = lambda i: jnp.minimum(i, nb - 1)
    lag = lambda i: jnp.maximum(i - 1, 0)
    own_spec = (pl.BlockSpec((tm, hc), lambda i, j: (now(i), j[0])) if kind == "col"
                else pl.BlockSpec((tm, hc), lambda i, j: (j[0] * nb + now(i), 0)))
    got_spec = lambda q: pl.BlockSpec((None, tm, hc), functools.partial(lambda i, j, q: (q, now(i), 0), q=q))
    grid_spec = pltpu.PrefetchScalarGridSpec(
        num_scalar_prefetch=1, grid=(nb + 1,),
        in_specs=[own_spec, got_spec(0), got_spec(1), got_spec(2)],
        out_specs=(pl.BlockSpec((tm, hc), lambda i, j: (now(i), 0)), pl.BlockSpec((tm, hc), lambda i, j: (lag(i), 0))),
        scratch_shapes=[pltpu.VMEM((tm, hc), F32)] + _xchg_scratch(tm, hc, F32))
    shp = jax.ShapeDtypeStruct((hr, hc), F32)
    return _pc(body, name=name, out_shape=(shp, shp), grid_spec=grid_spec,
               compiler_params=_cp("arbitrary"))(jvec, part, got, got, got)


def _adamw_join(w, mine, theirs, m, v, *, name):
    R2, wc = w.shape
    hr, hc = mine.shape
    assert hr < R2 <= 2 * hr and hc >= wc
    pref, rows8 = max(8, 2 ** 18 // hc), _round_up(R2, 8)
    tm = max(t for t in range(8, hr + 1, 8) if hr % t == 0 and rows8 % t == 0 and (t <= pref or t == 8))
    nbh = hr // tm
    cvec = lax.axis_index("c").reshape(1).astype(jnp.int32)

    def body(c_ref, w_ref, a_ref, b_ref, m_ref, v_ref, g_ref, d_ref, m2_ref, v2_ref):
        gv = jnp.where(pl.program_id(0) // nbh == c_ref[0], a_ref[...], b_ref[...])[:, :wc]
        m2 = ADAM_B1 * m_ref[...] + (1.0 - ADAM_B1) * gv
        v2 = ADAM_B2 * v_ref[...] + (1.0 - ADAM_B2) * (gv * gv)
        m_hat = m2 / (1.0 - ADAM_B1 ** ADAM_STEP)
        v_hat = v2 / (1.0 - ADAM_B2 ** ADAM_STEP)
        g_ref[...] = gv
        d_ref[...] = -ADAM_LR * (m_hat / (jnp.sqrt(v_hat) + ADAM_EPS) + ADAM_WD * w_ref[...])
        m2_ref[...] = m2
        v2_ref[...] = v2

    wspec = pl.BlockSpec((tm, wc), lambda i, c: (i, 0))
    mine_spec = pl.BlockSpec((tm, hc), lambda i, c: (jnp.where(i // nbh == c[0], i % nbh, 0), 0))
    theirs_spec = pl.BlockSpec((tm, hc), lambda i, c: (jnp.where(i // nbh == c[0], 0, i % nbh), 0))
    grid_spec = pltpu.PrefetchScalarGridSpec(
        num_scalar_prefetch=1, grid=(pl.cdiv(R2, tm),),
        in_specs=[wspec, mine_spec, theirs_spec, wspec, wspec], out_specs=(wspec,) * 4)
    return _pc(body, name=name, out_shape=(jax.ShapeDtypeStruct((R2, wc), F32),) * 4, grid_spec=grid_spec,
               compiler_params=_cp("parallel"))(cvec, w, mine, theirs, m, v)


def _all_reduce_small(pack):
    R = pack.shape[0]

    def body(p_ref, o_ref, buf, send, recv):
        x, y, c, _ = _place()
        me = 4 * x + 2 * y + c
        buf[0] = p_ref[...]
        cps = []
        for r in range(1, 8):
            peer = (x if r & 4 == 0 else 1 - x, y if r & 2 == 0 else 1 - y, c if r & 1 == 0 else 1 - c)
            cp = pltpu.make_async_remote_copy(p_ref, buf.at[r], send.at[r - 1], recv.at[r - 1],
                                              device_id=peer, device_id_type=MESH)
            cp.start()
            cps.append(cp)
        for cp in cps:
            cp.wait()
        acc = buf[jnp.bitwise_xor(me, 0)]
        for k in range(1, 8):
            acc = acc + buf[jnp.bitwise_xor(me, k)]
        o_ref[...] = acc

    return _pc(body, name="allreduce_small", out_shape=jax.ShapeDtypeStruct((R, LANES), F32),
               in_specs=[pl.BlockSpec(memory_space=pltpu.VMEM)], out_specs=pl.BlockSpec(memory_space=pltpu.VMEM),
               scratch_shapes=[pltpu.VMEM((8, R, LANES), F32), pltpu.SemaphoreType.DMA((7,)),
                               pltpu.SemaphoreType.DMA((7,))])(pack)


def _pack_small(arrs):
    parts, sizes = [], []
    for a in arrs:
        f = a.reshape(-1).astype(F32)
        n = _round_up(f.shape[0], 8 * LANES)
        parts.append(jnp.pad(f, (0, n - f.shape[0])).reshape(n // LANES, LANES))
        sizes.append((a.shape, f.shape[0], n // LANES))
    return jnp.concatenate(parts, axis=0), sizes


def _unpack_small(pack, sizes):
    out, r0 = [], 0
    for shape, cnt, rows in sizes:
        out.append(pack[r0:r0 + rows].reshape(-1)[:cnt].reshape(shape))
        r0 += rows
    return out


def _ffn_fwd(xin, norm, weight, tag, gather_on_gateup, gather_on_down):
    hb = _rms_fwd(xin, norm, name=tag + "_norm")
    cargo, landed = gather_on_gateup()
    res = _gateup(hb, weight("w_gate_up"), name=tag + "_gateup", tm=1024, cargo=cargo)
    if cargo is not None:
        res, lands = res
        landed(lands)
    gate, up, act = res
    cargo, landed = gather_on_down()
    xout = _mm(act, weight("w_down"), res=xin, scale=0.5, name=tag + "_down", tm=1024, tn=512, tk=8192, cargo=cargo)
    if cargo is not None:
        xout, lands = xout
        landed(lands)
    return xout, (hb, gate, up, act)


def _ffn_bwd(dy, dyb, xin, norm, wgu, wd, saved, tag, scatter, gu_piece=(0, 1)):
    hb, gate, up, act = saved
    d_wd = _mm(act, dyb, mode="tn", scale=0.5, out_dtype=BF16, name=tag + "_dwd", tm=512, tn=1024, tk=4096)
    cargo = scatter(tag + "_w_down", d_wd)
    rest, rest_done = scatter.leftover()
    dgu = _dact_swiglu(dyb, wd, gate, up, name=tag + "_dact", cargo=rest)
    if rest is not None:
        dgu, lands = dgu
        rest_done(lands)
    d_wgu, lands = _mm(hb, dgu, mode="tn", halves="b", out_dtype=BF16, name=tag + "_dwgu", tm=1024, tn=512, tk=4096,
                       cargo=cargo)
    scatter.done(lands)
    cargo = scatter(tag + "_w_gate_up", d_wgu, piece=gu_piece)
    dh, lands = _mm(dgu, wgu, mode="nt", halves="a", out_dtype=BF16, name=tag + "_dh", tm=1024, tn=512, tk=2816,
                    cargo=cargo)
    scatter.done(lands)
    dx, dxb, d_norm = _rms_bwd(xin, dh, norm, dy, name=tag + "_dnorm")
    return dx, dxb, d_norm


def kernel(x, mem, ffn1_norm, ffn1_w_gate_up, ffn1_w_down, mix_norm, mem_norm, w_in, b_forget, pool_w, pool_scale, w_pool_up, fox_q_norm, fox_k_norm, w_fox_o, w_mem_kv, mem_q_norm, mem_k_norm, w_mem_o, w_out, ffn2_norm, ffn2_w_gate_up, ffn2_w_down, loss_target, m_ffn1_norm, m_ffn1_w_gate_up, m_ffn1_w_down, m_mix_norm, m_mem_norm, m_w_in, m_b_forget, m_pool_w, m_pool_scale, m_w_pool_up, m_fox_q_norm, m_fox_k_norm, m_w_fox_o, m_w_mem_kv, m_mem_q_norm, m_mem_k_norm, m_w_mem_o, m_w_out, m_ffn2_norm, m_ffn2_w_gate_up, m_ffn2_w_down, v_ffn1_norm, v_ffn1_w_gate_up, v_ffn1_w_down, v_mix_norm, v_mem_norm, v_w_in, v_b_forget, v_pool_w, v_pool_scale, v_w_pool_up, v_fox_q_norm, v_fox_k_norm, v_w_fox_o, v_w_mem_kv, v_mem_q_norm, v_mem_k_norm, v_w_mem_o, v_w_out, v_ffn2_norm, v_ffn2_w_gate_up, v_ffn2_w_down):
    P = dict(locals())
    big = ["ffn1_w_gate_up", "ffn1_w_down", "w_in", "w_pool_up", "w_fox_o", "w_mem_kv", "w_mem_o", "w_out",
           "ffn2_w_gate_up", "ffn2_w_down"]
    kinds = ["col", "row", "col", "col", "col", "row", "col", "row", "col", "row"]
    small = ["ffn1_norm", "mix_norm", "mem_norm", "b_forget", "pool_w", "pool_scale", "fox_q_norm", "fox_k_norm",
             "mem_q_norm", "mem_k_norm", "ffn2_norm"]
    order = ["ffn1_norm", "ffn1_w_gate_up", "ffn1_w_down", "mix_norm", "mem_norm", "w_in", "b_forget", "pool_w",
             "pool_scale", "w_pool_up", "fox_q_norm", "fox_k_norm", "w_fox_o", "w_mem_kv", "mem_q_norm", "mem_k_norm",
             "w_mem_o", "w_out", "ffn2_norm", "ffn2_w_gate_up", "ffn2_w_down"]

    nb, S, D = x.shape
    T = nb * S
    ML = mem.shape[1]
    NF, HD = b_forget.shape[-1], fox_q_norm.shape[-1]
    FW = NF * HD
    MW, MHD = w_mem_o.shape[1], mem_q_norm.shape[-1]
    PG, PD = pool_w.shape[1], pool_w.shape[2]
    PW = PG * PD
    n_in = w_in.shape[-1]
    n_in_pad = _round_up(n_in, LANES)
    in_w = 4 * n_in
    assert HD * 2 == LANES and MHD == LANES and PD == LANES and in_w == PW + 3 * FW + NF + MW + 3 * D

    shards = []
    for nme in big:
        wl = P[nme][0].astype(BF16)
        if nme == "w_in":
            wl = jnp.pad(wl, ((0, 0), (0, n_in_pad - n_in)))
        shards.append(wl)
    shard, kind_of, full = dict(zip(big, shards)), dict(zip(big, kinds)), {}

    def landed(names):
        def fill(lands):
            for nme, l in zip(names, lands):
                full[nme] = _ag_fill(l, shard[nme], kind_of[nme], name="ag_fill_" + nme)
        return fill

    def gather(names, piece=(0, 1), lands=None):
        return _ag_cargo([shard[nme] for nme in names], [kind_of[nme] for nme in names], piece, lands)

    landed(["ffn1_w_gate_up"])(_run_cargo(gather(["ffn1_w_gate_up"]), name="ag_first"))
    begun = {}

    def on_gateup1():
        def fill(lands):
            landed(["ffn1_w_down"])(lands[:1])
            begun["w_in"] = lands[1]
        return _join_cargo(gather(["ffn1_w_down"]), gather(["w_in"], (0, 2))), fill

    x0 = x.reshape(T, D)
    x1, sv1 = _ffn_fwd(x0, ffn1_norm, lambda nme: full["ffn1_" + nme], "ffn1", on_gateup1,
                       lambda: (gather(["w_in"], (1, 2), [begun["w_in"]]), landed(["w_in"])))
    win_o = jnp.concatenate([full["w_in"][:, j * n_in_pad:j * n_in_pad + n_in] for j in range(4)], axis=1)
    o_q, o_k, o_v, o_f = PW, PW + FW, PW + 2 * FW, PW + 3 * FW
    o_qm, o_gate = o_f + NF, o_f + NF + MW
    p_q, p_k, p_v, p_pool, p_qm, p_f = 3 * D, 3 * D + FW, 3 * D + 2 * FW, 3 * D + 3 * FW, 3 * D + 3 * FW + PW, 3 * D + 3 * FW + PW + MW
    inp = _round_up(p_f + LANES, 512)
    win_p = jnp.concatenate([win_o[:, o_gate:], win_o[:, o_q:o_f], win_o[:, :PW], win_o[:, o_qm:o_gate],
                             win_o[:, o_f:o_qm], jnp.zeros((D, inp - p_f - NF), BF16)], axis=1)

    hb2 = _rms_fwd(x1, mix_norm, name="mix_norm")
    later = ["ffn2_w_down", "w_out", "w_fox_o", "w_mem_kv", "w_pool_up", "w_mem_o"]
    z, lands = _mm(hb2, win_p, name="in_proj", tm=1024, tn=1536, cargo=gather(later))
    landed(later)(lands)

    vpool = _pool_fwd(z, p_pool // LANES, pool_w[0], pool_scale, nb, S, name="pool_fwd")
    y_pool = _mm(vpool, full["w_pool_up"], name="pool_up", tm=1024, tn=512)

    b128 = jnp.pad(b_forget, ((0, 0), (0, LANES - NF)))
    c3 = _fox_c(z, p_f // LANES, b128, nb, S, NF, name="fox_c")
    qa, ka, vb = _fox_qk(z, p_q // LANES, p_k // LANES, p_v // LANES, c3, fox_q_norm, fox_k_norm, HD, NF,
                         name="fox_qk")
    (o_fox, qb), lands = _fox_fwd(qa, ka, vb, nb, S, HD, name="fox_fwd", cargo=gather(["ffn2_w_gate_up"]))
    landed(["ffn2_w_gate_up"])(lands)
    y_fox = _mm(o_fox, full["w_fox_o"], name="fox_out", tm=1024, tn=512)

    mem2 = mem.reshape(nb * ML, D)
    memn = _rms_fwd(mem2, mem_norm, name="mem_norm")
    kv = _mm(memn, full["w_mem_kv"], name="mem_kv")
    kmn = _headnorm_fwd(kv, 0, MW, mem_k_norm, MHD, name="mem_knorm")
    vmb = kv[:, MW:].astype(BF16)
    qmn = _headnorm_fwd(z, p_qm // LANES, MW, mem_q_norm, MHD, name="mem_qnorm")
    o_mem = _mem_fwd(qmn, kmn, vmb, nb, S, ML, name="mem_fwd")
    y_mem = _mm(o_mem, full["w_mem_o"], name="mem_out", tm=1024, tn=512)

    merged = _merge_fwd(z, y_pool, y_fox, y_mem, name="merge_fwd")
    x2 = _mm(merged, full["w_out"], res=x1, name="out_proj", tm=1024, tn=512)
    x3, sv2 = _ffn_fwd(x2, ffn2_norm, lambda nme: full["ffn2_" + nme], "ffn2", lambda: (None, None),
                       lambda: (None, None))
    dy3, dy3b, loss_part = _loss(x3, loss_target.reshape(T, D), name="loss")

    g, parts, got = {}, {}, {}
    rs_kind = dict(kind_of, w_in="row")

    leftovers = []

    def scatter(*named_grads, piece=(0, 1)):
        names = named_grads[0::2]
        for nme, grad in zip(names, named_grads[1::2]):
            parts[nme] = _rs_pair_sum(grad, rs_kind[nme], name="rs_pair_" + nme)
        scatter.names = names
        if piece[1] == 2:
            leftovers.append(names)
        return _scatter_cargo([parts[nme] for nme in names], [rs_kind[nme] for nme in names], piece)

    def scattered(lands):
        got.update(zip(scatter.names, lands))

    def leftover():
        if not leftovers:
            return None, None
        names = leftovers.pop(0)
        cargo = _scatter_cargo([parts[nme] for nme in names], [rs_kind[nme] for nme in names], (1, 2),
                               [got[nme] for nme in names])
        return cargo, lambda lands: got.update(zip(names, lands))

    scatter.done, scatter.leftover = scattered, leftover
    dx2, dx2b, g["ffn2_norm"] = _ffn_bwd(dy3, dy3b, x2, ffn2_norm, full["ffn2_w_gate_up"], full["ffn2_w_down"], sv2,
                                         "ffn2", scatter, gu_piece=(0, 2))

    dmerged = _mm(dx2b, full["w_out"], mode="nt", name="d_merged", tm=1024, tn=512)
    g["w_out"] = _mm(merged, dx2b, mode="tn", out_dtype=BF16, name="d_w_out", tk=4096)
    dyp, dyf, dym, dgl = _merge_bwd(dmerged, z, y_pool, y_fox, y_mem, name="merge_bwd")

    g["w_pool_up"] = _mm(vpool, dyp, mode="tn", out_dtype=BF16, name="d_w_pool_up", tk=4096)
    dvp = _mm(dyp, full["w_pool_up"], mode="nt", name="d_vpool", tm=1024)
    du_pool, g["pool_w"], g["pool_scale"] = _pool_bwd(z, p_pool // LANES, dvp, pool_w[0], pool_scale, nb, S,
                                                      name="pool_bwd")

    g["w_fox_o"] = _mm(o_fox, dyf, mode="tn", out_dtype=BF16, name="d_w_fox_o", tk=4096)
    do_fox = _mm(dyf, full["w_fox_o"], mode="nt", out_dtype=BF16, name="d_o_fox", tm=1024)
    rest, rest_done = leftover()
    (dqa, dka, dvf, dcs), lands = _fox_bwd(qb, ka, vb, do_fox, nb, S, HD, name="fox_bwd", cargo=rest)
    rest_done(lands)
    dq, dgq = _headnorm_bwd(z, p_q // LANES, FW, dqa, fox_q_norm, HD, HD ** -0.5, True, BF16, name="fox_dq")
    dk, dgk = _headnorm_bwd(z, p_k // LANES, FW, dka, fox_k_norm, HD, 1.0, True, BF16, name="fox_dk")
    g["fox_q_norm"], g["fox_k_norm"] = _fold_gain(dgq, HD), _fold_gain(dgk, HD)
    dc = -dcs[:, :, :LANES // HD, :].reshape(nb, NF, S).transpose(0, 2, 1).reshape(T, NF)
    df, db = _fox_c_bwd(jnp.pad(dc, ((0, 0), (0, LANES - NF))), z, p_f // LANES, b128, nb, S, name="fox_c_bwd")
    g["b_forget"] = db[:, :NF]

    g["w_mem_o"] = _mm(o_mem, dym, mode="tn", out_dtype=BF16, name="d_w_mem_o", tk=4096)
    do_mem = _mm(dym, full["w_mem_o"], mode="nt", out_dtype=BF16, name="d_o_mem", tm=1024)
    dqmn, dkmn, dvm = _mem_bwd(qmn, kmn, vmb, do_mem, nb, S, ML, name="mem_bwd")
    dqm, g["mem_q_norm"] = _headnorm_bwd(z, p_qm // LANES, MW, dqmn, mem_q_norm, MHD, 1.0, False, BF16, name="mem_dq")
    dkm, g["mem_k_norm"] = _headnorm_bwd(kv, 0, MW, dkmn, mem_k_norm, MHD, 1.0, False, BF16, name="mem_dk")
    dkv = jnp.concatenate([dkm, dvm.astype(BF16)], axis=1)
    g["w_mem_kv"] = _mm(memn, dkv, mode="tn", out_dtype=BF16, name="d_w_mem_kv")
    dmemn = _mm(dkv, full["w_mem_kv"], mode="nt", name="d_memn")
    _, _, g["mem_norm"] = _rms_bwd(mem2, dmemn, mem_norm, jnp.zeros_like(mem2), name="mem_dnorm")

    dz = jnp.concatenate([dgl, dq, dk, dvf.astype(BF16), du_pool, dqm, df.astype(BF16),
                          jnp.zeros((T, inp - p_f - LANES), BF16)], axis=1)
    cargo = scatter(*sum([[nme, g[nme]] for nme in ("w_out", "w_pool_up", "w_fox_o", "w_mem_o", "w_mem_kv")], []))
    dwin_p, lands = _mm(dz, hb2, mode="tn", out_dtype=BF16, name="d_w_in", tm=512, tn=1024, tk=4096, cargo=cargo)
    scattered(lands)
    dwin_o = jnp.concatenate([dwin_p[p_pool:p_qm], dwin_p[p_q:p_pool], dwin_p[p_f:p_f + NF], dwin_p[p_qm:p_f],
                              dwin_p[:p_q]], axis=0)
    cargo = scatter("w_in", jnp.concatenate(
        [jnp.pad(dwin_o[j * n_in:(j + 1) * n_in], ((0, n_in_pad - n_in), (0, 0))) for j in range(4)], axis=0),
        piece=(0, 2))
    dh2, lands = _mm(dz, win_p, mode="nt", out_dtype=BF16, name="d_h2", tm=1024, tn=512, tk=3584, cargo=cargo)
    scattered(lands)
    dx1, dx1b, g["mix_norm"] = _rms_bwd(x1, dh2, mix_norm, dx2, name="mix_dnorm")

    dx0, _, g["ffn1_norm"] = _ffn_bwd(dx1, dx1b, x0, ffn1_norm, full["ffn1_w_gate_up"], full["ffn1_w_down"], sv1,
                                      "ffn1", scatter)

    halves = {nme: _rs_sum_join(parts[nme], got[nme], rs_kind[nme], name="rs_join_" + nme) for nme in big}
    gfull = {}

    gpack, sizes = _pack_small([g[nme].reshape(P[nme].shape) for nme in small] + [loss_part[:, :1]])
    gsum = _unpack_small(_all_reduce_small(gpack), sizes)
    loss = gsum[-1].reshape(())
    for nme, a in zip(small, gsum[:-1]):
        gfull[nme] = a

    delta, new_m, new_v = {}, {}, {}
    for nme in big:
        shp = P[nme].shape
        two = (lambda a: a.reshape(shp[-2:]).T) if nme == "w_in" else (lambda a: a.reshape(shp[-2:]))
        back = (lambda a: a.T.reshape(shp)) if nme == "w_in" else (lambda a: a.reshape(shp))
        gfull[nme], delta[nme], new_m[nme], new_v[nme] = [
            back(a) for a in _adamw_join(two(P[nme]), *halves[nme], two(P["m_" + nme]), two(P["v_" + nme]),
                                         name="adamw_" + nme)]
    wpack, _ = _pack_small([P[nme] for nme in small])
    mpack, _ = _pack_small([P["m_" + nme] for nme in small])
    vpack, _ = _pack_small([P["v_" + nme] for nme in small])
    gpack2, ssz = _pack_small([gfull[nme] for nme in small])
    for dct, pk in zip((delta, new_m, new_v), _adamw(wpack, gpack2, mpack, vpack, name="adamw_small")):
        for nme, a in zip(small, _unpack_small(pk, ssz)):
            dct[nme] = a

    grad_x = dx0.reshape(nb, S, D)
    return (loss, grad_x, *[gfull[nme] for nme in order], *[delta[nme] for nme in order],
            *[new_m[nme] for nme in order], *[new_v[nme] for nme in order])
```

```python
import functools

import jax
import jax.numpy as jnp
from jax import lax
from jax.experimental import pallas as pl
from jax.experimental.pallas import tpu as pltpu

F32 = jnp.float32
BF16 = jnp.bfloat16
MESH = pl.DeviceIdType.MESH
EPS = 1e-6
POOL_WINDOWS = (2, 4, 8, 16)
ADAM_LR, ADAM_B1, ADAM_B2, ADAM_EPS, ADAM_WD, ADAM_STEP = 0.001, 0.9, 0.999, 1e-08, 0.01, 10
LANES = 128
VMEM_LIMIT = 56 * 1024 * 1024
NEG = -1e30


def _pc(body, *, name, cargo=None, **kw):
    if cargo is not None:
        body, kw = _load_cargo(body, cargo, kw)
    return pl.pallas_call(body, name=name, **kw)


def _cp(*sem):
    return pltpu.CompilerParams(dimension_semantics=sem, vmem_limit_bytes=VMEM_LIMIT)


def _tile(n, pref, mult=LANES):
    if n <= pref:
        return n
    t = (pref // mult) * mult
    while t >= mult:
        if n % t == 0:
            return t
        t -= mult
    return n


def _round_up(n, m):
    return (n + m - 1) // m * m


_DIMS = {"nn": (((1,), (0,)), ((), ())), "nt": (((1,), (1,)), ((), ())), "tn": (((0,), (0,)), ((), ()))}


def _split_cargo(res, cargo):
    if cargo is None:
        return res
    nco = len(cargo.out_shapes)
    own = res[:len(res) - nco]
    return (own[0] if len(own) == 1 else own), list(res[len(res) - nco:])


def _mm(a, b, *, name, mode="nn", out_dtype=F32, scale=1.0, res=None, tm=512, tn=512, tk=2048, cargo=None,
        halves=None):
    if halves == "a":
        assert mode == "nt"
        (_, M, Kh), (N, K2) = a.shape, b.shape
        K = 2 * Kh
    elif halves == "b":
        assert mode == "tn"
        (K, M), (_, K2, Nh) = a.shape, b.shape
        N = 2 * Nh
    elif mode == "nn":
        (M, K), (K2, N) = a.shape, b.shape
    elif mode == "nt":
        (M, K), (N, K2) = a.shape, b.shape
    else:
        (K, M), (K2, N) = a.shape, b.shape
    assert K == K2, (name, a.shape, b.shape)
    tm = _tile(M, tm, 8 if M % LANES else LANES)
    tn = _tile(N // 2 if halves == "b" else N, tn)
    tk = _tile(K // 2 if halves == "a" else K, tk)
    nk = K // tk
    dims = _DIMS[mode]
    if halves == "a":
        a_spec = pl.BlockSpec((None, tm, tk), lambda i, j, k: (k // (nk // 2), i, k % (nk // 2)))
    elif mode == "tn":
        a_spec = pl.BlockSpec((tk, tm), lambda i, j, k: (k, i))
    else:
        a_spec = pl.BlockSpec((tm, tk), lambda i, j, k: (i, k))
    if halves == "b":
        njh = N // 2 // tn
        b_spec = pl.BlockSpec((None, tk, tn), lambda i, j, k: (j // njh, k, j % njh))
    elif mode == "nt":
        b_spec = pl.BlockSpec((tn, tk), lambda i, j, k: (j, k))
    else:
        b_spec = pl.BlockSpec((tk, tn), lambda i, j, k: (k, j))
    o_spec = pl.BlockSpec((tm, tn), lambda i, j, k: (i, j))
    has_res = res is not None

    def body(*refs):
        a_ref, b_ref = refs[:2]
        r_ref = refs[2] if has_res else None
        o_ref = refs[3] if has_res else refs[2]

        def finish(v):
            v = v * scale
            if has_res:
                v = r_ref[...] + v
            o_ref[...] = v.astype(o_ref.dtype)

        prod = lax.dot_general(a_ref[...], b_ref[...], dims, preferred_element_type=F32)
        if nk == 1:
            finish(prod)
            return
        acc, k = refs[-1], pl.program_id(2)

        @pl.when(k == 0)
        def _():
            acc[...] = prod

        @pl.when(k > 0)
        def _():
            acc[...] += prod

        @pl.when(k == nk - 1)
        def _():
            finish(acc[...])

    out = _pc(body, name=name, cargo=cargo, out_shape=jax.ShapeDtypeStruct((M, N), out_dtype),
              grid=(M // tm, N // tn, nk),
              in_specs=[a_spec, b_spec] + ([o_spec] if has_res else []), out_specs=o_spec,
              scratch_shapes=[] if nk == 1 else [pltpu.VMEM((tm, tn), F32)],
              compiler_params=_cp("parallel", "parallel", "arbitrary"))(
                  *((a, b, res) if has_res else (a, b)), *(cargo.arrays if cargo else ()))
    return _split_cargo(out, cargo)


def _dact_swiglu(dyb, wd, gate, up, *, name, tm=1024, tn=512, cargo=None):
    M, K = dyb.shape
    Fh = wd.shape[0]
    tm, tn = _tile(M, tm), _tile(Fh, tn)

    rows = _tile(tm, 256, 16)

    def body(dy_ref, w_ref, g_ref, u_ref, o_ref):
        w = w_ref[...]
        for r in range(0, tm, rows):
            sl = slice(r, r + rows)
            d = lax.dot_general(dy_ref[sl, :], w, _DIMS["nt"], preferred_element_type=F32) * 0.5
            g, u = g_ref[sl, :].astype(F32), u_ref[sl, :].astype(F32)
            s = jax.nn.sigmoid(g)
            o_ref[0, sl, :] = (d * u * (s * (1.0 + g * (1.0 - s)))).astype(BF16)
            o_ref[1, sl, :] = (d * (g * s)).astype(BF16)

    t_spec = pl.BlockSpec((tm, tn), lambda i, j: (i, j))
    out = _pc(body, name=name, cargo=cargo, out_shape=jax.ShapeDtypeStruct((2, M, Fh), BF16),
              grid=(M // tm, Fh // tn),
              in_specs=[pl.BlockSpec((tm, K), lambda i, j: (i, 0)), pl.BlockSpec((tn, K), lambda i, j: (j, 0)),
                        t_spec, t_spec],
              out_specs=pl.BlockSpec((2, tm, tn), lambda i, j: (0, i, j)),
              compiler_params=_cp("parallel", "parallel"))(dyb, wd, gate, up, *(cargo.arrays if cargo else ()))
    return _split_cargo(out, cargo)


def _gateup(h, wgu, *, name, tm=512, tn=512, cargo=None):
    M, K = h.shape
    Fh = wgu.shape[1] // 2
    tm, tn = _tile(M, tm), _tile(Fh, tn)
    nj = Fh // tn

    def body(h_ref, wg_ref, wu_ref, g_ref, u_ref, a_ref):
        hv = h_ref[...]
        g = jnp.dot(hv, wg_ref[...], preferred_element_type=F32)
        u = jnp.dot(hv, wu_ref[...], preferred_element_type=F32)
        g_ref[...] = g.astype(BF16)
        u_ref[...] = u.astype(BF16)
        a_ref[...] = (g * jax.nn.sigmoid(g) * u).astype(BF16)

    o_spec = pl.BlockSpec((tm, tn), lambda i, j: (i, j))
    shp = jax.ShapeDtypeStruct((M, Fh), BF16)
    out = _pc(body, name=name, cargo=cargo, out_shape=(shp, shp, shp), grid=(M // tm, nj),
              in_specs=[pl.BlockSpec((tm, K), lambda i, j: (i, 0)),
                        pl.BlockSpec((K, tn), lambda i, j: (0, j)),
                        pl.BlockSpec((K, tn), lambda i, j: (0, j + nj))],
              out_specs=(o_spec, o_spec, o_spec),
              compiler_params=_cp("parallel", "parallel"))(h, wgu, wgu, *(cargo.arrays if cargo else ()))
    return _split_cargo(out, cargo)


def _rowmap(fn, ins, outs, *, rows, tm, name, accs=(), cargo=None):
    tm = _tile(rows, tm, 8)
    arrs, in_specs = [], []
    for d in ins:
        if d[0] == "row":
            _, a, w, cb = d
            w = a.shape[1] if w is None else w
            in_specs.append(pl.BlockSpec((tm, w), functools.partial(lambda i, cb: (i, cb), cb=cb)))
        else:
            a = d[1]
            in_specs.append(pl.BlockSpec(a.shape, functools.partial(lambda i, nd: (0,) * nd, nd=a.ndim)))
        arrs.append(a)
    out_shape = [jax.ShapeDtypeStruct((rows, w), dt) for w, dt in outs]
    out_specs = [pl.BlockSpec((tm, w), lambda i: (i, 0)) for w, _ in outs]
    for shp, dt in accs:
        out_shape.append(jax.ShapeDtypeStruct(shp, dt))
        out_specs.append(pl.BlockSpec(shp, functools.partial(lambda i, nd: (0,) * nd, nd=len(shp))))
    nin, nout, nacc = len(ins), len(outs), len(accs)

    def body(*refs):
        res = fn(*[r[...] for r in refs[:nin]])
        res = res if isinstance(res, (tuple, list)) else (res,)
        for r, v in zip(refs[nin:nin + nout], res[:nout]):
            r[...] = v.astype(r.dtype)
        if nacc:
            acc_refs = refs[nin + nout:]

            @pl.when(pl.program_id(0) == 0)
            def _():
                for r in acc_refs:
                    r[...] = jnp.zeros_like(r)

            for r, v in zip(acc_refs, res[nout:]):
                r[...] += v

    out = _pc(body, name=name, cargo=cargo, out_shape=tuple(out_shape), grid=(rows // tm,), in_specs=in_specs,
              out_specs=tuple(out_specs),
              compiler_params=_cp("arbitrary" if nacc else "parallel"))(*arrs, *(cargo.arrays if cargo else ()))
    return _split_cargo(out, cargo) if cargo else (out[0] if len(out) == 1 else out)


def _rms_fwd(x, g, *, name):
    def fn(xv, gv):
        return xv * lax.rsqrt(jnp.mean(xv * xv, axis=-1, keepdims=True) + EPS) * gv
    return _rowmap(fn, [("row", x, None, 0), ("bc", g)], [(x.shape[1], BF16)], rows=x.shape[0], tm=256, name=name)


def _rms_bwd(x, dh, g, dres, *, name):
    D = x.shape[1]

    def fn(xv, dhv, gv, drv):
        r = lax.rsqrt(jnp.mean(xv * xv, axis=-1, keepdims=True) + EPS)
        xh = xv * r
        dhf = dhv.astype(F32)
        dxn = dhf * gv
        dx = drv + r * (dxn - xh * jnp.mean(dxn * xh, axis=-1, keepdims=True))
        return dx, dx, jnp.sum(dhf * xh, axis=0, keepdims=True)

    return _rowmap(fn, [("row", x, None, 0), ("row", dh, None, 0), ("bc", g), ("row", dres, None, 0)],
                   [(D, F32), (D, BF16)], rows=x.shape[0], tm=256, name=name, accs=[((1, D), F32)])


def _loss(y, tgt, *, name):
    D = y.shape[1]

    def fn(yv, tv):
        e = yv - tv
        d = e * (1.0 / D)
        part = 0.5 * jnp.sum(jnp.sum(e * e, axis=-1, keepdims=True) * (1.0 / D), axis=0, keepdims=True)
        return d, d, jnp.broadcast_to(part, (1, LANES))

    return _rowmap(fn, [("row", y, None, 0), ("row", tgt, None, 0)], [(D, F32), (D, BF16)],
                   rows=y.shape[0], tm=256, name=name, accs=[((1, LANES), F32)])


def _merge_fwd(z, yp, yf, ym, *, name, cargo=None):
    D = yp.shape[1]

    def fn(gp, gf, gm, a, b, c):
        return jax.nn.sigmoid(gp) * a + jax.nn.sigmoid(gf) * b + jax.nn.sigmoid(gm) * c

    return _rowmap(fn, [("row", z, D, 0), ("row", z, D, 1), ("row", z, D, 2),
                        ("row", yp, None, 0), ("row", yf, None, 0), ("row", ym, None, 0)],
                   [(D, BF16)], rows=yp.shape[0], tm=128, name=name, cargo=cargo)


def _merge_bwd(dm, z, yp, yf, ym, *, name):
    D = yp.shape[1]

    def fn(d, gp, gf, gm, a, b, c):
        outs, dgl = [], []
        for gl, yv in ((gp, a), (gf, b), (gm, c)):
            s = jax.nn.sigmoid(gl)
            outs.append(d * s)
            dgl.append(d * yv * (s * (1.0 - s)))
        return (*outs, jnp.concatenate(dgl, axis=-1))

    return _rowmap(fn, [("row", dm, None, 0), ("row", z, D, 0), ("row", z, D, 1), ("row", z, D, 2),
                        ("row", yp, None, 0), ("row", yf, None, 0), ("row", ym, None, 0)],
                   [(D, BF16), (D, BF16), (D, BF16), (3 * D, BF16)], rows=yp.shape[0], tm=128, name=name)


def _adamw(w, g, m, v, *, name):
    C = w.shape[1]

    def fn(wv, gv, mv, vv):
        m2 = ADAM_B1 * mv + (1.0 - ADAM_B1) * gv
        v2 = ADAM_B2 * vv + (1.0 - ADAM_B2) * (gv * gv)
        m_hat = m2 / (1.0 - ADAM_B1 ** ADAM_STEP)
        v_hat = v2 / (1.0 - ADAM_B2 ** ADAM_STEP)
        return -ADAM_LR * (m_hat / (jnp.sqrt(v_hat) + ADAM_EPS) + ADAM_WD * wv), m2, v2

    tm = max(8, (262144 // C) // 8 * 8)
    return _rowmap(fn, [("row", a, None, 0) for a in (w, g, m, v)], [(C, F32)] * 3,
                   rows=w.shape[0], tm=tm, name=name)


def _seg_mean(v, hd):
    if hd == LANES:
        return jnp.mean(v, axis=-1, keepdims=True)
    lo = lax.broadcasted_iota(jnp.int32, v.shape, 1) < hd
    s0 = jnp.sum(jnp.where(lo, v, 0.0), axis=-1, keepdims=True)
    s1 = jnp.sum(jnp.where(lo, 0.0, v), axis=-1, keepdims=True)
    return jnp.where(lo, s0, s1) * (1.0 / hd)


def _gain128(g, hd):
    g = g.reshape(1, hd).astype(F32)
    return g if hd == LANES else jnp.concatenate([g, g], axis=-1)


def _headnorm_fwd(x, cb0, width, gain, hd, *, name):
    rows = x.shape[0]
    tm = _tile(rows, 512, 8)
    g128 = _gain128(gain, hd)

    def body(x_ref, g_ref, o_ref):
        xv = x_ref[...]
        o_ref[...] = (xv * lax.rsqrt(_seg_mean(xv * xv, hd) + EPS) * g_ref[...]).astype(BF16)

    return _pc(body, name=name, out_shape=jax.ShapeDtypeStruct((rows, width), BF16),
               grid=(rows // tm, width // LANES),
               in_specs=[pl.BlockSpec((tm, LANES), lambda i, j: (i, cb0 + j)),
                         pl.BlockSpec((1, LANES), lambda i, j: (0, 0))],
               out_specs=pl.BlockSpec((tm, LANES), lambda i, j: (i, j)),
               compiler_params=_cp("parallel", "parallel"))(x, g128)


def _headnorm_bwd(x, cb0, width, dy, gain, hd, post, spread, out_dtype, *, name):
    rows = x.shape[0]
    tm = _tile(rows, 512, 8)
    g128 = _gain128(gain, hd)
    dyw = 2 * LANES if spread else LANES

    def body(x_ref, dy_ref, g_ref, dx_ref, dg_ref):
        xv = x_ref[...]
        dyv = dy_ref[...]
        if spread:
            dyv = jnp.concatenate([dyv[:, :hd], dyv[:, LANES:LANES + hd]], axis=-1)
        dyv = dyv * post
        r = lax.rsqrt(_seg_mean(xv * xv, hd) + EPS)
        xh = xv * r
        dxn = dyv * g_ref[...]
        dx_ref[...] = (r * (dxn - xh * _seg_mean(dxn * xh, hd))).astype(dx_ref.dtype)

        @pl.when((pl.program_id(0) == 0) & (pl.program_id(1) == 0))
        def _():
            dg_ref[...] = jnp.zeros_like(dg_ref)

        dg_ref[...] += jnp.sum(dyv * xh, axis=0, keepdims=True)

    return _pc(body, name=name,
               out_shape=(jax.ShapeDtypeStruct((rows, width), out_dtype), jax.ShapeDtypeStruct((1, LANES), F32)),
               grid=(rows // tm, width // LANES),
               in_specs=[pl.BlockSpec((tm, LANES), lambda i, j: (i, cb0 + j)),
                         pl.BlockSpec((tm, dyw), lambda i, j: (i, j)),
                         pl.BlockSpec((1, LANES), lambda i, j: (0, 0))],
               out_specs=(pl.BlockSpec((tm, LANES), lambda i, j: (i, j)),
                          pl.BlockSpec((1, LANES), lambda i, j: (0, 0))),
               compiler_params=_cp("arbitrary", "arbitrary"))(x, dy, g128)


def _fold_gain(dg, hd):
    return dg if hd == LANES else dg[:, :hd] + dg[:, hd:]


def _shift_down(v, k, row):
    return jnp.where(row >= k, pltpu.roll(v, k, 0), 0.0)


def _shift_up(v, k, row, S):
    return jnp.where(row < S - k, pltpu.roll(v, S - k, 0), 0.0)


def _window_sum(v, g, row, shift):
    s, outs = v, []
    for k in (1, 2, 4, 8):
        s = s + shift(s, k)
        outs.append(s)
    return jnp.where(g == 0, outs[0], jnp.where(g == 1, outs[1], jnp.where(g == 2, outs[2], outs[3])))


def _pool_fwd(z, cb0, pool_w, pool_scale, nb, S, *, name):
    PG, PD = pool_w.shape[0], pool_w.shape[1]
    assert PD == LANES and PG <= len(POOL_WINDOWS)
    pw = pool_w.astype(BF16)

    def body(u_ref, w_ref, s_ref, o_ref):
        g = pl.program_id(0)
        u = u_ref[...]
        row = lax.broadcasted_iota(jnp.int32, u.shape, 0)
        cnt = jnp.minimum(row + 1, jnp.left_shift(2, g)).astype(F32)
        pooled = _window_sum(u, g, row, lambda v, k: _shift_down(v, k, row)) / cnt
        mixed = jnp.dot((pooled - u).astype(BF16), w_ref[...], preferred_element_type=F32)
        o_ref[...] = (mixed * s_ref[...]).astype(BF16)

    return _pc(body, name=name, out_shape=jax.ShapeDtypeStruct((nb * S, PG * PD), BF16), grid=(PG, nb),
               in_specs=[pl.BlockSpec((S, PD), lambda g, b: (b, cb0 + g)),
                         pl.BlockSpec((None, PD, PD), lambda g, b: (g, 0, 0)),
                         pl.BlockSpec((1, PD), lambda g, b: (0, g))],
               out_specs=pl.BlockSpec((S, PD), lambda g, b: (b, g)),
               compiler_params=_cp("parallel", "parallel"))(z, pw, pool_scale)


def _pool_bwd(z, cb0, dv, pool_w, pool_scale, nb, S, *, name):
    PG, PD = pool_w.shape[0], pool_w.shape[1]
    pw = pool_w.astype(BF16)

    def body(u_ref, dv_ref, w_ref, s_ref, du_ref, dw_ref, ds_ref):
        g, b = pl.program_id(0), pl.program_id(1)
        u, dvv, w = u_ref[...], dv_ref[...], w_ref[...]
        row = lax.broadcasted_iota(jnp.int32, u.shape, 0)
        cnt = jnp.minimum(row + 1, jnp.left_shift(2, g)).astype(F32)
        diff = (_window_sum(u, g, row, lambda v, k: _shift_down(v, k, row)) / cnt - u).astype(BF16)
        mixed = jnp.dot(diff, w, preferred_element_type=F32)
        dmixed = (dvv * s_ref[...]).astype(BF16)
        ddiff = lax.dot_general(dmixed, w, _DIMS["nt"], preferred_element_type=F32)
        du_ref[...] = (_window_sum(ddiff / cnt, g, row, lambda v, k: _shift_up(v, k, row, S)) - ddiff).astype(BF16)

        @pl.when(b == 0)
        def _():
            dw_ref[...] = jnp.zeros_like(dw_ref)
            ds_ref[...] = jnp.zeros_like(ds_ref)

        dw_ref[...] += lax.dot_general(diff, dmixed, _DIMS["tn"], preferred_element_type=F32)
        ds_ref[...] += jnp.sum(dvv * mixed, axis=0, keepdims=True)

    return _pc(body, name=name,
               out_shape=(jax.ShapeDtypeStruct((nb * S, PG * PD), BF16), jax.ShapeDtypeStruct((PG, PD, PD), F32),
                          jax.ShapeDtypeStruct((1, PG * PD), F32)),
               grid=(PG, nb),
               in_specs=[pl.BlockSpec((S, PD), lambda g, b: (b, cb0 + g)),
                         pl.BlockSpec((S, PD), lambda g, b: (b, g)),
                         pl.BlockSpec((None, PD, PD), lambda g, b: (g, 0, 0)),
                         pl.BlockSpec((1, PD), lambda g, b: (0, g))],
               out_specs=(pl.BlockSpec((S, PD), lambda g, b: (b, g)),
                          pl.BlockSpec((None, PD, PD), lambda g, b: (g, 0, 0)),
                          pl.BlockSpec((1, PD), lambda g, b: (0, g))),
               compiler_params=_cp("parallel", "arbitrary"))(z, dv, pw, pool_scale)


def _split3(c):
    hi = c.astype(BF16)
    r1 = c - hi.astype(F32)
    mid = r1.astype(BF16)
    lo = (r1 - mid.astype(F32)).astype(BF16)
    return hi, mid, lo


def _log_sigmoid(v):
    return jnp.minimum(v, 0.0) - jnp.log(1.0 + jnp.exp(-jnp.abs(v)))


def _fox_c(z, cb, b128, nb, S, NF, *, name):
    def body(f_ref, b_ref, hi_ref, mid_ref, lo_ref):
        v = f_ref[...] + b_ref[...]
        lane = lax.broadcasted_iota(jnp.int32, v.shape, 1)
        row = lax.broadcasted_iota(jnp.int32, v.shape, 0)
        c = jnp.where(lane < NF, _log_sigmoid(v), 0.0)
        k = 1
        while k < S:
            c = c + _shift_down(c, k, row)
            k *= 2
        hi, mid, lo = _split3(c)
        hi_ref[...], mid_ref[...], lo_ref[...] = hi, mid, lo

    shp = jax.ShapeDtypeStruct((nb * S, LANES), BF16)
    spec = pl.BlockSpec((S, LANES), lambda i: (i, 0))
    return _pc(body, name=name, out_shape=(shp, shp, shp), grid=(nb,),
               in_specs=[pl.BlockSpec((S, LANES), lambda i: (i, cb)), pl.BlockSpec((1, LANES), lambda i: (0, 0))],
               out_specs=(spec, spec, spec), compiler_params=_cp("parallel"))(z, b128)


def _fox_c_bwd(dc, z, cb, b128, nb, S, *, name):
    def body(dc_ref, f_ref, b_ref, df_ref, db_ref):
        d = dc_ref[...]
        row = lax.broadcasted_iota(jnp.int32, d.shape, 0)
        k = 1
        while k < S:
            d = d + _shift_up(d, k, row, S)
            k *= 2
        df = d * jax.nn.sigmoid(-(f_ref[...] + b_ref[...]))
        df_ref[...] = df

        @pl.when(pl.program_id(0) == 0)
        def _():
            db_ref[...] = jnp.zeros_like(db_ref)

        db_ref[...] += jnp.sum(df, axis=0, keepdims=True)

    return _pc(body, name=name,
               out_shape=(jax.ShapeDtypeStruct((nb * S, LANES), F32), jax.ShapeDtypeStruct((1, LANES), F32)),
               grid=(nb,),
               in_specs=[pl.BlockSpec((S, LANES), lambda i: (i, 0)), pl.BlockSpec((S, LANES), lambda i: (i, cb)),
                         pl.BlockSpec((1, LANES), lambda i: (0, 0))],
               out_specs=(pl.BlockSpec((S, LANES), lambda i: (i, 0)), pl.BlockSpec((1, LANES), lambda i: (0, 0))),
               compiler_params=_cp("arbitrary"))(dc, z, b128)


def _fox_qk(z, cbq, cbk, cbv, c3, gq, gk, HD, NF, *, name, cargo=None):
    T = z.shape[0]
    HP = NF * HD // LANES
    tm = _tile(T, 512, 8)
    scale = HD ** -0.5
    gq128, gk128 = _gain128(gq, HD), _gain128(gk, HD)

    def body(q_ref, k_ref, v_ref, hi_ref, mid_ref, lo_ref, gq_ref, gk_ref, qa_ref, ka_ref, vb_ref):
        hp = pl.program_id(1)
        q, k = q_ref[...], k_ref[...]
        qn = q * lax.rsqrt(_seg_mean(q * q, HD) + EPS) * (gq_ref[...] * scale)
        kn = k * lax.rsqrt(_seg_mean(k * k, HD) + EPS) * gk_ref[...]
        vb_ref[...] = v_ref[...].astype(BF16)
        c3v = jnp.concatenate([hi_ref[...], mid_ref[...], lo_ref[...]], axis=-1)
        r = lax.broadcasted_iota(jnp.int32, (3 * LANES, LANES - HD), 0)
        cc = lax.broadcasted_iota(jnp.int32, (3 * LANES, LANES - HD), 1)
        lane = lax.broadcasted_iota(jnp.int32, (tm, LANES - HD), 1)
        for hh in range(LANES // HD):
            head = hp * (LANES // HD) + hh
            sel_q = jnp.where((cc < 3) & (r == head + LANES * cc), 1.0, 0.0).astype(BF16)
            sel_k = jnp.where((cc >= 3) & (cc < 6) & (r == head + LANES * (cc - 3)), 1.0, 0.0).astype(BF16)
            qaug = jnp.dot(c3v, sel_q, preferred_element_type=F32) + jnp.where((lane >= 3) & (lane < 6), 1.0, 0.0)
            kaug = (jnp.where((lane < 3) | ((lane >= 6) & (lane < 9)), 1.0, 0.0)
                    - jnp.dot(c3v, sel_k, preferred_element_type=F32))
            sl = slice(hh * HD, (hh + 1) * HD)
            qa_ref[:, hh * LANES:(hh + 1) * LANES] = jnp.concatenate([qn[:, sl], qaug], axis=-1).astype(BF16)
            ka_ref[:, hh * LANES:(hh + 1) * LANES] = jnp.concatenate([kn[:, sl], kaug], axis=-1).astype(BF16)

    aw = (LANES // HD) * LANES
    blk = lambda cb: pl.BlockSpec((tm, LANES), functools.partial(lambda i, j, cb: (i, cb + j), cb=cb))
    cspec = pl.BlockSpec((tm, LANES), lambda i, j: (i, 0))
    gspec = pl.BlockSpec((1, LANES), lambda i, j: (0, 0))
    out = _pc(body, name=name, cargo=cargo,
              out_shape=(jax.ShapeDtypeStruct((T, HP * aw), BF16), jax.ShapeDtypeStruct((T, HP * aw), BF16),
                         jax.ShapeDtypeStruct((T, NF * HD), BF16)),
              grid=(T // tm, HP),
              in_specs=[blk(cbq), blk(cbk), blk(cbv), cspec, cspec, cspec, gspec, gspec],
              out_specs=(pl.BlockSpec((tm, aw), lambda i, j: (i, j)), pl.BlockSpec((tm, aw), lambda i, j: (i, j)),
                         pl.BlockSpec((tm, LANES), lambda i, j: (i, j))),
              compiler_params=_cp("parallel", "parallel"))(z, z, z, *c3, gq128, gk128,
                                                           *(cargo.arrays if cargo else ()))
    return _split_cargo(out, cargo)


def _fox_logits(qa, ka, fill):
    s = lax.dot_general(qa, ka, _DIMS["nt"], preferred_element_type=F32)
    tq, L = s.shape
    diag = s[:, L - tq:]
    keep = lax.broadcasted_iota(jnp.int32, diag.shape, 1) <= lax.broadcasted_iota(jnp.int32, diag.shape, 0)
    diag = jnp.where(keep, diag, fill)
    return diag if L == tq else jnp.concatenate([s[:, :L - tq], diag], axis=-1), keep


def _fox_fwd(qa, ka, vb, nb, S, HD, *, name, tq=256, cargo=None):
    T = qa.shape[0]
    nh = LANES // HD
    aw = nh * LANES
    HP = qa.shape[1] // aw
    tq = _tile(S, tq, 8)
    nq = S // tq

    def body(qa_ref, ka_ref, v_ref, o_ref, qb_ref):
        lane = lax.broadcasted_iota(jnp.int32, (tq, LANES), 1)
        for k in range(nq):
            @pl.when(pl.program_id(2) == k)
            def _(k=k):
                L = (k + 1) * tq
                outs = []
                v = v_ref[0:L, :]
                for hh in range(nh):
                    al = slice(hh * LANES, (hh + 1) * LANES)
                    qav = qa_ref[:, al]
                    s, _ = _fox_logits(qav, ka_ref[0:L, al], NEG)
                    m = jnp.max(s, axis=-1, keepdims=True)
                    e = jnp.exp(s - m)
                    l = jnp.sum(e, axis=-1, keepdims=True)
                    p = e / l
                    outs.append(jnp.dot(p.astype(BF16), v[:, hh * HD:(hh + 1) * HD], preferred_element_type=F32))
                    hi, mid, lo = _split3(-(m + jnp.log(l)))
                    qb_ref[:, al] = jnp.where(lane == HD + 6, hi, jnp.where(lane == HD + 7, mid,
                                                                         jnp.where(lane == HD + 8, lo, qav)))
                o_ref[...] = jnp.concatenate(outs, axis=-1).astype(BF16)

    qspec = pl.BlockSpec((tq, aw), lambda b, h, i: (b * nq + i, h))
    out = _pc(body, name=name, cargo=cargo,
              out_shape=(jax.ShapeDtypeStruct((T, HP * LANES), BF16), jax.ShapeDtypeStruct(qa.shape, BF16)),
              grid=(nb, HP, nq),
              in_specs=[qspec, pl.BlockSpec((S, aw), lambda b, h, i: (b, h)),
                        pl.BlockSpec((S, LANES), lambda b, h, i: (b, h))],
              out_specs=(pl.BlockSpec((tq, LANES), lambda b, h, i: (b * nq + i, h)), qspec),
              compiler_params=_cp("parallel", "parallel", "arbitrary"))(qa, ka, vb, *(cargo.arrays if cargo else ()))
    return _split_cargo(out, cargo)


def _fox_bwd(qa, ka, vb, do, nb, S, HD, *, name, tq=256, cargo=None):
    T = qa.shape[0]
    nh = LANES // HD
    aw = nh * LANES
    HP = qa.shape[1] // aw
    tq = _tile(S, tq, 8)
    nq = S // tq

    def body(qa_ref, ka_ref, v_ref, do_ref, dqa_ref, dka_ref, dv_ref, dcs_ref):
        i = pl.program_id(2)

        @pl.when(i == 0)
        def _():
            dka_ref[...] = jnp.zeros_like(dka_ref)
            dv_ref[...] = jnp.zeros_like(dv_ref)
            dcs_ref[...] = jnp.zeros_like(dcs_ref)

        for k in range(nq):
            @pl.when(i == k)
            def _(k=k):
                L = (k + 1) * tq
                dkas, dvs, css = [], [], []
                v, dob = v_ref[0:L, :], do_ref[...]
                for hh in range(nh):
                    al = slice(hh * LANES, (hh + 1) * LANES)
                    hl = slice(hh * HD, (hh + 1) * HD)
                    qav, kav, vv, dov = qa_ref[:, al], ka_ref[0:L, al], v[:, hl], dob[:, hl]
                    p = jnp.exp(_fox_logits(qav, kav, NEG)[0])
                    dp = lax.dot_general(dov, vv, _DIMS["nt"], preferred_element_type=F32)
                    dsf = p * (dp - jnp.sum(p * dp, axis=-1, keepdims=True))
                    css.append(jnp.sum(dsf, axis=0, keepdims=True))
                    ds = dsf.astype(BF16)
                    dqa_ref[:, al] = jnp.dot(ds, kav, preferred_element_type=F32)
                    dkas.append(lax.dot_general(ds, qav, _DIMS["tn"], preferred_element_type=F32))
                    dvs.append(lax.dot_general(p.astype(BF16), dov, _DIMS["tn"], preferred_element_type=F32))
                dka_ref[0:L, :] += jnp.concatenate(dkas, axis=-1)
                dv_ref[0:L, :] += jnp.concatenate(dvs, axis=-1)
                dcs_ref[:, 0:L] += jnp.concatenate(css + [jnp.zeros((8 - nh, L), F32)], axis=0)

    out = _pc(body, name=name, cargo=cargo,
              out_shape=(jax.ShapeDtypeStruct((T, HP * aw), F32), jax.ShapeDtypeStruct((T, HP * aw), F32),
                         jax.ShapeDtypeStruct((T, HP * LANES), F32), jax.ShapeDtypeStruct((nb, HP, 8, S), F32)),
              grid=(nb, HP, nq),
              in_specs=[pl.BlockSpec((tq, aw), lambda b, h, i: (b * nq + i, h)),
                        pl.BlockSpec((S, aw), lambda b, h, i: (b, h)),
                        pl.BlockSpec((S, LANES), lambda b, h, i: (b, h)),
                        pl.BlockSpec((tq, LANES), lambda b, h, i: (b * nq + i, h))],
              out_specs=(pl.BlockSpec((tq, aw), lambda b, h, i: (b * nq + i, h)),
                         pl.BlockSpec((S, aw), lambda b, h, i: (b, h)),
                         pl.BlockSpec((S, LANES), lambda b, h, i: (b, h)),
                         pl.BlockSpec((None, None, 8, S), lambda b, h, i: (b, h, 0, 0))),
              compiler_params=_cp("parallel", "parallel", "arbitrary"))(qa, ka, vb, do,
                                                                         *(cargo.arrays if cargo else ()))
    return _split_cargo(out, cargo)


def _mem_probs(q, k, scale):
    s = lax.dot_general(q, k, _DIMS["nt"], preferred_element_type=F32) * scale
    e = jnp.exp(s - jnp.max(s, axis=-1, keepdims=True))
    return e / jnp.sum(e, axis=-1, keepdims=True)


def _mem_fwd(q, k, v, nb, S, ML, *, name, tq=512):
    T, MW = q.shape
    nh = MW // LANES
    tq = _tile(S, tq, 8)
    nq = S // tq
    scale = LANES ** -0.5

    def body(q_ref, k_ref, v_ref, o_ref):
        for h in range(nh):
            sl = slice(h * LANES, (h + 1) * LANES)
            p = _mem_probs(q_ref[:, sl], k_ref[:, sl], scale)
            o_ref[:, sl] = jnp.dot(p.astype(BF16), v_ref[:, sl], preferred_element_type=F32).astype(BF16)

    return _pc(body, name=name, out_shape=jax.ShapeDtypeStruct((T, MW), BF16), grid=(nb, nq),
               in_specs=[pl.BlockSpec((tq, MW), lambda b, i: (b * nq + i, 0)),
                         pl.BlockSpec((ML, MW), lambda b, i: (b, 0)), pl.BlockSpec((ML, MW), lambda b, i: (b, 0))],
               out_specs=pl.BlockSpec((tq, MW), lambda b, i: (b * nq + i, 0)),
               compiler_params=_cp("parallel", "arbitrary"))(q, k, v)


def _mem_bwd(q, k, v, do, nb, S, ML, *, name, tq=512):
    T, MW = q.shape
    nh = MW // LANES
    tq = _tile(S, tq, 8)
    nq = S // tq
    scale = LANES ** -0.5

    def body(q_ref, k_ref, v_ref, do_ref, dq_ref, dk_ref, dv_ref):
        i = pl.program_id(1)

        @pl.when(i == 0)
        def _():
            dk_ref[...] = jnp.zeros_like(dk_ref)
            dv_ref[...] = jnp.zeros_like(dv_ref)

        for h in range(nh):
            sl = slice(h * LANES, (h + 1) * LANES)
            qv, kv, vv, dov = q_ref[:, sl], k_ref[:, sl], v_ref[:, sl], do_ref[:, sl]
            p = _mem_probs(qv, kv, scale)
            dp = lax.dot_general(dov, vv, _DIMS["nt"], preferred_element_type=F32)
            ds = (p * (dp - jnp.sum(p * dp, axis=-1, keepdims=True)) * scale).astype(BF16)
            dq_ref[:, sl] = jnp.dot(ds, kv, preferred_element_type=F32)
            dk_ref[:, sl] += lax.dot_general(ds, qv, _DIMS["tn"], preferred_element_type=F32)
            dv_ref[:, sl] += lax.dot_general(p.astype(BF16), dov, _DIMS["tn"], preferred_element_type=F32)

    kvspec = pl.BlockSpec((ML, MW), lambda b, i: (b, 0))
    qspec = pl.BlockSpec((tq, MW), lambda b, i: (b * nq + i, 0))
    return _pc(body, name=name,
               out_shape=(jax.ShapeDtypeStruct((T, MW), F32), jax.ShapeDtypeStruct((nb * ML, MW), F32),
                          jax.ShapeDtypeStruct((nb * ML, MW), F32)),
               grid=(nb, nq), in_specs=[qspec, kvspec, kvspec, qspec], out_specs=(qspec, kvspec, kvspec),
               compiler_params=_cp("parallel", "arbitrary"))(q, k, v, do)


def _place():
    x, y, c = lax.axis_index("x"), lax.axis_index("y"), lax.axis_index("c")
    chips = [(1 - x, y), (x, 1 - y), (1 - x, 1 - y)]
    return x, y, c, chips


def _any_specs(n):
    return [pl.BlockSpec(memory_space=pl.ANY)] * n


class _Cargo:
    def __init__(self, arrays, out_shapes, ncopies, copies, alias=None):
        self.arrays, self.out_shapes, self.ncopies, self.copies = list(arrays), list(out_shapes), ncopies, copies
        self.alias = alias or {}


def _load_cargo(body, cargo, kw):
    as_list = lambda v: list(v) if isinstance(v, (tuple, list)) else [v]
    grid = kw.get("grid", ())
    in_specs, out_specs, out_shape = as_list(kw["in_specs"]), as_list(kw["out_specs"]), as_list(kw["out_shape"])
    scratch = list(kw.get("scratch_shapes", ()))
    nin, nout, nci, nco, nscr = len(in_specs), len(out_specs), len(cargo.arrays), len(cargo.out_shapes), len(scratch)

    def loaded(*refs):
        ins, cin = refs[:nin], refs[nin:nin + nci]
        outs, cout = refs[nin + nci:nin + nci + nout], refs[nin + nci + nout:nin + nci + nout + nco]
        scr, (ssem, rsem) = refs[nin + nci + nout + nco:-2], refs[-2:]
        first = last = True
        for a, g in enumerate(grid):
            first = first & (pl.program_id(a) == 0)
            last = last & (pl.program_id(a) == g - 1)

        def start():
            for cp in cargo.copies(cin, cout, ssem, rsem, False):
                cp.start()

        def wait():
            for cp in cargo.copies(cin, cout, ssem, rsem, True):
                cp.wait_send()
                cp.wait_recv()

        if grid:
            pl.when(first)(start)
        else:
            start()
        body(*ins, *outs, *scr)
        if grid:
            pl.when(last)(wait)
        else:
            wait()

    kw = dict(kw, in_specs=in_specs + _any_specs(nci), out_specs=tuple(out_specs + _any_specs(nco)),
              out_shape=tuple(out_shape + cargo.out_shapes),
              scratch_shapes=scratch + [pltpu.SemaphoreType.DMA((cargo.ncopies,))] * 2)
    if cargo.alias:
        kw["input_output_aliases"] = {nin + i: nout + o for i, o in cargo.alias.items()}
    if grid:
        kw["compiler_params"] = _cp(*["arbitrary"] * len(grid))
    return loaded, kw


class _SemView:
    def __init__(self, ref, off):
        self.ref, self.off, self.at = ref, off, self

    def __getitem__(self, k):
        return self.ref.at[self.off + k]


def _join_cargo(a, b):
    na, nao = len(a.arrays), len(a.out_shapes)

    def copies(ins, outs, ssem, rsem, waiting):
        return (a.copies(ins[:na], outs[:nao], ssem, rsem, waiting) +
                b.copies(ins[na:], outs[nao:], _SemView(ssem, a.ncopies), _SemView(rsem, a.ncopies), waiting))

    alias = dict(a.alias)
    alias.update({na + i: nao + o for i, o in b.alias.items()})
    return _Cargo(a.arrays + b.arrays, a.out_shapes + b.out_shapes, a.ncopies + b.ncopies, copies, alias)


def _run_cargo(cargo, *, name):
    res = _pc(lambda: None, name=name, cargo=cargo, in_specs=[], out_specs=[], out_shape=[])(*cargo.arrays)
    return list(res)


def _full_shape(dims, kind):
    return (dims[0], 4 * dims[1]) if kind == "col" else (4 * dims[0], dims[1])


def _ag_copy(shard, full, kind, j_src, chip, c, ssem, rsem, piece):
    R, C = shard.shape
    p, npieces, cnt = (*piece, 1)[:3]
    rows = R // 2 // npieces * cnt
    row0 = pl.multiple_of(c * (R // 2) + p * (R // 2 // npieces), 16)
    if kind == "col":
        dst = full.at[pl.ds(row0, rows), pl.ds(pl.multiple_of(j_src * C, LANES), C)]
    else:
        dst = full.at[pl.ds(pl.multiple_of(j_src * R + row0, 16), rows), :]
    return pltpu.make_async_remote_copy(shard.at[pl.ds(row0, rows), :], dst, ssem, rsem,
                                        device_id=(*chip, c), device_id_type=MESH)


def _ag_cargo(shards, kinds, piece=(0, 1), lands=None):
    n = len(shards)

    def copies(ins, outs, ssem, rsem, waiting):
        x, y, c, chips = _place()
        return [_ag_copy(ins[w], outs[w], kinds[w], 2 * chip[0] + chip[1] if waiting else 2 * x + y, chip, c,
                         ssem.at[3 * w + q], rsem.at[3 * w + q], piece)
                for w in range(n) for q, chip in enumerate(chips)]

    shapes = [jax.ShapeDtypeStruct(_full_shape(s.shape, k), BF16) for s, k in zip(shards, kinds)]
    if lands is None:
        return _Cargo(shards, shapes, 3 * n, copies)
    return _Cargo(list(shards) + list(lands), shapes, 3 * n, copies, alias={n + w: w for w in range(n)})


def _scatter_cargo(parts, kinds, piece=(0, 1), lands=None):
    n = len(parts)
    pdims = [((p.shape[0], p.shape[1] // 4) if k == "col" else (p.shape[0] // 4, p.shape[1]))
             for p, k in zip(parts, kinds)]

    def copy(src, dst, kind, R, C, q, j, chip, c, ssem, rsem):
        rows = R // piece[1]
        r0 = piece[0] * rows
        if kind == "col":
            win = src.at[pl.ds(r0, rows), pl.ds(pl.multiple_of(j * C, LANES), C)]
        else:
            win = src.at[pl.ds(pl.multiple_of(j * R + r0, 16), rows), :]
        return pltpu.make_async_remote_copy(win, dst.at[q, pl.ds(r0, rows), :], ssem, rsem,
                                            device_id=(*chip, c), device_id_type=MESH)

    def copies(ins, outs, ssem, rsem, waiting):
        x, y, c, chips = _place()
        return [copy(ins[w], outs[w], kinds[w], *pdims[w], q, 2 * chip[0] + chip[1], chip, c,
                     ssem.at[3 * w + q], rsem.at[3 * w + q])
                for w in range(n) for q, chip in enumerate(chips)]

    shapes = [jax.ShapeDtypeStruct((3,) + d, BF16) for d in pdims]
    if lands is None:
        return _Cargo(parts, shapes, 3 * n, copies)
    return _Cargo(list(parts) + list(lands), shapes, 3 * n, copies, alias={n + w: w for w in range(n)})


def _ag_fill(full, shard, kind, *, name):
    R, C = shard.shape
    RH = R // 2
    tm = _tile(RH, max(16, (3 * 2 ** 19 // C) // 16 * 16), 16)
    nbh = RH // tm
    n, nown = 3 * nbh, 2 * nbh
    xyc = jnp.stack([lax.axis_index("x"), lax.axis_index("y"), lax.axis_index("c")]).astype(jnp.int32)

    def chip_of(q, x, y):
        keep_x, keep_y = q % 2, (2 - q) // 2
        return 2 * ((1 - x) + keep_x * (2 * x - 1)) + (1 - y) + keep_y * (2 * y - 1)

    def window(ref, j, row0):
        if kind == "col":
            return ref.at[pl.ds(pl.multiple_of(row0, 16), tm), pl.ds(pl.multiple_of(j * C, LANES), C)]
        return ref.at[pl.ds(pl.multiple_of(j * R + row0, 16), tm), :]

    def body(p_ref, send_ref, own_ref, out_ref, land, ssem, rsem, wsem):
        i = pl.program_id(0)
        x, y = lax.axis_index("x"), lax.axis_index("y")
        c, peer = _pair_peer()
        own_cp = pltpu.make_async_copy(own_ref, window(out_ref, 2 * x + y, jnp.minimum(i, nown - 1) * tm), wsem.at[1])

        @pl.when(i < nown)
        def _():
            own_cp.start()

        def consume(land, ps):
            b = i - 1
            cp = pltpu.make_async_copy(land.at[ps], window(out_ref, chip_of(b // nbh, x, y),
                                                           (1 - c) * RH + (b % nbh) * tm), wsem.at[0])
            cp.start()
            cp.wait()

        _exchange_step(i, n, send_ref, land, ssem, rsem, None, peer, consume)

        @pl.when(i < nown)
        def _():
            own_cp.wait()

    now = lambda i: jnp.minimum(i, n - 1)

    def send_map(i, p):
        q, b = now(i) // nbh, now(i) % nbh
        j = chip_of(q, p[0], p[1])
        return (p[2] * nbh + b, j) if kind == "col" else ((j * 2 + p[2]) * nbh + b, 0)

    grid_spec = pltpu.PrefetchScalarGridSpec(
        num_scalar_prefetch=1, grid=(n + 1,),
        in_specs=[pl.BlockSpec((tm, C), send_map),
                  pl.BlockSpec((tm, C), lambda i, p: (jnp.minimum(i, nown - 1), 0))],
        out_specs=pl.BlockSpec(memory_space=pl.ANY),
        scratch_shapes=[pltpu.VMEM((n, tm, C), BF16), pltpu.SemaphoreType.DMA((n,)), pltpu.SemaphoreType.DMA((n,)),
                        pltpu.SemaphoreType.DMA((2,))])
    return _pc(body, name=name, out_shape=jax.ShapeDtypeStruct(full.shape, BF16), grid_spec=grid_spec,
               input_output_aliases={1: 0}, compiler_params=_cp("arbitrary"))(xyc, full, shard)


def _pair_peer():
    x, y, c = lax.axis_index("x"), lax.axis_index("y"), lax.axis_index("c")
    return c, (x, y, 1 - c)


def _exchange_step(i, n, src_ref, land, ssem, rsem, credit, peer, consume):
    def rdma(slot):
        return pltpu.make_async_remote_copy(src_ref, land.at[slot], ssem.at[slot], rsem.at[slot],
                                            device_id=peer, device_id_type=MESH)

    reuse = credit is not None
    slot = i % 2 if reuse else jnp.minimum(i, n - 1)

    @pl.when(i < n)
    def _():
        if reuse:
            @pl.when(i >= 2)
            def _():
                pl.semaphore_wait(credit, 1)

        rdma(slot).start()

    @pl.when(i >= 1)
    def _():
        ps = (i - 1) % 2 if reuse else i - 1
        rdma(ps).wait_recv()
        consume(land, ps)

        if reuse:
            @pl.when(i - 1 < n - 2)
            def _():
                pl.semaphore_signal(credit, 1, device_id=peer, device_id_type=MESH)

    @pl.when(i < n)
    def _():
        rdma(slot).wait_send()


def _xchg_scratch(tm, C, dtype):
    return [pltpu.VMEM((2, tm, C), dtype), pltpu.SemaphoreType.DMA((2,)), pltpu.SemaphoreType.DMA((2,)),
            pltpu.SemaphoreType.REGULAR]


def _rs_pair_sum(g, kind, *, name):
    K, N = g.shape
    NS, RH = (1, K // 2) if kind == "col" else (4, K // 8)
    tm = _tile(RH, max(16, (3 * 2 ** 19 // N) // 16 * 16), 16)
    if NS * (RH // tm) < 2:
        tm = RH // 2
    nb = RH // tm
    n = NS * nb
    cvec = lax.axis_index("c").reshape(1).astype(jnp.int32)

    def body(c_ref, send_ref, own_ref, out_ref, land, ssem, rsem, credit):
        _, peer = _pair_peer()

        def consume(land, ps):
            out_ref[...] = (own_ref[...].astype(F32) + land[ps].astype(F32)).astype(BF16)

        _exchange_step(pl.program_id(0), n, send_ref, land, ssem, rsem, credit, peer, consume)

    now = lambda i: jnp.minimum(i, n - 1)
    lag = lambda i: jnp.maximum(i - 1, 0)
    grid_spec = pltpu.PrefetchScalarGridSpec(
        num_scalar_prefetch=1, grid=(n + 1,),
        in_specs=[pl.BlockSpec((tm, N), lambda i, c: ((now(i) // nb * 2 + 1 - c[0]) * nb + now(i) % nb, 0)),
                  pl.BlockSpec((tm, N), lambda i, c: ((lag(i) // nb * 2 + c[0]) * nb + lag(i) % nb, 0))],
        out_specs=pl.BlockSpec((tm, N), lambda i, c: (lag(i), 0)),
        scratch_shapes=_xchg_scratch(tm, N, BF16))
    return _pc(body, name=name, out_shape=jax.ShapeDtypeStruct((NS * RH, N), BF16), grid_spec=grid_spec,
               compiler_params=_cp("arbitrary"))(cvec, g, g)


def _rs_sum_join(part, got, kind, *, name):
    _, hr, hc = got.shape
    tm = _tile(hr, max(16, (3 * 2 ** 18 // hc) // 16 * 16), 16)
    if hr // tm < 2:
        tm = hr // 2
    nb = hr // tm
    jvec = (2 * lax.axis_index("x") + lax.axis_index("y")).reshape(1).astype(jnp.int32)

    def body(j_ref, p_ref, a_ref, b_ref, c_ref, mine_ref, theirs_ref, sbuf, land, ssem, rsem, credit):
        i = pl.program_id(0)
        _, peer = _pair_peer()

        @pl.when(i < nb)
        def _():
            v = ((p_ref[...].astype(F32) + a_ref[...].astype(F32)) + b_ref[...].astype(F32)) + c_ref[...].astype(F32)
            mine_ref[...] = v
            sbuf[...] = v

        def consume(land, ps):
            theirs_ref[...] = land[ps]

        _exchange_step(i, nb, sbuf, land, ssem, rsem, credit, peer, consume)

    now = lambda i: jnp.minimum(i, nb - 1)
    lag = lambda i: jnp.maximum(i - 1, 0)
    own_spec = (pl.BlockSpec((tm, hc), lambda i, j: (now(i), j[0])) if kind == "col"
                else pl.BlockSpec((tm, hc), lambda i, j: (j[0] * nb + now(i), 0)))
    got_spec = lambda q: pl.BlockSpec((None, tm, hc), functools.partial(lambda i, j, q: (q, now(i), 0), q=q))
    grid_spec = pltpu.PrefetchScalarGridSpec(
        num_scalar_prefetch=1, grid=(nb + 1,),
        in_specs=[own_spec, got_spec(0), got_spec(1), got_spec(2)],
        out_specs=(pl.BlockSpec((tm, hc), lambda i, j: (now(i), 0)), pl.BlockSpec((tm, hc), lambda i, j: (lag(i), 0))),
        scratch_shapes=[pltpu.VMEM((tm, hc), F32)] + _xchg_scratch(tm, hc, F32))
    shp = jax.ShapeDtypeStruct((hr, hc), F32)
    return _pc(body, name=name, out_shape=(shp, shp), grid_spec=grid_spec,
               compiler_params=_cp("arbitrary"))(jvec, part, got, got, got)


def _adamw_join(w, mine, theirs, m, v, *, name):
    R2, wc = w.shape
    hr, hc = mine.shape
    assert hr < R2 <= 2 * hr and hc >= wc
    pref, rows8 = max(8, 2 ** 18 // hc), _round_up(R2, 8)
    tm = max(t for t in range(8, hr + 1, 8) if hr % t == 0 and rows8 % t == 0 and (t <= pref or t == 8))
    nbh = hr // tm
    cvec = lax.axis_index("c").reshape(1).astype(jnp.int32)

    def body(c_ref, w_ref, a_ref, b_ref, m_ref, v_ref, g_ref, d_ref, m2_ref, v2_ref):
        gv = jnp.where(pl.program_id(0) // nbh == c_ref[0], a_ref[...], b_ref[...])[:, :wc]
        m2 = ADAM_B1 * m_ref[...] + (1.0 - ADAM_B1) * gv
        v2 = ADAM_B2 * v_ref[...] + (1.0 - ADAM_B2) * (gv * gv)
        m_hat = m2 / (1.0 - ADAM_B1 ** ADAM_STEP)
        v_hat = v2 / (1.0 - ADAM_B2 ** ADAM_STEP)
        g_ref[...] = gv
        d_ref[...] = -ADAM_LR * (m_hat / (jnp.sqrt(v_hat) + ADAM_EPS) + ADAM_WD * w_ref[...])
        m2_ref[...] = m2
        v2_ref[...] = v2

    wspec = pl.BlockSpec((tm, wc), lambda i, c: (i, 0))
    mine_spec = pl.BlockSpec((tm, hc), lambda i, c: (jnp.where(i // nbh == c[0], i % nbh, 0), 0))
    theirs_spec = pl.BlockSpec((tm, hc), lambda i, c: (jnp.where(i // nbh == c[0], 0, i % nbh), 0))
    grid_spec = pltpu.PrefetchScalarGridSpec(
        num_scalar_prefetch=1, grid=(pl.cdiv(R2, tm),),
        in_specs=[wspec, mine_spec, theirs_spec, wspec, wspec], out_specs=(wspec,) * 4)
    return _pc(body, name=name, out_shape=(jax.ShapeDtypeStruct((R2, wc), F32),) * 4, grid_spec=grid_spec,
               compiler_params=_cp("parallel"))(cvec, w, mine, theirs, m, v)


def _all_reduce_small(pack):
    R = pack.shape[0]

    def body(p_ref, o_ref, buf, send, recv):
        x, y, c, _ = _place()
        me = 4 * x + 2 * y + c
        buf[0] = p_ref[...]
        cps = []
        for r in range(1, 8):
            peer = (x if r & 4 == 0 else 1 - x, y if r & 2 == 0 else 1 - y, c if r & 1 == 0 else 1 - c)
            cp = pltpu.make_async_remote_copy(p_ref, buf.at[r], send.at[r - 1], recv.at[r - 1],
                                              device_id=peer, device_id_type=MESH)
            cp.start()
            cps.append(cp)
        for cp in cps:
            cp.wait()
        acc = buf[jnp.bitwise_xor(me, 0)]
        for k in range(1, 8):
            acc = acc + buf[jnp.bitwise_xor(me, k)]
        o_ref[...] = acc

    return _pc(body, name="allreduce_small", out_shape=jax.ShapeDtypeStruct((R, LANES), F32),
               in_specs=[pl.BlockSpec(memory_space=pltpu.VMEM)], out_specs=pl.BlockSpec(memory_space=pltpu.VMEM),
               scratch_shapes=[pltpu.VMEM((8, R, LANES), F32), pltpu.SemaphoreType.DMA((7,)),
                               pltpu.SemaphoreType.DMA((7,))])(pack)


def _pack_small(arrs):
    parts, sizes = [], []
    for a in arrs:
        f = a.reshape(-1).astype(F32)
        n = _round_up(f.shape[0], 8 * LANES)
        parts.append(jnp.pad(f, (0, n - f.shape[0])).reshape(n // LANES, LANES))
        sizes.append((a.shape, f.shape[0], n // LANES))
    return jnp.concatenate(parts, axis=0), sizes


def _unpack_small(pack, sizes):
    out, r0 = [], 0
    for shape, cnt, rows in sizes:
        out.append(pack[r0:r0 + rows].reshape(-1)[:cnt].reshape(shape))
        r0 += rows
    return out


def _ffn_fwd(xin, norm, weight, tag, gather_on_gateup, gather_on_down):
    hb = _rms_fwd(xin, norm, name=tag + "_norm")
    cargo, landed = gather_on_gateup()
    res = _gateup(hb, weight("w_gate_up"), name=tag + "_gateup", tm=1024, cargo=cargo)
    if cargo is not None:
        res, lands = res
        landed(lands)
    gate, up, act = res
    cargo, landed = gather_on_down()
    xout = _mm(act, weight("w_down"), res=xin, scale=0.5, name=tag + "_down", tm=1024, tn=512, tk=8192, cargo=cargo)
    if cargo is not None:
        xout, lands = xout
        landed(lands)
    return xout, (hb, gate, up, act)


def _ffn_bwd(dy, dyb, xin, norm, wgu, wd, saved, tag, scatter, gu_piece=(0, 1)):
    hb, gate, up, act = saved
    d_wd = _mm(act, dyb, mode="tn", scale=0.5, out_dtype=BF16, name=tag + "_dwd", tm=512, tn=1024, tk=4096)
    cargo = scatter(tag + "_w_down", d_wd)
    rest, rest_done = scatter.leftover()
    dgu = _dact_swiglu(dyb, wd, gate, up, name=tag + "_dact", cargo=rest)
    if rest is not None:
        dgu, lands = dgu
        rest_done(lands)
    d_wgu, lands = _mm(hb, dgu, mode="tn", halves="b", out_dtype=BF16, name=tag + "_dwgu", tm=1024, tn=512, tk=4096,
                       cargo=cargo)
    scatter.done(lands)
    cargo = scatter(tag + "_w_gate_up", d_wgu, piece=gu_piece)
    dh, lands = _mm(dgu, wgu, mode="nt", halves="a", out_dtype=BF16, name=tag + "_dh", tm=1024, tn=512, tk=5632,
                    cargo=cargo)
    scatter.done(lands)
    dx, dxb, d_norm = _rms_bwd(xin, dh, norm, dy, name=tag + "_dnorm")
    return dx, dxb, d_norm


def kernel(x, mem, ffn1_norm, ffn1_w_gate_up, ffn1_w_down, mix_norm, mem_norm, w_in, b_forget, pool_w, pool_scale, w_pool_up, fox_q_norm, fox_k_norm, w_fox_o, w_mem_kv, mem_q_norm, mem_k_norm, w_mem_o, w_out, ffn2_norm, ffn2_w_gate_up, ffn2_w_down, loss_target, m_ffn1_norm, m_ffn1_w_gate_up, m_ffn1_w_down, m_mix_norm, m_mem_norm, m_w_in, m_b_forget, m_pool_w, m_pool_scale, m_w_pool_up, m_fox_q_norm, m_fox_k_norm, m_w_fox_o, m_w_mem_kv, m_mem_q_norm, m_mem_k_norm, m_w_mem_o, m_w_out, m_ffn2_norm, m_ffn2_w_gate_up, m_ffn2_w_down, v_ffn1_norm, v_ffn1_w_gate_up, v_ffn1_w_down, v_mix_norm, v_mem_norm, v_w_in, v_b_forget, v_pool_w, v_pool_scale, v_w_pool_up, v_fox_q_norm, v_fox_k_norm, v_w_fox_o, v_w_mem_kv, v_mem_q_norm, v_mem_k_norm, v_w_mem_o, v_w_out, v_ffn2_norm, v_ffn2_w_gate_up, v_ffn2_w_down):
    P = dict(locals())
    big = ["ffn1_w_gate_up", "ffn1_w_down", "w_in", "w_pool_up", "w_fox_o", "w_mem_kv", "w_mem_o", "w_out",
           "ffn2_w_gate_up", "ffn2_w_down"]
    kinds = ["col", "row", "col", "col", "col", "row", "col", "row", "col", "row"]
    small = ["ffn1_norm", "mix_norm", "mem_norm", "b_forget", "pool_w", "pool_scale", "fox_q_norm", "fox_k_norm",
             "mem_q_norm", "mem_k_norm", "ffn2_norm"]
    order = ["ffn1_norm", "ffn1_w_gate_up", "ffn1_w_down", "mix_norm", "mem_norm", "w_in", "b_forget", "pool_w",
             "pool_scale", "w_pool_up", "fox_q_norm", "fox_k_norm", "w_fox_o", "w_mem_kv", "mem_q_norm", "mem_k_norm",
             "w_mem_o", "w_out", "ffn2_norm", "ffn2_w_gate_up", "ffn2_w_down"]

    nb, S, D = x.shape
    T = nb * S
    ML = mem.shape[1]
    NF, HD = b_forget.shape[-1], fox_q_norm.shape[-1]
    FW = NF * HD
    MW, MHD = w_mem_o.shape[1], mem_q_norm.shape[-1]
    PG, PD = pool_w.shape[1], pool_w.shape[2]
    PW = PG * PD
    n_in = w_in.shape[-1]
    n_in_pad = _round_up(n_in, LANES)
    in_w = 4 * n_in
    assert HD * 2 == LANES and MHD == LANES and PD == LANES and in_w == PW + 3 * FW + NF + MW + 3 * D

    shards = []
    for nme in big:
        wl = P[nme][0].astype(BF16)
        if nme == "w_in":
            wl = jnp.pad(wl, ((0, 0), (0, n_in_pad - n_in)))
        shards.append(wl)
    shard, kind_of, full = dict(zip(big, shards)), dict(zip(big, kinds)), {}

    def landed(names):
        def fill(lands):
            for nme, l in zip(names, lands):
                full[nme] = _ag_fill(l, shard[nme], kind_of[nme], name="ag_fill_" + nme)
        return fill

    def gather(names, piece=(0, 1), lands=None):
        return _ag_cargo([shard[nme] for nme in names], [kind_of[nme] for nme in names], piece, lands)

    landed(["ffn1_w_gate_up"])(_run_cargo(gather(["ffn1_w_gate_up"]), name="ag_first"))
    begun = {}

    def on_gateup1():
        def fill(lands):
            landed(["ffn1_w_down"])(lands[:1])
            begun["w_in"] = lands[1]
        return _join_cargo(gather(["ffn1_w_down"]), gather(["w_in"], (0, 2))), fill

    x0 = x.reshape(T, D)
    x1, sv1 = _ffn_fwd(x0, ffn1_norm, lambda nme: full["ffn1_" + nme], "ffn1", on_gateup1,
                       lambda: (gather(["w_in"], (1, 2), [begun["w_in"]]), landed(["w_in"])))
    win_o = jnp.concatenate([full["w_in"][:, j * n_in_pad:j * n_in_pad + n_in] for j in range(4)], axis=1)
    o_q, o_k, o_v, o_f = PW, PW + FW, PW + 2 * FW, PW + 3 * FW
    o_qm, o_gate = o_f + NF, o_f + NF + MW
    p_q, p_k, p_v, p_pool, p_qm, p_f = 3 * D, 3 * D + FW, 3 * D + 2 * FW, 3 * D + 3 * FW, 3 * D + 3 * FW + PW, 3 * D + 3 * FW + PW + MW
    inp = _round_up(p_f + LANES, 512)
    win_p = jnp.concatenate([win_o[:, o_gate:], win_o[:, o_q:o_f], win_o[:, :PW], win_o[:, o_qm:o_gate],
                             win_o[:, o_f:o_qm], jnp.zeros((D, inp - p_f - NF), BF16)], axis=1)

    hb2 = _rms_fwd(x1, mix_norm, name="mix_norm")
    later = ["ffn2_w_down", "w_fox_o", "w_mem_kv", "w_pool_up"]
    z, lands = _mm(hb2, win_p, name="in_proj", tm=1024, tn=1536, cargo=gather(later))
    landed(later)(lands)

    vpool = _pool_fwd(z, p_pool // LANES, pool_w[0], pool_scale, nb, S, name="pool_fwd")
    y_pool = _mm(vpool, full["w_pool_up"], out_dtype=BF16, name="pool_up", tm=1024, tn=512)

    b128 = jnp.pad(b_forget, ((0, 0), (0, LANES - NF)))
    c3 = _fox_c(z, p_f // LANES, b128, nb, S, NF, name="fox_c")
    later = ["w_out", "w_mem_o"]
    (qa, ka, vb), lands = _fox_qk(z, p_q // LANES, p_k // LANES, p_v // LANES, c3, fox_q_norm, fox_k_norm, HD, NF,
                                  name="fox_qk", cargo=gather(later))
    landed(later)(lands)
    (o_fox, qb), lands = _fox_fwd(qa, ka, vb, nb, S, HD, name="fox_fwd",
                                  cargo=gather(["ffn2_w_gate_up"], (0, 4, 3)))
    begun["ffn2_w_gate_up"] = lands[0]
    y_fox = _mm(o_fox, full["w_fox_o"], out_dtype=BF16, name="fox_out", tm=1024, tn=512)

    mem2 = mem.reshape(nb * ML, D)
    memn = _rms_fwd(mem2, mem_norm, name="mem_norm")
    kv = _mm(memn, full["w_mem_kv"], name="mem_kv")
    kmn = _headnorm_fwd(kv, 0, MW, mem_k_norm, MHD, name="mem_knorm")
    vmb = kv[:, MW:].astype(BF16)
    qmn = _headnorm_fwd(z, p_qm // LANES, MW, mem_q_norm, MHD, name="mem_qnorm")
    o_mem = _mem_fwd(qmn, kmn, vmb, nb, S, ML, name="mem_fwd")
    y_mem = _mm(o_mem, full["w_mem_o"], out_dtype=BF16, name="mem_out", tm=1024, tn=512)

    merged, lands = _merge_fwd(z, y_pool, y_fox, y_mem, name="merge_fwd",
                               cargo=gather(["ffn2_w_gate_up"], (3, 4), [begun["ffn2_w_gate_up"]]))
    landed(["ffn2_w_gate_up"])(lands)
    x2 = _mm(merged, full["w_out"], res=x1, name="out_proj", tm=1024, tn=512)
    x3, sv2 = _ffn_fwd(x2, ffn2_norm, lambda nme: full["ffn2_" + nme], "ffn2", lambda: (None, None),
                       lambda: (None, None))
    dy3, dy3b, loss_part = _loss(x3, loss_target.reshape(T, D), name="loss")

    g, parts, got = {}, {}, {}
    rs_kind = dict(kind_of, w_in="row")

    leftovers = []

    def scatter(*named_grads, piece=(0, 1)):
        names = named_grads[0::2]
        for nme, grad in zip(names, named_grads[1::2]):
            parts[nme] = _rs_pair_sum(grad, rs_kind[nme], name="rs_pair_" + nme)
        scatter.names = names
        if piece[1] == 2:
            leftovers.append(names)
        return _scatter_cargo([parts[nme] for nme in names], [rs_kind[nme] for nme in names], piece)

    def scattered(lands):
        got.update(zip(scatter.names, lands))

    def leftover():
        if not leftovers:
            return None, None
        names = leftovers.pop(0)
        cargo = _scatter_cargo([parts[nme] for nme in names], [rs_kind[nme] for nme in names], (1, 2),
                               [got[nme] for nme in names])
        return cargo, lambda lands: got.update(zip(names, lands))

    scatter.done, scatter.leftover = scattered, leftover
    dx2, dx2b, g["ffn2_norm"] = _ffn_bwd(dy3, dy3b, x2, ffn2_norm, full["ffn2_w_gate_up"], full["ffn2_w_down"], sv2,
                                         "ffn2", scatter, gu_piece=(0, 2))

    dmerged = _mm(dx2b, full["w_out"], mode="nt", out_dtype=BF16, name="d_merged", tm=1024, tn=512)
    g["w_out"] = _mm(merged, dx2b, mode="tn", out_dtype=BF16, name="d_w_out", tk=4096)
    dyp, dyf, dym, dgl = _merge_bwd(dmerged, z, y_pool, y_fox, y_mem, name="merge_bwd")

    g["w_pool_up"] = _mm(vpool, dyp, mode="tn", out_dtype=BF16, name="d_w_pool_up", tk=4096)
    dvp = _mm(dyp, full["w_pool_up"], mode="nt", name="d_vpool", tm=1024)
    du_pool, g["pool_w"], g["pool_scale"] = _pool_bwd(z, p_pool // LANES, dvp, pool_w[0], pool_scale, nb, S,
                                                      name="pool_bwd")

    g["w_fox_o"] = _mm(o_fox, dyf, mode="tn", out_dtype=BF16, name="d_w_fox_o", tk=4096)
    do_fox = _mm(dyf, full["w_fox_o"], mode="nt", out_dtype=BF16, name="d_o_fox", tm=1024)
    rest, rest_done = leftover()
    (dqa, dka, dvf, dcs), lands = _fox_bwd(qb, ka, vb, do_fox, nb, S, HD, name="fox_bwd", cargo=rest)
    rest_done(lands)
    dq, dgq = _headnorm_bwd(z, p_q // LANES, FW, dqa, fox_q_norm, HD, HD ** -0.5, True, BF16, name="fox_dq")
    dk, dgk = _headnorm_bwd(z, p_k // LANES, FW, dka, fox_k_norm, HD, 1.0, True, BF16, name="fox_dk")
    g["fox_q_norm"], g["fox_k_norm"] = _fold_gain(dgq, HD), _fold_gain(dgk, HD)
    dc = -dcs[:, :, :LANES // HD, :].reshape(nb, NF, S).transpose(0, 2, 1).reshape(T, NF)
    df, db = _fox_c_bwd(jnp.pad(dc, ((0, 0), (0, LANES - NF))), z, p_f // LANES, b128, nb, S, name="fox_c_bwd")
    g["b_forget"] = db[:, :NF]

    g["w_mem_o"] = _mm(o_mem, dym, mode="tn", out_dtype=BF16, name="d_w_mem_o", tk=4096)
    do_mem = _mm(dym, full["w_mem_o"], mode="nt", out_dtype=BF16, name="d_o_mem", tm=1024)
    dqmn, dkmn, dvm = _mem_bwd(qmn, kmn, vmb, do_mem, nb, S, ML, name="mem_bwd")
    dqm, g["mem_q_norm"] = _headnorm_bwd(z, p_qm // LANES, MW, dqmn, mem_q_norm, MHD, 1.0, False, BF16, name="mem_dq")
    dkm, g["mem_k_norm"] = _headnorm_bwd(kv, 0, MW, dkmn, mem_k_norm, MHD, 1.0, False, BF16, name="mem_dk")
    dkv = jnp.concatenate([dkm, dvm.astype(BF16)], axis=1)
    g["w_mem_kv"] = _mm(memn, dkv, mode="tn", out_dtype=BF16, name="d_w_mem_kv")
    dmemn = _mm(dkv, full["w_mem_kv"], mode="nt", name="d_memn")
    _, _, g["mem_norm"] = _rms_bwd(mem2, dmemn, mem_norm, jnp.zeros_like(mem2), name="mem_dnorm")

    dz = jnp.concatenate([dgl, dq, dk, dvf.astype(BF16), du_pool, dqm, df.astype(BF16),
                          jnp.zeros((T, inp - p_f - LANES), BF16)], axis=1)
    cargo = scatter(*sum([[nme, g[nme]] for nme in ("w_out", "w_pool_up", "w_fox_o", "w_mem_o", "w_mem_kv")], []))
    dwin_p, lands = _mm(dz, hb2, mode="tn", out_dtype=BF16, name="d_w_in", tm=512, tn=1024, tk=4096, cargo=cargo)
    scattered(lands)
    dwin_o = jnp.concatenate([dwin_p[p_pool:p_qm], dwin_p[p_q:p_pool], dwin_p[p_f:p_f + NF], dwin_p[p_qm:p_f],
                              dwin_p[:p_q]], axis=0)
    cargo = scatter("w_in", jnp.concatenate(
        [jnp.pad(dwin_o[j * n_in:(j + 1) * n_in], ((0, n_in_pad - n_in), (0, 0))) for j in range(4)], axis=0),
        piece=(0, 2))
    dh2, lands = _mm(dz, win_p, mode="nt", out_dtype=BF16, name="d_h2", tm=1024, tn=512, tk=3584, cargo=cargo)
    scattered(lands)
    dx1, dx1b, g["mix_norm"] = _rms_bwd(x1, dh2, mix_norm, dx2, name="mix_dnorm")

    dx0, _, g["ffn1_norm"] = _ffn_bwd(dx1, dx1b, x0, ffn1_norm, full["ffn1_w_gate_up"], full["ffn1_w_down"], sv1,
                                      "ffn1", scatter)

    halves = {nme: _rs_sum_join(parts[nme], got[nme], rs_kind[nme], name="rs_join_" + nme) for nme in big}
    gfull = {}

    gpack, sizes = _pack_small([g[nme].reshape(P[nme].shape) for nme in small] + [loss_part[:, :1]])
    gsum = _unpack_small(_all_reduce_small(gpack), sizes)
    loss = gsum[-1].reshape(())
    for nme, a in zip(small, gsum[:-1]):
        gfull[nme] = a

    delta, new_m, new_v = {}, {}, {}
    for nme in big:
        shp = P[nme].shape
        two = (lambda a: a.reshape(shp[-2:]).T) if nme == "w_in" else (lambda a: a.reshape(shp[-2:]))
        back = (lambda a: a.T.reshape(shp)) if nme == "w_in" else (lambda a: a.reshape(shp))
        gfull[nme], delta[nme], new_m[nme], new_v[nme] = [
            back(a) for a in _adamw_join(two(P[nme]), *halves[nme], two(P["m_" + nme]), two(P["v_" + nme]),
                                         name="adamw_" + nme)]
    wpack, _ = _pack_small([P[nme] for nme in small])
    mpack, _ = _pack_small([P["m_" + nme] for nme in small])
    vpack, _ = _pack_small([P["v_" + nme] for nme in small])
    gpack2, ssz = _pack_small([gfull[nme] for nme in small])
    for dct, pk in zip((delta, new_m, new_v), _adamw(wpack, gpack2, mpack, vpack, name="adamw_small")):
        for nme, a in zip(small, _unpack_small(pk, ssz)):
            dct[nme] = a

    grad_x = dx0.reshape(nb, S, D)
    return (loss, grad_x, *[gfull[nme] for nme in order], *[delta[nme] for nme in order],
            *[new_m[nme] for nme in order], *[new_v[nme] for nme in order])
```

```python
import functools

import jax
import jax.numpy as jnp
from jax import lax
from jax.experimental import pallas as pl
from jax.experimental.pallas import tpu as pltpu

F32 = jnp.float32
BF16 = jnp.bfloat16
MESH = pl.DeviceIdType.MESH
EPS = 1e-6
POOL_WINDOWS = (2, 4, 8, 16)
ADAM_LR, ADAM_B1, ADAM_B2, ADAM_EPS, ADAM_WD, ADAM_STEP = 0.001, 0.9, 0.999, 1e-08, 0.01, 10
LANES = 128
VMEM_LIMIT = 56 * 1024 * 1024
NEG = -1e30


def _pc(body, *, name, cargo=None, **kw):
    if cargo is not None:
        body, kw = _load_cargo(body, cargo, kw)
    return pl.pallas_call(body, name=name, **kw)


def _cp(*sem):
    return pltpu.CompilerParams(dimension_semantics=sem, vmem_limit_bytes=VMEM_LIMIT)


def _tile(n, pref, mult=LANES):
    if n <= pref:
        return n
    t = (pref // mult) * mult
    while t >= mult:
        if n % t == 0:
            return t
        t -= mult
    return n


def _round_up(n, m):
    return (n + m - 1) // m * m


_DIMS = {"nn": (((1,), (0,)), ((), ())), "nt": (((1,), (1,)), ((), ())), "tn": (((0,), (0,)), ((), ()))}


def _split_cargo(res, cargo):
    if cargo is None:
        return res
    nco = len(cargo.out_shapes)
    own = res[:len(res) - nco]
    return (own[0] if len(own) == 1 else own), list(res[len(res) - nco:])


def _mm(a, b, *, name, mode="nn", out_dtype=F32, scale=1.0, res=None, tm=512, tn=512, tk=2048, cargo=None,
        halves=None):
    if halves == "a":
        assert mode == "nt"
        (_, M, Kh), (N, K2) = a.shape, b.shape
        K = 2 * Kh
    elif halves == "b":
        assert mode == "tn"
        (K, M), (_, K2, Nh) = a.shape, b.shape
        N = 2 * Nh
    elif mode == "nn":
        (M, K), (K2, N) = a.shape, b.shape
    elif mode == "nt":
        (M, K), (N, K2) = a.shape, b.shape
    else:
        (K, M), (K2, N) = a.shape, b.shape
    assert K == K2, (name, a.shape, b.shape)
    tm = _tile(M, tm, 8 if M % LANES else LANES)
    tn = _tile(N // 2 if halves == "b" else N, tn)
    tk = _tile(K // 2 if halves == "a" else K, tk)
    nk = K // tk
    dims = _DIMS[mode]
    if halves == "a":
        a_spec = pl.BlockSpec((None, tm, tk), lambda i, j, k: (k // (nk // 2), i, k % (nk // 2)))
    elif mode == "tn":
        a_spec = pl.BlockSpec((tk, tm), lambda i, j, k: (k, i))
    else:
        a_spec = pl.BlockSpec((tm, tk), lambda i, j, k: (i, k))
    if halves == "b":
        njh = N // 2 // tn
        b_spec = pl.BlockSpec((None, tk, tn), lambda i, j, k: (j // njh, k, j % njh))
    elif mode == "nt":
        b_spec = pl.BlockSpec((tn, tk), lambda i, j, k: (j, k))
    else:
        b_spec = pl.BlockSpec((tk, tn), lambda i, j, k: (k, j))
    o_spec = pl.BlockSpec((tm, tn), lambda i, j, k: (i, j))
    has_res = res is not None

    def body(*refs):
        a_ref, b_ref = refs[:2]
        r_ref = refs[2] if has_res else None
        o_ref = refs[3] if has_res else refs[2]

        def finish(v):
            v = v * scale
            if has_res:
                v = r_ref[...] + v
            o_ref[...] = v.astype(o_ref.dtype)

        prod = lax.dot_general(a_ref[...], b_ref[...], dims, preferred_element_type=F32)
        if nk == 1:
            finish(prod)
            return
        acc, k = refs[-1], pl.program_id(2)

        @pl.when(k == 0)
        def _():
            acc[...] = prod

        @pl.when(k > 0)
        def _():
            acc[...] += prod

        @pl.when(k == nk - 1)
        def _():
            finish(acc[...])

    out = _pc(body, name=name, cargo=cargo, out_shape=jax.ShapeDtypeStruct((M, N), out_dtype),
              grid=(M // tm, N // tn, nk),
              in_specs=[a_spec, b_spec] + ([o_spec] if has_res else []), out_specs=o_spec,
              scratch_shapes=[] if nk == 1 else [pltpu.VMEM((tm, tn), F32)],
              compiler_params=_cp("parallel", "parallel", "arbitrary"))(
                  *((a, b, res) if has_res else (a, b)), *(cargo.arrays if cargo else ()))
    return _split_cargo(out, cargo)


def _dact_swiglu(dyb, wd, gate, up, *, name, tm=1024, tn=512, cargo=None):
    M, K = dyb.shape
    Fh = wd.shape[0]
    tm, tn = _tile(M, tm), _tile(Fh, tn)

    rows = _tile(tm, 256, 16)

    def body(dy_ref, w_ref, g_ref, u_ref, o_ref):
        w = w_ref[...]
        for r in range(0, tm, rows):
            sl = slice(r, r + rows)
            d = lax.dot_general(dy_ref[sl, :], w, _DIMS["nt"], preferred_element_type=F32) * 0.5
            g, u = g_ref[sl, :].astype(F32), u_ref[sl, :].astype(F32)
            s = jax.nn.sigmoid(g)
            o_ref[0, sl, :] = (d * u * (s * (1.0 + g * (1.0 - s)))).astype(BF16)
            o_ref[1, sl, :] = (d * (g * s)).astype(BF16)

    t_spec = pl.BlockSpec((tm, tn), lambda i, j: (i, j))
    out = _pc(body, name=name, cargo=cargo, out_shape=jax.ShapeDtypeStruct((2, M, Fh), BF16),
              grid=(M // tm, Fh // tn),
              in_specs=[pl.BlockSpec((tm, K), lambda i, j: (i, 0)), pl.BlockSpec((tn, K), lambda i, j: (j, 0)),
                        t_spec, t_spec],
              out_specs=pl.BlockSpec((2, tm, tn), lambda i, j: (0, i, j)),
              compiler_params=_cp("parallel", "parallel"))(dyb, wd, gate, up, *(cargo.arrays if cargo else ()))
    return _split_cargo(out, cargo)


def _gateup(h, wgu, *, name, tm=512, tn=512, cargo=None):
    M, K = h.shape
    Fh = wgu.shape[1] // 2
    tm, tn = _tile(M, tm), _tile(Fh, tn)
    nj = Fh // tn

    def body(h_ref, wg_ref, wu_ref, g_ref, u_ref, a_ref):
        hv = h_ref[...]
        g = jnp.dot(hv, wg_ref[...], preferred_element_type=F32)
        u = jnp.dot(hv, wu_ref[...], preferred_element_type=F32)
        g_ref[...] = g.astype(BF16)
        u_ref[...] = u.astype(BF16)
        a_ref[...] = (g * jax.nn.sigmoid(g) * u).astype(BF16)

    o_spec = pl.BlockSpec((tm, tn), lambda i, j: (i, j))
    shp = jax.ShapeDtypeStruct((M, Fh), BF16)
    out = _pc(body, name=name, cargo=cargo, out_shape=(shp, shp, shp), grid=(M // tm, nj),
              in_specs=[pl.BlockSpec((tm, K), lambda i, j: (i, 0)),
                        pl.BlockSpec((K, tn), lambda i, j: (0, j)),
                        pl.BlockSpec((K, tn), lambda i, j: (0, j + nj))],
              out_specs=(o_spec, o_spec, o_spec),
              compiler_params=_cp("parallel", "parallel"))(h, wgu, wgu, *(cargo.arrays if cargo else ()))
    return _split_cargo(out, cargo)


def _rowmap(fn, ins, outs, *, rows, tm, name, accs=(), cargo=None):
    tm = _tile(rows, tm, 8)
    arrs, in_specs = [], []
    for d in ins:
        if d[0] == "row":
            _, a, w, cb = d
            w = a.shape[1] if w is None else w
            in_specs.append(pl.BlockSpec((tm, w), functools.partial(lambda i, cb: (i, cb), cb=cb)))
        else:
            a = d[1]
            in_specs.append(pl.BlockSpec(a.shape, functools.partial(lambda i, nd: (0,) * nd, nd=a.ndim)))
        arrs.append(a)
    out_shape = [jax.ShapeDtypeStruct((rows, w), dt) for w, dt in outs]
    out_specs = [pl.BlockSpec((tm, w), lambda i: (i, 0)) for w, _ in outs]
    for shp, dt in accs:
        out_shape.append(jax.ShapeDtypeStruct(shp, dt))
        out_specs.append(pl.BlockSpec(shp, functools.partial(lambda i, nd: (0,) * nd, nd=len(shp))))
    nin, nout, nacc = len(ins), len(outs), len(accs)

    def body(*refs):
        res = fn(*[r[...] for r in refs[:nin]])
        res = res if isinstance(res, (tuple, list)) else (res,)
        for r, v in zip(refs[nin:nin + nout], res[:nout]):
            r[...] = v.astype(r.dtype)
        if nacc:
            acc_refs = refs[nin + nout:]

            @pl.when(pl.program_id(0) == 0)
            def _():
                for r in acc_refs:
                    r[...] = jnp.zeros_like(r)

            for r, v in zip(acc_refs, res[nout:]):
                r[...] += v

    out = _pc(body, name=name, cargo=cargo, out_shape=tuple(out_shape), grid=(rows // tm,), in_specs=in_specs,
              out_specs=tuple(out_specs),
              compiler_params=_cp("arbitrary" if nacc else "parallel"))(*arrs, *(cargo.arrays if cargo else ()))
    return _split_cargo(out, cargo) if cargo else (out[0] if len(out) == 1 else out)


def _rms_fwd(x, g, *, name):
    def fn(xv, gv):
        return xv * lax.rsqrt(jnp.mean(xv * xv, axis=-1, keepdims=True) + EPS) * gv
    return _rowmap(fn, [("row", x, None, 0), ("bc", g)], [(x.shape[1], BF16)], rows=x.shape[0], tm=256, name=name)


def _rms_bwd(x, dh, g, dres, *, name, cargo=None):
    D = x.shape[1]

    def fn(xv, dhv, gv, drv):
        r = lax.rsqrt(jnp.mean(xv * xv, axis=-1, keepdims=True) + EPS)
        xh = xv * r
        dhf = dhv.astype(F32)
        dxn = dhf * gv
        dx = drv + r * (dxn - xh * jnp.mean(dxn * xh, axis=-1, keepdims=True))
        return dx, dx, jnp.sum(dhf * xh, axis=0, keepdims=True)

    return _rowmap(fn, [("row", x, None, 0), ("row", dh, None, 0), ("bc", g), ("row", dres, None, 0)],
                   [(D, F32), (D, BF16)], rows=x.shape[0], tm=256, name=name, accs=[((1, D), F32)], cargo=cargo)


def _loss(y, tgt, *, name):
    D = y.shape[1]

    def fn(yv, tv):
        e = yv - tv
        d = e * (1.0 / D)
        part = 0.5 * jnp.sum(jnp.sum(e * e, axis=-1, keepdims=True) * (1.0 / D), axis=0, keepdims=True)
        return d, d, jnp.broadcast_to(part, (1, LANES))

    return _rowmap(fn, [("row", y, None, 0), ("row", tgt, None, 0)], [(D, F32), (D, BF16)],
                   rows=y.shape[0], tm=256, name=name, accs=[((1, LANES), F32)])


def _merge_fwd(z, yp, yf, ym, *, name, cargo=None):
    D = yp.shape[1]

    def fn(gp, gf, gm, a, b, c):
        return jax.nn.sigmoid(gp) * a + jax.nn.sigmoid(gf) * b + jax.nn.sigmoid(gm) * c

    return _rowmap(fn, [("row", z, D, 0), ("row", z, D, 1), ("row", z, D, 2),
                        ("row", yp, None, 0), ("row", yf, None, 0), ("row", ym, None, 0)],
                   [(D, BF16)], rows=yp.shape[0], tm=128, name=name, cargo=cargo)


def _merge_bwd(dm, z, yp, yf, ym, *, name):
    D = yp.shape[1]

    def fn(d, gp, gf, gm, a, b, c):
        outs, dgl = [], []
        for gl, yv in ((gp, a), (gf, b), (gm, c)):
            s = jax.nn.sigmoid(gl)
            outs.append(d * s)
            dgl.append(d * yv * (s * (1.0 - s)))
        return (*outs, jnp.concatenate(dgl, axis=-1))

    return _rowmap(fn, [("row", dm, None, 0), ("row", z, D, 0), ("row", z, D, 1), ("row", z, D, 2),
                        ("row", yp, None, 0), ("row", yf, None, 0), ("row", ym, None, 0)],
                   [(D, BF16), (D, BF16), (D, BF16), (3 * D, BF16)], rows=yp.shape[0], tm=128, name=name)


def _adamw(w, g, m, v, *, name):
    C = w.shape[1]

    def fn(wv, gv, mv, vv):
        m2 = ADAM_B1 * mv + (1.0 - ADAM_B1) * gv
        v2 = ADAM_B2 * vv + (1.0 - ADAM_B2) * (gv * gv)
        m_hat = m2 / (1.0 - ADAM_B1 ** ADAM_STEP)
        v_hat = v2 / (1.0 - ADAM_B2 ** ADAM_STEP)
        return -ADAM_LR * (m_hat / (jnp.sqrt(v_hat) + ADAM_EPS) + ADAM_WD * wv), m2, v2

    tm = max(8, (262144 // C) // 8 * 8)
    return _rowmap(fn, [("row", a, None, 0) for a in (w, g, m, v)], [(C, F32)] * 3,
                   rows=w.shape[0], tm=tm, name=name)


def _seg_mean(v, hd):
    if hd == LANES:
        return jnp.mean(v, axis=-1, keepdims=True)
    lo = lax.broadcasted_iota(jnp.int32, v.shape, 1) < hd
    s0 = jnp.sum(jnp.where(lo, v, 0.0), axis=-1, keepdims=True)
    s1 = jnp.sum(jnp.where(lo, 0.0, v), axis=-1, keepdims=True)
    return jnp.where(lo, s0, s1) * (1.0 / hd)


def _gain128(g, hd):
    g = g.reshape(1, hd).astype(F32)
    return g if hd == LANES else jnp.concatenate([g, g], axis=-1)


def _headnorm_fwd(x, cb0, width, gain, hd, *, name):
    rows = x.shape[0]
    tm = _tile(rows, 512, 8)
    g128 = _gain128(gain, hd)

    def body(x_ref, g_ref, o_ref):
        xv = x_ref[...]
        o_ref[...] = (xv * lax.rsqrt(_seg_mean(xv * xv, hd) + EPS) * g_ref[...]).astype(BF16)

    return _pc(body, name=name, out_shape=jax.ShapeDtypeStruct((rows, width), BF16),
               grid=(rows // tm, width // LANES),
               in_specs=[pl.BlockSpec((tm, LANES), lambda i, j: (i, cb0 + j)),
                         pl.BlockSpec((1, LANES), lambda i, j: (0, 0))],
               out_specs=pl.BlockSpec((tm, LANES), lambda i, j: (i, j)),
               compiler_params=_cp("parallel", "parallel"))(x, g128)


def _headnorm_bwd(x, cb0, width, dy, gain, hd, post, spread, out_dtype, *, name):
    rows = x.shape[0]
    tm = _tile(rows, 512, 8)
    g128 = _gain128(gain, hd)
    dyw = 2 * LANES if spread else LANES

    def body(x_ref, dy_ref, g_ref, dx_ref, dg_ref):
        xv = x_ref[...]
        dyv = dy_ref[...]
        if spread:
            dyv = jnp.concatenate([dyv[:, :hd], dyv[:, LANES:LANES + hd]], axis=-1)
        dyv = dyv * post
        r = lax.rsqrt(_seg_mean(xv * xv, hd) + EPS)
        xh = xv * r
        dxn = dyv * g_ref[...]
        dx_ref[...] = (r * (dxn - xh * _seg_mean(dxn * xh, hd))).astype(dx_ref.dtype)

        @pl.when((pl.program_id(0) == 0) & (pl.program_id(1) == 0))
        def _():
            dg_ref[...] = jnp.zeros_like(dg_ref)

        dg_ref[...] += jnp.sum(dyv * xh, axis=0, keepdims=True)

    return _pc(body, name=name,
               out_shape=(jax.ShapeDtypeStruct((rows, width), out_dtype), jax.ShapeDtypeStruct((1, LANES), F32)),
               grid=(rows // tm, width // LANES),
               in_specs=[pl.BlockSpec((tm, LANES), lambda i, j: (i, cb0 + j)),
                         pl.BlockSpec((tm, dyw), lambda i, j: (i, j)),
                         pl.BlockSpec((1, LANES), lambda i, j: (0, 0))],
               out_specs=(pl.BlockSpec((tm, LANES), lambda i, j: (i, j)),
                          pl.BlockSpec((1, LANES), lambda i, j: (0, 0))),
               compiler_params=_cp("arbitrary", "arbitrary"))(x, dy, g128)


def _fold_gain(dg, hd):
    return dg if hd == LANES else dg[:, :hd] + dg[:, hd:]


def _shift_down(v, k, row):
    return jnp.where(row >= k, pltpu.roll(v, k, 0), 0.0)


def _shift_up(v, k, row, S):
    return jnp.where(row < S - k, pltpu.roll(v, S - k, 0), 0.0)


def _window_sum(v, g, row, shift):
    s, outs = v, []
    for k in (1, 2, 4, 8):
        s = s + shift(s, k)
        outs.append(s)
    return jnp.where(g == 0, outs[0], jnp.where(g == 1, outs[1], jnp.where(g == 2, outs[2], outs[3])))


def _pool_fwd(z, cb0, pool_w, pool_scale, nb, S, *, name):
    PG, PD = pool_w.shape[0], pool_w.shape[1]
    assert PD == LANES and PG <= len(POOL_WINDOWS)
    pw = pool_w.astype(BF16)

    def body(u_ref, w_ref, s_ref, o_ref):
        g = pl.program_id(0)
        u = u_ref[...]
        row = lax.broadcasted_iota(jnp.int32, u.shape, 0)
        cnt = jnp.minimum(row + 1, jnp.left_shift(2, g)).astype(F32)
        pooled = _window_sum(u, g, row, lambda v, k: _shift_down(v, k, row)) / cnt
        mixed = jnp.dot((pooled - u).astype(BF16), w_ref[...], preferred_element_type=F32)
        o_ref[...] = (mixed * s_ref[...]).astype(BF16)

    return _pc(body, name=name, out_shape=jax.ShapeDtypeStruct((nb * S, PG * PD), BF16), grid=(PG, nb),
               in_specs=[pl.BlockSpec((S, PD), lambda g, b: (b, cb0 + g)),
                         pl.BlockSpec((None, PD, PD), lambda g, b: (g, 0, 0)),
                         pl.BlockSpec((1, PD), lambda g, b: (0, g))],
               out_specs=pl.BlockSpec((S, PD), lambda g, b: (b, g)),
               compiler_params=_cp("parallel", "parallel"))(z, pw, pool_scale)


def _pool_bwd(z, cb0, dv, pool_w, pool_scale, nb, S, *, name):
    PG, PD = pool_w.shape[0], pool_w.shape[1]
    pw = pool_w.astype(BF16)

    def body(u_ref, dv_ref, w_ref, s_ref, du_ref, dw_ref, ds_ref):
        g, b = pl.program_id(0), pl.program_id(1)
        u, dvv, w = u_ref[...], dv_ref[...], w_ref[...]
        row = lax.broadcasted_iota(jnp.int32, u.shape, 0)
        cnt = jnp.minimum(row + 1, jnp.left_shift(2, g)).astype(F32)
        diff = (_window_sum(u, g, row, lambda v, k: _shift_down(v, k, row)) / cnt - u).astype(BF16)
        mixed = jnp.dot(diff, w, preferred_element_type=F32)
        dmixed = (dvv * s_ref[...]).astype(BF16)
        ddiff = lax.dot_general(dmixed, w, _DIMS["nt"], preferred_element_type=F32)
        du_ref[...] = (_window_sum(ddiff / cnt, g, row, lambda v, k: _shift_up(v, k, row, S)) - ddiff).astype(BF16)

        @pl.when(b == 0)
        def _():
            dw_ref[...] = jnp.zeros_like(dw_ref)
            ds_ref[...] = jnp.zeros_like(ds_ref)

        dw_ref[...] += lax.dot_general(diff, dmixed, _DIMS["tn"], preferred_element_type=F32)
        ds_ref[...] += jnp.sum(dvv * mixed, axis=0, keepdims=True)

    return _pc(body, name=name,
               out_shape=(jax.ShapeDtypeStruct((nb * S, PG * PD), BF16), jax.ShapeDtypeStruct((PG, PD, PD), F32),
                          jax.ShapeDtypeStruct((1, PG * PD), F32)),
               grid=(PG, nb),
               in_specs=[pl.BlockSpec((S, PD), lambda g, b: (b, cb0 + g)),
                         pl.BlockSpec((S, PD), lambda g, b: (b, g)),
                         pl.BlockSpec((None, PD, PD), lambda g, b: (g, 0, 0)),
                         pl.BlockSpec((1, PD), lambda g, b: (0, g))],
               out_specs=(pl.BlockSpec((S, PD), lambda g, b: (b, g)),
                          pl.BlockSpec((None, PD, PD), lambda g, b: (g, 0, 0)),
                          pl.BlockSpec((1, PD), lambda g, b: (0, g))),
               compiler_params=_cp("parallel", "arbitrary"))(z, dv, pw, pool_scale)


def _split3(c):
    hi = c.astype(BF16)
    r1 = c - hi.astype(F32)
    mid = r1.astype(BF16)
    lo = (r1 - mid.astype(F32)).astype(BF16)
    return hi, mid, lo


def _log_sigmoid(v):
    return jnp.minimum(v, 0.0) - jnp.log(1.0 + jnp.exp(-jnp.abs(v)))


def _fox_c(z, cb, b128, nb, S, NF, *, name):
    def body(f_ref, b_ref, hi_ref, mid_ref, lo_ref):
        v = f_ref[...] + b_ref[...]
        lane = lax.broadcasted_iota(jnp.int32, v.shape, 1)
        row = lax.broadcasted_iota(jnp.int32, v.shape, 0)
        c = jnp.where(lane < NF, _log_sigmoid(v), 0.0)
        k = 1
        while k < S:
            c = c + _shift_down(c, k, row)
            k *= 2
        hi, mid, lo = _split3(c)
        hi_ref[...], mid_ref[...], lo_ref[...] = hi, mid, lo

    shp = jax.ShapeDtypeStruct((nb * S, LANES), BF16)
    spec = pl.BlockSpec((S, LANES), lambda i: (i, 0))
    return _pc(body, name=name, out_shape=(shp, shp, shp), grid=(nb,),
               in_specs=[pl.BlockSpec((S, LANES), lambda i: (i, cb)), pl.BlockSpec((1, LANES), lambda i: (0, 0))],
               out_specs=(spec, spec, spec), compiler_params=_cp("parallel"))(z, b128)


def _fox_c_bwd(dc, z, cb, b128, nb, S, *, name):
    def body(dc_ref, f_ref, b_ref, df_ref, db_ref):
        d = dc_ref[...]
        row = lax.broadcasted_iota(jnp.int32, d.shape, 0)
        k = 1
        while k < S:
            d = d + _shift_up(d, k, row, S)
            k *= 2
        df = d * jax.nn.sigmoid(-(f_ref[...] + b_ref[...]))
        df_ref[...] = df

        @pl.when(pl.program_id(0) == 0)
        def _():
            db_ref[...] = jnp.zeros_like(db_ref)

        db_ref[...] += jnp.sum(df, axis=0, keepdims=True)

    return _pc(body, name=name,
               out_shape=(jax.ShapeDtypeStruct((nb * S, LANES), F32), jax.ShapeDtypeStruct((1, LANES), F32)),
               grid=(nb,),
               in_specs=[pl.BlockSpec((S, LANES), lambda i: (i, 0)), pl.BlockSpec((S, LANES), lambda i: (i, cb)),
                         pl.BlockSpec((1, LANES), lambda i: (0, 0))],
               out_specs=(pl.BlockSpec((S, LANES), lambda i: (i, 0)), pl.BlockSpec((1, LANES), lambda i: (0, 0))),
               compiler_params=_cp("arbitrary"))(dc, z, b128)


def _fox_qk(z, cbq, cbk, cbv, c3, gq, gk, HD, NF, *, name, cargo=None):
    T = z.shape[0]
    HP = NF * HD // LANES
    tm = _tile(T, 512, 8)
    scale = HD ** -0.5
    gq128, gk128 = _gain128(gq, HD), _gain128(gk, HD)

    def body(q_ref, k_ref, v_ref, hi_ref, mid_ref, lo_ref, gq_ref, gk_ref, qa_ref, ka_ref, vb_ref):
        hp = pl.program_id(1)
        q, k = q_ref[...], k_ref[...]
        qn = q * lax.rsqrt(_seg_mean(q * q, HD) + EPS) * (gq_ref[...] * scale)
        kn = k * lax.rsqrt(_seg_mean(k * k, HD) + EPS) * gk_ref[...]
        vb_ref[...] = v_ref[...].astype(BF16)
        c3v = jnp.concatenate([hi_ref[...], mid_ref[...], lo_ref[...]], axis=-1)
        r = lax.broadcasted_iota(jnp.int32, (3 * LANES, LANES - HD), 0)
        cc = lax.broadcasted_iota(jnp.int32, (3 * LANES, LANES - HD), 1)
        lane = lax.broadcasted_iota(jnp.int32, (tm, LANES - HD), 1)
        for hh in range(LANES // HD):
            head = hp * (LANES // HD) + hh
            sel_q = jnp.where((cc < 3) & (r == head + LANES * cc), 1.0, 0.0).astype(BF16)
            sel_k = jnp.where((cc >= 3) & (cc < 6) & (r == head + LANES * (cc - 3)), 1.0, 0.0).astype(BF16)
            qaug = jnp.dot(c3v, sel_q, preferred_element_type=F32) + jnp.where((lane >= 3) & (lane < 6), 1.0, 0.0)
            kaug = (jnp.where((lane < 3) | ((lane >= 6) & (lane < 9)), 1.0, 0.0)
                    - jnp.dot(c3v, sel_k, preferred_element_type=F32))
            sl = slice(hh * HD, (hh + 1) * HD)
            qa_ref[:, hh * LANES:(hh + 1) * LANES] = jnp.concatenate([qn[:, sl], qaug], axis=-1).astype(BF16)
            ka_ref[:, hh * LANES:(hh + 1) * LANES] = jnp.concatenate([kn[:, sl], kaug], axis=-1).astype(BF16)

    aw = (LANES // HD) * LANES
    blk = lambda cb: pl.BlockSpec((tm, LANES), functools.partial(lambda i, j, cb: (i, cb + j), cb=cb))
    cspec = pl.BlockSpec((tm, LANES), lambda i, j: (i, 0))
    gspec = pl.BlockSpec((1, LANES), lambda i, j: (0, 0))
    out = _pc(body, name=name, cargo=cargo,
              out_shape=(jax.ShapeDtypeStruct((T, HP * aw), BF16), jax.ShapeDtypeStruct((T, HP * aw), BF16),
                         jax.ShapeDtypeStruct((T, NF * HD), BF16)),
              grid=(T // tm, HP),
              in_specs=[blk(cbq), blk(cbk), blk(cbv), cspec, cspec, cspec, gspec, gspec],
              out_specs=(pl.BlockSpec((tm, aw), lambda i, j: (i, j)), pl.BlockSpec((tm, aw), lambda i, j: (i, j)),
                         pl.BlockSpec((tm, LANES), lambda i, j: (i, j))),
              compiler_params=_cp("parallel", "parallel"))(z, z, z, *c3, gq128, gk128,
                                                           *(cargo.arrays if cargo else ()))
    return _split_cargo(out, cargo)


def _fox_logits(qa, ka, fill):
    s = lax.dot_general(qa, ka, _DIMS["nt"], preferred_element_type=F32)
    tq, L = s.shape
    diag = s[:, L - tq:]
    keep = lax.broadcasted_iota(jnp.int32, diag.shape, 1) <= lax.broadcasted_iota(jnp.int32, diag.shape, 0)
    diag = jnp.where(keep, diag, fill)
    return diag if L == tq else jnp.concatenate([s[:, :L - tq], diag], axis=-1), keep


def _fox_fwd(qa, ka, vb, nb, S, HD, *, name, tq=256, cargo=None):
    T = qa.shape[0]
    nh = LANES // HD
    aw = nh * LANES
    HP = qa.shape[1] // aw
    tq = _tile(S, tq, 8)
    nq = S // tq

    def body(qa_ref, ka_ref, v_ref, o_ref, qb_ref):
        lane = lax.broadcasted_iota(jnp.int32, (tq, LANES), 1)
        for k in range(nq):
            @pl.when(pl.program_id(2) == k)
            def _(k=k):
                L = (k + 1) * tq
                outs = []
                v = v_ref[0:L, :]
                for hh in range(nh):
                    al = slice(hh * LANES, (hh + 1) * LANES)
                    qav = qa_ref[:, al]
                    s, _ = _fox_logits(qav, ka_ref[0:L, al], NEG)
                    m = jnp.max(s, axis=-1, keepdims=True)
                    e = jnp.exp(s - m)
                    l = jnp.sum(e, axis=-1, keepdims=True)
                    p = e / l
                    outs.append(jnp.dot(p.astype(BF16), v[:, hh * HD:(hh + 1) * HD], preferred_element_type=F32))
                    hi, mid, lo = _split3(-(m + jnp.log(l)))
                    qb_ref[:, al] = jnp.where(lane == HD + 6, hi, jnp.where(lane == HD + 7, mid,
                                                                         jnp.where(lane == HD + 8, lo, qav)))
                o_ref[...] = jnp.concatenate(outs, axis=-1).astype(BF16)

    qspec = pl.BlockSpec((tq, aw), lambda b, h, i: (b * nq + i, h))
    out = _pc(body, name=name, cargo=cargo,
              out_shape=(jax.ShapeDtypeStruct((T, HP * LANES), BF16), jax.ShapeDtypeStruct(qa.shape, BF16)),
              grid=(nb, HP, nq),
              in_specs=[qspec, pl.BlockSpec((S, aw), lambda b, h, i: (b, h)),
                        pl.BlockSpec((S, LANES), lambda b, h, i: (b, h))],
              out_specs=(pl.BlockSpec((tq, LANES), lambda b, h, i: (b * nq + i, h)), qspec),
              compiler_params=_cp("parallel", "parallel", "arbitrary"))(qa, ka, vb, *(cargo.arrays if cargo else ()))
    return _split_cargo(out, cargo)


def _fox_bwd(qa, ka, vb, do, nb, S, HD, *, name, tq=256, cargo=None):
    T = qa.shape[0]
    nh = LANES // HD
    aw = nh * LANES
    HP = qa.shape[1] // aw
    tq = _tile(S, tq, 8)
    nq = S // tq

    def body(qa_ref, ka_ref, v_ref, do_ref, dqa_ref, dka_ref, dv_ref, dcs_ref):
        i = pl.program_id(2)

        @pl.when(i == 0)
        def _():
            dka_ref[...] = jnp.zeros_like(dka_ref)
            dv_ref[...] = jnp.zeros_like(dv_ref)
            dcs_ref[...] = jnp.zeros_like(dcs_ref)

        for k in range(nq):
            @pl.when(i == k)
            def _(k=k):
                L = (k + 1) * tq
                dkas, dvs, css = [], [], []
                v, dob = v_ref[0:L, :], do_ref[...]
                for hh in range(nh):
                    al = slice(hh * LANES, (hh + 1) * LANES)
                    hl = slice(hh * HD, (hh + 1) * HD)
                    qav, kav, vv, dov = qa_ref[:, al], ka_ref[0:L, al], v[:, hl], dob[:, hl]
                    p = jnp.exp(_fox_logits(qav, kav, NEG)[0])
                    dp = lax.dot_general(dov, vv, _DIMS["nt"], preferred_element_type=F32)
                    dsf = p * (dp - jnp.sum(p * dp, axis=-1, keepdims=True))
                    css.append(jnp.sum(dsf, axis=0, keepdims=True))
                    ds = dsf.astype(BF16)
                    dqa_ref[:, al] = jnp.dot(ds, kav, preferred_element_type=F32)
                    dkas.append(lax.dot_general(ds, qav, _DIMS["tn"], preferred_element_type=F32))
                    dvs.append(lax.dot_general(p.astype(BF16), dov, _DIMS["tn"], preferred_element_type=F32))
                dka_ref[0:L, :] += jnp.concatenate(dkas, axis=-1)
                dv_ref[0:L, :] += jnp.concatenate(dvs, axis=-1)
                dcs_ref[:, 0:L] += jnp.concatenate(css + [jnp.zeros((8 - nh, L), F32)], axis=0)

    out = _pc(body, name=name, cargo=cargo,
              out_shape=(jax.ShapeDtypeStruct((T, HP * aw), F32), jax.ShapeDtypeStruct((T, HP * aw), F32),
                         jax.ShapeDtypeStruct((T, HP * LANES), F32), jax.ShapeDtypeStruct((nb, HP, 8, S), F32)),
              grid=(nb, HP, nq),
              in_specs=[pl.BlockSpec((tq, aw), lambda b, h, i: (b * nq + i, h)),
                        pl.BlockSpec((S, aw), lambda b, h, i: (b, h)),
                        pl.BlockSpec((S, LANES), lambda b, h, i: (b, h)),
                        pl.BlockSpec((tq, LANES), lambda b, h, i: (b * nq + i, h))],
              out_specs=(pl.BlockSpec((tq, aw), lambda b, h, i: (b * nq + i, h)),
                         pl.BlockSpec((S, aw), lambda b, h, i: (b, h)),
                         pl.BlockSpec((S, LANES), lambda b, h, i: (b, h)),
                         pl.BlockSpec((None, None, 8, S), lambda b, h, i: (b, h, 0, 0))),
              compiler_params=_cp("parallel", "parallel", "arbitrary"))(qa, ka, vb, do,
                                                                         *(cargo.arrays if cargo else ()))
    return _split_cargo(out, cargo)


def _mem_probs(q, k, scale):
    s = lax.dot_general(q, k, _DIMS["nt"], preferred_element_type=F32) * scale
    e = jnp.exp(s - jnp.max(s, axis=-1, keepdims=True))
    return e / jnp.sum(e, axis=-1, keepdims=True)


def _mem_fwd(q, k, v, nb, S, ML, *, name, tq=512):
    T, MW = q.shape
    nh = MW // LANES
    tq = _tile(S, tq, 8)
    nq = S // tq
    scale = LANES ** -0.5

    def body(q_ref, k_ref, v_ref, o_ref):
        for h in range(nh):
            sl = slice(h * LANES, (h + 1) * LANES)
            p = _mem_probs(q_ref[:, sl], k_ref[:, sl], scale)
            o_ref[:, sl] = jnp.dot(p.astype(BF16), v_ref[:, sl], preferred_element_type=F32).astype(BF16)

    return _pc(body, name=name, out_shape=jax.ShapeDtypeStruct((T, MW), BF16), grid=(nb, nq),
               in_specs=[pl.BlockSpec((tq, MW), lambda b, i: (b * nq + i, 0)),
                         pl.BlockSpec((ML, MW), lambda b, i: (b, 0)), pl.BlockSpec((ML, MW), lambda b, i: (b, 0))],
               out_specs=pl.BlockSpec((tq, MW), lambda b, i: (b * nq + i, 0)),
               compiler_params=_cp("parallel", "arbitrary"))(q, k, v)


def _mem_bwd(q, k, v, do, nb, S, ML, *, name, tq=512):
    T, MW = q.shape
    nh = MW // LANES
    tq = _tile(S, tq, 8)
    nq = S // tq
    scale = LANES ** -0.5

    def body(q_ref, k_ref, v_ref, do_ref, dq_ref, dk_ref, dv_ref):
        i = pl.program_id(1)

        @pl.when(i == 0)
        def _():
            dk_ref[...] = jnp.zeros_like(dk_ref)
            dv_ref[...] = jnp.zeros_like(dv_ref)

        for h in range(nh):
            sl = slice(h * LANES, (h + 1) * LANES)
            qv, kv, vv, dov = q_ref[:, sl], k_ref[:, sl], v_ref[:, sl], do_ref[:, sl]
            p = _mem_probs(qv, kv, scale)
            dp = lax.dot_general(dov, vv, _DIMS["nt"], preferred_element_type=F32)
            ds = (p * (dp - jnp.sum(p * dp, axis=-1, keepdims=True)) * scale).astype(BF16)
            dq_ref[:, sl] = jnp.dot(ds, kv, preferred_element_type=F32)
            dk_ref[:, sl] += lax.dot_general(ds, qv, _DIMS["tn"], preferred_element_type=F32)
            dv_ref[:, sl] += lax.dot_general(p.astype(BF16), dov, _DIMS["tn"], preferred_element_type=F32)

    kvspec = pl.BlockSpec((ML, MW), lambda b, i: (b, 0))
    qspec = pl.BlockSpec((tq, MW), lambda b, i: (b * nq + i, 0))
    return _pc(body, name=name,
               out_shape=(jax.ShapeDtypeStruct((T, MW), F32), jax.ShapeDtypeStruct((nb * ML, MW), F32),
                          jax.ShapeDtypeStruct((nb * ML, MW), F32)),
               grid=(nb, nq), in_specs=[qspec, kvspec, kvspec, qspec], out_specs=(qspec, kvspec, kvspec),
               compiler_params=_cp("parallel", "arbitrary"))(q, k, v, do)


def _place():
    x, y, c = lax.axis_index("x"), lax.axis_index("y"), lax.axis_index("c")
    chips = [(1 - x, y), (x, 1 - y), (1 - x, 1 - y)]
    return x, y, c, chips


def _any_specs(n):
    return [pl.BlockSpec(memory_space=pl.ANY)] * n


class _Cargo:
    def __init__(self, arrays, out_shapes, ncopies, copies, alias=None):
        self.arrays, self.out_shapes, self.ncopies, self.copies = list(arrays), list(out_shapes), ncopies, copies
        self.alias = alias or {}


def _load_cargo(body, cargo, kw):
    as_list = lambda v: list(v) if isinstance(v, (tuple, list)) else [v]
    grid = kw.get("grid", ())
    in_specs, out_specs, out_shape = as_list(kw["in_specs"]), as_list(kw["out_specs"]), as_list(kw["out_shape"])
    scratch = list(kw.get("scratch_shapes", ()))
    nin, nout, nci, nco, nscr = len(in_specs), len(out_specs), len(cargo.arrays), len(cargo.out_shapes), len(scratch)

    def loaded(*refs):
        ins, cin = refs[:nin], refs[nin:nin + nci]
        outs, cout = refs[nin + nci:nin + nci + nout], refs[nin + nci + nout:nin + nci + nout + nco]
        scr, (ssem, rsem) = refs[nin + nci + nout + nco:-2], refs[-2:]
        first = last = True
        for a, g in enumerate(grid):
            first = first & (pl.program_id(a) == 0)
            last = last & (pl.program_id(a) == g - 1)

        def start():
            for cp in cargo.copies(cin, cout, ssem, rsem, False):
                cp.start()

        def wait():
            for cp in cargo.copies(cin, cout, ssem, rsem, True):
                cp.wait_send()
                cp.wait_recv()

        if grid:
            pl.when(first)(start)
        else:
            start()
        body(*ins, *outs, *scr)
        if grid:
            pl.when(last)(wait)
        else:
            wait()

    kw = dict(kw, in_specs=in_specs + _any_specs(nci), out_specs=tuple(out_specs + _any_specs(nco)),
              out_shape=tuple(out_shape + cargo.out_shapes),
              scratch_shapes=scratch + [pltpu.SemaphoreType.DMA((cargo.ncopies,))] * 2)
    if cargo.alias:
        kw["input_output_aliases"] = {nin + i: nout + o for i, o in cargo.alias.items()}
    if grid:
        kw["compiler_params"] = _cp(*["arbitrary"] * len(grid))
    return loaded, kw


class _SemView:
    def __init__(self, ref, off):
        self.ref, self.off, self.at = ref, off, self

    def __getitem__(self, k):
        return self.ref.at[self.off + k]


def _join_cargo(a, b):
    na, nao = len(a.arrays), len(a.out_shapes)

    def copies(ins, outs, ssem, rsem, waiting):
        return (a.copies(ins[:na], outs[:nao], ssem, rsem, waiting) +
                b.copies(ins[na:], outs[nao:], _SemView(ssem, a.ncopies), _SemView(rsem, a.ncopies), waiting))

    alias = dict(a.alias)
    alias.update({na + i: nao + o for i, o in b.alias.items()})
    return _Cargo(a.arrays + b.arrays, a.out_shapes + b.out_shapes, a.ncopies + b.ncopies, copies, alias)


def _run_cargo(cargo, *, name):
    res = _pc(lambda: None, name=name, cargo=cargo, in_specs=[], out_specs=[], out_shape=[])(*cargo.arrays)
    return list(res)


def _full_shape(dims, kind):
    return (dims[0], 4 * dims[1]) if kind == "col" else (4 * dims[0], dims[1])


def _ag_copy(shard, full, kind, j_src, chip, c, ssem, rsem, piece):
    R, C = shard.shape
    p, npieces, cnt = (*piece, 1)[:3]
    rows = R // 2 // npieces * cnt
    row0 = pl.multiple_of(c * (R // 2) + p * (R // 2 // npieces), 16)
    if kind == "col":
        dst = full.at[pl.ds(row0, rows), pl.ds(pl.multiple_of(j_src * C, LANES), C)]
    else:
        dst = full.at[pl.ds(pl.multiple_of(j_src * R + row0, 16), rows), :]
    return pltpu.make_async_remote_copy(shard.at[pl.ds(row0, rows), :], dst, ssem, rsem,
                                        device_id=(*chip, c), device_id_type=MESH)


def _ag_cargo(shards, kinds, piece=(0, 1), lands=None):
    n = len(shards)

    def copies(ins, outs, ssem, rsem, waiting):
        x, y, c, chips = _place()
        return [_ag_copy(ins[w], outs[w], kinds[w], 2 * chip[0] + chip[1] if waiting else 2 * x + y, chip, c,
                         ssem.at[3 * w + q], rsem.at[3 * w + q], piece)
                for w in range(n) for q, chip in enumerate(chips)]

    shapes = [jax.ShapeDtypeStruct(_full_shape(s.shape, k), BF16) for s, k in zip(shards, kinds)]
    if lands is None:
        return _Cargo(shards, shapes, 3 * n, copies)
    return _Cargo(list(shards) + list(lands), shapes, 3 * n, copies, alias={n + w: w for w in range(n)})


def _scatter_cargo(parts, kinds, piece=(0, 1), lands=None):
    n = len(parts)
    pdims = [((p.shape[0], p.shape[1] // 4) if k == "col" else (p.shape[0] // 4, p.shape[1]))
             for p, k in zip(parts, kinds)]

    def copy(src, dst, kind, R, C, q, j, chip, c, ssem, rsem):
        p, npieces, cnt = (*piece, 1)[:3]
        rows = R // npieces * cnt
        r0 = p * (R // npieces)
        if kind == "col":
            win = src.at[pl.ds(r0, rows), pl.ds(pl.multiple_of(j * C, LANES), C)]
        else:
            win = src.at[pl.ds(pl.multiple_of(j * R + r0, 16), rows), :]
        return pltpu.make_async_remote_copy(win, dst.at[q, pl.ds(r0, rows), :], ssem, rsem,
                                            device_id=(*chip, c), device_id_type=MESH)

    def copies(ins, outs, ssem, rsem, waiting):
        x, y, c, chips = _place()
        return [copy(ins[w], outs[w], kinds[w], *pdims[w], q, 2 * chip[0] + chip[1], chip, c,
                     ssem.at[3 * w + q], rsem.at[3 * w + q])
                for w in range(n) for q, chip in enumerate(chips)]

    shapes = [jax.ShapeDtypeStruct((3,) + d, BF16) for d in pdims]
    if lands is None:
        return _Cargo(parts, shapes, 3 * n, copies)
    return _Cargo(list(parts) + list(lands), shapes, 3 * n, copies, alias={n + w: w for w in range(n)})


def _ag_fill(full, shard, kind, *, name):
    R, C = shard.shape
    RH = R // 2
    tm = _tile(RH, max(16, (3 * 2 ** 19 // C) // 16 * 16), 16)
    nbh = RH // tm
    n, nown = 3 * nbh, 2 * nbh
    xyc = jnp.stack([lax.axis_index("x"), lax.axis_index("y"), lax.axis_index("c")]).astype(jnp.int32)

    def chip_of(q, x, y):
        keep_x, keep_y = q % 2, (2 - q) // 2
        return 2 * ((1 - x) + keep_x * (2 * x - 1)) + (1 - y) + keep_y * (2 * y - 1)

    def window(ref, j, row0):
        if kind == "col":
            return ref.at[pl.ds(pl.multiple_of(row0, 16), tm), pl.ds(pl.multiple_of(j * C, LANES), C)]
        return ref.at[pl.ds(pl.multiple_of(j * R + row0, 16), tm), :]

    def body(p_ref, send_ref, own_ref, out_ref, land, ssem, rsem, wsem):
        i = pl.program_id(0)
        x, y = lax.axis_index("x"), lax.axis_index("y")
        c, peer = _pair_peer()
        own_cp = pltpu.make_async_copy(own_ref, window(out_ref, 2 * x + y, jnp.minimum(i, nown - 1) * tm), wsem.at[1])

        @pl.when(i < nown)
        def _():
            own_cp.start()

        def consume(land, ps):
            b = i - 1
            cp = pltpu.make_async_copy(land.at[ps], window(out_ref, chip_of(b // nbh, x, y),
                                                           (1 - c) * RH + (b % nbh) * tm), wsem.at[0])
            cp.start()
            cp.wait()

        _exchange_step(i, n, send_ref, land, ssem, rsem, None, peer, consume)

        @pl.when(i < nown)
        def _():
            own_cp.wait()

    now = lambda i: jnp.minimum(i, n - 1)

    def send_map(i, p):
        q, b = now(i) // nbh, now(i) % nbh
        j = chip_of(q, p[0], p[1])
        return (p[2] * nbh + b, j) if kind == "col" else ((j * 2 + p[2]) * nbh + b, 0)

    grid_spec = pltpu.PrefetchScalarGridSpec(
        num_scalar_prefetch=1, grid=(n + 1,),
        in_specs=[pl.BlockSpec((tm, C), send_map),
                  pl.BlockSpec((tm, C), lambda i, p: (jnp.minimum(i, nown - 1), 0))],
        out_specs=pl.BlockSpec(memory_space=pl.ANY),
        scratch_shapes=[pltpu.VMEM((n, tm, C), BF16), pltpu.SemaphoreType.DMA((n,)), pltpu.SemaphoreType.DMA((n,)),
                        pltpu.SemaphoreType.DMA((2,))])
    return _pc(body, name=name, out_shape=jax.ShapeDtypeStruct(full.shape, BF16), grid_spec=grid_spec,
               input_output_aliases={1: 0}, compiler_params=_cp("arbitrary"))(xyc, full, shard)


def _pair_peer():
    x, y, c = lax.axis_index("x"), lax.axis_index("y"), lax.axis_index("c")
    return c, (x, y, 1 - c)


def _exchange_step(i, n, src_ref, land, ssem, rsem, credit, peer, consume):
    def rdma(slot):
        return pltpu.make_async_remote_copy(src_ref, land.at[slot], ssem.at[slot], rsem.at[slot],
                                            device_id=peer, device_id_type=MESH)

    reuse = credit is not None
    slot = i % 2 if reuse else jnp.minimum(i, n - 1)

    @pl.when(i < n)
    def _():
        if reuse:
            @pl.when(i >= 2)
            def _():
                pl.semaphore_wait(credit, 1)

        rdma(slot).start()

    @pl.when(i >= 1)
    def _():
        ps = (i - 1) % 2 if reuse else i - 1
        rdma(ps).wait_recv()
        consume(land, ps)

        if reuse:
            @pl.when(i - 1 < n - 2)
            def _():
                pl.semaphore_signal(credit, 1, device_id=peer, device_id_type=MESH)

    @pl.when(i < n)
    def _():
        rdma(slot).wait_send()


def _xchg_scratch(tm, C, dtype):
    return [pltpu.VMEM((2, tm, C), dtype), pltpu.SemaphoreType.DMA((2,)), pltpu.SemaphoreType.DMA((2,)),
            pltpu.SemaphoreType.REGULAR]


def _rs_pair_sum(g, kind, *, name):
    K, N = g.shape
    NS, RH = (1, K // 2) if kind == "col" else (4, K // 8)
    tm = _tile(RH, max(16, (3 * 2 ** 19 // N) // 16 * 16), 16)
    if NS * (RH // tm) < 2:
        tm = RH // 2
    nb = RH // tm
    n = NS * nb
    cvec = lax.axis_index("c").reshape(1).astype(jnp.int32)

    def body(c_ref, send_ref, own_ref, out_ref, land, ssem, rsem, credit):
        _, peer = _pair_peer()

        def consume(land, ps):
            out_ref[...] = (own_ref[...].astype(F32) + land[ps].astype(F32)).astype(BF16)

        _exchange_step(pl.program_id(0), n, send_ref, land, ssem, rsem, credit, peer, consume)

    now = lambda i: jnp.minimum(i, n - 1)
    lag = lambda i: jnp.maximum(i - 1, 0)
    grid_spec = pltpu.PrefetchScalarGridSpec(
        num_scalar_prefetch=1, grid=(n + 1,),
        in_specs=[pl.BlockSpec((tm, N), lambda i, c: ((now(i) // nb * 2 + 1 - c[0]) * nb + now(i) % nb, 0)),
                  pl.BlockSpec((tm, N), lambda i, c: ((lag(i) // nb * 2 + c[0]) * nb + lag(i) % nb, 0))],
        out_specs=pl.BlockSpec((tm, N), lambda i, c: (lag(i), 0)),
        scratch_shapes=_xchg_scratch(tm, N, BF16))
    return _pc(body, name=name, out_shape=jax.ShapeDtypeStruct((NS * RH, N), BF16), grid_spec=grid_spec,
               compiler_params=_cp("arbitrary"))(cvec, g, g)


def _rs_sum_join(part, got, kind, *, name):
    _, hr, hc = got.shape
    tm = _tile(hr, max(16, (3 * 2 ** 18 // hc) // 16 * 16), 16)
    if hr // tm < 2:
        tm = hr // 2
    nb = hr // tm
    jvec = (2 * lax.axis_index("x") + lax.axis_index("y")).reshape(1).astype(jnp.int32)

    def body(j_ref, p_ref, a_ref, b_ref, c_ref, mine_ref, theirs_ref, sbuf, land, ssem, rsem, credit):
        i = pl.program_id(0)
        _, peer = _pair_peer()

        @pl.when(i < nb)
        def _():
            v = ((p_ref[...].astype(F32) + a_ref[...].astype(F32)) + b_ref[...].astype(F32)) + c_ref[...].astype(F32)
            mine_ref[...] = v
            sbuf[...] = v

        def consume(land, ps):
            theirs_ref[...] = land[ps]

        _exchange_step(i, nb, sbuf, land, ssem, rsem, credit, peer, consume)

    now = lambda i: jnp.minimum(i, nb - 1)
    lag = lambda i: jnp.maximum(i - 1, 0)
    own_spec = (pl.BlockSpec((tm, hc), lambda i, j: (now(i), j[0])) if kind == "col"
                else pl.BlockSpec((tm, hc), lambda i, j: (j[0] * nb + now(i), 0)))
    got_spec = lambda q: pl.BlockSpec((None, tm, hc), functools.partial(lambda i, j, q: (q, now(i), 0), q=q))
    grid_spec = pltpu.PrefetchScalarGridSpec(
        num_scalar_prefetch=1, grid=(nb + 1,),
        in_specs=[own_spec, got_spec(0), got_spec(1), got_spec(2)],
        out_specs=(pl.BlockSpec((tm, hc), lambda i, j: (now(i), 0)), pl.BlockSpec((tm, hc), lambda i, j: (lag(i), 0))),
        scratch_shapes=[pltpu.VMEM((tm, hc), F32)] + _xchg_scratch(tm, hc, F32))
    shp = jax.ShapeDtypeStruct((hr, hc), F32)
    return _pc(body, name=name, out_shape=(shp, shp), grid_spec=grid_spec,
               compiler_params=_cp("arbitrary"))(jvec, part, got, got, got)


def _adamw_join(w, mine, theirs, m, v, *, name):
    R2, wc = w.shape
    hr, hc = mine.shape
    assert hr < R2 <= 2 * hr and hc >= wc
    pref, rows8 = max(8, 2 ** 18 // hc), _round_up(R2, 8)
    tm = max(t for t in range(8, hr + 1, 8) if hr % t == 0 and rows8 % t == 0 and (t <= pref or t == 8))
    nbh = hr // tm
    cvec = lax.axis_index("c").reshape(1).astype(jnp.int32)

    def body(c_ref, w_ref, a_ref, b_ref, m_ref, v_ref, g_ref, d_ref, m2_ref, v2_ref):
        gv = jnp.where(pl.program_id(0) // nbh == c_ref[0], a_ref[...], b_ref[...])[:, :wc]
        m2 = ADAM_B1 * m_ref[...] + (1.0 - ADAM_B1) * gv
        v2 = ADAM_B2 * v_ref[...] + (1.0 - ADAM_B2) * (gv * gv)
        m_hat = m2 / (1.0 - ADAM_B1 ** ADAM_STEP)
        v_hat = v2 / (1.0 - ADAM_B2 ** ADAM_STEP)
        g_ref[...] = gv
        d_ref[...] = -ADAM_LR * (m_hat / (jnp.sqrt(v_hat) + ADAM_EPS) + ADAM_WD * w_ref[...])
        m2_ref[...] = m2
        v2_ref[...] = v2

    wspec = pl.BlockSpec((tm, wc), lambda i, c: (i, 0))
    mine_spec = pl.BlockSpec((tm, hc), lambda i, c: (jnp.where(i // nbh == c[0], i % nbh, 0), 0))
    theirs_spec = pl.BlockSpec((tm, hc), lambda i, c: (jnp.where(i // nbh == c[0], 0, i % nbh), 0))
    grid_spec = pltpu.PrefetchScalarGridSpec(
        num_scalar_prefetch=1, grid=(pl.cdiv(R2, tm),),
        in_specs=[wspec, mine_spec, theirs_spec, wspec, wspec], out_specs=(wspec,) * 4)
    return _pc(body, name=name, out_shape=(jax.ShapeDtypeStruct((R2, wc), F32),) * 4, grid_spec=grid_spec,
               compiler_params=_cp("parallel"))(cvec, w, mine, theirs, m, v)


def _all_reduce_small(pack):
    R = pack.shape[0]

    def body(p_ref, o_ref, buf, send, recv):
        x, y, c, _ = _place()
        me = 4 * x + 2 * y + c
        buf[0] = p_ref[...]
        cps = []
        for r in range(1, 8):
            peer = (x if r & 4 == 0 else 1 - x, y if r & 2 == 0 else 1 - y, c if r & 1 == 0 else 1 - c)
            cp = pltpu.make_async_remote_copy(p_ref, buf.at[r], send.at[r - 1], recv.at[r - 1],
                                              device_id=peer, device_id_type=MESH)
            cp.start()
            cps.append(cp)
        for cp in cps:
            cp.wait()
        acc = buf[jnp.bitwise_xor(me, 0)]
        for k in range(1, 8):
            acc = acc + buf[jnp.bitwise_xor(me, k)]
        o_ref[...] = acc

    return _pc(body, name="allreduce_small", out_shape=jax.ShapeDtypeStruct((R, LANES), F32),
               in_specs=[pl.BlockSpec(memory_space=pltpu.VMEM)], out_specs=pl.BlockSpec(memory_space=pltpu.VMEM),
               scratch_shapes=[pltpu.VMEM((8, R, LANES), F32), pltpu.SemaphoreType.DMA((7,)),
                               pltpu.SemaphoreType.DMA((7,))])(pack)


def _pack_small(arrs):
    parts, sizes = [], []
    for a in arrs:
        f = a.reshape(-1).astype(F32)
        n = _round_up(f.shape[0], 8 * LANES)
        parts.append(jnp.pad(f, (0, n - f.shape[0])).reshape(n // LANES, LANES))
        sizes.append((a.shape, f.shape[0], n // LANES))
    return jnp.concatenate(parts, axis=0), sizes


def _unpack_small(pack, sizes):
    out, r0 = [], 0
    for shape, cnt, rows in sizes:
        out.append(pack[r0:r0 + rows].reshape(-1)[:cnt].reshape(shape))
        r0 += rows
    return out


def _ffn_fwd(xin, norm, weight, tag, gather_on_gateup, gather_on_down):
    hb = _rms_fwd(xin, norm, name=tag + "_norm")
    cargo, landed = gather_on_gateup()
    res = _gateup(hb, weight("w_gate_up"), name=tag + "_gateup", tm=1024, cargo=cargo)
    if cargo is not None:
        res, lands = res
        landed(lands)
    gate, up, act = res
    cargo, landed = gather_on_down()
    xout = _mm(act, weight("w_down"), res=xin, scale=0.5, name=tag + "_down", tm=1024, tn=512, tk=8192, cargo=cargo)
    if cargo is not None:
        xout, lands = xout
        landed(lands)
    return xout, (hb, gate, up, act)


def _ffn_bwd(dy, dyb, xin, norm, wgu, wd, saved, tag, scatter, gu_piece=(0, 1), rest_on_dnorm=False):
    hb, gate, up, act = saved
    d_wd = _mm(act, dyb, mode="tn", scale=0.5, out_dtype=BF16, name=tag + "_dwd", tm=512, tn=1024, tk=4096)
    cargo = scatter(tag + "_w_down", d_wd)
    rest, rest_done = scatter.leftover()
    dgu = _dact_swiglu(dyb, wd, gate, up, name=tag + "_dact", cargo=rest)
    if rest is not None:
        dgu, lands = dgu
        rest_done(lands)
    d_wgu, lands = _mm(hb, dgu, mode="tn", halves="b", out_dtype=BF16, name=tag + "_dwgu", tm=1024, tn=512, tk=4096,
                       cargo=cargo)
    scatter.done(lands)
    cargo = scatter(tag + "_w_gate_up", d_wgu, piece=gu_piece)
    dh, lands = _mm(dgu, wgu, mode="nt", halves="a", out_dtype=BF16, name=tag + "_dh", tm=1024, tn=512, tk=5632,
                    cargo=cargo)
    scatter.done(lands)
    rest, rest_done = scatter.leftover() if rest_on_dnorm else (None, None)
    res = _rms_bwd(xin, dh, norm, dy, name=tag + "_dnorm", cargo=rest)
    if rest is not None:
        res, lands = res
        rest_done(lands)
    return res


def kernel(x, mem, ffn1_norm, ffn1_w_gate_up, ffn1_w_down, mix_norm, mem_norm, w_in, b_forget, pool_w, pool_scale, w_pool_up, fox_q_norm, fox_k_norm, w_fox_o, w_mem_kv, mem_q_norm, mem_k_norm, w_mem_o, w_out, ffn2_norm, ffn2_w_gate_up, ffn2_w_down, loss_target, m_ffn1_norm, m_ffn1_w_gate_up, m_ffn1_w_down, m_mix_norm, m_mem_norm, m_w_in, m_b_forget, m_pool_w, m_pool_scale, m_w_pool_up, m_fox_q_norm, m_fox_k_norm, m_w_fox_o, m_w_mem_kv, m_mem_q_norm, m_mem_k_norm, m_w_mem_o, m_w_out, m_ffn2_norm, m_ffn2_w_gate_up, m_ffn2_w_down, v_ffn1_norm, v_ffn1_w_gate_up, v_ffn1_w_down, v_mix_norm, v_mem_norm, v_w_in, v_b_forget, v_pool_w, v_pool_scale, v_w_pool_up, v_fox_q_norm, v_fox_k_norm, v_w_fox_o, v_w_mem_kv, v_mem_q_norm, v_mem_k_norm, v_w_mem_o, v_w_out, v_ffn2_norm, v_ffn2_w_gate_up, v_ffn2_w_down):
    P = dict(locals())
    big = ["ffn1_w_gate_up", "ffn1_w_down", "w_in", "w_pool_up", "w_fox_o", "w_mem_kv", "w_mem_o", "w_out",
           "ffn2_w_gate_up", "ffn2_w_down"]
    kinds = ["col", "row", "col", "col", "col", "row", "col", "row", "col", "row"]
    small = ["ffn1_norm", "mix_norm", "mem_norm", "b_forget", "pool_w", "pool_scale", "fox_q_norm", "fox_k_norm",
             "mem_q_norm", "mem_k_norm", "ffn2_norm"]
    order = ["ffn1_norm", "ffn1_w_gate_up", "ffn1_w_down", "mix_norm", "mem_norm", "w_in", "b_forget", "pool_w",
             "pool_scale", "w_pool_up", "fox_q_norm", "fox_k_norm", "w_fox_o", "w_mem_kv", "mem_q_norm", "mem_k_norm",
             "w_mem_o", "w_out", "ffn2_norm", "ffn2_w_gate_up", "ffn2_w_down"]

    nb, S, D = x.shape
    T = nb * S
    ML = mem.shape[1]
    NF, HD = b_forget.shape[-1], fox_q_norm.shape[-1]
    FW = NF * HD
    MW, MHD = w_mem_o.shape[1], mem_q_norm.shape[-1]
    PG, PD = pool_w.shape[1], pool_w.shape[2]
    PW = PG * PD
    n_in = w_in.shape[-1]
    n_in_pad = _round_up(n_in, LANES)
    in_w = 4 * n_in
    assert HD * 2 == LANES and MHD == LANES and PD == LANES and in_w == PW + 3 * FW + NF + MW + 3 * D

    shards = []
    for nme in big:
        wl = P[nme][0].astype(BF16)
        if nme == "w_in":
            wl = jnp.pad(wl, ((0, 0), (0, n_in_pad - n_in)))
        shards.append(wl)
    shard, kind_of, full = dict(zip(big, shards)), dict(zip(big, kinds)), {}

    def landed(names):
        def fill(lands):
            for nme, l in zip(names, lands):
                full[nme] = _ag_fill(l, shard[nme], kind_of[nme], name="ag_fill_" + nme)
        return fill

    def gather(names, piece=(0, 1), lands=None):
        return _ag_cargo([shard[nme] for nme in names], [kind_of[nme] for nme in names], piece, lands)

    landed(["ffn1_w_gate_up"])(_run_cargo(gather(["ffn1_w_gate_up"]), name="ag_first"))
    begun = {}

    def on_gateup1():
        def fill(lands):
            landed(["ffn1_w_down"])(lands[:1])
            begun["w_in"] = lands[1]
        return _join_cargo(gather(["ffn1_w_down"]), gather(["w_in"], (0, 2))), fill

    x0 = x.reshape(T, D)
    x1, sv1 = _ffn_fwd(x0, ffn1_norm, lambda nme: full["ffn1_" + nme], "ffn1", on_gateup1,
                       lambda: (gather(["w_in"], (1, 2), [begun["w_in"]]), landed(["w_in"])))
    def gathered_cols(a, b):
        out = []
        for j in range(4):
            lo, hi = max(a, j * n_in), min(b, (j + 1) * n_in)
            if lo < hi:
                out.append(full["w_in"][:, j * (n_in_pad - n_in) + lo:j * (n_in_pad - n_in) + hi])
        return out

    o_q, o_k, o_v, o_f = PW, PW + FW, PW + 2 * FW, PW + 3 * FW
    o_qm, o_gate = o_f + NF, o_f + NF + MW
    p_q, p_k, p_v, p_pool, p_qm, p_f = 3 * D, 3 * D + FW, 3 * D + 2 * FW, 3 * D + 3 * FW, 3 * D + 3 * FW + PW, 3 * D + 3 * FW + PW + MW
    inp = _round_up(p_f + LANES, 512)
    win_p = jnp.concatenate(gathered_cols(o_gate, in_w) + gathered_cols(o_q, o_f) + gathered_cols(0, PW)
                            + gathered_cols(o_qm, o_gate) + gathered_cols(o_f, o_qm)
                            + [jnp.zeros((D, inp - p_f - NF), BF16)], axis=1)

    hb2 = _rms_fwd(x1, mix_norm, name="mix_norm")
    later = ["ffn2_w_down", "w_fox_o", "w_mem_kv", "w_pool_up"]
    z, lands = _mm(hb2, win_p, name="in_proj", tm=1024, tn=1536, cargo=gather(later))
    landed(later)(lands)

    vpool = _pool_fwd(z, p_pool // LANES, pool_w[0], pool_scale, nb, S, name="pool_fwd")
    y_pool = _mm(vpool, full["w_pool_up"], out_dtype=BF16, name="pool_up", tm=1024, tn=512)

    b128 = jnp.pad(b_forget, ((0, 0), (0, LANES - NF)))
    c3 = _fox_c(z, p_f // LANES, b128, nb, S, NF, name="fox_c")
    later = ["w_out", "w_mem_o"]
    (qa, ka, vb), lands = _fox_qk(z, p_q // LANES, p_k // LANES, p_v // LANES, c3, fox_q_norm, fox_k_norm, HD, NF,
                                  name="fox_qk", cargo=gather(later))
    landed(later)(lands)
    (o_fox, qb), lands = _fox_fwd(qa, ka, vb, nb, S, HD, name="fox_fwd",
                                  cargo=gather(["ffn2_w_gate_up"], (0, 4, 3)))
    begun["ffn2_w_gate_up"] = lands[0]
    y_fox = _mm(o_fox, full["w_fox_o"], out_dtype=BF16, name="fox_out", tm=1024, tn=512)

    mem2 = mem.reshape(nb * ML, D)
    memn = _rms_fwd(mem2, mem_norm, name="mem_norm")
    kv = _mm(memn, full["w_mem_kv"], name="mem_kv")
    kmn = _headnorm_fwd(kv, 0, MW, mem_k_norm, MHD, name="mem_knorm")
    vmb = kv[:, MW:].astype(BF16)
    qmn = _headnorm_fwd(z, p_qm // LANES, MW, mem_q_norm, MHD, name="mem_qnorm")
    o_mem = _mem_fwd(qmn, kmn, vmb, nb, S, ML, name="mem_fwd")
    y_mem = _mm(o_mem, full["w_mem_o"], out_dtype=BF16, name="mem_out", tm=1024, tn=512)

    merged, lands = _merge_fwd(z, y_pool, y_fox, y_mem, name="merge_fwd",
                               cargo=gather(["ffn2_w_gate_up"], (3, 4), [begun["ffn2_w_gate_up"]]))
    landed(["ffn2_w_gate_up"])(lands)
    x2 = _mm(merged, full["w_out"], res=x1, name="out_proj", tm=1024, tn=512)
    x3, sv2 = _ffn_fwd(x2, ffn2_norm, lambda nme: full["ffn2_" + nme], "ffn2", lambda: (None, None),
                       lambda: (None, None))
    dy3, dy3b, loss_part = _loss(x3, loss_target.reshape(T, D), name="loss")

    g, parts, got = {}, {}, {}
    rs_kind = dict(kind_of, w_in="row")

    leftovers = []

    def scatter(*named_grads, piece=(0, 1)):
        names = named_grads[0::2]
        for nme, grad in zip(names, named_grads[1::2]):
            parts[nme] = _rs_pair_sum(grad, rs_kind[nme], name="rs_pair_" + nme)
        scatter.names = names
        sent = (*piece, 1)[2]
        if sent < piece[1]:
            leftovers.append((names, (sent, piece[1], piece[1] - sent)))
        return _scatter_cargo([parts[nme] for nme in names], [rs_kind[nme] for nme in names], piece)

    def scattered(lands):
        got.update(zip(scatter.names, lands))

    def leftover():
        if not leftovers:
            return None, None
        names, piece = leftovers.pop(0)
        cargo = _scatter_cargo([parts[nme] for nme in names], [rs_kind[nme] for nme in names], piece,
                               [got[nme] for nme in names])
        return cargo, lambda lands: got.update(zip(names, lands))

    scatter.done, scatter.leftover = scattered, leftover
    dx2, dx2b, g["ffn2_norm"] = _ffn_bwd(dy3, dy3b, x2, ffn2_norm, full["ffn2_w_gate_up"], full["ffn2_w_down"], sv2,
                                         "ffn2", scatter, gu_piece=(0, 2))

    dmerged = _mm(dx2b, full["w_out"], mode="nt", out_dtype=BF16, name="d_merged", tm=1024, tn=512)
    g["w_out"] = _mm(merged, dx2b, mode="tn", out_dtype=BF16, name="d_w_out", tk=4096)
    dyp, dyf, dym, dgl = _merge_bwd(dmerged, z, y_pool, y_fox, y_mem, name="merge_bwd")

    g["w_pool_up"] = _mm(vpool, dyp, mode="tn", out_dtype=BF16, name="d_w_pool_up", tk=4096)
    dvp = _mm(dyp, full["w_pool_up"], mode="nt", name="d_vpool", tm=1024)
    du_pool, g["pool_w"], g["pool_scale"] = _pool_bwd(z, p_pool // LANES, dvp, pool_w[0], pool_scale, nb, S,
                                                      name="pool_bwd")

    g["w_fox_o"] = _mm(o_fox, dyf, mode="tn", out_dtype=BF16, name="d_w_fox_o", tk=4096)
    do_fox = _mm(dyf, full["w_fox_o"], mode="nt", out_dtype=BF16, name="d_o_fox", tm=1024)
    rest, rest_done = leftover()
    (dqa, dka, dvf, dcs), lands = _fox_bwd(qb, ka, vb, do_fox, nb, S, HD, name="fox_bwd", cargo=rest)
    rest_done(lands)
    dq, dgq = _headnorm_bwd(z, p_q // LANES, FW, dqa, fox_q_norm, HD, HD ** -0.5, True, BF16, name="fox_dq")
    dk, dgk = _headnorm_bwd(z, p_k // LANES, FW, dka, fox_k_norm, HD, 1.0, True, BF16, name="fox_dk")
    g["fox_q_norm"], g["fox_k_norm"] = _fold_gain(dgq, HD), _fold_gain(dgk, HD)
    dc = -dcs[:, :, :LANES // HD, :].reshape(nb, NF, S).transpose(0, 2, 1).reshape(T, NF)
    df, db = _fox_c_bwd(jnp.pad(dc, ((0, 0), (0, LANES - NF))), z, p_f // LANES, b128, nb, S, name="fox_c_bwd")
    g["b_forget"] = db[:, :NF]

    g["w_mem_o"] = _mm(o_mem, dym, mode="tn", out_dtype=BF16, name="d_w_mem_o", tk=4096)
    do_mem = _mm(dym, full["w_mem_o"], mode="nt", out_dtype=BF16, name="d_o_mem", tm=1024)
    dqmn, dkmn, dvm = _mem_bwd(qmn, kmn, vmb, do_mem, nb, S, ML, name="mem_bwd")
    dqm, g["mem_q_norm"] = _headnorm_bwd(z, p_qm // LANES, MW, dqmn, mem_q_norm, MHD, 1.0, False, BF16, name="mem_dq")
    dkm, g["mem_k_norm"] = _headnorm_bwd(kv, 0, MW, dkmn, mem_k_norm, MHD, 1.0, False, BF16, name="mem_dk")
    dkv = jnp.concatenate([dkm, dvm.astype(BF16)], axis=1)
    g["w_mem_kv"] = _mm(memn, dkv, mode="tn", out_dtype=BF16, name="d_w_mem_kv")
    dmemn = _mm(dkv, full["w_mem_kv"], mode="nt", name="d_memn")
    _, _, g["mem_norm"] = _rms_bwd(mem2, dmemn, mem_norm, jnp.zeros_like(mem2), name="mem_dnorm")

    dz = jnp.concatenate([dgl, dq, dk, dvf.astype(BF16), du_pool, dqm, df.astype(BF16),
                          jnp.zeros((T, inp - p_f - LANES), BF16)], axis=1)
    cargo = scatter(*sum([[nme, g[nme]] for nme in ("w_out", "w_pool_up", "w_fox_o", "w_mem_o", "w_mem_kv")], []))
    dwin_p, lands = _mm(dz, hb2, mode="tn", out_dtype=BF16, name="d_w_in", tm=512, tn=1024, tk=4096, cargo=cargo)
    scattered(lands)
    dwin_o = jnp.concatenate([dwin_p[p_pool:p_qm], dwin_p[p_q:p_pool], dwin_p[p_f:p_f + NF], dwin_p[p_qm:p_f],
                              dwin_p[:p_q]], axis=0)
    cargo = scatter("w_in", jnp.concatenate(
        [jnp.pad(dwin_o[j * n_in:(j + 1) * n_in], ((0, n_in_pad - n_in), (0, 0))) for j in range(4)], axis=0),
        piece=(0, 2))
    dh2, lands = _mm(dz, win_p, mode="nt", out_dtype=BF16, name="d_h2", tm=1024, tn=512, tk=3584, cargo=cargo)
    scattered(lands)
    dx1, dx1b, g["mix_norm"] = _rms_bwd(x1, dh2, mix_norm, dx2, name="mix_dnorm")

    dx0, _, g["ffn1_norm"] = _ffn_bwd(dx1, dx1b, x0, ffn1_norm, full["ffn1_w_gate_up"], full["ffn1_w_down"], sv1,
                                      "ffn1", scatter, gu_piece=(0, 4, 3), rest_on_dnorm=True)

    halves = {nme: _rs_sum_join(parts[nme], got[nme], rs_kind[nme], name="rs_join_" + nme) for nme in big}
    gfull = {}

    gpack, sizes = _pack_small([g[nme].reshape(P[nme].shape) for nme in small] + [loss_part[:, :1]])
    gsum = _unpack_small(_all_reduce_small(gpack), sizes)
    loss = gsum[-1].reshape(())
    for nme, a in zip(small, gsum[:-1]):
        gfull[nme] = a

    delta, new_m, new_v = {}, {}, {}
    for nme in big:
        shp = P[nme].shape
        two = (lambda a: a.reshape(shp[-2:]).T) if nme == "w_in" else (lambda a: a.reshape(shp[-2:]))
        back = (lambda a: a.T.reshape(shp)) if nme == "w_in" else (lambda a: a.reshape(shp))
        gfull[nme], delta[nme], new_m[nme], new_v[nme] = [
            back(a) for a in _adamw_join(two(P[nme]), *halves[nme], two(P["m_" + nme]), two(P["v_" + nme]),
                                         name="adamw_" + nme)]
    wpack, _ = _pack_small([P[nme] for nme in small])
    mpack, _ = _pack_small([P["m_" + nme] for nme in small])
    vpack, _ = _pack_small([P["v_" + nme] for nme in small])
    gpack2, ssz = _pack_small([gfull[nme] for nme in small])
    for dct, pk in zip((delta, new_m, new_v), _adamw(wpack, gpack2, mpack, vpack, name="adamw_small")):
        for nme, a in zip(small, _unpack_small(pk, ssz)):
            dct[nme] = a

    grad_x = dx0.reshape(nb, S, D)
    return (loss, grad_x, *[gfull[nme] for nme in order], *[delta[nme] for nme in order],
            *[new_m[nme] for nme in order], *[new_v[nme] for nme in order])
```

```python
import functools

import jax
import jax.numpy as jnp
from jax import lax
from jax.experimental import pallas as pl
from jax.experimental.pallas import tpu as pltpu

F32 = jnp.float32
BF16 = jnp.bfloat16
MESH = pl.DeviceIdType.MESH
EPS = 1e-6
POOL_WINDOWS = (2, 4, 8, 16)
ADAM_LR, ADAM_B1, ADAM_B2, ADAM_EPS, ADAM_WD, ADAM_STEP = 0.001, 0.9, 0.999, 1e-08, 0.01, 10
LANES = 128
VMEM_LIMIT = 56 * 1024 * 1024
NEG = -1e30


def _pc(body, *, name, cargo=None, **kw):
    if cargo is not None:
        body, kw = _load_cargo(body, cargo, kw)
    return pl.pallas_call(body, name=name, **kw)


def _cp(*sem):
    return pltpu.CompilerParams(dimension_semantics=sem, vmem_limit_bytes=VMEM_LIMIT)


def _tile(n, pref, mult=LANES):
    if n <= pref:
        return n
    t = (pref // mult) * mult
    while t >= mult:
        if n % t == 0:
            return t
        t -= mult
    return n


def _round_up(n, m):
    return (n + m - 1) // m * m


_DIMS = {"nn": (((1,), (0,)), ((), ())), "nt": (((1,), (1,)), ((), ())), "tn": (((0,), (0,)), ((), ()))}


def _split_cargo(res, cargo):
    if cargo is None:
        return res
    nco = len(cargo.out_shapes)
    own = res[:len(res) - nco]
    return (own[0] if len(own) == 1 else own), list(res[len(res) - nco:])


def _mm(a, b, *, name, mode="nn", out_dtype=F32, scale=1.0, res=None, tm=512, tn=512, tk=2048, cargo=None,
        halves=None):
    if halves == "a":
        assert mode == "nt"
        (_, M, Kh), (N, K2) = a.shape, b.shape
        K = 2 * Kh
    elif halves == "b":
        assert mode == "tn"
        (K, M), (_, K2, Nh) = a.shape, b.shape
        N = 2 * Nh
    elif mode == "nn":
        (M, K), (K2, N) = a.shape, b.shape
    elif mode == "nt":
        (M, K), (N, K2) = a.shape, b.shape
    else:
        (K, M), (K2, N) = a.shape, b.shape
    assert K == K2, (name, a.shape, b.shape)
    tm = _tile(M, tm, 8 if M % LANES else LANES)
    tn = _tile(N // 2 if halves == "b" else N, tn)
    tk = _tile(K // 2 if halves == "a" else K, tk)
    nk = K // tk
    dims = _DIMS[mode]
    if halves == "a":
        a_spec = pl.BlockSpec((None, tm, tk), lambda i, j, k: (k // (nk // 2), i, k % (nk // 2)))
    elif mode == "tn":
        a_spec = pl.BlockSpec((tk, tm), lambda i, j, k: (k, i))
    else:
        a_spec = pl.BlockSpec((tm, tk), lambda i, j, k: (i, k))
    if halves == "b":
        njh = N // 2 // tn
        b_spec = pl.BlockSpec((None, tk, tn), lambda i, j, k: (j // njh, k, j % njh))
    elif mode == "nt":
        b_spec = pl.BlockSpec((tn, tk), lambda i, j, k: (j, k))
    else:
        b_spec = pl.BlockSpec((tk, tn), lambda i, j, k: (k, j))
    o_spec = pl.BlockSpec((tm, tn), lambda i, j, k: (i, j))
    has_res = res is not None

    def body(*refs):
        a_ref, b_ref = refs[:2]
        r_ref = refs[2] if has_res else None
        o_ref = refs[3] if has_res else refs[2]

        def finish(v):
            v = v * scale
            if has_res:
                v = r_ref[...] + v
            o_ref[...] = v.astype(o_ref.dtype)

        prod = lax.dot_general(a_ref[...], b_ref[...], dims, preferred_element_type=F32)
        if nk == 1:
            finish(prod)
            return
        acc, k = refs[-1], pl.program_id(2)

        @pl.when(k == 0)
        def _():
            acc[...] = prod

        @pl.when(k > 0)
        def _():
            acc[...] += prod

        @pl.when(k == nk - 1)
        def _():
            finish(acc[...])

    out = _pc(body, name=name, cargo=cargo, out_shape=jax.ShapeDtypeStruct((M, N), out_dtype),
              grid=(M // tm, N // tn, nk),
              in_specs=[a_spec, b_spec] + ([o_spec] if has_res else []), out_specs=o_spec,
              scratch_shapes=[] if nk == 1 else [pltpu.VMEM((tm, tn), F32)],
              compiler_params=_cp("parallel", "parallel", "arbitrary"))(
                  *((a, b, res) if has_res else (a, b)), *(cargo.arrays if cargo else ()))
    return _split_cargo(out, cargo)


def _dact_swiglu(dyb, wd, gate, up, *, name, tm=1024, tn=512, cargo=None):
    M, K = dyb.shape
    Fh = wd.shape[0]
    tm, tn = _tile(M, tm), _tile(Fh, tn)

    rows = _tile(tm, 256, 16)

    def body(dy_ref, w_ref, g_ref, u_ref, o_ref):
        w = w_ref[...]
        for r in range(0, tm, rows):
            sl = slice(r, r + rows)
            d = lax.dot_general(dy_ref[sl, :], w, _DIMS["nt"], preferred_element_type=F32) * 0.5
            g, u = g_ref[sl, :].astype(F32), u_ref[sl, :].astype(F32)
            s = jax.nn.sigmoid(g)
            o_ref[0, sl, :] = (d * u * (s * (1.0 + g * (1.0 - s)))).astype(BF16)
            o_ref[1, sl, :] = (d * (g * s)).astype(BF16)

    t_spec = pl.BlockSpec((tm, tn), lambda i, j: (i, j))
    out = _pc(body, name=name, cargo=cargo, out_shape=jax.ShapeDtypeStruct((2, M, Fh), BF16),
              grid=(M // tm, Fh // tn),
              in_specs=[pl.BlockSpec((tm, K), lambda i, j: (i, 0)), pl.BlockSpec((tn, K), lambda i, j: (j, 0)),
                        t_spec, t_spec],
              out_specs=pl.BlockSpec((2, tm, tn), lambda i, j: (0, i, j)),
              compiler_params=_cp("parallel", "parallel"))(dyb, wd, gate, up, *(cargo.arrays if cargo else ()))
    return _split_cargo(out, cargo)


def _gateup(h, wgu, *, name, tm=512, tn=512, cargo=None):
    M, K = h.shape
    Fh = wgu.shape[1] // 2
    tm, tn = _tile(M, tm), _tile(Fh, tn)
    nj = Fh // tn

    def body(h_ref, wg_ref, wu_ref, g_ref, u_ref, a_ref):
        hv = h_ref[...]
        g = jnp.dot(hv, wg_ref[...], preferred_element_type=F32)
        u = jnp.dot(hv, wu_ref[...], preferred_element_type=F32)
        g_ref[...] = g.astype(BF16)
        u_ref[...] = u.astype(BF16)
        a_ref[...] = (g * jax.nn.sigmoid(g) * u).astype(BF16)

    o_spec = pl.BlockSpec((tm, tn), lambda i, j: (i, j))
    shp = jax.ShapeDtypeStruct((M, Fh), BF16)
    out = _pc(body, name=name, cargo=cargo, out_shape=(shp, shp, shp), grid=(M // tm, nj),
              in_specs=[pl.BlockSpec((tm, K), lambda i, j: (i, 0)),
                        pl.BlockSpec((K, tn), lambda i, j: (0, j)),
                        pl.BlockSpec((K, tn), lambda i, j: (0, j + nj))],
              out_specs=(o_spec, o_spec, o_spec),
              compiler_params=_cp("parallel", "parallel"))(h, wgu, wgu, *(cargo.arrays if cargo else ()))
    return _split_cargo(out, cargo)


def _rowmap(fn, ins, outs, *, rows, tm, name, accs=(), cargo=None):
    tm = _tile(rows, tm, 8)
    arrs, in_specs = [], []
    for d in ins:
        if d[0] == "row":
            _, a, w, cb = d
            w = a.shape[1] if w is None else w
            in_specs.append(pl.BlockSpec((tm, w), functools.partial(lambda i, cb: (i, cb), cb=cb)))
        else:
            a = d[1]
            in_specs.append(pl.BlockSpec(a.shape, functools.partial(lambda i, nd: (0,) * nd, nd=a.ndim)))
        arrs.append(a)
    out_shape = [jax.ShapeDtypeStruct((rows, w), dt) for w, dt in outs]
    out_specs = [pl.BlockSpec((tm, w), lambda i: (i, 0)) for w, _ in outs]
    for shp, dt in accs:
        out_shape.append(jax.ShapeDtypeStruct(shp, dt))
        out_specs.append(pl.BlockSpec(shp, functools.partial(lambda i, nd: (0,) * nd, nd=len(shp))))
    nin, nout, nacc = len(ins), len(outs), len(accs)

    def body(*refs):
        res = fn(*[r[...] for r in refs[:nin]])
        res = res if isinstance(res, (tuple, list)) else (res,)
        for r, v in zip(refs[nin:nin + nout], res[:nout]):
            r[...] = v.astype(r.dtype)
        if nacc:
            acc_refs = refs[nin + nout:]

            @pl.when(pl.program_id(0) == 0)
            def _():
                for r in acc_refs:
                    r[...] = jnp.zeros_like(r)

            for r, v in zip(acc_refs, res[nout:]):
                r[...] += v

    out = _pc(body, name=name, cargo=cargo, out_shape=tuple(out_shape), grid=(rows // tm,), in_specs=in_specs,
              out_specs=tuple(out_specs),
              compiler_params=_cp("arbitrary" if nacc else "parallel"))(*arrs, *(cargo.arrays if cargo else ()))
    return _split_cargo(out, cargo) if cargo else (out[0] if len(out) == 1 else out)


def _rms_fwd(x, g, *, name, cargo=None):
    def fn(xv, gv):
        return xv * lax.rsqrt(jnp.mean(xv * xv, axis=-1, keepdims=True) + EPS) * gv
    return _rowmap(fn, [("row", x, None, 0), ("bc", g)], [(x.shape[1], BF16)], rows=x.shape[0], tm=256, name=name,
                   cargo=cargo)


def _rms_bwd(x, dh, g, dres, *, name, cargo=None):
    D = x.shape[1]

    def fn(xv, dhv, gv, drv):
        r = lax.rsqrt(jnp.mean(xv * xv, axis=-1, keepdims=True) + EPS)
        xh = xv * r
        dhf = dhv.astype(F32)
        dxn = dhf * gv
        dx = drv + r * (dxn - xh * jnp.mean(dxn * xh, axis=-1, keepdims=True))
        return dx, dx, jnp.sum(dhf * xh, axis=0, keepdims=True)

    return _rowmap(fn, [("row", x, None, 0), ("row", dh, None, 0), ("bc", g), ("row", dres, None, 0)],
                   [(D, F32), (D, BF16)], rows=x.shape[0], tm=256, name=name, accs=[((1, D), F32)], cargo=cargo)


def _loss(y, tgt, *, name):
    D = y.shape[1]

    def fn(yv, tv):
        e = yv - tv
        d = e * (1.0 / D)
        part = 0.5 * jnp.sum(jnp.sum(e * e, axis=-1, keepdims=True) * (1.0 / D), axis=0, keepdims=True)
        return d, d, jnp.broadcast_to(part, (1, LANES))

    return _rowmap(fn, [("row", y, None, 0), ("row", tgt, None, 0)], [(D, F32), (D, BF16)],
                   rows=y.shape[0], tm=256, name=name, accs=[((1, LANES), F32)])


def _merge_fwd(z, yp, yf, ym, *, name, cargo=None):
    D = yp.shape[1]

    def fn(gp, gf, gm, a, b, c):
        return jax.nn.sigmoid(gp) * a + jax.nn.sigmoid(gf) * b + jax.nn.sigmoid(gm) * c

    return _rowmap(fn, [("row", z, D, 0), ("row", z, D, 1), ("row", z, D, 2),
                        ("row", yp, None, 0), ("row", yf, None, 0), ("row", ym, None, 0)],
                   [(D, BF16)], rows=yp.shape[0], tm=128, name=name, cargo=cargo)


def _merge_bwd(dm, z, yp, yf, ym, *, name):
    D = yp.shape[1]

    def fn(d, gp, gf, gm, a, b, c):
        outs, dgl = [], []
        for gl, yv in ((gp, a), (gf, b), (gm, c)):
            s = jax.nn.sigmoid(gl)
            outs.append(d * s)
            dgl.append(d * yv * (s * (1.0 - s)))
        return (*outs, jnp.concatenate(dgl, axis=-1))

    return _rowmap(fn, [("row", dm, None, 0), ("row", z, D, 0), ("row", z, D, 1), ("row", z, D, 2),
                        ("row", yp, None, 0), ("row", yf, None, 0), ("row", ym, None, 0)],
                   [(D, BF16), (D, BF16), (D, BF16), (3 * D, BF16)], rows=yp.shape[0], tm=128, name=name)


def _adamw(w, g, m, v, *, name):
    C = w.shape[1]

    def fn(wv, gv, mv, vv):
        m2 = ADAM_B1 * mv + (1.0 - ADAM_B1) * gv
        v2 = ADAM_B2 * vv + (1.0 - ADAM_B2) * (gv * gv)
        m_hat = m2 / (1.0 - ADAM_B1 ** ADAM_STEP)
        v_hat = v2 / (1.0 - ADAM_B2 ** ADAM_STEP)
        return -ADAM_LR * (m_hat / (jnp.sqrt(v_hat) + ADAM_EPS) + ADAM_WD * wv), m2, v2

    tm = max(8, (262144 // C) // 8 * 8)
    return _rowmap(fn, [("row", a, None, 0) for a in (w, g, m, v)], [(C, F32)] * 3,
                   rows=w.shape[0], tm=tm, name=name)


def _seg_mean(v, hd):
    if hd == LANES:
        return jnp.mean(v, axis=-1, keepdims=True)
    lo = lax.broadcasted_iota(jnp.int32, v.shape, 1) < hd
    s0 = jnp.sum(jnp.where(lo, v, 0.0), axis=-1, keepdims=True)
    s1 = jnp.sum(jnp.where(lo, 0.0, v), axis=-1, keepdims=True)
    return jnp.where(lo, s0, s1) * (1.0 / hd)


def _gain128(g, hd):
    g = g.reshape(1, hd).astype(F32)
    return g if hd == LANES else jnp.concatenate([g, g], axis=-1)


def _headnorm_fwd(x, cb0, width, gain, hd, *, name):
    rows = x.shape[0]
    tm = _tile(rows, 512, 8)
    g128 = _gain128(gain, hd)

    def body(x_ref, g_ref, o_ref):
        xv = x_ref[...]
        o_ref[...] = (xv * lax.rsqrt(_seg_mean(xv * xv, hd) + EPS) * g_ref[...]).astype(BF16)

    return _pc(body, name=name, out_shape=jax.ShapeDtypeStruct((rows, width), BF16),
               grid=(rows // tm, width // LANES),
               in_specs=[pl.BlockSpec((tm, LANES), lambda i, j: (i, cb0 + j)),
                         pl.BlockSpec((1, LANES), lambda i, j: (0, 0))],
               out_specs=pl.BlockSpec((tm, LANES), lambda i, j: (i, j)),
               compiler_params=_cp("parallel", "parallel"))(x, g128)


def _headnorm_bwd(x, cb0, width, dy, gain, hd, post, spread, out_dtype, *, name):
    rows = x.shape[0]
    tm = _tile(rows, 512, 8)
    g128 = _gain128(gain, hd)
    dyw = 2 * LANES if spread else LANES

    def body(x_ref, dy_ref, g_ref, dx_ref, dg_ref):
        xv = x_ref[...]
        dyv = dy_ref[...]
        if spread:
            dyv = jnp.concatenate([dyv[:, :hd], dyv[:, LANES:LANES + hd]], axis=-1)
        dyv = dyv * post
        r = lax.rsqrt(_seg_mean(xv * xv, hd) + EPS)
        xh = xv * r
        dxn = dyv * g_ref[...]
        dx_ref[...] = (r * (dxn - xh * _seg_mean(dxn * xh, hd))).astype(dx_ref.dtype)

        @pl.when((pl.program_id(0) == 0) & (pl.program_id(1) == 0))
        def _():
            dg_ref[...] = jnp.zeros_like(dg_ref)

        dg_ref[...] += jnp.sum(dyv * xh, axis=0, keepdims=True)

    return _pc(body, name=name,
               out_shape=(jax.ShapeDtypeStruct((rows, width), out_dtype), jax.ShapeDtypeStruct((1, LANES), F32)),
               grid=(rows // tm, width // LANES),
               in_specs=[pl.BlockSpec((tm, LANES), lambda i, j: (i, cb0 + j)),
                         pl.BlockSpec((tm, dyw), lambda i, j: (i, j)),
                         pl.BlockSpec((1, LANES), lambda i, j: (0, 0))],
               out_specs=(pl.BlockSpec((tm, LANES), lambda i, j: (i, j)),
                          pl.BlockSpec((1, LANES), lambda i, j: (0, 0))),
               compiler_params=_cp("arbitrary", "arbitrary"))(x, dy, g128)


def _fold_gain(dg, hd):
    return dg if hd == LANES else dg[:, :hd] + dg[:, hd:]


def _shift_down(v, k, row):
    return jnp.where(row >= k, pltpu.roll(v, k, 0), 0.0)


def _shift_up(v, k, row, S):
    return jnp.where(row < S - k, pltpu.roll(v, S - k, 0), 0.0)


def _window_sum(v, g, row, shift):
    s, outs = v, []
    for k in (1, 2, 4, 8):
        s = s + shift(s, k)
        outs.append(s)
    return jnp.where(g == 0, outs[0], jnp.where(g == 1, outs[1], jnp.where(g == 2, outs[2], outs[3])))


def _pool_fwd(z, cb0, pool_w, pool_scale, nb, S, *, name):
    PG, PD = pool_w.shape[0], pool_w.shape[1]
    assert PD == LANES and PG <= len(POOL_WINDOWS)
    pw = pool_w.astype(BF16)

    def body(u_ref, w_ref, s_ref, o_ref):
        g = pl.program_id(0)
        u = u_ref[...]
        row = lax.broadcasted_iota(jnp.int32, u.shape, 0)
        cnt = jnp.minimum(row + 1, jnp.left_shift(2, g)).astype(F32)
        pooled = _window_sum(u, g, row, lambda v, k: _shift_down(v, k, row)) / cnt
        mixed = jnp.dot((pooled - u).astype(BF16), w_ref[...], preferred_element_type=F32)
        o_ref[...] = (mixed * s_ref[...]).astype(BF16)

    return _pc(body, name=name, out_shape=jax.ShapeDtypeStruct((nb * S, PG * PD), BF16), grid=(PG, nb),
               in_specs=[pl.BlockSpec((S, PD), lambda g, b: (b, cb0 + g)),
                         pl.BlockSpec((None, PD, PD), lambda g, b: (g, 0, 0)),
                         pl.BlockSpec((1, PD), lambda g, b: (0, g))],
               out_specs=pl.BlockSpec((S, PD), lambda g, b: (b, g)),
               compiler_params=_cp("parallel", "parallel"))(z, pw, pool_scale)


def _pool_bwd(z, cb0, dv, pool_w, pool_scale, nb, S, *, name):
    PG, PD = pool_w.shape[0], pool_w.shape[1]
    pw = pool_w.astype(BF16)

    def body(u_ref, dv_ref, w_ref, s_ref, du_ref, dw_ref, ds_ref):
        g, b = pl.program_id(0), pl.program_id(1)
        u, dvv, w = u_ref[...], dv_ref[...], w_ref[...]
        row = lax.broadcasted_iota(jnp.int32, u.shape, 0)
        cnt = jnp.minimum(row + 1, jnp.left_shift(2, g)).astype(F32)
        diff = (_window_sum(u, g, row, lambda v, k: _shift_down(v, k, row)) / cnt - u).astype(BF16)
        mixed = jnp.dot(diff, w, preferred_element_type=F32)
        dmixed = (dvv * s_ref[...]).astype(BF16)
        ddiff = lax.dot_general(dmixed, w, _DIMS["nt"], preferred_element_type=F32)
        du_ref[...] = (_window_sum(ddiff / cnt, g, row, lambda v, k: _shift_up(v, k, row, S)) - ddiff).astype(BF16)

        @pl.when(b == 0)
        def _():
            dw_ref[...] = jnp.zeros_like(dw_ref)
            ds_ref[...] = jnp.zeros_like(ds_ref)

        dw_ref[...] += lax.dot_general(diff, dmixed, _DIMS["tn"], preferred_element_type=F32)
        ds_ref[...] += jnp.sum(dvv * mixed, axis=0, keepdims=True)

    return _pc(body, name=name,
               out_shape=(jax.ShapeDtypeStruct((nb * S, PG * PD), BF16), jax.ShapeDtypeStruct((PG, PD, PD), F32),
                          jax.ShapeDtypeStruct((1, PG * PD), F32)),
               grid=(PG, nb),
               in_specs=[pl.BlockSpec((S, PD), lambda g, b: (b, cb0 + g)),
                         pl.BlockSpec((S, PD), lambda g, b: (b, g)),
                         pl.BlockSpec((None, PD, PD), lambda g, b: (g, 0, 0)),
                         pl.BlockSpec((1, PD), lambda g, b: (0, g))],
               out_specs=(pl.BlockSpec((S, PD), lambda g, b: (b, g)),
                          pl.BlockSpec((None, PD, PD), lambda g, b: (g, 0, 0)),
                          pl.BlockSpec((1, PD), lambda g, b: (0, g))),
               compiler_params=_cp("parallel", "arbitrary"))(z, dv, pw, pool_scale)


def _split3(c):
    hi = c.astype(BF16)
    r1 = c - hi.astype(F32)
    mid = r1.astype(BF16)
    lo = (r1 - mid.astype(F32)).astype(BF16)
    return hi, mid, lo


def _log_sigmoid(v):
    return jnp.minimum(v, 0.0) - jnp.log(1.0 + jnp.exp(-jnp.abs(v)))


def _fox_c(z, cb, b128, nb, S, NF, *, name):
    def body(f_ref, b_ref, hi_ref, mid_ref, lo_ref):
        v = f_ref[...] + b_ref[...]
        lane = lax.broadcasted_iota(jnp.int32, v.shape, 1)
        row = lax.broadcasted_iota(jnp.int32, v.shape, 0)
        c = jnp.where(lane < NF, _log_sigmoid(v), 0.0)
        k = 1
        while k < S:
            c = c + _shift_down(c, k, row)
            k *= 2
        hi, mid, lo = _split3(c)
        hi_ref[...], mid_ref[...], lo_ref[...] = hi, mid, lo

    shp = jax.ShapeDtypeStruct((nb * S, LANES), BF16)
    spec = pl.BlockSpec((S, LANES), lambda i: (i, 0))
    return _pc(body, name=name, out_shape=(shp, shp, shp), grid=(nb,),
               in_specs=[pl.BlockSpec((S, LANES), lambda i: (i, cb)), pl.BlockSpec((1, LANES), lambda i: (0, 0))],
               out_specs=(spec, spec, spec), compiler_params=_cp("parallel"))(z, b128)


def _fox_c_bwd(dc, z, cb, b128, nb, S, *, name):
    def body(dc_ref, f_ref, b_ref, df_ref, db_ref):
        d = dc_ref[...]
        row = lax.broadcasted_iota(jnp.int32, d.shape, 0)
        k = 1
        while k < S:
            d = d + _shift_up(d, k, row, S)
            k *= 2
        df = d * jax.nn.sigmoid(-(f_ref[...] + b_ref[...]))
        df_ref[...] = df

        @pl.when(pl.program_id(0) == 0)
        def _():
            db_ref[...] = jnp.zeros_like(db_ref)

        db_ref[...] += jnp.sum(df, axis=0, keepdims=True)

    return _pc(body, name=name,
               out_shape=(jax.ShapeDtypeStruct((nb * S, LANES), F32), jax.ShapeDtypeStruct((1, LANES), F32)),
               grid=(nb,),
               in_specs=[pl.BlockSpec((S, LANES), lambda i: (i, 0)), pl.BlockSpec((S, LANES), lambda i: (i, cb)),
                         pl.BlockSpec((1, LANES), lambda i: (0, 0))],
               out_specs=(pl.BlockSpec((S, LANES), lambda i: (i, 0)), pl.BlockSpec((1, LANES), lambda i: (0, 0))),
               compiler_params=_cp("arbitrary"))(dc, z, b128)


def _fox_qk(z, cbq, cbk, cbv, c3, gq, gk, HD, NF, *, name, cargo=None):
    T = z.shape[0]
    HP = NF * HD // LANES
    tm = _tile(T, 512, 8)
    scale = HD ** -0.5
    gq128, gk128 = _gain128(gq, HD), _gain128(gk, HD)

    def body(q_ref, k_ref, v_ref, hi_ref, mid_ref, lo_ref, gq_ref, gk_ref, qa_ref, ka_ref, vb_ref):
        hp = pl.program_id(1)
        q, k = q_ref[...], k_ref[...]
        qn = q * lax.rsqrt(_seg_mean(q * q, HD) + EPS) * (gq_ref[...] * scale)
        kn = k * lax.rsqrt(_seg_mean(k * k, HD) + EPS) * gk_ref[...]
        vb_ref[...] = v_ref[...].astype(BF16)
        c3v = jnp.concatenate([hi_ref[...], mid_ref[...], lo_ref[...]], axis=-1)
        r = lax.broadcasted_iota(jnp.int32, (3 * LANES, LANES - HD), 0)
        cc = lax.broadcasted_iota(jnp.int32, (3 * LANES, LANES - HD), 1)
        lane = lax.broadcasted_iota(jnp.int32, (tm, LANES - HD), 1)
        for hh in range(LANES // HD):
            head = hp * (LANES // HD) + hh
            sel_q = jnp.where((cc < 3) & (r == head + LANES * cc), 1.0, 0.0).astype(BF16)
            sel_k = jnp.where((cc >= 3) & (cc < 6) & (r == head + LANES * (cc - 3)), 1.0, 0.0).astype(BF16)
            qaug = jnp.dot(c3v, sel_q, preferred_element_type=F32) + jnp.where((lane >= 3) & (lane < 6), 1.0, 0.0)
            kaug = (jnp.where((lane < 3) | ((lane >= 6) & (lane < 9)), 1.0, 0.0)
                    - jnp.dot(c3v, sel_k, preferred_element_type=F32))
            sl = slice(hh * HD, (hh + 1) * HD)
            qa_ref[:, hh * LANES:(hh + 1) * LANES] = jnp.concatenate([qn[:, sl], qaug], axis=-1).astype(BF16)
            ka_ref[:, hh * LANES:(hh + 1) * LANES] = jnp.concatenate([kn[:, sl], kaug], axis=-1).astype(BF16)

    aw = (LANES // HD) * LANES
    blk = lambda cb: pl.BlockSpec((tm, LANES), functools.partial(lambda i, j, cb: (i, cb + j), cb=cb))
    cspec = pl.BlockSpec((tm, LANES), lambda i, j: (i, 0))
    gspec = pl.BlockSpec((1, LANES), lambda i, j: (0, 0))
    out = _pc(body, name=name, cargo=cargo,
              out_shape=(jax.ShapeDtypeStruct((T, HP * aw), BF16), jax.ShapeDtypeStruct((T, HP * aw), BF16),
                         jax.ShapeDtypeStruct((T, NF * HD), BF16)),
              grid=(T // tm, HP),
              in_specs=[blk(cbq), blk(cbk), blk(cbv), cspec, cspec, cspec, gspec, gspec],
              out_specs=(pl.BlockSpec((tm, aw), lambda i, j: (i, j)), pl.BlockSpec((tm, aw), lambda i, j: (i, j)),
                         pl.BlockSpec((tm, LANES), lambda i, j: (i, j))),
              compiler_params=_cp("parallel", "parallel"))(z, z, z, *c3, gq128, gk128,
                                                           *(cargo.arrays if cargo else ()))
    return _split_cargo(out, cargo)


def _fox_logits(qa, ka, fill):
    s = lax.dot_general(qa, ka, _DIMS["nt"], preferred_element_type=F32)
    tq, L = s.shape
    diag = s[:, L - tq:]
    keep = lax.broadcasted_iota(jnp.int32, diag.shape, 1) <= lax.broadcasted_iota(jnp.int32, diag.shape, 0)
    diag = jnp.where(keep, diag, fill)
    return diag if L == tq else jnp.concatenate([s[:, :L - tq], diag], axis=-1), keep


def _fox_fwd(qa, ka, vb, nb, S, HD, *, name, tq=256, cargo=None):
    T = qa.shape[0]
    nh = LANES // HD
    aw = nh * LANES
    HP = qa.shape[1] // aw
    tq = _tile(S, tq, 8)
    nq = S // tq

    def body(qa_ref, ka_ref, v_ref, o_ref, qb_ref):
        lane = lax.broadcasted_iota(jnp.int32, (tq, LANES), 1)
        for k in range(nq):
            @pl.when(pl.program_id(2) == k)
            def _(k=k):
                L = (k + 1) * tq
                outs = []
                v = v_ref[0:L, :]
                for hh in range(nh):
                    al = slice(hh * LANES, (hh + 1) * LANES)
                    qav = qa_ref[:, al]
                    s, _ = _fox_logits(qav, ka_ref[0:L, al], NEG)
                    m = jnp.max(s, axis=-1, keepdims=True)
                    e = jnp.exp(s - m)
                    l = jnp.sum(e, axis=-1, keepdims=True)
                    p = e / l
                    outs.append(jnp.dot(p.astype(BF16), v[:, hh * HD:(hh + 1) * HD], preferred_element_type=F32))
                    hi, mid, lo = _split3(-(m + jnp.log(l)))
                    qb_ref[:, al] = jnp.where(lane == HD + 6, hi, jnp.where(lane == HD + 7, mid,
                                                                         jnp.where(lane == HD + 8, lo, qav)))
                o_ref[...] = jnp.concatenate(outs, axis=-1).astype(BF16)

    qspec = pl.BlockSpec((tq, aw), lambda b, h, i: (b * nq + i, h))
    out = _pc(body, name=name, cargo=cargo,
              out_shape=(jax.ShapeDtypeStruct((T, HP * LANES), BF16), jax.ShapeDtypeStruct(qa.shape, BF16)),
              grid=(nb, HP, nq),
              in_specs=[qspec, pl.BlockSpec((S, aw), lambda b, h, i: (b, h)),
                        pl.BlockSpec((S, LANES), lambda b, h, i: (b, h))],
              out_specs=(pl.BlockSpec((tq, LANES), lambda b, h, i: (b * nq + i, h)), qspec),
              compiler_params=_cp("parallel", "parallel", "arbitrary"))(qa, ka, vb, *(cargo.arrays if cargo else ()))
    return _split_cargo(out, cargo)


def _fox_bwd(qa, ka, vb, do, nb, S, HD, *, name, tq=256, cargo=None):
    T = qa.shape[0]
    nh = LANES // HD
    aw = nh * LANES
    HP = qa.shape[1] // aw
    tq = _tile(S, tq, 8)
    nq = S // tq

    def body(qa_ref, ka_ref, v_ref, do_ref, dqa_ref, dka_ref, dv_ref, dcs_ref):
        i = pl.program_id(2)

        @pl.when(i == 0)
        def _():
            dka_ref[...] = jnp.zeros_like(dka_ref)
            dv_ref[...] = jnp.zeros_like(dv_ref)
            dcs_ref[...] = jnp.zeros_like(dcs_ref)

        for k in range(nq):
            @pl.when(i == k)
            def _(k=k):
                L = (k + 1) * tq
                dkas, dvs, css = [], [], []
                v, dob = v_ref[0:L, :], do_ref[...]
                for hh in range(nh):
                    al = slice(hh * LANES, (hh + 1) * LANES)
                    hl = slice(hh * HD, (hh + 1) * HD)
                    qav, kav, vv, dov = qa_ref[:, al], ka_ref[0:L, al], v[:, hl], dob[:, hl]
                    p = jnp.exp(_fox_logits(qav, kav, NEG)[0])
                    dp = lax.dot_general(dov, vv, _DIMS["nt"], preferred_element_type=F32)
                    dsf = p * (dp - jnp.sum(p * dp, axis=-1, keepdims=True))
                    css.append(jnp.sum(dsf, axis=0, keepdims=True))
                    ds = dsf.astype(BF16)
                    dqa_ref[:, al] = jnp.dot(ds, kav, preferred_element_type=F32)
                    dkas.append(lax.dot_general(ds, qav, _DIMS["tn"], preferred_element_type=F32))
                    dvs.append(lax.dot_general(p.astype(BF16), dov, _DIMS["tn"], preferred_element_type=F32))
                dka_ref[0:L, :] += jnp.concatenate(dkas, axis=-1)
                dv_ref[0:L, :] += jnp.concatenate(dvs, axis=-1)
                dcs_ref[:, 0:L] += jnp.concatenate(css + [jnp.zeros((8 - nh, L), F32)], axis=0)

    out = _pc(body, name=name, cargo=cargo,
              out_shape=(jax.ShapeDtypeStruct((T, HP * aw), F32), jax.ShapeDtypeStruct((T, HP * aw), F32),
                         jax.ShapeDtypeStruct((T, HP * LANES), F32), jax.ShapeDtypeStruct((nb, HP, 8, S), F32)),
              grid=(nb, HP, nq),
              in_specs=[pl.BlockSpec((tq, aw), lambda b, h, i: (b * nq + i, h)),
                        pl.BlockSpec((S, aw), lambda b, h, i: (b, h)),
                        pl.BlockSpec((S, LANES), lambda b, h, i: (b, h)),
                        pl.BlockSpec((tq, LANES), lambda b, h, i: (b * nq + i, h))],
              out_specs=(pl.BlockSpec((tq, aw), lambda b, h, i: (b * nq + i, h)),
                         pl.BlockSpec((S, aw), lambda b, h, i: (b, h)),
                         pl.BlockSpec((S, LANES), lambda b, h, i: (b, h)),
                         pl.BlockSpec((None, None, 8, S), lambda b, h, i: (b, h, 0, 0))),
              compiler_params=_cp("parallel", "parallel", "arbitrary"))(qa, ka, vb, do,
                                                                         *(cargo.arrays if cargo else ()))
    return _split_cargo(out, cargo)


def _mem_probs(q, k, scale):
    s = lax.dot_general(q, k, _DIMS["nt"], preferred_element_type=F32) * scale
    e = jnp.exp(s - jnp.max(s, axis=-1, keepdims=True))
    return e / jnp.sum(e, axis=-1, keepdims=True)


def _mem_fwd(q, k, v, nb, S, ML, *, name, tq=512):
    T, MW = q.shape
    nh = MW // LANES
    tq = _tile(S, tq, 8)
    nq = S // tq
    scale = LANES ** -0.5

    def body(q_ref, k_ref, v_ref, o_ref):
        for h in range(nh):
            sl = slice(h * LANES, (h + 1) * LANES)
            p = _mem_probs(q_ref[:, sl], k_ref[:, sl], scale)
            o_ref[:, sl] = jnp.dot(p.astype(BF16), v_ref[:, sl], preferred_element_type=F32).astype(BF16)

    return _pc(body, name=name, out_shape=jax.ShapeDtypeStruct((T, MW), BF16), grid=(nb, nq),
               in_specs=[pl.BlockSpec((tq, MW), lambda b, i: (b * nq + i, 0)),
                         pl.BlockSpec((ML, MW), lambda b, i: (b, 0)), pl.BlockSpec((ML, MW), lambda b, i: (b, 0))],
               out_specs=pl.BlockSpec((tq, MW), lambda b, i: (b * nq + i, 0)),
               compiler_params=_cp("parallel", "arbitrary"))(q, k, v)


def _mem_bwd(q, k, v, do, nb, S, ML, *, name, tq=512):
    T, MW = q.shape
    nh = MW // LANES
    tq = _tile(S, tq, 8)
    nq = S // tq
    scale = LANES ** -0.5

    def body(q_ref, k_ref, v_ref, do_ref, dq_ref, dk_ref, dv_ref):
        i = pl.program_id(1)

        @pl.when(i == 0)
        def _():
            dk_ref[...] = jnp.zeros_like(dk_ref)
            dv_ref[...] = jnp.zeros_like(dv_ref)

        for h in range(nh):
            sl = slice(h * LANES, (h + 1) * LANES)
            qv, kv, vv, dov = q_ref[:, sl], k_ref[:, sl], v_ref[:, sl], do_ref[:, sl]
            p = _mem_probs(qv, kv, scale)
            dp = lax.dot_general(dov, vv, _DIMS["nt"], preferred_element_type=F32)
            ds = (p * (dp - jnp.sum(p * dp, axis=-1, keepdims=True)) * scale).astype(BF16)
            dq_ref[:, sl] = jnp.dot(ds, kv, preferred_element_type=F32)
            dk_ref[:, sl] += lax.dot_general(ds, qv, _DIMS["tn"], preferred_element_type=F32)
            dv_ref[:, sl] += lax.dot_general(p.astype(BF16), dov, _DIMS["tn"], preferred_element_type=F32)

    kvspec = pl.BlockSpec((ML, MW), lambda b, i: (b, 0))
    qspec = pl.BlockSpec((tq, MW), lambda b, i: (b * nq + i, 0))
    return _pc(body, name=name,
               out_shape=(jax.ShapeDtypeStruct((T, MW), F32), jax.ShapeDtypeStruct((nb * ML, MW), F32),
                          jax.ShapeDtypeStruct((nb * ML, MW), F32)),
               grid=(nb, nq), in_specs=[qspec, kvspec, kvspec, qspec], out_specs=(qspec, kvspec, kvspec),
               compiler_params=_cp("parallel", "arbitrary"))(q, k, v, do)


def _place():
    x, y, c = lax.axis_index("x"), lax.axis_index("y"), lax.axis_index("c")
    chips = [(1 - x, y), (x, 1 - y), (1 - x, 1 - y)]
    return x, y, c, chips


def _any_specs(n):
    return [pl.BlockSpec(memory_space=pl.ANY)] * n


class _Cargo:
    def __init__(self, arrays, out_shapes, ncopies, copies, alias=None):
        self.arrays, self.out_shapes, self.ncopies, self.copies = list(arrays), list(out_shapes), ncopies, copies
        self.alias = alias or {}


def _load_cargo(body, cargo, kw):
    as_list = lambda v: list(v) if isinstance(v, (tuple, list)) else [v]
    grid = kw.get("grid", ())
    in_specs, out_specs, out_shape = as_list(kw["in_specs"]), as_list(kw["out_specs"]), as_list(kw["out_shape"])
    scratch = list(kw.get("scratch_shapes", ()))
    nin, nout, nci, nco, nscr = len(in_specs), len(out_specs), len(cargo.arrays), len(cargo.out_shapes), len(scratch)

    def loaded(*refs):
        ins, cin = refs[:nin], refs[nin:nin + nci]
        outs, cout = refs[nin + nci:nin + nci + nout], refs[nin + nci + nout:nin + nci + nout + nco]
        scr, (ssem, rsem) = refs[nin + nci + nout + nco:-2], refs[-2:]
        first = last = True
        for a, g in enumerate(grid):
            first = first & (pl.program_id(a) == 0)
            last = last & (pl.program_id(a) == g - 1)

        def start():
            for cp in cargo.copies(cin, cout, ssem, rsem, False):
                cp.start()

        def wait():
            for cp in cargo.copies(cin, cout, ssem, rsem, True):
                cp.wait_send()
                cp.wait_recv()

        if grid:
            pl.when(first)(start)
        else:
            start()
        body(*ins, *outs, *scr)
        if grid:
            pl.when(last)(wait)
        else:
            wait()

    kw = dict(kw, in_specs=in_specs + _any_specs(nci), out_specs=tuple(out_specs + _any_specs(nco)),
              out_shape=tuple(out_shape + cargo.out_shapes),
              scratch_shapes=scratch + [pltpu.SemaphoreType.DMA((cargo.ncopies,))] * 2)
    if cargo.alias:
        kw["input_output_aliases"] = {nin + i: nout + o for i, o in cargo.alias.items()}
    if grid:
        kw["compiler_params"] = _cp(*["arbitrary"] * len(grid))
    return loaded, kw


class _SemView:
    def __init__(self, ref, off):
        self.ref, self.off, self.at = ref, off, self

    def __getitem__(self, k):
        return self.ref.at[self.off + k]


def _join_cargo(a, b):
    na, nao = len(a.arrays), len(a.out_shapes)

    def copies(ins, outs, ssem, rsem, waiting):
        return (a.copies(ins[:na], outs[:nao], ssem, rsem, waiting) +
                b.copies(ins[na:], outs[nao:], _SemView(ssem, a.ncopies), _SemView(rsem, a.ncopies), waiting))

    alias = dict(a.alias)
    alias.update({na + i: nao + o for i, o in b.alias.items()})
    return _Cargo(a.arrays + b.arrays, a.out_shapes + b.out_shapes, a.ncopies + b.ncopies, copies, alias)


def _run_cargo(cargo, *, name):
    res = _pc(lambda: None, name=name, cargo=cargo, in_specs=[], out_specs=[], out_shape=[])(*cargo.arrays)
    return list(res)


def _full_shape(dims, kind):
    return (dims[0], 4 * dims[1]) if kind == "col" else (4 * dims[0], dims[1])


def _ag_copy(shard, full, kind, j_src, chip, c, ssem, rsem, piece):
    R, C = shard.shape
    p, npieces, cnt = (*piece, 1)[:3]
    rows = R // 2 // npieces * cnt
    row0 = pl.multiple_of(c * (R // 2) + p * (R // 2 // npieces), 16)
    if kind == "col":
        dst = full.at[pl.ds(row0, rows), pl.ds(pl.multiple_of(j_src * C, LANES), C)]
    else:
        dst = full.at[pl.ds(pl.multiple_of(j_src * R + row0, 16), rows), :]
    return pltpu.make_async_remote_copy(shard.at[pl.ds(row0, rows), :], dst, ssem, rsem,
                                        device_id=(*chip, c), device_id_type=MESH)


def _ag_cargo(shards, kinds, piece=(0, 1), lands=None):
    n = len(shards)

    def copies(ins, outs, ssem, rsem, waiting):
        x, y, c, chips = _place()
        return [_ag_copy(ins[w], outs[w], kinds[w], 2 * chip[0] + chip[1] if waiting else 2 * x + y, chip, c,
                         ssem.at[3 * w + q], rsem.at[3 * w + q], piece)
                for w in range(n) for q, chip in enumerate(chips)]

    shapes = [jax.ShapeDtypeStruct(_full_shape(s.shape, k), BF16) for s, k in zip(shards, kinds)]
    if lands is None:
        return _Cargo(shards, shapes, 3 * n, copies)
    return _Cargo(list(shards) + list(lands), shapes, 3 * n, copies, alias={n + w: w for w in range(n)})


def _scatter_cargo(parts, kinds, piece=(0, 1), lands=None):
    n = len(parts)
    pdims = [((p.shape[0], p.shape[1] // 4) if k == "col" else (p.shape[0] // 4, p.shape[1]))
             for p, k in zip(parts, kinds)]

    def copy(src, dst, kind, R, C, q, j, chip, c, ssem, rsem):
        p, npieces, cnt = (*piece, 1)[:3]
        rows = R // npieces * cnt
        r0 = p * (R // npieces)
        if kind == "col":
            win = src.at[pl.ds(r0, rows), pl.ds(pl.multiple_of(j * C, LANES), C)]
        else:
            win = src.at[pl.ds(pl.multiple_of(j * R + r0, 16), rows), :]
        return pltpu.make_async_remote_copy(win, dst.at[q, pl.ds(r0, rows), :], ssem, rsem,
                                            device_id=(*chip, c), device_id_type=MESH)

    def copies(ins, outs, ssem, rsem, waiting):
        x, y, c, chips = _place()
        return [copy(ins[w], outs[w], kinds[w], *pdims[w], q, 2 * chip[0] + chip[1], chip, c,
                     ssem.at[3 * w + q], rsem.at[3 * w + q])
                for w in range(n) for q, chip in enumerate(chips)]

    shapes = [jax.ShapeDtypeStruct((3,) + d, BF16) for d in pdims]
    if lands is None:
        return _Cargo(parts, shapes, 3 * n, copies)
    return _Cargo(list(parts) + list(lands), shapes, 3 * n, copies, alias={n + w: w for w in range(n)})


def _ag_fill(full, shard, kind, *, name):
    R, C = shard.shape
    RH = R // 2
    tm = _tile(RH, max(16, (3 * 2 ** 19 // C) // 16 * 16), 16)
    nbh = RH // tm
    n, nown = 3 * nbh, 2 * nbh
    xyc = jnp.stack([lax.axis_index("x"), lax.axis_index("y"), lax.axis_index("c")]).astype(jnp.int32)

    def chip_of(q, x, y):
        keep_x, keep_y = q % 2, (2 - q) // 2
        return 2 * ((1 - x) + keep_x * (2 * x - 1)) + (1 - y) + keep_y * (2 * y - 1)

    def window(ref, j, row0):
        if kind == "col":
            return ref.at[pl.ds(pl.multiple_of(row0, 16), tm), pl.ds(pl.multiple_of(j * C, LANES), C)]
        return ref.at[pl.ds(pl.multiple_of(j * R + row0, 16), tm), :]

    def body(p_ref, send_ref, own_ref, out_ref, land, ssem, rsem, wsem):
        i = pl.program_id(0)
        x, y = lax.axis_index("x"), lax.axis_index("y")
        c, peer = _pair_peer()
        own_cp = pltpu.make_async_copy(own_ref, window(out_ref, 2 * x + y, jnp.minimum(i, nown - 1) * tm), wsem.at[1])

        @pl.when(i < nown)
        def _():
            own_cp.start()

        def consume(land, ps):
            b = i - 1
            cp = pltpu.make_async_copy(land.at[ps], window(out_ref, chip_of(b // nbh, x, y),
                                                           (1 - c) * RH + (b % nbh) * tm), wsem.at[0])
            cp.start()
            cp.wait()

        _exchange_step(i, n, send_ref, land, ssem, rsem, None, peer, consume)

        @pl.when(i < nown)
        def _():
            own_cp.wait()

    now = lambda i: jnp.minimum(i, n - 1)

    def send_map(i, p):
        q, b = now(i) // nbh, now(i) % nbh
        j = chip_of(q, p[0], p[1])
        return (p[2] * nbh + b, j) if kind == "col" else ((j * 2 + p[2]) * nbh + b, 0)

    grid_spec = pltpu.PrefetchScalarGridSpec(
        num_scalar_prefetch=1, grid=(n + 1,),
        in_specs=[pl.BlockSpec((tm, C), send_map),
                  pl.BlockSpec((tm, C), lambda i, p: (jnp.minimum(i, nown - 1), 0))],
        out_specs=pl.BlockSpec(memory_space=pl.ANY),
        scratch_shapes=[pltpu.VMEM((n, tm, C), BF16), pltpu.SemaphoreType.DMA((n,)), pltpu.SemaphoreType.DMA((n,)),
                        pltpu.SemaphoreType.DMA((2,))])
    return _pc(body, name=name, out_shape=jax.ShapeDtypeStruct(full.shape, BF16), grid_spec=grid_spec,
               input_output_aliases={1: 0}, compiler_params=_cp("arbitrary"))(xyc, full, shard)


def _pair_peer():
    x, y, c = lax.axis_index("x"), lax.axis_index("y"), lax.axis_index("c")
    return c, (x, y, 1 - c)


def _exchange_step(i, n, src_ref, land, ssem, rsem, credit, peer, consume):
    def rdma(slot):
        return pltpu.make_async_remote_copy(src_ref, land.at[slot], ssem.at[slot], rsem.at[slot],
                                            device_id=peer, device_id_type=MESH)

    reuse = credit is not None
    slot = i % 2 if reuse else jnp.minimum(i, n - 1)

    @pl.when(i < n)
    def _():
        if reuse:
            @pl.when(i >= 2)
            def _():
                pl.semaphore_wait(credit, 1)

        rdma(slot).start()

    @pl.when(i >= 1)
    def _():
        ps = (i - 1) % 2 if reuse else i - 1
        rdma(ps).wait_recv()
        consume(land, ps)

        if reuse:
            @pl.when(i - 1 < n - 2)
            def _():
                pl.semaphore_signal(credit, 1, device_id=peer, device_id_type=MESH)

    @pl.when(i < n)
    def _():
        rdma(slot).wait_send()


def _xchg_scratch(tm, C, dtype):
    return [pltpu.VMEM((2, tm, C), dtype), pltpu.SemaphoreType.DMA((2,)), pltpu.SemaphoreType.DMA((2,)),
            pltpu.SemaphoreType.REGULAR]


def _rs_pair_sum(g, kind, *, name):
    K, N = g.shape
    NS, RH = (1, K // 2) if kind == "col" else (4, K // 8)
    tm = _tile(RH, max(16, (3 * 2 ** 19 // N) // 16 * 16), 16)
    if NS * (RH // tm) < 2:
        tm = RH // 2
    nb = RH // tm
    n = NS * nb
    cvec = lax.axis_index("c").reshape(1).astype(jnp.int32)

    def body(c_ref, send_ref, own_ref, out_ref, land, ssem, rsem, credit):
        _, peer = _pair_peer()

        def consume(land, ps):
            out_ref[...] = (own_ref[...].astype(F32) + land[ps].astype(F32)).astype(BF16)

        _exchange_step(pl.program_id(0), n, send_ref, land, ssem, rsem, credit, peer, consume)

    now = lambda i: jnp.minimum(i, n - 1)
    lag = lambda i: jnp.maximum(i - 1, 0)
    grid_spec = pltpu.PrefetchScalarGridSpec(
        num_scalar_prefetch=1, grid=(n + 1,),
        in_specs=[pl.BlockSpec((tm, N), lambda i, c: ((now(i) // nb * 2 + 1 - c[0]) * nb + now(i) % nb, 0)),
                  pl.BlockSpec((tm, N), lambda i, c: ((lag(i) // nb * 2 + c[0]) * nb + lag(i) % nb, 0))],
        out_specs=pl.BlockSpec((tm, N), lambda i, c: (lag(i), 0)),
        scratch_shapes=_xchg_scratch(tm, N, BF16))
    return _pc(body, name=name, out_shape=jax.ShapeDtypeStruct((NS * RH, N), BF16), grid_spec=grid_spec,
               compiler_params=_cp("arbitrary"))(cvec, g, g)


def _rs_sum_join(part, got, kind, *, name):
    _, hr, hc = got.shape
    tm = _tile(hr, max(16, (3 * 2 ** 18 // hc) // 16 * 16), 16)
    if hr // tm < 2:
        tm = hr // 2
    nb = hr // tm
    jvec = (2 * lax.axis_index("x") + lax.axis_index("y")).reshape(1).astype(jnp.int32)

    def body(j_ref, p_ref, a_ref, b_ref, c_ref, mine_ref, theirs_ref, sbuf, land, ssem, rsem, credit):
        i = pl.program_id(0)
        _, peer = _pair_peer()

        @pl.when(i < nb)
        def _():
            v = ((p_ref[...].astype(F32) + a_ref[...].astype(F32)) + b_ref[...].astype(F32)) + c_ref[...].astype(F32)
            mine_ref[...] = v
            sbuf[...] = v

        def consume(land, ps):
            theirs_ref[...] = land[ps]

        _exchange_step(i, nb, sbuf, land, ssem, rsem, credit, peer, consume)

    now = lambda i: jnp.minimum(i, nb - 1)
    lag = lambda i: jnp.maximum(i - 1, 0)
    own_spec = (pl.BlockSpec((tm, hc), lambda i, j: (now(i), j[0])) if kind == "col"
                else pl.BlockSpec((tm, hc), lambda i, j: (j[0] * nb + now(i), 0)))
    got_spec = lambda q: pl.BlockSpec((None, tm, hc), functools.partial(lambda i, j, q: (q, now(i), 0), q=q))
    grid_spec = pltpu.PrefetchScalarGridSpec(
        num_scalar_prefetch=1, grid=(nb + 1,),
        in_specs=[own_spec, got_spec(0), got_spec(1), got_spec(2)],
        out_specs=(pl.BlockSpec((tm, hc), lambda i, j: (now(i), 0)), pl.BlockSpec((tm, hc), lambda i, j: (lag(i), 0))),
        scratch_shapes=[pltpu.VMEM((tm, hc), F32)] + _xchg_scratch(tm, hc, F32))
    shp = jax.ShapeDtypeStruct((hr, hc), F32)
    return _pc(body, name=name, out_shape=(shp, shp), grid_spec=grid_spec,
               compiler_params=_cp("arbitrary"))(jvec, part, got, got, got)


def _adamw_join(w, mine, theirs, m, v, *, name):
    R2, wc = w.shape
    hr, hc = mine.shape
    assert hr < R2 <= 2 * hr and hc >= wc
    pref, rows8 = max(8, 2 ** 18 // hc), _round_up(R2, 8)
    tm = max(t for t in range(8, hr + 1, 8) if hr % t == 0 and rows8 % t == 0 and (t <= pref or t == 8))
    nbh = hr // tm
    cvec = lax.axis_index("c").reshape(1).astype(jnp.int32)

    def body(c_ref, w_ref, a_ref, b_ref, m_ref, v_ref, g_ref, d_ref, m2_ref, v2_ref):
        gv = jnp.where(pl.program_id(0) // nbh == c_ref[0], a_ref[...], b_ref[...])[:, :wc]
        m2 = ADAM_B1 * m_ref[...] + (1.0 - ADAM_B1) * gv
        v2 = ADAM_B2 * v_ref[...] + (1.0 - ADAM_B2) * (gv * gv)
        m_hat = m2 / (1.0 - ADAM_B1 ** ADAM_STEP)
        v_hat = v2 / (1.0 - ADAM_B2 ** ADAM_STEP)
        g_ref[...] = gv
        d_ref[...] = -ADAM_LR * (m_hat / (jnp.sqrt(v_hat) + ADAM_EPS) + ADAM_WD * w_ref[...])
        m2_ref[...] = m2
        v2_ref[...] = v2

    wspec = pl.BlockSpec((tm, wc), lambda i, c: (i, 0))
    mine_spec = pl.BlockSpec((tm, hc), lambda i, c: (jnp.where(i // nbh == c[0], i % nbh, 0), 0))
    theirs_spec = pl.BlockSpec((tm, hc), lambda i, c: (jnp.where(i // nbh == c[0], 0, i % nbh), 0))
    grid_spec = pltpu.PrefetchScalarGridSpec(
        num_scalar_prefetch=1, grid=(pl.cdiv(R2, tm),),
        in_specs=[wspec, mine_spec, theirs_spec, wspec, wspec], out_specs=(wspec,) * 4)
    return _pc(body, name=name, out_shape=(jax.ShapeDtypeStruct((R2, wc), F32),) * 4, grid_spec=grid_spec,
               compiler_params=_cp("parallel"))(cvec, w, mine, theirs, m, v)


def _all_reduce_small(pack):
    R = pack.shape[0]

    def body(p_ref, o_ref, buf, send, recv):
        x, y, c, _ = _place()
        me = 4 * x + 2 * y + c
        buf[0] = p_ref[...]
        cps = []
        for r in range(1, 8):
            peer = (x if r & 4 == 0 else 1 - x, y if r & 2 == 0 else 1 - y, c if r & 1 == 0 else 1 - c)
            cp = pltpu.make_async_remote_copy(p_ref, buf.at[r], send.at[r - 1], recv.at[r - 1],
                                              device_id=peer, device_id_type=MESH)
            cp.start()
            cps.append(cp)
        for cp in cps:
            cp.wait()
        acc = buf[jnp.bitwise_xor(me, 0)]
        for k in range(1, 8):
            acc = acc + buf[jnp.bitwise_xor(me, k)]
        o_ref[...] = acc

    return _pc(body, name="allreduce_small", out_shape=jax.ShapeDtypeStruct((R, LANES), F32),
               in_specs=[pl.BlockSpec(memory_space=pltpu.VMEM)], out_specs=pl.BlockSpec(memory_space=pltpu.VMEM),
               scratch_shapes=[pltpu.VMEM((8, R, LANES), F32), pltpu.SemaphoreType.DMA((7,)),
                               pltpu.SemaphoreType.DMA((7,))])(pack)


def _pack_small(arrs):
    parts, sizes = [], []
    for a in arrs:
        f = a.reshape(-1).astype(F32)
        n = _round_up(f.shape[0], 8 * LANES)
        parts.append(jnp.pad(f, (0, n - f.shape[0])).reshape(n // LANES, LANES))
        sizes.append((a.shape, f.shape[0], n // LANES))
    return jnp.concatenate(parts, axis=0), sizes


def _unpack_small(pack, sizes):
    out, r0 = [], 0
    for shape, cnt, rows in sizes:
        out.append(pack[r0:r0 + rows].reshape(-1)[:cnt].reshape(shape))
        r0 += rows
    return out


def _ffn_fwd(xin, norm, weight, tag, gather_on_gateup, gather_on_down, gather_on_norm=lambda: (None, None)):
    cargo, landed = gather_on_norm()
    hb = _rms_fwd(xin, norm, name=tag + "_norm", cargo=cargo)
    if cargo is not None:
        hb, lands = hb
        landed(lands)
    cargo, landed = gather_on_gateup()
    res = _gateup(hb, weight("w_gate_up"), name=tag + "_gateup", tm=1024, cargo=cargo)
    if cargo is not None:
        res, lands = res
        landed(lands)
    gate, up, act = res
    cargo, landed = gather_on_down()
    xout = _mm(act, weight("w_down"), res=xin, scale=0.5, name=tag + "_down", tm=1024, tn=512, tk=8192, cargo=cargo)
    if cargo is not None:
        xout, lands = xout
        landed(lands)
    return xout, (hb, gate, up, act)


def _ffn_bwd(dy, dyb, xin, norm, wgu, wd, saved, tag, scatter, gu_piece=(0, 1), rest_on_dnorm=False):
    hb, gate, up, act = saved
    d_wd = _mm(act, dyb, mode="tn", scale=0.5, out_dtype=BF16, name=tag + "_dwd", tm=512, tn=1024, tk=4096)
    cargo = scatter(tag + "_w_down", d_wd)
    rest, rest_done = scatter.leftover()
    dgu = _dact_swiglu(dyb, wd, gate, up, name=tag + "_dact", cargo=rest)
    if rest is not None:
        dgu, lands = dgu
        rest_done(lands)
    d_wgu, lands = _mm(hb, dgu, mode="tn", halves="b", out_dtype=BF16, name=tag + "_dwgu", tm=1024, tn=512, tk=4096,
                       cargo=cargo)
    scatter.done(lands)
    cargo = scatter(tag + "_w_gate_up", d_wgu, piece=gu_piece)
    dh, lands = _mm(dgu, wgu, mode="nt", halves="a", out_dtype=BF16, name=tag + "_dh", tm=1024, tn=512, tk=5632,
                    cargo=cargo)
    scatter.done(lands)
    rest, rest_done = scatter.leftover() if rest_on_dnorm else (None, None)
    res = _rms_bwd(xin, dh, norm, dy, name=tag + "_dnorm", cargo=rest)
    if rest is not None:
        res, lands = res
        rest_done(lands)
    return res


def kernel(x, mem, ffn1_norm, ffn1_w_gate_up, ffn1_w_down, mix_norm, mem_norm, w_in, b_forget, pool_w, pool_scale, w_pool_up, fox_q_norm, fox_k_norm, w_fox_o, w_mem_kv, mem_q_norm, mem_k_norm, w_mem_o, w_out, ffn2_norm, ffn2_w_gate_up, ffn2_w_down, loss_target, m_ffn1_norm, m_ffn1_w_gate_up, m_ffn1_w_down, m_mix_norm, m_mem_norm, m_w_in, m_b_forget, m_pool_w, m_pool_scale, m_w_pool_up, m_fox_q_norm, m_fox_k_norm, m_w_fox_o, m_w_mem_kv, m_mem_q_norm, m_mem_k_norm, m_w_mem_o, m_w_out, m_ffn2_norm, m_ffn2_w_gate_up, m_ffn2_w_down, v_ffn1_norm, v_ffn1_w_gate_up, v_ffn1_w_down, v_mix_norm, v_mem_norm, v_w_in, v_b_forget, v_pool_w, v_pool_scale, v_w_pool_up, v_fox_q_norm, v_fox_k_norm, v_w_fox_o, v_w_mem_kv, v_mem_q_norm, v_mem_k_norm, v_w_mem_o, v_w_out, v_ffn2_norm, v_ffn2_w_gate_up, v_ffn2_w_down):
    P = dict(locals())
    big = ["ffn1_w_gate_up", "ffn1_w_down", "w_in", "w_pool_up", "w_fox_o", "w_mem_kv", "w_mem_o", "w_out",
           "ffn2_w_gate_up", "ffn2_w_down"]
    kinds = ["col", "row", "col", "col", "col", "row", "col", "row", "col", "row"]
    small = ["ffn1_norm", "mix_norm", "mem_norm", "b_forget", "pool_w", "pool_scale", "fox_q_norm", "fox_k_norm",
             "mem_q_norm", "mem_k_norm", "ffn2_norm"]
    order = ["ffn1_norm", "ffn1_w_gate_up", "ffn1_w_down", "mix_norm", "mem_norm", "w_in", "b_forget", "pool_w",
             "pool_scale", "w_pool_up", "fox_q_norm", "fox_k_norm", "w_fox_o", "w_mem_kv", "mem_q_norm", "mem_k_norm",
             "w_mem_o", "w_out", "ffn2_norm", "ffn2_w_gate_up", "ffn2_w_down"]

    nb, S, D = x.shape
    T = nb * S
    ML = mem.shape[1]
    NF, HD = b_forget.shape[-1], fox_q_norm.shape[-1]
    FW = NF * HD
    MW, MHD = w_mem_o.shape[1], mem_q_norm.shape[-1]
    PG, PD = pool_w.shape[1], pool_w.shape[2]
    PW = PG * PD
    n_in = w_in.shape[-1]
    n_in_pad = _round_up(n_in, LANES)
    in_w = 4 * n_in
    assert HD * 2 == LANES and MHD == LANES and PD == LANES and in_w == PW + 3 * FW + NF + MW + 3 * D

    shards = []
    for nme in big:
        wl = P[nme][0].astype(BF16)
        if nme == "w_in":
            wl = jnp.pad(wl, ((0, 0), (0, n_in_pad - n_in)))
        shards.append(wl)
    shard, kind_of, full = dict(zip(big, shards)), dict(zip(big, kinds)), {}

    def landed(names):
        def fill(lands):
            for nme, l in zip(names, lands):
                full[nme] = _ag_fill(l, shard[nme], kind_of[nme], name="ag_fill_" + nme)
        return fill

    def gather(names, piece=(0, 1), lands=None):
        return _ag_cargo([shard[nme] for nme in names], [kind_of[nme] for nme in names], piece, lands)

    landed(["ffn1_w_gate_up"])(_run_cargo(gather(["ffn1_w_gate_up"]), name="ag_first"))
    begun = {}

    def on_norm1():
        return gather(["ffn1_w_down"], (0, 4)), lambda lands: begun.update(ffn1_w_down=lands[0])

    def on_gateup1():
        def fill(lands):
            landed(["ffn1_w_down"])(lands[:1])
            begun["w_in"] = lands[1]
        return _join_cargo(gather(["ffn1_w_down"], (1, 4, 3), [begun["ffn1_w_down"]]),
                           gather(["w_in"], (0, 2))), fill

    x0 = x.reshape(T, D)
    x1, sv1 = _ffn_fwd(x0, ffn1_norm, lambda nme: full["ffn1_" + nme], "ffn1", on_gateup1,
                       lambda: (gather(["w_in"], (1, 2), [begun["w_in"]]), landed(["w_in"])), on_norm1)
    def gathered_cols(a, b):
        out = []
        for j in range(4):
            lo, hi = max(a, j * n_in), min(b, (j + 1) * n_in)
            if lo < hi:
                out.append(full["w_in"][:, j * (n_in_pad - n_in) + lo:j * (n_in_pad - n_in) + hi])
        return out

    o_q, o_k, o_v, o_f = PW, PW + FW, PW + 2 * FW, PW + 3 * FW
    o_qm, o_gate = o_f + NF, o_f + NF + MW
    p_q, p_k, p_v, p_pool, p_qm, p_f = 3 * D, 3 * D + FW, 3 * D + 2 * FW, 3 * D + 3 * FW, 3 * D + 3 * FW + PW, 3 * D + 3 * FW + PW + MW
    inp = _round_up(p_f + LANES, 512)
    win_p = jnp.concatenate(gathered_cols(o_gate, in_w) + gathered_cols(o_q, o_f) + gathered_cols(0, PW)
                            + gathered_cols(o_qm, o_gate) + gathered_cols(o_f, o_qm)
                            + [jnp.zeros((D, inp - p_f - NF), BF16)], axis=1)

    hb2 = _rms_fwd(x1, mix_norm, name="mix_norm")
    later = ["ffn2_w_down", "w_fox_o", "w_mem_kv", "w_pool_up"]
    z, lands = _mm(hb2, win_p, name="in_proj", tm=1024, tn=1536, cargo=gather(later))
    landed(later)(lands)

    vpool = _pool_fwd(z, p_pool // LANES, pool_w[0], pool_scale, nb, S, name="pool_fwd")
    y_pool = _mm(vpool, full["w_pool_up"], out_dtype=BF16, name="pool_up", tm=1024, tn=512)

    b128 = jnp.pad(b_forget, ((0, 0), (0, LANES - NF)))
    c3 = _fox_c(z, p_f // LANES, b128, nb, S, NF, name="fox_c")
    later = ["w_out", "w_mem_o"]
    (qa, ka, vb), lands = _fox_qk(z, p_q // LANES, p_k // LANES, p_v // LANES, c3, fox_q_norm, fox_k_norm, HD, NF,
                                  name="fox_qk", cargo=gather(later))
    landed(later)(lands)
    (o_fox, qb), lands = _fox_fwd(qa, ka, vb, nb, S, HD, name="fox_fwd",
                                  cargo=gather(["ffn2_w_gate_up"], (0, 4, 3)))
    begun["ffn2_w_gate_up"] = lands[0]
    y_fox = _mm(o_fox, full["w_fox_o"], out_dtype=BF16, name="fox_out", tm=1024, tn=512)

    mem2 = mem.reshape(nb * ML, D)
    memn = _rms_fwd(mem2, mem_norm, name="mem_norm")
    kv = _mm(memn, full["w_mem_kv"], name="mem_kv")
    kmn = _headnorm_fwd(kv, 0, MW, mem_k_norm, MHD, name="mem_knorm")
    vmb = kv[:, MW:].astype(BF16)
    qmn = _headnorm_fwd(z, p_qm // LANES, MW, mem_q_norm, MHD, name="mem_qnorm")
    o_mem = _mem_fwd(qmn, kmn, vmb, nb, S, ML, name="mem_fwd")
    y_mem = _mm(o_mem, full["w_mem_o"], out_dtype=BF16, name="mem_out", tm=1024, tn=512)

    merged, lands = _merge_fwd(z, y_pool, y_fox, y_mem, name="merge_fwd",
                               cargo=gather(["ffn2_w_gate_up"], (3, 4), [begun["ffn2_w_gate_up"]]))
    landed(["ffn2_w_gate_up"])(lands)
    x2 = _mm(merged, full["w_out"], res=x1, name="out_proj", tm=1024, tn=512)
    x3, sv2 = _ffn_fwd(x2, ffn2_norm, lambda nme: full["ffn2_" + nme], "ffn2", lambda: (None, None),
                       lambda: (None, None))
    dy3, dy3b, loss_part = _loss(x3, loss_target.reshape(T, D), name="loss")

    g, parts, got = {}, {}, {}
    rs_kind = dict(kind_of, w_in="row")

    leftovers = []

    def scatter(*named_grads, piece=(0, 1)):
        names = named_grads[0::2]
        for nme, grad in zip(names, named_grads[1::2]):
            parts[nme] = _rs_pair_sum(grad, rs_kind[nme], name="rs_pair_" + nme)
        scatter.names = names
        sent = (*piece, 1)[2]
        if sent < piece[1]:
            leftovers.append((names, (sent, piece[1], piece[1] - sent)))
        return _scatter_cargo([parts[nme] for nme in names], [rs_kind[nme] for nme in names], piece)

    def scattered(lands):
        got.update(zip(scatter.names, lands))

    def leftover():
        if not leftovers:
            return None, None
        names, piece = leftovers.pop(0)
        cargo = _scatter_cargo([parts[nme] for nme in names], [rs_kind[nme] for nme in names], piece,
                               [got[nme] for nme in names])
        return cargo, lambda lands: got.update(zip(names, lands))

    scatter.done, scatter.leftover = scattered, leftover
    dx2, dx2b, g["ffn2_norm"] = _ffn_bwd(dy3, dy3b, x2, ffn2_norm, full["ffn2_w_gate_up"], full["ffn2_w_down"], sv2,
                                         "ffn2", scatter, gu_piece=(0, 2))

    dmerged = _mm(dx2b, full["w_out"], mode="nt", out_dtype=BF16, name="d_merged", tm=1024, tn=512)
    g["w_out"] = _mm(merged, dx2b, mode="tn", out_dtype=BF16, name="d_w_out", tk=4096)
    dyp, dyf, dym, dgl = _merge_bwd(dmerged, z, y_pool, y_fox, y_mem, name="merge_bwd")

    g["w_pool_up"] = _mm(vpool, dyp, mode="tn", out_dtype=BF16, name="d_w_pool_up", tk=4096)
    dvp = _mm(dyp, full["w_pool_up"], mode="nt", name="d_vpool", tm=1024)
    du_pool, g["pool_w"], g["pool_scale"] = _pool_bwd(z, p_pool // LANES, dvp, pool_w[0], pool_scale, nb, S,
                                                      name="pool_bwd")

    g["w_fox_o"] = _mm(o_fox, dyf, mode="tn", out_dtype=BF16, name="d_w_fox_o", tk=4096)
    do_fox = _mm(dyf, full["w_fox_o"], mode="nt", out_dtype=BF16, name="d_o_fox", tm=1024)
    rest, rest_done = leftover()
    (dqa, dka, dvf, dcs), lands = _fox_bwd(qb, ka, vb, do_fox, nb, S, HD, name="fox_bwd", cargo=rest)
    rest_done(lands)
    dq, dgq = _headnorm_bwd(z, p_q // LANES, FW, dqa, fox_q_norm, HD, HD ** -0.5, True, BF16, name="fox_dq")
    dk, dgk = _headnorm_bwd(z, p_k // LANES, FW, dka, fox_k_norm, HD, 1.0, True, BF16, name="fox_dk")
    g["fox_q_norm"], g["fox_k_norm"] = _fold_gain(dgq, HD), _fold_gain(dgk, HD)
    dc = -dcs[:, :, :LANES // HD, :].reshape(nb, NF, S).transpose(0, 2, 1).reshape(T, NF)
    df, db = _fox_c_bwd(jnp.pad(dc, ((0, 0), (0, LANES - NF))), z, p_f // LANES, b128, nb, S, name="fox_c_bwd")
    g["b_forget"] = db[:, :NF]

    g["w_mem_o"] = _mm(o_mem, dym, mode="tn", out_dtype=BF16, name="d_w_mem_o", tk=4096)
    do_mem = _mm(dym, full["w_mem_o"], mode="nt", out_dtype=BF16, name="d_o_mem", tm=1024)
    dqmn, dkmn, dvm = _mem_bwd(qmn, kmn, vmb, do_mem, nb, S, ML, name="mem_bwd")
    dqm, g["mem_q_norm"] = _headnorm_bwd(z, p_qm // LANES, MW, dqmn, mem_q_norm, MHD, 1.0, False, BF16, name="mem_dq")
    dkm, g["mem_k_norm"] = _headnorm_bwd(kv, 0, MW, dkmn, mem_k_norm, MHD, 1.0, False, BF16, name="mem_dk")
    dkv = jnp.concatenate([dkm, dvm.astype(BF16)], axis=1)
    g["w_mem_kv"] = _mm(memn, dkv, mode="tn", out_dtype=BF16, name="d_w_mem_kv")
    dmemn = _mm(dkv, full["w_mem_kv"], mode="nt", name="d_memn")
    _, _, g["mem_norm"] = _rms_bwd(mem2, dmemn, mem_norm, jnp.zeros_like(mem2), name="mem_dnorm")

    dz = jnp.concatenate([dgl, dq, dk, dvf.astype(BF16), du_pool, dqm, df.astype(BF16),
                          jnp.zeros((T, inp - p_f - LANES), BF16)], axis=1)
    cargo = scatter(*sum([[nme, g[nme]] for nme in ("w_out", "w_pool_up", "w_fox_o", "w_mem_o", "w_mem_kv")], []))
    dwin_p, lands = _mm(dz, hb2, mode="tn", out_dtype=BF16, name="d_w_in", tm=512, tn=1024, tk=4096, cargo=cargo)
    scattered(lands)
    dwin_o = jnp.concatenate([dwin_p[p_pool:p_qm], dwin_p[p_q:p_pool], dwin_p[p_f:p_f + NF], dwin_p[p_qm:p_f],
                              dwin_p[:p_q]], axis=0)
    cargo = scatter("w_in", jnp.concatenate(
        [jnp.pad(dwin_o[j * n_in:(j + 1) * n_in], ((0, n_in_pad - n_in), (0, 0))) for j in range(4)], axis=0),
        piece=(0, 2))
    dh2, lands = _mm(dz, win_p, mode="nt", out_dtype=BF16, name="d_h2", tm=1024, tn=512, tk=3584, cargo=cargo)
    scattered(lands)
    dx1, dx1b, g["mix_norm"] = _rms_bwd(x1, dh2, mix_norm, dx2, name="mix_dnorm")

    dx0, _, g["ffn1_norm"] = _ffn_bwd(dx1, dx1b, x0, ffn1_norm, full["ffn1_w_gate_up"], full["ffn1_w_down"], sv1,
                                      "ffn1", scatter, gu_piece=(0, 4, 3), rest_on_dnorm=True)

    halves = {nme: _rs_sum_join(parts[nme], got[nme], rs_kind[nme], name="rs_join_" + nme) for nme in big}
    gfull = {}

    gpack, sizes = _pack_small([g[nme].reshape(P[nme].shape) for nme in small] + [loss_part[:, :1]])
    gsum = _unpack_small(_all_reduce_small(gpack), sizes)
    loss = gsum[-1].reshape(())
    for nme, a in zip(small, gsum[:-1]):
        gfull[nme] = a

    delta, new_m, new_v = {}, {}, {}
    for nme in big:
        shp = P[nme].shape
        two = (lambda a: a.reshape(shp[-2:]).T) if nme == "w_in" else (lambda a: a.reshape(shp[-2:]))
        back = (lambda a: a.T.reshape(shp)) if nme == "w_in" else (lambda a: a.reshape(shp))
        gfull[nme], delta[nme], new_m[nme], new_v[nme] = [
            back(a) for a in _adamw_join(two(P[nme]), *halves[nme], two(P["m_" + nme]), two(P["v_" + nme]),
                                         name="adamw_" + nme)]
    wpack, _ = _pack_small([P[nme] for nme in small])
    mpack, _ = _pack_small([P["m_" + nme] for nme in small])
    vpack, _ = _pack_small([P["v_" + nme] for nme in small])
    gpack2, ssz = _pack_small([gfull[nme] for nme in small])
    for dct, pk in zip((delta, new_m, new_v), _adamw(wpack, gpack2, mpack, vpack, name="adamw_small")):
        for nme, a in zip(small, _unpack_small(pk, ssz)):
            dct[nme] = a

    grad_x = dx0.reshape(nb, S, D)
    return (loss, grad_x, *[gfull[nme] for nme in order], *[delta[nme] for nme in order],
            *[new_m[nme] for nme in order], *[new_v[nme] for nme in order])
```

```python
import functools

import jax
import jax.numpy as jnp
from jax import lax
from jax.experimental import pallas as pl
from jax.experimental.pallas import tpu as pltpu

F32 = jnp.float32
BF16 = jnp.bfloat16
MESH = pl.DeviceIdType.MESH
EPS = 1e-6
POOL_WINDOWS = (2, 4, 8, 16)
ADAM_LR, ADAM_B1, ADAM_B2, ADAM_EPS, ADAM_WD, ADAM_STEP = 0.001, 0.9, 0.999, 1e-08, 0.01, 10
LANES = 128
VMEM_LIMIT = 56 * 1024 * 1024
NEG = -1e30


def _pc(body, *, name, cargo=None, **kw):
    if cargo is not None:
        body, kw = _load_cargo(body, cargo, kw)
    return pl.pallas_call(body, name=name, **kw)


def _cp(*sem):
    return pltpu.CompilerParams(dimension_semantics=sem, vmem_limit_bytes=VMEM_LIMIT)


def _tile(n, pref, mult=LANES):
    if n <= pref:
        return n
    t = (pref // mult) * mult
    while t >= mult:
        if n % t == 0:
            return t
        t -= mult
    return n


def _round_up(n, m):
    return (n + m - 1) // m * m


_DIMS = {"nn": (((1,), (0,)), ((), ())), "nt": (((1,), (1,)), ((), ())), "tn": (((0,), (0,)), ((), ()))}


def _split_cargo(res, cargo):
    if cargo is None:
        return res
    nco = len(cargo.out_shapes)
    own = res[:len(res) - nco]
    return (own[0] if len(own) == 1 else own), list(res[len(res) - nco:])


def _mm(a, b, *, name, mode="nn", out_dtype=F32, scale=1.0, res=None, tm=512, tn=512, tk=2048, cargo=None,
        halves=None):
    if halves == "a":
        assert mode == "nt"
        (_, M, Kh), (N, K2) = a.shape, b.shape
        K = 2 * Kh
    elif halves == "b":
        assert mode == "tn"
        (K, M), (_, K2, Nh) = a.shape, b.shape
        N = 2 * Nh
    elif mode == "nn":
        (M, K), (K2, N) = a.shape, b.shape
    elif mode == "nt":
        (M, K), (N, K2) = a.shape, b.shape
    else:
        (K, M), (K2, N) = a.shape, b.shape
    assert K == K2, (name, a.shape, b.shape)
    tm = _tile(M, tm, 8 if M % LANES else LANES)
    tn = _tile(N // 2 if halves == "b" else N, tn)
    tk = _tile(K // 2 if halves == "a" else K, tk)
    nk = K // tk
    dims = _DIMS[mode]
    if halves == "a":
        a_spec = pl.BlockSpec((None, tm, tk), lambda i, j, k: (k // (nk // 2), i, k % (nk // 2)))
    elif mode == "tn":
        a_spec = pl.BlockSpec((tk, tm), lambda i, j, k: (k, i))
    else:
        a_spec = pl.BlockSpec((tm, tk), lambda i, j, k: (i, k))
    if halves == "b":
        njh = N // 2 // tn
        b_spec = pl.BlockSpec((None, tk, tn), lambda i, j, k: (j // njh, k, j % njh))
    elif mode == "nt":
        b_spec = pl.BlockSpec((tn, tk), lambda i, j, k: (j, k))
    else:
        b_spec = pl.BlockSpec((tk, tn), lambda i, j, k: (k, j))
    o_spec = pl.BlockSpec((tm, tn), lambda i, j, k: (i, j))
    has_res = res is not None

    def body(*refs):
        a_ref, b_ref = refs[:2]
        r_ref = refs[2] if has_res else None
        o_ref = refs[3] if has_res else refs[2]

        def finish(v):
            v = v * scale
            if has_res:
                v = r_ref[...] + v
            o_ref[...] = v.astype(o_ref.dtype)

        prod = lax.dot_general(a_ref[...], b_ref[...], dims, preferred_element_type=F32)
        if nk == 1:
            finish(prod)
            return
        acc, k = refs[-1], pl.program_id(2)

        @pl.when(k == 0)
        def _():
            acc[...] = prod

        @pl.when(k > 0)
        def _():
            acc[...] += prod

        @pl.when(k == nk - 1)
        def _():
            finish(acc[...])

    out = _pc(body, name=name, cargo=cargo, out_shape=jax.ShapeDtypeStruct((M, N), out_dtype),
              grid=(M // tm, N // tn, nk),
              in_specs=[a_spec, b_spec] + ([o_spec] if has_res else []), out_specs=o_spec,
              scratch_shapes=[] if nk == 1 else [pltpu.VMEM((tm, tn), F32)],
              compiler_params=_cp("parallel", "parallel", "arbitrary"))(
                  *((a, b, res) if has_res else (a, b)), *(cargo.arrays if cargo else ()))
    return _split_cargo(out, cargo)


def _dact_swiglu(dyb, wd, gate, up, *, name, tm=1024, tn=512, cargo=None):
    M, K = dyb.shape
    Fh = wd.shape[0]
    tm, tn = _tile(M, tm), _tile(Fh, tn)

    def body(dy_ref, w_ref, g_ref, u_ref, o_ref):
        d = lax.dot_general(dy_ref[...], w_ref[...], _DIMS["nt"], preferred_element_type=F32) * 0.5
        g, u = g_ref[...].astype(F32), u_ref[...].astype(F32)
        s = jax.nn.sigmoid(g)
        o_ref[0] = (d * u * (s * (1.0 + g * (1.0 - s)))).astype(BF16)
        o_ref[1] = (d * (g * s)).astype(BF16)

    t_spec = pl.BlockSpec((tm, tn), lambda i, j: (i, j))
    out = _pc(body, name=name, cargo=cargo, out_shape=jax.ShapeDtypeStruct((2, M, Fh), BF16),
              grid=(M // tm, Fh // tn),
              in_specs=[pl.BlockSpec((tm, K), lambda i, j: (i, 0)), pl.BlockSpec((tn, K), lambda i, j: (j, 0)),
                        t_spec, t_spec],
              out_specs=pl.BlockSpec((2, tm, tn), lambda i, j: (0, i, j)),
              compiler_params=_cp("parallel", "parallel"))(dyb, wd, gate, up, *(cargo.arrays if cargo else ()))
    return _split_cargo(out, cargo)


def _gateup(h, wgu, *, name, tm=512, tn=512, cargo=None):
    M, K = h.shape
    Fh = wgu.shape[1] // 2
    tm, tn = _tile(M, tm), _tile(Fh, tn)
    nj = Fh // tn

    def body(h_ref, wg_ref, wu_ref, g_ref, u_ref, a_ref):
        hv = h_ref[...]
        g = jnp.dot(hv, wg_ref[...], preferred_element_type=F32)
        u = jnp.dot(hv, wu_ref[...], preferred_element_type=F32)
        g_ref[...] = g.astype(BF16)
        u_ref[...] = u.astype(BF16)
        a_ref[...] = (g * jax.nn.sigmoid(g) * u).astype(BF16)

    o_spec = pl.BlockSpec((tm, tn), lambda i, j: (i, j))
    shp = jax.ShapeDtypeStruct((M, Fh), BF16)
    out = _pc(body, name=name, cargo=cargo, out_shape=(shp, shp, shp), grid=(M // tm, nj),
              in_specs=[pl.BlockSpec((tm, K), lambda i, j: (i, 0)),
                        pl.BlockSpec((K, tn), lambda i, j: (0, j)),
                        pl.BlockSpec((K, tn), lambda i, j: (0, j + nj))],
              out_specs=(o_spec, o_spec, o_spec),
              compiler_params=_cp("parallel", "parallel"))(h, wgu, wgu, *(cargo.arrays if cargo else ()))
    return _split_cargo(out, cargo)


def _rowmap(fn, ins, outs, *, rows, tm, name, accs=(), cargo=None):
    tm = _tile(rows, tm, 8)
    arrs, in_specs = [], []
    for d in ins:
        if d[0] == "row":
            _, a, w, cb = d
            w = a.shape[1] if w is None else w
            in_specs.append(pl.BlockSpec((tm, w), functools.partial(lambda i, cb: (i, cb), cb=cb)))
        else:
            a = d[1]
            in_specs.append(pl.BlockSpec(a.shape, functools.partial(lambda i, nd: (0,) * nd, nd=a.ndim)))
        arrs.append(a)
    out_shape = [jax.ShapeDtypeStruct((rows, w), dt) for w, dt in outs]
    out_specs = [pl.BlockSpec((tm, w), lambda i: (i, 0)) for w, _ in outs]
    for shp, dt in accs:
        out_shape.append(jax.ShapeDtypeStruct(shp, dt))
        out_specs.append(pl.BlockSpec(shp, functools.partial(lambda i, nd: (0,) * nd, nd=len(shp))))
    nin, nout, nacc = len(ins), len(outs), len(accs)

    def body(*refs):
        res = fn(*[r[...] for r in refs[:nin]])
        res = res if isinstance(res, (tuple, list)) else (res,)
        for r, v in zip(refs[nin:nin + nout], res[:nout]):
            r[...] = v.astype(r.dtype)
        if nacc:
            acc_refs = refs[nin + nout:]

            @pl.when(pl.program_id(0) == 0)
            def _():
                for r in acc_refs:
                    r[...] = jnp.zeros_like(r)

            for r, v in zip(acc_refs, res[nout:]):
                r[...] += v

    out = _pc(body, name=name, cargo=cargo, out_shape=tuple(out_shape), grid=(rows // tm,), in_specs=in_specs,
              out_specs=tuple(out_specs),
              compiler_params=_cp("arbitrary" if nacc else "parallel"))(*arrs, *(cargo.arrays if cargo else ()))
    return _split_cargo(out, cargo) if cargo else (out[0] if len(out) == 1 else out)


def _rms_fwd(x, g, *, name, cargo=None):
    def fn(xv, gv):
        return xv * lax.rsqrt(jnp.mean(xv * xv, axis=-1, keepdims=True) + EPS) * gv
    return _rowmap(fn, [("row", x, None, 0), ("bc", g)], [(x.shape[1], BF16)], rows=x.shape[0], tm=256, name=name,
                   cargo=cargo)


def _rms_bwd(x, dh, g, dres, *, name, cargo=None):
    D = x.shape[1]

    def fn(xv, dhv, gv, drv):
        r = lax.rsqrt(jnp.mean(xv * xv, axis=-1, keepdims=True) + EPS)
        xh = xv * r
        dhf = dhv.astype(F32)
        dxn = dhf * gv
        dx = drv + r * (dxn - xh * jnp.mean(dxn * xh, axis=-1, keepdims=True))
        return dx, dx, jnp.sum(dhf * xh, axis=0, keepdims=True)

    return _rowmap(fn, [("row", x, None, 0), ("row", dh, None, 0), ("bc", g), ("row", dres, None, 0)],
                   [(D, F32), (D, BF16)], rows=x.shape[0], tm=256, name=name, accs=[((1, D), F32)], cargo=cargo)


def _loss(y, tgt, *, name):
    D = y.shape[1]

    def fn(yv, tv):
        e = yv - tv
        d = e * (1.0 / D)
        part = 0.5 * jnp.sum(jnp.sum(e * e, axis=-1, keepdims=True) * (1.0 / D), axis=0, keepdims=True)
        return d, d, jnp.broadcast_to(part, (1, LANES))

    return _rowmap(fn, [("row", y, None, 0), ("row", tgt, None, 0)], [(D, F32), (D, BF16)],
                   rows=y.shape[0], tm=256, name=name, accs=[((1, LANES), F32)])


def _merge_fwd(z, yp, yf, ym, *, name, cargo=None):
    D = yp.shape[1]

    def fn(gp, gf, gm, a, b, c):
        return jax.nn.sigmoid(gp) * a + jax.nn.sigmoid(gf) * b + jax.nn.sigmoid(gm) * c

    return _rowmap(fn, [("row", z, D, 0), ("row", z, D, 1), ("row", z, D, 2),
                        ("row", yp, None, 0), ("row", yf, None, 0), ("row", ym, None, 0)],
                   [(D, BF16)], rows=yp.shape[0], tm=128, name=name, cargo=cargo)


def _merge_bwd(dm, z, yp, yf, ym, *, name):
    D = yp.shape[1]

    def fn(d, gp, gf, gm, a, b, c):
        outs, dgl = [], []
        for gl, yv in ((gp, a), (gf, b), (gm, c)):
            s = jax.nn.sigmoid(gl)
            outs.append(d * s)
            dgl.append(d * yv * (s * (1.0 - s)))
        return (*outs, jnp.concatenate(dgl, axis=-1))

    return _rowmap(fn, [("row", dm, None, 0), ("row", z, D, 0), ("row", z, D, 1), ("row", z, D, 2),
                        ("row", yp, None, 0), ("row", yf, None, 0), ("row", ym, None, 0)],
                   [(D, BF16), (D, BF16), (D, BF16), (3 * D, BF16)], rows=yp.shape[0], tm=128, name=name)


def _adamw(w, g, m, v, *, name):
    C = w.shape[1]

    def fn(wv, gv, mv, vv):
        m2 = ADAM_B1 * mv + (1.0 - ADAM_B1) * gv
        v2 = ADAM_B2 * vv + (1.0 - ADAM_B2) * (gv * gv)
        m_hat = m2 / (1.0 - ADAM_B1 ** ADAM_STEP)
        v_hat = v2 / (1.0 - ADAM_B2 ** ADAM_STEP)
        return -ADAM_LR * (m_hat / (jnp.sqrt(v_hat) + ADAM_EPS) + ADAM_WD * wv), m2, v2

    tm = max(8, (262144 // C) // 8 * 8)
    return _rowmap(fn, [("row", a, None, 0) for a in (w, g, m, v)], [(C, F32)] * 3,
                   rows=w.shape[0], tm=tm, name=name)


def _seg_mean(v, hd):
    if hd == LANES:
        return jnp.mean(v, axis=-1, keepdims=True)
    lo = lax.broadcasted_iota(jnp.int32, v.shape, 1) < hd
    s0 = jnp.sum(jnp.where(lo, v, 0.0), axis=-1, keepdims=True)
    s1 = jnp.sum(jnp.where(lo, 0.0, v), axis=-1, keepdims=True)
    return jnp.where(lo, s0, s1) * (1.0 / hd)


def _gain128(g, hd):
    g = g.reshape(1, hd).astype(F32)
    return g if hd == LANES else jnp.concatenate([g, g], axis=-1)


def _headnorm_fwd(x, cb0, width, gain, hd, *, name):
    rows = x.shape[0]
    tm = _tile(rows, 512, 8)
    g128 = _gain128(gain, hd)

    def body(x_ref, g_ref, o_ref):
        xv = x_ref[...]
        o_ref[...] = (xv * lax.rsqrt(_seg_mean(xv * xv, hd) + EPS) * g_ref[...]).astype(BF16)

    return _pc(body, name=name, out_shape=jax.ShapeDtypeStruct((rows, width), BF16),
               grid=(rows // tm, width // LANES),
               in_specs=[pl.BlockSpec((tm, LANES), lambda i, j: (i, cb0 + j)),
                         pl.BlockSpec((1, LANES), lambda i, j: (0, 0))],
               out_specs=pl.BlockSpec((tm, LANES), lambda i, j: (i, j)),
               compiler_params=_cp("parallel", "parallel"))(x, g128)


def _headnorm_bwd(x, cb0, width, dy, gain, hd, post, spread, out_dtype, *, name):
    rows = x.shape[0]
    tm = _tile(rows, 512, 8)
    g128 = _gain128(gain, hd)
    dyw = 2 * LANES if spread else LANES

    def body(x_ref, dy_ref, g_ref, dx_ref, dg_ref):
        xv = x_ref[...]
        dyv = dy_ref[...]
        if spread:
            dyv = jnp.concatenate([dyv[:, :hd], dyv[:, LANES:LANES + hd]], axis=-1)
        dyv = dyv * post
        r = lax.rsqrt(_seg_mean(xv * xv, hd) + EPS)
        xh = xv * r
        dxn = dyv * g_ref[...]
        dx_ref[...] = (r * (dxn - xh * _seg_mean(dxn * xh, hd))).astype(dx_ref.dtype)

        @pl.when((pl.program_id(0) == 0) & (pl.program_id(1) == 0))
        def _():
            dg_ref[...] = jnp.zeros_like(dg_ref)

        dg_ref[...] += jnp.sum(dyv * xh, axis=0, keepdims=True)

    return _pc(body, name=name,
               out_shape=(jax.ShapeDtypeStruct((rows, width), out_dtype), jax.ShapeDtypeStruct((1, LANES), F32)),
               grid=(rows // tm, width // LANES),
               in_specs=[pl.BlockSpec((tm, LANES), lambda i, j: (i, cb0 + j)),
                         pl.BlockSpec((tm, dyw), lambda i, j: (i, j)),
                         pl.BlockSpec((1, LANES), lambda i, j: (0, 0))],
               out_specs=(pl.BlockSpec((tm, LANES), lambda i, j: (i, j)),
                          pl.BlockSpec((1, LANES), lambda i, j: (0, 0))),
               compiler_params=_cp("arbitrary", "arbitrary"))(x, dy, g128)


def _fold_gain(dg, hd):
    return dg if hd == LANES else dg[:, :hd] + dg[:, hd:]


def _shift_down(v, k, row):
    return jnp.where(row >= k, pltpu.roll(v, k, 0), 0.0)


def _shift_up(v, k, row, S):
    return jnp.where(row < S - k, pltpu.roll(v, S - k, 0), 0.0)


def _window_sum(v, g, row, shift):
    s, outs = v, []
    for k in (1, 2, 4, 8):
        s = s + shift(s, k)
        outs.append(s)
    return jnp.where(g == 0, outs[0], jnp.where(g == 1, outs[1], jnp.where(g == 2, outs[2], outs[3])))


def _pool_fwd(z, cb0, pool_w, pool_scale, nb, S, *, name):
    PG, PD = pool_w.shape[0], pool_w.shape[1]
    assert PD == LANES and PG <= len(POOL_WINDOWS)
    pw = pool_w.astype(BF16)

    def body(u_ref, w_ref, s_ref, o_ref):
        g = pl.program_id(0)
        u = u_ref[...]
        row = lax.broadcasted_iota(jnp.int32, u.shape, 0)
        cnt = jnp.minimum(row + 1, jnp.left_shift(2, g)).astype(F32)
        pooled = _window_sum(u, g, row, lambda v, k: _shift_down(v, k, row)) / cnt
        mixed = jnp.dot((pooled - u).astype(BF16), w_ref[...], preferred_element_type=F32)
        o_ref[...] = (mixed * s_ref[...]).astype(BF16)

    return _pc(body, name=name, out_shape=jax.ShapeDtypeStruct((nb * S, PG * PD), BF16), grid=(PG, nb),
               in_specs=[pl.BlockSpec((S, PD), lambda g, b: (b, cb0 + g)),
                         pl.BlockSpec((None, PD, PD), lambda g, b: (g, 0, 0)),
                         pl.BlockSpec((1, PD), lambda g, b: (0, g))],
               out_specs=pl.BlockSpec((S, PD), lambda g, b: (b, g)),
               compiler_params=_cp("parallel", "parallel"))(z, pw, pool_scale)


def _pool_bwd(z, cb0, dv, pool_w, pool_scale, nb, S, *, name):
    PG, PD = pool_w.shape[0], pool_w.shape[1]
    pw = pool_w.astype(BF16)

    def body(u_ref, dv_ref, w_ref, s_ref, du_ref, dw_ref, ds_ref):
        g, b = pl.program_id(0), pl.program_id(1)
        u, dvv, w = u_ref[...], dv_ref[...], w_ref[...]
        row = lax.broadcasted_iota(jnp.int32, u.shape, 0)
        cnt = jnp.minimum(row + 1, jnp.left_shift(2, g)).astype(F32)
        diff = (_window_sum(u, g, row, lambda v, k: _shift_down(v, k, row)) / cnt - u).astype(BF16)
        mixed = jnp.dot(diff, w, preferred_element_type=F32)
        dmixed = (dvv * s_ref[...]).astype(BF16)
        ddiff = lax.dot_general(dmixed, w, _DIMS["nt"], preferred_element_type=F32)
        du_ref[...] = (_window_sum(ddiff / cnt, g, row, lambda v, k: _shift_up(v, k, row, S)) - ddiff).astype(BF16)

        @pl.when(b == 0)
        def _():
            dw_ref[...] = jnp.zeros_like(dw_ref)
            ds_ref[...] = jnp.zeros_like(ds_ref)

        dw_ref[...] += lax.dot_general(diff, dmixed, _DIMS["tn"], preferred_element_type=F32)
        ds_ref[...] += jnp.sum(dvv * mixed, axis=0, keepdims=True)

    return _pc(body, name=name,
               out_shape=(jax.ShapeDtypeStruct((nb * S, PG * PD), BF16), jax.ShapeDtypeStruct((PG, PD, PD), F32),
                          jax.ShapeDtypeStruct((1, PG * PD), F32)),
               grid=(PG, nb),
               in_specs=[pl.BlockSpec((S, PD), lambda g, b: (b, cb0 + g)),
                         pl.BlockSpec((S, PD), lambda g, b: (b, g)),
                         pl.BlockSpec((None, PD, PD), lambda g, b: (g, 0, 0)),
                         pl.BlockSpec((1, PD), lambda g, b: (0, g))],
               out_specs=(pl.BlockSpec((S, PD), lambda g, b: (b, g)),
                          pl.BlockSpec((None, PD, PD), lambda g, b: (g, 0, 0)),
                          pl.BlockSpec((1, PD), lambda g, b: (0, g))),
               compiler_params=_cp("parallel", "arbitrary"))(z, dv, pw, pool_scale)


def _split3(c):
    hi = c.astype(BF16)
    r1 = c - hi.astype(F32)
    mid = r1.astype(BF16)
    lo = (r1 - mid.astype(F32)).astype(BF16)
    return hi, mid, lo


def _log_sigmoid(v):
    return jnp.minimum(v, 0.0) - jnp.log(1.0 + jnp.exp(-jnp.abs(v)))


def _fox_c(z, cb, b128, nb, S, NF, *, name):
    def body(f_ref, b_ref, hi_ref, mid_ref, lo_ref):
        v = f_ref[...] + b_ref[...]
        lane = lax.broadcasted_iota(jnp.int32, v.shape, 1)
        row = lax.broadcasted_iota(jnp.int32, v.shape, 0)
        c = jnp.where(lane < NF, _log_sigmoid(v), 0.0)
        k = 1
        while k < S:
            c = c + _shift_down(c, k, row)
            k *= 2
        hi, mid, lo = _split3(c)
        hi_ref[...], mid_ref[...], lo_ref[...] = hi, mid, lo

    shp = jax.ShapeDtypeStruct((nb * S, LANES), BF16)
    spec = pl.BlockSpec((S, LANES), lambda i: (i, 0))
    return _pc(body, name=name, out_shape=(shp, shp, shp), grid=(nb,),
               in_specs=[pl.BlockSpec((S, LANES), lambda i: (i, cb)), pl.BlockSpec((1, LANES), lambda i: (0, 0))],
               out_specs=(spec, spec, spec), compiler_params=_cp("parallel"))(z, b128)


def _fox_c_bwd(dc, z, cb, b128, nb, S, *, name):
    def body(dc_ref, f_ref, b_ref, df_ref, db_ref):
        d = dc_ref[...]
        row = lax.broadcasted_iota(jnp.int32, d.shape, 0)
        k = 1
        while k < S:
            d = d + _shift_up(d, k, row, S)
            k *= 2
        df = d * jax.nn.sigmoid(-(f_ref[...] + b_ref[...]))
        df_ref[...] = df

        @pl.when(pl.program_id(0) == 0)
        def _():
            db_ref[...] = jnp.zeros_like(db_ref)

        db_ref[...] += jnp.sum(df, axis=0, keepdims=True)

    return _pc(body, name=name,
               out_shape=(jax.ShapeDtypeStruct((nb * S, LANES), F32), jax.ShapeDtypeStruct((1, LANES), F32)),
               grid=(nb,),
               in_specs=[pl.BlockSpec((S, LANES), lambda i: (i, 0)), pl.BlockSpec((S, LANES), lambda i: (i, cb)),
                         pl.BlockSpec((1, LANES), lambda i: (0, 0))],
               out_specs=(pl.BlockSpec((S, LANES), lambda i: (i, 0)), pl.BlockSpec((1, LANES), lambda i: (0, 0))),
               compiler_params=_cp("arbitrary"))(dc, z, b128)


def _fox_qk(z, cbq, cbk, cbv, c3, gq, gk, HD, NF, *, name, cargo=None):
    T = z.shape[0]
    HP = NF * HD // LANES
    tm = _tile(T, 512, 8)
    scale = HD ** -0.5
    gq128, gk128 = _gain128(gq, HD), _gain128(gk, HD)

    def body(q_ref, k_ref, v_ref, hi_ref, mid_ref, lo_ref, gq_ref, gk_ref, qa_ref, ka_ref, vb_ref):
        hp = pl.program_id(1)
        q, k = q_ref[...], k_ref[...]
        qn = q * lax.rsqrt(_seg_mean(q * q, HD) + EPS) * (gq_ref[...] * scale)
        kn = k * lax.rsqrt(_seg_mean(k * k, HD) + EPS) * gk_ref[...]
        vb_ref[...] = v_ref[...].astype(BF16)
        c3v = jnp.concatenate([hi_ref[...], mid_ref[...], lo_ref[...]], axis=-1)
        r = lax.broadcasted_iota(jnp.int32, (3 * LANES, LANES - HD), 0)
        cc = lax.broadcasted_iota(jnp.int32, (3 * LANES, LANES - HD), 1)
        lane = lax.broadcasted_iota(jnp.int32, (tm, LANES - HD), 1)
        for hh in range(LANES // HD):
            head = hp * (LANES // HD) + hh
            sel_q = jnp.where((cc < 3) & (r == head + LANES * cc), 1.0, 0.0).astype(BF16)
            sel_k = jnp.where((cc >= 3) & (cc < 6) & (r == head + LANES * (cc - 3)), 1.0, 0.0).astype(BF16)
            qaug = jnp.dot(c3v, sel_q, preferred_element_type=F32) + jnp.where((lane >= 3) & (lane < 6), 1.0, 0.0)
            kaug = (jnp.where((lane < 3) | ((lane >= 6) & (lane < 9)), 1.0, 0.0)
                    - jnp.dot(c3v, sel_k, preferred_element_type=F32))
            sl = slice(hh * HD, (hh + 1) * HD)
            qa_ref[:, hh * LANES:(hh + 1) * LANES] = jnp.concatenate([qn[:, sl], qaug], axis=-1).astype(BF16)
            ka_ref[:, hh * LANES:(hh + 1) * LANES] = jnp.concatenate([kn[:, sl], kaug], axis=-1).astype(BF16)

    aw = (LANES // HD) * LANES
    blk = lambda cb: pl.BlockSpec((tm, LANES), functools.partial(lambda i, j, cb: (i, cb + j), cb=cb))
    cspec = pl.BlockSpec((tm, LANES), lambda i, j: (i, 0))
    gspec = pl.BlockSpec((1, LANES), lambda i, j: (0, 0))
    out = _pc(body, name=name, cargo=cargo,
              out_shape=(jax.ShapeDtypeStruct((T, HP * aw), BF16), jax.ShapeDtypeStruct((T, HP * aw), BF16),
                         jax.ShapeDtypeStruct((T, NF * HD), BF16)),
              grid=(T // tm, HP),
              in_specs=[blk(cbq), blk(cbk), blk(cbv), cspec, cspec, cspec, gspec, gspec],
              out_specs=(pl.BlockSpec((tm, aw), lambda i, j: (i, j)), pl.BlockSpec((tm, aw), lambda i, j: (i, j)),
                         pl.BlockSpec((tm, LANES), lambda i, j: (i, j))),
              compiler_params=_cp("parallel", "parallel"))(z, z, z, *c3, gq128, gk128,
                                                           *(cargo.arrays if cargo else ()))
    return _split_cargo(out, cargo)


def _fox_logits(qa, ka, fill):
    s = lax.dot_general(qa, ka, _DIMS["nt"], preferred_element_type=F32)
    tq, L = s.shape
    diag = s[:, L - tq:]
    keep = lax.broadcasted_iota(jnp.int32, diag.shape, 1) <= lax.broadcasted_iota(jnp.int32, diag.shape, 0)
    diag = jnp.where(keep, diag, fill)
    return diag if L == tq else jnp.concatenate([s[:, :L - tq], diag], axis=-1), keep


def _fox_fwd(qa, ka, vb, nb, S, HD, *, name, tq=256, cargo=None):
    T = qa.shape[0]
    nh = LANES // HD
    aw = nh * LANES
    HP = qa.shape[1] // aw
    tq = _tile(S, tq, 8)
    nq = S // tq

    def body(qa_ref, ka_ref, v_ref, o_ref, qb_ref):
        lane = lax.broadcasted_iota(jnp.int32, (tq, LANES), 1)
        for k in range(nq):
            @pl.when(pl.program_id(2) == k)
            def _(k=k):
                L = (k + 1) * tq
                outs = []
                v = v_ref[0:L, :]
                for hh in range(nh):
                    al = slice(hh * LANES, (hh + 1) * LANES)
                    qav = qa_ref[:, al]
                    s, _ = _fox_logits(qav, ka_ref[0:L, al], NEG)
                    m = jnp.max(s, axis=-1, keepdims=True)
                    e = jnp.exp(s - m)
                    l = jnp.sum(e, axis=-1, keepdims=True)
                    p = e / l
                    outs.append(jnp.dot(p.astype(BF16), v[:, hh * HD:(hh + 1) * HD], preferred_element_type=F32))
                    hi, mid, lo = _split3(-(m + jnp.log(l)))
                    qb_ref[:, al] = jnp.where(lane == HD + 6, hi, jnp.where(lane == HD + 7, mid,
                                                                         jnp.where(lane == HD + 8, lo, qav)))
                o_ref[...] = jnp.concatenate(outs, axis=-1).astype(BF16)

    qspec = pl.BlockSpec((tq, aw), lambda b, h, i: (b * nq + i, h))
    out = _pc(body, name=name, cargo=cargo,
              out_shape=(jax.ShapeDtypeStruct((T, HP * LANES), BF16), jax.ShapeDtypeStruct(qa.shape, BF16)),
              grid=(nb, HP, nq),
              in_specs=[qspec, pl.BlockSpec((S, aw), lambda b, h, i: (b, h)),
                        pl.BlockSpec((S, LANES), lambda b, h, i: (b, h))],
              out_specs=(pl.BlockSpec((tq, LANES), lambda b, h, i: (b * nq + i, h)), qspec),
              compiler_params=_cp("parallel", "parallel", "arbitrary"))(qa, ka, vb, *(cargo.arrays if cargo else ()))
    return _split_cargo(out, cargo)


def _fox_bwd(qa, ka, vb, do, nb, S, HD, *, name, tq=256, cargo=None):
    T = qa.shape[0]
    nh = LANES // HD
    aw = nh * LANES
    HP = qa.shape[1] // aw
    tq = _tile(S, tq, 8)
    nq = S // tq

    def body(qa_ref, ka_ref, v_ref, do_ref, dqa_ref, dka_ref, dv_ref, dcs_ref):
        i = pl.program_id(2)

        @pl.when(i == 0)
        def _():
            dka_ref[...] = jnp.zeros_like(dka_ref)
            dv_ref[...] = jnp.zeros_like(dv_ref)
            dcs_ref[...] = jnp.zeros_like(dcs_ref)

        for k in range(nq):
            @pl.when(i == k)
            def _(k=k):
                L = (k + 1) * tq
                dkas, dvs, css = [], [], []
                v, dob = v_ref[0:L, :], do_ref[...]
                for hh in range(nh):
                    al = slice(hh * LANES, (hh + 1) * LANES)
                    hl = slice(hh * HD, (hh + 1) * HD)
                    qav, kav, vv, dov = qa_ref[:, al], ka_ref[0:L, al], v[:, hl], dob[:, hl]
                    p = jnp.exp(_fox_logits(qav, kav, NEG)[0])
                    dp = lax.dot_general(dov, vv, _DIMS["nt"], preferred_element_type=F32)
                    dsf = p * (dp - jnp.sum(p * dp, axis=-1, keepdims=True))
                    css.append(jnp.sum(dsf, axis=0, keepdims=True))
                    ds = dsf.astype(BF16)
                    dqa_ref[:, al] = jnp.dot(ds, kav, preferred_element_type=F32)
                    dkas.append(lax.dot_general(ds, qav, _DIMS["tn"], preferred_element_type=F32))
                    dvs.append(lax.dot_general(p.astype(BF16), dov, _DIMS["tn"], preferred_element_type=F32))
                dka_ref[0:L, :] += jnp.concatenate(dkas, axis=-1)
                dv_ref[0:L, :] += jnp.concatenate(dvs, axis=-1)
                dcs_ref[:, 0:L] += jnp.concatenate(css + [jnp.zeros((8 - nh, L), F32)], axis=0)

    out = _pc(body, name=name, cargo=cargo,
              out_shape=(jax.ShapeDtypeStruct((T, HP * aw), F32), jax.ShapeDtypeStruct((T, HP * aw), F32),
                         jax.ShapeDtypeStruct((T, HP * LANES), F32), jax.ShapeDtypeStruct((nb, HP, 8, S), F32)),
              grid=(nb, HP, nq),
              in_specs=[pl.BlockSpec((tq, aw), lambda b, h, i: (b * nq + i, h)),
                        pl.BlockSpec((S, aw), lambda b, h, i: (b, h)),
                        pl.BlockSpec((S, LANES), lambda b, h, i: (b, h)),
                        pl.BlockSpec((tq, LANES), lambda b, h, i: (b * nq + i, h))],
              out_specs=(pl.BlockSpec((tq, aw), lambda b, h, i: (b * nq + i, h)),
                         pl.BlockSpec((S, aw), lambda b, h, i: (b, h)),
                         pl.BlockSpec((S, LANES), lambda b, h, i: (b, h)),
                         pl.BlockSpec((None, None, 8, S), lambda b, h, i: (b, h, 0, 0))),
              compiler_params=_cp("parallel", "parallel", "arbitrary"))(qa, ka, vb, do,
                                                                         *(cargo.arrays if cargo else ()))
    return _split_cargo(out, cargo)


def _mem_probs(q, k, scale):
    s = lax.dot_general(q, k, _DIMS["nt"], preferred_element_type=F32) * scale
    e = jnp.exp(s - jnp.max(s, axis=-1, keepdims=True))
    return e / jnp.sum(e, axis=-1, keepdims=True)


def _mem_fwd(q, k, v, nb, S, ML, *, name, tq=512):
    T, MW = q.shape
    nh = MW // LANES
    tq = _tile(S, tq, 8)
    nq = S // tq
    scale = LANES ** -0.5

    def body(q_ref, k_ref, v_ref, o_ref):
        for h in range(nh):
            sl = slice(h * LANES, (h + 1) * LANES)
            p = _mem_probs(q_ref[:, sl], k_ref[:, sl], scale)
            o_ref[:, sl] = jnp.dot(p.astype(BF16), v_ref[:, sl], preferred_element_type=F32).astype(BF16)

    return _pc(body, name=name, out_shape=jax.ShapeDtypeStruct((T, MW), BF16), grid=(nb, nq),
               in_specs=[pl.BlockSpec((tq, MW), lambda b, i: (b * nq + i, 0)),
                         pl.BlockSpec((ML, MW), lambda b, i: (b, 0)), pl.BlockSpec((ML, MW), lambda b, i: (b, 0))],
               out_specs=pl.BlockSpec((tq, MW), lambda b, i: (b * nq + i, 0)),
               compiler_params=_cp("parallel", "arbitrary"))(q, k, v)


def _mem_bwd(q, k, v, do, nb, S, ML, *, name, tq=512):
    T, MW = q.shape
    nh = MW // LANES
    tq = _tile(S, tq, 8)
    nq = S // tq
    scale = LANES ** -0.5

    def body(q_ref, k_ref, v_ref, do_ref, dq_ref, dk_ref, dv_ref):
        i = pl.program_id(1)

        @pl.when(i == 0)
        def _():
            dk_ref[...] = jnp.zeros_like(dk_ref)
            dv_ref[...] = jnp.zeros_like(dv_ref)

        for h in range(nh):
            sl = slice(h * LANES, (h + 1) * LANES)
            qv, kv, vv, dov = q_ref[:, sl], k_ref[:, sl], v_ref[:, sl], do_ref[:, sl]
            p = _mem_probs(qv, kv, scale)
            dp = lax.dot_general(dov, vv, _DIMS["nt"], preferred_element_type=F32)
            ds = (p * (dp - jnp.sum(p * dp, axis=-1, keepdims=True)) * scale).astype(BF16)
            dq_ref[:, sl] = jnp.dot(ds, kv, preferred_element_type=F32)
            dk_ref[:, sl] += lax.dot_general(ds, qv, _DIMS["tn"], preferred_element_type=F32)
            dv_ref[:, sl] += lax.dot_general(p.astype(BF16), dov, _DIMS["tn"], preferred_element_type=F32)

    kvspec = pl.BlockSpec((ML, MW), lambda b, i: (b, 0))
    qspec = pl.BlockSpec((tq, MW), lambda b, i: (b * nq + i, 0))
    return _pc(body, name=name,
               out_shape=(jax.ShapeDtypeStruct((T, MW), F32), jax.ShapeDtypeStruct((nb * ML, MW), F32),
                          jax.ShapeDtypeStruct((nb * ML, MW), F32)),
               grid=(nb, nq), in_specs=[qspec, kvspec, kvspec, qspec], out_specs=(qspec, kvspec, kvspec),
               compiler_params=_cp("parallel", "arbitrary"))(q, k, v, do)


def _place():
    x, y, c = lax.axis_index("x"), lax.axis_index("y"), lax.axis_index("c")
    chips = [(1 - x, y), (x, 1 - y), (1 - x, 1 - y)]
    return x, y, c, chips


def _any_specs(n):
    return [pl.BlockSpec(memory_space=pl.ANY)] * n


class _Cargo:
    def __init__(self, arrays, out_shapes, ncopies, copies, alias=None):
        self.arrays, self.out_shapes, self.ncopies, self.copies = list(arrays), list(out_shapes), ncopies, copies
        self.alias = alias or {}


def _load_cargo(body, cargo, kw):
    as_list = lambda v: list(v) if isinstance(v, (tuple, list)) else [v]
    grid = kw.get("grid", ())
    in_specs, out_specs, out_shape = as_list(kw["in_specs"]), as_list(kw["out_specs"]), as_list(kw["out_shape"])
    scratch = list(kw.get("scratch_shapes", ()))
    nin, nout, nci, nco, nscr = len(in_specs), len(out_specs), len(cargo.arrays), len(cargo.out_shapes), len(scratch)

    def loaded(*refs):
        ins, cin = refs[:nin], refs[nin:nin + nci]
        outs, cout = refs[nin + nci:nin + nci + nout], refs[nin + nci + nout:nin + nci + nout + nco]
        scr, (ssem, rsem) = refs[nin + nci + nout + nco:-2], refs[-2:]
        first = last = True
        for a, g in enumerate(grid):
            first = first & (pl.program_id(a) == 0)
            last = last & (pl.program_id(a) == g - 1)

        def start():
            for cp in cargo.copies(cin, cout, ssem, rsem, False):
                cp.start()

        def wait():
            for cp in cargo.copies(cin, cout, ssem, rsem, True):
                cp.wait_send()
                cp.wait_recv()

        if grid:
            pl.when(first)(start)
        else:
            start()
        body(*ins, *outs, *scr)
        if grid:
            pl.when(last)(wait)
        else:
            wait()

    kw = dict(kw, in_specs=in_specs + _any_specs(nci), out_specs=tuple(out_specs + _any_specs(nco)),
              out_shape=tuple(out_shape + cargo.out_shapes),
              scratch_shapes=scratch + [pltpu.SemaphoreType.DMA((cargo.ncopies,))] * 2)
    if cargo.alias:
        kw["input_output_aliases"] = {nin + i: nout + o for i, o in cargo.alias.items()}
    if grid:
        kw["compiler_params"] = _cp(*["arbitrary"] * len(grid))
    return loaded, kw


class _SemView:
    def __init__(self, ref, off):
        self.ref, self.off, self.at = ref, off, self

    def __getitem__(self, k):
        return self.ref.at[self.off + k]


def _join_cargo(a, b):
    na, nao = len(a.arrays), len(a.out_shapes)

    def copies(ins, outs, ssem, rsem, waiting):
        return (a.copies(ins[:na], outs[:nao], ssem, rsem, waiting) +
                b.copies(ins[na:], outs[nao:], _SemView(ssem, a.ncopies), _SemView(rsem, a.ncopies), waiting))

    alias = dict(a.alias)
    alias.update({na + i: nao + o for i, o in b.alias.items()})
    return _Cargo(a.arrays + b.arrays, a.out_shapes + b.out_shapes, a.ncopies + b.ncopies, copies, alias)


def _run_cargo(cargo, *, name):
    res = _pc(lambda: None, name=name, cargo=cargo, in_specs=[], out_specs=[], out_shape=[])(*cargo.arrays)
    return list(res)


def _full_shape(dims, kind):
    return (dims[0], 4 * dims[1]) if kind == "col" else (4 * dims[0], dims[1])


def _ag_copy(shard, full, kind, j_src, chip, c, ssem, rsem, piece):
    R, C = shard.shape
    p, npieces, cnt = (*piece, 1)[:3]
    rows = R // 2 // npieces * cnt
    row0 = pl.multiple_of(c * (R // 2) + p * (R // 2 // npieces), 16)
    if kind == "col":
        dst = full.at[pl.ds(row0, rows), pl.ds(pl.multiple_of(j_src * C, LANES), C)]
    else:
        dst = full.at[pl.ds(pl.multiple_of(j_src * R + row0, 16), rows), :]
    return pltpu.make_async_remote_copy(shard.at[pl.ds(row0, rows), :], dst, ssem, rsem,
                                        device_id=(*chip, c), device_id_type=MESH)


def _ag_cargo(shards, kinds, piece=(0, 1), lands=None):
    n = len(shards)

    def copies(ins, outs, ssem, rsem, waiting):
        x, y, c, chips = _place()
        return [_ag_copy(ins[w], outs[w], kinds[w], 2 * chip[0] + chip[1] if waiting else 2 * x + y, chip, c,
                         ssem.at[3 * w + q], rsem.at[3 * w + q], piece)
                for w in range(n) for q, chip in enumerate(chips)]

    shapes = [jax.ShapeDtypeStruct(_full_shape(s.shape, k), BF16) for s, k in zip(shards, kinds)]
    if lands is None:
        return _Cargo(shards, shapes, 3 * n, copies)
    return _Cargo(list(shards) + list(lands), shapes, 3 * n, copies, alias={n + w: w for w in range(n)})


def _scatter_cargo(parts, kinds, piece=(0, 1), lands=None):
    n = len(parts)
    pdims = [((p.shape[0], p.shape[1] // 4) if k == "col" else (p.shape[0] // 4, p.shape[1]))
             for p, k in zip(parts, kinds)]

    def copy(src, dst, kind, R, C, q, j, chip, c, ssem, rsem):
        p, npieces, cnt = (*piece, 1)[:3]
        rows = R // npieces * cnt
        r0 = p * (R // npieces)
        if kind == "col":
            win = src.at[pl.ds(r0, rows), pl.ds(pl.multiple_of(j * C, LANES), C)]
        else:
            win = src.at[pl.ds(pl.multiple_of(j * R + r0, 16), rows), :]
        return pltpu.make_async_remote_copy(win, dst.at[q, pl.ds(r0, rows), :], ssem, rsem,
                                            device_id=(*chip, c), device_id_type=MESH)

    def copies(ins, outs, ssem, rsem, waiting):
        x, y, c, chips = _place()
        return [copy(ins[w], outs[w], kinds[w], *pdims[w], q, 2 * chip[0] + chip[1], chip, c,
                     ssem.at[3 * w + q], rsem.at[3 * w + q])
                for w in range(n) for q, chip in enumerate(chips)]

    shapes = [jax.ShapeDtypeStruct((3,) + d, BF16) for d in pdims]
    if lands is None:
        return _Cargo(parts, shapes, 3 * n, copies)
    return _Cargo(list(parts) + list(lands), shapes, 3 * n, copies, alias={n + w: w for w in range(n)})


def _ag_fill(full, shard, kind, *, name):
    R, C = shard.shape
    RH = R // 2
    tm = _tile(RH, max(16, (3 * 2 ** 19 // C) // 16 * 16), 16)
    nbh = RH // tm
    n, nown = 3 * nbh, 2 * nbh
    xyc = jnp.stack([lax.axis_index("x"), lax.axis_index("y"), lax.axis_index("c")]).astype(jnp.int32)

    def chip_of(q, x, y):
        keep_x, keep_y = q % 2, (2 - q) // 2
        return 2 * ((1 - x) + keep_x * (2 * x - 1)) + (1 - y) + keep_y * (2 * y - 1)

    def window(ref, j, row0):
        if kind == "col":
            return ref.at[pl.ds(pl.multiple_of(row0, 16), tm), pl.ds(pl.multiple_of(j * C, LANES), C)]
        return ref.at[pl.ds(pl.multiple_of(j * R + row0, 16), tm), :]

    def body(p_ref, send_ref, own_ref, out_ref, land, ssem, rsem, wsem):
        i = pl.program_id(0)
        x, y = lax.axis_index("x"), lax.axis_index("y")
        c, peer = _pair_peer()
        own_cp = pltpu.make_async_copy(own_ref, window(out_ref, 2 * x + y, jnp.minimum(i, nown - 1) * tm), wsem.at[1])

        @pl.when(i < nown)
        def _():
            own_cp.start()

        def consume(land, ps):
            b = i - 1
            cp = pltpu.make_async_copy(land.at[ps], window(out_ref, chip_of(b // nbh, x, y),
                                                           (1 - c) * RH + (b % nbh) * tm), wsem.at[0])
            cp.start()
            cp.wait()

        _exchange_step(i, n, send_ref, land, ssem, rsem, None, peer, consume)

        @pl.when(i < nown)
        def _():
            own_cp.wait()

    now = lambda i: jnp.minimum(i, n - 1)

    def send_map(i, p):
        q, b = now(i) // nbh, now(i) % nbh
        j = chip_of(q, p[0], p[1])
        return (p[2] * nbh + b, j) if kind == "col" else ((j * 2 + p[2]) * nbh + b, 0)

    grid_spec = pltpu.PrefetchScalarGridSpec(
        num_scalar_prefetch=1, grid=(n + 1,),
        in_specs=[pl.BlockSpec((tm, C), send_map),
                  pl.BlockSpec((tm, C), lambda i, p: (jnp.minimum(i, nown - 1), 0))],
        out_specs=pl.BlockSpec(memory_space=pl.ANY),
        scratch_shapes=[pltpu.VMEM((n, tm, C), BF16), pltpu.SemaphoreType.DMA((n,)), pltpu.SemaphoreType.DMA((n,)),
                        pltpu.SemaphoreType.DMA((2,))])
    return _pc(body, name=name, out_shape=jax.ShapeDtypeStruct(full.shape, BF16), grid_spec=grid_spec,
               input_output_aliases={1: 0}, compiler_params=_cp("arbitrary"))(xyc, full, shard)


def _pair_peer():
    x, y, c = lax.axis_index("x"), lax.axis_index("y"), lax.axis_index("c")
    return c, (x, y, 1 - c)


def _exchange_step(i, n, src_ref, land, ssem, rsem, credit, peer, consume):
    def rdma(slot):
        return pltpu.make_async_remote_copy(src_ref, land.at[slot], ssem.at[slot], rsem.at[slot],
                                            device_id=peer, device_id_type=MESH)

    reuse = credit is not None
    slot = i % 2 if reuse else jnp.minimum(i, n - 1)

    @pl.when(i < n)
    def _():
        if reuse:
            @pl.when(i >= 2)
            def _():
                pl.semaphore_wait(credit, 1)

        rdma(slot).start()

    @pl.when(i >= 1)
    def _():
        ps = (i - 1) % 2 if reuse else i - 1
        rdma(ps).wait_recv()
        consume(land, ps)

        if reuse:
            @pl.when(i - 1 < n - 2)
            def _():
                pl.semaphore_signal(credit, 1, device_id=peer, device_id_type=MESH)

    @pl.when(i < n)
    def _():
        rdma(slot).wait_send()


def _xchg_scratch(tm, C, dtype):
    return [pltpu.VMEM((2, tm, C), dtype), pltpu.SemaphoreType.DMA((2,)), pltpu.SemaphoreType.DMA((2,)),
            pltpu.SemaphoreType.REGULAR]


def _rs_pair_sum(g, kind, *, name):
    K, N = g.shape
    NS, RH = (1, K // 2) if kind == "col" else (4, K // 8)
    tm = _tile(RH, max(16, (3 * 2 ** 19 // N) // 16 * 16), 16)
    if NS * (RH // tm) < 2:
        tm = RH // 2
    nb = RH // tm
    n = NS * nb
    cvec = lax.axis_index("c").reshape(1).astype(jnp.int32)

    def body(c_ref, send_ref, own_ref, out_ref, land, ssem, rsem, credit):
        _, peer = _pair_peer()

        def consume(land, ps):
            out_ref[...] = (own_ref[...].astype(F32) + land[ps].astype(F32)).astype(BF16)

        _exchange_step(pl.program_id(0), n, send_ref, land, ssem, rsem, credit, peer, consume)

    now = lambda i: jnp.minimum(i, n - 1)
    lag = lambda i: jnp.maximum(i - 1, 0)
    grid_spec = pltpu.PrefetchScalarGridSpec(
        num_scalar_prefetch=1, grid=(n + 1,),
        in_specs=[pl.BlockSpec((tm, N), lambda i, c: ((now(i) // nb * 2 + 1 - c[0]) * nb + now(i) % nb, 0)),
                  pl.BlockSpec((tm, N), lambda i, c: ((lag(i) // nb * 2 + c[0]) * nb + lag(i) % nb, 0))],
        out_specs=pl.BlockSpec((tm, N), lambda i, c: (lag(i), 0)),
        scratch_shapes=_xchg_scratch(tm, N, BF16))
    return _pc(body, name=name, out_shape=jax.ShapeDtypeStruct((NS * RH, N), BF16), grid_spec=grid_spec,
               compiler_params=_cp("arbitrary"))(cvec, g, g)


def _rs_sum_join(part, got, kind, *, name):
    _, hr, hc = got.shape
    tm = _tile(hr, max(16, (3 * 2 ** 18 // hc) // 16 * 16), 16)
    if hr // tm < 2:
        tm = hr // 2
    nb = hr // tm
    jvec = (2 * lax.axis_index("x") + lax.axis_index("y")).reshape(1).astype(jnp.int32)

    def body(j_ref, p_ref, a_ref, b_ref, c_ref, mine_ref, theirs_ref, sbuf, land, ssem, rsem, credit):
        i = pl.program_id(0)
        _, peer = _pair_peer()

        @pl.when(i < nb)
        def _():
            v = ((p_ref[...].astype(F32) + a_ref[...].astype(F32)) + b_ref[...].astype(F32)) + c_ref[...].astype(F32)
            mine_ref[...] = v
            sbuf[...] = v

        def consume(land, ps):
            theirs_ref[...] = land[ps]

        _exchange_step(i, nb, sbuf, land, ssem, rsem, credit, peer, consume)

    now = lambda i: jnp.minimum(i, nb - 1)
    lag = lambda i: jnp.maximum(i - 1, 0)
    own_spec = (pl.BlockSpec((tm, hc), lambda i, j: (now(i), j[0])) if kind == "col"
                else pl.BlockSpec((tm, hc), lambda i, j: (j[0] * nb + now(i), 0)))
    got_spec = lambda q: pl.BlockSpec((None, tm, hc), functools.partial(lambda i, j, q: (q, now(i), 0), q=q))
    grid_spec = pltpu.PrefetchScalarGridSpec(
        num_scalar_prefetch=1, grid=(nb + 1,),
        in_specs=[own_spec, got_spec(0), got_spec(1), got_spec(2)],
        out_specs=(pl.BlockSpec((tm, hc), lambda i, j: (now(i), 0)), pl.BlockSpec((tm, hc), lambda i, j: (lag(i), 0))),
        scratch_shapes=[pltpu.VMEM((tm, hc), F32)] + _xchg_scratch(tm, hc, F32))
    shp = jax.ShapeDtypeStruct((hr, hc), F32)
    return _pc(body, name=name, out_shape=(shp, shp), grid_spec=grid_spec,
               compiler_params=_cp("arbitrary"))(jvec, part, got, got, got)


def _adamw_join(w, mine, theirs, m, v, *, name):
    R2, wc = w.shape
    hr, hc = mine.shape
    assert hr < R2 <= 2 * hr and hc >= wc
    pref, rows8 = max(8, 2 ** 18 // hc), _round_up(R2, 8)
    tm = max(t for t in range(8, hr + 1, 8) if hr % t == 0 and rows8 % t == 0 and (t <= pref or t == 8))
    nbh = hr // tm
    cvec = lax.axis_index("c").reshape(1).astype(jnp.int32)

    def body(c_ref, w_ref, a_ref, b_ref, m_ref, v_ref, g_ref, d_ref, m2_ref, v2_ref):
        gv = jnp.where(pl.program_id(0) // nbh == c_ref[0], a_ref[...], b_ref[...])[:, :wc]
        m2 = ADAM_B1 * m_ref[...] + (1.0 - ADAM_B1) * gv
        v2 = ADAM_B2 * v_ref[...] + (1.0 - ADAM_B2) * (gv * gv)
        m_hat = m2 / (1.0 - ADAM_B1 ** ADAM_STEP)
        v_hat = v2 / (1.0 - ADAM_B2 ** ADAM_STEP)
        g_ref[...] = gv
        d_ref[...] = -ADAM_LR * (m_hat / (jnp.sqrt(v_hat) + ADAM_EPS) + ADAM_WD * w_ref[...])
        m2_ref[...] = m2
        v2_ref[...] = v2

    wspec = pl.BlockSpec((tm, wc), lambda i, c: (i, 0))
    mine_spec = pl.BlockSpec((tm, hc), lambda i, c: (jnp.where(i // nbh == c[0], i % nbh, 0), 0))
    theirs_spec = pl.BlockSpec((tm, hc), lambda i, c: (jnp.where(i // nbh == c[0], 0, i % nbh), 0))
    grid_spec = pltpu.PrefetchScalarGridSpec(
        num_scalar_prefetch=1, grid=(pl.cdiv(R2, tm),),
        in_specs=[wspec, mine_spec, theirs_spec, wspec, wspec], out_specs=(wspec,) * 4)
    return _pc(body, name=name, out_shape=(jax.ShapeDtypeStruct((R2, wc), F32),) * 4, grid_spec=grid_spec,
               compiler_params=_cp("parallel"))(cvec, w, mine, theirs, m, v)


def _all_reduce_small(pack):
    R = pack.shape[0]

    def body(p_ref, o_ref, buf, send, recv):
        x, y, c, _ = _place()
        me = 4 * x + 2 * y + c
        buf[0] = p_ref[...]
        cps = []
        for r in range(1, 8):
            peer = (x if r & 4 == 0 else 1 - x, y if r & 2 == 0 else 1 - y, c if r & 1 == 0 else 1 - c)
            cp = pltpu.make_async_remote_copy(p_ref, buf.at[r], send.at[r - 1], recv.at[r - 1],
                                              device_id=peer, device_id_type=MESH)
            cp.start()
            cps.append(cp)
        for cp in cps:
            cp.wait()
        acc = buf[jnp.bitwise_xor(me, 0)]
        for k in range(1, 8):
            acc = acc + buf[jnp.bitwise_xor(me, k)]
        o_ref[...] = acc

    return _pc(body, name="allreduce_small", out_shape=jax.ShapeDtypeStruct((R, LANES), F32),
               in_specs=[pl.BlockSpec(memory_space=pltpu.VMEM)], out_specs=pl.BlockSpec(memory_space=pltpu.VMEM),
               scratch_shapes=[pltpu.VMEM((8, R, LANES), F32), pltpu.SemaphoreType.DMA((7,)),
                               pltpu.SemaphoreType.DMA((7,))])(pack)


def _pack_small(arrs):
    parts, sizes = [], []
    for a in arrs:
        f = a.reshape(-1).astype(F32)
        n = _round_up(f.shape[0], 8 * LANES)
        parts.append(jnp.pad(f, (0, n - f.shape[0])).reshape(n // LANES, LANES))
        sizes.append((a.shape, f.shape[0], n // LANES))
    return jnp.concatenate(parts, axis=0), sizes


def _unpack_small(pack, sizes):
    out, r0 = [], 0
    for shape, cnt, rows in sizes:
        out.append(pack[r0:r0 + rows].reshape(-1)[:cnt].reshape(shape))
        r0 += rows
    return out


def _ffn_fwd(xin, norm, weight, tag, gather_on_gateup, gather_on_down, gather_on_norm=lambda: (None, None)):
    cargo, landed = gather_on_norm()
    hb = _rms_fwd(xin, norm, name=tag + "_norm", cargo=cargo)
    if cargo is not None:
        hb, lands = hb
        landed(lands)
    cargo, landed = gather_on_gateup()
    res = _gateup(hb, weight("w_gate_up"), name=tag + "_gateup", tm=1024, cargo=cargo)
    if cargo is not None:
        res, lands = res
        landed(lands)
    gate, up, act = res
    cargo, landed = gather_on_down()
    xout = _mm(act, weight("w_down"), res=xin, scale=0.5, name=tag + "_down", tm=1024, tn=512, tk=8192, cargo=cargo)
    if cargo is not None:
        xout, lands = xout
        landed(lands)
    return xout, (hb, gate, up, act)


def _ffn_bwd(dy, dyb, xin, norm, wgu, wd, saved, tag, scatter, gu_piece=(0, 1), rest_on_dnorm=False):
    hb, gate, up, act = saved
    d_wd = _mm(act, dyb, mode="tn", scale=0.5, out_dtype=BF16, name=tag + "_dwd", tm=512, tn=1024, tk=4096)
    cargo = scatter(tag + "_w_down", d_wd)
    rest, rest_done = scatter.leftover()
    dgu = _dact_swiglu(dyb, wd, gate, up, name=tag + "_dact", cargo=rest)
    if rest is not None:
        dgu, lands = dgu
        rest_done(lands)
    d_wgu, lands = _mm(hb, dgu, mode="tn", halves="b", out_dtype=BF16, name=tag + "_dwgu", tm=1024, tn=512, tk=4096,
                       cargo=cargo)
    scatter.done(lands)
    cargo = scatter(tag + "_w_gate_up", d_wgu, piece=gu_piece)
    dh, lands = _mm(dgu, wgu, mode="nt", halves="a", out_dtype=BF16, name=tag + "_dh", tm=1024, tn=512, tk=5632,
                    cargo=cargo)
    scatter.done(lands)
    rest, rest_done = scatter.leftover() if rest_on_dnorm else (None, None)
    res = _rms_bwd(xin, dh, norm, dy, name=tag + "_dnorm", cargo=rest)
    if rest is not None:
        res, lands = res
        rest_done(lands)
    return res


def kernel(x, mem, ffn1_norm, ffn1_w_gate_up, ffn1_w_down, mix_norm, mem_norm, w_in, b_forget, pool_w, pool_scale, w_pool_up, fox_q_norm, fox_k_norm, w_fox_o, w_mem_kv, mem_q_norm, mem_k_norm, w_mem_o, w_out, ffn2_norm, ffn2_w_gate_up, ffn2_w_down, loss_target, m_ffn1_norm, m_ffn1_w_gate_up, m_ffn1_w_down, m_mix_norm, m_mem_norm, m_w_in, m_b_forget, m_pool_w, m_pool_scale, m_w_pool_up, m_fox_q_norm, m_fox_k_norm, m_w_fox_o, m_w_mem_kv, m_mem_q_norm, m_mem_k_norm, m_w_mem_o, m_w_out, m_ffn2_norm, m_ffn2_w_gate_up, m_ffn2_w_down, v_ffn1_norm, v_ffn1_w_gate_up, v_ffn1_w_down, v_mix_norm, v_mem_norm, v_w_in, v_b_forget, v_pool_w, v_pool_scale, v_w_pool_up, v_fox_q_norm, v_fox_k_norm, v_w_fox_o, v_w_mem_kv, v_mem_q_norm, v_mem_k_norm, v_w_mem_o, v_w_out, v_ffn2_norm, v_ffn2_w_gate_up, v_ffn2_w_down):
    P = dict(locals())
    big = ["ffn1_w_gate_up", "ffn1_w_down", "w_in", "w_pool_up", "w_fox_o", "w_mem_kv", "w_mem_o", "w_out",
           "ffn2_w_gate_up", "ffn2_w_down"]
    kinds = ["col", "row", "col", "col", "col", "row", "col", "row", "col", "row"]
    small = ["ffn1_norm", "mix_norm", "mem_norm", "b_forget", "pool_w", "pool_scale", "fox_q_norm", "fox_k_norm",
             "mem_q_norm", "mem_k_norm", "ffn2_norm"]
    order = ["ffn1_norm", "ffn1_w_gate_up", "ffn1_w_down", "mix_norm", "mem_norm", "w_in", "b_forget", "pool_w",
             "pool_scale", "w_pool_up", "fox_q_norm", "fox_k_norm", "w_fox_o", "w_mem_kv", "mem_q_norm", "mem_k_norm",
             "w_mem_o", "w_out", "ffn2_norm", "ffn2_w_gate_up", "ffn2_w_down"]

    nb, S, D = x.shape
    T = nb * S
    ML = mem.shape[1]
    NF, HD = b_forget.shape[-1], fox_q_norm.shape[-1]
    FW = NF * HD
    MW, MHD = w_mem_o.shape[1], mem_q_norm.shape[-1]
    PG, PD = pool_w.shape[1], pool_w.shape[2]
    PW = PG * PD
    n_in = w_in.shape[-1]
    n_in_pad = _round_up(n_in, LANES)
    in_w = 4 * n_in
    assert HD * 2 == LANES and MHD == LANES and PD == LANES and in_w == PW + 3 * FW + NF + MW + 3 * D

    shards = []
    for nme in big:
        wl = P[nme][0].astype(BF16)
        if nme == "w_in":
            wl = jnp.pad(wl, ((0, 0), (0, n_in_pad - n_in)))
        shards.append(wl)
    shard, kind_of, full = dict(zip(big, shards)), dict(zip(big, kinds)), {}

    def landed(names):
        def fill(lands):
            for nme, l in zip(names, lands):
                full[nme] = _ag_fill(l, shard[nme], kind_of[nme], name="ag_fill_" + nme)
        return fill

    def gather(names, piece=(0, 1), lands=None):
        return _ag_cargo([shard[nme] for nme in names], [kind_of[nme] for nme in names], piece, lands)

    landed(["ffn1_w_gate_up"])(_run_cargo(gather(["ffn1_w_gate_up"]), name="ag_first"))
    begun = {}

    def on_norm1():
        return gather(["ffn1_w_down"], (0, 4)), lambda lands: begun.update(ffn1_w_down=lands[0])

    def on_gateup1():
        def fill(lands):
            landed(["ffn1_w_down"])(lands[:1])
            begun["w_in"] = lands[1]
        return _join_cargo(gather(["ffn1_w_down"], (1, 4, 3), [begun["ffn1_w_down"]]),
                           gather(["w_in"], (0, 2))), fill

    x0 = x.reshape(T, D)
    x1, sv1 = _ffn_fwd(x0, ffn1_norm, lambda nme: full["ffn1_" + nme], "ffn1", on_gateup1,
                       lambda: (gather(["w_in"], (1, 2), [begun["w_in"]]), landed(["w_in"])), on_norm1)
    def gathered_cols(a, b):
        out = []
        for j in range(4):
            lo, hi = max(a, j * n_in), min(b, (j + 1) * n_in)
            if lo < hi:
                out.append(full["w_in"][:, j * (n_in_pad - n_in) + lo:j * (n_in_pad - n_in) + hi])
        return out

    o_q, o_k, o_v, o_f = PW, PW + FW, PW + 2 * FW, PW + 3 * FW
    o_qm, o_gate = o_f + NF, o_f + NF + MW
    p_q, p_k, p_v, p_pool, p_qm, p_f = 3 * D, 3 * D + FW, 3 * D + 2 * FW, 3 * D + 3 * FW, 3 * D + 3 * FW + PW, 3 * D + 3 * FW + PW + MW
    inp = _round_up(p_f + LANES, 512)
    win_p = jnp.concatenate(gathered_cols(o_gate, in_w) + gathered_cols(o_q, o_f) + gathered_cols(0, PW)
                            + gathered_cols(o_qm, o_gate) + gathered_cols(o_f, o_qm)
                            + [jnp.zeros((D, inp - p_f - NF), BF16)], axis=1)

    hb2 = _rms_fwd(x1, mix_norm, name="mix_norm")
    later = ["ffn2_w_down", "w_fox_o", "w_mem_kv", "w_pool_up"]
    z, lands = _mm(hb2, win_p, name="in_proj", tm=1024, tn=1536, cargo=gather(later))
    landed(later)(lands)

    vpool = _pool_fwd(z, p_pool // LANES, pool_w[0], pool_scale, nb, S, name="pool_fwd")
    y_pool = _mm(vpool, full["w_pool_up"], out_dtype=BF16, name="pool_up", tm=1024, tn=512)

    b128 = jnp.pad(b_forget, ((0, 0), (0, LANES - NF)))
    c3 = _fox_c(z, p_f // LANES, b128, nb, S, NF, name="fox_c")
    later = ["w_out", "w_mem_o"]
    (qa, ka, vb), lands = _fox_qk(z, p_q // LANES, p_k // LANES, p_v // LANES, c3, fox_q_norm, fox_k_norm, HD, NF,
                                  name="fox_qk", cargo=gather(later))
    landed(later)(lands)
    (o_fox, qb), lands = _fox_fwd(qa, ka, vb, nb, S, HD, name="fox_fwd",
                                  cargo=gather(["ffn2_w_gate_up"], (0, 4, 3)))
    begun["ffn2_w_gate_up"] = lands[0]
    y_fox = _mm(o_fox, full["w_fox_o"], out_dtype=BF16, name="fox_out", tm=1024, tn=512)

    mem2 = mem.reshape(nb * ML, D)
    memn = _rms_fwd(mem2, mem_norm, name="mem_norm")
    kv = _mm(memn, full["w_mem_kv"], name="mem_kv")
    kmn = _headnorm_fwd(kv, 0, MW, mem_k_norm, MHD, name="mem_knorm")
    vmb = kv[:, MW:].astype(BF16)
    qmn = _headnorm_fwd(z, p_qm // LANES, MW, mem_q_norm, MHD, name="mem_qnorm")
    o_mem = _mem_fwd(qmn, kmn, vmb, nb, S, ML, name="mem_fwd")
    y_mem = _mm(o_mem, full["w_mem_o"], out_dtype=BF16, name="mem_out", tm=1024, tn=512)

    merged, lands = _merge_fwd(z, y_pool, y_fox, y_mem, name="merge_fwd",
                               cargo=gather(["ffn2_w_gate_up"], (3, 4), [begun["ffn2_w_gate_up"]]))
    landed(["ffn2_w_gate_up"])(lands)
    x2 = _mm(merged, full["w_out"], res=x1, name="out_proj", tm=1024, tn=512)
    x3, sv2 = _ffn_fwd(x2, ffn2_norm, lambda nme: full["ffn2_" + nme], "ffn2", lambda: (None, None),
                       lambda: (None, None))
    dy3, dy3b, loss_part = _loss(x3, loss_target.reshape(T, D), name="loss")

    g, parts, got = {}, {}, {}
    rs_kind = dict(kind_of, w_in="row")

    leftovers = []

    def scatter(*named_grads, piece=(0, 1)):
        names = named_grads[0::2]
        for nme, grad in zip(names, named_grads[1::2]):
            parts[nme] = _rs_pair_sum(grad, rs_kind[nme], name="rs_pair_" + nme)
        scatter.names = names
        sent = (*piece, 1)[2]
        if sent < piece[1]:
            leftovers.append((names, (sent, piece[1], piece[1] - sent)))
        return _scatter_cargo([parts[nme] for nme in names], [rs_kind[nme] for nme in names], piece)

    def scattered(lands):
        got.update(zip(scatter.names, lands))

    def leftover():
        if not leftovers:
            return None, None
        names, piece = leftovers.pop(0)
        cargo = _scatter_cargo([parts[nme] for nme in names], [rs_kind[nme] for nme in names], piece,
                               [got[nme] for nme in names])
        return cargo, lambda lands: got.update(zip(names, lands))

    scatter.done, scatter.leftover = scattered, leftover
    dx2, dx2b, g["ffn2_norm"] = _ffn_bwd(dy3, dy3b, x2, ffn2_norm, full["ffn2_w_gate_up"], full["ffn2_w_down"], sv2,
                                         "ffn2", scatter, gu_piece=(0, 2))

    dmerged = _mm(dx2b, full["w_out"], mode="nt", out_dtype=BF16, name="d_merged", tm=1024, tn=512)
    g["w_out"] = _mm(merged, dx2b, mode="tn", out_dtype=BF16, name="d_w_out", tk=4096)
    dyp, dyf, dym, dgl = _merge_bwd(dmerged, z, y_pool, y_fox, y_mem, name="merge_bwd")

    g["w_pool_up"] = _mm(vpool, dyp, mode="tn", out_dtype=BF16, name="d_w_pool_up", tk=4096)
    dvp = _mm(dyp, full["w_pool_up"], mode="nt", name="d_vpool", tm=1024)
    du_pool, g["pool_w"], g["pool_scale"] = _pool_bwd(z, p_pool // LANES, dvp, pool_w[0], pool_scale, nb, S,
                                                      name="pool_bwd")

    g["w_fox_o"] = _mm(o_fox, dyf, mode="tn", out_dtype=BF16, name="d_w_fox_o", tk=4096)
    do_fox = _mm(dyf, full["w_fox_o"], mode="nt", out_dtype=BF16, name="d_o_fox", tm=1024)
    rest, rest_done = leftover()
    (dqa, dka, dvf, dcs), lands = _fox_bwd(qb, ka, vb, do_fox, nb, S, HD, name="fox_bwd", cargo=rest)
    rest_done(lands)
    dq, dgq = _headnorm_bwd(z, p_q // LANES, FW, dqa, fox_q_norm, HD, HD ** -0.5, True, BF16, name="fox_dq")
    dk, dgk = _headnorm_bwd(z, p_k // LANES, FW, dka, fox_k_norm, HD, 1.0, True, BF16, name="fox_dk")
    g["fox_q_norm"], g["fox_k_norm"] = _fold_gain(dgq, HD), _fold_gain(dgk, HD)
    dc = -dcs[:, :, :LANES // HD, :].reshape(nb, NF, S).transpose(0, 2, 1).reshape(T, NF)
    df, db = _fox_c_bwd(jnp.pad(dc, ((0, 0), (0, LANES - NF))), z, p_f // LANES, b128, nb, S, name="fox_c_bwd")
    g["b_forget"] = db[:, :NF]

    g["w_mem_o"] = _mm(o_mem, dym, mode="tn", out_dtype=BF16, name="d_w_mem_o", tk=4096)
    do_mem = _mm(dym, full["w_mem_o"], mode="nt", out_dtype=BF16, name="d_o_mem", tm=1024)
    dqmn, dkmn, dvm = _mem_bwd(qmn, kmn, vmb, do_mem, nb, S, ML, name="mem_bwd")
    dqm, g["mem_q_norm"] = _headnorm_bwd(z, p_qm // LANES, MW, dqmn, mem_q_norm, MHD, 1.0, False, BF16, name="mem_dq")
    dkm, g["mem_k_norm"] = _headnorm_bwd(kv, 0, MW, dkmn, mem_k_norm, MHD, 1.0, False, BF16, name="mem_dk")
    dkv = jnp.concatenate([dkm, dvm.astype(BF16)], axis=1)
    g["w_mem_kv"] = _mm(memn, dkv, mode="tn", out_dtype=BF16, name="d_w_mem_kv")
    dmemn = _mm(dkv, full["w_mem_kv"], mode="nt", name="d_memn")
    _, _, g["mem_norm"] = _rms_bwd(mem2, dmemn, mem_norm, jnp.zeros_like(mem2), name="mem_dnorm")

    dz = jnp.concatenate([dgl, dq, dk, dvf.astype(BF16), du_pool, dqm, df.astype(BF16),
                          jnp.zeros((T, inp - p_f - LANES), BF16)], axis=1)
    cargo = scatter(*sum([[nme, g[nme]] for nme in ("w_out", "w_pool_up", "w_fox_o", "w_mem_o", "w_mem_kv")], []))
    dwin_p, lands = _mm(dz, hb2, mode="tn", out_dtype=BF16, name="d_w_in", tm=512, tn=1024, tk=4096, cargo=cargo)
    scattered(lands)
    dwin_o = jnp.concatenate([dwin_p[p_pool:p_qm], dwin_p[p_q:p_pool], dwin_p[p_f:p_f + NF], dwin_p[p_qm:p_f],
                              dwin_p[:p_q]], axis=0)
    cargo = scatter("w_in", jnp.concatenate(
        [jnp.pad(dwin_o[j * n_in:(j + 1) * n_in], ((0, n_in_pad - n_in), (0, 0))) for j in range(4)], axis=0),
        piece=(0, 2))
    dh2, lands = _mm(dz, win_p, mode="nt", out_dtype=BF16, name="d_h2", tm=1024, tn=512, tk=3584, cargo=cargo)
    scattered(lands)
    dx1, dx1b, g["mix_norm"] = _rms_bwd(x1, dh2, mix_norm, dx2, name="mix_dnorm")

    dx0, _, g["ffn1_norm"] = _ffn_bwd(dx1, dx1b, x0, ffn1_norm, full["ffn1_w_gate_up"], full["ffn1_w_down"], sv1,
                                      "ffn1", scatter, gu_piece=(0, 4, 3), rest_on_dnorm=True)

    halves = {nme: _rs_sum_join(parts[nme], got[nme], rs_kind[nme], name="rs_join_" + nme) for nme in big}
    gfull = {}

    gpack, sizes = _pack_small([g[nme].reshape(P[nme].shape) for nme in small] + [loss_part[:, :1]])
    gsum = _unpack_small(_all_reduce_small(gpack), sizes)
    loss = gsum[-1].reshape(())
    for nme, a in zip(small, gsum[:-1]):
        gfull[nme] = a

    delta, new_m, new_v = {}, {}, {}
    for nme in big:
        shp = P[nme].shape
        two = (lambda a: a.reshape(shp[-2:]).T) if nme == "w_in" else (lambda a: a.reshape(shp[-2:]))
        back = (lambda a: a.T.reshape(shp)) if nme == "w_in" else (lambda a: a.reshape(shp))
        gfull[nme], delta[nme], new_m[nme], new_v[nme] = [
            back(a) for a in _adamw_join(two(P[nme]), *halves[nme], two(P["m_" + nme]), two(P["v_" + nme]),
                                         name="adamw_" + nme)]
    wpack, _ = _pack_small([P[nme] for nme in small])
    mpack, _ = _pack_small([P["m_" + nme] for nme in small])
    vpack, _ = _pack_small([P["v_" + nme] for nme in small])
    gpack2, ssz = _pack_small([gfull[nme] for nme in small])
    for dct, pk in zip((delta, new_m, new_v), _adamw(wpack, gpack2, mpack, vpack, name="adamw_small")):
        for nme, a in zip(small, _unpack_small(pk, ssz)):
            dct[nme] = a

    grad_x = dx0.reshape(nb, S, D)
    return (loss, grad_x, *[gfull[nme] for nme in order], *[delta[nme] for nme in order],
            *[new_m[nme] for nme in order], *[new_v[nme] for nme in order])
```

```python
import functools

import jax
import jax.numpy as jnp
from jax import lax
from jax.experimental import pallas as pl
from jax.experimental.pallas import tpu as pltpu

F32 = jnp.float32
BF16 = jnp.bfloat16
MESH = pl.DeviceIdType.MESH
EPS = 1e-6
POOL_WINDOWS = (2, 4, 8, 16)
ADAM_LR, ADAM_B1, ADAM_B2, ADAM_EPS, ADAM_WD, ADAM_STEP = 0.001, 0.9, 0.999, 1e-08, 0.01, 10
LANES = 128
VMEM_LIMIT = 56 * 1024 * 1024
NEG = -1e30


def _pc(body, *, name, cargo=None, **kw):
    if cargo is not None:
        body, kw = _load_cargo(body, cargo, kw)
    return pl.pallas_call(body, name=name, **kw)


def _cp(*sem):
    return pltpu.CompilerParams(dimension_semantics=sem, vmem_limit_bytes=VMEM_LIMIT)


def _tile(n, pref, mult=LANES):
    if n <= pref:
        return n
    t = (pref // mult) * mult
    while t >= mult:
        if n % t == 0:
            return t
        t -= mult
    return n


def _round_up(n, m):
    return (n + m - 1) // m * m


_DIMS = {"nn": (((1,), (0,)), ((), ())), "nt": (((1,), (1,)), ((), ())), "tn": (((0,), (0,)), ((), ()))}


def _split_cargo(res, cargo):
    if cargo is None:
        return res
    nco = len(cargo.out_shapes)
    own = res[:len(res) - nco]
    return (own[0] if len(own) == 1 else own), list(res[len(res) - nco:])


def _mm(a, b, *, name, mode="nn", out_dtype=F32, scale=1.0, res=None, tm=512, tn=512, tk=2048, cargo=None,
        halves=None):
    if halves == "a":
        assert mode == "nt"
        (_, M, Kh), (N, K2) = a.shape, b.shape
        K = 2 * Kh
    elif halves == "b":
        assert mode == "tn"
        (K, M), (_, K2, Nh) = a.shape, b.shape
        N = 2 * Nh
    elif mode == "nn":
        (M, K), (K2, N) = a.shape, b.shape
    elif mode == "nt":
        (M, K), (N, K2) = a.shape, b.shape
    else:
        (K, M), (K2, N) = a.shape, b.shape
    assert K == K2, (name, a.shape, b.shape)
    tm = _tile(M, tm, 8 if M % LANES else LANES)
    tn = _tile(N // 2 if halves == "b" else N, tn)
    tk = _tile(K // 2 if halves == "a" else K, tk)
    nk = K // tk
    dims = _DIMS[mode]
    if halves == "a":
        a_spec = pl.BlockSpec((None, tm, tk), lambda i, j, k: (k // (nk // 2), i, k % (nk // 2)))
    elif mode == "tn":
        a_spec = pl.BlockSpec((tk, tm), lambda i, j, k: (k, i))
    else:
        a_spec = pl.BlockSpec((tm, tk), lambda i, j, k: (i, k))
    if halves == "b":
        njh = N // 2 // tn
        b_spec = pl.BlockSpec((None, tk, tn), lambda i, j, k: (j // njh, k, j % njh))
    elif mode == "nt":
        b_spec = pl.BlockSpec((tn, tk), lambda i, j, k: (j, k))
    else:
        b_spec = pl.BlockSpec((tk, tn), lambda i, j, k: (k, j))
    o_spec = pl.BlockSpec((tm, tn), lambda i, j, k: (i, j))
    has_res = res is not None

    def body(*refs):
        a_ref, b_ref = refs[:2]
        r_ref = refs[2] if has_res else None
        o_ref = refs[3] if has_res else refs[2]

        def finish(v):
            v = v * scale
            if has_res:
                v = r_ref[...] + v
            o_ref[...] = v.astype(o_ref.dtype)

        prod = lax.dot_general(a_ref[...], b_ref[...], dims, preferred_element_type=F32)
        if nk == 1:
            finish(prod)
            return
        acc, k = refs[-1], pl.program_id(2)

        @pl.when(k == 0)
        def _():
            acc[...] = prod

        @pl.when(k > 0)
        def _():
            acc[...] += prod

        @pl.when(k == nk - 1)
        def _():
            finish(acc[...])

    out = _pc(body, name=name, cargo=cargo, out_shape=jax.ShapeDtypeStruct((M, N), out_dtype),
              grid=(M // tm, N // tn, nk),
              in_specs=[a_spec, b_spec] + ([o_spec] if has_res else []), out_specs=o_spec,
              scratch_shapes=[] if nk == 1 else [pltpu.VMEM((tm, tn), F32)],
              compiler_params=_cp("parallel", "parallel", "arbitrary"))(
                  *((a, b, res) if has_res else (a, b)), *(cargo.arrays if cargo else ()))
    return _split_cargo(out, cargo)


def _dact_swiglu(dyb, wd, gate, up, *, name, tm=1024, tn=512, cargo=None):
    M, K = dyb.shape
    Fh = wd.shape[0]
    tm, tn = _tile(M, tm), _tile(Fh, tn)

    def body(dy_ref, w_ref, g_ref, u_ref, o_ref):
        d = lax.dot_general(dy_ref[...], w_ref[...], _DIMS["nt"], preferred_element_type=F32) * 0.5
        g, u = g_ref[...].astype(F32), u_ref[...].astype(F32)
        s = jax.nn.sigmoid(g)
        o_ref[0] = (d * u * (s * (1.0 + g * (1.0 - s)))).astype(BF16)
        o_ref[1] = (d * (g * s)).astype(BF16)

    t_spec = pl.BlockSpec((tm, tn), lambda i, j: (i, j))
    out = _pc(body, name=name, cargo=cargo, out_shape=jax.ShapeDtypeStruct((2, M, Fh), BF16),
              grid=(M // tm, Fh // tn),
              in_specs=[pl.BlockSpec((tm, K), lambda i, j: (i, 0)), pl.BlockSpec((tn, K), lambda i, j: (j, 0)),
                        t_spec, t_spec],
              out_specs=pl.BlockSpec((2, tm, tn), lambda i, j: (0, i, j)),
              compiler_params=_cp("parallel", "parallel"))(dyb, wd, gate, up, *(cargo.arrays if cargo else ()))
    return _split_cargo(out, cargo)


def _gateup(h, wgu, *, name, tm=512, tn=512, cargo=None):
    M, K = h.shape
    Fh = wgu.shape[1] // 2
    tm, tn = _tile(M, tm), _tile(Fh, tn)
    nj = Fh // tn

    def body(h_ref, wg_ref, wu_ref, g_ref, u_ref, a_ref):
        hv = h_ref[...]
        g = jnp.dot(hv, wg_ref[...], preferred_element_type=F32)
        u = jnp.dot(hv, wu_ref[...], preferred_element_type=F32)
        g_ref[...] = g.astype(BF16)
        u_ref[...] = u.astype(BF16)
        a_ref[...] = (g * jax.nn.sigmoid(g) * u).astype(BF16)

    o_spec = pl.BlockSpec((tm, tn), lambda i, j: (i, j))
    shp = jax.ShapeDtypeStruct((M, Fh), BF16)
    out = _pc(body, name=name, cargo=cargo, out_shape=(shp, shp, shp), grid=(M // tm, nj),
              in_specs=[pl.BlockSpec((tm, K), lambda i, j: (i, 0)),
                        pl.BlockSpec((K, tn), lambda i, j: (0, j)),
                        pl.BlockSpec((K, tn), lambda i, j: (0, j + nj))],
              out_specs=(o_spec, o_spec, o_spec),
              compiler_params=_cp("parallel", "parallel"))(h, wgu, wgu, *(cargo.arrays if cargo else ()))
    return _split_cargo(out, cargo)


def _rowmap(fn, ins, outs, *, rows, tm, name, accs=(), cargo=None):
    tm = _tile(rows, tm, 8)
    arrs, in_specs = [], []
    for d in ins:
        if d[0] == "row":
            _, a, w, cb = d
            w = a.shape[1] if w is None else w
            in_specs.append(pl.BlockSpec((tm, w), functools.partial(lambda i, cb: (i, cb), cb=cb)))
        else:
            a = d[1]
            in_specs.append(pl.BlockSpec(a.shape, functools.partial(lambda i, nd: (0,) * nd, nd=a.ndim)))
        arrs.append(a)
    out_shape = [jax.ShapeDtypeStruct((rows, w), dt) for w, dt in outs]
    out_specs = [pl.BlockSpec((tm, w), lambda i: (i, 0)) for w, _ in outs]
    for shp, dt in accs:
        out_shape.append(jax.ShapeDtypeStruct(shp, dt))
        out_specs.append(pl.BlockSpec(shp, functools.partial(lambda i, nd: (0,) * nd, nd=len(shp))))
    nin, nout, nacc = len(ins), len(outs), len(accs)

    def body(*refs):
        res = fn(*[r[...] for r in refs[:nin]])
        res = res if isinstance(res, (tuple, list)) else (res,)
        for r, v in zip(refs[nin:nin + nout], res[:nout]):
            r[...] = v.astype(r.dtype)
        if nacc:
            acc_refs = refs[nin + nout:]

            @pl.when(pl.program_id(0) == 0)
            def _():
                for r in acc_refs:
                    r[...] = jnp.zeros_like(r)

            for r, v in zip(acc_refs, res[nout:]):
                r[...] += v

    out = _pc(body, name=name, cargo=cargo, out_shape=tuple(out_shape), grid=(rows // tm,), in_specs=in_specs,
              out_specs=tuple(out_specs),
              compiler_params=_cp("arbitrary" if nacc else "parallel"))(*arrs, *(cargo.arrays if cargo else ()))
    return _split_cargo(out, cargo) if cargo else (out[0] if len(out) == 1 else out)


def _rms_fwd(x, g, *, name, cargo=None):
    def fn(xv, gv):
        return xv * lax.rsqrt(jnp.mean(xv * xv, axis=-1, keepdims=True) + EPS) * gv
    return _rowmap(fn, [("row", x, None, 0), ("bc", g)], [(x.shape[1], BF16)], rows=x.shape[0], tm=256, name=name,
                   cargo=cargo)


def _rms_bwd(x, dh, g, dres, *, name, cargo=None):
    D = x.shape[1]

    def fn(xv, dhv, gv, drv):
        r = lax.rsqrt(jnp.mean(xv * xv, axis=-1, keepdims=True) + EPS)
        xh = xv * r
        dhf = dhv.astype(F32)
        dxn = dhf * gv
        dx = drv + r * (dxn - xh * jnp.mean(dxn * xh, axis=-1, keepdims=True))
        return dx, dx, jnp.sum(dhf * xh, axis=0, keepdims=True)

    return _rowmap(fn, [("row", x, None, 0), ("row", dh, None, 0), ("bc", g), ("row", dres, None, 0)],
                   [(D, F32), (D, BF16)], rows=x.shape[0], tm=256, name=name, accs=[((1, D), F32)], cargo=cargo)


def _loss(y, tgt, *, name):
    D = y.shape[1]

    def fn(yv, tv):
        e = yv - tv
        d = e * (1.0 / D)
        part = 0.5 * jnp.sum(jnp.sum(e * e, axis=-1, keepdims=True) * (1.0 / D), axis=0, keepdims=True)
        return d, d, jnp.broadcast_to(part, (1, LANES))

    return _rowmap(fn, [("row", y, None, 0), ("row", tgt, None, 0)], [(D, F32), (D, BF16)],
                   rows=y.shape[0], tm=256, name=name, accs=[((1, LANES), F32)])


def _merge_fwd(z, yp, yf, ym, *, name, cargo=None):
    D = yp.shape[1]

    def fn(gp, gf, gm, a, b, c):
        return jax.nn.sigmoid(gp) * a + jax.nn.sigmoid(gf) * b + jax.nn.sigmoid(gm) * c

    return _rowmap(fn, [("row", z, D, 0), ("row", z, D, 1), ("row", z, D, 2),
                        ("row", yp, None, 0), ("row", yf, None, 0), ("row", ym, None, 0)],
                   [(D, BF16)], rows=yp.shape[0], tm=128, name=name, cargo=cargo)


def _merge_bwd(dm, z, yp, yf, ym, *, name):
    D = yp.shape[1]

    def fn(d, gp, gf, gm, a, b, c):
        outs, dgl = [], []
        for gl, yv in ((gp, a), (gf, b), (gm, c)):
            s = jax.nn.sigmoid(gl)
            outs.append(d * s)
            dgl.append(d * yv * (s * (1.0 - s)))
        return (*outs, jnp.concatenate(dgl, axis=-1))

    return _rowmap(fn, [("row", dm, None, 0), ("row", z, D, 0), ("row", z, D, 1), ("row", z, D, 2),
                        ("row", yp, None, 0), ("row", yf, None, 0), ("row", ym, None, 0)],
                   [(D, BF16), (D, BF16), (D, BF16), (3 * D, BF16)], rows=yp.shape[0], tm=128, name=name)


def _adamw(w, g, m, v, *, name):
    C = w.shape[1]

    def fn(wv, gv, mv, vv):
        m2 = ADAM_B1 * mv + (1.0 - ADAM_B1) * gv
        v2 = ADAM_B2 * vv + (1.0 - ADAM_B2) * (gv * gv)
        m_hat = m2 / (1.0 - ADAM_B1 ** ADAM_STEP)
        v_hat = v2 / (1.0 - ADAM_B2 ** ADAM_STEP)
        return -ADAM_LR * (m_hat / (jnp.sqrt(v_hat) + ADAM_EPS) + ADAM_WD * wv), m2, v2

    tm = max(8, (262144 // C) // 8 * 8)
    return _rowmap(fn, [("row", a, None, 0) for a in (w, g, m, v)], [(C, F32)] * 3,
                   rows=w.shape[0], tm=tm, name=name)


def _seg_mean(v, hd):
    if hd == LANES:
        return jnp.mean(v, axis=-1, keepdims=True)
    lo = lax.broadcasted_iota(jnp.int32, v.shape, 1) < hd
    s0 = jnp.sum(jnp.where(lo, v, 0.0), axis=-1, keepdims=True)
    s1 = jnp.sum(jnp.where(lo, 0.0, v), axis=-1, keepdims=True)
    return jnp.where(lo, s0, s1) * (1.0 / hd)


def _gain128(g, hd):
    g = g.reshape(1, hd).astype(F32)
    return g if hd == LANES else jnp.concatenate([g, g], axis=-1)


def _headnorm_fwd(x, cb0, width, gain, hd, *, name):
    rows = x.shape[0]
    tm = _tile(rows, 512, 8)
    g128 = _gain128(gain, hd)

    def body(x_ref, g_ref, o_ref):
        xv = x_ref[...]
        o_ref[...] = (xv * lax.rsqrt(_seg_mean(xv * xv, hd) + EPS) * g_ref[...]).astype(BF16)

    return _pc(body, name=name, out_shape=jax.ShapeDtypeStruct((rows, width), BF16),
               grid=(rows // tm, width // LANES),
               in_specs=[pl.BlockSpec((tm, LANES), lambda i, j: (i, cb0 + j)),
                         pl.BlockSpec((1, LANES), lambda i, j: (0, 0))],
               out_specs=pl.BlockSpec((tm, LANES), lambda i, j: (i, j)),
               compiler_params=_cp("parallel", "parallel"))(x, g128)


def _headnorm_bwd(x, cb0, width, dy, gain, hd, post, spread, out_dtype, *, name):
    rows = x.shape[0]
    tm = _tile(rows, 512, 8)
    g128 = _gain128(gain, hd)
    dyw = 2 * LANES if spread else LANES

    def body(x_ref, dy_ref, g_ref, dx_ref, dg_ref):
        xv = x_ref[...]
        dyv = dy_ref[...]
        if spread:
            dyv = jnp.concatenate([dyv[:, :hd], dyv[:, LANES:LANES + hd]], axis=-1)
        dyv = dyv * post
        r = lax.rsqrt(_seg_mean(xv * xv, hd) + EPS)
        xh = xv * r
        dxn = dyv * g_ref[...]
        dx_ref[...] = (r * (dxn - xh * _seg_mean(dxn * xh, hd))).astype(dx_ref.dtype)

        @pl.when((pl.program_id(0) == 0) & (pl.program_id(1) == 0))
        def _():
            dg_ref[...] = jnp.zeros_like(dg_ref)

        dg_ref[...] += jnp.sum(dyv * xh, axis=0, keepdims=True)

    return _pc(body, name=name,
               out_shape=(jax.ShapeDtypeStruct((rows, width), out_dtype), jax.ShapeDtypeStruct((1, LANES), F32)),
               grid=(rows // tm, width // LANES),
               in_specs=[pl.BlockSpec((tm, LANES), lambda i, j: (i, cb0 + j)),
                         pl.BlockSpec((tm, dyw), lambda i, j: (i, j)),
                         pl.BlockSpec((1, LANES), lambda i, j: (0, 0))],
               out_specs=(pl.BlockSpec((tm, LANES), lambda i, j: (i, j)),
                          pl.BlockSpec((1, LANES), lambda i, j: (0, 0))),
               compiler_params=_cp("arbitrary", "arbitrary"))(x, dy, g128)


def _fold_gain(dg, hd):
    return dg if hd == LANES else dg[:, :hd] + dg[:, hd:]


def _shift_down(v, k, row):
    return jnp.where(row >= k, pltpu.roll(v, k, 0), 0.0)


def _shift_up(v, k, row, S):
    return jnp.where(row < S - k, pltpu.roll(v, S - k, 0), 0.0)


def _window_sum(v, g, row, shift):
    s, outs = v, []
    for k in (1, 2, 4, 8):
        s = s + shift(s, k)
        outs.append(s)
    return jnp.where(g == 0, outs[0], jnp.where(g == 1, outs[1], jnp.where(g == 2, outs[2], outs[3])))


def _pool_fwd(z, cb0, pool_w, pool_scale, nb, S, *, name):
    PG, PD = pool_w.shape[0], pool_w.shape[1]
    assert PD == LANES and PG <= len(POOL_WINDOWS)
    pw = pool_w.astype(BF16)

    def body(u_ref, w_ref, s_ref, o_ref):
        g = pl.program_id(0)
        u = u_ref[...]
        row = lax.broadcasted_iota(jnp.int32, u.shape, 0)
        cnt = jnp.minimum(row + 1, jnp.left_shift(2, g)).astype(F32)
        pooled = _window_sum(u, g, row, lambda v, k: _shift_down(v, k, row)) / cnt
        mixed = jnp.dot((pooled - u).astype(BF16), w_ref[...], preferred_element_type=F32)
        o_ref[...] = (mixed * s_ref[...]).astype(BF16)

    return _pc(body, name=name, out_shape=jax.ShapeDtypeStruct((nb * S, PG * PD), BF16), grid=(PG, nb),
               in_specs=[pl.BlockSpec((S, PD), lambda g, b: (b, cb0 + g)),
                         pl.BlockSpec((None, PD, PD), lambda g, b: (g, 0, 0)),
                         pl.BlockSpec((1, PD), lambda g, b: (0, g))],
               out_specs=pl.BlockSpec((S, PD), lambda g, b: (b, g)),
               compiler_params=_cp("parallel", "parallel"))(z, pw, pool_scale)


def _pool_bwd(z, cb0, dv, pool_w, pool_scale, nb, S, *, name):
    PG, PD = pool_w.shape[0], pool_w.shape[1]
    pw = pool_w.astype(BF16)

    def body(u_ref, dv_ref, w_ref, s_ref, du_ref, dw_ref, ds_ref):
        g, b = pl.program_id(0), pl.program_id(1)
        u, dvv, w = u_ref[...], dv_ref[...], w_ref[...]
        row = lax.broadcasted_iota(jnp.int32, u.shape, 0)
        cnt = jnp.minimum(row + 1, jnp.left_shift(2, g)).astype(F32)
        diff = (_window_sum(u, g, row, lambda v, k: _shift_down(v, k, row)) / cnt - u).astype(BF16)
        mixed = jnp.dot(diff, w, preferred_element_type=F32)
        dmixed = (dvv * s_ref[...]).astype(BF16)
        ddiff = lax.dot_general(dmixed, w, _DIMS["nt"], preferred_element_type=F32)
        du_ref[...] = (_window_sum(ddiff / cnt, g, row, lambda v, k: _shift_up(v, k, row, S)) - ddiff).astype(BF16)

        @pl.when(b == 0)
        def _():
            dw_ref[...] = jnp.zeros_like(dw_ref)
            ds_ref[...] = jnp.zeros_like(ds_ref)

        dw_ref[...] += lax.dot_general(diff, dmixed, _DIMS["tn"], preferred_element_type=F32)
        ds_ref[...] += jnp.sum(dvv * mixed, axis=0, keepdims=True)

    return _pc(body, name=name,
               out_shape=(jax.ShapeDtypeStruct((nb * S, PG * PD), BF16), jax.ShapeDtypeStruct((PG, PD, PD), F32),
                          jax.ShapeDtypeStruct((1, PG * PD), F32)),
               grid=(PG, nb),
               in_specs=[pl.BlockSpec((S, PD), lambda g, b: (b, cb0 + g)),
                         pl.BlockSpec((S, PD), lambda g, b: (b, g)),
                         pl.BlockSpec((None, PD, PD), lambda g, b: (g, 0, 0)),
                         pl.BlockSpec((1, PD), lambda g, b: (0, g))],
               out_specs=(pl.BlockSpec((S, PD), lambda g, b: (b, g)),
                          pl.BlockSpec((None, PD, PD), lambda g, b: (g, 0, 0)),
                          pl.BlockSpec((1, PD), lambda g, b: (0, g))),
               compiler_params=_cp("parallel", "arbitrary"))(z, dv, pw, pool_scale)


def _split3(c):
    hi = c.astype(BF16)
    r1 = c - hi.astype(F32)
    mid = r1.astype(BF16)
    lo = (r1 - mid.astype(F32)).astype(BF16)
    return hi, mid, lo


def _log_sigmoid(v):
    return jnp.minimum(v, 0.0) - jnp.log(1.0 + jnp.exp(-jnp.abs(v)))


def _fox_c(z, cb, b128, nb, S, NF, *, name):
    def body(f_ref, b_ref, hi_ref, mid_ref, lo_ref):
        v = f_ref[...] + b_ref[...]
        lane = lax.broadcasted_iota(jnp.int32, v.shape, 1)
        row = lax.broadcasted_iota(jnp.int32, v.shape, 0)
        c = jnp.where(lane < NF, _log_sigmoid(v), 0.0)
        k = 1
        while k < S:
            c = c + _shift_down(c, k, row)
            k *= 2
        hi, mid, lo = _split3(c)
        hi_ref[...], mid_ref[...], lo_ref[...] = hi, mid, lo

    shp = jax.ShapeDtypeStruct((nb * S, LANES), BF16)
    spec = pl.BlockSpec((S, LANES), lambda i: (i, 0))
    return _pc(body, name=name, out_shape=(shp, shp, shp), grid=(nb,),
               in_specs=[pl.BlockSpec((S, LANES), lambda i: (i, cb)), pl.BlockSpec((1, LANES), lambda i: (0, 0))],
               out_specs=(spec, spec, spec), compiler_params=_cp("parallel"))(z, b128)


def _fox_c_bwd(dc, z, cb, b128, nb, S, *, name):
    def body(dc_ref, f_ref, b_ref, df_ref, db_ref):
        d = dc_ref[...]
        row = lax.broadcasted_iota(jnp.int32, d.shape, 0)
        k = 1
        while k < S:
            d = d + _shift_up(d, k, row, S)
            k *= 2
        df = d * jax.nn.sigmoid(-(f_ref[...] + b_ref[...]))
        df_ref[...] = df

        @pl.when(pl.program_id(0) == 0)
        def _():
            db_ref[...] = jnp.zeros_like(db_ref)

        db_ref[...] += jnp.sum(df, axis=0, keepdims=True)

    return _pc(body, name=name,
               out_shape=(jax.ShapeDtypeStruct((nb * S, LANES), F32), jax.ShapeDtypeStruct((1, LANES), F32)),
               grid=(nb,),
               in_specs=[pl.BlockSpec((S, LANES), lambda i: (i, 0)), pl.BlockSpec((S, LANES), lambda i: (i, cb)),
                         pl.BlockSpec((1, LANES), lambda i: (0, 0))],
               out_specs=(pl.BlockSpec((S, LANES), lambda i: (i, 0)), pl.BlockSpec((1, LANES), lambda i: (0, 0))),
               compiler_params=_cp("arbitrary"))(dc, z, b128)


def _fox_qk(z, cbq, cbk, cbv, c3, gq, gk, HD, NF, *, name, cargo=None):
    T = z.shape[0]
    HP = NF * HD // LANES
    tm = _tile(T, 512, 8)
    scale = HD ** -0.5
    gq128, gk128 = _gain128(gq, HD), _gain128(gk, HD)

    def body(q_ref, k_ref, v_ref, hi_ref, mid_ref, lo_ref, gq_ref, gk_ref, qa_ref, ka_ref, vb_ref):
        hp = pl.program_id(1)
        q, k = q_ref[...], k_ref[...]
        qn = q * lax.rsqrt(_seg_mean(q * q, HD) + EPS) * (gq_ref[...] * scale)
        kn = k * lax.rsqrt(_seg_mean(k * k, HD) + EPS) * gk_ref[...]
        vb_ref[...] = v_ref[...].astype(BF16)
        c3v = jnp.concatenate([hi_ref[...], mid_ref[...], lo_ref[...]], axis=-1)
        r = lax.broadcasted_iota(jnp.int32, (3 * LANES, LANES - HD), 0)
        cc = lax.broadcasted_iota(jnp.int32, (3 * LANES, LANES - HD), 1)
        lane = lax.broadcasted_iota(jnp.int32, (tm, LANES - HD), 1)
        for hh in range(LANES // HD):
            head = hp * (LANES // HD) + hh
            sel_q = jnp.where((cc < 3) & (r == head + LANES * cc), 1.0, 0.0).astype(BF16)
            sel_k = jnp.where((cc >= 3) & (cc < 6) & (r == head + LANES * (cc - 3)), 1.0, 0.0).astype(BF16)
            qaug = jnp.dot(c3v, sel_q, preferred_element_type=F32) + jnp.where((lane >= 3) & (lane < 6), 1.0, 0.0)
            kaug = (jnp.where((lane < 3) | ((lane >= 6) & (lane < 9)), 1.0, 0.0)
                    - jnp.dot(c3v, sel_k, preferred_element_type=F32))
            sl = slice(hh * HD, (hh + 1) * HD)
            qa_ref[:, hh * LANES:(hh + 1) * LANES] = jnp.concatenate([qn[:, sl], qaug], axis=-1).astype(BF16)
            ka_ref[:, hh * LANES:(hh + 1) * LANES] = jnp.concatenate([kn[:, sl], kaug], axis=-1).astype(BF16)

    aw = (LANES // HD) * LANES
    blk = lambda cb: pl.BlockSpec((tm, LANES), functools.partial(lambda i, j, cb: (i, cb + j), cb=cb))
    cspec = pl.BlockSpec((tm, LANES), lambda i, j: (i, 0))
    gspec = pl.BlockSpec((1, LANES), lambda i, j: (0, 0))
    out = _pc(body, name=name, cargo=cargo,
              out_shape=(jax.ShapeDtypeStruct((T, HP * aw), BF16), jax.ShapeDtypeStruct((T, HP * aw), BF16),
                         jax.ShapeDtypeStruct((T, NF * HD), BF16)),
              grid=(T // tm, HP),
              in_specs=[blk(cbq), blk(cbk), blk(cbv), cspec, cspec, cspec, gspec, gspec],
              out_specs=(pl.BlockSpec((tm, aw), lambda i, j: (i, j)), pl.BlockSpec((tm, aw), lambda i, j: (i, j)),
                         pl.BlockSpec((tm, LANES), lambda i, j: (i, j))),
              compiler_params=_cp("parallel", "parallel"))(z, z, z, *c3, gq128, gk128,
                                                           *(cargo.arrays if cargo else ()))
    return _split_cargo(out, cargo)


def _fox_logits(qa, ka, fill):
    s = lax.dot_general(qa, ka, _DIMS["nt"], preferred_element_type=F32)
    tq, L = s.shape
    diag = s[:, L - tq:]
    keep = lax.broadcasted_iota(jnp.int32, diag.shape, 1) <= lax.broadcasted_iota(jnp.int32, diag.shape, 0)
    diag = jnp.where(keep, diag, fill)
    return diag if L == tq else jnp.concatenate([s[:, :L - tq], diag], axis=-1), keep


def _fox_fwd(qa, ka, vb, nb, S, HD, *, name, tq=256, cargo=None):
    T = qa.shape[0]
    nh = LANES // HD
    aw = nh * LANES
    HP = qa.shape[1] // aw
    tq = _tile(S, tq, 8)
    nq = S // tq

    def body(qa_ref, ka_ref, v_ref, o_ref, qb_ref):
        lane = lax.broadcasted_iota(jnp.int32, (tq, LANES), 1)
        for k in range(nq):
            @pl.when(pl.program_id(2) == k)
            def _(k=k):
                L = (k + 1) * tq
                outs = []
                v = v_ref[0:L, :]
                for hh in range(nh):
                    al = slice(hh * LANES, (hh + 1) * LANES)
                    qav = qa_ref[:, al]
                    s, _ = _fox_logits(qav, ka_ref[0:L, al], NEG)
                    m = jnp.max(s, axis=-1, keepdims=True)
                    e = jnp.exp(s - m)
                    l = jnp.sum(e, axis=-1, keepdims=True)
                    p = e / l
                    outs.append(jnp.dot(p.astype(BF16), v[:, hh * HD:(hh + 1) * HD], preferred_element_type=F32))
                    hi, mid, lo = _split3(-(m + jnp.log(l)))
                    qb_ref[:, al] = jnp.where(lane == HD + 6, hi, jnp.where(lane == HD + 7, mid,
                                                                         jnp.where(lane == HD + 8, lo, qav)))
                o_ref[...] = jnp.concatenate(outs, axis=-1).astype(BF16)

    qspec = pl.BlockSpec((tq, aw), lambda b, h, i: (b * nq + i, h))
    out = _pc(body, name=name, cargo=cargo,
              out_shape=(jax.ShapeDtypeStruct((T, HP * LANES), BF16), jax.ShapeDtypeStruct(qa.shape, BF16)),
              grid=(nb, HP, nq),
              in_specs=[qspec, pl.BlockSpec((S, aw), lambda b, h, i: (b, h)),
                        pl.BlockSpec((S, LANES), lambda b, h, i: (b, h))],
              out_specs=(pl.BlockSpec((tq, LANES), lambda b, h, i: (b * nq + i, h)), qspec),
              compiler_params=_cp("parallel", "parallel", "arbitrary"))(qa, ka, vb, *(cargo.arrays if cargo else ()))
    return _split_cargo(out, cargo)


def _fox_bwd(qa, ka, vb, do, nb, S, HD, *, name, tq=256, cargo=None):
    T = qa.shape[0]
    nh = LANES // HD
    aw = nh * LANES
    HP = qa.shape[1] // aw
    tq = _tile(S, tq, 8)
    nq = S // tq

    def body(qa_ref, ka_ref, v_ref, do_ref, dqa_ref, dka_ref, dv_ref, dcs_ref):
        i = pl.program_id(2)

        @pl.when(i == 0)
        def _():
            dka_ref[...] = jnp.zeros_like(dka_ref)
            dv_ref[...] = jnp.zeros_like(dv_ref)
            dcs_ref[...] = jnp.zeros_like(dcs_ref)

        for k in range(nq):
            @pl.when(i == k)
            def _(k=k):
                L = (k + 1) * tq
                dkas, dvs, css = [], [], []
                v, dob = v_ref[0:L, :], do_ref[...]
                for hh in range(nh):
                    al = slice(hh * LANES, (hh + 1) * LANES)
                    hl = slice(hh * HD, (hh + 1) * HD)
                    qav, kav, vv, dov = qa_ref[:, al], ka_ref[0:L, al], v[:, hl], dob[:, hl]
                    p = jnp.exp(_fox_logits(qav, kav, NEG)[0])
                    dp = lax.dot_general(dov, vv, _DIMS["nt"], preferred_element_type=F32)
                    dsf = p * (dp - jnp.sum(p * dp, axis=-1, keepdims=True))
                    css.append(jnp.sum(dsf, axis=0, keepdims=True))
                    ds = dsf.astype(BF16)
                    dqa_ref[:, al] = jnp.dot(ds, kav, preferred_element_type=F32)
                    dkas.append(lax.dot_general(ds, qav, _DIMS["tn"], preferred_element_type=F32))
                    dvs.append(lax.dot_general(p.astype(BF16), dov, _DIMS["tn"], preferred_element_type=F32))
                dka_ref[0:L, :] += jnp.concatenate(dkas, axis=-1)
                dv_ref[0:L, :] += jnp.concatenate(dvs, axis=-1)
                dcs_ref[:, 0:L] += jnp.concatenate(css + [jnp.zeros((8 - nh, L), F32)], axis=0)

    out = _pc(body, name=name, cargo=cargo,
              out_shape=(jax.ShapeDtypeStruct((T, HP * aw), F32), jax.ShapeDtypeStruct((T, HP * aw), F32),
                         jax.ShapeDtypeStruct((T, HP * LANES), F32), jax.ShapeDtypeStruct((nb, HP, 8, S), F32)),
              grid=(nb, HP, nq),
              in_specs=[pl.BlockSpec((tq, aw), lambda b, h, i: (b * nq + i, h)),
                        pl.BlockSpec((S, aw), lambda b, h, i: (b, h)),
                        pl.BlockSpec((S, LANES), lambda b, h, i: (b, h)),
                        pl.BlockSpec((tq, LANES), lambda b, h, i: (b * nq + i, h))],
              out_specs=(pl.BlockSpec((tq, aw), lambda b, h, i: (b * nq + i, h)),
                         pl.BlockSpec((S, aw), lambda b, h, i: (b, h)),
                         pl.BlockSpec((S, LANES), lambda b, h, i: (b, h)),
                         pl.BlockSpec((None, None, 8, S), lambda b, h, i: (b, h, 0, 0))),
              compiler_params=_cp("parallel", "parallel", "arbitrary"))(qa, ka, vb, do,
                                                                         *(cargo.arrays if cargo else ()))
    return _split_cargo(out, cargo)


def _mem_probs(q, k, scale):
    s = lax.dot_general(q, k, _DIMS["nt"], preferred_element_type=F32) * scale
    e = jnp.exp(s - jnp.max(s, axis=-1, keepdims=True))
    return e / jnp.sum(e, axis=-1, keepdims=True)


def _mem_fwd(q, k, v, nb, S, ML, *, name, tq=512):
    T, MW = q.shape
    nh = MW // LANES
    tq = _tile(S, tq, 8)
    nq = S // tq
    scale = LANES ** -0.5

    def body(q_ref, k_ref, v_ref, o_ref):
        for h in range(nh):
            sl = slice(h * LANES, (h + 1) * LANES)
            p = _mem_probs(q_ref[:, sl], k_ref[:, sl], scale)
            o_ref[:, sl] = jnp.dot(p.astype(BF16), v_ref[:, sl], preferred_element_type=F32).astype(BF16)

    return _pc(body, name=name, out_shape=jax.ShapeDtypeStruct((T, MW), BF16), grid=(nb, nq),
               in_specs=[pl.BlockSpec((tq, MW), lambda b, i: (b * nq + i, 0)),
                         pl.BlockSpec((ML, MW), lambda b, i: (b, 0)), pl.BlockSpec((ML, MW), lambda b, i: (b, 0))],
               out_specs=pl.BlockSpec((tq, MW), lambda b, i: (b * nq + i, 0)),
               compiler_params=_cp("parallel", "arbitrary"))(q, k, v)


def _mem_bwd(q, k, v, do, nb, S, ML, *, name, tq=512):
    T, MW = q.shape
    nh = MW // LANES
    tq = _tile(S, tq, 8)
    nq = S // tq
    scale = LANES ** -0.5

    def body(q_ref, k_ref, v_ref, do_ref, dq_ref, dk_ref, dv_ref):
        i = pl.program_id(1)

        @pl.when(i == 0)
        def _():
            dk_ref[...] = jnp.zeros_like(dk_ref)
            dv_ref[...] = jnp.zeros_like(dv_ref)

        for h in range(nh):
            sl = slice(h * LANES, (h + 1) * LANES)
            qv, kv, vv, dov = q_ref[:, sl], k_ref[:, sl], v_ref[:, sl], do_ref[:, sl]
            p = _mem_probs(qv, kv, scale)
            dp = lax.dot_general(dov, vv, _DIMS["nt"], preferred_element_type=F32)
            ds = (p * (dp - jnp.sum(p * dp, axis=-1, keepdims=True)) * scale).astype(BF16)
            dq_ref[:, sl] = jnp.dot(ds, kv, preferred_element_type=F32)
            dk_ref[:, sl] += lax.dot_general(ds, qv, _DIMS["tn"], preferred_element_type=F32)
            dv_ref[:, sl] += lax.dot_general(p.astype(BF16), dov, _DIMS["tn"], preferred_element_type=F32)

    kvspec = pl.BlockSpec((ML, MW), lambda b, i: (b, 0))
    qspec = pl.BlockSpec((tq, MW), lambda b, i: (b * nq + i, 0))
    return _pc(body, name=name,
               out_shape=(jax.ShapeDtypeStruct((T, MW), F32), jax.ShapeDtypeStruct((nb * ML, MW), F32),
                          jax.ShapeDtypeStruct((nb * ML, MW), F32)),
               grid=(nb, nq), in_specs=[qspec, kvspec, kvspec, qspec], out_specs=(qspec, kvspec, kvspec),
               compiler_params=_cp("parallel", "arbitrary"))(q, k, v, do)


def _place():
    x, y, c = lax.axis_index("x"), lax.axis_index("y"), lax.axis_index("c")
    chips = [(1 - x, y), (x, 1 - y), (1 - x, 1 - y)]
    return x, y, c, chips


def _any_specs(n):
    return [pl.BlockSpec(memory_space=pl.ANY)] * n


class _Cargo:
    def __init__(self, arrays, out_shapes, ncopies, copies, alias=None):
        self.arrays, self.out_shapes, self.ncopies, self.copies = list(arrays), list(out_shapes), ncopies, copies
        self.alias = alias or {}


def _load_cargo(body, cargo, kw):
    as_list = lambda v: list(v) if isinstance(v, (tuple, list)) else [v]
    grid = kw.get("grid", ())
    in_specs, out_specs, out_shape = as_list(kw["in_specs"]), as_list(kw["out_specs"]), as_list(kw["out_shape"])
    scratch = list(kw.get("scratch_shapes", ()))
    nin, nout, nci, nco, nscr = len(in_specs), len(out_specs), len(cargo.arrays), len(cargo.out_shapes), len(scratch)

    def loaded(*refs):
        ins, cin = refs[:nin], refs[nin:nin + nci]
        outs, cout = refs[nin + nci:nin + nci + nout], refs[nin + nci + nout:nin + nci + nout + nco]
        scr, (ssem, rsem) = refs[nin + nci + nout + nco:-2], refs[-2:]
        first = last = True
        for a, g in enumerate(grid):
            first = first & (pl.program_id(a) == 0)
            last = last & (pl.program_id(a) == g - 1)

        def start():
            for cp in cargo.copies(cin, cout, ssem, rsem, False):
                cp.start()

        def wait():
            for cp in cargo.copies(cin, cout, ssem, rsem, True):
                cp.wait_send()
                cp.wait_recv()

        if grid:
            pl.when(first)(start)
        else:
            start()
        body(*ins, *outs, *scr)
        if grid:
            pl.when(last)(wait)
        else:
            wait()

    kw = dict(kw, in_specs=in_specs + _any_specs(nci), out_specs=tuple(out_specs + _any_specs(nco)),
              out_shape=tuple(out_shape + cargo.out_shapes),
              scratch_shapes=scratch + [pltpu.SemaphoreType.DMA((cargo.ncopies,))] * 2)
    if cargo.alias:
        kw["input_output_aliases"] = {nin + i: nout + o for i, o in cargo.alias.items()}
    if grid:
        kw["compiler_params"] = _cp(*["arbitrary"] * len(grid))
    return loaded, kw


class _SemView:
    def __init__(self, ref, off):
        self.ref, self.off, self.at = ref, off, self

    def __getitem__(self, k):
        return self.ref.at[self.off + k]


def _join_cargo(a, b):
    na, nao = len(a.arrays), len(a.out_shapes)

    def copies(ins, outs, ssem, rsem, waiting):
        return (a.copies(ins[:na], outs[:nao], ssem, rsem, waiting) +
                b.copies(ins[na:], outs[nao:], _SemView(ssem, a.ncopies), _SemView(rsem, a.ncopies), waiting))

    alias = dict(a.alias)
    alias.update({na + i: nao + o for i, o in b.alias.items()})
    return _Cargo(a.arrays + b.arrays, a.out_shapes + b.out_shapes, a.ncopies + b.ncopies, copies, alias)


def _run_cargo(cargo, *, name):
    res = _pc(lambda: None, name=name, cargo=cargo, in_specs=[], out_specs=[], out_shape=[])(*cargo.arrays)
    return list(res)


def _full_shape(dims, kind):
    return (dims[0], 4 * dims[1]) if kind == "col" else (4 * dims[0], dims[1])


def _ag_copy(shard, full, kind, j_src, chip, c, ssem, rsem, piece):
    R, C = shard.shape
    p, npieces, cnt = (*piece, 1)[:3]
    rows = R // 2 // npieces * cnt
    row0 = pl.multiple_of(c * (R // 2) + p * (R // 2 // npieces), 16)
    if kind == "col":
        dst = full.at[pl.ds(row0, rows), pl.ds(pl.multiple_of(j_src * C, LANES), C)]
    else:
        dst = full.at[pl.ds(pl.multiple_of(j_src * R + row0, 16), rows), :]
    return pltpu.make_async_remote_copy(shard.at[pl.ds(row0, rows), :], dst, ssem, rsem,
                                        device_id=(*chip, c), device_id_type=MESH)


def _ag_cargo(shards, kinds, piece=(0, 1), lands=None):
    n = len(shards)

    def copies(ins, outs, ssem, rsem, waiting):
        x, y, c, chips = _place()
        return [_ag_copy(ins[w], outs[w], kinds[w], 2 * chip[0] + chip[1] if waiting else 2 * x + y, chip, c,
                         ssem.at[3 * w + q], rsem.at[3 * w + q], piece)
                for w in range(n) for q, chip in enumerate(chips)]

    shapes = [jax.ShapeDtypeStruct(_full_shape(s.shape, k), BF16) for s, k in zip(shards, kinds)]
    if lands is None:
        return _Cargo(shards, shapes, 3 * n, copies)
    return _Cargo(list(shards) + list(lands), shapes, 3 * n, copies, alias={n + w: w for w in range(n)})


def _scatter_cargo(parts, kinds, piece=(0, 1), lands=None):
    n = len(parts)
    pdims = [((p.shape[0], p.shape[1] // 4) if k == "col" else (p.shape[0] // 4, p.shape[1]))
             for p, k in zip(parts, kinds)]

    def copy(src, dst, kind, R, C, q, j, chip, c, ssem, rsem):
        p, npieces, cnt = (*piece, 1)[:3]
        rows = R // npieces * cnt
        r0 = p * (R // npieces)
        if kind == "col":
            win = src.at[pl.ds(r0, rows), pl.ds(pl.multiple_of(j * C, LANES), C)]
        else:
            win = src.at[pl.ds(pl.multiple_of(j * R + r0, 16), rows), :]
        return pltpu.make_async_remote_copy(win, dst.at[q, pl.ds(r0, rows), :], ssem, rsem,
                                            device_id=(*chip, c), device_id_type=MESH)

    def copies(ins, outs, ssem, rsem, waiting):
        x, y, c, chips = _place()
        return [copy(ins[w], outs[w], kinds[w], *pdims[w], q, 2 * chip[0] + chip[1], chip, c,
                     ssem.at[3 * w + q], rsem.at[3 * w + q])
                for w in range(n) for q, chip in enumerate(chips)]

    shapes = [jax.ShapeDtypeStruct((3,) + d, BF16) for d in pdims]
    if lands is None:
        return _Cargo(parts, shapes, 3 * n, copies)
    return _Cargo(list(parts) + list(lands), shapes, 3 * n, copies, alias={n + w: w for w in range(n)})


def _ag_fill(full, shard, kind, *, name):
    R, C = shard.shape
    RH = R // 2
    tm = _tile(RH, max(16, (3 * 2 ** 19 // C) // 16 * 16), 16)
    nbh = RH // tm
    n, nown = 3 * nbh, 2 * nbh
    xyc = jnp.stack([lax.axis_index("x"), lax.axis_index("y"), lax.axis_index("c")]).astype(jnp.int32)

    def chip_of(q, x, y):
        keep_x, keep_y = q % 2, (2 - q) // 2
        return 2 * ((1 - x) + keep_x * (2 * x - 1)) + (1 - y) + keep_y * (2 * y - 1)

    def window(ref, j, row0):
        if kind == "col":
            return ref.at[pl.ds(pl.multiple_of(row0, 16), tm), pl.ds(pl.multiple_of(j * C, LANES), C)]
        return ref.at[pl.ds(pl.multiple_of(j * R + row0, 16), tm), :]

    def body(p_ref, send_ref, own_ref, out_ref, land, ssem, rsem, wsem):
        i = pl.program_id(0)
        x, y = lax.axis_index("x"), lax.axis_index("y")
        c, peer = _pair_peer()
        own_cp = pltpu.make_async_copy(own_ref, window(out_ref, 2 * x + y, jnp.minimum(i, nown - 1) * tm), wsem.at[1])

        @pl.when(i < nown)
        def _():
            own_cp.start()

        def consume(land, ps):
            b = i - 1
            cp = pltpu.make_async_copy(land.at[ps], window(out_ref, chip_of(b // nbh, x, y),
                                                           (1 - c) * RH + (b % nbh) * tm), wsem.at[0])
            cp.start()
            cp.wait()

        _exchange_step(i, n, send_ref, land, ssem, rsem, None, peer, consume)

        @pl.when(i < nown)
        def _():
            own_cp.wait()

    now = lambda i: jnp.minimum(i, n - 1)

    def send_map(i, p):
        q, b = now(i) // nbh, now(i) % nbh
        j = chip_of(q, p[0], p[1])
        return (p[2] * nbh + b, j) if kind == "col" else ((j * 2 + p[2]) * nbh + b, 0)

    grid_spec = pltpu.PrefetchScalarGridSpec(
        num_scalar_prefetch=1, grid=(n + 1,),
        in_specs=[pl.BlockSpec((tm, C), send_map),
                  pl.BlockSpec((tm, C), lambda i, p: (jnp.minimum(i, nown - 1), 0))],
        out_specs=pl.BlockSpec(memory_space=pl.ANY),
        scratch_shapes=[pltpu.VMEM((n, tm, C), BF16), pltpu.SemaphoreType.DMA((n,)), pltpu.SemaphoreType.DMA((n,)),
                        pltpu.SemaphoreType.DMA((2,))])
    return _pc(body, name=name, out_shape=jax.ShapeDtypeStruct(full.shape, BF16), grid_spec=grid_spec,
               input_output_aliases={1: 0}, compiler_params=_cp("arbitrary"))(xyc, full, shard)


def _pair_peer():
    x, y, c = lax.axis_index("x"), lax.axis_index("y"), lax.axis_index("c")
    return c, (x, y, 1 - c)


def _exchange_step(i, n, src_ref, land, ssem, rsem, credit, peer, consume):
    def rdma(slot):
        return pltpu.make_async_remote_copy(src_ref, land.at[slot], ssem.at[slot], rsem.at[slot],
                                            device_id=peer, device_id_type=MESH)

    reuse = credit is not None
    slot = i % 2 if reuse else jnp.minimum(i, n - 1)

    @pl.when(i < n)
    def _():
        if reuse:
            @pl.when(i >= 2)
            def _():
                pl.semaphore_wait(credit, 1)

        rdma(slot).start()

    @pl.when(i >= 1)
    def _():
        ps = (i - 1) % 2 if reuse else i - 1
        rdma(ps).wait_recv()
        consume(land, ps)

        if reuse:
            @pl.when(i - 1 < n - 2)
            def _():
                pl.semaphore_signal(credit, 1, device_id=peer, device_id_type=MESH)

    @pl.when(i < n)
    def _():
        rdma(slot).wait_send()


def _xchg_scratch(tm, C, dtype):
    return [pltpu.VMEM((2, tm, C), dtype), pltpu.SemaphoreType.DMA((2,)), pltpu.SemaphoreType.DMA((2,)),
            pltpu.SemaphoreType.REGULAR]


def _rs_pair_sum(g, kind, *, name):
    K, N = g.shape
    NS, RH = (1, K // 2) if kind == "col" else (4, K // 8)
    tm = _tile(RH, max(16, (3 * 2 ** 19 // N) // 16 * 16), 16)
    if NS * (RH // tm) < 2:
        tm = RH // 2
    nb = RH // tm
    n = NS * nb
    cvec = lax.axis_index("c").reshape(1).astype(jnp.int32)

    def body(c_ref, send_ref, own_ref, out_ref, land, ssem, rsem, credit):
        _, peer = _pair_peer()

        def consume(land, ps):
            out_ref[...] = (own_ref[...].astype(F32) + land[ps].astype(F32)).astype(BF16)

        _exchange_step(pl.program_id(0), n, send_ref, land, ssem, rsem, credit, peer, consume)

    now = lambda i: jnp.minimum(i, n - 1)
    lag = lambda i: jnp.maximum(i - 1, 0)
    grid_spec = pltpu.PrefetchScalarGridSpec(
        num_scalar_prefetch=1, grid=(n + 1,),
        in_specs=[pl.BlockSpec((tm, N), lambda i, c: ((now(i) // nb * 2 + 1 - c[0]) * nb + now(i) % nb, 0)),
                  pl.BlockSpec((tm, N), lambda i, c: ((lag(i) // nb * 2 + c[0]) * nb + lag(i) % nb, 0))],
        out_specs=pl.BlockSpec((tm, N), lambda i, c: (lag(i), 0)),
        scratch_shapes=_xchg_scratch(tm, N, BF16))
    return _pc(body, name=name, out_shape=jax.ShapeDtypeStruct((NS * RH, N), BF16), grid_spec=grid_spec,
               compiler_params=_cp("arbitrary"))(cvec, g, g)


def _rs_sum_join(part, got, kind, *, name):
    _, hr, hc = got.shape
    tm = _tile(hr, max(16, (3 * 2 ** 18 // hc) // 16 * 16), 16)
    if hr // tm < 2:
        tm = hr // 2
    nb = hr // tm
    jvec = (2 * lax.axis_index("x") + lax.axis_index("y")).reshape(1).astype(jnp.int32)

    def body(j_ref, p_ref, a_ref, b_ref, c_ref, mine_ref, theirs_ref, sbuf, land, ssem, rsem, credit):
        i = pl.program_id(0)
        _, peer = _pair_peer()

        @pl.when(i < nb)
        def _():
            v = ((p_ref[...].astype(F32) + a_ref[...].astype(F32)) + b_ref[...].astype(F32)) + c_ref[...].astype(F32)
            vb = v.astype(BF16)
            mine_ref[...] = vb.astype(F32)
            sbuf[...] = vb

        def consume(land, ps):
            theirs_ref[...] = land[ps].astype(F32)

        _exchange_step(i, nb, sbuf, land, ssem, rsem, credit, peer, consume)

    now = lambda i: jnp.minimum(i, nb - 1)
    lag = lambda i: jnp.maximum(i - 1, 0)
    own_spec = (pl.BlockSpec((tm, hc), lambda i, j: (now(i), j[0])) if kind == "col"
                else pl.BlockSpec((tm, hc), lambda i, j: (j[0] * nb + now(i), 0)))
    got_spec = lambda q: pl.BlockSpec((None, tm, hc), functools.partial(lambda i, j, q: (q, now(i), 0), q=q))
    grid_spec = pltpu.PrefetchScalarGridSpec(
        num_scalar_prefetch=1, grid=(nb + 1,),
        in_specs=[own_spec, got_spec(0), got_spec(1), got_spec(2)],
        out_specs=(pl.BlockSpec((tm, hc), lambda i, j: (now(i), 0)), pl.BlockSpec((tm, hc), lambda i, j: (lag(i), 0))),
        scratch_shapes=[pltpu.VMEM((tm, hc), BF16)] + _xchg_scratch(tm, hc, BF16))
    shp = jax.ShapeDtypeStruct((hr, hc), F32)
    return _pc(body, name=name, out_shape=(shp, shp), grid_spec=grid_spec,
               compiler_params=_cp("arbitrary"))(jvec, part, got, got, got)


def _adamw_join(w, mine, theirs, m, v, *, name):
    R2, wc = w.shape
    hr, hc = mine.shape
    assert hr < R2 <= 2 * hr and hc >= wc
    pref, rows8 = max(8, 2 ** 18 // hc), _round_up(R2, 8)
    tm = max(t for t in range(8, hr + 1, 8) if hr % t == 0 and rows8 % t == 0 and (t <= pref or t == 8))
    nbh = hr // tm
    cvec = lax.axis_index("c").reshape(1).astype(jnp.int32)

    def body(c_ref, w_ref, a_ref, b_ref, m_ref, v_ref, g_ref, d_ref, m2_ref, v2_ref):
        gv = jnp.where(pl.program_id(0) // nbh == c_ref[0], a_ref[...], b_ref[...])[:, :wc]
        m2 = ADAM_B1 * m_ref[...] + (1.0 - ADAM_B1) * gv
        v2 = ADAM_B2 * v_ref[...] + (1.0 - ADAM_B2) * (gv * gv)
        m_hat = m2 / (1.0 - ADAM_B1 ** ADAM_STEP)
        v_hat = v2 / (1.0 - ADAM_B2 ** ADAM_STEP)
        g_ref[...] = gv
        d_ref[...] = -ADAM_LR * (m_hat / (jnp.sqrt(v_hat) + ADAM_EPS) + ADAM_WD * w_ref[...])
        m2_ref[...] = m2
        v2_ref[...] = v2

    wspec = pl.BlockSpec((tm, wc), lambda i, c: (i, 0))
    mine_spec = pl.BlockSpec((tm, hc), lambda i, c: (jnp.where(i // nbh == c[0], i % nbh, 0), 0))
    theirs_spec = pl.BlockSpec((tm, hc), lambda i, c: (jnp.where(i // nbh == c[0], 0, i % nbh), 0))
    grid_spec = pltpu.PrefetchScalarGridSpec(
        num_scalar_prefetch=1, grid=(pl.cdiv(R2, tm),),
        in_specs=[wspec, mine_spec, theirs_spec, wspec, wspec], out_specs=(wspec,) * 4)
    return _pc(body, name=name, out_shape=(jax.ShapeDtypeStruct((R2, wc), F32),) * 4, grid_spec=grid_spec,
               compiler_params=_cp("parallel"))(cvec, w, mine, theirs, m, v)


def _all_reduce_small(pack):
    R = pack.shape[0]

    def body(p_ref, o_ref, buf, send, recv):
        x, y, c, _ = _place()
        me = 4 * x + 2 * y + c
        buf[0] = p_ref[...]
        cps = []
        for r in range(1, 8):
            peer = (x if r & 4 == 0 else 1 - x, y if r & 2 == 0 else 1 - y, c if r & 1 == 0 else 1 - c)
            cp = pltpu.make_async_remote_copy(p_ref, buf.at[r], send.at[r - 1], recv.at[r - 1],
                                              device_id=peer, device_id_type=MESH)
            cp.start()
            cps.append(cp)
        for cp in cps:
            cp.wait()
        acc = buf[jnp.bitwise_xor(me, 0)]
        for k in range(1, 8):
            acc = acc + buf[jnp.bitwise_xor(me, k)]
        o_ref[...] = acc

    return _pc(body, name="allreduce_small", out_shape=jax.ShapeDtypeStruct((R, LANES), F32),
               in_specs=[pl.BlockSpec(memory_space=pltpu.VMEM)], out_specs=pl.BlockSpec(memory_space=pltpu.VMEM),
               scratch_shapes=[pltpu.VMEM((8, R, LANES), F32), pltpu.SemaphoreType.DMA((7,)),
                               pltpu.SemaphoreType.DMA((7,))])(pack)


def _pack_small(arrs):
    parts, sizes = [], []
    for a in arrs:
        f = a.reshape(-1).astype(F32)
        n = _round_up(f.shape[0], 8 * LANES)
        parts.append(jnp.pad(f, (0, n - f.shape[0])).reshape(n // LANES, LANES))
        sizes.append((a.shape, f.shape[0], n // LANES))
    return jnp.concatenate(parts, axis=0), sizes


def _unpack_small(pack, sizes):
    out, r0 = [], 0
    for shape, cnt, rows in sizes:
        out.append(pack[r0:r0 + rows].reshape(-1)[:cnt].reshape(shape))
        r0 += rows
    return out


def _ffn_fwd(xin, norm, weight, tag, gather_on_gateup, gather_on_down, gather_on_norm=lambda: (None, None)):
    cargo, landed = gather_on_norm()
    hb = _rms_fwd(xin, norm, name=tag + "_norm", cargo=cargo)
    if cargo is not None:
        hb, lands = hb
        landed(lands)
    cargo, landed = gather_on_gateup()
    res = _gateup(hb, weight("w_gate_up"), name=tag + "_gateup", tm=1024, cargo=cargo)
    if cargo is not None:
        res, lands = res
        landed(lands)
    gate, up, act = res
    cargo, landed = gather_on_down()
    xout = _mm(act, weight("w_down"), res=xin, scale=0.5, name=tag + "_down", tm=1024, tn=512, tk=8192, cargo=cargo)
    if cargo is not None:
        xout, lands = xout
        landed(lands)
    return xout, (hb, gate, up, act)


def _ffn_bwd(dy, dyb, xin, norm, wgu, wd, saved, tag, scatter, gu_piece=(0, 1), rest_on_dnorm=False):
    hb, gate, up, act = saved
    d_wd = _mm(act, dyb, mode="tn", scale=0.5, out_dtype=BF16, name=tag + "_dwd", tm=512, tn=1024, tk=4096)
    cargo = scatter(tag + "_w_down", d_wd)
    rest, rest_done = scatter.leftover()
    dgu = _dact_swiglu(dyb, wd, gate, up, name=tag + "_dact", cargo=rest)
    if rest is not None:
        dgu, lands = dgu
        rest_done(lands)
    d_wgu, lands = _mm(hb, dgu, mode="tn", halves="b", out_dtype=BF16, name=tag + "_dwgu", tm=1024, tn=512, tk=4096,
                       cargo=cargo)
    scatter.done(lands)
    cargo = scatter(tag + "_w_gate_up", d_wgu, piece=gu_piece)
    dh, lands = _mm(dgu, wgu, mode="nt", halves="a", out_dtype=BF16, name=tag + "_dh", tm=1024, tn=512, tk=5632,
                    cargo=cargo)
    scatter.done(lands)
    rest, rest_done = scatter.leftover() if rest_on_dnorm else (None, None)
    res = _rms_bwd(xin, dh, norm, dy, name=tag + "_dnorm", cargo=rest)
    if rest is not None:
        res, lands = res
        rest_done(lands)
    return res


def kernel(x, mem, ffn1_norm, ffn1_w_gate_up, ffn1_w_down, mix_norm, mem_norm, w_in, b_forget, pool_w, pool_scale, w_pool_up, fox_q_norm, fox_k_norm, w_fox_o, w_mem_kv, mem_q_norm, mem_k_norm, w_mem_o, w_out, ffn2_norm, ffn2_w_gate_up, ffn2_w_down, loss_target, m_ffn1_norm, m_ffn1_w_gate_up, m_ffn1_w_down, m_mix_norm, m_mem_norm, m_w_in, m_b_forget, m_pool_w, m_pool_scale, m_w_pool_up, m_fox_q_norm, m_fox_k_norm, m_w_fox_o, m_w_mem_kv, m_mem_q_norm, m_mem_k_norm, m_w_mem_o, m_w_out, m_ffn2_norm, m_ffn2_w_gate_up, m_ffn2_w_down, v_ffn1_norm, v_ffn1_w_gate_up, v_ffn1_w_down, v_mix_norm, v_mem_norm, v_w_in, v_b_forget, v_pool_w, v_pool_scale, v_w_pool_up, v_fox_q_norm, v_fox_k_norm, v_w_fox_o, v_w_mem_kv, v_mem_q_norm, v_mem_k_norm, v_w_mem_o, v_w_out, v_ffn2_norm, v_ffn2_w_gate_up, v_ffn2_w_down):
    P = dict(locals())
    big = ["ffn1_w_gate_up", "ffn1_w_down", "w_in", "w_pool_up", "w_fox_o", "w_mem_kv", "w_mem_o", "w_out",
           "ffn2_w_gate_up", "ffn2_w_down"]
    kinds = ["col", "row", "col", "col", "col", "row", "col", "row", "col", "row"]
    small = ["ffn1_norm", "mix_norm", "mem_norm", "b_forget", "pool_w", "pool_scale", "fox_q_norm", "fox_k_norm",
             "mem_q_norm", "mem_k_norm", "ffn2_norm"]
    order = ["ffn1_norm", "ffn1_w_gate_up", "ffn1_w_down", "mix_norm", "mem_norm", "w_in", "b_forget", "pool_w",
             "pool_scale", "w_pool_up", "fox_q_norm", "fox_k_norm", "w_fox_o", "w_mem_kv", "mem_q_norm", "mem_k_norm",
             "w_mem_o", "w_out", "ffn2_norm", "ffn2_w_gate_up", "ffn2_w_down"]

    nb, S, D = x.shape
    T = nb * S
    ML = mem.shape[1]
    NF, HD = b_forget.shape[-1], fox_q_norm.shape[-1]
    FW = NF * HD
    MW, MHD = w_mem_o.shape[1], mem_q_norm.shape[-1]
    PG, PD = pool_w.shape[1], pool_w.shape[2]
    PW = PG * PD
    n_in = w_in.shape[-1]
    n_in_pad = _round_up(n_in, LANES)
    in_w = 4 * n_in
    assert HD * 2 == LANES and MHD == LANES and PD == LANES and in_w == PW + 3 * FW + NF + MW + 3 * D

    shards = []
    for nme in big:
        wl = P[nme][0].astype(BF16)
        if nme == "w_in":
            wl = jnp.pad(wl, ((0, 0), (0, n_in_pad - n_in)))
        shards.append(wl)
    shard, kind_of, full = dict(zip(big, shards)), dict(zip(big, kinds)), {}

    def landed(names):
        def fill(lands):
            for nme, l in zip(names, lands):
                full[nme] = _ag_fill(l, shard[nme], kind_of[nme], name="ag_fill_" + nme)
        return fill

    def gather(names, piece=(0, 1), lands=None):
        return _ag_cargo([shard[nme] for nme in names], [kind_of[nme] for nme in names], piece, lands)

    landed(["ffn1_w_gate_up"])(_run_cargo(gather(["ffn1_w_gate_up"]), name="ag_first"))
    begun = {}

    def on_norm1():
        return gather(["ffn1_w_down"], (0, 4)), lambda lands: begun.update(ffn1_w_down=lands[0])

    def on_gateup1():
        def fill(lands):
            landed(["ffn1_w_down"])(lands[:1])
            begun["w_in"] = lands[1]
        return _join_cargo(gather(["ffn1_w_down"], (1, 4, 3), [begun["ffn1_w_down"]]),
                           gather(["w_in"], (0, 2))), fill

    x0 = x.reshape(T, D)
    x1, sv1 = _ffn_fwd(x0, ffn1_norm, lambda nme: full["ffn1_" + nme], "ffn1", on_gateup1,
                       lambda: (gather(["w_in"], (1, 2), [begun["w_in"]]), landed(["w_in"])), on_norm1)
    def gathered_cols(a, b):
        out = []
        for j in range(4):
            lo, hi = max(a, j * n_in), min(b, (j + 1) * n_in)
            if lo < hi:
                out.append(full["w_in"][:, j * (n_in_pad - n_in) + lo:j * (n_in_pad - n_in) + hi])
        return out

    o_q, o_k, o_v, o_f = PW, PW + FW, PW + 2 * FW, PW + 3 * FW
    o_qm, o_gate = o_f + NF, o_f + NF + MW
    p_q, p_k, p_v, p_pool, p_qm, p_f = 3 * D, 3 * D + FW, 3 * D + 2 * FW, 3 * D + 3 * FW, 3 * D + 3 * FW + PW, 3 * D + 3 * FW + PW + MW
    inp = _round_up(p_f + LANES, 512)
    win_p = jnp.concatenate(gathered_cols(o_gate, in_w) + gathered_cols(o_q, o_f) + gathered_cols(0, PW)
                            + gathered_cols(o_qm, o_gate) + gathered_cols(o_f, o_qm)
                            + [jnp.zeros((D, inp - p_f - NF), BF16)], axis=1)

    hb2 = _rms_fwd(x1, mix_norm, name="mix_norm")
    later = ["ffn2_w_down", "w_fox_o", "w_mem_kv", "w_pool_up"]
    z, lands = _mm(hb2, win_p, name="in_proj", tm=1024, tn=1536, cargo=gather(later))
    landed(later)(lands)

    vpool = _pool_fwd(z, p_pool // LANES, pool_w[0], pool_scale, nb, S, name="pool_fwd")
    y_pool = _mm(vpool, full["w_pool_up"], out_dtype=BF16, name="pool_up", tm=1024, tn=512)

    b128 = jnp.pad(b_forget, ((0, 0), (0, LANES - NF)))
    c3 = _fox_c(z, p_f // LANES, b128, nb, S, NF, name="fox_c")
    later = ["w_out", "w_mem_o"]
    (qa, ka, vb), lands = _fox_qk(z, p_q // LANES, p_k // LANES, p_v // LANES, c3, fox_q_norm, fox_k_norm, HD, NF,
                                  name="fox_qk", cargo=gather(later))
    landed(later)(lands)
    (o_fox, qb), lands = _fox_fwd(qa, ka, vb, nb, S, HD, name="fox_fwd",
                                  cargo=gather(["ffn2_w_gate_up"], (0, 4, 3)))
    begun["ffn2_w_gate_up"] = lands[0]
    y_fox = _mm(o_fox, full["w_fox_o"], out_dtype=BF16, name="fox_out", tm=1024, tn=512)

    mem2 = mem.reshape(nb * ML, D)
    memn = _rms_fwd(mem2, mem_norm, name="mem_norm")
    kv = _mm(memn, full["w_mem_kv"], name="mem_kv")
    kmn = _headnorm_fwd(kv, 0, MW, mem_k_norm, MHD, name="mem_knorm")
    vmb = kv[:, MW:].astype(BF16)
    qmn = _headnorm_fwd(z, p_qm // LANES, MW, mem_q_norm, MHD, name="mem_qnorm")
    o_mem = _mem_fwd(qmn, kmn, vmb, nb, S, ML, name="mem_fwd")
    y_mem = _mm(o_mem, full["w_mem_o"], out_dtype=BF16, name="mem_out", tm=1024, tn=512)

    merged, lands = _merge_fwd(z, y_pool, y_fox, y_mem, name="merge_fwd",
                               cargo=gather(["ffn2_w_gate_up"], (3, 4), [begun["ffn2_w_gate_up"]]))
    landed(["ffn2_w_gate_up"])(lands)
    x2 = _mm(merged, full["w_out"], res=x1, name="out_proj", tm=1024, tn=512)
    x3, sv2 = _ffn_fwd(x2, ffn2_norm, lambda nme: full["ffn2_" + nme], "ffn2", lambda: (None, None),
                       lambda: (None, None))
    dy3, dy3b, loss_part = _loss(x3, loss_target.reshape(T, D), name="loss")

    g, parts, got = {}, {}, {}
    rs_kind = dict(kind_of, w_in="row")

    leftovers = []

    def scatter(*named_grads, piece=(0, 1)):
        names = named_grads[0::2]
        for nme, grad in zip(names, named_grads[1::2]):
            parts[nme] = _rs_pair_sum(grad, rs_kind[nme], name="rs_pair_" + nme)
        scatter.names = names
        sent = (*piece, 1)[2]
        if sent < piece[1]:
            leftovers.append((names, (sent, piece[1], piece[1] - sent)))
        return _scatter_cargo([parts[nme] for nme in names], [rs_kind[nme] for nme in names], piece)

    def scattered(lands):
        got.update(zip(scatter.names, lands))

    def leftover():
        if not leftovers:
            return None, None
        names, piece = leftovers.pop(0)
        cargo = _scatter_cargo([parts[nme] for nme in names], [rs_kind[nme] for nme in names], piece,
                               [got[nme] for nme in names])
        return cargo, lambda lands: got.update(zip(names, lands))

    scatter.done, scatter.leftover = scattered, leftover
    dx2, dx2b, g["ffn2_norm"] = _ffn_bwd(dy3, dy3b, x2, ffn2_norm, full["ffn2_w_gate_up"], full["ffn2_w_down"], sv2,
                                         "ffn2", scatter, gu_piece=(0, 2))

    dmerged = _mm(dx2b, full["w_out"], mode="nt", out_dtype=BF16, name="d_merged", tm=1024, tn=512)
    g["w_out"] = _mm(merged, dx2b, mode="tn", out_dtype=BF16, name="d_w_out", tk=4096)
    dyp, dyf, dym, dgl = _merge_bwd(dmerged, z, y_pool, y_fox, y_mem, name="merge_bwd")

    g["w_pool_up"] = _mm(vpool, dyp, mode="tn", out_dtype=BF16, name="d_w_pool_up", tk=4096)
    dvp = _mm(dyp, full["w_pool_up"], mode="nt", name="d_vpool", tm=1024)
    du_pool, g["pool_w"], g["pool_scale"] = _pool_bwd(z, p_pool // LANES, dvp, pool_w[0], pool_scale, nb, S,
                                                      name="pool_bwd")

    g["w_fox_o"] = _mm(o_fox, dyf, mode="tn", out_dtype=BF16, name="d_w_fox_o", tk=4096)
    do_fox = _mm(dyf, full["w_fox_o"], mode="nt", out_dtype=BF16, name="d_o_fox", tm=1024)
    rest, rest_done = leftover()
    (dqa, dka, dvf, dcs), lands = _fox_bwd(qb, ka, vb, do_fox, nb, S, HD, name="fox_bwd", cargo=rest)
    rest_done(lands)
    dq, dgq = _headnorm_bwd(z, p_q // LANES, FW, dqa, fox_q_norm, HD, HD ** -0.5, True, BF16, name="fox_dq")
    dk, dgk = _headnorm_bwd(z, p_k // LANES, FW, dka, fox_k_norm, HD, 1.0, True, BF16, name="fox_dk")
    g["fox_q_norm"], g["fox_k_norm"] = _fold_gain(dgq, HD), _fold_gain(dgk, HD)
    dc = -dcs[:, :, :LANES // HD, :].reshape(nb, NF, S).transpose(0, 2, 1).reshape(T, NF)
    df, db = _fox_c_bwd(jnp.pad(dc, ((0, 0), (0, LANES - NF))), z, p_f // LANES, b128, nb, S, name="fox_c_bwd")
    g["b_forget"] = db[:, :NF]

    g["w_mem_o"] = _mm(o_mem, dym, mode="tn", out_dtype=BF16, name="d_w_mem_o", tk=4096)
    do_mem = _mm(dym, full["w_mem_o"], mode="nt", out_dtype=BF16, name="d_o_mem", tm=1024)
    dqmn, dkmn, dvm = _mem_bwd(qmn, kmn, vmb, do_mem, nb, S, ML, name="mem_bwd")
    dqm, g["mem_q_norm"] = _headnorm_bwd(z, p_qm // LANES, MW, dqmn, mem_q_norm, MHD, 1.0, False, BF16, name="mem_dq")
    dkm, g["mem_k_norm"] = _headnorm_bwd(kv, 0, MW, dkmn, mem_k_norm, MHD, 1.0, False, BF16, name="mem_dk")
    dkv = jnp.concatenate([dkm, dvm.astype(BF16)], axis=1)
    g["w_mem_kv"] = _mm(memn, dkv, mode="tn", out_dtype=BF16, name="d_w_mem_kv")
    dmemn = _mm(dkv, full["w_mem_kv"], mode="nt", name="d_memn")
    _, _, g["mem_norm"] = _rms_bwd(mem2, dmemn, mem_norm, jnp.zeros_like(mem2), name="mem_dnorm")

    dz = jnp.concatenate([dgl, dq, dk, dvf.astype(BF16), du_pool, dqm, df.astype(BF16),
                          jnp.zeros((T, inp - p_f - LANES), BF16)], axis=1)
    cargo = scatter(*sum([[nme, g[nme]] for nme in ("w_out", "w_pool_up", "w_fox_o", "w_mem_o", "w_mem_kv")], []))
    dwin_p, lands = _mm(dz, hb2, mode="tn", out_dtype=BF16, name="d_w_in", tm=512, tn=1024, tk=4096, cargo=cargo)
    scattered(lands)
    dwin_o = jnp.concatenate([dwin_p[p_pool:p_qm], dwin_p[p_q:p_pool], dwin_p[p_f:p_f + NF], dwin_p[p_qm:p_f],
                              dwin_p[:p_q]], axis=0)
    cargo = scatter("w_in", jnp.concatenate(
        [jnp.pad(dwin_o[j * n_in:(j + 1) * n_in], ((0, n_in_pad - n_in), (0, 0))) for j in range(4)], axis=0),
        piece=(0, 2))
    dh2, lands = _mm(dz, win_p, mode="nt", out_dtype=BF16, name="d_h2", tm=1024, tn=512, tk=3584, cargo=cargo)
    scattered(lands)
    dx1, dx1b, g["mix_norm"] = _rms_bwd(x1, dh2, mix_norm, dx2, name="mix_dnorm")

    dx0, _, g["ffn1_norm"] = _ffn_bwd(dx1, dx1b, x0, ffn1_norm, full["ffn1_w_gate_up"], full["ffn1_w_down"], sv1,
                                      "ffn1", scatter, gu_piece=(0, 4, 3), rest_on_dnorm=True)

    halves = {nme: _rs_sum_join(parts[nme], got[nme], rs_kind[nme], name="rs_join_" + nme) for nme in big}
    gfull = {}

    gpack, sizes = _pack_small([g[nme].reshape(P[nme].shape) for nme in small] + [loss_part[:, :1]])
    gsum = _unpack_small(_all_reduce_small(gpack), sizes)
    loss = gsum[-1].reshape(())
    for nme, a in zip(small, gsum[:-1]):
        gfull[nme] = a

    delta, new_m, new_v = {}, {}, {}
    for nme in big:
        shp = P[nme].shape
        two = (lambda a: a.reshape(shp[-2:]).T) if nme == "w_in" else (lambda a: a.reshape(shp[-2:]))
        back = (lambda a: a.T.reshape(shp)) if nme == "w_in" else (lambda a: a.reshape(shp))
        gfull[nme], delta[nme], new_m[nme], new_v[nme] = [
            back(a) for a in _adamw_join(two(P[nme]), *halves[nme], two(P["m_" + nme]), two(P["v_" + nme]),
                                         name="adamw_" + nme)]
    wpack, _ = _pack_small([P[nme] for nme in small])
    mpack, _ = _pack_small([P["m_" + nme] for nme in small])
    vpack, _ = _pack_small([P["v_" + nme] for nme in small])
    gpack2, ssz = _pack_small([gfull[nme] for nme in small])
    for dct, pk in zip((delta, new_m, new_v), _adamw(wpack, gpack2, mpack, vpack, name="adamw_small")):
        for nme, a in zip(small, _unpack_small(pk, ssz)):
            dct[nme] = a

    grad_x = dx0.reshape(nb, S, D)
    return (loss, grad_x, *[gfull[nme] for nme in order], *[delta[nme] for nme in order],
            *[new_m[nme] for nme in order], *[new_v[nme] for nme in order])
```
